```python
import jax, jax.numpy as jnp
from jax import lax
import numpy as np

D_MODEL = 1024
BATCH = 32
SEQ = 2048
DEPTH = 2

GRID_W = 64
CTX_LEN = 256
HEAD_DIM = 64
ATTN_W = D_MODEL // 2
CONV_W = D_MODEL // 4
POOL_W = D_MODEL // 4
MIX_W = ATTN_W + CONV_W + POOL_W
ATTN_HEADS = ATTN_W // HEAD_DIM
KV_HEADS = ATTN_HEADS // 4
KV_W = KV_HEADS * HEAD_DIM
IN_W = ATTN_W + 2 * KV_W + 2 * CONV_W + POOL_W
WINDOW = 128
Q_BLOCK = 128
SPAN = Q_BLOCK + 2 * WINDOW
CONV_KERNEL = 31
POOL_WINDOWS = (2, 4, 8, 16)
POOL_GROUP = POOL_W // len(POOL_WINDOWS)
ROPE_BASE = 10000.0
D_FF = -(-8 * D_MODEL // (3 * 256)) * 256
EPS = 1e-6
NEG = -1e30

kernel_name = "hybrid_parallel_groups_dit_block"


def rms_norm(x, g):
    xf = x.astype(jnp.float32)
    y = xf * lax.rsqrt(jnp.mean(xf * xf, axis=-1, keepdims=True) + EPS)
    return (y * g.astype(jnp.float32)).astype(x.dtype)


def axial_rope_tables(n, dtype):
    rows = n // GRID_W
    row = jnp.repeat(jnp.arange(rows), GRID_W).astype(jnp.float32)
    col = jnp.tile(jnp.arange(GRID_W), rows).astype(jnp.float32)
    half = HEAD_DIM // 2
    inv = ROPE_BASE ** (-jnp.arange(0, half, 2, dtype=jnp.float32) / half)
    ar = row[:, None] * inv
    ac = col[:, None] * inv
    ang = jnp.concatenate([ar, ar, ac, ac], axis=-1)
    return jnp.cos(ang).astype(dtype), jnp.sin(ang).astype(dtype)


def apply_rope(x, cos, sin):
    xr = x.reshape(*x.shape[:-1], 2, 2, HEAD_DIM // 4)
    rot = jnp.stack([-xr[..., 1, :], xr[..., 0, :]], axis=-2).reshape(x.shape)
    return x * cos[:, None, :] + rot * sin[:, None, :]


def split_in(u):
    b, n, _ = u.shape
    q, k, v, cu, pu = jnp.split(
        u, [ATTN_W, ATTN_W + KV_W, ATTN_W + 2 * KV_W, ATTN_W + 2 * KV_W + 2 * CONV_W], axis=-1)
    return (q.reshape(b, n, ATTN_HEADS, HEAD_DIM), k.reshape(b, n, KV_HEADS, HEAD_DIM),
            v.reshape(b, n, KV_HEADS, HEAD_DIM), cu, pu)


def window_attention(q, k, v, k_ctx, v_ctx, sink):
    b, n, h, hd = q.shape
    kvh = k.shape[2]
    grp = h // kvh
    n_ctx = k_ctx.shape[1]
    nb = n // Q_BLOCK
    scale = HEAD_DIM ** -0.5
    pad = ((0, 0), (WINDOW, WINDOW), (0, 0), (0, 0))
    k_pad = jnp.pad(k, pad)
    v_pad = jnp.pad(v, pad)
    sink_b = jnp.broadcast_to(sink.astype(jnp.float32).reshape(1, kvh, grp, 1, 1), (b, kvh, grp, Q_BLOCK, 1))

    def one_block(i):
        start = i * Q_BLOCK
        qb = lax.dynamic_slice_in_dim(q, start, Q_BLOCK, axis=1).reshape(b, Q_BLOCK, kvh, grp, hd)
        kb = lax.dynamic_slice_in_dim(k_pad, start, SPAN, axis=1)
        vb = lax.dynamic_slice_in_dim(v_pad, start, SPAN, axis=1)
        qpos = start + jnp.arange(Q_BLOCK)
        kpos = start - WINDOW + jnp.arange(SPAN)
        valid = ((jnp.abs(qpos[:, None] - kpos[None, :]) <= WINDOW)
                 & (kpos >= 0)[None, :] & (kpos < n)[None, :])
        s_loc = jnp.einsum('bqkgd,bjkd->bkgqj', qb, kb).astype(jnp.float32) * scale
        s_loc = jnp.where(valid, s_loc, NEG)
        s_ctx = jnp.einsum('bqkgd,bckd->bkgqc', qb, k_ctx).astype(jnp.float32) * scale
        p = jax.nn.softmax(jnp.concatenate([sink_b, s_ctx, s_loc], axis=-1), axis=-1).astype(v.dtype)
        o = (jnp.einsum('bkgqc,bckd->bqkgd', p[..., 1:1 + n_ctx], v_ctx)
             + jnp.einsum('bkgqj,bjkd->bqkgd', p[..., 1 + n_ctx:], vb))
        return o.reshape(b, Q_BLOCK, h * hd)

    o = lax.map(one_block, jnp.arange(nb))
    return jnp.moveaxis(o, 0, 1).reshape(b, n, h * hd)


def context_attention(qc, kc, vc, sink):
    b, n_ctx, h, hd = qc.shape
    kvh = kc.shape[2]
    grp = h // kvh
    qg = qc.reshape(b, n_ctx, kvh, grp, hd)
    s = jnp.einsum('bqkgd,bckd->bkgqc', qg, kc).astype(jnp.float32) * (HEAD_DIM ** -0.5)
    sink_b = jnp.broadcast_to(sink.astype(jnp.float32).reshape(1, kvh, grp, 1, 1), (b, kvh, grp, n_ctx, 1))
    p = jax.nn.softmax(jnp.concatenate([sink_b, s], axis=-1), axis=-1).astype(vc.dtype)
    o = jnp.einsum('bkgqc,bckd->bqkgd', p[..., 1:], vc)
    return o.reshape(b, n_ctx, h * hd)


def conv_module(u, dw, dw_b, ln_g, ln_b):
    a, g = jnp.split(u, 2, axis=-1)
    h = a * jax.nn.sigmoid(g)
    h = lax.conv_general_dilated(
        h, dw[:, None, :], window_strides=(1,),
        padding=[(CONV_KERNEL // 2, CONV_KERNEL // 2)],
        dimension_numbers=('NWC', 'WIO', 'NWC'), feature_group_count=CONV_W) + dw_b
    hf = h.astype(jnp.float32)
    mu = jnp.mean(hf, axis=-1, keepdims=True)
    var = jnp.mean(jnp.square(hf - mu), axis=-1, keepdims=True)
    hn = (hf - mu) * lax.rsqrt(var + EPS) * ln_g.astype(jnp.float32) + ln_b.astype(jnp.float32)
    return jax.nn.silu(hn).astype(u.dtype)


def pool_mixer(p, w, scale):
    b, n, ch = p.shape
    t = jnp.arange(n)
    pf = p.astype(jnp.float32).reshape(b, n, len(POOL_WINDOWS), POOL_GROUP)
    cs = jnp.pad(jnp.cumsum(pf, axis=1), ((0, 0), (1, 0), (0, 0), (0, 0)))
    outs = []
    for gi, win in enumerate(POOL_WINDOWS):
        lo = jnp.maximum(t - win // 2, 0)
        hi = jnp.minimum(t + win - 1 - win // 2, n - 1)
        cg = cs[:, :, gi]
        mean = (cg[:, hi + 1] - cg[:, lo]) / (hi - lo + 1).astype(jnp.float32)[None, :, None]
        outs.append(mean - pf[:, :, gi])
    y = jnp.stack(outs, axis=2).astype(p.dtype)
    y = jnp.einsum('bsgc,gcd->bsgd', y, w).reshape(b, n, ch)
    return y * scale


def mixer_output(attn, cu, pu, w_out, dw, dw_b, ln_g, ln_b, pw, ps):
    conv = conv_module(cu, dw, dw_b, ln_g, ln_b)
    pool = pool_mixer(pu, pw, ps)
    return jnp.concatenate([attn, conv, pool], axis=-1) @ w_out


def swiglu(h, w_in, w_out):
    g, u = jnp.split(h @ w_in, 2, axis=-1)
    return (jax.nn.silu(g) * u) @ w_out


def _fwd_setup_inputs(seed: int = 0) -> dict:
    key = jax.random.key(seed)
    ks = jax.random.split(key, 24)
    f32 = jnp.float32
    nrm = lambda k, shape, s: jax.random.normal(k, shape, f32) * s
    return {
        "x": nrm(ks[0], (BATCH, SEQ, D_MODEL), 1.0),
        "c": nrm(ks[1], (BATCH, D_MODEL), 1.0),
        "ctx": nrm(ks[2], (BATCH, CTX_LEN, D_MODEL), 1.0),
        "c_ctx": nrm(ks[3], (D_MODEL,), 1.0),
        "w_mod": nrm(ks[4], (DEPTH, D_MODEL, 6 * D_MODEL), 0.5 * D_MODEL ** -0.5),
        "b_mod": nrm(ks[5], (DEPTH, 6 * D_MODEL), 0.01),
        "norm1_g": 1.0 + nrm(ks[6], (DEPTH, D_MODEL), 0.05),
        "norm2_g": 1.0 + nrm(ks[7], (DEPTH, D_MODEL), 0.05),
        "w_in": nrm(ks[8], (DEPTH, D_MODEL, IN_W), D_MODEL ** -0.5),
        "conv_dw": nrm(ks[9], (DEPTH, CONV_KERNEL, CONV_W), CONV_KERNEL ** -0.5),
        "conv_dw_b": nrm(ks[10], (DEPTH, CONV_W), 0.01),
        "conv_ln_g": 1.0 + nrm(ks[11], (DEPTH, CONV_W), 0.05),
        "conv_ln_b": nrm(ks[12], (DEPTH, CONV_W), 0.01),
        "attn_sink": nrm(ks[13], (DEPTH, ATTN_HEADS), 0.5),
        "pool_w": nrm(ks[14], (DEPTH, len(POOL_WINDOWS), POOL_GROUP, POOL_GROUP), POOL_GROUP ** -0.5),
        "pool_scale": 1.0 + nrm(ks[15], (DEPTH, POOL_W), 0.05),
        "w_out": nrm(ks[16], (DEPTH, MIX_W, D_MODEL), MIX_W ** -0.5),
        "w_ffn_in": nrm(ks[17], (DEPTH, D_MODEL, 2 * D_FF), D_MODEL ** -0.5),
        "w_ffn_out": nrm(ks[18], (DEPTH, D_FF, D_MODEL), D_FF ** -0.5),
        "final_g": 1.0 + nrm(ks[19], (D_MODEL,), 0.05),
    }


def _fwd_reference(x, c, ctx, c_ctx, w_mod, b_mod, norm1_g, norm2_g, w_in, conv_dw, conv_dw_b,
              conv_ln_g, conv_ln_b, attn_sink, pool_w, pool_scale, w_out, w_ffn_in, w_ffn_out, final_g):
    b, n, _ = x.shape
    n_ctx = ctx.shape[1]
    cos, sin = axial_rope_tables(n, x.dtype)
    cx = ctx
    for l in range(DEPTH):
        last = l == DEPTH - 1
        m = (jax.nn.silu(c) @ w_mod[l] + b_mod[l])[:, None, :]
        sh1, sc1, g1, sh2, sc2, g2 = jnp.split(m, 6, axis=-1)
        mc = jax.nn.silu(c_ctx) @ w_mod[l] + b_mod[l]
        csh1, csc1, cg1, csh2, csc2, cg2 = jnp.split(mc, 6)

        hl = rms_norm(x, norm1_g[l]) * (1.0 + sc1) + sh1
        hc = rms_norm(cx, norm1_g[l]) * (1.0 + csc1) + csh1
        q, k, v, cu, pu = split_in(hl @ w_in[l])
        if last:
            kvc = hc @ w_in[l][:, ATTN_W:ATTN_W + 2 * KV_W]
            kc, vc = [t.reshape(b, n_ctx, KV_HEADS, HEAD_DIM) for t in jnp.split(kvc, 2, axis=-1)]
        else:
            qc, kc, vc, cuc, puc = split_in(hc @ w_in[l])
        q = apply_rope(q, cos, sin)
        k = apply_rope(k, cos, sin)
        attn = window_attention(q, k, v, kc, vc, attn_sink[l])
        x = x + g1 * mixer_output(attn, cu, pu, w_out[l], conv_dw[l], conv_dw_b[l],
                                  conv_ln_g[l], conv_ln_b[l], pool_w[l], pool_scale[l])
        if not last:
            attn_c = context_attention(qc, kc, vc, attn_sink[l])
            cx = cx + cg1 * mixer_output(attn_c, cuc, puc, w_out[l], conv_dw[l], conv_dw_b[l],
                                         conv_ln_g[l], conv_ln_b[l], pool_w[l], pool_scale[l])

        x = x + g2 * swiglu(rms_norm(x, norm2_g[l]) * (1.0 + sc2) + sh2, w_ffn_in[l], w_ffn_out[l])
        if not last:
            cx = cx + cg2 * swiglu(rms_norm(cx, norm2_g[l]) * (1.0 + csc2) + csh2, w_ffn_in[l], w_ffn_out[l])
    return rms_norm(x, final_g)


import jax as _jax
import jax.numpy as _jnp

TWIN_FORMAT = 'train_step'
FWD_PARAMS = ['x', 'c', 'ctx', 'c_ctx', 'w_mod', 'b_mod', 'norm1_g', 'norm2_g', 'w_in', 'conv_dw', 'conv_dw_b', 'conv_ln_g', 'conv_ln_b', 'attn_sink', 'pool_w', 'pool_scale', 'w_out', 'w_ffn_in', 'w_ffn_out', 'final_g']
TWIN_WEIGHTS = ['c_ctx', 'w_mod', 'b_mod', 'norm1_g', 'norm2_g', 'w_in', 'conv_dw', 'conv_dw_b', 'conv_ln_g', 'conv_ln_b', 'attn_sink', 'pool_w', 'pool_scale', 'w_out', 'w_ffn_in', 'w_ffn_out', 'final_g']
TWIN_DIFF_INPUT = 'x'
TWIN_INPUTS = ['x', 'c', 'ctx', 'c_ctx', 'w_mod', 'b_mod', 'norm1_g', 'norm2_g', 'w_in', 'conv_dw', 'conv_dw_b', 'conv_ln_g', 'conv_ln_b', 'attn_sink', 'pool_w', 'pool_scale', 'w_out', 'w_ffn_in', 'w_ffn_out', 'final_g', 'loss_target', 'm_c_ctx', 'm_w_mod', 'm_b_mod', 'm_norm1_g', 'm_norm2_g', 'm_w_in', 'm_conv_dw', 'm_conv_dw_b', 'm_conv_ln_g', 'm_conv_ln_b', 'm_attn_sink', 'm_pool_w', 'm_pool_scale', 'm_w_out', 'm_w_ffn_in', 'm_w_ffn_out', 'm_final_g', 'v_c_ctx', 'v_w_mod', 'v_b_mod', 'v_norm1_g', 'v_norm2_g', 'v_w_in', 'v_conv_dw', 'v_conv_dw_b', 'v_conv_ln_g', 'v_conv_ln_b', 'v_attn_sink', 'v_pool_w', 'v_pool_scale', 'v_w_out', 'v_w_ffn_in', 'v_w_ffn_out', 'v_final_g']
TWIN_OUTPUTS = ['loss', 'grad_x', 'grad_c_ctx', 'grad_w_mod', 'grad_b_mod', 'grad_norm1_g', 'grad_norm2_g', 'grad_w_in', 'grad_conv_dw', 'grad_conv_dw_b', 'grad_conv_ln_g', 'grad_conv_ln_b', 'grad_attn_sink', 'grad_pool_w', 'grad_pool_scale', 'grad_w_out', 'grad_w_ffn_in', 'grad_w_ffn_out', 'grad_final_g', 'delta_c_ctx', 'delta_w_mod', 'delta_b_mod', 'delta_norm1_g', 'delta_norm2_g', 'delta_w_in', 'delta_conv_dw', 'delta_conv_dw_b', 'delta_conv_ln_g', 'delta_conv_ln_b', 'delta_attn_sink', 'delta_pool_w', 'delta_pool_scale', 'delta_w_out', 'delta_w_ffn_in', 'delta_w_ffn_out', 'delta_final_g', 'new_m_c_ctx', 'new_m_w_mod', 'new_m_b_mod', 'new_m_norm1_g', 'new_m_norm2_g', 'new_m_w_in', 'new_m_conv_dw', 'new_m_conv_dw_b', 'new_m_conv_ln_g', 'new_m_conv_ln_b', 'new_m_attn_sink', 'new_m_pool_w', 'new_m_pool_scale', 'new_m_w_out', 'new_m_w_ffn_in', 'new_m_w_ffn_out', 'new_m_final_g', 'new_v_c_ctx', 'new_v_w_mod', 'new_v_b_mod', 'new_v_norm1_g', 'new_v_norm2_g', 'new_v_w_in', 'new_v_conv_dw', 'new_v_conv_dw_b', 'new_v_conv_ln_g', 'new_v_conv_ln_b', 'new_v_attn_sink', 'new_v_pool_w', 'new_v_pool_scale', 'new_v_w_out', 'new_v_w_ffn_in', 'new_v_w_ffn_out', 'new_v_final_g']
TWIN_LEAF_KINDS = {'loss': 'loss', 'grad_x': 'grad_x', 'grad_c_ctx': 'grad_w', 'grad_w_mod': 'grad_w', 'grad_b_mod': 'grad_w', 'grad_norm1_g': 'grad_w', 'grad_norm2_g': 'grad_w', 'grad_w_in': 'grad_w', 'grad_conv_dw': 'grad_w', 'grad_conv_dw_b': 'grad_w', 'grad_conv_ln_g': 'grad_w', 'grad_conv_ln_b': 'grad_w', 'grad_attn_sink': 'grad_w', 'grad_pool_w': 'grad_w', 'grad_pool_scale': 'grad_w', 'grad_w_out': 'grad_w', 'grad_w_ffn_in': 'grad_w', 'grad_w_ffn_out': 'grad_w', 'grad_final_g': 'grad_w', 'delta_c_ctx': 'delta_w', 'delta_w_mod': 'delta_w', 'delta_b_mod': 'delta_w', 'delta_norm1_g': 'delta_w', 'delta_norm2_g': 'delta_w', 'delta_w_in': 'delta_w', 'delta_conv_dw': 'delta_w', 'delta_conv_dw_b': 'delta_w', 'delta_conv_ln_g': 'delta_w', 'delta_conv_ln_b': 'delta_w', 'delta_attn_sink': 'delta_w', 'delta_pool_w': 'delta_w', 'delta_pool_scale': 'delta_w', 'delta_w_out': 'delta_w', 'delta_w_ffn_in': 'delta_w', 'delta_w_ffn_out': 'delta_w', 'delta_final_g': 'delta_w', 'new_m_c_ctx': 'new_m', 'new_m_w_mod': 'new_m', 'new_m_b_mod': 'new_m', 'new_m_norm1_g': 'new_m', 'new_m_norm2_g': 'new_m', 'new_m_w_in': 'new_m', 'new_m_conv_dw': 'new_m', 'new_m_conv_dw_b': 'new_m', 'new_m_conv_ln_g': 'new_m', 'new_m_conv_ln_b': 'new_m', 'new_m_attn_sink': 'new_m', 'new_m_pool_w': 'new_m', 'new_m_pool_scale': 'new_m', 'new_m_w_out': 'new_m', 'new_m_w_ffn_in': 'new_m', 'new_m_w_ffn_out': 'new_m', 'new_m_final_g': 'new_m', 'new_v_c_ctx': 'new_v', 'new_v_w_mod': 'new_v', 'new_v_b_mod': 'new_v', 'new_v_norm1_g': 'new_v', 'new_v_norm2_g': 'new_v', 'new_v_w_in': 'new_v', 'new_v_conv_dw': 'new_v', 'new_v_conv_dw_b': 'new_v', 'new_v_conv_ln_g': 'new_v', 'new_v_conv_ln_b': 'new_v', 'new_v_attn_sink': 'new_v', 'new_v_pool_w': 'new_v', 'new_v_pool_scale': 'new_v', 'new_v_w_out': 'new_v', 'new_v_w_ffn_in': 'new_v', 'new_v_w_ffn_out': 'new_v', 'new_v_final_g': 'new_v'}


def _forward(args):
    return _fwd_reference(*[args[k] for k in FWD_PARAMS])


def _output_shape():
    out = _jax.eval_shape(lambda: _forward(_fwd_setup_inputs(0)))
    return out.shape, out.dtype

N_MICROBATCH = 1
ADAM_LR = 0.001
ADAM_B1 = 0.9
ADAM_B2 = 0.999
ADAM_EPS = 1e-08
ADAM_WD = 0.01
ADAM_STEP = 10
PER_EXAMPLE_BATCH_AXIS = {'x': 0, 'c': 0, 'ctx': 0, 'loss_target': 0}
SHARED_INPUTS = []
_WEIGHT_DTYPES = {'c_ctx': _jnp.float32, 'w_mod': _jnp.float32, 'b_mod': _jnp.float32, 'norm1_g': _jnp.float32, 'norm2_g': _jnp.float32, 'w_in': _jnp.float32, 'conv_dw': _jnp.float32, 'conv_dw_b': _jnp.float32, 'conv_ln_g': _jnp.float32, 'conv_ln_b': _jnp.float32, 'attn_sink': _jnp.float32, 'pool_w': _jnp.float32, 'pool_scale': _jnp.float32, 'w_out': _jnp.float32, 'w_ffn_in': _jnp.float32, 'w_ffn_out': _jnp.float32, 'final_g': _jnp.float32}
MOMENT_SCALE = {'c_ctx': 1.587519e-02, 'w_mod': 8.701546e-02, 'b_mod': 1.533405e-01, 'norm1_g': 4.437856e-02, 'norm2_g': 7.577422e-02, 'w_in': 3.865448e-02, 'conv_dw': 4.890557e-02, 'conv_dw_b': 9.631899e-02, 'conv_ln_g': 6.018342e-02, 'conv_ln_b': 5.940947e-02, 'attn_sink': 2.705150e-04, 'pool_w': 7.191907e-02, 'pool_scale': 7.228647e-02, 'w_out': 4.662048e-02, 'w_ffn_in': 3.339373e-02, 'w_ffn_out': 5.436585e-02, 'final_g': 6.397252e+01}


def _to_microbatches(a, axis):
    t = _jnp.moveaxis(a, axis, 0)
    t = t.reshape((N_MICROBATCH, t.shape[0] // N_MICROBATCH) + t.shape[1:])
    return _jnp.moveaxis(t, 1, axis + 1)


def setup_inputs(seed: int = 0) -> dict:
    inp = _fwd_setup_inputs(seed)
    key = _jax.random.fold_in(_jax.random.key(seed), 7919)
    shape, _ = _output_shape()
    out = dict(inp)
    out["loss_target"] = _jax.random.normal(_jax.random.fold_in(key, 0), shape, _jnp.float32)
    for i, name in enumerate(TWIN_WEIGHTS):
        w = inp[name].astype(_jnp.float32)
        if MOMENT_SCALE is None:
            s = _jnp.sqrt(_jnp.mean(_jnp.square(w)) + 1e-30)
        else:
            s = MOMENT_SCALE[name]
        km, kv = _jax.random.split(_jax.random.fold_in(key, i + 1))
        out[name] = w
        out["m_" + name] = s * _jax.random.normal(km, w.shape, _jnp.float32)
        out["v_" + name] = (s * s) * _jax.random.uniform(kv, w.shape, _jnp.float32, 0.5, 1.5)
    if N_MICROBATCH > 1:
        for name, axis in PER_EXAMPLE_BATCH_AXIS.items():
            out[name] = _to_microbatches(out[name], axis)
    return {'x': out['x'], 'c': out['c'], 'ctx': out['ctx'], 'c_ctx': out['c_ctx'], 'w_mod': out['w_mod'], 'b_mod': out['b_mod'], 'norm1_g': out['norm1_g'], 'norm2_g': out['norm2_g'], 'w_in': out['w_in'], 'conv_dw': out['conv_dw'], 'conv_dw_b': out['conv_dw_b'], 'conv_ln_g': out['conv_ln_g'], 'conv_ln_b': out['conv_ln_b'], 'attn_sink': out['attn_sink'], 'pool_w': out['pool_w'], 'pool_scale': out['pool_scale'], 'w_out': out['w_out'], 'w_ffn_in': out['w_ffn_in'], 'w_ffn_out': out['w_ffn_out'], 'final_g': out['final_g'], 'loss_target': out['loss_target'], 'm_c_ctx': out['m_c_ctx'], 'm_w_mod': out['m_w_mod'], 'm_b_mod': out['m_b_mod'], 'm_norm1_g': out['m_norm1_g'], 'm_norm2_g': out['m_norm2_g'], 'm_w_in': out['m_w_in'], 'm_conv_dw': out['m_conv_dw'], 'm_conv_dw_b': out['m_conv_dw_b'], 'm_conv_ln_g': out['m_conv_ln_g'], 'm_conv_ln_b': out['m_conv_ln_b'], 'm_attn_sink': out['m_attn_sink'], 'm_pool_w': out['m_pool_w'], 'm_pool_scale': out['m_pool_scale'], 'm_w_out': out['m_w_out'], 'm_w_ffn_in': out['m_w_ffn_in'], 'm_w_ffn_out': out['m_w_ffn_out'], 'm_final_g': out['m_final_g'], 'v_c_ctx': out['v_c_ctx'], 'v_w_mod': out['v_w_mod'], 'v_b_mod': out['v_b_mod'], 'v_norm1_g': out['v_norm1_g'], 'v_norm2_g': out['v_norm2_g'], 'v_w_in': out['v_w_in'], 'v_conv_dw': out['v_conv_dw'], 'v_conv_dw_b': out['v_conv_dw_b'], 'v_conv_ln_g': out['v_conv_ln_g'], 'v_conv_ln_b': out['v_conv_ln_b'], 'v_attn_sink': out['v_attn_sink'], 'v_pool_w': out['v_pool_w'], 'v_pool_scale': out['v_pool_scale'], 'v_w_out': out['v_w_out'], 'v_w_ffn_in': out['v_w_ffn_in'], 'v_w_ffn_out': out['v_w_ffn_out'], 'v_final_g': out['v_final_g']}


def _loss(weights, diff, rest, loss_target):
    with _jax.named_scope("forward"):
        args = {**rest, TWIN_DIFF_INPUT: diff, **{k: w.astype(_WEIGHT_DTYPES[k]) for k, w in weights.items()}}
        y = _forward(args)
    with _jax.named_scope("loss_head"):
        err = _jnp.square(y.astype(_jnp.float32) - loss_target)
        return 0.5 * _jnp.sum(_jnp.mean(err, axis=-1)) if err.ndim else 0.5 * err


def _adamw(w, g, m, v):
    m = ADAM_B1 * m + (1.0 - ADAM_B1) * g
    v = ADAM_B2 * v + (1.0 - ADAM_B2) * _jnp.square(g)
    m_hat = m / (1.0 - ADAM_B1 ** ADAM_STEP)
    v_hat = v / (1.0 - ADAM_B2 ** ADAM_STEP)
    delta = -ADAM_LR * (m_hat / (_jnp.sqrt(v_hat) + ADAM_EPS) + ADAM_WD * w)
    return delta, m, v


def reference(x, c, ctx, c_ctx, w_mod, b_mod, norm1_g, norm2_g, w_in, conv_dw, conv_dw_b, conv_ln_g, conv_ln_b, attn_sink, pool_w, pool_scale, w_out, w_ffn_in, w_ffn_out, final_g, loss_target, m_c_ctx, m_w_mod, m_b_mod, m_norm1_g, m_norm2_g, m_w_in, m_conv_dw, m_conv_dw_b, m_conv_ln_g, m_conv_ln_b, m_attn_sink, m_pool_w, m_pool_scale, m_w_out, m_w_ffn_in, m_w_ffn_out, m_final_g, v_c_ctx, v_w_mod, v_b_mod, v_norm1_g, v_norm2_g, v_w_in, v_conv_dw, v_conv_dw_b, v_conv_ln_g, v_conv_ln_b, v_attn_sink, v_pool_w, v_pool_scale, v_w_out, v_w_ffn_in, v_w_ffn_out, v_final_g):
    given = dict(x=x, c=c, ctx=ctx, c_ctx=c_ctx, w_mod=w_mod, b_mod=b_mod, norm1_g=norm1_g, norm2_g=norm2_g, w_in=w_in, conv_dw=conv_dw, conv_dw_b=conv_dw_b, conv_ln_g=conv_ln_g, conv_ln_b=conv_ln_b, attn_sink=attn_sink, pool_w=pool_w, pool_scale=pool_scale, w_out=w_out, w_ffn_in=w_ffn_in, w_ffn_out=w_ffn_out, final_g=final_g, loss_target=loss_target, m_c_ctx=m_c_ctx, m_w_mod=m_w_mod, m_b_mod=m_b_mod, m_norm1_g=m_norm1_g, m_norm2_g=m_norm2_g, m_w_in=m_w_in, m_conv_dw=m_conv_dw, m_conv_dw_b=m_conv_dw_b, m_conv_ln_g=m_conv_ln_g, m_conv_ln_b=m_conv_ln_b, m_attn_sink=m_attn_sink, m_pool_w=m_pool_w, m_pool_scale=m_pool_scale, m_w_out=m_w_out, m_w_ffn_in=m_w_ffn_in, m_w_ffn_out=m_w_ffn_out, m_final_g=m_final_g, v_c_ctx=v_c_ctx, v_w_mod=v_w_mod, v_b_mod=v_b_mod, v_norm1_g=v_norm1_g, v_norm2_g=v_norm2_g, v_w_in=v_w_in, v_conv_dw=v_conv_dw, v_conv_dw_b=v_conv_dw_b, v_conv_ln_g=v_conv_ln_g, v_conv_ln_b=v_conv_ln_b, v_attn_sink=v_attn_sink, v_pool_w=v_pool_w, v_pool_scale=v_pool_scale, v_w_out=v_w_out, v_w_ffn_in=v_w_ffn_in, v_w_ffn_out=v_w_ffn_out, v_final_g=v_final_g)
    weights = {n: given[n] for n in TWIN_WEIGHTS}
    shared = {n: given[n] for n in SHARED_INPUTS}
    per_example = {n: given[n] for n in ['x', 'c', 'ctx']}
    grad_fn = _jax.value_and_grad(_loss, argnums=(0, 1))

    def one_microbatch(ex, loss_target):
        ex = dict(ex)
        diff = ex.pop(TWIN_DIFF_INPUT)
        return grad_fn(weights, diff, {**shared, **ex}, loss_target)

    if N_MICROBATCH == 1:
        loss, (grad_w, grad_x) = one_microbatch(per_example, given["loss_target"])
    else:
        def body(carry, xs):
            loss_sum, grad_sum = carry
            l_k, (gw_k, gx_k) = one_microbatch(xs[0], xs[1])
            with _jax.named_scope("update"):
                return (loss_sum + l_k, _jax.tree.map(_jnp.add, grad_sum, gw_k)), gx_k

        init = (_jnp.zeros((), _jnp.float32), _jax.tree.map(_jnp.zeros_like, weights))
        (loss, grad_w), grad_x = _jax.lax.scan(body, init, (per_example, given["loss_target"]))
    with _jax.named_scope("update"):
        delta_w, new_m, new_v = {}, {}, {}
        for n in TWIN_WEIGHTS:
            delta_w[n], new_m[n], new_v[n] = _adamw(weights[n], grad_w[n], given["m_" + n], given["v_" + n])
    return (loss, grad_x, *[grad_w[n] for n in TWIN_WEIGHTS], *[delta_w[n] for n in TWIN_WEIGHTS],
            *[new_m[n] for n in TWIN_WEIGHTS], *[new_v[n] for n in TWIN_WEIGHTS])
```

```python
import functools

import numpy as np
import jax
import jax.numpy as jnp
from jax import lax
from jax.experimental import pallas as pl
from jax.experimental.pallas import tpu as pltpu

F32 = jnp.float32
BF16 = jnp.bfloat16

D_MODEL = 1024
GRID_W = 64
HEAD_DIM = 64
ATTN_W = 512
CONV_W = 256
POOL_W = 256
ATTN_HEADS = 8
KV_HEADS = 2
GROUP = ATTN_HEADS // KV_HEADS
KV_W = KV_HEADS * HEAD_DIM
IN_W = ATTN_W + 2 * KV_W + 2 * CONV_W + POOL_W
WINDOW = 128
Q_BLOCK = 128
SPAN = Q_BLOCK + 2 * WINDOW
CONV_KERNEL = 31
POOL_WINDOWS = (2, 4, 8, 16)
POOL_GROUP = 64
ROPE_BASE = 10000.0
D_FF = 2816
EPS = 1e-6
NEG = -1e30
N_DEV = 8
AXES = ("x", "y", "c")

ADAM_LR = 0.001
ADAM_B1 = 0.9
ADAM_B2 = 0.999
ADAM_EPS = 1e-08
ADAM_WD = 0.01
ADAM_STEP = 10

VMEM_LIMIT = 56 * 1024 * 1024
HALO = 16
SEQ_CHUNK = 256


def _params(*sem):
    return pltpu.CompilerParams(dimension_semantics=sem, vmem_limit_bytes=VMEM_LIMIT)


def _tile(dim, target):
    if dim <= target:
        return dim
    t = (target // 128) * 128
    while t >= 128:
        if dim % t == 0:
            return t
        t -= 128
    raise ValueError(f"no tile for {dim}")


def _mm(a, b, *, name, trans_a=False, trans_b=False, out_dtype=F32, tm=512, tn=512, tk=1024):
    if trans_a:
        K, M = a.shape
    else:
        M, K = a.shape
    if trans_b:
        N, K2 = b.shape
    else:
        K2, N = b.shape
    assert K == K2, (a.shape, b.shape)
    tm = _tile(M, tm)
    tn = _tile(N, tn)
    tk = _tile(K, tk) if trans_a else K
    nk = K // tk
    dims = (((0,) if trans_a else (1,), (1,) if trans_b else (0,)), ((), ()))

    def body(a_ref, b_ref, o_ref, *scratch):
        part = lax.dot_general(a_ref[...].astype(BF16), b_ref[...].astype(BF16), dims, preferred_element_type=F32)
        if nk == 1:
            o_ref[...] = part.astype(o_ref.dtype)
        else:
            acc_ref, = scratch
            k = pl.program_id(2)

            @pl.when(k == 0)
            def _():
                acc_ref[...] = part

            @pl.when(k > 0)
            def _():
                acc_ref[...] += part

            @pl.when(k == nk - 1)
            def _():
                o_ref[...] = acc_ref[...].astype(o_ref.dtype)

    a_spec = pl.BlockSpec((tk, tm), lambda i, j, k: (k, i)) if trans_a else pl.BlockSpec((tm, tk), lambda i, j, k: (i, k))
    b_spec = pl.BlockSpec((tn, tk), lambda i, j, k: (j, k)) if trans_b else pl.BlockSpec((tk, tn), lambda i, j, k: (k, j))
    return pl.pallas_call(
        body, name=name, grid=(M // tm, N // tn, nk),
        in_specs=[a_spec, b_spec], out_specs=pl.BlockSpec((tm, tn), lambda i, j, k: (i, j)),
        out_shape=jax.ShapeDtypeStruct((M, N), out_dtype),
        scratch_shapes=[pltpu.VMEM((tm, tn), F32)] if nk > 1 else [],
        compiler_params=_params("parallel", "parallel", "arbitrary"),
    )(a, b)


def _make_linear(tag):
    @jax.custom_vjp
    def linear(a, w):
        lead = a.shape[:-1]
        y = _mm(a.reshape(-1, a.shape[-1]), w, name=f"{tag}_fwd")
        return y.reshape(*lead, w.shape[1])

    def fwd(a, w):
        return linear(a, w), (a, w)

    def bwd(res, dy):
        a, w = res
        a2 = a.reshape(-1, a.shape[-1])
        dy2 = dy.reshape(-1, dy.shape[-1])
        da = _mm(dy2, w, trans_b=True, name=f"{tag}_dx")
        dw = _mm(a2, dy2, trans_a=True, out_dtype=BF16, name=f"{tag}_dw")
        return da.reshape(a.shape), dw

    linear.defvjp(fwd, bwd)
    return linear


def _rope_tables(n):
    rows = n // GRID_W
    row = jnp.repeat(jnp.arange(rows), GRID_W).astype(F32)
    col = jnp.tile(jnp.arange(GRID_W), rows).astype(F32)
    half = HEAD_DIM // 2
    inv = ROPE_BASE ** (-jnp.arange(0, half, 2, dtype=F32) / half)
    ar = row[:, None] * inv
    ac = col[:, None] * inv
    ang = jnp.concatenate([ar, ar, ac, ac], axis=-1)
    return jnp.cos(ang), jnp.sin(ang)


def _rot_half(x):
    w = x.shape[-1]
    lane = lax.broadcasted_iota(jnp.int32, x.shape, 1)
    up = pltpu.roll(x, w - 16, 1)
    down = pltpu.roll(x, 16, 1)
    return jnp.where((lane & 16) == 0, -up, down)


def _rope(x, cos, sin):
    return x * cos + _rot_half(x) * sin


def _rope_bwd(d, cos, sin):
    return d * cos - _rot_half(d * sin)


def _attn_core(q_st, k_all, v_all, sink_rows, valid):
    s = lax.dot_general(q_st.astype(BF16), k_all.astype(BF16), (((1,), (1,)), ((), ())),
                        preferred_element_type=F32) * (HEAD_DIM ** -0.5)
    if valid is not None:
        s = jnp.where(valid, s, NEG)
    mx = jnp.maximum(jnp.max(s, axis=1, keepdims=True), sink_rows)
    e = jnp.exp(s - mx)
    den = jnp.sum(e, axis=1, keepdims=True) + jnp.exp(sink_rows - mx)
    p = e / den
    return lax.dot_general(p.astype(BF16), v_all.astype(BF16), (((1,), (0,)), ((), ())), preferred_element_type=F32)


def _stack_heads(x, kh):
    return jnp.concatenate([x[:, (GROUP * kh + g) * HEAD_DIM:(GROUP * kh + g + 1) * HEAD_DIM] for g in range(GROUP)], axis=0)


def _sink_rows(sink, kh):
    return jnp.concatenate([jnp.broadcast_to(sink[:, GROUP * kh + g:GROUP * kh + g + 1], (Q_BLOCK, 1)) for g in range(GROUP)], axis=0)


def _valid_mask(start, s0, n_ctx):
    rows = GROUP * Q_BLOCK
    cols = n_ctx + SPAN
    r = lax.broadcasted_iota(jnp.int32, (rows, cols), 0)
    c = lax.broadcasted_iota(jnp.int32, (rows, cols), 1)
    qpos = start + (r & (Q_BLOCK - 1))
    kpos = s0 + c - n_ctx
    return (c < n_ctx) | (jnp.abs(qpos - kpos) <= WINDOW)


def _span_start(i, n):
    start = i * Q_BLOCK
    s0 = jnp.clip(start - WINDOW, 0, n - SPAN)
    return start, pl.multiple_of(s0, Q_BLOCK)


def _attn_fwd(u, kvc, sink, cos, sin, *, local, name):
    B, n, _ = u.shape
    n_ctx = kvc.shape[1]
    nb = n // Q_BLOCK
    assert (not local) or n >= SPAN

    def body(q_ref, k_ref, v_ref, kc_ref, vc_ref, sink_ref, cq_ref, sq_ref, ck_ref, sk_ref, o_ref):
        i = pl.program_id(1)
        q = q_ref[...]
        sink_v = sink_ref[...]
        kc = kc_ref[...]
        vc = vc_ref[...]
        if local:
            start, s0 = _span_start(i, n)
            q = _rope(q, cq_ref[...], sq_ref[...])
            ks = _rope(k_ref[pl.ds(s0, SPAN), :], ck_ref[pl.ds(s0, SPAN), :], sk_ref[pl.ds(s0, SPAN), :])
            vs = v_ref[pl.ds(s0, SPAN), :]
            k_cat = jnp.concatenate([kc, ks], axis=0)
            v_cat = jnp.concatenate([vc, vs], axis=0)
            valid = _valid_mask(start, s0, n_ctx)
        else:
            k_cat, v_cat, valid = kc, vc, None
        for kh in range(KV_HEADS):
            sl = slice(kh * HEAD_DIM, (kh + 1) * HEAD_DIM)
            o = _attn_core(_stack_heads(q, kh), k_cat[:, sl], v_cat[:, sl], _sink_rows(sink_v, kh), valid)
            for g in range(GROUP):
                h = GROUP * kh + g
                o_ref[:, h * HEAD_DIM:(h + 1) * HEAD_DIM] = o[g * Q_BLOCK:(g + 1) * Q_BLOCK, :]

    seq = lambda blk: pl.BlockSpec((None, n, KV_W), lambda b, i: (b, 0, blk))
    ctxs = lambda blk: pl.BlockSpec((None, n_ctx, KV_W), lambda b, i: (b, 0, blk))
    full = lambda a: pl.BlockSpec(a.shape, lambda b, i: (0,) * a.ndim)
    cos_q, sin_q = jnp.tile(cos, (1, ATTN_HEADS)), jnp.tile(sin, (1, ATTN_HEADS))
    cos_k, sin_k = jnp.tile(cos, (1, KV_HEADS)), jnp.tile(sin, (1, KV_HEADS))
    return pl.pallas_call(
        body, name=name, grid=(B, nb),
        in_specs=[pl.BlockSpec((None, Q_BLOCK, ATTN_W), lambda b, i: (b, i, 0)),
                  seq(ATTN_W // KV_W), seq(ATTN_W // KV_W + 1), ctxs(0), ctxs(1), full(sink),
                  pl.BlockSpec((Q_BLOCK, ATTN_W), lambda b, i: (i, 0)), pl.BlockSpec((Q_BLOCK, ATTN_W), lambda b, i: (i, 0)),
                  full(cos_k), full(sin_k)],
        out_specs=pl.BlockSpec((None, Q_BLOCK, ATTN_W), lambda b, i: (b, i, 0)),
        out_shape=jax.ShapeDtypeStruct((B, n, ATTN_W), F32),
        compiler_params=_params("parallel", "arbitrary"),
    )(u, u, u, kvc, kvc, sink, cos_q, sin_q, cos_k, sin_k)


def _attn_bwd(u, kvc, sink, cos, sin, do_src, do_blk, *, local, name):
    B, n, _ = u.shape
    n_ctx = kvc.shape[1]
    nb = n // Q_BLOCK

    def body(q_ref, k_ref, v_ref, kc_ref, vc_ref, sink_ref, cq_ref, sq_ref, ck_ref, sk_ref, do_ref,
             dq_ref, dk_ref, dv_ref, dkc_ref, dvc_ref, dsink_ref):
        b = pl.program_id(0)
        i = pl.program_id(1)

        @pl.when(i == 0)
        def _():
            dk_ref[...] = jnp.zeros_like(dk_ref)
            dv_ref[...] = jnp.zeros_like(dv_ref)
            dkc_ref[...] = jnp.zeros_like(dkc_ref)
            dvc_ref[...] = jnp.zeros_like(dvc_ref)

        @pl.when((i == 0) & (b == 0))
        def _():
            dsink_ref[...] = jnp.zeros_like(dsink_ref)

        q = q_ref[...]
        do = do_ref[...]
        sink_v = sink_ref[...]
        kc = kc_ref[...]
        vc = vc_ref[...]
        if local:
            start, s0 = _span_start(i, n)
            ck = ck_ref[pl.ds(s0, SPAN), :]
            sk = sk_ref[pl.ds(s0, SPAN), :]
            q = _rope(q, cq_ref[...], sq_ref[...])
            ks = _rope(k_ref[pl.ds(s0, SPAN), :], ck, sk)
            vs = v_ref[pl.ds(s0, SPAN), :]
            k_cat = jnp.concatenate([kc, ks], axis=0)
            v_cat = jnp.concatenate([vc, vs], axis=0)
            valid = _valid_mask(start, s0, n_ctx)
        else:
            k_cat, v_cat, valid = kc, vc, None
        dsink = jnp.zeros((1, ATTN_HEADS), F32)
        lane8 = lax.broadcasted_iota(jnp.int32, (1, ATTN_HEADS), 1)
        dk_parts, dv_parts = [], []
        for kh in range(KV_HEADS):
            sl = slice(kh * HEAD_DIM, (kh + 1) * HEAD_DIM)
            core = functools.partial(_attn_core, valid=valid)
            _, vjp = jax.vjp(core, _stack_heads(q, kh), k_cat[:, sl], v_cat[:, sl], _sink_rows(sink_v, kh))
            dq_st, dk_all, dv_all, dsr = vjp(_stack_heads(do, kh))
            for g in range(GROUP):
                h = GROUP * kh + g
                dq_ref[:, h * HEAD_DIM:(h + 1) * HEAD_DIM] = dq_st[g * Q_BLOCK:(g + 1) * Q_BLOCK, :]
                dsink = dsink + jnp.where(lane8 == h, jnp.sum(dsr[g * Q_BLOCK:(g + 1) * Q_BLOCK, :]), 0.0)
            dk_parts.append(dk_all)
            dv_parts.append(dv_all)
        dk_cat = jnp.concatenate(dk_parts, axis=1)
        dv_cat = jnp.concatenate(dv_parts, axis=1)
        dsink_ref[...] += dsink
        dkc_ref[...] += dk_cat[:n_ctx, :]
        dvc_ref[...] += dv_cat[:n_ctx, :]
        if local:
            dq_ref[...] = _rope_bwd(dq_ref[...], cq_ref[...], sq_ref[...])
            dk_ref[pl.ds(s0, SPAN), :] += _rope_bwd(dk_cat[n_ctx:, :], ck, sk)
            dv_ref[pl.ds(s0, SPAN), :] += dv_cat[n_ctx:, :]

    seq = lambda blk: pl.BlockSpec((None, n, KV_W), lambda b, i: (b, 0, blk))
    ctxs = lambda blk: pl.BlockSpec((None, n_ctx, KV_W), lambda b, i: (b, 0, blk))
    full = lambda a: pl.BlockSpec(a.shape, lambda b, i: (0,) * a.ndim)
    qblk = lambda blk: pl.BlockSpec((None, Q_BLOCK, ATTN_W), lambda b, i: (b, i, blk))
    cos_q, sin_q = jnp.tile(cos, (1, ATTN_HEADS)), jnp.tile(sin, (1, ATTN_HEADS))
    cos_k, sin_k = jnp.tile(cos, (1, KV_HEADS)), jnp.tile(sin, (1, KV_HEADS))
    acc = lambda rows: pl.BlockSpec((None, rows, KV_W), lambda b, i: (b, 0, 0))
    return pl.pallas_call(
        body, name=name, grid=(B, nb),
        in_specs=[qblk(0), seq(ATTN_W // KV_W), seq(ATTN_W // KV_W + 1), ctxs(0), ctxs(1), full(sink),
                  pl.BlockSpec((Q_BLOCK, ATTN_W), lambda b, i: (i, 0)), pl.BlockSpec((Q_BLOCK, ATTN_W), lambda b, i: (i, 0)),
                  full(cos_k), full(sin_k), qblk(do_blk)],
        out_specs=[qblk(0), acc(n), acc(n), acc(n_ctx), acc(n_ctx), pl.BlockSpec((1, ATTN_HEADS), lambda b, i: (0, 0))],
        out_shape=[jax.ShapeDtypeStruct((B, n, ATTN_W), F32), jax.ShapeDtypeStruct((B, n, KV_W), F32),
                   jax.ShapeDtypeStruct((B, n, KV_W), F32), jax.ShapeDtypeStruct((B, n_ctx, KV_W), F32),
                   jax.ShapeDtypeStruct((B, n_ctx, KV_W), F32), jax.ShapeDtypeStruct((1, ATTN_HEADS), F32)],
        compiler_params=_params("arbitrary", "arbitrary"),
    )(u, u, u, kvc, kvc, sink, cos_q, sin_q, cos_k, sin_k, do_src)


def _conv_chunk(s, n, a_ext, g_ext, dw, dw_b, ln_g, ln_b):
    del s, n
    r = a_ext.shape[0] - 2 * HALO
    h = a_ext * jax.nn.sigmoid(g_ext)
    acc = jnp.broadcast_to(dw_b, (r, CONV_W))
    first = HALO - CONV_KERNEL // 2
    for k in range(CONV_KERNEL):
        acc = acc + h[first + k:first + k + r, :] * dw[k:k + 1, :]
    mu = jnp.mean(acc, axis=-1, keepdims=True)
    var = jnp.mean(jnp.square(acc - mu), axis=-1, keepdims=True)
    hn = (acc - mu) * lax.rsqrt(var + EPS) * ln_g + ln_b
    return hn * jax.nn.sigmoid(hn)


def _pool_chunk(s, n, p_ext, w_bd, scale):
    r = p_ext.shape[0] - 2 * HALO
    lane = lax.broadcasted_iota(jnp.int32, (1, POOL_W), 1)
    win = jnp.left_shift(2, lane // POOL_GROUP)
    half = win // 2
    acc = jnp.zeros((r, POOL_W), F32)
    for d in range(-(POOL_WINDOWS[-1] // 2), POOL_WINDOWS[-1] - POOL_WINDOWS[-1] // 2):
        inside = (d >= -half) & (d <= win - 1 - half)
        acc = acc + jnp.where(inside, p_ext[HALO + d:HALO + d + r, :], 0.0)
    t = s + lax.broadcasted_iota(jnp.int32, (r, 1), 0)
    lo = jnp.maximum(t - half, 0)
    hi = jnp.minimum(t + win - 1 - half, n - 1)
    y = acc / (hi - lo + 1).astype(F32) - p_ext[HALO:HALO + r, :]
    out = lax.dot_general(y.astype(BF16), w_bd.astype(BF16), (((1,), (0,)), ((), ())), preferred_element_type=F32)
    return out * scale


def _seq_specs(rows, params):
    specs = [pl.BlockSpec((None, a.shape[1], w), functools.partial(lambda b, blk: (b, 0, blk), blk=blk)) for a, w, blk in rows]
    specs += [pl.BlockSpec(p.shape, functools.partial(lambda b, nd: (0,) * nd, nd=p.ndim)) for p in params]
    return specs


def _fill_padded(pad_ref, row_ref, n):
    w = pad_ref.shape[1]
    pad_ref[pl.ds(0, HALO), :] = jnp.zeros((HALO, w), F32)
    pad_ref[pl.ds(HALO + n, HALO), :] = jnp.zeros((HALO, w), F32)
    pad_ref[pl.ds(HALO, n), :] = row_ref[...]


def _seq_fwd(fn, rows, params, out_w, *, name):
    B, n = rows[0][0].shape[:2]
    r = min(SEQ_CHUNK, n)
    nr, npar = len(rows), len(params)

    def body(*refs):
        row_refs, par_refs, o_ref, pads = refs[:nr], refs[nr:nr + npar], refs[nr + npar], refs[nr + npar + 1:]
        for rr, p in zip(row_refs, pads):
            _fill_padded(p, rr, n)
        pars = [p[...] for p in par_refs]

        def chunk(ci, carry):
            s = pl.multiple_of(ci * r, r)
            ext = [p[pl.ds(s, r + 2 * HALO), :] for p in pads]
            o_ref[pl.ds(s, r), :] = fn(s, n, *ext, *pars)
            return carry

        lax.fori_loop(0, n // r, chunk, 0)

    return pl.pallas_call(
        body, name=name, grid=(B,),
        in_specs=_seq_specs(rows, params),
        out_specs=pl.BlockSpec((None, n, out_w), lambda b: (b, 0, 0)),
        out_shape=jax.ShapeDtypeStruct((B, n, out_w), F32),
        scratch_shapes=[pltpu.VMEM((n + 2 * HALO, w), F32) for _, w, _ in rows],
        compiler_params=_params("parallel"),
    )(*[a for a, _, _ in rows], *params)


def _seq_bwd(fn, rows, params, dout, *, name):
    B, n = rows[0][0].shape[:2]
    r = min(SEQ_CHUNK, n)
    nr, npar = len(rows), len(params)

    def body(*refs):
        row_refs, par_refs, do_ref = refs[:nr], refs[nr:nr + npar], refs[nr + npar]
        outs = refs[nr + npar + 1:]
        drow_refs, dpar_refs = outs[:nr], outs[nr:nr + npar]
        scratch = outs[nr + npar:]
        pads, dpads = scratch[:nr], scratch[nr:]
        for rr, p, dp in zip(row_refs, pads, dpads):
            _fill_padded(p, rr, n)
            dp[...] = jnp.zeros_like(dp)

        @pl.when(pl.program_id(0) == 0)
        def _():
            for d in dpar_refs:
                d[...] = jnp.zeros_like(d)

        pars = [p[...] for p in par_refs]

        def chunk(ci, carry):
            s = pl.multiple_of(ci * r, r)
            ext = [p[pl.ds(s, r + 2 * HALO), :] for p in pads]
            _, vjp = jax.vjp(functools.partial(fn, s, n), *ext, *pars)
            grads = vjp(do_ref[pl.ds(s, r), :])
            for dp, g in zip(dpads, grads[:nr]):
                dp[pl.ds(s, r + 2 * HALO), :] += g
            for d, g in zip(dpar_refs, grads[nr:]):
                d[...] += g
            return carry

        lax.fori_loop(0, n // r, chunk, 0)
        for d, dp in zip(drow_refs, dpads):
            d[...] = dp[pl.ds(HALO, n), :]

    da, dw_, dblk = dout
    return pl.pallas_call(
        body, name=name, grid=(B,),
        in_specs=_seq_specs(rows, params) + [pl.BlockSpec((None, n, dw_), lambda b: (b, 0, dblk))],
        out_specs=[pl.BlockSpec((None, n, w), lambda b: (b, 0, 0)) for _, w, _ in rows]
        + [pl.BlockSpec(p.shape, functools.partial(lambda b, nd: (0,) * nd, nd=p.ndim)) for p in params],
        out_shape=[jax.ShapeDtypeStruct((B, n, w), F32) for _, w, _ in rows]
        + [jax.ShapeDtypeStruct(p.shape, F32) for p in params],
        scratch_shapes=[pltpu.VMEM((n + 2 * HALO, w), F32) for _, w, _ in rows] * 2,
        compiler_params=_params("arbitrary"),
    )(*[a for a, _, _ in rows], *params, da)


_CONV_A_BLK = (ATTN_W + 2 * KV_W) // CONV_W
_CONV_G_BLK = _CONV_A_BLK + 1
_POOL_BLK = _CONV_A_BLK + 2


def _make_mixer(tag, local):
    @jax.custom_vjp
    def mixer(u, kvc, sink, dw, dw_b, ln_g, ln_b, w_bd, scale):
        cos, sin = _rope_tables(max(u.shape[1], GRID_W))
        attn = _attn_fwd(u, kvc, sink, cos, sin, local=local, name=f"{tag}_attn_fwd")
        conv = _seq_fwd(_conv_chunk, [(u, CONV_W, _CONV_A_BLK), (u, CONV_W, _CONV_G_BLK)], [dw, dw_b, ln_g, ln_b], CONV_W,
                        name=f"{tag}_conv_fwd")
        pool = _seq_fwd(_pool_chunk, [(u, POOL_W, _POOL_BLK)], [w_bd, scale], POOL_W, name=f"{tag}_pool_fwd")
        return jnp.concatenate([attn, conv, pool], axis=-1)

    def fwd(*args):
        return mixer(*args), args

    def bwd(res, dmix):
        u, kvc, sink, dw, dw_b, ln_g, ln_b, w_bd, scale = res
        cos, sin = _rope_tables(max(u.shape[1], GRID_W))
        dq, dk, dv, dkc, dvc, dsink = _attn_bwd(u, kvc, sink, cos, sin, dmix, 0, local=local, name=f"{tag}_attn_bwd")
        da, dg, ddw, ddw_b, dln_g, dln_b = _seq_bwd(
            _conv_chunk, [(u, CONV_W, _CONV_A_BLK), (u, CONV_W, _CONV_G_BLK)], [dw, dw_b, ln_g, ln_b],
            (dmix, CONV_W, ATTN_W // CONV_W), name=f"{tag}_conv_bwd")
        dpu, dw_bd, dscale = _seq_bwd(_pool_chunk, [(u, POOL_W, _POOL_BLK)], [w_bd, scale],
                                      (dmix, POOL_W, (ATTN_W + CONV_W) // POOL_W), name=f"{tag}_pool_bwd")
        du = jnp.concatenate([dq, dk, dv, da, dg, dpu], axis=-1)
        dkvc = jnp.concatenate([dkc, dvc], axis=-1)
        return du, dkvc, dsink, ddw, ddw_b, dln_g, dln_b, dw_bd, dscale

    mixer.defvjp(fwd, bwd)
    return mixer


def _row_specs(arrs, kinds, tr):
    specs = []
    for a, kind in zip(arrs, kinds):
        if kind == "row":
            specs.append(pl.BlockSpec((None, tr, a.shape[2]), lambda b, j: (b, j, 0)))
        elif kind == "batch":
            specs.append(pl.BlockSpec((None, 1, a.shape[2]), lambda b, j: (b, 0, 0)))
        else:
            specs.append(pl.BlockSpec(a.shape, functools.partial(lambda b, j, nd: (0,) * nd, nd=a.ndim)))
    return specs


def _rowwise_fwd(fn, ins, kinds, outs, tr, *, name):
    B, n = ins[0].shape[:2]
    ni = len(ins)

    def body(*refs):
        res = fn(*[r[...] for r in refs[:ni]])
        for o, v in zip(refs[ni:], res):
            o[...] = v.astype(o.dtype)

    return pl.pallas_call(
        body, name=name, grid=(B, n // tr),
        in_specs=_row_specs(ins, kinds, tr),
        out_specs=[pl.BlockSpec((None, tr, w), lambda b, j: (b, j, 0)) for w, _ in outs],
        out_shape=[jax.ShapeDtypeStruct((B, n, w), dt) for w, dt in outs],
        compiler_params=_params("parallel", "parallel"),
    )(*ins)


def _rowwise_bwd(fn, ins, kinds, need, cts, tr, *, name):
    B, n = ins[0].shape[:2]
    ni, nc = len(ins), len(cts)
    idx = [i for i in range(ni) if need[i]]

    def body(*refs):
        in_refs, ct_refs, out_refs = refs[:ni], refs[ni:ni + nc], refs[ni + nc:]
        b, j = pl.program_id(0), pl.program_id(1)
        _, vjp = jax.vjp(fn, *[r[...] for r in in_refs])
        grads = vjp(tuple(c[...] for c in ct_refs))
        for o, i in zip(out_refs, idx):
            g = grads[i]
            if kinds[i] == "row":
                o[...] = g
            else:
                first = (j == 0) if kinds[i] == "batch" else ((j == 0) & (b == 0))

                @pl.when(first)
                def _(o=o, g=g):
                    o[...] = g

                @pl.when(jnp.logical_not(first))
                def _(o=o, g=g):
                    o[...] += g

    specs = _row_specs(ins, kinds, tr)
    return pl.pallas_call(
        body, name=name, grid=(B, n // tr),
        in_specs=specs + [pl.BlockSpec((None, tr, c.shape[2]), lambda b, j: (b, j, 0)) for c in cts],
        out_specs=[specs[i] for i in idx],
        out_shape=[jax.ShapeDtypeStruct(ins[i].shape, F32) for i in idx],
        compiler_params=_params("arbitrary", "arbitrary"),
    )(*ins, *cts)


def _make_rowwise(tag, fn, kinds, need, out_widths, tr):
    @jax.custom_vjp
    def op(*ins):
        return tuple(_rowwise_fwd(fn, ins, kinds, [(w, F32) for w in out_widths], min(tr, ins[0].shape[1]), name=f"{tag}_fwd"))

    def fwd(*ins):
        return op(*ins), ins

    def bwd(ins, cts):
        grads = iter(_rowwise_bwd(fn, ins, kinds, need, cts, min(tr, ins[0].shape[1]), name=f"{tag}_bwd"))
        return tuple(next(grads) if nd else jnp.zeros_like(a) for a, nd in zip(ins, need))

    op.defvjp(fwd, bwd)
    return op


def _modnorm_tile(x, g, sc, sh):
    y = x * lax.rsqrt(jnp.mean(x * x, axis=-1, keepdims=True) + EPS)
    return ((y * g) * (1.0 + sc) + sh,)


def _gated_tile(x, y, gate):
    return (x + gate * y,)


def _swiglu_tile(gu):
    g, u = gu[:, :D_FF], gu[:, D_FF:]
    return (g * jax.nn.sigmoid(g) * u,)


def _loss_head(x, final_g, target, *, name):
    B, n, d = x.shape
    tr = min(256, n)

    def tile_loss(xv, g, t):
        y = xv * lax.rsqrt(jnp.mean(xv * xv, axis=-1, keepdims=True) + EPS) * g
        return 0.5 * jnp.sum(jnp.mean(jnp.square(y - t), axis=-1))

    def body(x_ref, g_ref, t_ref, loss_ref, dx_ref, dg_ref):
        b, j = pl.program_id(0), pl.program_id(1)
        val, (dx, dg) = jax.value_and_grad(tile_loss, argnums=(0, 1))(x_ref[...], g_ref[...], t_ref[...])
        dx_ref[...] = dx

        @pl.when(j == 0)
        def _():
            loss_ref[...] = jnp.zeros_like(loss_ref)

        @pl.when((j == 0) & (b == 0))
        def _():
            dg_ref[...] = jnp.zeros_like(dg_ref)

        loss_ref[...] += jnp.full(loss_ref.shape, val, F32)
        dg_ref[...] += dg

    row = pl.BlockSpec((None, tr, d), lambda b, j: (b, j, 0))
    return pl.pallas_call(
        body, name=name, grid=(B, n // tr),
        in_specs=[row, pl.BlockSpec((1, d), lambda b, j: (0, 0)), row],
        out_specs=[pl.BlockSpec((None, 1, 128), lambda b, j: (b, 0, 0)), row, pl.BlockSpec((1, d), lambda b, j: (0, 0))],
        out_shape=[jax.ShapeDtypeStruct((B, 1, 128), F32), jax.ShapeDtypeStruct((B, n, d), F32), jax.ShapeDtypeStruct((1, d), F32)],
        compiler_params=_params("arbitrary", "arbitrary"),
    )(x, final_g, target)


def _exchange(arrs, *, scatter, name):
    k = len(arrs)

    def body(*refs):
        ins, outs = refs[:k], refs[k:2 * k]
        send_sems, recv_sems, local_sems = refs[2 * k:]
        x, y, c = lax.axis_index("x"), lax.axis_index("y"), lax.axis_index("c")
        me = 4 * x + 2 * y + c
        owns, sends, recvs = [], [], []
        for a in range(k):
            own = pltpu.make_async_copy(ins[a].at[me] if scatter else ins[a], outs[a].at[me], local_sems.at[a])
            own.start()
            owns.append(own)
            for r in range(1, N_DEV):
                fx, fy, fc = (r >> 2) & 1, (r >> 1) & 1, r & 1
                px, py, pc = (x + fx) % 2, (y + fy) % 2, (c + fc) % 2
                peer = 4 * px + 2 * py + pc
                s = a * (N_DEV - 1) + r - 1
                cp = pltpu.make_async_remote_copy(
                    src_ref=ins[a].at[peer] if scatter else ins[a], dst_ref=outs[a].at[me],
                    send_sem=send_sems.at[s], recv_sem=recv_sems.at[s],
                    device_id=(px, py, pc), device_id_type=pl.DeviceIdType.MESH)
                cp.start()
                sends.append(cp)
                recvs.append(pltpu.make_async_remote_copy(
                    src_ref=ins[a].at[peer] if scatter else ins[a], dst_ref=outs[a].at[peer],
                    send_sem=send_sems.at[s], recv_sem=recv_sems.at[s],
                    device_id=(px, py, pc), device_id_type=pl.DeviceIdType.MESH))
        for rc in recvs:
            rc.wait_recv()
        for cp in sends:
            cp.wait_send()
        for own in owns:
            own.wait()

    any_spec = pl.BlockSpec(memory_space=pl.ANY)
    return pl.pallas_call(
        body, name=name,
        in_specs=[any_spec] * k, out_specs=[any_spec] * k,
        out_shape=[jax.ShapeDtypeStruct(a.shape if scatter else (N_DEV,) + a.shape, a.dtype) for a in arrs],
        scratch_shapes=[pltpu.SemaphoreType.DMA((k * (N_DEV - 1),)), pltpu.SemaphoreType.DMA((k * (N_DEV - 1),)),
                        pltpu.SemaphoreType.DMA((k,))],
        compiler_params=pltpu.CompilerParams(has_side_effects=True),
    )(*arrs)


MOD_ROWS = 48


def _mod_tile(cc, w, b):
    s = cc * jax.nn.sigmoid(cc)
    return lax.dot_general(s.astype(BF16), w.astype(BF16), (((1,), (0,)), ((), ())), preferred_element_type=F32) + b


def _mod_fwd(cc, w_mod, b_shard, *, name):
    L, d, wcols = w_mod.shape

    def body(cc_ref, w_ref, b_ref, o_ref):
        o_ref[...] = _mod_tile(cc_ref[...], w_ref[...], b_ref[...])

    return pl.pallas_call(
        body, name=name, grid=(L,),
        in_specs=[pl.BlockSpec((MOD_ROWS, d), lambda l: (0, 0)), pl.BlockSpec((None, d, wcols), lambda l: (l, 0, 0)),
                  pl.BlockSpec((None, 1, wcols), lambda l: (l, 0, 0))],
        out_specs=pl.BlockSpec((None, MOD_ROWS, wcols), lambda l: (l, 0, 0)),
        out_shape=jax.ShapeDtypeStruct((L, MOD_ROWS, wcols), F32),
        compiler_params=_params("parallel"),
    )(cc, w_mod, b_shard)


def _mod_bwd(cc, w_mod, b_shard, dm, *, name):
    L, d, wcols = w_mod.shape

    def body(cc_ref, w_ref, b_ref, dm_ref, dcc_ref, dw_ref):
        _, vjp = jax.vjp(_mod_tile, cc_ref[...], w_ref[...], b_ref[...])
        dcc, dw, _ = vjp(dm_ref[...])
        dw_ref[...] = dw

        @pl.when(pl.program_id(0) == 0)
        def _():
            dcc_ref[...] = dcc

        @pl.when(pl.program_id(0) > 0)
        def _():
            dcc_ref[...] += dcc

    return pl.pallas_call(
        body, name=name, grid=(L,),
        in_specs=[pl.BlockSpec((MOD_ROWS, d), lambda l: (0, 0)), pl.BlockSpec((None, d, wcols), lambda l: (l, 0, 0)),
                  pl.BlockSpec((None, 1, wcols), lambda l: (l, 0, 0)), pl.BlockSpec((None, MOD_ROWS, wcols), lambda l: (l, 0, 0))],
        out_specs=[pl.BlockSpec((MOD_ROWS, d), lambda l: (0, 0)), pl.BlockSpec((None, d, wcols), lambda l: (l, 0, 0))],
        out_shape=[jax.ShapeDtypeStruct((MOD_ROWS, d), F32), jax.ShapeDtypeStruct((L, d, wcols), F32)],
        compiler_params=_params("arbitrary"),
    )(cc, w_mod, b_shard, dm)


def _sum_leading(a, *, name):
    K, R, C = a.shape
    tr = _tile8(R, 256)

    def body(a_ref, o_ref):
        acc = a_ref[0].astype(F32)
        for i in range(1, K):
            acc = acc + a_ref[i].astype(F32)
        o_ref[...] = acc

    return pl.pallas_call(
        body, name=name, grid=(R // tr,),
        in_specs=[pl.BlockSpec((K, tr, C), lambda i: (0, i, 0))],
        out_specs=pl.BlockSpec((tr, C), lambda i: (i, 0)),
        out_shape=jax.ShapeDtypeStruct((R, C), F32),
        compiler_params=_params("parallel"),
    )(a)


def _tile8(dim, target):
    if dim <= target:
        return dim
    t = (target // 8) * 8
    while t >= 8:
        if dim % t == 0:
            return t
        t -= 8
    raise ValueError(f"no row tile for {dim}")


def _adamw_math(g, w, m, v):
    m = ADAM_B1 * m + (1.0 - ADAM_B1) * g
    v = ADAM_B2 * v + (1.0 - ADAM_B2) * jnp.square(g)
    m_hat = m / (1.0 - ADAM_B1 ** ADAM_STEP)
    v_hat = v / (1.0 - ADAM_B2 ** ADAM_STEP)
    delta = -ADAM_LR * (m_hat / (jnp.sqrt(v_hat) + ADAM_EPS) + ADAM_WD * w)
    return delta, m, v


def _adamw(g, w, m, v, *, name):
    R, C = w.shape
    parts = g.ndim == 3
    tr = _tile8(R, 256)

    def body(g_ref, w_ref, m_ref, v_ref, go_ref, d_ref, mo_ref, vo_ref):
        if parts:
            gv = g_ref[0].astype(F32)
            for i in range(1, g_ref.shape[0]):
                gv = gv + g_ref[i].astype(F32)
        else:
            gv = g_ref[...]
        go_ref[...] = gv
        d_ref[...], mo_ref[...], vo_ref[...] = _adamw_math(gv, w_ref[...], m_ref[...], v_ref[...])

    tile = pl.BlockSpec((tr, C), lambda i: (i, 0))
    g_spec = pl.BlockSpec((g.shape[0], tr, C), lambda i: (0, i, 0)) if parts else tile
    return pl.pallas_call(
        body, name=name, grid=(R // tr,),
        in_specs=[g_spec, tile, tile, tile], out_specs=[tile] * 4,
        out_shape=[jax.ShapeDtypeStruct((R, C), F32)] * 4,
        compiler_params=_params("parallel"),
    )(g, w, m, v)


def _block_diag(w):
    g = w.shape[0]
    rows = [jnp.concatenate([w[i] if j == i else jnp.zeros_like(w[i]) for j in range(g)], axis=1) for i in range(g)]
    return jnp.concatenate(rows, axis=0)


def _forward(x, ctx, m_loc, m_ctx, p):
    B = x.shape[0]
    depth = m_loc.shape[0]
    modnorm = lambda tag: _make_rowwise(tag, _modnorm_tile, ("row", "glob", "batch", "batch"), (True,) * 4, (D_MODEL,), 256)
    gated = lambda tag: _make_rowwise(tag, _gated_tile, ("row", "row", "batch"), (True,) * 3, (D_MODEL,), 256)
    swiglu = lambda tag: _make_rowwise(tag, _swiglu_tile, ("row",), (True,), (D_FF,), 128)
    cx = ctx
    for l in range(depth):
        last = l == depth - 1
        sh1, sc1, g1, sh2, sc2, g2 = [t[:, None, :] for t in jnp.split(m_loc[l], 6, axis=-1)]
        cmods = [jnp.broadcast_to(t[None, None, :], (B, 1, D_MODEL)) for t in jnp.split(m_ctx[l], 6)]
        csh1, csc1, cg1, csh2, csc2, cg2 = cmods
        n1 = p["norm1_g"][l][None, :]
        n2 = p["norm2_g"][l][None, :]
        w_in = p["w_in"][l]
        mix_args = (p["attn_sink"][l][None, :], p["conv_dw"][l], p["conv_dw_b"][l][None, :], p["conv_ln_g"][l][None, :],
                    p["conv_ln_b"][l][None, :], _block_diag(p["pool_w"][l]), p["pool_scale"][l][None, :])

        hl, = modnorm(f"l{l}_norm1")(x, n1, sc1, sh1)
        hc, = modnorm(f"l{l}_cnorm1")(cx, n1, csc1, csh1)
        u = _make_linear(f"l{l}_in")(hl, w_in)
        if last:
            kvc = _make_linear(f"l{l}_cin")(hc, w_in[:, ATTN_W:ATTN_W + 2 * KV_W])
        else:
            uc = _make_linear(f"l{l}_cin")(hc, w_in)
            kvc = uc[:, :, ATTN_W:ATTN_W + 2 * KV_W]
        mix = _make_mixer(f"l{l}_mix", True)(u, kvc, *mix_args)
        y = _make_linear(f"l{l}_out")(mix, p["w_out"][l])
        x, = gated(f"l{l}_res1")(x, y, g1)
        if not last:
            mixc = _make_mixer(f"l{l}_cmix", False)(uc, kvc, *mix_args)
            yc = _make_linear(f"l{l}_cout")(mixc, p["w_out"][l])
            cx, = gated(f"l{l}_cres1")(cx, yc, cg1)

        h2, = modnorm(f"l{l}_norm2")(x, n2, sc2, sh2)
        gu = _make_linear(f"l{l}_ffn_in")(h2, p["w_ffn_in"][l])
        act, = swiglu(f"l{l}_act")(gu)
        y2 = _make_linear(f"l{l}_ffn_out")(act, p["w_ffn_out"][l])
        x, = gated(f"l{l}_res2")(x, y2, g2)
        if not last:
            hc2, = modnorm(f"l{l}_cnorm2")(cx, n2, csc2, csh2)
            guc = _make_linear(f"l{l}_cffn_in")(hc2, p["w_ffn_in"][l])
            actc, = swiglu(f"l{l}_cact")(guc)
            yc2 = _make_linear(f"l{l}_cffn_out")(actc, p["w_ffn_out"][l])
            cx, = gated(f"l{l}_cres2")(cx, yc2, cg2)
    return x


def _local_step(x, ctx, m_loc, m_ctx, p, final_g, target):
    xl, vjp = jax.vjp(_forward, x, ctx, m_loc, m_ctx, p)
    loss_rows, dxl, dfinal = _loss_head(xl, final_g[None, :], target, name="loss_head")
    dx, _, dm_loc, dm_ctx, dp = vjp(dxl)
    return jnp.sum(loss_rows[:, 0, 0]), dx, dm_loc, dm_ctx, dp, dfinal[0]


PACK_COLS = 1024


def _pack(arrs):
    flat = jnp.concatenate([a.reshape(-1).astype(F32) for a in arrs])
    rows = -(-flat.shape[0] // (8 * PACK_COLS)) * 8
    return jnp.pad(flat, (0, rows * PACK_COLS - flat.shape[0])).reshape(rows, PACK_COLS)


def _unpack(slab, like):
    flat = slab.reshape(-1)
    out, off = [], 0
    for a in like:
        out.append(flat[off:off + a.size].reshape(a.shape))
        off += a.size
    return out


def _shard_cols(gathered):
    _, L, R, C = gathered.shape
    return jnp.transpose(gathered, (1, 2, 0, 3)).reshape(L, R, N_DEV * C)


def _shard_rows(gathered):
    _, L, R, C = gathered.shape
    return jnp.transpose(gathered, (1, 0, 2, 3)).reshape(L, N_DEV * R, C)


def _split_cols(full):
    L, R, C8 = full.shape
    return jnp.transpose(full.reshape(L, R, N_DEV, C8 // N_DEV), (2, 0, 1, 3))


def _split_rows(full):
    L, R8, C = full.shape
    return jnp.transpose(full.reshape(L, N_DEV, R8 // N_DEV, C), (1, 0, 2, 3))


def _as_rows(a, leading=0):
    return a.reshape(*a.shape[:leading], -1, PACK_COLS)


SMALL = ("c_ctx", "b_mod", "norm1_g", "norm2_g", "conv_dw_b", "conv_ln_g", "conv_ln_b", "attn_sink", "pool_w",
         "pool_scale", "final_g", "conv_dw")
BIG = ("w_mod", "w_in", "w_out", "w_ffn_in", "w_ffn_out")
ORDER = ("c_ctx", "w_mod", "b_mod", "norm1_g", "norm2_g", "w_in", "conv_dw", "conv_dw_b", "conv_ln_g", "conv_ln_b",
         "attn_sink", "pool_w", "pool_scale", "w_out", "w_ffn_in", "w_ffn_out", "final_g")


def kernel(x, c, ctx, c_ctx, w_mod, b_mod, norm1_g, norm2_g, w_in, conv_dw, conv_dw_b, conv_ln_g, conv_ln_b, attn_sink, pool_w, pool_scale, w_out, w_ffn_in, w_ffn_out, final_g, loss_target, m_c_ctx, m_w_mod, m_b_mod, m_norm1_g, m_norm2_g, m_w_in, m_conv_dw, m_conv_dw_b, m_conv_ln_g, m_conv_ln_b, m_attn_sink, m_pool_w, m_pool_scale, m_w_out, m_w_ffn_in, m_w_ffn_out, m_final_g, v_c_ctx, v_w_mod, v_b_mod, v_norm1_g, v_norm2_g, v_w_in, v_conv_dw, v_conv_dw_b, v_conv_ln_g, v_conv_ln_b, v_attn_sink, v_pool_w, v_pool_scale, v_w_out, v_w_ffn_in, v_w_ffn_out, v_final_g):
    w = dict(c_ctx=c_ctx, w_mod=w_mod, b_mod=b_mod, norm1_g=norm1_g, norm2_g=norm2_g, w_in=w_in, conv_dw=conv_dw,
             conv_dw_b=conv_dw_b, conv_ln_g=conv_ln_g, conv_ln_b=conv_ln_b, attn_sink=attn_sink, pool_w=pool_w,
             pool_scale=pool_scale, w_out=w_out, w_ffn_in=w_ffn_in, w_ffn_out=w_ffn_out, final_g=final_g)
    mom = dict(c_ctx=m_c_ctx, w_mod=m_w_mod, b_mod=m_b_mod, norm1_g=m_norm1_g, norm2_g=m_norm2_g, w_in=m_w_in,
               conv_dw=m_conv_dw, conv_dw_b=m_conv_dw_b, conv_ln_g=m_conv_ln_g, conv_ln_b=m_conv_ln_b,
               attn_sink=m_attn_sink, pool_w=m_pool_w, pool_scale=m_pool_scale, w_out=m_w_out, w_ffn_in=m_w_ffn_in,
               w_ffn_out=m_w_ffn_out, final_g=m_final_g)
    var = dict(c_ctx=v_c_ctx, w_mod=v_w_mod, b_mod=v_b_mod, norm1_g=v_norm1_g, norm2_g=v_norm2_g, w_in=v_w_in,
               conv_dw=v_conv_dw, conv_dw_b=v_conv_dw_b, conv_ln_g=v_conv_ln_g, conv_ln_b=v_conv_ln_b,
               attn_sink=v_attn_sink, pool_w=v_pool_w, pool_scale=v_pool_scale, w_out=v_w_out, w_ffn_in=v_w_ffn_in,
               w_ffn_out=v_w_ffn_out, final_g=v_final_g)
    B = x.shape[0]
    depth = w_mod.shape[0]
    mod_cols = w_mod.shape[2]
    dw_cols = conv_dw.shape[2]
    me = 4 * lax.axis_index("x") + 2 * lax.axis_index("y") + lax.axis_index("c")

    c_all, dw_all, g_in, g_out, g_fin, g_fout = _exchange(
        [c, conv_dw, w_in.astype(BF16), w_out.astype(BF16), w_ffn_in.astype(BF16), w_ffn_out.astype(BF16)],
        scatter=False, name="gather_weights")
    p = dict(norm1_g=norm1_g, norm2_g=norm2_g, conv_dw=_shard_cols(dw_all), conv_dw_b=conv_dw_b, conv_ln_g=conv_ln_g,
             conv_ln_b=conv_ln_b, attn_sink=attn_sink, pool_w=pool_w, pool_scale=pool_scale,
             w_in=_shard_cols(g_in), w_out=_shard_rows(g_out), w_ffn_in=_shard_cols(g_fin), w_ffn_out=_shard_rows(g_fout))

    cc = jnp.concatenate([c_all.reshape(N_DEV * B, D_MODEL), jnp.broadcast_to(c_ctx[None, :], (N_DEV, D_MODEL)),
                          jnp.zeros((MOD_ROWS - N_DEV * B - N_DEV, D_MODEL), F32)], axis=0)
    b_shard = lax.dynamic_slice_in_dim(b_mod, me * mod_cols, mod_cols, axis=1)[:, None, :]
    m_part = _mod_fwd(cc, w_mod, b_shard, name="mod_fwd")
    m_all, = _exchange([m_part], scatter=False, name="gather_mod")
    m_full = _shard_cols(m_all)
    m_loc = lax.dynamic_slice_in_dim(m_full, me * B, B, axis=1)
    m_ctx = m_full[:, N_DEV * B, :]

    loss_part, dx, dm_loc, dm_ctx, dp, dfinal = _local_step(x, ctx, m_loc, m_ctx, p, final_g, loss_target)
    loss = lax.psum(loss_part, AXES)

    dm_rows = jnp.concatenate([dm_loc, dm_ctx[:, None, :], jnp.zeros((depth, 8 - B - 1, 6 * D_MODEL), F32)], axis=1)
    dm_all, = _exchange([dm_rows], scatter=False, name="gather_dmod")
    dm_full = jnp.concatenate([
        jnp.transpose(dm_all[:, :, :B, :], (1, 0, 2, 3)).reshape(depth, N_DEV * B, 6 * D_MODEL),
        jnp.transpose(dm_all[:, :, B, :], (1, 0, 2)),
        jnp.zeros((depth, MOD_ROWS - N_DEV * B - N_DEV, 6 * D_MODEL), F32)], axis=1)
    g_b_mod = jnp.stack([_sum_leading(dm_full[l][:, None, :], name=f"b_mod_grad{l}")[0] for l in range(depth)])
    dm_mine = lax.dynamic_slice_in_dim(dm_full, me * mod_cols, mod_cols, axis=2)
    dcc, g_w_mod = _mod_bwd(cc, w_mod, b_shard, dm_mine, name="mod_bwd")
    g_c_ctx_part = jnp.sum(dcc[N_DEV * B:N_DEV * B + N_DEV], axis=0)

    small_like = [c_ctx, norm1_g, norm2_g, conv_dw_b, conv_ln_g, conv_ln_b, attn_sink, pool_w, pool_scale, final_g,
                  dp["conv_dw"]]
    small_part = _pack([g_c_ctx_part, dp["norm1_g"], dp["norm2_g"], dp["conv_dw_b"], dp["conv_ln_g"], dp["conv_ln_b"],
                        dp["attn_sink"], dp["pool_w"], dp["pool_scale"], dfinal, dp["conv_dw"]])
    small_all, = _exchange([small_part], scatter=False, name="gather_small")
    small_sum = _unpack(_sum_leading(small_all, name="sum_small"), small_like)
    g = dict(zip(("c_ctx", "norm1_g", "norm2_g", "conv_dw_b", "conv_ln_g", "conv_ln_b", "attn_sink", "pool_w",
                  "pool_scale", "final_g"), small_sum[:-1]))
    g["b_mod"] = g_b_mod
    g["conv_dw"] = lax.dynamic_slice_in_dim(small_sum[-1], me * dw_cols, dw_cols, axis=2)

    parts = _exchange([_split_cols(dp["w_in"]), _split_rows(dp["w_out"]), _split_cols(dp["w_ffn_in"]),
                       _split_rows(dp["w_ffn_out"])], scatter=True, name="scatter_grads")

    delta, new_m, new_v = {}, {}, {}
    for name, gpart in zip(("w_in", "w_out", "w_ffn_in", "w_ffn_out"), parts):
        shp = w[name].shape
        res = _adamw(_as_rows(gpart, 1), _as_rows(w[name]), _as_rows(mom[name]), _as_rows(var[name]), name=f"adamw_{name}")
        g[name], delta[name], new_m[name], new_v[name] = [r.reshape(shp) for r in res]
    res = _adamw(_as_rows(g_w_mod), _as_rows(w_mod), _as_rows(m_w_mod), _as_rows(v_w_mod), name="adamw_w_mod")
    g["w_mod"], delta["w_mod"], new_m["w_mod"], new_v["w_mod"] = [r.reshape(w_mod.shape) for r in res]
    res = _adamw(_pack([g[k] for k in SMALL]), _pack([w[k] for k in SMALL]), _pack([mom[k] for k in SMALL]),
                 _pack([var[k] for k in SMALL]), name="adamw_small")
    like = [w[k] for k in SMALL]
    for dst, slab in zip((delta, new_m, new_v), res[1:]):
        dst.update(zip(SMALL, _unpack(slab, like)))

    return (loss, dx, *[g[k] for k in ORDER], *[delta[k] for k in ORDER], *[new_m[k] for k in ORDER],
            *[new_v[k] for k in ORDER])
```

```python
import functools

import numpy as np
import jax
import jax.numpy as jnp
from jax import lax
from jax.experimental import pallas as pl
from jax.experimental.pallas import tpu as pltpu

F32 = jnp.float32
BF16 = jnp.bfloat16

D_MODEL = 1024
GRID_W = 64
HEAD_DIM = 64
ATTN_W = 512
CONV_W = 256
POOL_W = 256
ATTN_HEADS = 8
KV_HEADS = 2
GROUP = ATTN_HEADS // KV_HEADS
KV_W = KV_HEADS * HEAD_DIM
IN_W = ATTN_W + 2 * KV_W + 2 * CONV_W + POOL_W
WINDOW = 128
Q_BLOCK = 128
SPAN = Q_BLOCK + 2 * WINDOW
CONV_KERNEL = 31
POOL_WINDOWS = (2, 4, 8, 16)
POOL_GROUP = 64
ROPE_BASE = 10000.0
D_FF = 2816
EPS = 1e-6
NEG = -1e30
N_DEV = 8
AXES = ("x", "y", "c")

ADAM_LR = 0.001
ADAM_B1 = 0.9
ADAM_B2 = 0.999
ADAM_EPS = 1e-08
ADAM_WD = 0.01
ADAM_STEP = 10

VMEM_LIMIT = 56 * 1024 * 1024
HALO = 16
SEQ_CHUNK = 256


def _params(*sem):
    return pltpu.CompilerParams(dimension_semantics=sem, vmem_limit_bytes=VMEM_LIMIT)


def _tile(dim, target):
    if dim <= target:
        return dim
    t = (target // 128) * 128
    while t >= 128:
        if dim % t == 0:
            return t
        t -= 128
    raise ValueError(f"no tile for {dim}")


MM_K_WHOLE = 2816


def _dot(a, b):
    return lax.dot_general(a.astype(BF16), b.astype(BF16), (((1,), (0,)), ((), ())), preferred_element_type=F32)


def _mm(a, b, *, name, out_dtype=F32, tm=1408, tn=512, tk=2048):
    M, K = a.shape
    K2, N = b.shape
    assert K == K2, (a.shape, b.shape)
    tm = _tile(M, tm)
    tn = _tile(N, tn)
    tk = K if K <= MM_K_WHOLE else _tile(K, tk)
    nk = K // tk

    def body(a_ref, b_ref, o_ref, *scratch):
        part = _dot(a_ref[...], b_ref[...])
        if nk == 1:
            o_ref[...] = part.astype(o_ref.dtype)
        else:
            acc_ref, = scratch
            k = pl.program_id(2)

            @pl.when(k == 0)
            def _():
                acc_ref[...] = part

            @pl.when(k > 0)
            def _():
                acc_ref[...] += part

            @pl.when(k == nk - 1)
            def _():
                o_ref[...] = acc_ref[...].astype(o_ref.dtype)

    return pl.pallas_call(
        body, name=name, grid=(M // tm, N // tn, nk),
        in_specs=[pl.BlockSpec((tm, tk), lambda i, j, k: (i, k)), pl.BlockSpec((tk, tn), lambda i, j, k: (k, j))],
        out_specs=pl.BlockSpec((tm, tn), lambda i, j, k: (i, j)),
        out_shape=jax.ShapeDtypeStruct((M, N), out_dtype),
        scratch_shapes=[pltpu.VMEM((tm, tn), F32)] if nk > 1 else [],
        compiler_params=_params("parallel", "parallel", "arbitrary"),
    )(a, b)


FF_TILE = 256


def _interleave_ffn(w):
    lead = w.shape[:-1]
    t = w.reshape(*lead, 2, D_FF // FF_TILE, FF_TILE)
    return jnp.swapaxes(t, -3, -2).reshape(*lead, 2 * D_FF)


def _deinterleave_ffn(w):
    lead = w.shape[:-1]
    t = w.reshape(*lead, D_FF // FF_TILE, 2, FF_TILE)
    return jnp.swapaxes(t, -3, -2).reshape(*lead, 2 * D_FF)


def _swiglu(gu):
    g, u = gu[:, :FF_TILE], gu[:, FF_TILE:]
    return g * jax.nn.sigmoid(g) * u


def _mm_swiglu(a, w_il, *, name, tm=1024):
    M, K = a.shape
    tm = _tile(M, tm)

    def body(a_ref, b_ref, gu_ref, act_ref):
        gu = _dot(a_ref[...], b_ref[...])
        gu_ref[...] = gu
        act_ref[...] = _swiglu(gu).astype(act_ref.dtype)

    return pl.pallas_call(
        body, name=name, grid=(M // tm, D_FF // FF_TILE),
        in_specs=[pl.BlockSpec((tm, K), lambda i, j: (i, 0)), pl.BlockSpec((K, 2 * FF_TILE), lambda i, j: (0, j))],
        out_specs=[pl.BlockSpec((tm, 2 * FF_TILE), lambda i, j: (i, j)), pl.BlockSpec((tm, FF_TILE), lambda i, j: (i, j))],
        out_shape=[jax.ShapeDtypeStruct((M, 2 * D_FF), F32), jax.ShapeDtypeStruct((M, D_FF), BF16)],
        compiler_params=_params("parallel", "parallel"),
    )(a, w_il)


def _mm_dswiglu(dy, w_out_t, gu, *, name, tm=1024):
    M, K = dy.shape
    tm = _tile(M, tm)

    def body(dy_ref, b_ref, gu_ref, o_ref):
        dact = _dot(dy_ref[...], b_ref[...])
        _, vjp = jax.vjp(_swiglu, gu_ref[...])
        o_ref[...] = vjp(dact)[0].astype(o_ref.dtype)

    return pl.pallas_call(
        body, name=name, grid=(M // tm, D_FF // FF_TILE),
        in_specs=[pl.BlockSpec((tm, K), lambda i, j: (i, 0)), pl.BlockSpec((K, FF_TILE), lambda i, j: (0, j)),
                  pl.BlockSpec((tm, 2 * FF_TILE), lambda i, j: (i, j))],
        out_specs=pl.BlockSpec((tm, 2 * FF_TILE), lambda i, j: (i, j)),
        out_shape=jax.ShapeDtypeStruct((M, 2 * D_FF), BF16),
        compiler_params=_params("parallel", "parallel"),
    )(dy, w_out_t, gu)


def _rope_tables(n):
    rows = n // GRID_W
    row = jnp.repeat(jnp.arange(rows), GRID_W).astype(F32)
    col = jnp.tile(jnp.arange(GRID_W), rows).astype(F32)
    half = HEAD_DIM // 2
    inv = ROPE_BASE ** (-jnp.arange(0, half, 2, dtype=F32) / half)
    ar = row[:, None] * inv
    ac = col[:, None] * inv
    ang = jnp.concatenate([ar, ar, ac, ac], axis=-1)
    return jnp.cos(ang), jnp.sin(ang)


def _rot_half(x):
    w = x.shape[-1]
    lane = lax.broadcasted_iota(jnp.int32, x.shape, 1)
    up = pltpu.roll(x, w - 16, 1)
    down = pltpu.roll(x, 16, 1)
    return jnp.where((lane & 16) == 0, -up, down)


def _rope(x, cos, sin):
    return x * cos + _rot_half(x) * sin


def _rope_bwd(d, cos, sin):
    return d * cos - _rot_half(d * sin)


def _attn_core(q_st, k_all, v_all, sink_rows, valid):
    s = lax.dot_general(q_st.astype(BF16), k_all.astype(BF16), (((1,), (1,)), ((), ())),
                        preferred_element_type=F32) * (HEAD_DIM ** -0.5)
    if valid is not None:
        s = jnp.where(valid, s, NEG)
    mx = jnp.maximum(jnp.max(s, axis=1, keepdims=True), sink_rows)
    e = jnp.exp(s - mx)
    den = jnp.sum(e, axis=1, keepdims=True) + jnp.exp(sink_rows - mx)
    p = e / den
    return lax.dot_general(p.astype(BF16), v_all.astype(BF16), (((1,), (0,)), ((), ())), preferred_element_type=F32)


def _stack_heads(x, kh):
    return jnp.concatenate([x[:, (GROUP * kh + g) * HEAD_DIM:(GROUP * kh + g + 1) * HEAD_DIM] for g in range(GROUP)], axis=0)


def _sink_rows(sink, kh):
    return jnp.concatenate([jnp.broadcast_to(sink[:, GROUP * kh + g:GROUP * kh + g + 1], (Q_BLOCK, 1)) for g in range(GROUP)], axis=0)


def _valid_mask(start, s0, n_ctx):
    rows = GROUP * Q_BLOCK
    cols = n_ctx + SPAN
    r = lax.broadcasted_iota(jnp.int32, (rows, cols), 0)
    c = lax.broadcasted_iota(jnp.int32, (rows, cols), 1)
    qpos = start + (r & (Q_BLOCK - 1))
    kpos = s0 + c - n_ctx
    return (c < n_ctx) | (jnp.abs(qpos - kpos) <= WINDOW)


def _span_start(i, n):
    start = i * Q_BLOCK
    s0 = jnp.clip(start - WINDOW, 0, n - SPAN)
    return start, pl.multiple_of(s0, Q_BLOCK)


def _riding(body, exch, n_in, n_out, grid):
    if exch is None:
        return body, [], [], [], []
    arrs, scatter = exch
    k = len(arrs)

    def wrapped(*refs):
        ins, xin = refs[:n_in], refs[n_in:n_in + k]
        outs, xout = refs[n_in + k:n_in + k + n_out], refs[n_in + k + n_out:n_in + 2 * k + n_out]
        sems = refs[n_in + 2 * k + n_out:]
        b, i = pl.program_id(0), pl.program_id(1)

        @pl.when((b == 0) & (i == 0))
        def _():
            _exch_start(xin, xout, sems, scatter)

        body(*ins, *outs)

        @pl.when((b == grid[0] - 1) & (i == grid[1] - 1))
        def _():
            _exch_wait(xin, xout, sems, scatter)

    any_spec = pl.BlockSpec(memory_space=pl.ANY)
    return wrapped, [any_spec] * k, [any_spec] * k, _exch_out_shapes(arrs, scatter), _exch_sems(k)


def _attn_fwd(u, kvc, sink, cos, sin, *, local, name, exch=None):
    B, n, _ = u.shape
    n_ctx = kvc.shape[1]
    nb = n // Q_BLOCK
    assert (not local) or n >= SPAN

    def body(q_ref, k_ref, v_ref, kc_ref, vc_ref, sink_ref, cq_ref, sq_ref, ck_ref, sk_ref, o_ref):
        i = pl.program_id(1)
        q = q_ref[...]
        sink_v = sink_ref[...]
        kc = kc_ref[...]
        vc = vc_ref[...]
        if local:
            start, s0 = _span_start(i, n)
            q = _rope(q, cq_ref[...], sq_ref[...])
            ks = _rope(k_ref[pl.ds(s0, SPAN), :], ck_ref[pl.ds(s0, SPAN), :], sk_ref[pl.ds(s0, SPAN), :])
            vs = v_ref[pl.ds(s0, SPAN), :]
            k_cat = jnp.concatenate([kc, ks], axis=0)
            v_cat = jnp.concatenate([vc, vs], axis=0)
            valid = _valid_mask(start, s0, n_ctx)
        else:
            k_cat, v_cat, valid = kc, vc, None
        for kh in range(KV_HEADS):
            sl = slice(kh * HEAD_DIM, (kh + 1) * HEAD_DIM)
            o = _attn_core(_stack_heads(q, kh), k_cat[:, sl], v_cat[:, sl], _sink_rows(sink_v, kh), valid)
            for g in range(GROUP):
                h = GROUP * kh + g
                o_ref[:, h * HEAD_DIM:(h + 1) * HEAD_DIM] = o[g * Q_BLOCK:(g + 1) * Q_BLOCK, :]

    seq = lambda blk: pl.BlockSpec((None, n, KV_W), lambda b, i: (b, 0, blk))
    ctxs = lambda blk: pl.BlockSpec((None, n_ctx, KV_W), lambda b, i: (b, 0, blk))
    full = lambda a: pl.BlockSpec(a.shape, lambda b, i: (0,) * a.ndim)
    cos_q, sin_q = jnp.tile(cos, (1, ATTN_HEADS)), jnp.tile(sin, (1, ATTN_HEADS))
    cos_k, sin_k = jnp.tile(cos, (1, KV_HEADS)), jnp.tile(sin, (1, KV_HEADS))
    body, x_in, x_out, x_shapes, x_sems = _riding(body, exch, 10, 1, (B, nb))
    return pl.pallas_call(
        body, name=name, grid=(B, nb),
        in_specs=[pl.BlockSpec((None, Q_BLOCK, ATTN_W), lambda b, i: (b, i, 0)),
                  seq(ATTN_W // KV_W), seq(ATTN_W // KV_W + 1), ctxs(0), ctxs(1), full(sink),
                  pl.BlockSpec((Q_BLOCK, ATTN_W), lambda b, i: (i, 0)), pl.BlockSpec((Q_BLOCK, ATTN_W), lambda b, i: (i, 0)),
                  full(cos_k), full(sin_k)] + x_in,
        out_specs=[pl.BlockSpec((None, Q_BLOCK, ATTN_W), lambda b, i: (b, i, 0))] + x_out,
        out_shape=[jax.ShapeDtypeStruct((B, n, ATTN_W), F32)] + x_shapes,
        scratch_shapes=x_sems,
        compiler_params=_params("arbitrary", "arbitrary"),
    )(u, u, u, kvc, kvc, sink, cos_q, sin_q, cos_k, sin_k, *(exch[0] if exch else []))


def _attn_bwd(u, kvc, sink, cos, sin, do_src, do_blk, *, local, name, exch=None):
    B, n, _ = u.shape
    n_ctx = kvc.shape[1]
    nb = n // Q_BLOCK

    def body(q_ref, k_ref, v_ref, kc_ref, vc_ref, sink_ref, cq_ref, sq_ref, ck_ref, sk_ref, do_ref,
             dq_ref, dk_ref, dv_ref, dkc_ref, dvc_ref, dsink_ref):
        b = pl.program_id(0)
        i = pl.program_id(1)

        @pl.when(i == 0)
        def _():
            dk_ref[...] = jnp.zeros_like(dk_ref)
            dv_ref[...] = jnp.zeros_like(dv_ref)
            dkc_ref[...] = jnp.zeros_like(dkc_ref)
            dvc_ref[...] = jnp.zeros_like(dvc_ref)

        @pl.when((i == 0) & (b == 0))
        def _():
            dsink_ref[...] = jnp.zeros_like(dsink_ref)

        q = q_ref[...]
        do = do_ref[...]
        sink_v = sink_ref[...]
        kc = kc_ref[...]
        vc = vc_ref[...]
        if local:
            start, s0 = _span_start(i, n)
            ck = ck_ref[pl.ds(s0, SPAN), :]
            sk = sk_ref[pl.ds(s0, SPAN), :]
            q = _rope(q, cq_ref[...], sq_ref[...])
            ks = _rope(k_ref[pl.ds(s0, SPAN), :], ck, sk)
            vs = v_ref[pl.ds(s0, SPAN), :]
            k_cat = jnp.concatenate([kc, ks], axis=0)
            v_cat = jnp.concatenate([vc, vs], axis=0)
            valid = _valid_mask(start, s0, n_ctx)
        else:
            k_cat, v_cat, valid = kc, vc, None
        dsink = jnp.zeros((1, ATTN_HEADS), F32)
        lane8 = lax.broadcasted_iota(jnp.int32, (1, ATTN_HEADS), 1)
        dk_parts, dv_parts = [], []
        for kh in range(KV_HEADS):
            sl = slice(kh * HEAD_DIM, (kh + 1) * HEAD_DIM)
            core = functools.partial(_attn_core, valid=valid)
            _, vjp = jax.vjp(core, _stack_heads(q, kh), k_cat[:, sl], v_cat[:, sl], _sink_rows(sink_v, kh))
            dq_st, dk_all, dv_all, dsr = vjp(_stack_heads(do, kh))
            for g in range(GROUP):
                h = GROUP * kh + g
                dq_ref[:, h * HEAD_DIM:(h + 1) * HEAD_DIM] = dq_st[g * Q_BLOCK:(g + 1) * Q_BLOCK, :]
                dsink = dsink + jnp.where(lane8 == h, jnp.sum(dsr[g * Q_BLOCK:(g + 1) * Q_BLOCK, :]), 0.0)
            dk_parts.append(dk_all)
            dv_parts.append(dv_all)
        dk_cat = jnp.concatenate(dk_parts, axis=1)
        dv_cat = jnp.concatenate(dv_parts, axis=1)
        dsink_ref[...] += dsink
        dkc_ref[...] += dk_cat[:n_ctx, :]
        dvc_ref[...] += dv_cat[:n_ctx, :]
        if local:
            dq_ref[...] = _rope_bwd(dq_ref[...], cq_ref[...], sq_ref[...])
            dk_ref[pl.ds(s0, SPAN), :] += _rope_bwd(dk_cat[n_ctx:, :], ck, sk)
            dv_ref[pl.ds(s0, SPAN), :] += dv_cat[n_ctx:, :]

    seq = lambda blk: pl.BlockSpec((None, n, KV_W), lambda b, i: (b, 0, blk))
    ctxs = lambda blk: pl.BlockSpec((None, n_ctx, KV_W), lambda b, i: (b, 0, blk))
    full = lambda a: pl.BlockSpec(a.shape, lambda b, i: (0,) * a.ndim)
    qblk = lambda blk: pl.BlockSpec((None, Q_BLOCK, ATTN_W), lambda b, i: (b, i, blk))
    cos_q, sin_q = jnp.tile(cos, (1, ATTN_HEADS)), jnp.tile(sin, (1, ATTN_HEADS))
    cos_k, sin_k = jnp.tile(cos, (1, KV_HEADS)), jnp.tile(sin, (1, KV_HEADS))
    acc = lambda rows: pl.BlockSpec((None, rows, KV_W), lambda b, i: (b, 0, 0))
    body, x_in, x_out, x_shapes, x_sems = _riding(body, exch, 11, 6, (B, nb))
    return pl.pallas_call(
        body, name=name, grid=(B, nb),
        in_specs=[qblk(0), seq(ATTN_W // KV_W), seq(ATTN_W // KV_W + 1), ctxs(0), ctxs(1), full(sink),
                  pl.BlockSpec((Q_BLOCK, ATTN_W), lambda b, i: (i, 0)), pl.BlockSpec((Q_BLOCK, ATTN_W), lambda b, i: (i, 0)),
                  full(cos_k), full(sin_k), qblk(do_blk)] + x_in,
        out_specs=[qblk(0), acc(n), acc(n), acc(n_ctx), acc(n_ctx), pl.BlockSpec((1, ATTN_HEADS), lambda b, i: (0, 0))] + x_out,
        out_shape=[jax.ShapeDtypeStruct((B, n, ATTN_W), F32), jax.ShapeDtypeStruct((B, n, KV_W), F32),
                   jax.ShapeDtypeStruct((B, n, KV_W), F32), jax.ShapeDtypeStruct((B, n_ctx, KV_W), F32),
                   jax.ShapeDtypeStruct((B, n_ctx, KV_W), F32), jax.ShapeDtypeStruct((1, ATTN_HEADS), F32)] + x_shapes,
        scratch_shapes=x_sems,
        compiler_params=_params("arbitrary", "arbitrary"),
    )(u, u, u, kvc, kvc, sink, cos_q, sin_q, cos_k, sin_k, do_src, *(exch[0] if exch else []))


def _conv_chunk(s, n, a_ext, g_ext, dw, dw_b, ln_g, ln_b):
    del s, n
    r = a_ext.shape[0] - 2 * HALO
    h = a_ext * jax.nn.sigmoid(g_ext)
    acc = jnp.broadcast_to(dw_b, (r, CONV_W))
    first = HALO - CONV_KERNEL // 2
    for k in range(CONV_KERNEL):
        acc = acc + h[first + k:first + k + r, :] * dw[k:k + 1, :]
    mu = jnp.mean(acc, axis=-1, keepdims=True)
    var = jnp.mean(jnp.square(acc - mu), axis=-1, keepdims=True)
    hn = (acc - mu) * lax.rsqrt(var + EPS) * ln_g + ln_b
    return hn * jax.nn.sigmoid(hn)


def _pool_chunk(s, n, p_ext, w_bd, scale):
    r = p_ext.shape[0] - 2 * HALO
    lane = lax.broadcasted_iota(jnp.int32, (1, POOL_W), 1)
    win = jnp.left_shift(2, lane // POOL_GROUP)
    half = win // 2
    acc = jnp.zeros((r, POOL_W), F32)
    for d in range(-(POOL_WINDOWS[-1] // 2), POOL_WINDOWS[-1] - POOL_WINDOWS[-1] // 2):
        inside = (d >= -half) & (d <= win - 1 - half)
        acc = acc + jnp.where(inside, p_ext[HALO + d:HALO + d + r, :], 0.0)
    t = s + lax.broadcasted_iota(jnp.int32, (r, 1), 0)
    lo = jnp.maximum(t - half, 0)
    hi = jnp.minimum(t + win - 1 - half, n - 1)
    y = acc / (hi - lo + 1).astype(F32) - p_ext[HALO:HALO + r, :]
    out = lax.dot_general(y.astype(BF16), w_bd.astype(BF16), (((1,), (0,)), ((), ())), preferred_element_type=F32)
    return out * scale


def _seq_specs(rows, params):
    specs = [pl.BlockSpec((None, a.shape[1], w), functools.partial(lambda b, blk: (b, 0, blk), blk=blk)) for a, w, blk in rows]
    specs += [pl.BlockSpec(p.shape, functools.partial(lambda b, nd: (0,) * nd, nd=p.ndim)) for p in params]
    return specs


def _fill_padded(pad_ref, row_ref, n):
    w = pad_ref.shape[1]
    pad_ref[pl.ds(0, HALO), :] = jnp.zeros((HALO, w), F32)
    pad_ref[pl.ds(HALO + n, HALO), :] = jnp.zeros((HALO, w), F32)
    pad_ref[pl.ds(HALO, n), :] = row_ref[...]


def _seq_fwd(fn, rows, params, out_w, *, name):
    B, n = rows[0][0].shape[:2]
    r = min(SEQ_CHUNK, n)
    nr, npar = len(rows), len(params)

    def body(*refs):
        row_refs, par_refs, o_ref, pads = refs[:nr], refs[nr:nr + npar], refs[nr + npar], refs[nr + npar + 1:]
        for rr, p in zip(row_refs, pads):
            _fill_padded(p, rr, n)
        pars = [p[...] for p in par_refs]

        def chunk(ci, carry):
            s = pl.multiple_of(ci * r, r)
            ext = [p[pl.ds(s, r + 2 * HALO), :] for p in pads]
            o_ref[pl.ds(s, r), :] = fn(s, n, *ext, *pars)
            return carry

        lax.fori_loop(0, n // r, chunk, 0)

    return pl.pallas_call(
        body, name=name, grid=(B,),
        in_specs=_seq_specs(rows, params),
        out_specs=pl.BlockSpec((None, n, out_w), lambda b: (b, 0, 0)),
        out_shape=jax.ShapeDtypeStruct((B, n, out_w), F32),
        scratch_shapes=[pltpu.VMEM((n + 2 * HALO, w), F32) for _, w, _ in rows],
        compiler_params=_params("parallel"),
    )(*[a for a, _, _ in rows], *params)


def _seq_bwd(fn, rows, params, dout, *, name):
    B, n = rows[0][0].shape[:2]
    r = min(SEQ_CHUNK, n)
    nr, npar = len(rows), len(params)

    def body(*refs):
        row_refs, par_refs, do_ref = refs[:nr], refs[nr:nr + npar], refs[nr + npar]
        outs = refs[nr + npar + 1:]
        drow_refs, dpar_refs = outs[:nr], outs[nr:nr + npar]
        scratch = outs[nr + npar:]
        pads, dpads = scratch[:nr], scratch[nr:]
        for rr, p, dp in zip(row_refs, pads, dpads):
            _fill_padded(p, rr, n)
            dp[...] = jnp.zeros_like(dp)

        @pl.when(pl.program_id(0) == 0)
        def _():
            for d in dpar_refs:
                d[...] = jnp.zeros_like(d)

        pars = [p[...] for p in par_refs]

        def chunk(ci, carry):
            s = pl.multiple_of(ci * r, r)
            ext = [p[pl.ds(s, r + 2 * HALO), :] for p in pads]
            _, vjp = jax.vjp(functools.partial(fn, s, n), *ext, *pars)
            grads = vjp(do_ref[pl.ds(s, r), :])
            for dp, g in zip(dpads, grads[:nr]):
                dp[pl.ds(s, r + 2 * HALO), :] += g
            for d, g in zip(dpar_refs, grads[nr:]):
                d[...] += g
            return carry

        lax.fori_loop(0, n // r, chunk, 0)
        for d, dp in zip(drow_refs, dpads):
            d[...] = dp[pl.ds(HALO, n), :]

    da, dw_, dblk = dout
    return pl.pallas_call(
        body, name=name, grid=(B,),
        in_specs=_seq_specs(rows, params) + [pl.BlockSpec((None, n, dw_), lambda b: (b, 0, dblk))],
        out_specs=[pl.BlockSpec((None, n, w), lambda b: (b, 0, 0)) for _, w, _ in rows]
        + [pl.BlockSpec(p.shape, functools.partial(lambda b, nd: (0,) * nd, nd=p.ndim)) for p in params],
        out_shape=[jax.ShapeDtypeStruct((B, n, w), F32) for _, w, _ in rows]
        + [jax.ShapeDtypeStruct(p.shape, F32) for p in params],
        scratch_shapes=[pltpu.VMEM((n + 2 * HALO, w), F32) for _, w, _ in rows] * 2,
        compiler_params=_params("arbitrary"),
    )(*[a for a, _, _ in rows], *params, da)


_CONV_A_BLK = (ATTN_W + 2 * KV_W) // CONV_W
_CONV_G_BLK = _CONV_A_BLK + 1
_POOL_BLK = _CONV_A_BLK + 2


def _mixer_fwd(tag, u, kvc, margs, local, exch=None):
    sink, dw, dw_b, ln_g, ln_b, w_bd, scale = margs
    cos, sin = _rope_tables(max(u.shape[1], GRID_W))
    attn, *got = _attn_fwd(u, kvc, sink, cos, sin, local=local, name=f"{tag}_attn_fwd", exch=exch)
    conv = _seq_fwd(_conv_chunk, [(u, CONV_W, _CONV_A_BLK), (u, CONV_W, _CONV_G_BLK)], [dw, dw_b, ln_g, ln_b], CONV_W,
                    name=f"{tag}_conv_fwd")
    pool = _seq_fwd(_pool_chunk, [(u, POOL_W, _POOL_BLK)], [w_bd, scale], POOL_W, name=f"{tag}_pool_fwd")
    return jnp.concatenate([attn, conv, pool], axis=-1).astype(BF16), got


def _mixer_bwd(tag, u, kvc, margs, dmix, local, exch=None):
    sink, dw, dw_b, ln_g, ln_b, w_bd, scale = margs
    cos, sin = _rope_tables(max(u.shape[1], GRID_W))
    dq, dk, dv, dkc, dvc, dsink, *got = _attn_bwd(u, kvc, sink, cos, sin, dmix, 0, local=local, name=f"{tag}_attn_bwd",
                                                  exch=exch)
    da, dg, ddw, ddw_b, dln_g, dln_b = _seq_bwd(
        _conv_chunk, [(u, CONV_W, _CONV_A_BLK), (u, CONV_W, _CONV_G_BLK)], [dw, dw_b, ln_g, ln_b],
        (dmix, CONV_W, ATTN_W // CONV_W), name=f"{tag}_conv_bwd")
    dpu, dw_bd, dscale = _seq_bwd(_pool_chunk, [(u, POOL_W, _POOL_BLK)], [w_bd, scale],
                                  (dmix, POOL_W, (ATTN_W + CONV_W) // POOL_W), name=f"{tag}_pool_bwd")
    return (dq, dk, dv, da, dg, dpu), (dkc, dvc), (dsink, ddw, ddw_b, dln_g, dln_b, dw_bd, dscale), got


def _row_specs(arrs, kinds, tr):
    specs = []
    for a, kind in zip(arrs, kinds):
        if kind == "row":
            specs.append(pl.BlockSpec((None, tr, a.shape[2]), lambda b, j: (b, j, 0)))
        elif kind == "batch":
            specs.append(pl.BlockSpec((None, 1, a.shape[2]), lambda b, j: (b, 0, 0)))
        else:
            specs.append(pl.BlockSpec(a.shape, functools.partial(lambda b, j, nd: (0,) * nd, nd=a.ndim)))
    return specs


def _rowwise_fwd(fn, ins, kinds, outs, tr, *, name):
    B, n = ins[0].shape[:2]
    ni = len(ins)

    def body(*refs):
        res = fn(*[r[...] for r in refs[:ni]])
        for o, v in zip(refs[ni:], res):
            o[...] = v.astype(o.dtype)

    return pl.pallas_call(
        body, name=name, grid=(B, n // tr),
        in_specs=_row_specs(ins, kinds, tr),
        out_specs=[pl.BlockSpec((None, tr, w), lambda b, j: (b, j, 0)) for w, _ in outs],
        out_shape=[jax.ShapeDtypeStruct((B, n, w), dt) for w, dt in outs],
        compiler_params=_params("parallel", "parallel"),
    )(*ins)


def _rowwise_bwd(fn, ins, kinds, gdtypes, cts, tr, *, name):
    B, n = ins[0].shape[:2]
    ni, nc = len(ins), len(cts)
    idx = list(range(ni))

    def body(*refs):
        in_refs, ct_refs, out_refs = refs[:ni], refs[ni:ni + nc], refs[ni + nc:]
        b, j = pl.program_id(0), pl.program_id(1)
        _, vjp = jax.vjp(fn, *[r[...].astype(F32) for r in in_refs])
        grads = vjp(tuple(c[...].astype(F32) for c in ct_refs))
        for o, i in zip(out_refs, idx):
            g = grads[i]
            if kinds[i] == "row":
                o[...] = g.astype(o.dtype)
            else:
                first = (j == 0) if kinds[i] == "batch" else ((j == 0) & (b == 0))

                @pl.when(first)
                def _(o=o, g=g):
                    o[...] = g

                @pl.when(jnp.logical_not(first))
                def _(o=o, g=g):
                    o[...] += g

    specs = _row_specs(ins, kinds, tr)
    return pl.pallas_call(
        body, name=name, grid=(B, n // tr),
        in_specs=specs + [pl.BlockSpec((None, tr, c.shape[2]), lambda b, j: (b, j, 0)) for c in cts],
        out_specs=[specs[i] for i in idx],
        out_shape=[jax.ShapeDtypeStruct(ins[i].shape, gdtypes[i]) for i in idx],
        compiler_params=_params("arbitrary", "arbitrary"),
    )(*ins, *cts)


ROW_TILE = 256


def _rms_mod(x, g, sc, sh):
    y = x * lax.rsqrt(jnp.mean(x * x, axis=-1, keepdims=True) + EPS)
    return (y * g) * (1.0 + sc) + sh


def _norm_tile(x, g, sc, sh):
    return x, _rms_mod(x, g, sc, sh)


def _res_norm_tile(xb, y, gate, g, sc, sh):
    x = xb + gate * y
    return x, _rms_mod(x, g, sc, sh)


_NORM_KINDS = ("row", "glob", "batch", "batch")
_RES_NORM_KINDS = ("row", "row", "batch", "glob", "batch", "batch")


def _norm_fwd(tag, st, g, sc, sh):
    xb, y, gate = st
    tr = min(ROW_TILE, xb.shape[1])
    d = xb.shape[2]
    if y is None:
        h, = _rowwise_fwd(lambda *a: (_rms_mod(*a),), [xb, g, sc, sh], _NORM_KINDS, [(d, BF16)], tr, name=f"{tag}_fwd")
        return xb, h
    return _rowwise_fwd(_res_norm_tile, [xb, y, gate, g, sc, sh], _RES_NORM_KINDS, [(d, F32), (d, BF16)], tr, name=f"{tag}_fwd")


def _norm_bwd(tag, st, g, sc, sh, dx, dh):
    xb, y, gate = st
    tr = min(ROW_TILE, xb.shape[1])
    if y is None:
        dxb, dg, dsc, dsh = _rowwise_bwd(_norm_tile, [xb, g, sc, sh], _NORM_KINDS, [F32] * 4, [dx, dh], tr, name=f"{tag}_bwd")
        return dxb, None, None, dg, dsc, dsh
    return tuple(_rowwise_bwd(_res_norm_tile, [xb, y, gate, g, sc, sh], _RES_NORM_KINDS, [F32, BF16, F32, F32, F32, F32],
                              [dx, dh], tr, name=f"{tag}_bwd"))


def _loss_head(st, final_g, target, *, name):
    xb, y, gate = st
    B, n, d = xb.shape
    tr = min(ROW_TILE, n)

    def tile_loss(xv, yv, gt, g, t):
        x = xv + gt * yv
        out = x * lax.rsqrt(jnp.mean(x * x, axis=-1, keepdims=True) + EPS) * g
        return 0.5 * jnp.sum(jnp.mean(jnp.square(out - t), axis=-1))

    def body(x_ref, y_ref, gate_ref, g_ref, t_ref, loss_ref, dx_ref, dy_ref, dgate_ref, dg_ref):
        b, j = pl.program_id(0), pl.program_id(1)
        val, (dx, dy, dgate, dg) = jax.value_and_grad(tile_loss, argnums=(0, 1, 2, 3))(
            x_ref[...], y_ref[...], gate_ref[...], g_ref[...], t_ref[...])
        dx_ref[...] = dx
        dy_ref[...] = dy.astype(dy_ref.dtype)

        @pl.when(j == 0)
        def _():
            loss_ref[...] = jnp.zeros_like(loss_ref)
            dgate_ref[...] = jnp.zeros_like(dgate_ref)

        @pl.when((j == 0) & (b == 0))
        def _():
            dg_ref[...] = jnp.zeros_like(dg_ref)

        loss_ref[...] += jnp.full(loss_ref.shape, val, F32)
        dgate_ref[...] += dgate
        dg_ref[...] += dg

    row = pl.BlockSpec((None, tr, d), lambda b, j: (b, j, 0))
    per_sample = pl.BlockSpec((None, 1, d), lambda b, j: (b, 0, 0))
    whole = pl.BlockSpec((1, d), lambda b, j: (0, 0))
    return pl.pallas_call(
        body, name=name, grid=(B, n // tr),
        in_specs=[row, row, per_sample, whole, row],
        out_specs=[pl.BlockSpec((None, 1, 128), lambda b, j: (b, 0, 0)), row, row, per_sample, whole],
        out_shape=[jax.ShapeDtypeStruct((B, 1, 128), F32), jax.ShapeDtypeStruct((B, n, d), F32),
                   jax.ShapeDtypeStruct((B, n, d), BF16), jax.ShapeDtypeStruct((B, 1, d), F32), jax.ShapeDtypeStruct((1, d), F32)],
        compiler_params=_params("arbitrary", "arbitrary"),
    )(xb, y, gate, final_g, target)


def _exchange(arrs, *, scatter, name):
    k = len(arrs)

    def body(*refs):
        ins, outs, sems = refs[:k], refs[k:2 * k], refs[2 * k:]
        _exch_start(ins, outs, sems, scatter)
        _exch_wait(ins, outs, sems, scatter)

    any_spec = pl.BlockSpec(memory_space=pl.ANY)
    return pl.pallas_call(
        body, name=name,
        in_specs=[any_spec] * k, out_specs=[any_spec] * k,
        out_shape=_exch_out_shapes(arrs, scatter), scratch_shapes=_exch_sems(k),
        compiler_params=pltpu.CompilerParams(has_side_effects=True),
    )(*arrs)


def _exch_out_shapes(arrs, scatter):
    return [jax.ShapeDtypeStruct(a.shape if scatter else (N_DEV,) + a.shape, a.dtype) for a in arrs]


def _exch_sems(k):
    return [pltpu.SemaphoreType.DMA((k * (N_DEV - 1),)), pltpu.SemaphoreType.DMA((k * (N_DEV - 1),)),
            pltpu.SemaphoreType.DMA((k,))]


def _exch_copies(ins, outs, sems, scatter):
    send_sems, recv_sems, local_sems = sems
    x, y, c = lax.axis_index("x"), lax.axis_index("y"), lax.axis_index("c")
    me = 4 * x + 2 * y + c
    owns, sends, recvs = [], [], []
    for a in range(len(ins)):
        owns.append(pltpu.make_async_copy(ins[a].at[me] if scatter else ins[a], outs[a].at[me], local_sems.at[a]))
        for r in range(1, N_DEV):
            fx, fy, fc = (r >> 2) & 1, (r >> 1) & 1, r & 1
            px, py, pc = (x + fx) % 2, (y + fy) % 2, (c + fc) % 2
            peer = 4 * px + 2 * py + pc
            s = a * (N_DEV - 1) + r - 1
            mk = functools.partial(pltpu.make_async_remote_copy, src_ref=ins[a].at[peer] if scatter else ins[a],
                                   send_sem=send_sems.at[s], recv_sem=recv_sems.at[s],
                                   device_id=(px, py, pc), device_id_type=pl.DeviceIdType.MESH)
            sends.append(mk(dst_ref=outs[a].at[me]))
            recvs.append(mk(dst_ref=outs[a].at[peer]))
    return owns, sends, recvs


def _exch_start(ins, outs, sems, scatter):
    owns, sends, _ = _exch_copies(ins, outs, sems, scatter)
    for cp in owns + sends:
        cp.start()


def _exch_wait(ins, outs, sems, scatter):
    owns, sends, recvs = _exch_copies(ins, outs, sems, scatter)
    for rc in recvs:
        rc.wait_recv()
    for cp in sends:
        cp.wait_send()
    for own in owns:
        own.wait()


MOD_ROWS = 48


def _mod_tile(cc, w, b):
    s = cc * jax.nn.sigmoid(cc)
    return lax.dot_general(s.astype(BF16), w.astype(BF16), (((1,), (0,)), ((), ())), preferred_element_type=F32) + b


def _mod_fwd(cc, w_mod, b_shard, *, name):
    L, d, wcols = w_mod.shape

    def body(cc_ref, w_ref, b_ref, o_ref):
        o_ref[...] = _mod_tile(cc_ref[...], w_ref[...], b_ref[...])

    return pl.pallas_call(
        body, name=name, grid=(L,),
        in_specs=[pl.BlockSpec((MOD_ROWS, d), lambda l: (0, 0)), pl.BlockSpec((None, d, wcols), lambda l: (l, 0, 0)),
                  pl.BlockSpec((None, 1, wcols), lambda l: (l, 0, 0))],
        out_specs=pl.BlockSpec((None, MOD_ROWS, wcols), lambda l: (l, 0, 0)),
        out_shape=jax.ShapeDtypeStruct((L, MOD_ROWS, wcols), F32),
        compiler_params=_params("parallel"),
    )(cc, w_mod, b_shard)


def _mod_bwd(cc, w_mod, b_shard, dm, *, name):
    L, d, wcols = w_mod.shape

    def body(cc_ref, w_ref, b_ref, dm_ref, dcc_ref, dw_ref):
        _, vjp = jax.vjp(_mod_tile, cc_ref[...], w_ref[...], b_ref[...])
        dcc, dw, _ = vjp(dm_ref[...])
        dw_ref[...] = dw

        @pl.when(pl.program_id(0) == 0)
        def _():
            dcc_ref[...] = dcc

        @pl.when(pl.program_id(0) > 0)
        def _():
            dcc_ref[...] += dcc

    return pl.pallas_call(
        body, name=name, grid=(L,),
        in_specs=[pl.BlockSpec((MOD_ROWS, d), lambda l: (0, 0)), pl.BlockSpec((None, d, wcols), lambda l: (l, 0, 0)),
                  pl.BlockSpec((None, 1, wcols), lambda l: (l, 0, 0)), pl.BlockSpec((None, MOD_ROWS, wcols), lambda l: (l, 0, 0))],
        out_specs=[pl.BlockSpec((MOD_ROWS, d), lambda l: (0, 0)), pl.BlockSpec((None, d, wcols), lambda l: (l, 0, 0))],
        out_shape=[jax.ShapeDtypeStruct((MOD_ROWS, d), F32), jax.ShapeDtypeStruct((L, d, wcols), F32)],
        compiler_params=_params("arbitrary"),
    )(cc, w_mod, b_shard, dm)


def _sum_leading(a, *, name):
    K, R, C = a.shape
    tr = _tile8(R, 256)

    def body(a_ref, o_ref):
        acc = a_ref[0].astype(F32)
        for i in range(1, K):
            acc = acc + a_ref[i].astype(F32)
        o_ref[...] = acc

    return pl.pallas_call(
        body, name=name, grid=(R // tr,),
        in_specs=[pl.BlockSpec((K, tr, C), lambda i: (0, i, 0))],
        out_specs=pl.BlockSpec((tr, C), lambda i: (i, 0)),
        out_shape=jax.ShapeDtypeStruct((R, C), F32),
        compiler_params=_params("parallel"),
    )(a)


def _tile8(dim, target):
    if dim <= target:
        return dim
    t = (target // 8) * 8
    while t >= 8:
        if dim % t == 0:
            return t
        t -= 8
    raise ValueError(f"no row tile for {dim}")


def _adamw_math(g, w, m, v):
    m = ADAM_B1 * m + (1.0 - ADAM_B1) * g
    v = ADAM_B2 * v + (1.0 - ADAM_B2) * jnp.square(g)
    m_hat = m / (1.0 - ADAM_B1 ** ADAM_STEP)
    v_hat = v / (1.0 - ADAM_B2 ** ADAM_STEP)
    delta = -ADAM_LR * (m_hat / (jnp.sqrt(v_hat) + ADAM_EPS) + ADAM_WD * w)
    return delta, m, v


def _adamw(g, w, m, v, *, name):
    R, C = w.shape
    parts = g.ndim == 3
    tr = _tile8(R, 256)

    def body(g_ref, w_ref, m_ref, v_ref, go_ref, d_ref, mo_ref, vo_ref):
        if parts:
            gv = g_ref[0].astype(F32)
            for i in range(1, g_ref.shape[0]):
                gv = gv + g_ref[i].astype(F32)
        else:
            gv = g_ref[...]
        go_ref[...] = gv
        d_ref[...], mo_ref[...], vo_ref[...] = _adamw_math(gv, w_ref[...], m_ref[...], v_ref[...])

    tile = pl.BlockSpec((tr, C), lambda i: (i, 0))
    g_spec = pl.BlockSpec((g.shape[0], tr, C), lambda i: (0, i, 0)) if parts else tile
    return pl.pallas_call(
        body, name=name, grid=(R // tr,),
        in_specs=[g_spec, tile, tile, tile], out_specs=[tile] * 4,
        out_shape=[jax.ShapeDtypeStruct((R, C), F32)] * 4,
        compiler_params=_params("parallel"),
    )(g, w, m, v)


def _block_diag(w):
    g = w.shape[0]
    rows = [jnp.concatenate([w[i] if j == i else jnp.zeros_like(w[i]) for j in range(g)], axis=1) for i in range(g)]
    return jnp.concatenate(rows, axis=0)


def _diag_blocks(w_bd):
    g = POOL_W // POOL_GROUP
    return jnp.stack([w_bd[i * POOL_GROUP:(i + 1) * POOL_GROUP, i * POOL_GROUP:(i + 1) * POOL_GROUP] for i in range(g)])


def _flat(a):
    return a.reshape(-1, a.shape[-1])


def _sub_block_fwd(tag, st, mods, wl, kvc, *, local, kv_only, exch=None):
    sh1, sc1, g1, sh2, sc2, g2 = mods
    B, n, d = st[0].shape
    x, h = _norm_fwd(f"{tag}_norm1", st, wl["n1"], sc1, sh1)
    if kv_only:
        kv = _mm(_flat(h), wl["w_in"][:, ATTN_W:ATTN_W + 2 * KV_W], name=f"{tag}_kv").reshape(B, n, 2 * KV_W)
        return None, dict(st=st, h=h, kvc=kv), []
    u = _mm(_flat(h), wl["w_in"], name=f"{tag}_in").reshape(B, n, IN_W)
    if not local:
        kvc = u[:, :, ATTN_W:ATTN_W + 2 * KV_W]
    mix, got = _mixer_fwd(f"{tag}_mix", u, kvc, wl["margs"], local, exch)
    y = _mm(_flat(mix), wl["w_out"], name=f"{tag}_out").reshape(B, n, d)
    st2 = (x, y, g1)
    x1, h2 = _norm_fwd(f"{tag}_norm2", st2, wl["n2"], sc2, sh2)
    gu, act = _mm_swiglu(_flat(h2), wl["w_ffn_in"], name=f"{tag}_ffn_in")
    y2 = _mm(act, wl["w_ffn_out"], name=f"{tag}_ffn_out").reshape(B, n, d)
    return (x1, y2, g2), dict(st=st, h=h, u=u, kvc=kvc, mix=mix, st2=st2, h2=h2, gu=gu, act=act), got


def _sub_block_bwd(tag, sv, mods, wl, dx1, dy2, dkv_in, *, local, kv_only, exch=None):
    sh1, sc1, g1, sh2, sc2, g2 = mods
    B, n, d = sv["st"][0].shape
    gw = {}
    if kv_only:
        dkv = _flat(dkv_in).astype(BF16)
        dh = _mm(dkv, wl["w_in_t"][ATTN_W:ATTN_W + 2 * KV_W, :], name=f"{tag}_kv_dx").reshape(B, n, d)
        gw["w_in_kv"] = _mm(_flat(sv["h"]).T, dkv, out_dtype=BF16, name=f"{tag}_kv_dw")
        dxb, dy_prev, dgate_prev, gw["n1"], dsc1, dsh1 = _norm_bwd(f"{tag}_norm1", sv["st"], wl["n1"], sc1, sh1,
                                                                    jnp.zeros((B, n, d), F32), dh)
        return (dxb, dy_prev, dgate_prev), gw, dict(sh1=dsh1, sc1=dsc1), None, []

    dy2f = _flat(dy2)
    gw["w_ffn_out"] = _mm(sv["act"].T, dy2f, out_dtype=BF16, name=f"{tag}_ffn_out_dw")
    dgu = _mm_dswiglu(dy2f, wl["w_ffn_out_t"], sv["gu"], name=f"{tag}_ffn_out_dx")
    dh2 = _mm(dgu, wl["w_ffn_in_t"], name=f"{tag}_ffn_in_dx").reshape(B, n, d)
    gw["w_ffn_in"] = _mm(_flat(sv["h2"]).T, dgu, out_dtype=BF16, name=f"{tag}_ffn_in_dw")
    dx, dy, dg1, gw["n2"], dsc2, dsh2 = _norm_bwd(f"{tag}_norm2", sv["st2"], wl["n2"], sc2, sh2, dx1, dh2)

    dyf = _flat(dy)
    dmix = _mm(dyf, wl["w_out_t"], name=f"{tag}_out_dx").reshape(B, n, d)
    gw["w_out"] = _mm(_flat(sv["mix"]).T, dyf, out_dtype=BF16, name=f"{tag}_out_dw")
    (dq, dk, dv, da, dg, dpu), (dkc, dvc), gw["margs"], got = _mixer_bwd(f"{tag}_mix", sv["u"], sv["kvc"], wl["margs"], dmix,
                                                                     local, exch)
    if local:
        dkv_out = jnp.concatenate([dkc, dvc], axis=-1)
    else:
        dk = dkc + dkv_in[:, :, :KV_W]
        dv = dvc + dkv_in[:, :, KV_W:]
        dkv_out = None
    du = _flat(jnp.concatenate([dq, dk, dv, da, dg, dpu], axis=-1).astype(BF16))
    dh = _mm(du, wl["w_in_t"], name=f"{tag}_in_dx").reshape(B, n, d)
    gw["w_in"] = _mm(_flat(sv["h"]).T, du, out_dtype=BF16, name=f"{tag}_in_dw")
    dxb, dy_prev, dgate_prev, gw["n1"], dsc1, dsh1 = _norm_bwd(f"{tag}_norm1", sv["st"], wl["n1"], sc1, sh1, dx, dh)
    return (dxb, dy_prev, dgate_prev), gw, dict(sh1=dsh1, sc1=dsc1, g1=dg1, sh2=dsh2, sc2=dsc2), dkv_out, got


BIG_W = ("w_in", "w_out", "w_ffn_in", "w_ffn_out")


def _local_step(x, ctx, m_loc, m_ctx, p, final_g, target, big):
    B = x.shape[0]
    depth = m_loc.shape[0]
    lat_mods = [[t[:, None, :] for t in jnp.split(m_loc[l], 6, axis=-1)] for l in range(depth)]
    ctx_mods = [[jnp.broadcast_to(t[None, None, :], (B, 1, D_MODEL)) for t in jnp.split(m_ctx[l], 6)] for l in range(depth)]

    st, cst = (x, None, None), (ctx, None, None)
    layers, saved, csaved = [], [], []
    got = []
    for l in range(depth):
        last = l == depth - 1
        wb = big.weights(l, got)
        w_ffn_in = _interleave_ffn(wb["w_ffn_in"])
        wl = dict(
            n1=p["norm1_g"][l][None, :], n2=p["norm2_g"][l][None, :],
            w_in=wb["w_in"], w_out=wb["w_out"], w_ffn_in=w_ffn_in, w_ffn_out=wb["w_ffn_out"],
            w_in_t=wb["w_in"].T, w_out_t=wb["w_out"].T, w_ffn_in_t=w_ffn_in.T, w_ffn_out_t=wb["w_ffn_out"].T,
            margs=(p["attn_sink"][l][None, :], p["conv_dw"][l], p["conv_dw_b"][l][None, :], p["conv_ln_g"][l][None, :],
                   p["conv_ln_b"][l][None, :], _block_diag(p["pool_w"][l]), p["pool_scale"][l][None, :]))
        layers.append(wl)
        cst, csv, _ = _sub_block_fwd(f"l{l}c", cst, ctx_mods[l], wl, None, local=False, kv_only=last)
        st, sv, got = _sub_block_fwd(f"l{l}", st, lat_mods[l], wl, csv["kvc"], local=True, kv_only=False,
                                     exch=big.gather_exch(l))
        saved.append(sv)
        csaved.append(csv)
    loss_rows, dx1, dy2, dgate, dfinal = _loss_head(st, final_g[None, :], target, name="loss_head")

    dm_loc, dm_ctx = [None] * depth, [None] * depth
    gws = [None] * depth
    cdx1 = cdy2 = cdgate = None
    upper = None
    for l in reversed(range(depth)):
        last = l == depth - 1
        (dx1, dy2_prev, dgate_prev), gw, dm, dkv, got = _sub_block_bwd(
            f"l{l}", saved[l], lat_mods[l], layers[l], dx1, dy2, None, local=True, kv_only=False,
            exch=None if upper is None else big.scatter_exch(l + 1, upper))
        if upper is not None:
            big.done(l + 1, upper, got)
        dm["g2"] = dgate
        (cdx1, cdy2_prev, cdgate_prev), cgw, cdm, _, _ = _sub_block_bwd(
            f"l{l}c", csaved[l], ctx_mods[l], layers[l], cdx1, cdy2, dkv, local=False, kv_only=last)
        if not last:
            cdm["g2"] = cdgate
        dy2, dgate, cdy2, cdgate = dy2_prev, dgate_prev, cdy2_prev, cdgate_prev
        order = ("sh1", "sc1", "g1", "sh2", "sc2", "g2")
        dm_loc[l] = jnp.concatenate([dm[k][:, 0, :] for k in order], axis=-1)
        dm_ctx[l] = jnp.concatenate([jnp.sum(cdm[k][:, 0, :], axis=0) if k in cdm else jnp.zeros((D_MODEL,), F32)
                                     for k in order])
        if last:
            g_in = gw["w_in"].at[:, ATTN_W:ATTN_W + 2 * KV_W].add(cgw["w_in_kv"])
            merged = dict(w_in=g_in, w_out=gw["w_out"], w_ffn_in=gw["w_ffn_in"], w_ffn_out=gw["w_ffn_out"],
                          n1=gw["n1"] + cgw["n1"], n2=gw["n2"], margs=gw["margs"])
        else:
            merged = {k: gw[k] + cgw[k] for k in ("w_in", "w_out", "w_ffn_in", "w_ffn_out", "n1", "n2")}
            merged["margs"] = tuple(a + b for a, b in zip(gw["margs"], cgw["margs"]))
        merged["w_ffn_in"] = _deinterleave_ffn(merged["w_ffn_in"])
        gws[l] = merged
        upper = {k: merged[k] for k in BIG_W}
    big.done(0, upper, [])

    stack = lambda f: jnp.stack([f(gws[l]) for l in range(depth)])
    dp = dict(
        norm1_g=stack(lambda g: g["n1"][0]), norm2_g=stack(lambda g: g["n2"][0]),
        attn_sink=stack(lambda g: g["margs"][0][0]), conv_dw=stack(lambda g: g["margs"][1]),
        conv_dw_b=stack(lambda g: g["margs"][2][0]), conv_ln_g=stack(lambda g: g["margs"][3][0]),
        conv_ln_b=stack(lambda g: g["margs"][4][0]), pool_w=stack(lambda g: _diag_blocks(g["margs"][5])),
        pool_scale=stack(lambda g: g["margs"][6][0]))
    return jnp.sum(loss_rows[:, 0, 0]), dx1, jnp.stack(dm_loc), jnp.stack(dm_ctx), dp, dfinal[0]


PACK_COLS = 1024


def _pack(arrs):
    flat = jnp.concatenate([a.reshape(-1).astype(F32) for a in arrs])
    rows = -(-flat.shape[0] // (8 * PACK_COLS)) * 8
    return jnp.pad(flat, (0, rows * PACK_COLS - flat.shape[0])).reshape(rows, PACK_COLS)


def _unpack(slab, like):
    flat = slab.reshape(-1)
    out, off = [], 0
    for a in like:
        out.append(flat[off:off + a.size].reshape(a.shape))
        off += a.size
    return out


def _shard_cols(gathered):
    _, L, R, C = gathered.shape
    return jnp.transpose(gathered, (1, 2, 0, 3)).reshape(L, R, N_DEV * C)


class _ShardedWeights:
    BY_COLS = ("w_in", "w_ffn_in")

    def __init__(self, shards, first):
        self.shards = shards
        self.first = first
        self.depth = shards[BIG_W[0]].shape[0]
        self.parts = {}
        self.full = {}

    def weights(self, l, got):
        blocks = self.first if l == 0 else got
        out = {}
        for name, g in zip(BIG_W, blocks):
            _, R, C = g.shape
            out[name] = jnp.transpose(g, (1, 0, 2)).reshape(R, N_DEV * C) if name in self.BY_COLS else g.reshape(N_DEV * R, C)
        return out

    def gather_exch(self, l):
        if l + 1 >= self.depth:
            return None
        return [self.shards[name][l + 1] for name in BIG_W], False

    def split(self, grads):
        out = []
        for name in BIG_W:
            g = grads[name]
            if name in self.BY_COLS:
                R, C8 = g.shape
                out.append(jnp.transpose(g.reshape(R, N_DEV, C8 // N_DEV), (1, 0, 2)))
            else:
                R8, C = g.shape
                out.append(g.reshape(N_DEV, R8 // N_DEV, C))
        return out

    def scatter_exch(self, l, grads):
        return self.split(grads), True

    def done(self, l, grads, got):
        if got:
            self.parts[l] = got
        else:
            self.full[l] = grads


def _as_rows(a, leading=0):
    return a.reshape(*a.shape[:leading], -1, PACK_COLS)


SMALL = ("c_ctx", "b_mod", "norm1_g", "norm2_g", "conv_dw_b", "conv_ln_g", "conv_ln_b", "attn_sink", "pool_w",
         "pool_scale", "final_g", "conv_dw")
BIG = ("w_mod", "w_in", "w_out", "w_ffn_in", "w_ffn_out")
ORDER = ("c_ctx", "w_mod", "b_mod", "norm1_g", "norm2_g", "w_in", "conv_dw", "conv_dw_b", "conv_ln_g", "conv_ln_b",
         "attn_sink", "pool_w", "pool_scale", "w_out", "w_ffn_in", "w_ffn_out", "final_g")


def kernel(x, c, ctx, c_ctx, w_mod, b_mod, norm1_g, norm2_g, w_in, conv_dw, conv_dw_b, conv_ln_g, conv_ln_b, attn_sink, pool_w, pool_scale, w_out, w_ffn_in, w_ffn_out, final_g, loss_target, m_c_ctx, m_w_mod, m_b_mod, m_norm1_g, m_norm2_g, m_w_in, m_conv_dw, m_conv_dw_b, m_conv_ln_g, m_conv_ln_b, m_attn_sink, m_pool_w, m_pool_scale, m_w_out, m_w_ffn_in, m_w_ffn_out, m_final_g, v_c_ctx, v_w_mod, v_b_mod, v_norm1_g, v_norm2_g, v_w_in, v_conv_dw, v_conv_dw_b, v_conv_ln_g, v_conv_ln_b, v_attn_sink, v_pool_w, v_pool_scale, v_w_out, v_w_ffn_in, v_w_ffn_out, v_final_g):
    w = dict(c_ctx=c_ctx, w_mod=w_mod, b_mod=b_mod, norm1_g=norm1_g, norm2_g=norm2_g, w_in=w_in, conv_dw=conv_dw,
             conv_dw_b=conv_dw_b, conv_ln_g=conv_ln_g, conv_ln_b=conv_ln_b, attn_sink=attn_sink, pool_w=pool_w,
             pool_scale=pool_scale, w_out=w_out, w_ffn_in=w_ffn_in, w_ffn_out=w_ffn_out, final_g=final_g)
    mom = dict(c_ctx=m_c_ctx, w_mod=m_w_mod, b_mod=m_b_mod, norm1_g=m_norm1_g, norm2_g=m_norm2_g, w_in=m_w_in,
               conv_dw=m_conv_dw, conv_dw_b=m_conv_dw_b, conv_ln_g=m_conv_ln_g, conv_ln_b=m_conv_ln_b,
               attn_sink=m_attn_sink, pool_w=m_pool_w, pool_scale=m_pool_scale, w_out=m_w_out, w_ffn_in=m_w_ffn_in,
               w_ffn_out=m_w_ffn_out, final_g=m_final_g)
    var = dict(c_ctx=v_c_ctx, w_mod=v_w_mod, b_mod=v_b_mod, norm1_g=v_norm1_g, norm2_g=v_norm2_g, w_in=v_w_in,
               conv_dw=v_conv_dw, conv_dw_b=v_conv_dw_b, conv_ln_g=v_conv_ln_g, conv_ln_b=v_conv_ln_b,
               attn_sink=v_attn_sink, pool_w=v_pool_w, pool_scale=v_pool_scale, w_out=v_w_out, w_ffn_in=v_w_ffn_in,
               w_ffn_out=v_w_ffn_out, final_g=v_final_g)
    B = x.shape[0]
    depth = w_mod.shape[0]
    mod_cols = w_mod.shape[2]
    dw_cols = conv_dw.shape[2]
    me = 4 * lax.axis_index("x") + 2 * lax.axis_index("y") + lax.axis_index("c")

    shards = {name: w[name].astype(BF16) for name in BIG_W}
    c_all, dw_all, *first = _exchange([c, conv_dw] + [shards[name][0] for name in BIG_W], scatter=False, name="gather_first")
    big = _ShardedWeights(shards, first)
    p = dict(norm1_g=norm1_g, norm2_g=norm2_g, conv_dw=_shard_cols(dw_all), conv_dw_b=conv_dw_b, conv_ln_g=conv_ln_g,
             conv_ln_b=conv_ln_b, attn_sink=attn_sink, pool_w=pool_w, pool_scale=pool_scale)

    cc = jnp.concatenate([c_all.reshape(N_DEV * B, D_MODEL), jnp.broadcast_to(c_ctx[None, :], (N_DEV, D_MODEL)),
                          jnp.zeros((MOD_ROWS - N_DEV * B - N_DEV, D_MODEL), F32)], axis=0)
    b_shard = lax.dynamic_slice_in_dim(b_mod, me * mod_cols, mod_cols, axis=1)[:, None, :]
    m_part = _mod_fwd(cc, w_mod, b_shard, name="mod_fwd")
    m_all, = _exchange([m_part], scatter=False, name="gather_mod")
    m_full = _shard_cols(m_all)
    m_loc = lax.dynamic_slice_in_dim(m_full, me * B, B, axis=1)
    m_ctx = m_full[:, N_DEV * B, :]

    loss_part, dx, dm_loc, dm_ctx, dp, dfinal = _local_step(x, ctx, m_loc, m_ctx, p, final_g, loss_target, big)
    loss = lax.psum(loss_part, AXES)

    dm_rows = jnp.concatenate([dm_loc, dm_ctx[:, None, :], jnp.zeros((depth, 8 - B - 1, 6 * D_MODEL), F32)], axis=1)
    dm_all, = _exchange([dm_rows], scatter=False, name="gather_dmod")
    dm_full = jnp.concatenate([
        jnp.transpose(dm_all[:, :, :B, :], (1, 0, 2, 3)).reshape(depth, N_DEV * B, 6 * D_MODEL),
        jnp.transpose(dm_all[:, :, B, :], (1, 0, 2)),
        jnp.zeros((depth, MOD_ROWS - N_DEV * B - N_DEV, 6 * D_MODEL), F32)], axis=1)
    g_b_mod = jnp.stack([_sum_leading(dm_full[l][:, None, :], name=f"b_mod_grad{l}")[0] for l in range(depth)])
    dm_mine = lax.dynamic_slice_in_dim(dm_full, me * mod_cols, mod_cols, axis=2)
    dcc, g_w_mod = _mod_bwd(cc, w_mod, b_shard, dm_mine, name="mod_bwd")
    g_c_ctx_part = jnp.sum(dcc[N_DEV * B:N_DEV * B + N_DEV], axis=0)

    small_like = [c_ctx, norm1_g, norm2_g, conv_dw_b, conv_ln_g, conv_ln_b, attn_sink, pool_w, pool_scale, final_g,
                  dp["conv_dw"]]
    small_part = _pack([g_c_ctx_part, dp["norm1_g"], dp["norm2_g"], dp["conv_dw_b"], dp["conv_ln_g"], dp["conv_ln_b"],
                        dp["attn_sink"], dp["pool_w"], dp["pool_scale"], dfinal, dp["conv_dw"]])
    small_all, = _exchange([small_part], scatter=False, name="gather_small")
    small_sum = _unpack(_sum_leading(small_all, name="sum_small"), small_like)
    g = dict(zip(("c_ctx", "norm1_g", "norm2_g", "conv_dw_b", "conv_ln_g", "conv_ln_b", "attn_sink", "pool_w",
                  "pool_scale", "final_g"), small_sum[:-1]))
    g["b_mod"] = g_b_mod
    g["conv_dw"] = lax.dynamic_slice_in_dim(small_sum[-1], me * dw_cols, dw_cols, axis=2)

    for l, grads in big.full.items():
        big.parts[l] = _exchange(big.split(grads), scatter=True, name=f"scatter_grads_l{l}")
    parts = [jnp.stack([big.parts[l][i] for l in range(depth)], axis=1) for i in range(len(BIG_W))]

    delta, new_m, new_v = {}, {}, {}
    for name, gpart in zip(("w_in", "w_out", "w_ffn_in", "w_ffn_out"), parts):
        shp = w[name].shape
        res = _adamw(_as_rows(gpart, 1), _as_rows(w[name]), _as_rows(mom[name]), _as_rows(var[name]), name=f"adamw_{name}")
        g[name], delta[name], new_m[name], new_v[name] = [r.reshape(shp) for r in res]
    res = _adamw(_as_rows(g_w_mod), _as_rows(w_mod), _as_rows(m_w_mod), _as_rows(v_w_mod), name="adamw_w_mod")
    g["w_mod"], delta["w_mod"], new_m["w_mod"], new_v["w_mod"] = [r.reshape(w_mod.shape) for r in res]
    res = _adamw(_pack([g[k] for k in SMALL]), _pack([w[k] for k in SMALL]), _pack([mom[k] for k in SMALL]),
                 _pack([var[k] for k in SMALL]), name="adamw_small")
    like = [w[k] for k in SMALL]
    for dst, slab in zip((delta, new_m, new_v), res[1:]):
        dst.update(zip(SMALL, _unpack(slab, like)))

    return (loss, dx, *[g[k] for k in ORDER], *[delta[k] for k in ORDER], *[new_m[k] for k in ORDER],
            *[new_v[k] for k in ORDER])
```

```python
import functools

import numpy as np
import jax
import jax.numpy as jnp
from jax import lax
from jax.experimental import pallas as pl
from jax.experimental.pallas import tpu as pltpu

F32 = jnp.float32
BF16 = jnp.bfloat16

D_MODEL = 1024
GRID_W = 64
HEAD_DIM = 64
ATTN_W = 512
CONV_W = 256
POOL_W = 256
ATTN_HEADS = 8
KV_HEADS = 2
GROUP = ATTN_HEADS // KV_HEADS
KV_W = KV_HEADS * HEAD_DIM
IN_W = ATTN_W + 2 * KV_W + 2 * CONV_W + POOL_W
WINDOW = 128
Q_BLOCK = 128
SPAN = Q_BLOCK + 2 * WINDOW
CONV_KERNEL = 31
POOL_WINDOWS = (2, 4, 8, 16)
POOL_GROUP = 64
ROPE_BASE = 10000.0
D_FF = 2816
EPS = 1e-6
NEG = -1e30
N_DEV = 8
AXES = ("x", "y", "c")

ADAM_LR = 0.001
ADAM_B1 = 0.9
ADAM_B2 = 0.999
ADAM_EPS = 1e-08
ADAM_WD = 0.01
ADAM_STEP = 10

VMEM_LIMIT = 56 * 1024 * 1024
HALO = 16
SEQ_CHUNK = 256


def _params(*sem):
    return pltpu.CompilerParams(dimension_semantics=sem, vmem_limit_bytes=VMEM_LIMIT)


def _tile(dim, target):
    if dim <= target:
        return dim
    t = (target // 128) * 128
    while t >= 128:
        if dim % t == 0:
            return t
        t -= 128
    raise ValueError(f"no tile for {dim}")


MM_K_WHOLE = 2816


def _dot(a, b):
    return lax.dot_general(a.astype(BF16), b.astype(BF16), (((1,), (0,)), ((), ())), preferred_element_type=F32)


def _mm(a, b, *, name, out_dtype=F32, tm=1408, tn=512, tk=2048):
    M, K = a.shape
    K2, N = b.shape
    assert K == K2, (a.shape, b.shape)
    tm = _tile(M, tm)
    tn = _tile(N, tn)
    tk = K if K <= MM_K_WHOLE else _tile(K, tk)
    nk = K // tk

    def body(a_ref, b_ref, o_ref, *scratch):
        part = _dot(a_ref[...], b_ref[...])
        if nk == 1:
            o_ref[...] = part.astype(o_ref.dtype)
        else:
            acc_ref, = scratch
            k = pl.program_id(2)

            @pl.when(k == 0)
            def _():
                acc_ref[...] = part

            @pl.when(k > 0)
            def _():
                acc_ref[...] += part

            @pl.when(k == nk - 1)
            def _():
                o_ref[...] = acc_ref[...].astype(o_ref.dtype)

    return pl.pallas_call(
        body, name=name, grid=(M // tm, N // tn, nk),
        in_specs=[pl.BlockSpec((tm, tk), lambda i, j, k: (i, k)), pl.BlockSpec((tk, tn), lambda i, j, k: (k, j))],
        out_specs=pl.BlockSpec((tm, tn), lambda i, j, k: (i, j)),
        out_shape=jax.ShapeDtypeStruct((M, N), out_dtype),
        scratch_shapes=[pltpu.VMEM((tm, tn), F32)] if nk > 1 else [],
        compiler_params=_params("parallel", "parallel", "arbitrary"),
    )(a, b)


FF_TILE = 256


def _interleave_ffn(w):
    tiles = D_FF // FF_TILE
    cols = [w[..., half * D_FF + j * FF_TILE:half * D_FF + (j + 1) * FF_TILE] for j in range(tiles) for half in range(2)]
    return jnp.concatenate(cols, axis=-1)


def _deinterleave_ffn(w):
    tiles = D_FF // FF_TILE
    cols = [w[..., (2 * j + half) * FF_TILE:(2 * j + half + 1) * FF_TILE] for half in range(2) for j in range(tiles)]
    return jnp.concatenate(cols, axis=-1)


def _swiglu(gu):
    g, u = gu[:, :FF_TILE], gu[:, FF_TILE:]
    return g * jax.nn.sigmoid(g) * u


def _mm_swiglu(a, w_il, *, name, tm=1024, exch=None):
    M, K = a.shape
    tm = _tile(M, tm)

    def body(a_ref, b_ref, gu_ref, act_ref, act_t_ref):
        gu = _dot(a_ref[...], b_ref[...])
        gu_ref[...] = gu
        act = _swiglu(gu)
        act_ref[...] = act.astype(act_ref.dtype)
        act_t_ref[...] = act.T.astype(act_t_ref.dtype)

    grid = (M // tm, D_FF // FF_TILE)
    body, x_in, x_out, x_shapes, x_sems = _riding(body, exch, 2, 3, grid)
    return pl.pallas_call(
        body, name=name, grid=grid,
        in_specs=[pl.BlockSpec((tm, K), lambda i, j: (i, 0)), pl.BlockSpec((K, 2 * FF_TILE), lambda i, j: (0, j))] + x_in,
        out_specs=[pl.BlockSpec((tm, 2 * FF_TILE), lambda i, j: (i, j)), pl.BlockSpec((tm, FF_TILE), lambda i, j: (i, j)),
                   pl.BlockSpec((FF_TILE, tm), lambda i, j: (j, i))] + x_out,
        out_shape=[jax.ShapeDtypeStruct((M, 2 * D_FF), F32), jax.ShapeDtypeStruct((M, D_FF), BF16),
                   jax.ShapeDtypeStruct((D_FF, M), BF16)] + x_shapes,
        scratch_shapes=x_sems,
        compiler_params=_params("arbitrary", "arbitrary") if exch else _params("parallel", "parallel"),
    )(a, w_il, *(exch[0] if exch else []))


def _mm_dswiglu(dy, w_out_t, gu, *, name, tm=1024):
    M, K = dy.shape
    tm = _tile(M, tm)

    def body(dy_ref, b_ref, gu_ref, o_ref):
        dact = _dot(dy_ref[...], b_ref[...])
        _, vjp = jax.vjp(_swiglu, gu_ref[...])
        o_ref[...] = vjp(dact)[0].astype(o_ref.dtype)

    return pl.pallas_call(
        body, name=name, grid=(M // tm, D_FF // FF_TILE),
        in_specs=[pl.BlockSpec((tm, K), lambda i, j: (i, 0)), pl.BlockSpec((K, FF_TILE), lambda i, j: (0, j)),
                  pl.BlockSpec((tm, 2 * FF_TILE), lambda i, j: (i, j))],
        out_specs=pl.BlockSpec((tm, 2 * FF_TILE), lambda i, j: (i, j)),
        out_shape=jax.ShapeDtypeStruct((M, 2 * D_FF), BF16),
        compiler_params=_params("parallel", "parallel"),
    )(dy, w_out_t, gu)


def _rope_tables(n):
    rows = n // GRID_W
    row = jnp.repeat(jnp.arange(rows), GRID_W).astype(F32)
    col = jnp.tile(jnp.arange(GRID_W), rows).astype(F32)
    half = HEAD_DIM // 2
    inv = ROPE_BASE ** (-jnp.arange(0, half, 2, dtype=F32) / half)
    ar = row[:, None] * inv
    ac = col[:, None] * inv
    ang = jnp.concatenate([ar, ar, ac, ac], axis=-1)
    return jnp.cos(ang), jnp.sin(ang)


def _rot_half(x):
    w = x.shape[-1]
    lane = lax.broadcasted_iota(jnp.int32, x.shape, 1)
    up = pltpu.roll(x, w - 16, 1)
    down = pltpu.roll(x, 16, 1)
    return jnp.where((lane & 16) == 0, -up, down)


def _rope(x, cos, sin):
    return x * cos + _rot_half(x) * sin


def _rope_bwd(d, cos, sin):
    return d * cos - _rot_half(d * sin)


def _dot_nt(a, b):
    return lax.dot_general(a.astype(BF16), b.astype(BF16), (((1,), (1,)), ((), ())), preferred_element_type=F32)


def _attn_core(q_st, kc, vc, ks, vs, sink_rows, bias):
    qs = q_st * (HEAD_DIM ** -0.5)
    s_c = _dot_nt(qs, kc)
    mx = jnp.maximum(jnp.max(s_c, axis=1, keepdims=True), sink_rows)
    if ks is not None:
        s_l = _dot_nt(qs, ks) + bias
        mx = jnp.maximum(mx, jnp.max(s_l, axis=1, keepdims=True))
    e_c = jnp.exp(s_c - mx)
    den = jnp.sum(e_c, axis=1, keepdims=True) + jnp.exp(sink_rows - mx)
    if ks is not None:
        e_l = jnp.exp(s_l - mx)
        den = den + jnp.sum(e_l, axis=1, keepdims=True)
    inv = 1.0 / den
    o = _dot(e_c * inv, vc)
    if ks is not None:
        o = o + _dot(e_l * inv, vs)
    return o


def _stack_heads(x, kh):
    return jnp.concatenate([x[:, (GROUP * kh + g) * HEAD_DIM:(GROUP * kh + g + 1) * HEAD_DIM] for g in range(GROUP)], axis=0)


def _sink_rows(sink, kh):
    return jnp.concatenate([jnp.broadcast_to(sink[:, GROUP * kh + g:GROUP * kh + g + 1], (Q_BLOCK, 1)) for g in range(GROUP)], axis=0)


def _window_bias(start, s0):
    r = lax.broadcasted_iota(jnp.int32, (Q_BLOCK, SPAN), 0)
    c = lax.broadcasted_iota(jnp.int32, (Q_BLOCK, SPAN), 1)
    bias = jnp.where(jnp.abs(start - s0 + r - c) <= WINDOW, 0.0, NEG).astype(F32)
    return jnp.concatenate([bias] * GROUP, axis=0)


def _span_start(i, n):
    start = i * Q_BLOCK
    s0 = jnp.clip(start - WINDOW, 0, n - SPAN)
    return start, pl.multiple_of(s0, Q_BLOCK)


def _riding(body, exch, n_in, n_out, grid):
    if exch is None:
        return body, [], [], [], []
    arrs, scatter = exch
    k = len(arrs)

    def wrapped(*refs):
        ins, xin = refs[:n_in], refs[n_in:n_in + k]
        outs, xout = refs[n_in + k:n_in + k + n_out], refs[n_in + k + n_out:n_in + 2 * k + n_out]
        sems = refs[n_in + 2 * k + n_out:]
        b, i = pl.program_id(0), pl.program_id(1)

        @pl.when((b == 0) & (i == 0))
        def _():
            _exch_start(xin, xout, sems, scatter)

        body(*ins, *outs)

        @pl.when((b == grid[0] - 1) & (i == grid[1] - 1))
        def _():
            _exch_wait(xin, xout, sems, scatter)

    any_spec = pl.BlockSpec(memory_space=pl.ANY)
    return wrapped, [any_spec] * k, [any_spec] * k, _exch_out_shapes(arrs, scatter), _exch_sems(k)


def _attn_fwd(u, kvc, sink, cos, sin, *, local, name, exch=None):
    B, n, _ = u.shape
    n_ctx = kvc.shape[1]
    nb = n // Q_BLOCK
    assert (not local) or n >= SPAN

    def body(q_ref, k_ref, v_ref, kc_ref, vc_ref, sink_ref, cq_ref, sq_ref, ck_ref, sk_ref, o_ref):
        i = pl.program_id(1)
        q = q_ref[...]
        sink_v = sink_ref[...]
        kc = kc_ref[...]
        vc = vc_ref[...]
        if local:
            start, s0 = _span_start(i, n)
            q = _rope(q, cq_ref[...], sq_ref[...])
            ks = _rope(k_ref[pl.ds(s0, SPAN), :], ck_ref[pl.ds(s0, SPAN), :], sk_ref[pl.ds(s0, SPAN), :])
            vs = v_ref[pl.ds(s0, SPAN), :]
            bias = _window_bias(start, s0)
        for kh in range(KV_HEADS):
            sl = slice(kh * HEAD_DIM, (kh + 1) * HEAD_DIM)
            window = (ks[:, sl], vs[:, sl]) if local else (None, None)
            o = _attn_core(_stack_heads(q, kh), kc[:, sl], vc[:, sl], *window, _sink_rows(sink_v, kh), bias if local else None)
            for g in range(GROUP):
                h = GROUP * kh + g
                o_ref[:, h * HEAD_DIM:(h + 1) * HEAD_DIM] = o[g * Q_BLOCK:(g + 1) * Q_BLOCK, :]

    seq = lambda blk: pl.BlockSpec((None, n, KV_W), lambda b, i: (b, 0, blk))
    ctxs = lambda blk: pl.BlockSpec((None, n_ctx, KV_W), lambda b, i: (b, 0, blk))
    full = lambda a: pl.BlockSpec(a.shape, lambda b, i: (0,) * a.ndim)
    cos_q, sin_q = jnp.tile(cos, (1, ATTN_HEADS)), jnp.tile(sin, (1, ATTN_HEADS))
    cos_k, sin_k = jnp.tile(cos, (1, KV_HEADS)), jnp.tile(sin, (1, KV_HEADS))
    body, x_in, x_out, x_shapes, x_sems = _riding(body, exch, 10, 1, (B, nb))
    return pl.pallas_call(
        body, name=name, grid=(B, nb),
        in_specs=[pl.BlockSpec((None, Q_BLOCK, ATTN_W), lambda b, i: (b, i, 0)),
                  seq(ATTN_W // KV_W), seq(ATTN_W // KV_W + 1), ctxs(0), ctxs(1), full(sink),
                  pl.BlockSpec((Q_BLOCK, ATTN_W), lambda b, i: (i, 0)), pl.BlockSpec((Q_BLOCK, ATTN_W), lambda b, i: (i, 0)),
                  full(cos_k), full(sin_k)] + x_in,
        out_specs=[pl.BlockSpec((None, Q_BLOCK, ATTN_W), lambda b, i: (b, i, 0))] + x_out,
        out_shape=[jax.ShapeDtypeStruct((B, n, ATTN_W), F32)] + x_shapes,
        scratch_shapes=x_sems,
        compiler_params=_params("arbitrary", "arbitrary"),
    )(u, u, u, kvc, kvc, sink, cos_q, sin_q, cos_k, sin_k, *(exch[0] if exch else []))


def _attn_bwd(u, kvc, sink, cos, sin, do_src, do_blk, *, local, name, exch=None):
    B, n, _ = u.shape
    n_ctx = kvc.shape[1]
    nb = n // Q_BLOCK

    def body(q_ref, k_ref, v_ref, kc_ref, vc_ref, sink_ref, cq_ref, sq_ref, ck_ref, sk_ref, do_ref,
             dq_ref, dk_ref, dv_ref, dkc_ref, dvc_ref, dsink_ref):
        b = pl.program_id(0)
        i = pl.program_id(1)

        @pl.when(i == 0)
        def _():
            dk_ref[...] = jnp.zeros_like(dk_ref)
            dv_ref[...] = jnp.zeros_like(dv_ref)
            dkc_ref[...] = jnp.zeros_like(dkc_ref)
            dvc_ref[...] = jnp.zeros_like(dvc_ref)

        @pl.when((i == 0) & (b == 0))
        def _():
            dsink_ref[...] = jnp.zeros_like(dsink_ref)

        q = q_ref[...]
        do = do_ref[...]
        sink_v = sink_ref[...]
        kc = kc_ref[...]
        vc = vc_ref[...]
        if local:
            start, s0 = _span_start(i, n)
            ck = ck_ref[pl.ds(s0, SPAN), :]
            sk = sk_ref[pl.ds(s0, SPAN), :]
            q = _rope(q, cq_ref[...], sq_ref[...])
            ks = _rope(k_ref[pl.ds(s0, SPAN), :], ck, sk)
            vs = v_ref[pl.ds(s0, SPAN), :]
            bias = _window_bias(start, s0)
        dsink = jnp.zeros((1, ATTN_HEADS), F32)
        lane8 = lax.broadcasted_iota(jnp.int32, (1, ATTN_HEADS), 1)
        parts = []
        for kh in range(KV_HEADS):
            sl = slice(kh * HEAD_DIM, (kh + 1) * HEAD_DIM)
            if local:
                core = functools.partial(_attn_core, bias=bias)
                _, vjp = jax.vjp(core, _stack_heads(q, kh), kc[:, sl], vc[:, sl], ks[:, sl], vs[:, sl], _sink_rows(sink_v, kh))
                dq_st, *dkv, dsr = vjp(_stack_heads(do, kh))
            else:
                core = lambda qq, kk, vv, sr: _attn_core(qq, kk, vv, None, None, sr, None)
                _, vjp = jax.vjp(core, _stack_heads(q, kh), kc[:, sl], vc[:, sl], _sink_rows(sink_v, kh))
                dq_st, *dkv, dsr = vjp(_stack_heads(do, kh))
            for g in range(GROUP):
                h = GROUP * kh + g
                dq_ref[:, h * HEAD_DIM:(h + 1) * HEAD_DIM] = dq_st[g * Q_BLOCK:(g + 1) * Q_BLOCK, :]
                dsink = dsink + jnp.where(lane8 == h, jnp.sum(dsr[g * Q_BLOCK:(g + 1) * Q_BLOCK, :]), 0.0)
            parts.append(dkv)
        dkc, dvc, *dwin = [jnp.concatenate(t, axis=1) for t in zip(*parts)]
        dsink_ref[...] += dsink
        dkc_ref[...] += dkc
        dvc_ref[...] += dvc
        if local:
            dq_ref[...] = _rope_bwd(dq_ref[...], cq_ref[...], sq_ref[...])
            dk_ref[pl.ds(s0, SPAN), :] += _rope_bwd(dwin[0], ck, sk)
            dv_ref[pl.ds(s0, SPAN), :] += dwin[1]

    seq = lambda blk: pl.BlockSpec((None, n, KV_W), lambda b, i: (b, 0, blk))
    ctxs = lambda blk: pl.BlockSpec((None, n_ctx, KV_W), lambda b, i: (b, 0, blk))
    full = lambda a: pl.BlockSpec(a.shape, lambda b, i: (0,) * a.ndim)
    qblk = lambda blk: pl.BlockSpec((None, Q_BLOCK, ATTN_W), lambda b, i: (b, i, blk))
    cos_q, sin_q = jnp.tile(cos, (1, ATTN_HEADS)), jnp.tile(sin, (1, ATTN_HEADS))
    cos_k, sin_k = jnp.tile(cos, (1, KV_HEADS)), jnp.tile(sin, (1, KV_HEADS))
    acc = lambda rows: pl.BlockSpec((None, rows, KV_W), lambda b, i: (b, 0, 0))
    body, x_in, x_out, x_shapes, x_sems = _riding(body, exch, 11, 6, (B, nb))
    return pl.pallas_call(
        body, name=name, grid=(B, nb),
        in_specs=[qblk(0), seq(ATTN_W // KV_W), seq(ATTN_W // KV_W + 1), ctxs(0), ctxs(1), full(sink),
                  pl.BlockSpec((Q_BLOCK, ATTN_W), lambda b, i: (i, 0)), pl.BlockSpec((Q_BLOCK, ATTN_W), lambda b, i: (i, 0)),
                  full(cos_k), full(sin_k), qblk(do_blk)] + x_in,
        out_specs=[qblk(0), acc(n), acc(n), acc(n_ctx), acc(n_ctx), pl.BlockSpec((1, ATTN_HEADS), lambda b, i: (0, 0))] + x_out,
        out_shape=[jax.ShapeDtypeStruct((B, n, ATTN_W), F32), jax.ShapeDtypeStruct((B, n, KV_W), F32),
                   jax.ShapeDtypeStruct((B, n, KV_W), F32), jax.ShapeDtypeStruct((B, n_ctx, KV_W), F32),
                   jax.ShapeDtypeStruct((B, n_ctx, KV_W), F32), jax.ShapeDtypeStruct((1, ATTN_HEADS), F32)] + x_shapes,
        scratch_shapes=x_sems,
        compiler_params=_params("arbitrary", "arbitrary"),
    )(u, u, u, kvc, kvc, sink, cos_q, sin_q, cos_k, sin_k, do_src, *(exch[0] if exch else []))


def _conv_chunk(s, n, a_ext, g_ext, dw, dw_b, ln_g, ln_b):
    del s, n
    r = a_ext.shape[0] - 2 * HALO
    h = a_ext * jax.nn.sigmoid(g_ext)
    acc = jnp.broadcast_to(dw_b, (r, CONV_W))
    first = HALO - CONV_KERNEL // 2
    for k in range(CONV_KERNEL):
        acc = acc + h[first + k:first + k + r, :] * dw[k:k + 1, :]
    mu = jnp.mean(acc, axis=-1, keepdims=True)
    var = jnp.mean(jnp.square(acc - mu), axis=-1, keepdims=True)
    hn = (acc - mu) * lax.rsqrt(var + EPS) * ln_g + ln_b
    return hn * jax.nn.sigmoid(hn)


def _pool_chunk(s, n, p_ext, w_bd, scale):
    r = p_ext.shape[0] - 2 * HALO
    lane = lax.broadcasted_iota(jnp.int32, (1, POOL_W), 1)
    win = jnp.left_shift(2, lane // POOL_GROUP)
    half = win // 2
    acc = jnp.zeros((r, POOL_W), F32)
    for d in range(-(POOL_WINDOWS[-1] // 2), POOL_WINDOWS[-1] - POOL_WINDOWS[-1] // 2):
        inside = (d >= -half) & (d <= win - 1 - half)
        acc = acc + jnp.where(inside, p_ext[HALO + d:HALO + d + r, :], 0.0)
    t = s + lax.broadcasted_iota(jnp.int32, (r, 1), 0)
    lo = jnp.maximum(t - half, 0)
    hi = jnp.minimum(t + win - 1 - half, n - 1)
    y = acc / (hi - lo + 1).astype(F32) - p_ext[HALO:HALO + r, :]
    out = lax.dot_general(y.astype(BF16), w_bd.astype(BF16), (((1,), (0,)), ((), ())), preferred_element_type=F32)
    return out * scale


def _seq_specs(rows, params):
    specs = [pl.BlockSpec((None, a.shape[1], w), functools.partial(lambda b, blk: (b, 0, blk), blk=blk)) for a, w, blk in rows]
    specs += [pl.BlockSpec(p.shape, functools.partial(lambda b, nd: (0,) * nd, nd=p.ndim)) for p in params]
    return specs


def _fill_padded(pad_ref, row_ref, n):
    w = pad_ref.shape[1]
    pad_ref[pl.ds(0, HALO), :] = jnp.zeros((HALO, w), F32)
    pad_ref[pl.ds(HALO + n, HALO), :] = jnp.zeros((HALO, w), F32)
    pad_ref[pl.ds(HALO, n), :] = row_ref[...]


def _seq_fwd(fn, rows, params, out_w, *, name):
    B, n = rows[0][0].shape[:2]
    r = min(SEQ_CHUNK, n)
    nr, npar = len(rows), len(params)

    def body(*refs):
        row_refs, par_refs, o_ref, pads = refs[:nr], refs[nr:nr + npar], refs[nr + npar], refs[nr + npar + 1:]
        for rr, p in zip(row_refs, pads):
            _fill_padded(p, rr, n)
        pars = [p[...] for p in par_refs]

        def chunk(ci, carry):
            s = pl.multiple_of(ci * r, r)
            ext = [p[pl.ds(s, r + 2 * HALO), :] for p in pads]
            o_ref[pl.ds(s, r), :] = fn(s, n, *ext, *pars)
            return carry

        lax.fori_loop(0, n // r, chunk, 0)

    return pl.pallas_call(
        body, name=name, grid=(B,),
        in_specs=_seq_specs(rows, params),
        out_specs=pl.BlockSpec((None, n, out_w), lambda b: (b, 0, 0)),
        out_shape=jax.ShapeDtypeStruct((B, n, out_w), F32),
        scratch_shapes=[pltpu.VMEM((n + 2 * HALO, w), F32) for _, w, _ in rows],
        compiler_params=_params("parallel"),
    )(*[a for a, _, _ in rows], *params)


def _seq_bwd(fn, rows, params, dout, *, name):
    B, n = rows[0][0].shape[:2]
    r = min(SEQ_CHUNK, n)
    nr, npar = len(rows), len(params)

    def body(*refs):
        row_refs, par_refs, do_ref = refs[:nr], refs[nr:nr + npar], refs[nr + npar]
        outs = refs[nr + npar + 1:]
        drow_refs, dpar_refs = outs[:nr], outs[nr:nr + npar]
        scratch = outs[nr + npar:]
        pads, dpads = scratch[:nr], scratch[nr:]
        for rr, p, dp in zip(row_refs, pads, dpads):
            _fill_padded(p, rr, n)
            dp[...] = jnp.zeros_like(dp)

        @pl.when(pl.program_id(0) == 0)
        def _():
            for d in dpar_refs:
                d[...] = jnp.zeros_like(d)

        pars = [p[...] for p in par_refs]

        def chunk(ci, carry):
            s = pl.multiple_of(ci * r, r)
            ext = [p[pl.ds(s, r + 2 * HALO), :] for p in pads]
            _, vjp = jax.vjp(functools.partial(fn, s, n), *ext, *pars)
            grads = vjp(do_ref[pl.ds(s, r), :])
            for dp, g in zip(dpads, grads[:nr]):
                dp[pl.ds(s, r + 2 * HALO), :] += g
            for d, g in zip(dpar_refs, grads[nr:]):
                d[...] += g
            return carry

        lax.fori_loop(0, n // r, chunk, 0)
        for d, dp in zip(drow_refs, dpads):
            d[...] = dp[pl.ds(HALO, n), :]

    da, dw_, dblk = dout
    return pl.pallas_call(
        body, name=name, grid=(B,),
        in_specs=_seq_specs(rows, params) + [pl.BlockSpec((None, n, dw_), lambda b: (b, 0, dblk))],
        out_specs=[pl.BlockSpec((None, n, w), lambda b: (b, 0, 0)) for _, w, _ in rows]
        + [pl.BlockSpec(p.shape, functools.partial(lambda b, nd: (0,) * nd, nd=p.ndim)) for p in params],
        out_shape=[jax.ShapeDtypeStruct((B, n, w), F32) for _, w, _ in rows]
        + [jax.ShapeDtypeStruct(p.shape, F32) for p in params],
        scratch_shapes=[pltpu.VMEM((n + 2 * HALO, w), F32) for _, w, _ in rows] * 2,
        compiler_params=_params("arbitrary"),
    )(*[a for a, _, _ in rows], *params, da)


_CONV_A_BLK = (ATTN_W + 2 * KV_W) // CONV_W
_CONV_G_BLK = _CONV_A_BLK + 1
_POOL_BLK = _CONV_A_BLK + 2


def _mixer_fwd(tag, u, kvc, margs, local, exch=None):
    sink, dw, dw_b, ln_g, ln_b, w_bd, scale = margs
    cos, sin = _rope_tables(max(u.shape[1], GRID_W))
    attn, *got = _attn_fwd(u, kvc, sink, cos, sin, local=local, name=f"{tag}_attn_fwd", exch=exch)
    conv = _seq_fwd(_conv_chunk, [(u, CONV_W, _CONV_A_BLK), (u, CONV_W, _CONV_G_BLK)], [dw, dw_b, ln_g, ln_b], CONV_W,
                    name=f"{tag}_conv_fwd")
    pool = _seq_fwd(_pool_chunk, [(u, POOL_W, _POOL_BLK)], [w_bd, scale], POOL_W, name=f"{tag}_pool_fwd")
    return jnp.concatenate([attn, conv, pool], axis=-1).astype(BF16), got


def _mixer_bwd(tag, u, kvc, margs, dmix, local, exch=None):
    sink, dw, dw_b, ln_g, ln_b, w_bd, scale = margs
    cos, sin = _rope_tables(max(u.shape[1], GRID_W))
    dq, dk, dv, dkc, dvc, dsink, *got = _attn_bwd(u, kvc, sink, cos, sin, dmix, 0, local=local, name=f"{tag}_attn_bwd",
                                                  exch=exch)
    da, dg, ddw, ddw_b, dln_g, dln_b = _seq_bwd(
        _conv_chunk, [(u, CONV_W, _CONV_A_BLK), (u, CONV_W, _CONV_G_BLK)], [dw, dw_b, ln_g, ln_b],
        (dmix, CONV_W, ATTN_W // CONV_W), name=f"{tag}_conv_bwd")
    dpu, dw_bd, dscale = _seq_bwd(_pool_chunk, [(u, POOL_W, _POOL_BLK)], [w_bd, scale],
                                  (dmix, POOL_W, (ATTN_W + CONV_W) // POOL_W), name=f"{tag}_pool_bwd")
    return (dq, dk, dv, da, dg, dpu), (dkc, dvc), (dsink, ddw, ddw_b, dln_g, dln_b, dw_bd, dscale), got


def _row_specs(arrs, kinds, tr):
    specs = []
    for a, kind in zip(arrs, kinds):
        if kind == "row":
            specs.append(pl.BlockSpec((None, tr, a.shape[2]), lambda b, j: (b, j, 0)))
        elif kind == "batch":
            specs.append(pl.BlockSpec((None, 1, a.shape[2]), lambda b, j: (b, 0, 0)))
        else:
            specs.append(pl.BlockSpec(a.shape, functools.partial(lambda b, j, nd: (0,) * nd, nd=a.ndim)))
    return specs


def _rowwise_fwd(fn, ins, kinds, outs, tr, *, name, transposed=None):
    B, n = ins[0].shape[:2]
    ni, no = len(ins), len(outs)
    nj = n // tr

    def body(*refs):
        res = fn(*[r[...] for r in refs[:ni]])
        for o, v in zip(refs[ni:ni + no], res):
            o[...] = v.astype(o.dtype)
        if transposed is not None:
            refs[ni + no][...] = res[transposed].T.astype(refs[ni + no].dtype)

    out_specs = [pl.BlockSpec((None, tr, w), lambda b, j: (b, j, 0)) for w, _ in outs]
    out_shape = [jax.ShapeDtypeStruct((B, n, w), dt) for w, dt in outs]
    if transposed is not None:
        w, dt = outs[transposed]
        out_specs.append(pl.BlockSpec((w, tr), lambda b, j: (0, b * nj + j)))
        out_shape.append(jax.ShapeDtypeStruct((w, B * n), dt))
    return pl.pallas_call(
        body, name=name, grid=(B, nj),
        in_specs=_row_specs(ins, kinds, tr), out_specs=out_specs, out_shape=out_shape,
        compiler_params=_params("parallel", "parallel"),
    )(*ins)


def _rowwise_bwd(fn, ins, kinds, gdtypes, cts, tr, *, name):
    B, n = ins[0].shape[:2]
    ni, nc = len(ins), len(cts)
    idx = list(range(ni))

    def body(*refs):
        in_refs, ct_refs, out_refs = refs[:ni], refs[ni:ni + nc], refs[ni + nc:]
        b, j = pl.program_id(0), pl.program_id(1)
        _, vjp = jax.vjp(fn, *[r[...].astype(F32) for r in in_refs])
        grads = vjp(tuple(c[...].astype(F32) for c in ct_refs))
        for o, i in zip(out_refs, idx):
            g = grads[i]
            if kinds[i] == "row":
                o[...] = g.astype(o.dtype)
            else:
                first = (j == 0) if kinds[i] == "batch" else ((j == 0) & (b == 0))

                @pl.when(first)
                def _(o=o, g=g):
                    o[...] = g

                @pl.when(jnp.logical_not(first))
                def _(o=o, g=g):
                    o[...] += g

    specs = _row_specs(ins, kinds, tr)
    return pl.pallas_call(
        body, name=name, grid=(B, n // tr),
        in_specs=specs + [pl.BlockSpec((None, tr, c.shape[2]), lambda b, j: (b, j, 0)) for c in cts],
        out_specs=[specs[i] for i in idx],
        out_shape=[jax.ShapeDtypeStruct(ins[i].shape, gdtypes[i]) for i in idx],
        compiler_params=_params("arbitrary", "arbitrary"),
    )(*ins, *cts)


ROW_TILE = 256


def _rms_mod(x, g, sc, sh):
    y = x * lax.rsqrt(jnp.mean(x * x, axis=-1, keepdims=True) + EPS)
    return (y * g) * (1.0 + sc) + sh


def _norm_tile(x, g, sc, sh):
    return x, _rms_mod(x, g, sc, sh)


def _res_norm_tile(xb, y, gate, g, sc, sh):
    x = xb + gate * y
    return x, _rms_mod(x, g, sc, sh)


_NORM_KINDS = ("row", "glob", "batch", "batch")
_RES_NORM_KINDS = ("row", "row", "batch", "glob", "batch", "batch")


def _norm_fwd(tag, st, g, sc, sh):
    xb, y, gate = st
    tr = min(ROW_TILE, xb.shape[1])
    d = xb.shape[2]
    if y is None:
        h, h_t = _rowwise_fwd(lambda *a: (_rms_mod(*a),), [xb, g, sc, sh], _NORM_KINDS, [(d, BF16)], tr,
                              name=f"{tag}_fwd", transposed=0)
        return xb, h, h_t
    return _rowwise_fwd(_res_norm_tile, [xb, y, gate, g, sc, sh], _RES_NORM_KINDS, [(d, F32), (d, BF16)], tr,
                        name=f"{tag}_fwd", transposed=1)


def _norm_bwd(tag, st, g, sc, sh, dx, dh):
    xb, y, gate = st
    tr = min(ROW_TILE, xb.shape[1])
    if y is None:
        dxb, dg, dsc, dsh = _rowwise_bwd(_norm_tile, [xb, g, sc, sh], _NORM_KINDS, [F32] * 4, [dx, dh], tr, name=f"{tag}_bwd")
        return dxb, None, None, dg, dsc, dsh
    return tuple(_rowwise_bwd(_res_norm_tile, [xb, y, gate, g, sc, sh], _RES_NORM_KINDS, [F32, BF16, F32, F32, F32, F32],
                              [dx, dh], tr, name=f"{tag}_bwd"))


def _loss_head(st, final_g, target, *, name):
    xb, y, gate = st
    B, n, d = xb.shape
    tr = min(ROW_TILE, n)

    def tile_loss(xv, yv, gt, g, t):
        x = xv + gt * yv
        out = x * lax.rsqrt(jnp.mean(x * x, axis=-1, keepdims=True) + EPS) * g
        return 0.5 * jnp.sum(jnp.mean(jnp.square(out - t), axis=-1))

    def body(x_ref, y_ref, gate_ref, g_ref, t_ref, loss_ref, dx_ref, dy_ref, dgate_ref, dg_ref):
        b, j = pl.program_id(0), pl.program_id(1)
        val, (dx, dy, dgate, dg) = jax.value_and_grad(tile_loss, argnums=(0, 1, 2, 3))(
            x_ref[...], y_ref[...], gate_ref[...], g_ref[...], t_ref[...])
        dx_ref[...] = dx
        dy_ref[...] = dy.astype(dy_ref.dtype)

        @pl.when(j == 0)
        def _():
            loss_ref[...] = jnp.zeros_like(loss_ref)
            dgate_ref[...] = jnp.zeros_like(dgate_ref)

        @pl.when((j == 0) & (b == 0))
        def _():
            dg_ref[...] = jnp.zeros_like(dg_ref)

        loss_ref[...] += jnp.full(loss_ref.shape, val, F32)
        dgate_ref[...] += dgate
        dg_ref[...] += dg

    row = pl.BlockSpec((None, tr, d), lambda b, j: (b, j, 0))
    per_sample = pl.BlockSpec((None, 1, d), lambda b, j: (b, 0, 0))
    whole = pl.BlockSpec((1, d), lambda b, j: (0, 0))
    return pl.pallas_call(
        body, name=name, grid=(B, n // tr),
        in_specs=[row, row, per_sample, whole, row],
        out_specs=[pl.BlockSpec((None, 1, 128), lambda b, j: (b, 0, 0)), row, row, per_sample, whole],
        out_shape=[jax.ShapeDtypeStruct((B, 1, 128), F32), jax.ShapeDtypeStruct((B, n, d), F32),
                   jax.ShapeDtypeStruct((B, n, d), BF16), jax.ShapeDtypeStruct((B, 1, d), F32), jax.ShapeDtypeStruct((1, d), F32)],
        compiler_params=_params("arbitrary", "arbitrary"),
    )(xb, y, gate, final_g, target)


def _exchange(arrs, *, scatter, name):
    k = len(arrs)

    def body(*refs):
        ins, outs, sems = refs[:k], refs[k:2 * k], refs[2 * k:]
        _exch_start(ins, outs, sems, scatter)
        _exch_wait(ins, outs, sems, scatter)

    any_spec = pl.BlockSpec(memory_space=pl.ANY)
    return pl.pallas_call(
        body, name=name,
        in_specs=[any_spec] * k, out_specs=[any_spec] * k,
        out_shape=_exch_out_shapes(arrs, scatter), scratch_shapes=_exch_sems(k),
        compiler_params=pltpu.CompilerParams(has_side_effects=True),
    )(*arrs)


def _exch_out_shapes(arrs, scatter):
    return [jax.ShapeDtypeStruct(a.shape if scatter else (N_DEV,) + a.shape, a.dtype) for a in arrs]


def _exch_sems(k):
    return [pltpu.SemaphoreType.DMA((k * (N_DEV - 1),)), pltpu.SemaphoreType.DMA((k * (N_DEV - 1),)),
            pltpu.SemaphoreType.DMA((k,))]


def _exch_copies(ins, outs, sems, scatter):
    send_sems, recv_sems, local_sems = sems
    x, y, c = lax.axis_index("x"), lax.axis_index("y"), lax.axis_index("c")
    me = 4 * x + 2 * y + c
    owns, sends, recvs = [], [], []
    for a in range(len(ins)):
        owns.append(pltpu.make_async_copy(ins[a].at[me] if scatter else ins[a], outs[a].at[me], local_sems.at[a]))
        for r in range(1, N_DEV):
            fx, fy, fc = (r >> 2) & 1, (r >> 1) & 1, r & 1
            px, py, pc = (x + fx) % 2, (y + fy) % 2, (c + fc) % 2
            peer = 4 * px + 2 * py + pc
            s = a * (N_DEV - 1) + r - 1
            mk = functools.partial(pltpu.make_async_remote_copy, src_ref=ins[a].at[peer] if scatter else ins[a],
                                   send_sem=send_sems.at[s], recv_sem=recv_sems.at[s],
                                   device_id=(px, py, pc), device_id_type=pl.DeviceIdType.MESH)
            sends.append(mk(dst_ref=outs[a].at[me]))
            recvs.append(mk(dst_ref=outs[a].at[peer]))
    return owns, sends, recvs


def _exch_start(ins, outs, sems, scatter):
    owns, sends, _ = _exch_copies(ins, outs, sems, scatter)
    for cp in owns + sends:
        cp.start()


def _exch_wait(ins, outs, sems, scatter):
    owns, sends, recvs = _exch_copies(ins, outs, sems, scatter)
    for rc in recvs:
        rc.wait_recv()
    for cp in sends:
        cp.wait_send()
    for own in owns:
        own.wait()


MOD_ROWS = 48


def _mod_tile(cc, w, b):
    s = cc * jax.nn.sigmoid(cc)
    return lax.dot_general(s.astype(BF16), w.astype(BF16), (((1,), (0,)), ((), ())), preferred_element_type=F32) + b


def _mod_fwd(cc, w_mod, b_shard, *, name):
    L, d, wcols = w_mod.shape

    def body(cc_ref, w_ref, b_ref, o_ref):
        o_ref[...] = _mod_tile(cc_ref[...], w_ref[...], b_ref[...])

    return pl.pallas_call(
        body, name=name, grid=(L,),
        in_specs=[pl.BlockSpec((MOD_ROWS, d), lambda l: (0, 0)), pl.BlockSpec((None, d, wcols), lambda l: (l, 0, 0)),
                  pl.BlockSpec((None, 1, wcols), lambda l: (l, 0, 0))],
        out_specs=pl.BlockSpec((None, MOD_ROWS, wcols), lambda l: (l, 0, 0)),
        out_shape=jax.ShapeDtypeStruct((L, MOD_ROWS, wcols), F32),
        compiler_params=_params("parallel"),
    )(cc, w_mod, b_shard)


def _mod_bwd(cc, w_mod, b_shard, dm, *, name):
    L, d, wcols = w_mod.shape

    def body(cc_ref, w_ref, b_ref, dm_ref, dcc_ref, dw_ref):
        _, vjp = jax.vjp(_mod_tile, cc_ref[...], w_ref[...], b_ref[...])
        dcc, dw, _ = vjp(dm_ref[...])
        dw_ref[...] = dw

        @pl.when(pl.program_id(0) == 0)
        def _():
            dcc_ref[...] = dcc

        @pl.when(pl.program_id(0) > 0)
        def _():
            dcc_ref[...] += dcc

    return pl.pallas_call(
        body, name=name, grid=(L,),
        in_specs=[pl.BlockSpec((MOD_ROWS, d), lambda l: (0, 0)), pl.BlockSpec((None, d, wcols), lambda l: (l, 0, 0)),
                  pl.BlockSpec((None, 1, wcols), lambda l: (l, 0, 0)), pl.BlockSpec((None, MOD_ROWS, wcols), lambda l: (l, 0, 0))],
        out_specs=[pl.BlockSpec((MOD_ROWS, d), lambda l: (0, 0)), pl.BlockSpec((None, d, wcols), lambda l: (l, 0, 0))],
        out_shape=[jax.ShapeDtypeStruct((MOD_ROWS, d), F32), jax.ShapeDtypeStruct((L, d, wcols), F32)],
        compiler_params=_params("arbitrary"),
    )(cc, w_mod, b_shard, dm)


def _sum_leading(a, *, name):
    K, R, C = a.shape
    tr = _tile8(R, 256)

    def body(a_ref, o_ref):
        acc = a_ref[0].astype(F32)
        for i in range(1, K):
            acc = acc + a_ref[i].astype(F32)
        o_ref[...] = acc

    return pl.pallas_call(
        body, name=name, grid=(R // tr,),
        in_specs=[pl.BlockSpec((K, tr, C), lambda i: (0, i, 0))],
        out_specs=pl.BlockSpec((tr, C), lambda i: (i, 0)),
        out_shape=jax.ShapeDtypeStruct((R, C), F32),
        compiler_params=_params("parallel"),
    )(a)


def _tile8(dim, target):
    if dim <= target:
        return dim
    t = (target // 8) * 8
    while t >= 8:
        if dim % t == 0:
            return t
        t -= 8
    raise ValueError(f"no row tile for {dim}")


def _adamw_math(g, w, m, v):
    m = ADAM_B1 * m + (1.0 - ADAM_B1) * g
    v = ADAM_B2 * v + (1.0 - ADAM_B2) * jnp.square(g)
    m_hat = m / (1.0 - ADAM_B1 ** ADAM_STEP)
    v_hat = v / (1.0 - ADAM_B2 ** ADAM_STEP)
    delta = -ADAM_LR * (m_hat / (jnp.sqrt(v_hat) + ADAM_EPS) + ADAM_WD * w)
    return delta, m, v


def _adamw(g, w, m, v, *, name):
    R, C = w.shape
    parts = g.ndim == 3
    tr = _tile8(R, 256)

    def body(g_ref, w_ref, m_ref, v_ref, go_ref, d_ref, mo_ref, vo_ref):
        if parts:
            gv = g_ref[0].astype(F32)
            for i in range(1, g_ref.shape[0]):
                gv = gv + g_ref[i].astype(F32)
        else:
            gv = g_ref[...]
        go_ref[...] = gv
        d_ref[...], mo_ref[...], vo_ref[...] = _adamw_math(gv, w_ref[...], m_ref[...], v_ref[...])

    tile = pl.BlockSpec((tr, C), lambda i: (i, 0))
    g_spec = pl.BlockSpec((g.shape[0], tr, C), lambda i: (0, i, 0)) if parts else tile
    return pl.pallas_call(
        body, name=name, grid=(R // tr,),
        in_specs=[g_spec, tile, tile, tile], out_specs=[tile] * 4,
        out_shape=[jax.ShapeDtypeStruct((R, C), F32)] * 4,
        compiler_params=_params("parallel"),
    )(g, w, m, v)


def _block_diag(w):
    g = w.shape[0]
    rows = [jnp.concatenate([w[i] if j == i else jnp.zeros_like(w[i]) for j in range(g)], axis=1) for i in range(g)]
    return jnp.concatenate(rows, axis=0)


def _diag_blocks(w_bd):
    g = POOL_W // POOL_GROUP
    return jnp.stack([w_bd[i * POOL_GROUP:(i + 1) * POOL_GROUP, i * POOL_GROUP:(i + 1) * POOL_GROUP] for i in range(g)])


def _flat(a):
    return a.reshape(-1, a.shape[-1])


def _mix_half_fwd(tag, st, mods, wl, kvc, *, local, kv_only, exch=None):
    sh1, sc1, g1 = mods[:3]
    B, n, d = st[0].shape
    x, h, h_t = _norm_fwd(f"{tag}_norm1", st, wl["n1"], sc1, sh1)
    if kv_only:
        kv = _mm(_flat(h), wl["w_in"][:, ATTN_W:ATTN_W + 2 * KV_W], name=f"{tag}_kv").reshape(B, n, 2 * KV_W)
        return None, dict(st=st, h_t=h_t, kvc=kv), []
    u = _mm(_flat(h), wl["w_in"], name=f"{tag}_in").reshape(B, n, IN_W)
    if not local:
        kvc = u[:, :, ATTN_W:ATTN_W + 2 * KV_W]
    mix, got = _mixer_fwd(f"{tag}_mix", u, kvc, wl["margs"], local, exch)
    y = _mm(_flat(mix), wl["w_out"], name=f"{tag}_out").reshape(B, n, d)
    return (x, y, g1), dict(st=st, h_t=h_t, u=u, kvc=kvc, mix=mix), got


def _ffn_half_fwd(tag, st2, mods, wl, exch=None):
    sh2, sc2, g2 = mods[3:]
    B, n, d = st2[0].shape
    x1, h2, h2_t = _norm_fwd(f"{tag}_norm2", st2, wl["n2"], sc2, sh2)
    gu, act, act_t, *got = _mm_swiglu(_flat(h2), wl["w_ffn_in"], name=f"{tag}_ffn_in", exch=exch)
    y2 = _mm(act, wl["w_ffn_out"], name=f"{tag}_ffn_out").reshape(B, n, d)
    return (x1, y2, g2), dict(st2=st2, h2_t=h2_t, gu=gu, act_t=act_t), got


def _ffn_half_bwd(tag, sv, mods, wl, dx1, dy2):
    sh2, sc2, _ = mods[3:]
    B, n, d = sv["st2"][0].shape
    gw = {}
    dy2f = _flat(dy2)
    gw["w_ffn_out"] = _mm(sv["act_t"], dy2f, out_dtype=BF16, name=f"{tag}_ffn_out_dw")
    dgu = _mm_dswiglu(dy2f, wl["w_ffn_out_t"], sv["gu"], name=f"{tag}_ffn_out_dx")
    dh2 = _mm(dgu, wl["w_ffn_in_t"], name=f"{tag}_ffn_in_dx").reshape(B, n, d)
    gw["w_ffn_in"] = _deinterleave_ffn(_mm(sv["h2_t"], dgu, out_dtype=BF16, name=f"{tag}_ffn_in_dw"))
    dx, dy, dg1, gw["n2"], dsc2, dsh2 = _norm_bwd(f"{tag}_norm2", sv["st2"], wl["n2"], sc2, sh2, dx1, dh2)
    return (dx, dy, dg1), gw, dict(sh2=dsh2, sc2=dsc2)


def _mix_half_bwd(tag, sv, mods, wl, dx, dy, dkv_in, *, local, kv_only, exch=None):
    sh1, sc1, _ = mods[:3]
    B, n, d = sv["st"][0].shape
    gw = {}
    if kv_only:
        dkv = _flat(dkv_in).astype(BF16)
        dh = _mm(dkv, wl["w_in_t"][ATTN_W:ATTN_W + 2 * KV_W, :], name=f"{tag}_kv_dx").reshape(B, n, d)
        gw["w_in_kv"] = _mm(sv["h_t"], dkv, out_dtype=BF16, name=f"{tag}_kv_dw")
        dxb, dy_prev, dgate_prev, gw["n1"], dsc1, dsh1 = _norm_bwd(f"{tag}_norm1", sv["st"], wl["n1"], sc1, sh1,
                                                                    jnp.zeros((B, n, d), F32), dh)
        return (dxb, dy_prev, dgate_prev), gw, dict(sh1=dsh1, sc1=dsc1), None, []

    dyf = _flat(dy)
    dmix = _mm(dyf, wl["w_out_t"], name=f"{tag}_out_dx").reshape(B, n, d)
    gw["w_out"] = _mm(_flat(sv["mix"]).T, dyf, out_dtype=BF16, name=f"{tag}_out_dw")
    (dq, dk, dv, da, dg, dpu), (dkc, dvc), gw["margs"], got = _mixer_bwd(f"{tag}_mix", sv["u"], sv["kvc"], wl["margs"], dmix,
                                                                     local, exch)
    if local:
        dkv_out = jnp.concatenate([dkc, dvc], axis=-1)
    else:
        dk = dkc + dkv_in[:, :, :KV_W]
        dv = dvc + dkv_in[:, :, KV_W:]
        dkv_out = None
    du = _flat(jnp.concatenate([dq, dk, dv, da, dg, dpu], axis=-1).astype(BF16))
    dh = _mm(du, wl["w_in_t"], name=f"{tag}_in_dx").reshape(B, n, d)
    gw["w_in"] = _mm(sv["h_t"], du, out_dtype=BF16, name=f"{tag}_in_dw")
    dxb, dy_prev, dgate_prev, gw["n1"], dsc1, dsh1 = _norm_bwd(f"{tag}_norm1", sv["st"], wl["n1"], sc1, sh1, dx, dh)
    return (dxb, dy_prev, dgate_prev), gw, dict(sh1=dsh1, sc1=dsc1), dkv_out, got


BIG_W = ("w_in", "w_out", "w_ffn_in", "w_ffn_out")


def _local_step(x, ctx, m_loc, m_ctx, p, final_g, target, big):
    B = x.shape[0]
    depth = m_loc.shape[0]
    lat_mods = [[t[:, None, :] for t in jnp.split(m_loc[l], 6, axis=-1)] for l in range(depth)]
    ctx_mods = [[jnp.broadcast_to(t[None, None, :], (B, 1, D_MODEL)) for t in jnp.split(m_ctx[l], 6)] for l in range(depth)]

    st, cst = (x, None, None), (ctx, None, None)
    w_mix, w_ffn, sv_mix, sv_ffn, csv_mix, csv_ffn = [], [], [], [], [], []
    got = []
    for l in range(depth):
        last = l == depth - 1
        wb = big.mix_weights(l, got)
        wm = dict(n1=p["norm1_g"][l][None, :], w_in=wb["w_in"], w_out=wb["w_out"], w_in_t=wb["w_in"].T, w_out_t=wb["w_out"].T,
                  margs=(p["attn_sink"][l][None, :], p["conv_dw"][l], p["conv_dw_b"][l][None, :], p["conv_ln_g"][l][None, :],
                         p["conv_ln_b"][l][None, :], _block_diag(p["pool_w"][l]), p["pool_scale"][l][None, :]))
        cst, csv, _ = _mix_half_fwd(f"l{l}c", cst, ctx_mods[l], wm, None, local=False, kv_only=last)
        st, sv, got = _mix_half_fwd(f"l{l}", st, lat_mods[l], wm, csv["kvc"], local=True, kv_only=False,
                                    exch=big.ride_attn_fwd(l))
        w_mix.append(wm)
        sv_mix.append(sv)
        csv_mix.append(csv)
        wb = big.ffn_weights(l, got)
        w_ffn_in = _interleave_ffn(wb["w_ffn_in"])
        wf = dict(n2=p["norm2_g"][l][None, :], w_ffn_in=w_ffn_in, w_ffn_out=wb["w_ffn_out"],
                  w_ffn_in_t=w_ffn_in.T, w_ffn_out_t=wb["w_ffn_out"].T)
        csv = None
        if not last:
            cst, csv, _ = _ffn_half_fwd(f"l{l}c", cst, ctx_mods[l], wf)
        st, sv, got = _ffn_half_fwd(f"l{l}", st, lat_mods[l], wf, exch=big.ride_ffn_fwd(l))
        w_ffn.append(wf)
        sv_ffn.append(sv)
        csv_ffn.append(csv)
    loss_rows, dx, dy, dgate, dfinal = _loss_head(st, final_g[None, :], target, name="loss_head")

    dm_loc, dm_ctx = [None] * depth, [None] * depth
    small = [None] * depth
    cdx = cdy = cdgate = None
    up_mix = None
    for l in reversed(range(depth)):
        last = l == depth - 1
        dm, cdm = dict(g2=dgate), {}
        (dx, dy, dm["g1"]), gf, d = _ffn_half_bwd(f"l{l}", sv_ffn[l], lat_mods[l], w_ffn[l], dx, dy)
        dm.update(d)
        if not last:
            cdm["g2"] = cdgate
            (cdx, cdy, cdm["g1"]), cgf, d = _ffn_half_bwd(f"l{l}c", csv_ffn[l], ctx_mods[l], w_ffn[l], cdx, cdy)
            cdm.update(d)
            gf = {k: gf[k] + cgf[k] for k in gf}
        ffn_grads = {k: gf[k] for k in _ShardedWeights.FFN}
        (dx, dy, dgate), gm, d, dkv, got = _mix_half_bwd(f"l{l}", sv_mix[l], lat_mods[l], w_mix[l], dx, dy, None, local=True,
                                                        kv_only=False, exch=big.ride_attn_bwd(l, ffn_grads, up_mix))
        big.took(l, ffn_grads, up_mix, got)
        dm.update(d)
        (cdx, cdy, cdgate), cgm, d, _, _ = _mix_half_bwd(f"l{l}c", csv_mix[l], ctx_mods[l], w_mix[l], cdx, cdy, dkv,
                                                        local=False, kv_only=last)
        cdm.update(d)
        order = ("sh1", "sc1", "g1", "sh2", "sc2", "g2")
        dm_loc[l] = jnp.concatenate([dm[k][:, 0, :] for k in order], axis=-1)
        dm_ctx[l] = jnp.concatenate([jnp.sum(cdm[k][:, 0, :], axis=0) if k in cdm else jnp.zeros((D_MODEL,), F32)
                                     for k in order])
        if last:
            up_mix = dict(w_in=gm["w_in"].at[:, ATTN_W:ATTN_W + 2 * KV_W].add(cgm["w_in_kv"]), w_out=gm["w_out"])
            margs = gm["margs"]
        else:
            up_mix = {k: gm[k] + cgm[k] for k in _ShardedWeights.MIX}
            margs = tuple(a + b for a, b in zip(gm["margs"], cgm["margs"]))
        small[l] = dict(n1=gm["n1"] + cgm["n1"], n2=gf["n2"], margs=margs)
    big.leftover(up_mix)

    stack = lambda f: jnp.stack([f(small[l]) for l in range(depth)])
    dp = dict(
        norm1_g=stack(lambda g: g["n1"][0]), norm2_g=stack(lambda g: g["n2"][0]),
        attn_sink=stack(lambda g: g["margs"][0][0]), conv_dw=stack(lambda g: g["margs"][1]),
        conv_dw_b=stack(lambda g: g["margs"][2][0]), conv_ln_g=stack(lambda g: g["margs"][3][0]),
        conv_ln_b=stack(lambda g: g["margs"][4][0]), pool_w=stack(lambda g: _diag_blocks(g["margs"][5])),
        pool_scale=stack(lambda g: g["margs"][6][0]))
    return jnp.sum(loss_rows[:, 0, 0]), dx, jnp.stack(dm_loc), jnp.stack(dm_ctx), dp, dfinal[0]


PACK_COLS = 1024


def _pack(arrs):
    flat = jnp.concatenate([a.reshape(-1).astype(F32) for a in arrs])
    rows = -(-flat.shape[0] // (8 * PACK_COLS)) * 8
    return jnp.pad(flat, (0, rows * PACK_COLS - flat.shape[0])).reshape(rows, PACK_COLS)


def _unpack(slab, like):
    flat = slab.reshape(-1)
    out, off = [], 0
    for a in like:
        out.append(flat[off:off + a.size].reshape(a.shape))
        off += a.size
    return out


def _shard_cols(gathered):
    _, L, R, C = gathered.shape
    return jnp.transpose(gathered, (1, 2, 0, 3)).reshape(L, R, N_DEV * C)


class _ShardedWeights:
    MIX = ("w_in", "w_out")
    FFN = ("w_ffn_in", "w_ffn_out")
    BY_COLS = ("w_in", "w_ffn_in")

    def __init__(self, shards, first):
        self.shards = shards
        self.first = first
        self.depth = shards[BIG_W[0]].shape[0]
        self.parts = [dict() for _ in range(self.depth)]
        self.left = None

    def _join(self, names, blocks):
        out = {}
        for name, g in zip(names, blocks):
            _, R, C = g.shape
            out[name] = jnp.transpose(g, (1, 0, 2)).reshape(R, N_DEV * C) if name in self.BY_COLS else g.reshape(N_DEV * R, C)
        return out

    def cut(self, names, grads):
        out = []
        for name in names:
            g = grads[name]
            if name in self.BY_COLS:
                R, C8 = g.shape
                out.append(jnp.transpose(g.reshape(R, N_DEV, C8 // N_DEV), (1, 0, 2)))
            else:
                R8, C = g.shape
                out.append(g.reshape(N_DEV, R8 // N_DEV, C))
        return out

    def mix_weights(self, l, got):
        return self._join(self.MIX, self.first if l == 0 else got)

    def ffn_weights(self, l, got):
        return self._join(self.FFN, got)

    def ride_attn_fwd(self, l):
        return [self.shards[name][l] for name in self.FFN], False

    def ride_ffn_fwd(self, l):
        if l + 1 >= self.depth:
            return None
        return [self.shards[name][l + 1] for name in self.MIX], False

    def ride_attn_bwd(self, l, ffn_grads, up_mix):
        return self.cut(self.FFN, ffn_grads) + (self.cut(self.MIX, up_mix) if up_mix is not None else []), True

    def took(self, l, ffn_grads, up_mix, got):
        self.parts[l].update(zip(self.FFN, got[:2]))
        if up_mix is not None:
            self.parts[l + 1].update(zip(self.MIX, got[2:]))

    def leftover(self, mix_grads):
        self.left = mix_grads


def _as_rows(a, leading=0):
    return a.reshape(*a.shape[:leading], -1, PACK_COLS)


SMALL = ("c_ctx", "b_mod", "norm1_g", "norm2_g", "conv_dw_b", "conv_ln_g", "conv_ln_b", "attn_sink", "pool_w",
         "pool_scale", "final_g", "conv_dw")
BIG = ("w_mod", "w_in", "w_out", "w_ffn_in", "w_ffn_out")
ORDER = ("c_ctx", "w_mod", "b_mod", "norm1_g", "norm2_g", "w_in", "conv_dw", "conv_dw_b", "conv_ln_g", "conv_ln_b",
         "attn_sink", "pool_w", "pool_scale", "w_out", "w_ffn_in", "w_ffn_out", "final_g")


def kernel(x, c, ctx, c_ctx, w_mod, b_mod, norm1_g, norm2_g, w_in, conv_dw, conv_dw_b, conv_ln_g, conv_ln_b, attn_sink, pool_w, pool_scale, w_out, w_ffn_in, w_ffn_out, final_g, loss_target, m_c_ctx, m_w_mod, m_b_mod, m_norm1_g, m_norm2_g, m_w_in, m_conv_dw, m_conv_dw_b, m_conv_ln_g, m_conv_ln_b, m_attn_sink, m_pool_w, m_pool_scale, m_w_out, m_w_ffn_in, m_w_ffn_out, m_final_g, v_c_ctx, v_w_mod, v_b_mod, v_norm1_g, v_norm2_g, v_w_in, v_conv_dw, v_conv_dw_b, v_conv_ln_g, v_conv_ln_b, v_attn_sink, v_pool_w, v_pool_scale, v_w_out, v_w_ffn_in, v_w_ffn_out, v_final_g):
    w = dict(c_ctx=c_ctx, w_mod=w_mod, b_mod=b_mod, norm1_g=norm1_g, norm2_g=norm2_g, w_in=w_in, conv_dw=conv_dw,
             conv_dw_b=conv_dw_b, conv_ln_g=conv_ln_g, conv_ln_b=conv_ln_b, attn_sink=attn_sink, pool_w=pool_w,
             pool_scale=pool_scale, w_out=w_out, w_ffn_in=w_ffn_in, w_ffn_out=w_ffn_out, final_g=final_g)
    mom = dict(c_ctx=m_c_ctx, w_mod=m_w_mod, b_mod=m_b_mod, norm1_g=m_norm1_g, norm2_g=m_norm2_g, w_in=m_w_in,
               conv_dw=m_conv_dw, conv_dw_b=m_conv_dw_b, conv_ln_g=m_conv_ln_g, conv_ln_b=m_conv_ln_b,
               attn_sink=m_attn_sink, pool_w=m_pool_w, pool_scale=m_pool_scale, w_out=m_w_out, w_ffn_in=m_w_ffn_in,
               w_ffn_out=m_w_ffn_out, final_g=m_final_g)
    var = dict(c_ctx=v_c_ctx, w_mod=v_w_mod, b_mod=v_b_mod, norm1_g=v_norm1_g, norm2_g=v_norm2_g, w_in=v_w_in,
               conv_dw=v_conv_dw, conv_dw_b=v_conv_dw_b, conv_ln_g=v_conv_ln_g, conv_ln_b=v_conv_ln_b,
               attn_sink=v_attn_sink, pool_w=v_pool_w, pool_scale=v_pool_scale, w_out=v_w_out, w_ffn_in=v_w_ffn_in,
               w_ffn_out=v_w_ffn_out, final_g=v_final_g)
    B = x.shape[0]
    depth = w_mod.shape[0]
    mod_cols = w_mod.shape[2]
    dw_cols = conv_dw.shape[2]
    me = 4 * lax.axis_index("x") + 2 * lax.axis_index("y") + lax.axis_index("c")

    shards = {name: w[name].astype(BF16) for name in BIG_W}
    c_all, dw_all, *first = _exchange([c, conv_dw] + [shards[name][0] for name in _ShardedWeights.MIX], scatter=False,
                                      name="gather_first")
    big = _ShardedWeights(shards, first)
    p = dict(norm1_g=norm1_g, norm2_g=norm2_g, conv_dw=_shard_cols(dw_all), conv_dw_b=conv_dw_b, conv_ln_g=conv_ln_g,
             conv_ln_b=conv_ln_b, attn_sink=attn_sink, pool_w=pool_w, pool_scale=pool_scale)

    cc = jnp.concatenate([c_all.reshape(N_DEV * B, D_MODEL), jnp.broadcast_to(c_ctx[None, :], (N_DEV, D_MODEL)),
                          jnp.zeros((MOD_ROWS - N_DEV * B - N_DEV, D_MODEL), F32)], axis=0)
    b_shard = lax.dynamic_slice_in_dim(b_mod, me * mod_cols, mod_cols, axis=1)[:, None, :]
    m_part = _mod_fwd(cc, w_mod, b_shard, name="mod_fwd")
    m_all, = _exchange([m_part], scatter=False, name="gather_mod")
    m_full = _shard_cols(m_all)
    m_loc = lax.dynamic_slice_in_dim(m_full, me * B, B, axis=1)
    m_ctx = m_full[:, N_DEV * B, :]

    loss_part, dx, dm_loc, dm_ctx, dp, dfinal = _local_step(x, ctx, m_loc, m_ctx, p, final_g, loss_target, big)
    loss = lax.psum(loss_part, AXES)

    dm_rows = jnp.concatenate([dm_loc, dm_ctx[:, None, :], jnp.zeros((depth, 8 - B - 1, 6 * D_MODEL), F32)], axis=1)
    dm_all, = _exchange([dm_rows], scatter=False, name="gather_dmod")
    dm_full = jnp.concatenate([
        jnp.transpose(dm_all[:, :, :B, :], (1, 0, 2, 3)).reshape(depth, N_DEV * B, 6 * D_MODEL),
        jnp.transpose(dm_all[:, :, B, :], (1, 0, 2)),
        jnp.zeros((depth, MOD_ROWS - N_DEV * B - N_DEV, 6 * D_MODEL), F32)], axis=1)
    g_b_mod = jnp.stack([_sum_leading(dm_full[l][:, None, :], name=f"b_mod_grad{l}")[0] for l in range(depth)])
    dm_mine = lax.dynamic_slice_in_dim(dm_full, me * mod_cols, mod_cols, axis=2)
    dcc, g_w_mod = _mod_bwd(cc, w_mod, b_shard, dm_mine, name="mod_bwd")
    g_c_ctx_part = jnp.sum(dcc[N_DEV * B:N_DEV * B + N_DEV], axis=0)

    small_like = [c_ctx, norm1_g, norm2_g, conv_dw_b, conv_ln_g, conv_ln_b, attn_sink, pool_w, pool_scale, final_g,
                  dp["conv_dw"]]
    small_part = _pack([g_c_ctx_part, dp["norm1_g"], dp["norm2_g"], dp["conv_dw_b"], dp["conv_ln_g"], dp["conv_ln_b"],
                        dp["attn_sink"], dp["pool_w"], dp["pool_scale"], dfinal, dp["conv_dw"]])
    small_all, = _exchange([small_part], scatter=False, name="gather_small")
    small_sum = _unpack(_sum_leading(small_all, name="sum_small"), small_like)
    g = dict(zip(("c_ctx", "norm1_g", "norm2_g", "conv_dw_b", "conv_ln_g", "conv_ln_b", "attn_sink", "pool_w",
                  "pool_scale", "final_g"), small_sum[:-1]))
    g["b_mod"] = g_b_mod
    g["conv_dw"] = lax.dynamic_slice_in_dim(small_sum[-1], me * dw_cols, dw_cols, axis=2)

    big.parts[0].update(zip(big.MIX, _exchange(big.cut(big.MIX, big.left), scatter=True, name="scatter_last")))
    parts = [jnp.stack([big.parts[l][name] for l in range(depth)], axis=1) for name in BIG_W]

    delta, new_m, new_v = {}, {}, {}
    for name, gpart in zip(("w_in", "w_out", "w_ffn_in", "w_ffn_out"), parts):
        shp = w[name].shape
        res = _adamw(_as_rows(gpart, 1), _as_rows(w[name]), _as_rows(mom[name]), _as_rows(var[name]), name=f"adamw_{name}")
        g[name], delta[name], new_m[name], new_v[name] = [r.reshape(shp) for r in res]
    res = _adamw(_as_rows(g_w_mod), _as_rows(w_mod), _as_rows(m_w_mod), _as_rows(v_w_mod), name="adamw_w_mod")
    g["w_mod"], delta["w_mod"], new_m["w_mod"], new_v["w_mod"] = [r.reshape(w_mod.shape) for r in res]
    res = _adamw(_pack([g[k] for k in SMALL]), _pack([w[k] for k in SMALL]), _pack([mom[k] for k in SMALL]),
                 _pack([var[k] for k in SMALL]), name="adamw_small")
    like = [w[k] for k in SMALL]
    for dst, slab in zip((delta, new_m, new_v), res[1:]):
        dst.update(zip(SMALL, _unpack(slab, like)))

    return (loss, dx, *[g[k] for k in ORDER], *[delta[k] for k in ORDER], *[new_m[k] for k in ORDER],
            *[new_v[k] for k in ORDER])
```

```python
import functools

import numpy as np
import jax
import jax.numpy as jnp
from jax import lax
from jax.experimental import pallas as pl
from jax.experimental.pallas import tpu as pltpu

F32 = jnp.float32
BF16 = jnp.bfloat16

D_MODEL = 1024
GRID_W = 64
HEAD_DIM = 64
ATTN_W = 512
CONV_W = 256
POOL_W = 256
ATTN_HEADS = 8
KV_HEADS = 2
GROUP = ATTN_HEADS // KV_HEADS
KV_W = KV_HEADS * HEAD_DIM
IN_W = ATTN_W + 2 * KV_W + 2 * CONV_W + POOL_W
WINDOW = 128
Q_BLOCK = 128
SPAN = Q_BLOCK + 2 * WINDOW
CONV_KERNEL = 31
POOL_WINDOWS = (2, 4, 8, 16)
POOL_GROUP = 64
ROPE_BASE = 10000.0
D_FF = 2816
EPS = 1e-6
NEG = -1e30
N_DEV = 8
AXES = ("x", "y", "c")

ADAM_LR = 0.001
ADAM_B1 = 0.9
ADAM_B2 = 0.999
ADAM_EPS = 1e-08
ADAM_WD = 0.01
ADAM_STEP = 10

VMEM_LIMIT = 56 * 1024 * 1024
HALO = 16
SEQ_CHUNK = 256


def _params(*sem):
    return pltpu.CompilerParams(dimension_semantics=sem, vmem_limit_bytes=VMEM_LIMIT)


def _tile(dim, target):
    if dim <= target:
        return dim
    t = (target // 128) * 128
    while t >= 128:
        if dim % t == 0:
            return t
        t -= 128
    raise ValueError(f"no tile for {dim}")


MM_K_WHOLE = 2816


def _dot(a, b):
    return lax.dot_general(a.astype(BF16), b.astype(BF16), (((1,), (0,)), ((), ())), preferred_element_type=F32)


def _mm(a, b, *, name, out_dtype=F32, tm=1408, tn=512, tk=2048):
    M, K = a.shape
    K2, N = b.shape
    assert K == K2, (a.shape, b.shape)
    tm = _tile(M, tm)
    tn = _tile(N, tn)
    tk = K if K <= MM_K_WHOLE else _tile(K, tk)
    nk = K // tk

    def body(a_ref, b_ref, o_ref, *scratch):
        part = _dot(a_ref[...], b_ref[...])
        if nk == 1:
            o_ref[...] = part.astype(o_ref.dtype)
        else:
            acc_ref, = scratch
            k = pl.program_id(2)

            @pl.when(k == 0)
            def _():
                acc_ref[...] = part

            @pl.when(k > 0)
            def _():
                acc_ref[...] += part

            @pl.when(k == nk - 1)
            def _():
                o_ref[...] = acc_ref[...].astype(o_ref.dtype)

    return pl.pallas_call(
        body, name=name, grid=(M // tm, N // tn, nk),
        in_specs=[pl.BlockSpec((tm, tk), lambda i, j, k: (i, k)), pl.BlockSpec((tk, tn), lambda i, j, k: (k, j))],
        out_specs=pl.BlockSpec((tm, tn), lambda i, j, k: (i, j)),
        out_shape=jax.ShapeDtypeStruct((M, N), out_dtype),
        scratch_shapes=[pltpu.VMEM((tm, tn), F32)] if nk > 1 else [],
        compiler_params=_params("parallel", "parallel", "arbitrary"),
    )(a, b)


FF_TILE = 256


def _interleave_ffn(w):
    tiles = D_FF // FF_TILE
    cols = [w[..., half * D_FF + j * FF_TILE:half * D_FF + (j + 1) * FF_TILE] for j in range(tiles) for half in range(2)]
    return jnp.concatenate(cols, axis=-1)


def _deinterleave_ffn(w):
    tiles = D_FF // FF_TILE
    cols = [w[..., (2 * j + half) * FF_TILE:(2 * j + half + 1) * FF_TILE] for half in range(2) for j in range(tiles)]
    return jnp.concatenate(cols, axis=-1)


def _swiglu(gu):
    g, u = gu[:, :FF_TILE], gu[:, FF_TILE:]
    return g * jax.nn.sigmoid(g) * u


def _mm_swiglu(a, w_il, *, name, tm=1024, exch=None):
    M, K = a.shape
    tm = _tile(M, tm)

    def body(a_ref, b_ref, gu_ref, act_ref, act_t_ref):
        gu = _dot(a_ref[...], b_ref[...])
        gu_ref[...] = gu.astype(gu_ref.dtype)
        act = _swiglu(gu)
        act_ref[...] = act.astype(act_ref.dtype)
        act_t_ref[...] = act.T.astype(act_t_ref.dtype)

    grid = (M // tm, D_FF // FF_TILE)
    body, x_in, x_out, x_shapes, x_sems = _riding(body, exch, 2, 3, grid)
    return pl.pallas_call(
        body, name=name, grid=grid,
        in_specs=[pl.BlockSpec((tm, K), lambda i, j: (i, 0)), pl.BlockSpec((K, 2 * FF_TILE), lambda i, j: (0, j))] + x_in,
        out_specs=[pl.BlockSpec((tm, 2 * FF_TILE), lambda i, j: (i, j)), pl.BlockSpec((tm, FF_TILE), lambda i, j: (i, j)),
                   pl.BlockSpec((FF_TILE, tm), lambda i, j: (j, i))] + x_out,
        out_shape=[jax.ShapeDtypeStruct((M, 2 * D_FF), BF16), jax.ShapeDtypeStruct((M, D_FF), BF16),
                   jax.ShapeDtypeStruct((D_FF, M), BF16)] + x_shapes,
        scratch_shapes=x_sems,
        compiler_params=_params("arbitrary", "arbitrary") if exch else _params("parallel", "parallel"),
    )(a, w_il, *(exch[0] if exch else []))


def _mm_dswiglu(dy, w_out_t, gu, *, name, tm=1024):
    M, K = dy.shape
    tm = _tile(M, tm)

    def body(dy_ref, b_ref, gu_ref, o_ref):
        dact = _dot(dy_ref[...], b_ref[...])
        _, vjp = jax.vjp(_swiglu, gu_ref[...].astype(F32))
        o_ref[...] = vjp(dact)[0].astype(o_ref.dtype)

    return pl.pallas_call(
        body, name=name, grid=(M // tm, D_FF // FF_TILE),
        in_specs=[pl.BlockSpec((tm, K), lambda i, j: (i, 0)), pl.BlockSpec((K, FF_TILE), lambda i, j: (0, j)),
                  pl.BlockSpec((tm, 2 * FF_TILE), lambda i, j: (i, j))],
        out_specs=pl.BlockSpec((tm, 2 * FF_TILE), lambda i, j: (i, j)),
        out_shape=jax.ShapeDtypeStruct((M, 2 * D_FF), BF16),
        compiler_params=_params("parallel", "parallel"),
    )(dy, w_out_t, gu)


def _rope_tables(n):
    rows = n // GRID_W
    row = jnp.repeat(jnp.arange(rows), GRID_W).astype(F32)
    col = jnp.tile(jnp.arange(GRID_W), rows).astype(F32)
    half = HEAD_DIM // 2
    inv = ROPE_BASE ** (-jnp.arange(0, half, 2, dtype=F32) / half)
    ar = row[:, None] * inv
    ac = col[:, None] * inv
    ang = jnp.concatenate([ar, ar, ac, ac], axis=-1)
    return jnp.cos(ang), jnp.sin(ang)


def _rot_half(x):
    w = x.shape[-1]
    lane = lax.broadcasted_iota(jnp.int32, x.shape, 1)
    up = pltpu.roll(x, w - 16, 1)
    down = pltpu.roll(x, 16, 1)
    return jnp.where((lane & 16) == 0, -up, down)


def _rope(x, cos, sin):
    return x * cos + _rot_half(x) * sin


def _rope_bwd(d, cos, sin):
    return d * cos - _rot_half(d * sin)


def _dot_nt(a, b):
    return lax.dot_general(a.astype(BF16), b.astype(BF16), (((1,), (1,)), ((), ())), preferred_element_type=F32)


def _attn_core(q_st, kc, vc, ks, vs, sink_rows, bias):
    qs = q_st * (HEAD_DIM ** -0.5)
    s_c = _dot_nt(qs, kc)
    mx = jnp.maximum(jnp.max(s_c, axis=1, keepdims=True), sink_rows)
    if ks is not None:
        s_l = _dot_nt(qs, ks) + bias
        mx = jnp.maximum(mx, jnp.max(s_l, axis=1, keepdims=True))
    e_c = jnp.exp(s_c - mx)
    den = jnp.sum(e_c, axis=1, keepdims=True) + jnp.exp(sink_rows - mx)
    if ks is not None:
        e_l = jnp.exp(s_l - mx)
        den = den + jnp.sum(e_l, axis=1, keepdims=True)
    inv = 1.0 / den
    o = _dot(e_c * inv, vc)
    if ks is not None:
        o = o + _dot(e_l * inv, vs)
    return o


def _stack_heads(x, kh):
    return jnp.concatenate([x[:, (GROUP * kh + g) * HEAD_DIM:(GROUP * kh + g + 1) * HEAD_DIM] for g in range(GROUP)], axis=0)


def _sink_rows(sink, kh):
    return jnp.concatenate([jnp.broadcast_to(sink[:, GROUP * kh + g:GROUP * kh + g + 1], (Q_BLOCK, 1)) for g in range(GROUP)], axis=0)


def _window_bias(start, s0):
    r = lax.broadcasted_iota(jnp.int32, (Q_BLOCK, SPAN), 0)
    c = lax.broadcasted_iota(jnp.int32, (Q_BLOCK, SPAN), 1)
    bias = jnp.where(jnp.abs(start - s0 + r - c) <= WINDOW, 0.0, NEG).astype(F32)
    return jnp.concatenate([bias] * GROUP, axis=0)


def _span_start(i, n):
    start = i * Q_BLOCK
    s0 = jnp.clip(start - WINDOW, 0, n - SPAN)
    return start, pl.multiple_of(s0, Q_BLOCK)


def _riding(body, exch, n_in, n_out, grid):
    if exch is None:
        return body, [], [], [], []
    arrs, scatter = exch
    k = len(arrs)

    def wrapped(*refs):
        ins, xin = refs[:n_in], refs[n_in:n_in + k]
        outs, xout = refs[n_in + k:n_in + k + n_out], refs[n_in + k + n_out:n_in + 2 * k + n_out]
        sems = refs[n_in + 2 * k + n_out:]
        b, i = pl.program_id(0), pl.program_id(1)

        @pl.when((b == 0) & (i == 0))
        def _():
            _exch_start(xin, xout, sems, scatter)

        body(*ins, *outs)

        @pl.when((b == grid[0] - 1) & (i == grid[1] - 1))
        def _():
            _exch_wait(xin, xout, sems, scatter)

    any_spec = pl.BlockSpec(memory_space=pl.ANY)
    return wrapped, [any_spec] * k, [any_spec] * k, _exch_out_shapes(arrs, scatter), _exch_sems(k)


def _attn_fwd(u, kvc, sink, cos, sin, *, local, name, exch=None):
    B, n, _ = u.shape
    n_ctx = kvc.shape[1]
    nb = n // Q_BLOCK
    assert (not local) or n >= SPAN

    def body(q_ref, k_ref, v_ref, kc_ref, vc_ref, sink_ref, cq_ref, sq_ref, ck_ref, sk_ref, o_ref):
        i = pl.program_id(1)
        q = q_ref[...]
        sink_v = sink_ref[...]
        kc = kc_ref[...]
        vc = vc_ref[...]
        if local:
            start, s0 = _span_start(i, n)
            q = _rope(q, cq_ref[...], sq_ref[...])
            ks = _rope(k_ref[pl.ds(s0, SPAN), :], ck_ref[pl.ds(s0, SPAN), :], sk_ref[pl.ds(s0, SPAN), :])
            vs = v_ref[pl.ds(s0, SPAN), :]
            bias = _window_bias(start, s0)
        for kh in range(KV_HEADS):
            sl = slice(kh * HEAD_DIM, (kh + 1) * HEAD_DIM)
            window = (ks[:, sl], vs[:, sl]) if local else (None, None)
            o = _attn_core(_stack_heads(q, kh), kc[:, sl], vc[:, sl], *window, _sink_rows(sink_v, kh), bias if local else None)
            for g in range(GROUP):
                h = GROUP * kh + g
                o_ref[:, h * HEAD_DIM:(h + 1) * HEAD_DIM] = o[g * Q_BLOCK:(g + 1) * Q_BLOCK, :]

    seq = lambda blk: pl.BlockSpec((None, n, KV_W), lambda b, i: (b, 0, blk))
    ctxs = lambda blk: pl.BlockSpec((None, n_ctx, KV_W), lambda b, i: (b, 0, blk))
    full = lambda a: pl.BlockSpec(a.shape, lambda b, i: (0,) * a.ndim)
    cos_q, sin_q = jnp.tile(cos, (1, ATTN_HEADS)), jnp.tile(sin, (1, ATTN_HEADS))
    cos_k, sin_k = jnp.tile(cos, (1, KV_HEADS)), jnp.tile(sin, (1, KV_HEADS))
    body, x_in, x_out, x_shapes, x_sems = _riding(body, exch, 10, 1, (B, nb))
    return pl.pallas_call(
        body, name=name, grid=(B, nb),
        in_specs=[pl.BlockSpec((None, Q_BLOCK, ATTN_W), lambda b, i: (b, i, 0)),
                  seq(ATTN_W // KV_W), seq(ATTN_W // KV_W + 1), ctxs(0), ctxs(1), full(sink),
                  pl.BlockSpec((Q_BLOCK, ATTN_W), lambda b, i: (i, 0)), pl.BlockSpec((Q_BLOCK, ATTN_W), lambda b, i: (i, 0)),
                  full(cos_k), full(sin_k)] + x_in,
        out_specs=[pl.BlockSpec((None, Q_BLOCK, ATTN_W), lambda b, i: (b, i, 0))] + x_out,
        out_shape=[jax.ShapeDtypeStruct((B, n, ATTN_W), F32)] + x_shapes,
        scratch_shapes=x_sems,
        compiler_params=_params("arbitrary", "arbitrary"),
    )(u, u, u, kvc, kvc, sink, cos_q, sin_q, cos_k, sin_k, *(exch[0] if exch else []))


def _attn_bwd(u, kvc, sink, cos, sin, do_src, do_blk, *, local, name, exch=None):
    B, n, _ = u.shape
    n_ctx = kvc.shape[1]
    nb = n // Q_BLOCK

    def body(q_ref, k_ref, v_ref, kc_ref, vc_ref, sink_ref, cq_ref, sq_ref, ck_ref, sk_ref, do_ref,
             dq_ref, dk_ref, dv_ref, dkc_ref, dvc_ref, dsink_ref):
        b = pl.program_id(0)
        i = pl.program_id(1)

        @pl.when(i == 0)
        def _():
            dk_ref[...] = jnp.zeros_like(dk_ref)
            dv_ref[...] = jnp.zeros_like(dv_ref)
            dkc_ref[...] = jnp.zeros_like(dkc_ref)
            dvc_ref[...] = jnp.zeros_like(dvc_ref)

        @pl.when((i == 0) & (b == 0))
        def _():
            dsink_ref[...] = jnp.zeros_like(dsink_ref)

        q = q_ref[...]
        do = do_ref[...]
        sink_v = sink_ref[...]
        kc = kc_ref[...]
        vc = vc_ref[...]
        if local:
            start, s0 = _span_start(i, n)
            ck = ck_ref[pl.ds(s0, SPAN), :]
            sk = sk_ref[pl.ds(s0, SPAN), :]
            q = _rope(q, cq_ref[...], sq_ref[...])
            ks = _rope(k_ref[pl.ds(s0, SPAN), :], ck, sk)
            vs = v_ref[pl.ds(s0, SPAN), :]
            bias = _window_bias(start, s0)
        dsink = jnp.zeros((1, ATTN_HEADS), F32)
        lane8 = lax.broadcasted_iota(jnp.int32, (1, ATTN_HEADS), 1)
        parts = []
        for kh in range(KV_HEADS):
            sl = slice(kh * HEAD_DIM, (kh + 1) * HEAD_DIM)
            if local:
                core = functools.partial(_attn_core, bias=bias)
                _, vjp = jax.vjp(core, _stack_heads(q, kh), kc[:, sl], vc[:, sl], ks[:, sl], vs[:, sl], _sink_rows(sink_v, kh))
                dq_st, *dkv, dsr = vjp(_stack_heads(do, kh))
            else:
                core = lambda qq, kk, vv, sr: _attn_core(qq, kk, vv, None, None, sr, None)
                _, vjp = jax.vjp(core, _stack_heads(q, kh), kc[:, sl], vc[:, sl], _sink_rows(sink_v, kh))
                dq_st, *dkv, dsr = vjp(_stack_heads(do, kh))
            for g in range(GROUP):
                h = GROUP * kh + g
                dq_ref[:, h * HEAD_DIM:(h + 1) * HEAD_DIM] = dq_st[g * Q_BLOCK:(g + 1) * Q_BLOCK, :]
                dsink = dsink + jnp.where(lane8 == h, jnp.sum(dsr[g * Q_BLOCK:(g + 1) * Q_BLOCK, :]), 0.0)
            parts.append(dkv)
        dkc, dvc, *dwin = [jnp.concatenate(t, axis=1) for t in zip(*parts)]
        dsink_ref[...] += dsink
        dkc_ref[...] += dkc
        dvc_ref[...] += dvc
        if local:
            dq_ref[...] = _rope_bwd(dq_ref[...], cq_ref[...], sq_ref[...])
            dk_ref[pl.ds(s0, SPAN), :] += _rope_bwd(dwin[0], ck, sk)
            dv_ref[pl.ds(s0, SPAN), :] += dwin[1]

    seq = lambda blk: pl.BlockSpec((None, n, KV_W), lambda b, i: (b, 0, blk))
    ctxs = lambda blk: pl.BlockSpec((None, n_ctx, KV_W), lambda b, i: (b, 0, blk))
    full = lambda a: pl.BlockSpec(a.shape, lambda b, i: (0,) * a.ndim)
    qblk = lambda blk: pl.BlockSpec((None, Q_BLOCK, ATTN_W), lambda b, i: (b, i, blk))
    cos_q, sin_q = jnp.tile(cos, (1, ATTN_HEADS)), jnp.tile(sin, (1, ATTN_HEADS))
    cos_k, sin_k = jnp.tile(cos, (1, KV_HEADS)), jnp.tile(sin, (1, KV_HEADS))
    acc = lambda rows: pl.BlockSpec((None, rows, KV_W), lambda b, i: (b, 0, 0))
    body, x_in, x_out, x_shapes, x_sems = _riding(body, exch, 11, 6, (B, nb))
    return pl.pallas_call(
        body, name=name, grid=(B, nb),
        in_specs=[qblk(0), seq(ATTN_W // KV_W), seq(ATTN_W // KV_W + 1), ctxs(0), ctxs(1), full(sink),
                  pl.BlockSpec((Q_BLOCK, ATTN_W), lambda b, i: (i, 0)), pl.BlockSpec((Q_BLOCK, ATTN_W), lambda b, i: (i, 0)),
                  full(cos_k), full(sin_k), qblk(do_blk)] + x_in,
        out_specs=[qblk(0), acc(n), acc(n), acc(n_ctx), acc(n_ctx), pl.BlockSpec((1, ATTN_HEADS), lambda b, i: (0, 0))] + x_out,
        out_shape=[jax.ShapeDtypeStruct((B, n, ATTN_W), F32), jax.ShapeDtypeStruct((B, n, KV_W), F32),
                   jax.ShapeDtypeStruct((B, n, KV_W), F32), jax.ShapeDtypeStruct((B, n_ctx, KV_W), F32),
                   jax.ShapeDtypeStruct((B, n_ctx, KV_W), F32), jax.ShapeDtypeStruct((1, ATTN_HEADS), F32)] + x_shapes,
        scratch_shapes=x_sems,
        compiler_params=_params("arbitrary", "arbitrary"),
    )(u, u, u, kvc, kvc, sink, cos_q, sin_q, cos_k, sin_k, do_src, *(exch[0] if exch else []))


def _conv_chunk(s, n, a_ext, g_ext, dw, dw_b, ln_g, ln_b):
    del s, n
    r = a_ext.shape[0] - 2 * HALO
    h = a_ext * jax.nn.sigmoid(g_ext)
    acc = jnp.broadcast_to(dw_b, (r, CONV_W))
    first = HALO - CONV_KERNEL // 2
    span = r + 8 * ((first + CONV_KERNEL - 1) // 8)
    shifted = [h[b:b + span, :] for b in range(8)]
    for k in range(CONV_KERNEL):
        o = first + k
        acc = acc + shifted[o % 8][o - o % 8:o - o % 8 + r, :] * dw[k:k + 1, :]
    mu = jnp.mean(acc, axis=-1, keepdims=True)
    var = jnp.mean(jnp.square(acc - mu), axis=-1, keepdims=True)
    hn = (acc - mu) * lax.rsqrt(var + EPS) * ln_g + ln_b
    return hn * jax.nn.sigmoid(hn)


def _pool_chunk(s, n, p_ext, w_bd, scale):
    r = p_ext.shape[0] - 2 * HALO
    lane = lax.broadcasted_iota(jnp.int32, (1, POOL_W), 1)
    win = jnp.left_shift(2, lane // POOL_GROUP)
    half = win // 2
    acc = jnp.zeros((r, POOL_W), F32)
    for d in range(-(POOL_WINDOWS[-1] // 2), POOL_WINDOWS[-1] - POOL_WINDOWS[-1] // 2):
        inside = (d >= -half) & (d <= win - 1 - half)
        acc = acc + jnp.where(inside, p_ext[HALO + d:HALO + d + r, :], 0.0)
    t = s + lax.broadcasted_iota(jnp.int32, (r, 1), 0)
    lo = jnp.maximum(t - half, 0)
    hi = jnp.minimum(t + win - 1 - half, n - 1)
    y = acc / (hi - lo + 1).astype(F32) - p_ext[HALO:HALO + r, :]
    out = lax.dot_general(y.astype(BF16), w_bd.astype(BF16), (((1,), (0,)), ((), ())), preferred_element_type=F32)
    return out * scale


def _seq_specs(rows, params):
    specs = [pl.BlockSpec((None, a.shape[1], w), functools.partial(lambda b, blk: (b, 0, blk), blk=blk)) for a, w, blk in rows]
    specs += [pl.BlockSpec(p.shape, functools.partial(lambda b, nd: (0,) * nd, nd=p.ndim)) for p in params]
    return specs


def _fill_padded(pad_ref, row_ref, n):
    w = pad_ref.shape[1]
    pad_ref[pl.ds(0, HALO), :] = jnp.zeros((HALO, w), F32)
    pad_ref[pl.ds(HALO + n, HALO), :] = jnp.zeros((HALO, w), F32)
    pad_ref[pl.ds(HALO, n), :] = row_ref[...]


def _seq_fwd(fn, rows, params, out_w, *, name):
    B, n = rows[0][0].shape[:2]
    r = min(SEQ_CHUNK, n)
    nr, npar = len(rows), len(params)

    def body(*refs):
        row_refs, par_refs, o_ref, pads = refs[:nr], refs[nr:nr + npar], refs[nr + npar], refs[nr + npar + 1:]
        for rr, p in zip(row_refs, pads):
            _fill_padded(p, rr, n)
        pars = [p[...] for p in par_refs]

        def chunk(ci, carry):
            s = pl.multiple_of(ci * r, r)
            ext = [p[pl.ds(s, r + 2 * HALO), :] for p in pads]
            o_ref[pl.ds(s, r), :] = fn(s, n, *ext, *pars)
            return carry

        lax.fori_loop(0, n // r, chunk, 0)

    return pl.pallas_call(
        body, name=name, grid=(B,),
        in_specs=_seq_specs(rows, params),
        out_specs=pl.BlockSpec((None, n, out_w), lambda b: (b, 0, 0)),
        out_shape=jax.ShapeDtypeStruct((B, n, out_w), F32),
        scratch_shapes=[pltpu.VMEM((n + 2 * HALO, w), F32) for _, w, _ in rows],
        compiler_params=_params("parallel"),
    )(*[a for a, _, _ in rows], *params)


def _seq_bwd(fn, rows, params, dout, *, name):
    B, n = rows[0][0].shape[:2]
    r = min(SEQ_CHUNK, n)
    nr, npar = len(rows), len(params)

    def body(*refs):
        row_refs, par_refs, do_ref = refs[:nr], refs[nr:nr + npar], refs[nr + npar]
        outs = refs[nr + npar + 1:]
        drow_refs, dpar_refs = outs[:nr], outs[nr:nr + npar]
        scratch = outs[nr + npar:]
        pads, dpads = scratch[:nr], scratch[nr:]
        for rr, p, dp in zip(row_refs, pads, dpads):
            _fill_padded(p, rr, n)
            dp[...] = jnp.zeros_like(dp)

        @pl.when(pl.program_id(0) == 0)
        def _():
            for d in dpar_refs:
                d[...] = jnp.zeros_like(d)

        pars = [p[...] for p in par_refs]

        def chunk(ci, carry):
            s = pl.multiple_of(ci * r, r)
            ext = [p[pl.ds(s, r + 2 * HALO), :] for p in pads]
            _, vjp = jax.vjp(functools.partial(fn, s, n), *ext, *pars)
            grads = vjp(do_ref[pl.ds(s, r), :])
            for dp, g in zip(dpads, grads[:nr]):
                dp[pl.ds(s, r + 2 * HALO), :] += g
            for d, g in zip(dpar_refs, grads[nr:]):
                d[...] += g
            return carry

        lax.fori_loop(0, n // r, chunk, 0)
        for d, dp in zip(drow_refs, dpads):
            d[...] = dp[pl.ds(HALO, n), :]

    da, dw_, dblk = dout
    return pl.pallas_call(
        body, name=name, grid=(B,),
        in_specs=_seq_specs(rows, params) + [pl.BlockSpec((None, n, dw_), lambda b: (b, 0, dblk))],
        out_specs=[pl.BlockSpec((None, n, w), lambda b: (b, 0, 0)) for _, w, _ in rows]
        + [pl.BlockSpec(p.shape, functools.partial(lambda b, nd: (0,) * nd, nd=p.ndim)) for p in params],
        out_shape=[jax.ShapeDtypeStruct((B, n, w), F32) for _, w, _ in rows]
        + [jax.ShapeDtypeStruct(p.shape, F32) for p in params],
        scratch_shapes=[pltpu.VMEM((n + 2 * HALO, w), F32) for _, w, _ in rows] * 2,
        compiler_params=_params("arbitrary"),
    )(*[a for a, _, _ in rows], *params, da)


_CONV_A_BLK = (ATTN_W + 2 * KV_W) // CONV_W
_CONV_G_BLK = _CONV_A_BLK + 1
_POOL_BLK = _CONV_A_BLK + 2


def _mixer_fwd(tag, u, kvc, margs, local, exch=None):
    sink, dw, dw_b, ln_g, ln_b, w_bd, scale = margs
    cos, sin = _rope_tables(max(u.shape[1], GRID_W))
    attn, *got = _attn_fwd(u, kvc, sink, cos, sin, local=local, name=f"{tag}_attn_fwd", exch=exch)
    conv = _seq_fwd(_conv_chunk, [(u, CONV_W, _CONV_A_BLK), (u, CONV_W, _CONV_G_BLK)], [dw, dw_b, ln_g, ln_b], CONV_W,
                    name=f"{tag}_conv_fwd")
    pool = _seq_fwd(_pool_chunk, [(u, POOL_W, _POOL_BLK)], [w_bd, scale], POOL_W, name=f"{tag}_pool_fwd")
    return jnp.concatenate([attn, conv, pool], axis=-1).astype(BF16), got


def _mixer_bwd(tag, u, kvc, margs, dmix, local, exch=None):
    sink, dw, dw_b, ln_g, ln_b, w_bd, scale = margs
    cos, sin = _rope_tables(max(u.shape[1], GRID_W))
    dq, dk, dv, dkc, dvc, dsink, *got = _attn_bwd(u, kvc, sink, cos, sin, dmix, 0, local=local, name=f"{tag}_attn_bwd",
                                                  exch=exch)
    da, dg, ddw, ddw_b, dln_g, dln_b = _seq_bwd(
        _conv_chunk, [(u, CONV_W, _CONV_A_BLK), (u, CONV_W, _CONV_G_BLK)], [dw, dw_b, ln_g, ln_b],
        (dmix, CONV_W, ATTN_W // CONV_W), name=f"{tag}_conv_bwd")
    dpu, dw_bd, dscale = _seq_bwd(_pool_chunk, [(u, POOL_W, _POOL_BLK)], [w_bd, scale],
                                  (dmix, POOL_W, (ATTN_W + CONV_W) // POOL_W), name=f"{tag}_pool_bwd")
    return (dq, dk, dv, da, dg, dpu), (dkc, dvc), (dsink, ddw, ddw_b, dln_g, dln_b, dw_bd, dscale), got


def _row_specs(arrs, kinds, tr):
    specs = []
    for a, kind in zip(arrs, kinds):
        if kind == "row":
            specs.append(pl.BlockSpec((None, tr, a.shape[2]), lambda b, j: (b, j, 0)))
        elif kind == "batch":
            specs.append(pl.BlockSpec((None, 1, a.shape[2]), lambda b, j: (b, 0, 0)))
        else:
            specs.append(pl.BlockSpec(a.shape, functools.partial(lambda b, j, nd: (0,) * nd, nd=a.ndim)))
    return specs


def _rowwise_fwd(fn, ins, kinds, outs, tr, *, name, transposed=None):
    B, n = ins[0].shape[:2]
    ni, no = len(ins), len(outs)
    nj = n // tr

    def body(*refs):
        res = fn(*[r[...] for r in refs[:ni]])
        for o, v in zip(refs[ni:ni + no], res):
            o[...] = v.astype(o.dtype)
        if transposed is not None:
            refs[ni + no][...] = res[transposed].T.astype(refs[ni + no].dtype)

    out_specs = [pl.BlockSpec((None, tr, w), lambda b, j: (b, j, 0)) for w, _ in outs]
    out_shape = [jax.ShapeDtypeStruct((B, n, w), dt) for w, dt in outs]
    if transposed is not None:
        w, dt = outs[transposed]
        out_specs.append(pl.BlockSpec((w, tr), lambda b, j: (0, b * nj + j)))
        out_shape.append(jax.ShapeDtypeStruct((w, B * n), dt))
    return pl.pallas_call(
        body, name=name, grid=(B, nj),
        in_specs=_row_specs(ins, kinds, tr), out_specs=out_specs, out_shape=out_shape,
        compiler_params=_params("parallel", "parallel"),
    )(*ins)


def _rowwise_bwd(fn, ins, kinds, gdtypes, cts, tr, *, name):
    B, n = ins[0].shape[:2]
    ni, nc = len(ins), len(cts)
    idx = list(range(ni))

    def body(*refs):
        in_refs, ct_refs, out_refs = refs[:ni], refs[ni:ni + nc], refs[ni + nc:]
        b, j = pl.program_id(0), pl.program_id(1)
        _, vjp = jax.vjp(fn, *[r[...].astype(F32) for r in in_refs])
        grads = vjp(tuple(c[...].astype(F32) for c in ct_refs))
        for o, i in zip(out_refs, idx):
            g = grads[i]
            if kinds[i] == "row":
                o[...] = g.astype(o.dtype)
            else:
                first = (j == 0) if kinds[i] == "batch" else ((j == 0) & (b == 0))

                @pl.when(first)
                def _(o=o, g=g):
                    o[...] = g

                @pl.when(jnp.logical_not(first))
                def _(o=o, g=g):
                    o[...] += g

    specs = _row_specs(ins, kinds, tr)
    return pl.pallas_call(
        body, name=name, grid=(B, n // tr),
        in_specs=specs + [pl.BlockSpec((None, tr, c.shape[2]), lambda b, j: (b, j, 0)) for c in cts],
        out_specs=[specs[i] for i in idx],
        out_shape=[jax.ShapeDtypeStruct(ins[i].shape, gdtypes[i]) for i in idx],
        compiler_params=_params("arbitrary", "arbitrary"),
    )(*ins, *cts)


ROW_TILE = 256


def _rms_mod(x, g, sc, sh):
    y = x * lax.rsqrt(jnp.mean(x * x, axis=-1, keepdims=True) + EPS)
    return (y * g) * (1.0 + sc) + sh


def _norm_tile(x, g, sc, sh):
    return x, _rms_mod(x, g, sc, sh)


def _res_norm_tile(xb, y, gate, g, sc, sh):
    x = xb + gate * y
    return x, _rms_mod(x, g, sc, sh)


_NORM_KINDS = ("row", "glob", "batch", "batch")
_RES_NORM_KINDS = ("row", "row", "batch", "glob", "batch", "batch")


def _norm_fwd(tag, st, g, sc, sh):
    xb, y, gate = st
    tr = min(ROW_TILE, xb.shape[1])
    d = xb.shape[2]
    if y is None:
        h, h_t = _rowwise_fwd(lambda *a: (_rms_mod(*a),), [xb, g, sc, sh], _NORM_KINDS, [(d, BF16)], tr,
                              name=f"{tag}_fwd", transposed=0)
        return xb, h, h_t
    return _rowwise_fwd(_res_norm_tile, [xb, y, gate, g, sc, sh], _RES_NORM_KINDS, [(d, F32), (d, BF16)], tr,
                        name=f"{tag}_fwd", transposed=1)


def _norm_bwd(tag, st, g, sc, sh, dx, dh):
    xb, y, gate = st
    tr = min(ROW_TILE, xb.shape[1])
    if y is None:
        dxb, dg, dsc, dsh = _rowwise_bwd(_norm_tile, [xb, g, sc, sh], _NORM_KINDS, [F32] * 4, [dx, dh], tr, name=f"{tag}_bwd")
        return dxb, None, None, dg, dsc, dsh
    return tuple(_rowwise_bwd(_res_norm_tile, [xb, y, gate, g, sc, sh], _RES_NORM_KINDS, [F32, BF16, F32, F32, F32, F32],
                              [dx, dh], tr, name=f"{tag}_bwd"))


def _loss_head(st, final_g, target, *, name):
    xb, y, gate = st
    B, n, d = xb.shape
    tr = min(ROW_TILE, n)

    def tile_loss(xv, yv, gt, g, t):
        x = xv + gt * yv
        out = x * lax.rsqrt(jnp.mean(x * x, axis=-1, keepdims=True) + EPS) * g
        return 0.5 * jnp.sum(jnp.mean(jnp.square(out - t), axis=-1))

    def body(x_ref, y_ref, gate_ref, g_ref, t_ref, loss_ref, dx_ref, dy_ref, dgate_ref, dg_ref):
        b, j = pl.program_id(0), pl.program_id(1)
        val, (dx, dy, dgate, dg) = jax.value_and_grad(tile_loss, argnums=(0, 1, 2, 3))(
            x_ref[...], y_ref[...], gate_ref[...], g_ref[...], t_ref[...])
        dx_ref[...] = dx
        dy_ref[...] = dy.astype(dy_ref.dtype)

        @pl.when(j == 0)
        def _():
            loss_ref[...] = jnp.zeros_like(loss_ref)
            dgate_ref[...] = jnp.zeros_like(dgate_ref)

        @pl.when((j == 0) & (b == 0))
        def _():
            dg_ref[...] = jnp.zeros_like(dg_ref)

        loss_ref[...] += jnp.full(loss_ref.shape, val, F32)
        dgate_ref[...] += dgate
        dg_ref[...] += dg

    row = pl.BlockSpec((None, tr, d), lambda b, j: (b, j, 0))
    per_sample = pl.BlockSpec((None, 1, d), lambda b, j: (b, 0, 0))
    whole = pl.BlockSpec((1, d), lambda b, j: (0, 0))
    return pl.pallas_call(
        body, name=name, grid=(B, n // tr),
        in_specs=[row, row, per_sample, whole, row],
        out_specs=[pl.BlockSpec((None, 1, 128), lambda b, j: (b, 0, 0)), row, row, per_sample, whole],
        out_shape=[jax.ShapeDtypeStruct((B, 1, 128), F32), jax.ShapeDtypeStruct((B, n, d), F32),
                   jax.ShapeDtypeStruct((B, n, d), BF16), jax.ShapeDtypeStruct((B, 1, d), F32), jax.ShapeDtypeStruct((1, d), F32)],
        compiler_params=_params("arbitrary", "arbitrary"),
    )(xb, y, gate, final_g, target)


def _exchange(arrs, *, scatter, name):
    k = len(arrs)

    def body(*refs):
        ins, outs, sems = refs[:k], refs[k:2 * k], refs[2 * k:]
        _exch_start(ins, outs, sems, scatter)
        _exch_wait(ins, outs, sems, scatter)

    any_spec = pl.BlockSpec(memory_space=pl.ANY)
    return pl.pallas_call(
        body, name=name,
        in_specs=[any_spec] * k, out_specs=[any_spec] * k,
        out_shape=_exch_out_shapes(arrs, scatter), scratch_shapes=_exch_sems(k),
        compiler_params=pltpu.CompilerParams(has_side_effects=True),
    )(*arrs)


def _exch_out_shapes(arrs, scatter):
    return [jax.ShapeDtypeStruct(a.shape if scatter else (N_DEV,) + a.shape, a.dtype) for a in arrs]


def _exch_sems(k):
    return [pltpu.SemaphoreType.DMA((k * (N_DEV - 1),)), pltpu.SemaphoreType.DMA((k * (N_DEV - 1),)),
            pltpu.SemaphoreType.DMA((k,))]


def _exch_copies(ins, outs, sems, scatter):
    send_sems, recv_sems, local_sems = sems
    x, y, c = lax.axis_index("x"), lax.axis_index("y"), lax.axis_index("c")
    me = 4 * x + 2 * y + c
    owns, sends, recvs = [], [], []
    for a in range(len(ins)):
        owns.append(pltpu.make_async_copy(ins[a].at[me] if scatter else ins[a], outs[a].at[me], local_sems.at[a]))
        for r in range(1, N_DEV):
            fx, fy, fc = (r >> 2) & 1, (r >> 1) & 1, r & 1
            px, py, pc = (x + fx) % 2, (y + fy) % 2, (c + fc) % 2
            peer = 4 * px + 2 * py + pc
            s = a * (N_DEV - 1) + r - 1
            mk = functools.partial(pltpu.make_async_remote_copy, src_ref=ins[a].at[peer] if scatter else ins[a],
                                   send_sem=send_sems.at[s], recv_sem=recv_sems.at[s],
                                   device_id=(px, py, pc), device_id_type=pl.DeviceIdType.MESH)
            sends.append(mk(dst_ref=outs[a].at[me]))
            recvs.append(mk(dst_ref=outs[a].at[peer]))
    return owns, sends, recvs


def _exch_start(ins, outs, sems, scatter):
    owns, sends, _ = _exch_copies(ins, outs, sems, scatter)
    for cp in owns + sends:
        cp.start()


def _exch_wait(ins, outs, sems, scatter):
    owns, sends, recvs = _exch_copies(ins, outs, sems, scatter)
    for rc in recvs:
        rc.wait_recv()
    for cp in sends:
        cp.wait_send()
    for own in owns:
        own.wait()


MOD_ROWS = 48


def _mod_tile(cc, w, b):
    s = cc * jax.nn.sigmoid(cc)
    return lax.dot_general(s.astype(BF16), w.astype(BF16), (((1,), (0,)), ((), ())), preferred_element_type=F32) + b


def _mod_fwd(cc, w_mod, b_shard, *, name):
    L, d, wcols = w_mod.shape

    def body(cc_ref, w_ref, b_ref, o_ref):
        o_ref[...] = _mod_tile(cc_ref[...], w_ref[...], b_ref[...])

    return pl.pallas_call(
        body, name=name, grid=(L,),
        in_specs=[pl.BlockSpec((MOD_ROWS, d), lambda l: (0, 0)), pl.BlockSpec((None, d, wcols), lambda l: (l, 0, 0)),
                  pl.BlockSpec((None, 1, wcols), lambda l: (l, 0, 0))],
        out_specs=pl.BlockSpec((None, MOD_ROWS, wcols), lambda l: (l, 0, 0)),
        out_shape=jax.ShapeDtypeStruct((L, MOD_ROWS, wcols), F32),
        compiler_params=_params("parallel"),
    )(cc, w_mod, b_shard)


def _mod_bwd(cc, w_mod, b_shard, dm, *, name):
    L, d, wcols = w_mod.shape

    def body(cc_ref, w_ref, b_ref, dm_ref, dcc_ref, dw_ref):
        _, vjp = jax.vjp(_mod_tile, cc_ref[...], w_ref[...], b_ref[...])
        dcc, dw, _ = vjp(dm_ref[...])
        dw_ref[...] = dw

        @pl.when(pl.program_id(0) == 0)
        def _():
            dcc_ref[...] = dcc

        @pl.when(pl.program_id(0) > 0)
        def _():
            dcc_ref[...] += dcc

    return pl.pallas_call(
        body, name=name, grid=(L,),
        in_specs=[pl.BlockSpec((MOD_ROWS, d), lambda l: (0, 0)), pl.BlockSpec((None, d, wcols), lambda l: (l, 0, 0)),
                  pl.BlockSpec((None, 1, wcols), lambda l: (l, 0, 0)), pl.BlockSpec((None, MOD_ROWS, wcols), lambda l: (l, 0, 0))],
        out_specs=[pl.BlockSpec((MOD_ROWS, d), lambda l: (0, 0)), pl.BlockSpec((None, d, wcols), lambda l: (l, 0, 0))],
        out_shape=[jax.ShapeDtypeStruct((MOD_ROWS, d), F32), jax.ShapeDtypeStruct((L, d, wcols), F32)],
        compiler_params=_params("arbitrary"),
    )(cc, w_mod, b_shard, dm)


def _sum_leading(a, *, name):
    K, R, C = a.shape
    tr = _tile8(R, 256)

    def body(a_ref, o_ref):
        acc = a_ref[0].astype(F32)
        for i in range(1, K):
            acc = acc + a_ref[i].astype(F32)
        o_ref[...] = acc

    return pl.pallas_call(
        body, name=name, grid=(R // tr,),
        in_specs=[pl.BlockSpec((K, tr, C), lambda i: (0, i, 0))],
        out_specs=pl.BlockSpec((tr, C), lambda i: (i, 0)),
        out_shape=jax.ShapeDtypeStruct((R, C), F32),
        compiler_params=_params("parallel"),
    )(a)


def _tile8(dim, target):
    if dim <= target:
        return dim
    t = (target // 8) * 8
    while t >= 8:
        if dim % t == 0:
            return t
        t -= 8
    raise ValueError(f"no row tile for {dim}")


def _adamw_math(g, w, m, v):
    m = ADAM_B1 * m + (1.0 - ADAM_B1) * g
    v = ADAM_B2 * v + (1.0 - ADAM_B2) * jnp.square(g)
    m_hat = m / (1.0 - ADAM_B1 ** ADAM_STEP)
    v_hat = v / (1.0 - ADAM_B2 ** ADAM_STEP)
    delta = -ADAM_LR * (m_hat / (jnp.sqrt(v_hat) + ADAM_EPS) + ADAM_WD * w)
    return delta, m, v


def _adamw(g, w, m, v, *, name):
    L, R, C = w.shape
    parts = isinstance(g, (list, tuple))
    gs = list(g) if parts else [g]
    ng = len(gs)
    tr = _tile8(R, 256)

    def body(*refs):
        g_refs = refs[:ng]
        w_ref, m_ref, v_ref, go_ref, d_ref, mo_ref, vo_ref = refs[ng:]
        if parts:
            layer = pl.program_id(0)
            gv = None
            for li, g_ref in enumerate(g_refs):
                acc = g_ref[0].astype(F32)
                for i in range(1, N_DEV):
                    acc = acc + g_ref[i].astype(F32)
                gv = acc if gv is None else jnp.where(layer == li, acc, gv)
        else:
            gv = g_refs[0][...]
        go_ref[...] = gv
        d_ref[...], mo_ref[...], vo_ref[...] = _adamw_math(gv, w_ref[...], m_ref[...], v_ref[...])

    tile = pl.BlockSpec((None, tr, C), lambda l, i: (l, i, 0))
    g_specs = [pl.BlockSpec((N_DEV, tr, C), lambda l, i: (0, i, 0))] * ng if parts else [tile]
    return pl.pallas_call(
        body, name=name, grid=(L, R // tr),
        in_specs=g_specs + [tile, tile, tile], out_specs=[tile] * 4,
        out_shape=[jax.ShapeDtypeStruct((L, R, C), F32)] * 4,
        compiler_params=_params("parallel", "parallel"),
    )(*gs, w, m, v)


def _adamw_small(gs, ws, ms, vs, *, name):
    k = len(ws)

    def body(*refs):
        g_refs, w_refs, m_refs, v_refs = refs[:k], refs[k:2 * k], refs[2 * k:3 * k], refs[3 * k:4 * k]
        d_refs, mo_refs, vo_refs = refs[4 * k:5 * k], refs[5 * k:6 * k], refs[6 * k:]
        for i in range(k):
            d_refs[i][...], mo_refs[i][...], vo_refs[i][...] = _adamw_math(g_refs[i][...], w_refs[i][...], m_refs[i][...],
                                                                         v_refs[i][...])

    shapes = [jax.ShapeDtypeStruct(a.shape, F32) for a in ws]
    out = pl.pallas_call(body, name=name, out_shape=shapes * 3, compiler_params=pltpu.CompilerParams(vmem_limit_bytes=VMEM_LIMIT))(
        *gs, *ws, *ms, *vs)
    return out[:k], out[k:2 * k], out[2 * k:]


def _block_diag(w):
    g = w.shape[0]
    rows = [jnp.concatenate([w[i] if j == i else jnp.zeros_like(w[i]) for j in range(g)], axis=1) for i in range(g)]
    return jnp.concatenate(rows, axis=0)


def _diag_blocks(w_bd):
    g = POOL_W // POOL_GROUP
    return jnp.stack([w_bd[i * POOL_GROUP:(i + 1) * POOL_GROUP, i * POOL_GROUP:(i + 1) * POOL_GROUP] for i in range(g)])


def _flat(a):
    return a.reshape(-1, a.shape[-1])


def _mix_half_fwd(tag, st, mods, wl, kvc, *, local, kv_only, exch=None):
    sh1, sc1, g1 = mods[:3]
    B, n, d = st[0].shape
    x, h, h_t = _norm_fwd(f"{tag}_norm1", st, wl["n1"], sc1, sh1)
    if kv_only:
        kv = _mm(_flat(h), wl["w_in"][:, ATTN_W:ATTN_W + 2 * KV_W], name=f"{tag}_kv").reshape(B, n, 2 * KV_W)
        return None, dict(st=st, h_t=h_t, kvc=kv), []
    u = _mm(_flat(h), wl["w_in"], name=f"{tag}_in").reshape(B, n, IN_W)
    if not local:
        kvc = u[:, :, ATTN_W:ATTN_W + 2 * KV_W]
    mix, got = _mixer_fwd(f"{tag}_mix", u, kvc, wl["margs"], local, exch)
    y = _mm(_flat(mix), wl["w_out"], name=f"{tag}_out").reshape(B, n, d)
    return (x, y, g1), dict(st=st, h_t=h_t, u=u, kvc=kvc, mix=mix), got


def _ffn_half_fwd(tag, st2, mods, wl, exch=None):
    sh2, sc2, g2 = mods[3:]
    B, n, d = st2[0].shape
    x1, h2, h2_t = _norm_fwd(f"{tag}_norm2", st2, wl["n2"], sc2, sh2)
    gu, act, act_t, *got = _mm_swiglu(_flat(h2), wl["w_ffn_in"], name=f"{tag}_ffn_in", exch=exch)
    y2 = _mm(act, wl["w_ffn_out"], name=f"{tag}_ffn_out").reshape(B, n, d)
    return (x1, y2, g2), dict(st2=st2, h2_t=h2_t, gu=gu, act_t=act_t), got


def _ffn_half_bwd(tag, sv, mods, wl, dx1, dy2):
    sh2, sc2, _ = mods[3:]
    B, n, d = sv["st2"][0].shape
    gw = {}
    dy2f = _flat(dy2)
    gw["w_ffn_out"] = _mm(sv["act_t"], dy2f, out_dtype=BF16, name=f"{tag}_ffn_out_dw")
    dgu = _mm_dswiglu(dy2f, wl["w_ffn_out_t"], sv["gu"], name=f"{tag}_ffn_out_dx")
    dh2 = _mm(dgu, wl["w_ffn_in_t"], name=f"{tag}_ffn_in_dx").reshape(B, n, d)
    gw["w_ffn_in"] = _deinterleave_ffn(_mm(sv["h2_t"], dgu, out_dtype=BF16, name=f"{tag}_ffn_in_dw"))
    dx, dy, dg1, gw["n2"], dsc2, dsh2 = _norm_bwd(f"{tag}_norm2", sv["st2"], wl["n2"], sc2, sh2, dx1, dh2)
    return (dx, dy, dg1), gw, dict(sh2=dsh2, sc2=dsc2)


def _mix_half_bwd(tag, sv, mods, wl, dx, dy, dkv_in, *, local, kv_only, exch=None):
    sh1, sc1, _ = mods[:3]
    B, n, d = sv["st"][0].shape
    gw = {}
    if kv_only:
        dkv = _flat(dkv_in).astype(BF16)
        dh = _mm(dkv, wl["w_in_t"][ATTN_W:ATTN_W + 2 * KV_W, :], name=f"{tag}_kv_dx").reshape(B, n, d)
        gw["w_in_kv"] = _mm(sv["h_t"], dkv, out_dtype=BF16, name=f"{tag}_kv_dw")
        dxb, dy_prev, dgate_prev, gw["n1"], dsc1, dsh1 = _norm_bwd(f"{tag}_norm1", sv["st"], wl["n1"], sc1, sh1,
                                                                    jnp.zeros((B, n, d), F32), dh)
        return (dxb, dy_prev, dgate_prev), gw, dict(sh1=dsh1, sc1=dsc1), None, []

    dyf = _flat(dy)
    dmix = _mm(dyf, wl["w_out_t"], name=f"{tag}_out_dx").reshape(B, n, d)
    gw["w_out"] = _mm(_flat(sv["mix"]).T, dyf, out_dtype=BF16, name=f"{tag}_out_dw")
    (dq, dk, dv, da, dg, dpu), (dkc, dvc), gw["margs"], got = _mixer_bwd(f"{tag}_mix", sv["u"], sv["kvc"], wl["margs"], dmix,
                                                                     local, exch)
    if local:
        dkv_out = jnp.concatenate([dkc, dvc], axis=-1)
    else:
        dk = dkc + dkv_in[:, :, :KV_W]
        dv = dvc + dkv_in[:, :, KV_W:]
        dkv_out = None
    du = _flat(jnp.concatenate([dq, dk, dv, da, dg, dpu], axis=-1).astype(BF16))
    dh = _mm(du, wl["w_in_t"], name=f"{tag}_in_dx").reshape(B, n, d)
    gw["w_in"] = _mm(sv["h_t"], du, out_dtype=BF16, name=f"{tag}_in_dw")
    dxb, dy_prev, dgate_prev, gw["n1"], dsc1, dsh1 = _norm_bwd(f"{tag}_norm1", sv["st"], wl["n1"], sc1, sh1, dx, dh)
    return (dxb, dy_prev, dgate_prev), gw, dict(sh1=dsh1, sc1=dsc1), dkv_out, got


BIG_W = ("w_in", "w_out", "w_ffn_in", "w_ffn_out")


def _local_step(x, ctx, m_loc, m_ctx, p, final_g, target, big):
    B = x.shape[0]
    depth = m_loc.shape[0]
    lat_mods = [[t[:, None, :] for t in jnp.split(m_loc[l], 6, axis=-1)] for l in range(depth)]
    ctx_mods = [[jnp.broadcast_to(t[None, None, :], (B, 1, D_MODEL)) for t in jnp.split(m_ctx[l], 6)] for l in range(depth)]

    st, cst = (x, None, None), (ctx, None, None)
    w_mix, w_ffn, sv_mix, sv_ffn, csv_mix, csv_ffn = [], [], [], [], [], []
    got = []
    for l in range(depth):
        last = l == depth - 1
        wb = big.mix_weights(l, got)
        wm = dict(n1=p["norm1_g"][l][None, :], w_in=wb["w_in"], w_out=wb["w_out"], w_in_t=wb["w_in"].T, w_out_t=wb["w_out"].T,
                  margs=(p["attn_sink"][l][None, :], p["conv_dw"][l], p["conv_dw_b"][l][None, :], p["conv_ln_g"][l][None, :],
                         p["conv_ln_b"][l][None, :], _block_diag(p["pool_w"][l]), p["pool_scale"][l][None, :]))
        cst, csv, _ = _mix_half_fwd(f"l{l}c", cst, ctx_mods[l], wm, None, local=False, kv_only=last)
        st, sv, got = _mix_half_fwd(f"l{l}", st, lat_mods[l], wm, csv["kvc"], local=True, kv_only=False,
                                    exch=big.ride_attn_fwd(l))
        w_mix.append(wm)
        sv_mix.append(sv)
        csv_mix.append(csv)
        wb = big.ffn_weights(l, got)
        w_ffn_in = _interleave_ffn(wb["w_ffn_in"])
        wf = dict(n2=p["norm2_g"][l][None, :], w_ffn_in=w_ffn_in, w_ffn_out=wb["w_ffn_out"],
                  w_ffn_in_t=w_ffn_in.T, w_ffn_out_t=wb["w_ffn_out"].T)
        csv = None
        if not last:
            cst, csv, _ = _ffn_half_fwd(f"l{l}c", cst, ctx_mods[l], wf)
        st, sv, got = _ffn_half_fwd(f"l{l}", st, lat_mods[l], wf, exch=big.ride_ffn_fwd(l))
        w_ffn.append(wf)
        sv_ffn.append(sv)
        csv_ffn.append(csv)
    loss_rows, dx, dy, dgate, dfinal = _loss_head(st, final_g[None, :], target, name="loss_head")

    dm_loc, dm_ctx = [None] * depth, [None] * depth
    small = [None] * depth
    cdx = cdy = cdgate = None
    up_mix = None
    for l in reversed(range(depth)):
        last = l == depth - 1
        dm, cdm = dict(g2=dgate), {}
        (dx, dy, dm["g1"]), gf, d = _ffn_half_bwd(f"l{l}", sv_ffn[l], lat_mods[l], w_ffn[l], dx, dy)
        dm.update(d)
        if not last:
            cdm["g2"] = cdgate
            (cdx, cdy, cdm["g1"]), cgf, d = _ffn_half_bwd(f"l{l}c", csv_ffn[l], ctx_mods[l], w_ffn[l], cdx, cdy)
            cdm.update(d)
            gf = {k: gf[k] + cgf[k] for k in gf}
        ffn_grads = {k: gf[k] for k in _ShardedWeights.FFN}
        (dx, dy, dgate), gm, d, dkv, got = _mix_half_bwd(f"l{l}", sv_mix[l], lat_mods[l], w_mix[l], dx, dy, None, local=True,
                                                        kv_only=False, exch=big.ride_attn_bwd(l, ffn_grads, up_mix))
        big.took(l, ffn_grads, up_mix, got)
        dm.update(d)
        (cdx, cdy, cdgate), cgm, d, _, _ = _mix_half_bwd(f"l{l}c", csv_mix[l], ctx_mods[l], w_mix[l], cdx, cdy, dkv,
                                                        local=False, kv_only=last)
        cdm.update(d)
        order = ("sh1", "sc1", "g1", "sh2", "sc2", "g2")
        dm_loc[l] = jnp.concatenate([dm[k][:, 0, :] for k in order], axis=-1)
        dm_ctx[l] = jnp.concatenate([jnp.sum(cdm[k][:, 0, :], axis=0) if k in cdm else jnp.zeros((D_MODEL,), F32)
                                     for k in order])
        if last:
            up_mix = dict(w_in=gm["w_in"].at[:, ATTN_W:ATTN_W + 2 * KV_W].add(cgm["w_in_kv"]), w_out=gm["w_out"])
            margs = gm["margs"]
        else:
            up_mix = {k: gm[k] + cgm[k] for k in _ShardedWeights.MIX}
            margs = tuple(a + b for a, b in zip(gm["margs"], cgm["margs"]))
        small[l] = dict(n1=gm["n1"] + cgm["n1"], n2=gf["n2"], margs=margs)
    big.leftover(up_mix)

    stack = lambda f: jnp.stack([f(small[l]) for l in range(depth)])
    dp = dict(
        norm1_g=stack(lambda g: g["n1"][0]), norm2_g=stack(lambda g: g["n2"][0]),
        attn_sink=stack(lambda g: g["margs"][0][0]), conv_dw=stack(lambda g: g["margs"][1]),
        conv_dw_b=stack(lambda g: g["margs"][2][0]), conv_ln_g=stack(lambda g: g["margs"][3][0]),
        conv_ln_b=stack(lambda g: g["margs"][4][0]), pool_w=stack(lambda g: _diag_blocks(g["margs"][5])),
        pool_scale=stack(lambda g: g["margs"][6][0]))
    return jnp.sum(loss_rows[:, 0, 0]), dx, jnp.stack(dm_loc), jnp.stack(dm_ctx), dp, dfinal[0]


PACK_COLS = 1024


def _pack(arrs):
    flat = jnp.concatenate([a.reshape(-1).astype(F32) for a in arrs])
    rows = -(-flat.shape[0] // (8 * PACK_COLS)) * 8
    return jnp.pad(flat, (0, rows * PACK_COLS - flat.shape[0])).reshape(rows, PACK_COLS)


def _unpack(slab, like):
    flat = slab.reshape(-1)
    out, off = [], 0
    for a in like:
        out.append(flat[off:off + a.size].reshape(a.shape))
        off += a.size
    return out


def _shard_cols(gathered):
    _, L, R, C = gathered.shape
    return jnp.transpose(gathered, (1, 2, 0, 3)).reshape(L, R, N_DEV * C)


class _ShardedWeights:
    MIX = ("w_in", "w_out")
    FFN = ("w_ffn_in", "w_ffn_out")
    BY_COLS = ("w_in", "w_ffn_in")

    def __init__(self, shards, first):
        self.shards = shards
        self.first = first
        self.depth = shards[BIG_W[0]].shape[0]
        self.parts = [dict() for _ in range(self.depth)]
        self.left = None

    def _join(self, names, blocks):
        out = {}
        for name, g in zip(names, blocks):
            _, R, C = g.shape
            out[name] = jnp.transpose(g, (1, 0, 2)).reshape(R, N_DEV * C) if name in self.BY_COLS else g.reshape(N_DEV * R, C)
        return out

    def cut(self, names, grads):
        out = []
        for name in names:
            g = grads[name]
            if name in self.BY_COLS:
                R, C8 = g.shape
                out.append(jnp.transpose(g.reshape(R, N_DEV, C8 // N_DEV), (1, 0, 2)))
            else:
                R8, C = g.shape
                out.append(g.reshape(N_DEV, R8 // N_DEV, C))
        return out

    def mix_weights(self, l, got):
        return self._join(self.MIX, self.first if l == 0 else got)

    def ffn_weights(self, l, got):
        return self._join(self.FFN, got)

    def ride_attn_fwd(self, l):
        return [self.shards[name][l] for name in self.FFN], False

    def ride_ffn_fwd(self, l):
        if l + 1 >= self.depth:
            return None
        return [self.shards[name][l + 1] for name in self.MIX], False

    def ride_attn_bwd(self, l, ffn_grads, up_mix):
        return self.cut(self.FFN, ffn_grads) + (self.cut(self.MIX, up_mix) if up_mix is not None else []), True

    def took(self, l, ffn_grads, up_mix, got):
        self.parts[l].update(zip(self.FFN, got[:2]))
        if up_mix is not None:
            self.parts[l + 1].update(zip(self.MIX, got[2:]))

    def leftover(self, mix_grads):
        self.left = mix_grads


def _as_rows(a, leading=0):
    return a.reshape(*a.shape[:leading], -1, PACK_COLS)


SMALL = ("c_ctx", "b_mod", "norm1_g", "norm2_g", "conv_dw_b", "conv_ln_g", "conv_ln_b", "attn_sink", "pool_w",
         "pool_scale", "final_g", "conv_dw")
BIG = ("w_mod", "w_in", "w_out", "w_ffn_in", "w_ffn_out")
ORDER = ("c_ctx", "w_mod", "b_mod", "norm1_g", "norm2_g", "w_in", "conv_dw", "conv_dw_b", "conv_ln_g", "conv_ln_b",
         "attn_sink", "pool_w", "pool_scale", "w_out", "w_ffn_in", "w_ffn_out", "final_g")


def kernel(x, c, ctx, c_ctx, w_mod, b_mod, norm1_g, norm2_g, w_in, conv_dw, conv_dw_b, conv_ln_g, conv_ln_b, attn_sink, pool_w, pool_scale, w_out, w_ffn_in, w_ffn_out, final_g, loss_target, m_c_ctx, m_w_mod, m_b_mod, m_norm1_g, m_norm2_g, m_w_in, m_conv_dw, m_conv_dw_b, m_conv_ln_g, m_conv_ln_b, m_attn_sink, m_pool_w, m_pool_scale, m_w_out, m_w_ffn_in, m_w_ffn_out, m_final_g, v_c_ctx, v_w_mod, v_b_mod, v_norm1_g, v_norm2_g, v_w_in, v_conv_dw, v_conv_dw_b, v_conv_ln_g, v_conv_ln_b, v_attn_sink, v_pool_w, v_pool_scale, v_w_out, v_w_ffn_in, v_w_ffn_out, v_final_g):
    w = dict(c_ctx=c_ctx, w_mod=w_mod, b_mod=b_mod, norm1_g=norm1_g, norm2_g=norm2_g, w_in=w_in, conv_dw=conv_dw,
             conv_dw_b=conv_dw_b, conv_ln_g=conv_ln_g, conv_ln_b=conv_ln_b, attn_sink=attn_sink, pool_w=pool_w,
             pool_scale=pool_scale, w_out=w_out, w_ffn_in=w_ffn_in, w_ffn_out=w_ffn_out, final_g=final_g)
    mom = dict(c_ctx=m_c_ctx, w_mod=m_w_mod, b_mod=m_b_mod, norm1_g=m_norm1_g, norm2_g=m_norm2_g, w_in=m_w_in,
               conv_dw=m_conv_dw, conv_dw_b=m_conv_dw_b, conv_ln_g=m_conv_ln_g, conv_ln_b=m_conv_ln_b,
               attn_sink=m_attn_sink, pool_w=m_pool_w, pool_scale=m_pool_scale, w_out=m_w_out, w_ffn_in=m_w_ffn_in,
               w_ffn_out=m_w_ffn_out, final_g=m_final_g)
    var = dict(c_ctx=v_c_ctx, w_mod=v_w_mod, b_mod=v_b_mod, norm1_g=v_norm1_g, norm2_g=v_norm2_g, w_in=v_w_in,
               conv_dw=v_conv_dw, conv_dw_b=v_conv_dw_b, conv_ln_g=v_conv_ln_g, conv_ln_b=v_conv_ln_b,
               attn_sink=v_attn_sink, pool_w=v_pool_w, pool_scale=v_pool_scale, w_out=v_w_out, w_ffn_in=v_w_ffn_in,
               w_ffn_out=v_w_ffn_out, final_g=v_final_g)
    B = x.shape[0]
    depth = w_mod.shape[0]
    mod_cols = w_mod.shape[2]
    dw_cols = conv_dw.shape[2]
    me = 4 * lax.axis_index("x") + 2 * lax.axis_index("y") + lax.axis_index("c")

    shards = {name: w[name].astype(BF16) for name in BIG_W}
    c_all, dw_all, *first = _exchange([c, conv_dw] + [shards[name][0] for name in _ShardedWeights.MIX], scatter=False,
                                      name="gather_first")
    big = _ShardedWeights(shards, first)
    p = dict(norm1_g=norm1_g, norm2_g=norm2_g, conv_dw=_shard_cols(dw_all), conv_dw_b=conv_dw_b, conv_ln_g=conv_ln_g,
             conv_ln_b=conv_ln_b, attn_sink=attn_sink, pool_w=pool_w, pool_scale=pool_scale)

    cc = jnp.concatenate([c_all.reshape(N_DEV * B, D_MODEL), jnp.broadcast_to(c_ctx[None, :], (N_DEV, D_MODEL)),
                          jnp.zeros((MOD_ROWS - N_DEV * B - N_DEV, D_MODEL), F32)], axis=0)
    b_shard = lax.dynamic_slice_in_dim(b_mod, me * mod_cols, mod_cols, axis=1)[:, None, :]
    m_part = _mod_fwd(cc, w_mod, b_shard, name="mod_fwd")
    m_all, = _exchange([m_part], scatter=False, name="gather_mod")
    m_full = _shard_cols(m_all)
    m_loc = lax.dynamic_slice_in_dim(m_full, me * B, B, axis=1)
    m_ctx = m_full[:, N_DEV * B, :]

    loss_part, dx, dm_loc, dm_ctx, dp, dfinal = _local_step(x, ctx, m_loc, m_ctx, p, final_g, loss_target, big)
    loss = lax.psum(loss_part, AXES)

    dm_rows = jnp.concatenate([dm_loc, dm_ctx[:, None, :], jnp.zeros((depth, 8 - B - 1, 6 * D_MODEL), F32)], axis=1)
    dm_all, = _exchange([dm_rows], scatter=False, name="gather_dmod")
    dm_full = jnp.concatenate([
        jnp.transpose(dm_all[:, :, :B, :], (1, 0, 2, 3)).reshape(depth, N_DEV * B, 6 * D_MODEL),
        jnp.transpose(dm_all[:, :, B, :], (1, 0, 2)),
        jnp.zeros((depth, MOD_ROWS - N_DEV * B - N_DEV, 6 * D_MODEL), F32)], axis=1)
    g_b_mod = jnp.stack([_sum_leading(dm_full[l][:, None, :], name=f"b_mod_grad{l}")[0] for l in range(depth)])
    dm_mine = lax.dynamic_slice_in_dim(dm_full, me * mod_cols, mod_cols, axis=2)
    dcc, g_w_mod = _mod_bwd(cc, w_mod, b_shard, dm_mine, name="mod_bwd")
    g_c_ctx_part = jnp.sum(dcc[N_DEV * B:N_DEV * B + N_DEV], axis=0)

    small_like = [c_ctx, norm1_g, norm2_g, conv_dw_b, conv_ln_g, conv_ln_b, attn_sink, pool_w, pool_scale, final_g,
                  dp["conv_dw"]]
    small_part = _pack([g_c_ctx_part, dp["norm1_g"], dp["norm2_g"], dp["conv_dw_b"], dp["conv_ln_g"], dp["conv_ln_b"],
                        dp["attn_sink"], dp["pool_w"], dp["pool_scale"], dfinal, dp["conv_dw"]])
    small_all, = _exchange([small_part], scatter=False, name="gather_small")
    small_sum = _unpack(_sum_leading(small_all, name="sum_small"), small_like)
    g = dict(zip(("c_ctx", "norm1_g", "norm2_g", "conv_dw_b", "conv_ln_g", "conv_ln_b", "attn_sink", "pool_w",
                  "pool_scale", "final_g"), small_sum[:-1]))
    g["b_mod"] = g_b_mod
    g["conv_dw"] = lax.dynamic_slice_in_dim(small_sum[-1], me * dw_cols, dw_cols, axis=2)

    big.parts[0].update(zip(big.MIX, _exchange(big.cut(big.MIX, big.left), scatter=True, name="scatter_last")))

    delta, new_m, new_v = {}, {}, {}
    for name in BIG_W:
        g[name], delta[name], new_m[name], new_v[name] = _adamw(
            [big.parts[l][name] for l in range(depth)], w[name], mom[name], var[name], name=f"adamw_{name}")
    g["w_mod"], delta["w_mod"], new_m["w_mod"], new_v["w_mod"] = _adamw(g_w_mod, w_mod, m_w_mod, v_w_mod, name="adamw_w_mod")
    res = _adamw_small([g[k] for k in SMALL], [w[k] for k in SMALL], [mom[k] for k in SMALL], [var[k] for k in SMALL],
                       name="adamw_small")
    for dst, arrs in zip((delta, new_m, new_v), res):
        dst.update(zip(SMALL, arrs))

    return (loss, dx, *[g[k] for k in ORDER], *[delta[k] for k in ORDER], *[new_m[k] for k in ORDER],
            *[new_v[k] for k in ORDER])
```

```python
import functools

import numpy as np
import jax
import jax.numpy as jnp
from jax import lax
from jax.experimental import pallas as pl
from jax.experimental.pallas import tpu as pltpu

F32 = jnp.float32
BF16 = jnp.bfloat16

D_MODEL = 1024
GRID_W = 64
HEAD_DIM = 64
ATTN_W = 512
CONV_W = 256
POOL_W = 256
ATTN_HEADS = 8
KV_HEADS = 2
GROUP = ATTN_HEADS // KV_HEADS
KV_W = KV_HEADS * HEAD_DIM
IN_W = ATTN_W + 2 * KV_W + 2 * CONV_W + POOL_W
WINDOW = 128
Q_BLOCK = 128
SPAN = Q_BLOCK + 2 * WINDOW
CONV_KERNEL = 31
POOL_WINDOWS = (2, 4, 8, 16)
POOL_GROUP = 64
ROPE_BASE = 10000.0
D_FF = 2816
EPS = 1e-6
NEG = -1e30
N_DEV = 8
AXES = ("x", "y", "c")

ADAM_LR = 0.001
ADAM_B1 = 0.9
ADAM_B2 = 0.999
ADAM_EPS = 1e-08
ADAM_WD = 0.01
ADAM_STEP = 10

VMEM_LIMIT = 56 * 1024 * 1024
HALO = 16
SEQ_CHUNK = 256


def _params(*sem):
    return pltpu.CompilerParams(dimension_semantics=sem, vmem_limit_bytes=VMEM_LIMIT)


def _tile(dim, target):
    if dim <= target:
        return dim
    t = (target // 128) * 128
    while t >= 128:
        if dim % t == 0:
            return t
        t -= 128
    raise ValueError(f"no tile for {dim}")


MM_K_WHOLE = 5632


def _dot(a, b):
    return lax.dot_general(a.astype(BF16), b.astype(BF16), (((1,), (0,)), ((), ())), preferred_element_type=F32)


def _mm(a, b, *, name, out_dtype=F32, tm=1408, tn=512, tk=2048):
    M, K = a.shape
    K2, N = b.shape
    assert K == K2, (a.shape, b.shape)
    tm = _tile(M, tm)
    tn = _tile(N, tn)
    tk = K if K <= MM_K_WHOLE else _tile(K, tk)
    nk = K // tk

    def body(a_ref, b_ref, o_ref, *scratch):
        part = _dot(a_ref[...], b_ref[...])
        if nk == 1:
            o_ref[...] = part.astype(o_ref.dtype)
        else:
            acc_ref, = scratch
            k = pl.program_id(2)

            @pl.when(k == 0)
            def _():
                acc_ref[...] = part

            @pl.when(k > 0)
            def _():
                acc_ref[...] += part

            @pl.when(k == nk - 1)
            def _():
                o_ref[...] = acc_ref[...].astype(o_ref.dtype)

    return pl.pallas_call(
        body, name=name, grid=(M // tm, N // tn, nk),
        in_specs=[pl.BlockSpec((tm, tk), lambda i, j, k: (i, k)), pl.BlockSpec((tk, tn), lambda i, j, k: (k, j))],
        out_specs=pl.BlockSpec((tm, tn), lambda i, j, k: (i, j)),
        out_shape=jax.ShapeDtypeStruct((M, N), out_dtype),
        scratch_shapes=[pltpu.VMEM((tm, tn), F32)] if nk > 1 else [],
        compiler_params=_params("parallel", "parallel", "arbitrary"),
    )(a, b)


FF_TILE = 256


def _interleave_ffn(w):
    tiles = D_FF // FF_TILE
    cols = [w[..., half * D_FF + j * FF_TILE:half * D_FF + (j + 1) * FF_TILE] for j in range(tiles) for half in range(2)]
    return jnp.concatenate(cols, axis=-1)


def _deinterleave_ffn(w):
    tiles = D_FF // FF_TILE
    cols = [w[..., (2 * j + half) * FF_TILE:(2 * j + half + 1) * FF_TILE] for half in range(2) for j in range(tiles)]
    return jnp.concatenate(cols, axis=-1)


def _swiglu(gu):
    g, u = gu[:, :FF_TILE], gu[:, FF_TILE:]
    return g * jax.nn.sigmoid(g) * u


def _mm_swiglu(a, w_il, *, name, tm=1024, exch=None):
    M, K = a.shape
    tm = _tile(M, tm)

    def body(a_ref, b_ref, gu_ref, act_ref, act_t_ref):
        gu = _dot(a_ref[...], b_ref[...])
        gu_ref[...] = gu.astype(gu_ref.dtype)
        act = _swiglu(gu)
        act_ref[...] = act.astype(act_ref.dtype)
        act_t_ref[...] = act.T.astype(act_t_ref.dtype)

    grid = (M // tm, D_FF // FF_TILE)
    body, x_in, x_out, x_shapes, x_sems = _riding(body, exch, 2, 3, grid)
    return pl.pallas_call(
        body, name=name, grid=grid,
        in_specs=[pl.BlockSpec((tm, K), lambda i, j: (i, 0)), pl.BlockSpec((K, 2 * FF_TILE), lambda i, j: (0, j))] + x_in,
        out_specs=[pl.BlockSpec((tm, 2 * FF_TILE), lambda i, j: (i, j)), pl.BlockSpec((tm, FF_TILE), lambda i, j: (i, j)),
                   pl.BlockSpec((FF_TILE, tm), lambda i, j: (j, i))] + x_out,
        out_shape=[jax.ShapeDtypeStruct((M, 2 * D_FF), BF16), jax.ShapeDtypeStruct((M, D_FF), BF16),
                   jax.ShapeDtypeStruct((D_FF, M), BF16)] + x_shapes,
        scratch_shapes=x_sems,
        compiler_params=_params("arbitrary", "arbitrary") if exch else _params("parallel", "parallel"),
    )(a, w_il, *(exch[0] if exch else []))


def _mm_dswiglu(dy, w_out_t, gu, *, name, tm=1024):
    M, K = dy.shape
    tm = _tile(M, tm)

    def body(dy_ref, b_ref, gu_ref, o_ref):
        dact = _dot(dy_ref[...], b_ref[...])
        _, vjp = jax.vjp(_swiglu, gu_ref[...].astype(F32))
        o_ref[...] = vjp(dact)[0].astype(o_ref.dtype)

    return pl.pallas_call(
        body, name=name, grid=(M // tm, D_FF // FF_TILE),
        in_specs=[pl.BlockSpec((tm, K), lambda i, j: (i, 0)), pl.BlockSpec((K, FF_TILE), lambda i, j: (0, j)),
                  pl.BlockSpec((tm, 2 * FF_TILE), lambda i, j: (i, j))],
        out_specs=pl.BlockSpec((tm, 2 * FF_TILE), lambda i, j: (i, j)),
        out_shape=jax.ShapeDtypeStruct((M, 2 * D_FF), BF16),
        compiler_params=_params("parallel", "parallel"),
    )(dy, w_out_t, gu)


def _rope_tables(n):
    rows = n // GRID_W
    row = jnp.repeat(jnp.arange(rows), GRID_W).astype(F32)
    col = jnp.tile(jnp.arange(GRID_W), rows).astype(F32)
    half = HEAD_DIM // 2
    inv = ROPE_BASE ** (-jnp.arange(0, half, 2, dtype=F32) / half)
    ar = row[:, None] * inv
    ac = col[:, None] * inv
    ang = jnp.concatenate([ar, ar, ac, ac], axis=-1)
    return jnp.cos(ang), jnp.sin(ang)


def _rot_half(x):
    w = x.shape[-1]
    lane = lax.broadcasted_iota(jnp.int32, x.shape, 1)
    up = pltpu.roll(x, w - 16, 1)
    down = pltpu.roll(x, 16, 1)
    return jnp.where((lane & 16) == 0, -up, down)


def _rope(x, cos, sin):
    return x * cos + _rot_half(x) * sin


def _rope_bwd(d, cos, sin):
    return d * cos - _rot_half(d * sin)


def _dot_nt(a, b):
    return lax.dot_general(a.astype(BF16), b.astype(BF16), (((1,), (1,)), ((), ())), preferred_element_type=F32)


def _softmax_sink(s, sink_rows):
    mx = jnp.maximum(jnp.max(s, axis=1, keepdims=True), sink_rows)
    e = jnp.exp(s - mx)
    es = jnp.exp(sink_rows - mx)
    inv = 1.0 / (jnp.sum(e, axis=1, keepdims=True) + es)
    return e * inv, es * inv


def _attn_operands(q_ref, k_ref, v_ref, kc_ref, vc_ref, cq_ref, sq_ref, ck_ref, sk_ref, i, n, n_ctx, local):
    q = q_ref[...]
    k_all, v_all, bias, s0, ck, sk = kc_ref[...], vc_ref[...], None, None, None, None
    if local:
        start, s0 = _span_start(i, n)
        ck = ck_ref[pl.ds(s0, SPAN), :]
        sk = sk_ref[pl.ds(s0, SPAN), :]
        q = _rope(q, cq_ref[...], sq_ref[...])
        k_all = jnp.concatenate([k_all, _rope(k_ref[pl.ds(s0, SPAN), :], ck, sk)], axis=0)
        v_all = jnp.concatenate([v_all, v_ref[pl.ds(s0, SPAN), :]], axis=0)
        bias = _window_bias(start, s0, n_ctx)
    q = (q * (HEAD_DIM ** -0.5)).astype(BF16)
    return q, k_all.astype(BF16), v_all.astype(BF16), bias, s0, ck, sk


def _stack_heads(x, kh):
    return jnp.concatenate([x[:, (GROUP * kh + g) * HEAD_DIM:(GROUP * kh + g + 1) * HEAD_DIM] for g in range(GROUP)], axis=0)


def _sink_rows(sink, kh):
    return jnp.concatenate([jnp.broadcast_to(sink[:, GROUP * kh + g:GROUP * kh + g + 1], (Q_BLOCK, 1)) for g in range(GROUP)], axis=0)


def _window_bias(start, s0, n_ctx):
    r = lax.broadcasted_iota(jnp.int32, (Q_BLOCK, n_ctx + SPAN), 0)
    c = lax.broadcasted_iota(jnp.int32, (Q_BLOCK, n_ctx + SPAN), 1)
    ok = (c < n_ctx) | (jnp.abs(start - s0 + r - (c - n_ctx)) <= WINDOW)
    return jnp.concatenate([jnp.where(ok, 0.0, NEG).astype(F32)] * GROUP, axis=0)


def _span_start(i, n):
    start = i * Q_BLOCK
    s0 = jnp.clip(start - WINDOW, 0, n - SPAN)
    return start, pl.multiple_of(s0, Q_BLOCK)


def _riding(body, exch, n_in, n_out, grid):
    if exch is None:
        return body, [], [], [], []
    arrs, scatter = exch
    k = len(arrs)

    def wrapped(*refs):
        ins, xin = refs[:n_in], refs[n_in:n_in + k]
        outs, xout = refs[n_in + k:n_in + k + n_out], refs[n_in + k + n_out:n_in + 2 * k + n_out]
        sems = refs[n_in + 2 * k + n_out:]
        b, i = pl.program_id(0), pl.program_id(1)

        @pl.when((b == 0) & (i == 0))
        def _():
            _exch_start(xin, xout, sems, scatter)

        body(*ins, *outs)

        @pl.when((b == grid[0] - 1) & (i == grid[1] - 1))
        def _():
            _exch_wait(xin, xout, sems, scatter)

    any_spec = pl.BlockSpec(memory_space=pl.ANY)
    return wrapped, [any_spec] * k, [any_spec] * k, _exch_out_shapes(arrs, scatter), _exch_sems(k)


def _attn_fwd(u, kvc, sink, cos, sin, *, local, name, exch=None):
    B, n, _ = u.shape
    n_ctx = kvc.shape[1]
    nb = n // Q_BLOCK
    assert (not local) or n >= SPAN

    def body(q_ref, k_ref, v_ref, kc_ref, vc_ref, sink_ref, cq_ref, sq_ref, ck_ref, sk_ref, o_ref):
        q, k_all, v_all, bias, _, _, _ = _attn_operands(q_ref, k_ref, v_ref, kc_ref, vc_ref, cq_ref, sq_ref, ck_ref, sk_ref,
                                                        pl.program_id(1), n, n_ctx, local)
        sink_v = sink_ref[...]
        sl = lambda kh: slice(kh * HEAD_DIM, (kh + 1) * HEAD_DIM)
        ss = [_dot_nt(_stack_heads(q, kh), k_all[:, sl(kh)]) for kh in range(KV_HEADS)]
        ps = [_softmax_sink(s if bias is None else s + bias, _sink_rows(sink_v, kh))[0].astype(BF16) for kh, s in enumerate(ss)]
        for kh, p in enumerate(ps):
            o = _dot(p, v_all[:, sl(kh)])
            for g in range(GROUP):
                h = GROUP * kh + g
                o_ref[:, h * HEAD_DIM:(h + 1) * HEAD_DIM] = o[g * Q_BLOCK:(g + 1) * Q_BLOCK, :]

    seq = lambda blk: pl.BlockSpec((None, n, KV_W), lambda b, i: (b, 0, blk))
    ctxs = lambda blk: pl.BlockSpec((None, n_ctx, KV_W), lambda b, i: (b, 0, blk))
    full = lambda a: pl.BlockSpec(a.shape, lambda b, i: (0,) * a.ndim)
    cos_q, sin_q = jnp.tile(cos, (1, ATTN_HEADS)), jnp.tile(sin, (1, ATTN_HEADS))
    cos_k, sin_k = jnp.tile(cos, (1, KV_HEADS)), jnp.tile(sin, (1, KV_HEADS))
    body, x_in, x_out, x_shapes, x_sems = _riding(body, exch, 10, 1, (B, nb))
    return pl.pallas_call(
        body, name=name, grid=(B, nb),
        in_specs=[pl.BlockSpec((None, Q_BLOCK, ATTN_W), lambda b, i: (b, i, 0)),
                  seq(ATTN_W // KV_W), seq(ATTN_W // KV_W + 1), ctxs(0), ctxs(1), full(sink),
                  pl.BlockSpec((Q_BLOCK, ATTN_W), lambda b, i: (i, 0)), pl.BlockSpec((Q_BLOCK, ATTN_W), lambda b, i: (i, 0)),
                  full(cos_k), full(sin_k)] + x_in,
        out_specs=[pl.BlockSpec((None, Q_BLOCK, ATTN_W), lambda b, i: (b, i, 0))] + x_out,
        out_shape=[jax.ShapeDtypeStruct((B, n, ATTN_W), F32)] + x_shapes,
        scratch_shapes=x_sems,
        compiler_params=_params("arbitrary", "arbitrary"),
    )(u, u, u, kvc, kvc, sink, cos_q, sin_q, cos_k, sin_k, *(exch[0] if exch else []))


def _attn_bwd(u, kvc, sink, cos, sin, do_src, do_blk, *, local, name, exch=None):
    B, n, _ = u.shape
    n_ctx = kvc.shape[1]
    nb = n // Q_BLOCK

    def body(q_ref, k_ref, v_ref, kc_ref, vc_ref, sink_ref, cq_ref, sq_ref, ck_ref, sk_ref, do_ref,
             dq_ref, dk_ref, dv_ref, dkc_ref, dvc_ref, dsink_ref):
        b = pl.program_id(0)
        i = pl.program_id(1)

        @pl.when(i == 0)
        def _():
            dk_ref[...] = jnp.zeros_like(dk_ref)
            dv_ref[...] = jnp.zeros_like(dv_ref)
            dkc_ref[...] = jnp.zeros_like(dkc_ref)
            dvc_ref[...] = jnp.zeros_like(dvc_ref)

        @pl.when((i == 0) & (b == 0))
        def _():
            dsink_ref[...] = jnp.zeros_like(dsink_ref)

        q, k_all, v_all, bias, s0, ck, sk = _attn_operands(q_ref, k_ref, v_ref, kc_ref, vc_ref, cq_ref, sq_ref, ck_ref, sk_ref,
                                                           i, n, n_ctx, local)
        do = do_ref[...].astype(BF16)
        sink_v = sink_ref[...]
        sl = lambda kh: slice(kh * HEAD_DIM, (kh + 1) * HEAD_DIM)
        heads = range(KV_HEADS)
        q_st = [_stack_heads(q, kh) for kh in heads]
        do_st = [_stack_heads(do, kh) for kh in heads]
        ss = [_dot_nt(q_st[kh], k_all[:, sl(kh)]) for kh in heads]
        dps = [_dot_nt(do_st[kh], v_all[:, sl(kh)]) for kh in heads]
        p_bf, ds_bf = [], []
        dsink = jnp.zeros((1, ATTN_HEADS), F32)
        lane8 = lax.broadcasted_iota(jnp.int32, (1, ATTN_HEADS), 1)
        for kh in heads:
            p, p_sink = _softmax_sink(ss[kh] if bias is None else ss[kh] + bias, _sink_rows(sink_v, kh))
            delta = jnp.sum(p * dps[kh], axis=1, keepdims=True)
            ds = p * (dps[kh] - delta)
            dsr = -(p_sink * delta)
            for g in range(GROUP):
                dsink = dsink + jnp.where(lane8 == GROUP * kh + g, jnp.sum(dsr[g * Q_BLOCK:(g + 1) * Q_BLOCK, :]), 0.0)
            p_bf.append(p.astype(BF16))
            ds_bf.append(ds.astype(BF16))
        over_rows = (((0,), (0,)), ((), ()))
        dks, dvs = [], []
        for kh in heads:
            dq_st = _dot(ds_bf[kh], k_all[:, sl(kh)]) * (HEAD_DIM ** -0.5)
            for g in range(GROUP):
                h = GROUP * kh + g
                dq_ref[:, h * HEAD_DIM:(h + 1) * HEAD_DIM] = dq_st[g * Q_BLOCK:(g + 1) * Q_BLOCK, :]
            dvs.append(lax.dot_general(p_bf[kh], do_st[kh], over_rows, preferred_element_type=F32))
            dks.append(lax.dot_general(ds_bf[kh], q_st[kh], over_rows, preferred_element_type=F32))
        dk_cat = jnp.concatenate(dks, axis=1)
        dv_cat = jnp.concatenate(dvs, axis=1)
        dsink_ref[...] += dsink
        dkc_ref[...] += dk_cat[:n_ctx, :]
        dvc_ref[...] += dv_cat[:n_ctx, :]
        if local:
            dq_ref[...] = _rope_bwd(dq_ref[...], cq_ref[...], sq_ref[...])
            dk_ref[pl.ds(s0, SPAN), :] += _rope_bwd(dk_cat[n_ctx:, :], ck, sk)
            dv_ref[pl.ds(s0, SPAN), :] += dv_cat[n_ctx:, :]

    seq = lambda blk: pl.BlockSpec((None, n, KV_W), lambda b, i: (b, 0, blk))
    ctxs = lambda blk: pl.BlockSpec((None, n_ctx, KV_W), lambda b, i: (b, 0, blk))
    full = lambda a: pl.BlockSpec(a.shape, lambda b, i: (0,) * a.ndim)
    qblk = lambda blk: pl.BlockSpec((None, Q_BLOCK, ATTN_W), lambda b, i: (b, i, blk))
    cos_q, sin_q = jnp.tile(cos, (1, ATTN_HEADS)), jnp.tile(sin, (1, ATTN_HEADS))
    cos_k, sin_k = jnp.tile(cos, (1, KV_HEADS)), jnp.tile(sin, (1, KV_HEADS))
    acc = lambda rows: pl.BlockSpec((None, rows, KV_W), lambda b, i: (b, 0, 0))
    body, x_in, x_out, x_shapes, x_sems = _riding(body, exch, 11, 6, (B, nb))
    return pl.pallas_call(
        body, name=name, grid=(B, nb),
        in_specs=[qblk(0), seq(ATTN_W // KV_W), seq(ATTN_W // KV_W + 1), ctxs(0), ctxs(1), full(sink),
                  pl.BlockSpec((Q_BLOCK, ATTN_W), lambda b, i: (i, 0)), pl.BlockSpec((Q_BLOCK, ATTN_W), lambda b, i: (i, 0)),
                  full(cos_k), full(sin_k), qblk(do_blk)] + x_in,
        out_specs=[qblk(0), acc(n), acc(n), acc(n_ctx), acc(n_ctx), pl.BlockSpec((1, ATTN_HEADS), lambda b, i: (0, 0))] + x_out,
        out_shape=[jax.ShapeDtypeStruct((B, n, ATTN_W), F32), jax.ShapeDtypeStruct((B, n, KV_W), F32),
                   jax.ShapeDtypeStruct((B, n, KV_W), F32), jax.ShapeDtypeStruct((B, n_ctx, KV_W), F32),
                   jax.ShapeDtypeStruct((B, n_ctx, KV_W), F32), jax.ShapeDtypeStruct((1, ATTN_HEADS), F32)] + x_shapes,
        scratch_shapes=x_sems,
        compiler_params=_params("arbitrary", "arbitrary"),
    )(u, u, u, kvc, kvc, sink, cos_q, sin_q, cos_k, sin_k, do_src, *(exch[0] if exch else []))


def _conv_chunk(s, n, a_ext, g_ext, dw, dw_b, ln_g, ln_b):
    del s, n
    r = a_ext.shape[0] - 2 * HALO
    h = a_ext * jax.nn.sigmoid(g_ext)
    acc = jnp.broadcast_to(dw_b, (r, CONV_W))
    first = HALO - CONV_KERNEL // 2
    span = r + 8 * ((first + CONV_KERNEL - 1) // 8)
    shifted = [h[b:b + span, :] for b in range(8)]
    for k in range(CONV_KERNEL):
        o = first + k
        acc = acc + shifted[o % 8][o - o % 8:o - o % 8 + r, :] * dw[k:k + 1, :]
    mu = jnp.mean(acc, axis=-1, keepdims=True)
    var = jnp.mean(jnp.square(acc - mu), axis=-1, keepdims=True)
    hn = (acc - mu) * lax.rsqrt(var + EPS) * ln_g + ln_b
    return hn * jax.nn.sigmoid(hn)


def _pool_chunk(s, n, p_ext, w_bd, scale):
    r = p_ext.shape[0] - 2 * HALO
    lane = lax.broadcasted_iota(jnp.int32, (1, POOL_W), 1)
    win = jnp.left_shift(2, lane // POOL_GROUP)
    half = win // 2
    acc = jnp.zeros((r, POOL_W), F32)
    for d in range(-(POOL_WINDOWS[-1] // 2), POOL_WINDOWS[-1] - POOL_WINDOWS[-1] // 2):
        inside = (d >= -half) & (d <= win - 1 - half)
        acc = acc + jnp.where(inside, p_ext[HALO + d:HALO + d + r, :], 0.0)
    t = s + lax.broadcasted_iota(jnp.int32, (r, 1), 0)
    lo = jnp.maximum(t - half, 0)
    hi = jnp.minimum(t + win - 1 - half, n - 1)
    y = acc / (hi - lo + 1).astype(F32) - p_ext[HALO:HALO + r, :]
    out = lax.dot_general(y.astype(BF16), w_bd.astype(BF16), (((1,), (0,)), ((), ())), preferred_element_type=F32)
    return out * scale


def _seq_specs(rows, params):
    specs = [pl.BlockSpec((None, a.shape[1], w), functools.partial(lambda b, blk: (b, 0, blk), blk=blk)) for a, w, blk in rows]
    specs += [pl.BlockSpec(p.shape, functools.partial(lambda b, nd: (0,) * nd, nd=p.ndim)) for p in params]
    return specs


def _fill_padded(pad_ref, row_ref, n):
    w = pad_ref.shape[1]
    pad_ref[pl.ds(0, HALO), :] = jnp.zeros((HALO, w), F32)
    pad_ref[pl.ds(HALO + n, HALO), :] = jnp.zeros((HALO, w), F32)
    pad_ref[pl.ds(HALO, n), :] = row_ref[...]


def _seq_fwd(fn, rows, params, out_w, *, name):
    B, n = rows[0][0].shape[:2]
    r = min(SEQ_CHUNK, n)
    nr, npar = len(rows), len(params)

    def body(*refs):
        row_refs, par_refs, o_ref, pads = refs[:nr], refs[nr:nr + npar], refs[nr + npar], refs[nr + npar + 1:]
        for rr, p in zip(row_refs, pads):
            _fill_padded(p, rr, n)
        pars = [p[...] for p in par_refs]

        def chunk(ci, carry):
            s = pl.multiple_of(ci * r, r)
            ext = [p[pl.ds(s, r + 2 * HALO), :] for p in pads]
            o_ref[pl.ds(s, r), :] = fn(s, n, *ext, *pars)
            return carry

        lax.fori_loop(0, n // r, chunk, 0)

    return pl.pallas_call(
        body, name=name, grid=(B,),
        in_specs=_seq_specs(rows, params),
        out_specs=pl.BlockSpec((None, n, out_w), lambda b: (b, 0, 0)),
        out_shape=jax.ShapeDtypeStruct((B, n, out_w), F32),
        scratch_shapes=[pltpu.VMEM((n + 2 * HALO, w), F32) for _, w, _ in rows],
        compiler_params=_params("parallel"),
    )(*[a for a, _, _ in rows], *params)


def _seq_bwd(fn, rows, params, dout, *, name):
    B, n = rows[0][0].shape[:2]
    r = min(SEQ_CHUNK, n)
    nr, npar = len(rows), len(params)

    def body(*refs):
        row_refs, par_refs, do_ref = refs[:nr], refs[nr:nr + npar], refs[nr + npar]
        outs = refs[nr + npar + 1:]
        drow_refs, dpar_refs = outs[:nr], outs[nr:nr + npar]
        scratch = outs[nr + npar:]
        pads, dpads = scratch[:nr], scratch[nr:]
        for rr, p, dp in zip(row_refs, pads, dpads):
            _fill_padded(p, rr, n)
            dp[...] = jnp.zeros_like(dp)

        @pl.when(pl.program_id(0) == 0)
        def _():
            for d in dpar_refs:
                d[...] = jnp.zeros_like(d)

        pars = [p[...] for p in par_refs]

        def chunk(ci, carry):
            s = pl.multiple_of(ci * r, r)
            ext = [p[pl.ds(s, r + 2 * HALO), :] for p in pads]
            _, vjp = jax.vjp(functools.partial(fn, s, n), *ext, *pars)
            grads = vjp(do_ref[pl.ds(s, r), :])
            for dp, g in zip(dpads, grads[:nr]):
                dp[pl.ds(s, r + 2 * HALO), :] += g
            for d, g in zip(dpar_refs, grads[nr:]):
                d[...] += g
            return carry

        lax.fori_loop(0, n // r, chunk, 0)
        for d, dp in zip(drow_refs, dpads):
            d[...] = dp[pl.ds(HALO, n), :]

    da, dw_, dblk = dout
    return pl.pallas_call(
        body, name=name, grid=(B,),
        in_specs=_seq_specs(rows, params) + [pl.BlockSpec((None, n, dw_), lambda b: (b, 0, dblk))],
        out_specs=[pl.BlockSpec((None, n, w), lambda b: (b, 0, 0)) for _, w, _ in rows]
        + [pl.BlockSpec(p.shape, functools.partial(lambda b, nd: (0,) * nd, nd=p.ndim)) for p in params],
        out_shape=[jax.ShapeDtypeStruct((B, n, w), F32) for _, w, _ in rows]
        + [jax.ShapeDtypeStruct(p.shape, F32) for p in params],
        scratch_shapes=[pltpu.VMEM((n + 2 * HALO, w), F32) for _, w, _ in rows] * 2,
        compiler_params=_params("arbitrary"),
    )(*[a for a, _, _ in rows], *params, da)


_CONV_A_BLK = (ATTN_W + 2 * KV_W) // CONV_W
_CONV_G_BLK = _CONV_A_BLK + 1
_POOL_BLK = _CONV_A_BLK + 2


def _mixer_fwd(tag, u, kvc, margs, local, exch=None):
    sink, dw, dw_b, ln_g, ln_b, w_bd, scale = margs
    cos, sin = _rope_tables(max(u.shape[1], GRID_W))
    attn, *got = _attn_fwd(u, kvc, sink, cos, sin, local=local, name=f"{tag}_attn_fwd", exch=exch)
    conv = _seq_fwd(_conv_chunk, [(u, CONV_W, _CONV_A_BLK), (u, CONV_W, _CONV_G_BLK)], [dw, dw_b, ln_g, ln_b], CONV_W,
                    name=f"{tag}_conv_fwd")
    pool = _seq_fwd(_pool_chunk, [(u, POOL_W, _POOL_BLK)], [w_bd, scale], POOL_W, name=f"{tag}_pool_fwd")
    return jnp.concatenate([attn, conv, pool], axis=-1).astype(BF16), got


def _mixer_bwd(tag, u, kvc, margs, dmix, local, exch=None):
    sink, dw, dw_b, ln_g, ln_b, w_bd, scale = margs
    cos, sin = _rope_tables(max(u.shape[1], GRID_W))
    dq, dk, dv, dkc, dvc, dsink, *got = _attn_bwd(u, kvc, sink, cos, sin, dmix, 0, local=local, name=f"{tag}_attn_bwd",
                                                  exch=exch)
    da, dg, ddw, ddw_b, dln_g, dln_b = _seq_bwd(
        _conv_chunk, [(u, CONV_W, _CONV_A_BLK), (u, CONV_W, _CONV_G_BLK)], [dw, dw_b, ln_g, ln_b],
        (dmix, CONV_W, ATTN_W // CONV_W), name=f"{tag}_conv_bwd")
    dpu, dw_bd, dscale = _seq_bwd(_pool_chunk, [(u, POOL_W, _POOL_BLK)], [w_bd, scale],
                                  (dmix, POOL_W, (ATTN_W + CONV_W) // POOL_W), name=f"{tag}_pool_bwd")
    return (dq, dk, dv, da, dg, dpu), (dkc, dvc), (dsink, ddw, ddw_b, dln_g, dln_b, dw_bd, dscale), got


def _row_specs(arrs, kinds, tr):
    specs = []
    for a, kind in zip(arrs, kinds):
        if kind == "row":
            specs.append(pl.BlockSpec((None, tr, a.shape[2]), lambda b, j: (b, j, 0)))
        elif kind == "batch":
            specs.append(pl.BlockSpec((None, 1, a.shape[2]), lambda b, j: (b, 0, 0)))
        else:
            specs.append(pl.BlockSpec(a.shape, functools.partial(lambda b, j, nd: (0,) * nd, nd=a.ndim)))
    return specs


def _rowwise_fwd(fn, ins, kinds, outs, tr, *, name, transposed=None):
    B, n = ins[0].shape[:2]
    ni, no = len(ins), len(outs)
    nj = n // tr

    def body(*refs):
        res = fn(*[r[...] for r in refs[:ni]])
        for o, v in zip(refs[ni:ni + no], res):
            o[...] = v.astype(o.dtype)
        if transposed is not None:
            refs[ni + no][...] = res[transposed].T.astype(refs[ni + no].dtype)

    out_specs = [pl.BlockSpec((None, tr, w), lambda b, j: (b, j, 0)) for w, _ in outs]
    out_shape = [jax.ShapeDtypeStruct((B, n, w), dt) for w, dt in outs]
    if transposed is not None:
        w, dt = outs[transposed]
        out_specs.append(pl.BlockSpec((w, tr), lambda b, j: (0, b * nj + j)))
        out_shape.append(jax.ShapeDtypeStruct((w, B * n), dt))
    return pl.pallas_call(
        body, name=name, grid=(B, nj),
        in_specs=_row_specs(ins, kinds, tr), out_specs=out_specs, out_shape=out_shape,
        compiler_params=_params("parallel", "parallel"),
    )(*ins)


def _rowwise_bwd(fn, ins, kinds, gdtypes, cts, tr, *, name):
    B, n = ins[0].shape[:2]
    ni, nc = len(ins), len(cts)
    idx = list(range(ni))

    def body(*refs):
        in_refs, ct_refs, out_refs = refs[:ni], refs[ni:ni + nc], refs[ni + nc:]
        b, j = pl.program_id(0), pl.program_id(1)
        _, vjp = jax.vjp(fn, *[r[...].astype(F32) for r in in_refs])
        grads = vjp(tuple(c[...].astype(F32) for c in ct_refs))
        for o, i in zip(out_refs, idx):
            g = grads[i]
            if kinds[i] == "row":
                o[...] = g.astype(o.dtype)
            else:
                first = (j == 0) if kinds[i] == "batch" else ((j == 0) & (b == 0))

                @pl.when(first)
                def _(o=o, g=g):
                    o[...] = g

                @pl.when(jnp.logical_not(first))
                def _(o=o, g=g):
                    o[...] += g

    specs = _row_specs(ins, kinds, tr)
    return pl.pallas_call(
        body, name=name, grid=(B, n // tr),
        in_specs=specs + [pl.BlockSpec((None, tr, c.shape[2]), lambda b, j: (b, j, 0)) for c in cts],
        out_specs=[specs[i] for i in idx],
        out_shape=[jax.ShapeDtypeStruct(ins[i].shape, gdtypes[i]) for i in idx],
        compiler_params=_params("arbitrary", "arbitrary"),
    )(*ins, *cts)


ROW_TILE = 256


def _rms_mod(x, g, sc, sh):
    y = x * lax.rsqrt(jnp.mean(x * x, axis=-1, keepdims=True) + EPS)
    return (y * g) * (1.0 + sc) + sh


def _norm_tile(x, g, sc, sh):
    return x, _rms_mod(x, g, sc, sh)


def _res_norm_tile(xb, y, gate, g, sc, sh):
    x = xb + gate * y
    return x, _rms_mod(x, g, sc, sh)


_NORM_KINDS = ("row", "glob", "batch", "batch")
_RES_NORM_KINDS = ("row", "row", "batch", "glob", "batch", "batch")


def _norm_fwd(tag, st, g, sc, sh):
    xb, y, gate = st
    tr = min(ROW_TILE, xb.shape[1])
    d = xb.shape[2]
    if y is None:
        h, h_t = _rowwise_fwd(lambda *a: (_rms_mod(*a),), [xb, g, sc, sh], _NORM_KINDS, [(d, BF16)], tr,
                              name=f"{tag}_fwd", transposed=0)
        return xb, h, h_t
    return _rowwise_fwd(_res_norm_tile, [xb, y, gate, g, sc, sh], _RES_NORM_KINDS, [(d, F32), (d, BF16)], tr,
                        name=f"{tag}_fwd", transposed=1)


def _norm_bwd(tag, st, g, sc, sh, dx, dh):
    xb, y, gate = st
    tr = min(ROW_TILE, xb.shape[1])
    if y is None:
        dxb, dg, dsc, dsh = _rowwise_bwd(_norm_tile, [xb, g, sc, sh], _NORM_KINDS, [F32] * 4, [dx, dh], tr, name=f"{tag}_bwd")
        return dxb, None, None, dg, dsc, dsh
    return tuple(_rowwise_bwd(_res_norm_tile, [xb, y, gate, g, sc, sh], _RES_NORM_KINDS, [F32, BF16, F32, F32, F32, F32],
                              [dx, dh], tr, name=f"{tag}_bwd"))


def _loss_head(st, final_g, target, *, name):
    xb, y, gate = st
    B, n, d = xb.shape
    tr = min(ROW_TILE, n)

    def tile_loss(xv, yv, gt, g, t):
        x = xv + gt * yv
        out = x * lax.rsqrt(jnp.mean(x * x, axis=-1, keepdims=True) + EPS) * g
        return 0.5 * jnp.sum(jnp.mean(jnp.square(out - t), axis=-1))

    def body(x_ref, y_ref, gate_ref, g_ref, t_ref, loss_ref, dx_ref, dy_ref, dgate_ref, dg_ref):
        b, j = pl.program_id(0), pl.program_id(1)
        val, (dx, dy, dgate, dg) = jax.value_and_grad(tile_loss, argnums=(0, 1, 2, 3))(
            x_ref[...], y_ref[...], gate_ref[...], g_ref[...], t_ref[...])
        dx_ref[...] = dx
        dy_ref[...] = dy.astype(dy_ref.dtype)

        @pl.when(j == 0)
        def _():
            loss_ref[...] = jnp.zeros_like(loss_ref)
            dgate_ref[...] = jnp.zeros_like(dgate_ref)

        @pl.when((j == 0) & (b == 0))
        def _():
            dg_ref[...] = jnp.zeros_like(dg_ref)

        loss_ref[...] += jnp.full(loss_ref.shape, val, F32)
        dgate_ref[...] += dgate
        dg_ref[...] += dg

    row = pl.BlockSpec((None, tr, d), lambda b, j: (b, j, 0))
    per_sample = pl.BlockSpec((None, 1, d), lambda b, j: (b, 0, 0))
    whole = pl.BlockSpec((1, d), lambda b, j: (0, 0))
    return pl.pallas_call(
        body, name=name, grid=(B, n // tr),
        in_specs=[row, row, per_sample, whole, row],
        out_specs=[pl.BlockSpec((None, 1, 128), lambda b, j: (b, 0, 0)), row, row, per_sample, whole],
        out_shape=[jax.ShapeDtypeStruct((B, 1, 128), F32), jax.ShapeDtypeStruct((B, n, d), F32),
                   jax.ShapeDtypeStruct((B, n, d), BF16), jax.ShapeDtypeStruct((B, 1, d), F32), jax.ShapeDtypeStruct((1, d), F32)],
        compiler_params=_params("arbitrary", "arbitrary"),
    )(xb, y, gate, final_g, target)


def _exchange(arrs, *, scatter, name):
    k = len(arrs)

    def body(*refs):
        ins, outs, sems = refs[:k], refs[k:2 * k], refs[2 * k:]
        _exch_start(ins, outs, sems, scatter)
        _exch_wait(ins, outs, sems, scatter)

    any_spec = pl.BlockSpec(memory_space=pl.ANY)
    return pl.pallas_call(
        body, name=name,
        in_specs=[any_spec] * k, out_specs=[any_spec] * k,
        out_shape=_exch_out_shapes(arrs, scatter), scratch_shapes=_exch_sems(k),
        compiler_params=pltpu.CompilerParams(has_side_effects=True),
    )(*arrs)


def _exch_out_shapes(arrs, scatter):
    return [jax.ShapeDtypeStruct(a.shape if scatter else (N_DEV,) + a.shape, a.dtype) for a in arrs]


def _exch_sems(k):
    return [pltpu.SemaphoreType.DMA((k * (N_DEV - 1),)), pltpu.SemaphoreType.DMA((k * (N_DEV - 1),)),
            pltpu.SemaphoreType.DMA((k,))]


def _exch_copies(ins, outs, sems, scatter):
    send_sems, recv_sems, local_sems = sems
    x, y, c = lax.axis_index("x"), lax.axis_index("y"), lax.axis_index("c")
    me = 4 * x + 2 * y + c
    owns, sends, recvs = [], [], []
    for a in range(len(ins)):
        owns.append(pltpu.make_async_copy(ins[a].at[me] if scatter else ins[a], outs[a].at[me], local_sems.at[a]))
        for r in range(1, N_DEV):
            fx, fy, fc = (r >> 2) & 1, (r >> 1) & 1, r & 1
            px, py, pc = (x + fx) % 2, (y + fy) % 2, (c + fc) % 2
            peer = 4 * px + 2 * py + pc
            s = a * (N_DEV - 1) + r - 1
            mk = functools.partial(pltpu.make_async_remote_copy, src_ref=ins[a].at[peer] if scatter else ins[a],
                                   send_sem=send_sems.at[s], recv_sem=recv_sems.at[s],
                                   device_id=(px, py, pc), device_id_type=pl.DeviceIdType.MESH)
            sends.append(mk(dst_ref=outs[a].at[me]))
            recvs.append(mk(dst_ref=outs[a].at[peer]))
    return owns, sends, recvs


def _exch_start(ins, outs, sems, scatter):
    owns, sends, _ = _exch_copies(ins, outs, sems, scatter)
    for cp in owns + sends:
        cp.start()


def _exch_wait(ins, outs, sems, scatter):
    owns, sends, recvs = _exch_copies(ins, outs, sems, scatter)
    for rc in recvs:
        rc.wait_recv()
    for cp in sends:
        cp.wait_send()
    for own in owns:
        own.wait()


MOD_ROWS = 48


def _mod_tile(cc, w, b):
    s = cc * jax.nn.sigmoid(cc)
    return lax.dot_general(s.astype(BF16), w.astype(BF16), (((1,), (0,)), ((), ())), preferred_element_type=F32) + b


def _mod_fwd(cc, w_mod, b_shard, *, name):
    L, d, wcols = w_mod.shape

    def body(cc_ref, w_ref, b_ref, o_ref):
        o_ref[...] = _mod_tile(cc_ref[...], w_ref[...], b_ref[...])

    return pl.pallas_call(
        body, name=name, grid=(L,),
        in_specs=[pl.BlockSpec((MOD_ROWS, d), lambda l: (0, 0)), pl.BlockSpec((None, d, wcols), lambda l: (l, 0, 0)),
                  pl.BlockSpec((None, 1, wcols), lambda l: (l, 0, 0))],
        out_specs=pl.BlockSpec((None, MOD_ROWS, wcols), lambda l: (l, 0, 0)),
        out_shape=jax.ShapeDtypeStruct((L, MOD_ROWS, wcols), F32),
        compiler_params=_params("parallel"),
    )(cc, w_mod, b_shard)


def _mod_bwd(cc, w_mod, b_shard, dm, *, name):
    L, d, wcols = w_mod.shape

    def body(cc_ref, w_ref, b_ref, dm_ref, dcc_ref, dw_ref):
        _, vjp = jax.vjp(_mod_tile, cc_ref[...], w_ref[...], b_ref[...])
        dcc, dw, _ = vjp(dm_ref[...])
        dw_ref[...] = dw

        @pl.when(pl.program_id(0) == 0)
        def _():
            dcc_ref[...] = dcc

        @pl.when(pl.program_id(0) > 0)
        def _():
            dcc_ref[...] += dcc

    return pl.pallas_call(
        body, name=name, grid=(L,),
        in_specs=[pl.BlockSpec((MOD_ROWS, d), lambda l: (0, 0)), pl.BlockSpec((None, d, wcols), lambda l: (l, 0, 0)),
                  pl.BlockSpec((None, 1, wcols), lambda l: (l, 0, 0)), pl.BlockSpec((None, MOD_ROWS, wcols), lambda l: (l, 0, 0))],
        out_specs=[pl.BlockSpec((MOD_ROWS, d), lambda l: (0, 0)), pl.BlockSpec((None, d, wcols), lambda l: (l, 0, 0))],
        out_shape=[jax.ShapeDtypeStruct((MOD_ROWS, d), F32), jax.ShapeDtypeStruct((L, d, wcols), F32)],
        compiler_params=_params("arbitrary"),
    )(cc, w_mod, b_shard, dm)


def _sum_leading(a, *, name):
    K, R, C = a.shape
    tr = _tile8(R, 256)

    def body(a_ref, o_ref):
        acc = a_ref[0].astype(F32)
        for i in range(1, K):
            acc = acc + a_ref[i].astype(F32)
        o_ref[...] = acc

    return pl.pallas_call(
        body, name=name, grid=(R // tr,),
        in_specs=[pl.BlockSpec((K, tr, C), lambda i: (0, i, 0))],
        out_specs=pl.BlockSpec((tr, C), lambda i: (i, 0)),
        out_shape=jax.ShapeDtypeStruct((R, C), F32),
        compiler_params=_params("parallel"),
    )(a)


def _tile8(dim, target):
    if dim <= target:
        return dim
    t = (target // 8) * 8
    while t >= 8:
        if dim % t == 0:
            return t
        t -= 8
    raise ValueError(f"no row tile for {dim}")


def _adamw_math(g, w, m, v):
    m = ADAM_B1 * m + (1.0 - ADAM_B1) * g
    v = ADAM_B2 * v + (1.0 - ADAM_B2) * jnp.square(g)
    m_hat = m / (1.0 - ADAM_B1 ** ADAM_STEP)
    v_hat = v / (1.0 - ADAM_B2 ** ADAM_STEP)
    delta = -ADAM_LR * (m_hat / (jnp.sqrt(v_hat) + ADAM_EPS) + ADAM_WD * w)
    return delta, m, v


def _adamw(g, w, m, v, *, name):
    L, R, C = w.shape
    parts = isinstance(g, (list, tuple))
    gs = list(g) if parts else [g]
    ng = len(gs)
    tr = _tile8(R, 256)

    def body(*refs):
        g_refs = refs[:ng]
        w_ref, m_ref, v_ref, go_ref, d_ref, mo_ref, vo_ref = refs[ng:]
        if parts:
            layer = pl.program_id(0)
            gv = None
            for li, g_ref in enumerate(g_refs):
                acc = g_ref[0].astype(F32)
                for i in range(1, N_DEV):
                    acc = acc + g_ref[i].astype(F32)
                gv = acc if gv is None else jnp.where(layer == li, acc, gv)
        else:
            gv = g_refs[0][...]
        go_ref[...] = gv
        d_ref[...], mo_ref[...], vo_ref[...] = _adamw_math(gv, w_ref[...], m_ref[...], v_ref[...])

    tile = pl.BlockSpec((None, tr, C), lambda l, i: (l, i, 0))
    g_specs = [pl.BlockSpec((N_DEV, tr, C), lambda l, i: (0, i, 0))] * ng if parts else [tile]
    return pl.pallas_call(
        body, name=name, grid=(L, R // tr),
        in_specs=g_specs + [tile, tile, tile], out_specs=[tile] * 4,
        out_shape=[jax.ShapeDtypeStruct((L, R, C), F32)] * 4,
        compiler_params=_params("parallel", "parallel"),
    )(*gs, w, m, v)


def _adamw_small(gs, ws, ms, vs, *, name):
    k = len(ws)

    def body(*refs):
        g_refs, w_refs, m_refs, v_refs = refs[:k], refs[k:2 * k], refs[2 * k:3 * k], refs[3 * k:4 * k]
        d_refs, mo_refs, vo_refs = refs[4 * k:5 * k], refs[5 * k:6 * k], refs[6 * k:]
        for i in range(k):
            d_refs[i][...], mo_refs[i][...], vo_refs[i][...] = _adamw_math(g_refs[i][...], w_refs[i][...], m_refs[i][...],
                                                                         v_refs[i][...])

    shapes = [jax.ShapeDtypeStruct(a.shape, F32) for a in ws]
    out = pl.pallas_call(body, name=name, out_shape=shapes * 3, compiler_params=pltpu.CompilerParams(vmem_limit_bytes=VMEM_LIMIT))(
        *gs, *ws, *ms, *vs)
    return out[:k], out[k:2 * k], out[2 * k:]


def _block_diag(w):
    g = w.shape[0]
    rows = [jnp.concatenate([w[i] if j == i else jnp.zeros_like(w[i]) for j in range(g)], axis=1) for i in range(g)]
    return jnp.concatenate(rows, axis=0)


def _diag_blocks(w_bd):
    g = POOL_W // POOL_GROUP
    return jnp.stack([w_bd[i * POOL_GROUP:(i + 1) * POOL_GROUP, i * POOL_GROUP:(i + 1) * POOL_GROUP] for i in range(g)])


def _flat(a):
    return a.reshape(-1, a.shape[-1])


def _mix_half_fwd(tag, st, mods, wl, kvc, *, local, kv_only, exch=None):
    sh1, sc1, g1 = mods[:3]
    B, n, d = st[0].shape
    x, h, h_t = _norm_fwd(f"{tag}_norm1", st, wl["n1"], sc1, sh1)
    if kv_only:
        kv = _mm(_flat(h), wl["w_in"][:, ATTN_W:ATTN_W + 2 * KV_W], name=f"{tag}_kv").reshape(B, n, 2 * KV_W)
        return None, dict(st=st, h_t=h_t, kvc=kv), []
    u = _mm(_flat(h), wl["w_in"], name=f"{tag}_in").reshape(B, n, IN_W)
    if not local:
        kvc = u[:, :, ATTN_W:ATTN_W + 2 * KV_W]
    mix, got = _mixer_fwd(f"{tag}_mix", u, kvc, wl["margs"], local, exch)
    y = _mm(_flat(mix), wl["w_out"], name=f"{tag}_out").reshape(B, n, d)
    return (x, y, g1), dict(st=st, h_t=h_t, u=u, kvc=kvc, mix=mix), got


def _ffn_half_fwd(tag, st2, mods, wl, exch=None):
    sh2, sc2, g2 = mods[3:]
    B, n, d = st2[0].shape
    x1, h2, h2_t = _norm_fwd(f"{tag}_norm2", st2, wl["n2"], sc2, sh2)
    gu, act, act_t, *got = _mm_swiglu(_flat(h2), wl["w_ffn_in"], name=f"{tag}_ffn_in", exch=exch)
    y2 = _mm(act, wl["w_ffn_out"], name=f"{tag}_ffn_out").reshape(B, n, d)
    return (x1, y2, g2), dict(st2=st2, h2_t=h2_t, gu=gu, act_t=act_t), got


def _ffn_half_bwd(tag, sv, mods, wl, dx1, dy2):
    sh2, sc2, _ = mods[3:]
    B, n, d = sv["st2"][0].shape
    gw = {}
    dy2f = _flat(dy2)
    gw["w_ffn_out"] = _mm(sv["act_t"], dy2f, out_dtype=BF16, name=f"{tag}_ffn_out_dw")
    dgu = _mm_dswiglu(dy2f, wl["w_ffn_out_t"], sv["gu"], name=f"{tag}_ffn_out_dx")
    dh2 = _mm(dgu, wl["w_ffn_in_t"], name=f"{tag}_ffn_in_dx").reshape(B, n, d)
    gw["w_ffn_in"] = _deinterleave_ffn(_mm(sv["h2_t"], dgu, out_dtype=BF16, name=f"{tag}_ffn_in_dw"))
    dx, dy, dg1, gw["n2"], dsc2, dsh2 = _norm_bwd(f"{tag}_norm2", sv["st2"], wl["n2"], sc2, sh2, dx1, dh2)
    return (dx, dy, dg1), gw, dict(sh2=dsh2, sc2=dsc2)


def _mix_half_bwd(tag, sv, mods, wl, dx, dy, dkv_in, *, local, kv_only, exch=None):
    sh1, sc1, _ = mods[:3]
    B, n, d = sv["st"][0].shape
    gw = {}
    if kv_only:
        dkv = _flat(dkv_in).astype(BF16)
        dh = _mm(dkv, wl["w_in_t"][ATTN_W:ATTN_W + 2 * KV_W, :], name=f"{tag}_kv_dx").reshape(B, n, d)
        gw["w_in_kv"] = _mm(sv["h_t"], dkv, out_dtype=BF16, name=f"{tag}_kv_dw")
        dxb, dy_prev, dgate_prev, gw["n1"], dsc1, dsh1 = _norm_bwd(f"{tag}_norm1", sv["st"], wl["n1"], sc1, sh1,
                                                                    jnp.zeros((B, n, d), F32), dh)
        return (dxb, dy_prev, dgate_prev), gw, dict(sh1=dsh1, sc1=dsc1), None, []

    dyf = _flat(dy)
    dmix = _mm(dyf, wl["w_out_t"], name=f"{tag}_out_dx").reshape(B, n, d)
    gw["w_out"] = _mm(_flat(sv["mix"]).T, dyf, out_dtype=BF16, name=f"{tag}_out_dw")
    (dq, dk, dv, da, dg, dpu), (dkc, dvc), gw["margs"], got = _mixer_bwd(f"{tag}_mix", sv["u"], sv["kvc"], wl["margs"], dmix,
                                                                     local, exch)
    if local:
        dkv_out = jnp.concatenate([dkc, dvc], axis=-1)
    else:
        dk = dkc + dkv_in[:, :, :KV_W]
        dv = dvc + dkv_in[:, :, KV_W:]
        dkv_out = None
    du = _flat(jnp.concatenate([dq, dk, dv, da, dg, dpu], axis=-1).astype(BF16))
    dh = _mm(du, wl["w_in_t"], name=f"{tag}_in_dx").reshape(B, n, d)
    gw["w_in"] = _mm(sv["h_t"], du, out_dtype=BF16, name=f"{tag}_in_dw")
    dxb, dy_prev, dgate_prev, gw["n1"], dsc1, dsh1 = _norm_bwd(f"{tag}_norm1", sv["st"], wl["n1"], sc1, sh1, dx, dh)
    return (dxb, dy_prev, dgate_prev), gw, dict(sh1=dsh1, sc1=dsc1), dkv_out, got


BIG_W = ("w_in", "w_out", "w_ffn_in", "w_ffn_out")


def _local_step(x, ctx, m_loc, m_ctx, p, final_g, target, big):
    B = x.shape[0]
    depth = m_loc.shape[0]
    lat_mods = [[t[:, None, :] for t in jnp.split(m_loc[l], 6, axis=-1)] for l in range(depth)]
    ctx_mods = [[jnp.broadcast_to(t[None, None, :], (B, 1, D_MODEL)) for t in jnp.split(m_ctx[l], 6)] for l in range(depth)]

    st, cst = (x, None, None), (ctx, None, None)
    w_mix, w_ffn, sv_mix, sv_ffn, csv_mix, csv_ffn = [], [], [], [], [], []
    got = []
    for l in range(depth):
        last = l == depth - 1
        wb = big.mix_weights(l, got)
        wm = dict(n1=p["norm1_g"][l][None, :], w_in=wb["w_in"], w_out=wb["w_out"], w_in_t=wb["w_in"].T, w_out_t=wb["w_out"].T,
                  margs=(p["attn_sink"][l][None, :], p["conv_dw"][l], p["conv_dw_b"][l][None, :], p["conv_ln_g"][l][None, :],
                         p["conv_ln_b"][l][None, :], _block_diag(p["pool_w"][l]), p["pool_scale"][l][None, :]))
        cst, csv, _ = _mix_half_fwd(f"l{l}c", cst, ctx_mods[l], wm, None, local=False, kv_only=last)
        st, sv, got = _mix_half_fwd(f"l{l}", st, lat_mods[l], wm, csv["kvc"], local=True, kv_only=False,
                                    exch=big.ride_attn_fwd(l))
        w_mix.append(wm)
        sv_mix.append(sv)
        csv_mix.append(csv)
        wb = big.ffn_weights(l, got)
        w_ffn_in = _interleave_ffn(wb["w_ffn_in"])
        wf = dict(n2=p["norm2_g"][l][None, :], w_ffn_in=w_ffn_in, w_ffn_out=wb["w_ffn_out"],
                  w_ffn_in_t=w_ffn_in.T, w_ffn_out_t=wb["w_ffn_out"].T)
        csv = None
        if not last:
            cst, csv, _ = _ffn_half_fwd(f"l{l}c", cst, ctx_mods[l], wf)
        st, sv, got = _ffn_half_fwd(f"l{l}", st, lat_mods[l], wf, exch=big.ride_ffn_fwd(l))
        w_ffn.append(wf)
        sv_ffn.append(sv)
        csv_ffn.append(csv)
    loss_rows, dx, dy, dgate, dfinal = _loss_head(st, final_g[None, :], target, name="loss_head")

    dm_loc, dm_ctx = [None] * depth, [None] * depth
    small = [None] * depth
    cdx = cdy = cdgate = None
    up_mix = None
    for l in reversed(range(depth)):
        last = l == depth - 1
        dm, cdm = dict(g2=dgate), {}
        (dx, dy, dm["g1"]), gf, d = _ffn_half_bwd(f"l{l}", sv_ffn[l], lat_mods[l], w_ffn[l], dx, dy)
        dm.update(d)
        if not last:
            cdm["g2"] = cdgate
            (cdx, cdy, cdm["g1"]), cgf, d = _ffn_half_bwd(f"l{l}c", csv_ffn[l], ctx_mods[l], w_ffn[l], cdx, cdy)
            cdm.update(d)
            gf = {k: gf[k] + cgf[k] for k in gf}
        ffn_grads = {k: gf[k] for k in _ShardedWeights.FFN}
        (dx, dy, dgate), gm, d, dkv, got = _mix_half_bwd(f"l{l}", sv_mix[l], lat_mods[l], w_mix[l], dx, dy, None, local=True,
                                                        kv_only=False, exch=big.ride_attn_bwd(l, ffn_grads, up_mix))
        big.took(l, ffn_grads, up_mix, got)
        dm.update(d)
        (cdx, cdy, cdgate), cgm, d, _, _ = _mix_half_bwd(f"l{l}c", csv_mix[l], ctx_mods[l], w_mix[l], cdx, cdy, dkv,
                                                        local=False, kv_only=last)
        cdm.update(d)
        order = ("sh1", "sc1", "g1", "sh2", "sc2", "g2")
        dm_loc[l] = jnp.concatenate([dm[k][:, 0, :] for k in order], axis=-1)
        dm_ctx[l] = jnp.concatenate([jnp.sum(cdm[k][:, 0, :], axis=0) if k in cdm else jnp.zeros((D_MODEL,), F32)
                                     for k in order])
        if last:
            up_mix = dict(w_in=gm["w_in"].at[:, ATTN_W:ATTN_W + 2 * KV_W].add(cgm["w_in_kv"]), w_out=gm["w_out"])
            margs = gm["margs"]
        else:
            up_mix = {k: gm[k] + cgm[k] for k in _ShardedWeights.MIX}
            margs = tuple(a + b for a, b in zip(gm["margs"], cgm["margs"]))
        small[l] = dict(n1=gm["n1"] + cgm["n1"], n2=gf["n2"], margs=margs)
    big.leftover(up_mix)

    stack = lambda f: jnp.stack([f(small[l]) for l in range(depth)])
    dp = dict(
        norm1_g=stack(lambda g: g["n1"][0]), norm2_g=stack(lambda g: g["n2"][0]),
        attn_sink=stack(lambda g: g["margs"][0][0]), conv_dw=stack(lambda g: g["margs"][1]),
        conv_dw_b=stack(lambda g: g["margs"][2][0]), conv_ln_g=stack(lambda g: g["margs"][3][0]),
        conv_ln_b=stack(lambda g: g["margs"][4][0]), pool_w=stack(lambda g: _diag_blocks(g["margs"][5])),
        pool_scale=stack(lambda g: g["margs"][6][0]))
    return jnp.sum(loss_rows[:, 0, 0]), dx, jnp.stack(dm_loc), jnp.stack(dm_ctx), dp, dfinal[0]


PACK_COLS = 1024


def _pack(arrs):
    flat = jnp.concatenate([a.reshape(-1).astype(F32) for a in arrs])
    rows = -(-flat.shape[0] // (8 * PACK_COLS)) * 8
    return jnp.pad(flat, (0, rows * PACK_COLS - flat.shape[0])).reshape(rows, PACK_COLS)


def _unpack(slab, like):
    flat = slab.reshape(-1)
    out, off = [], 0
    for a in like:
        out.append(flat[off:off + a.size].reshape(a.shape))
        off += a.size
    return out


def _shard_cols(gathered):
    _, L, R, C = gathered.shape
    return jnp.transpose(gathered, (1, 2, 0, 3)).reshape(L, R, N_DEV * C)


class _ShardedWeights:
    MIX = ("w_in", "w_out")
    FFN = ("w_ffn_in", "w_ffn_out")
    BY_COLS = ("w_in", "w_ffn_in")

    def __init__(self, shards, first):
        self.shards = shards
        self.first = first
        self.depth = shards[BIG_W[0]].shape[0]
        self.parts = [dict() for _ in range(self.depth)]
        self.left = None

    def _join(self, names, blocks):
        out = {}
        for name, g in zip(names, blocks):
            _, R, C = g.shape
            out[name] = jnp.transpose(g, (1, 0, 2)).reshape(R, N_DEV * C) if name in self.BY_COLS else g.reshape(N_DEV * R, C)
        return out

    def cut(self, names, grads):
        out = []
        for name in names:
            g = grads[name]
            if name in self.BY_COLS:
                R, C8 = g.shape
                out.append(jnp.transpose(g.reshape(R, N_DEV, C8 // N_DEV), (1, 0, 2)))
            else:
                R8, C = g.shape
                out.append(g.reshape(N_DEV, R8 // N_DEV, C))
        return out

    def mix_weights(self, l, got):
        return self._join(self.MIX, self.first if l == 0 else got)

    def ffn_weights(self, l, got):
        return self._join(self.FFN, got)

    def ride_attn_fwd(self, l):
        return [self.shards[name][l] for name in self.FFN], False

    def ride_ffn_fwd(self, l):
        if l + 1 >= self.depth:
            return None
        return [self.shards[name][l + 1] for name in self.MIX], False

    def ride_attn_bwd(self, l, ffn_grads, up_mix):
        return self.cut(self.FFN, ffn_grads) + (self.cut(self.MIX, up_mix) if up_mix is not None else []), True

    def took(self, l, ffn_grads, up_mix, got):
        self.parts[l].update(zip(self.FFN, got[:2]))
        if up_mix is not None:
            self.parts[l + 1].update(zip(self.MIX, got[2:]))

    def leftover(self, mix_grads):
        self.left = mix_grads


def _as_rows(a, leading=0):
    return a.reshape(*a.shape[:leading], -1, PACK_COLS)


SMALL = ("c_ctx", "b_mod", "norm1_g", "norm2_g", "conv_dw_b", "conv_ln_g", "conv_ln_b", "attn_sink", "pool_w",
         "pool_scale", "final_g", "conv_dw")
BIG = ("w_mod", "w_in", "w_out", "w_ffn_in", "w_ffn_out")
ORDER = ("c_ctx", "w_mod", "b_mod", "norm1_g", "norm2_g", "w_in", "conv_dw", "conv_dw_b", "conv_ln_g", "conv_ln_b",
         "attn_sink", "pool_w", "pool_scale", "w_out", "w_ffn_in", "w_ffn_out", "final_g")


def kernel(x, c, ctx, c_ctx, w_mod, b_mod, norm1_g, norm2_g, w_in, conv_dw, conv_dw_b, conv_ln_g, conv_ln_b, attn_sink, pool_w, pool_scale, w_out, w_ffn_in, w_ffn_out, final_g, loss_target, m_c_ctx, m_w_mod, m_b_mod, m_norm1_g, m_norm2_g, m_w_in, m_conv_dw, m_conv_dw_b, m_conv_ln_g, m_conv_ln_b, m_attn_sink, m_pool_w, m_pool_scale, m_w_out, m_w_ffn_in, m_w_ffn_out, m_final_g, v_c_ctx, v_w_mod, v_b_mod, v_norm1_g, v_norm2_g, v_w_in, v_conv_dw, v_conv_dw_b, v_conv_ln_g, v_conv_ln_b, v_attn_sink, v_pool_w, v_pool_scale, v_w_out, v_w_ffn_in, v_w_ffn_out, v_final_g):
    w = dict(c_ctx=c_ctx, w_mod=w_mod, b_mod=b_mod, norm1_g=norm1_g, norm2_g=norm2_g, w_in=w_in, conv_dw=conv_dw,
             conv_dw_b=conv_dw_b, conv_ln_g=conv_ln_g, conv_ln_b=conv_ln_b, attn_sink=attn_sink, pool_w=pool_w,
             pool_scale=pool_scale, w_out=w_out, w_ffn_in=w_ffn_in, w_ffn_out=w_ffn_out, final_g=final_g)
    mom = dict(c_ctx=m_c_ctx, w_mod=m_w_mod, b_mod=m_b_mod, norm1_g=m_norm1_g, norm2_g=m_norm2_g, w_in=m_w_in,
               conv_dw=m_conv_dw, conv_dw_b=m_conv_dw_b, conv_ln_g=m_conv_ln_g, conv_ln_b=m_conv_ln_b,
               attn_sink=m_attn_sink, pool_w=m_pool_w, pool_scale=m_pool_scale, w_out=m_w_out, w_ffn_in=m_w_ffn_in,
               w_ffn_out=m_w_ffn_out, final_g=m_final_g)
    var = dict(c_ctx=v_c_ctx, w_mod=v_w_mod, b_mod=v_b_mod, norm1_g=v_norm1_g, norm2_g=v_norm2_g, w_in=v_w_in,
               conv_dw=v_conv_dw, conv_dw_b=v_conv_dw_b, conv_ln_g=v_conv_ln_g, conv_ln_b=v_conv_ln_b,
               attn_sink=v_attn_sink, pool_w=v_pool_w, pool_scale=v_pool_scale, w_out=v_w_out, w_ffn_in=v_w_ffn_in,
               w_ffn_out=v_w_ffn_out, final_g=v_final_g)
    B = x.shape[0]
    depth = w_mod.shape[0]
    mod_cols = w_mod.shape[2]
    dw_cols = conv_dw.shape[2]
    me = 4 * lax.axis_index("x") + 2 * lax.axis_index("y") + lax.axis_index("c")

    shards = {name: w[name].astype(BF16) for name in BIG_W}
    c_all, dw_all, *first = _exchange([c, conv_dw] + [shards[name][0] for name in _ShardedWeights.MIX], scatter=False,
                                      name="gather_first")
    big = _ShardedWeights(shards, first)
    p = dict(norm1_g=norm1_g, norm2_g=norm2_g, conv_dw=_shard_cols(dw_all), conv_dw_b=conv_dw_b, conv_ln_g=conv_ln_g,
             conv_ln_b=conv_ln_b, attn_sink=attn_sink, pool_w=pool_w, pool_scale=pool_scale)

    cc = jnp.concatenate([c_all.reshape(N_DEV * B, D_MODEL), jnp.broadcast_to(c_ctx[None, :], (N_DEV, D_MODEL)),
                          jnp.zeros((MOD_ROWS - N_DEV * B - N_DEV, D_MODEL), F32)], axis=0)
    b_shard = lax.dynamic_slice_in_dim(b_mod, me * mod_cols, mod_cols, axis=1)[:, None, :]
    m_part = _mod_fwd(cc, w_mod, b_shard, name="mod_fwd")
    m_all, = _exchange([m_part], scatter=False, name="gather_mod")
    m_full = _shard_cols(m_all)
    m_loc = lax.dynamic_slice_in_dim(m_full, me * B, B, axis=1)
    m_ctx = m_full[:, N_DEV * B, :]

    loss_part, dx, dm_loc, dm_ctx, dp, dfinal = _local_step(x, ctx, m_loc, m_ctx, p, final_g, loss_target, big)
    loss = lax.psum(loss_part, AXES)

    dm_rows = jnp.concatenate([dm_loc, dm_ctx[:, None, :], jnp.zeros((depth, 8 - B - 1, 6 * D_MODEL), F32)], axis=1)
    dm_all, = _exchange([dm_rows], scatter=False, name="gather_dmod")
    dm_full = jnp.concatenate([
        jnp.transpose(dm_all[:, :, :B, :], (1, 0, 2, 3)).reshape(depth, N_DEV * B, 6 * D_MODEL),
        jnp.transpose(dm_all[:, :, B, :], (1, 0, 2)),
        jnp.zeros((depth, MOD_ROWS - N_DEV * B - N_DEV, 6 * D_MODEL), F32)], axis=1)
    g_b_mod = jnp.stack([_sum_leading(dm_full[l][:, None, :], name=f"b_mod_grad{l}")[0] for l in range(depth)])
    dm_mine = lax.dynamic_slice_in_dim(dm_full, me * mod_cols, mod_cols, axis=2)
    dcc, g_w_mod = _mod_bwd(cc, w_mod, b_shard, dm_mine, name="mod_bwd")
    g_c_ctx_part = jnp.sum(dcc[N_DEV * B:N_DEV * B + N_DEV], axis=0)

    small_like = [c_ctx, norm1_g, norm2_g, conv_dw_b, conv_ln_g, conv_ln_b, attn_sink, pool_w, pool_scale, final_g,
                  dp["conv_dw"]]
    small_part = _pack([g_c_ctx_part, dp["norm1_g"], dp["norm2_g"], dp["conv_dw_b"], dp["conv_ln_g"], dp["conv_ln_b"],
                        dp["attn_sink"], dp["pool_w"], dp["pool_scale"], dfinal, dp["conv_dw"]])
    small_all, = _exchange([small_part], scatter=False, name="gather_small")
    small_sum = _unpack(_sum_leading(small_all, name="sum_small"), small_like)
    g = dict(zip(("c_ctx", "norm1_g", "norm2_g", "conv_dw_b", "conv_ln_g", "conv_ln_b", "attn_sink", "pool_w",
                  "pool_scale", "final_g"), small_sum[:-1]))
    g["b_mod"] = g_b_mod
    g["conv_dw"] = lax.dynamic_slice_in_dim(small_sum[-1], me * dw_cols, dw_cols, axis=2)

    big.parts[0].update(zip(big.MIX, _exchange(big.cut(big.MIX, big.left), scatter=True, name="scatter_last")))

    delta, new_m, new_v = {}, {}, {}
    for name in BIG_W:
        g[name], delta[name], new_m[name], new_v[name] = _adamw(
            [big.parts[l][name] for l in range(depth)], w[name], mom[name], var[name], name=f"adamw_{name}")
    g["w_mod"], delta["w_mod"], new_m["w_mod"], new_v["w_mod"] = _adamw(g_w_mod, w_mod, m_w_mod, v_w_mod, name="adamw_w_mod")
    res = _adamw_small([g[k] for k in SMALL], [w[k] for k in SMALL], [mom[k] for k in SMALL], [var[k] for k in SMALL],
                       name="adamw_small")
    for dst, arrs in zip((delta, new_m, new_v), res):
        dst.update(zip(SMALL, arrs))

    return (loss, dx, *[g[k] for k in ORDER], *[delta[k] for k in ORDER], *[new_m[k] for k in ORDER],
            *[new_v[k] for k in ORDER])
```

```python
import functools

import numpy as np
import jax
import jax.numpy as jnp
from jax import lax
from jax.experimental import pallas as pl
from jax.experimental.pallas import tpu as pltpu

F32 = jnp.float32
BF16 = jnp.bfloat16

D_MODEL = 1024
GRID_W = 64
HEAD_DIM = 64
ATTN_W = 512
CONV_W = 256
POOL_W = 256
ATTN_HEADS = 8
KV_HEADS = 2
GROUP = ATTN_HEADS // KV_HEADS
KV_W = KV_HEADS * HEAD_DIM
IN_W = ATTN_W + 2 * KV_W + 2 * CONV_W + POOL_W
WINDOW = 128
Q_BLOCK = 128
SPAN = Q_BLOCK + 2 * WINDOW
CONV_KERNEL = 31
POOL_WINDOWS = (2, 4, 8, 16)
POOL_GROUP = 64
ROPE_BASE = 10000.0
D_FF = 2816
EPS = 1e-6
NEG = -1e30
N_DEV = 8
AXES = ("x", "y", "c")

ADAM_LR = 0.001
ADAM_B1 = 0.9
ADAM_B2 = 0.999
ADAM_EPS = 1e-08
ADAM_WD = 0.01
ADAM_STEP = 10

VMEM_LIMIT = 56 * 1024 * 1024
HALO = 16
SEQ_CHUNK = 256


def _params(*sem):
    return pltpu.CompilerParams(dimension_semantics=sem, vmem_limit_bytes=VMEM_LIMIT)


def _tile(dim, target):
    if dim <= target:
        return dim
    t = (target // 128) * 128
    while t >= 128:
        if dim % t == 0:
            return t
        t -= 128
    raise ValueError(f"no tile for {dim}")


MM_VMEM_BUDGET = 40 * 1024 * 1024


def _dot(a, b):
    return lax.dot_general(a.astype(BF16), b.astype(BF16), (((1,), (0,)), ((), ())), preferred_element_type=F32)


def _mm_vmem(tm, tn, tk, whole, a_bytes, b_bytes, o_bytes):
    return 2 * (tm * tk * a_bytes + tk * tn * b_bytes + tm * tn * o_bytes) + (0 if whole else tm * tn * 4)


def _mm(a, b, *, name, out_dtype=F32, tm=1408, tn=512, trans_b=False):
    M, K = a.shape
    N, K2 = b.shape if trans_b else b.shape[::-1]
    assert K == K2, (a.shape, b.shape)
    tm = _tile(M, tm)
    tn = _tile(N, tn)
    sizes = (a.dtype.itemsize, b.dtype.itemsize, jnp.dtype(out_dtype).itemsize)
    tk = next(t for t in range(K, 0, -128) if K % t == 0 and _mm_vmem(tm, tn, t, t == K, *sizes) <= MM_VMEM_BUDGET)
    nk = K // tk

    def body(a_ref, b_ref, o_ref, *scratch):
        part = (_dot_nt if trans_b else _dot)(a_ref[...], b_ref[...])
        if nk == 1:
            o_ref[...] = part.astype(o_ref.dtype)
        else:
            acc_ref, = scratch
            k = pl.program_id(2)

            @pl.when(k == 0)
            def _():
                acc_ref[...] = part

            @pl.when(k > 0)
            def _():
                acc_ref[...] += part

            @pl.when(k == nk - 1)
            def _():
                o_ref[...] = acc_ref[...].astype(o_ref.dtype)

    return pl.pallas_call(
        body, name=name, grid=(M // tm, N // tn, nk),
        in_specs=[pl.BlockSpec((tm, tk), lambda i, j, k: (i, k)),
                  pl.BlockSpec((tn, tk), lambda i, j, k: (j, k)) if trans_b else pl.BlockSpec((tk, tn), lambda i, j, k: (k, j))],
        out_specs=pl.BlockSpec((tm, tn), lambda i, j, k: (i, j)),
        out_shape=jax.ShapeDtypeStruct((M, N), out_dtype),
        scratch_shapes=[pltpu.VMEM((tm, tn), F32)] if nk > 1 else [],
        compiler_params=_params("parallel", "parallel", "arbitrary"),
    )(a, b)


FF_TILE = 256


def _interleave_ffn(w):
    tiles = D_FF // FF_TILE
    cols = [w[..., half * D_FF + j * FF_TILE:half * D_FF + (j + 1) * FF_TILE] for j in range(tiles) for half in range(2)]
    return jnp.concatenate(cols, axis=-1)


def _deinterleave_ffn(w):
    tiles = D_FF // FF_TILE
    cols = [w[..., (2 * j + half) * FF_TILE:(2 * j + half + 1) * FF_TILE] for half in range(2) for j in range(tiles)]
    return jnp.concatenate(cols, axis=-1)


def _swiglu(gu):
    g, u = gu[:, :FF_TILE], gu[:, FF_TILE:]
    return g * jax.nn.sigmoid(g) * u


def _mm_swiglu(a, w_il, *, name, tm=1024, exch=None):
    M, K = a.shape
    tm = _tile(M, tm)

    def body(a_ref, b_ref, gu_ref, act_ref, act_t_ref):
        gu = _dot(a_ref[...], b_ref[...])
        gu_ref[...] = gu.astype(gu_ref.dtype)
        act = _swiglu(gu)
        act_ref[...] = act.astype(act_ref.dtype)
        act_t_ref[...] = act.T.astype(act_t_ref.dtype)

    grid = (M // tm, D_FF // FF_TILE)
    body, x_in, x_out, x_shapes, x_sems = _riding(body, exch, 2, 3, grid)
    return pl.pallas_call(
        body, name=name, grid=grid,
        in_specs=[pl.BlockSpec((tm, K), lambda i, j: (i, 0)), pl.BlockSpec((K, 2 * FF_TILE), lambda i, j: (0, j))] + x_in,
        out_specs=[pl.BlockSpec((tm, 2 * FF_TILE), lambda i, j: (i, j)), pl.BlockSpec((tm, FF_TILE), lambda i, j: (i, j)),
                   pl.BlockSpec((FF_TILE, tm), lambda i, j: (j, i))] + x_out,
        out_shape=[jax.ShapeDtypeStruct((M, 2 * D_FF), BF16), jax.ShapeDtypeStruct((M, D_FF), BF16),
                   jax.ShapeDtypeStruct((D_FF, M), BF16)] + x_shapes,
        scratch_shapes=x_sems,
        compiler_params=_params("arbitrary", "arbitrary") if exch else _params("parallel", "parallel"),
    )(a, w_il, *(exch[0] if exch else []))


def _mm_dswiglu(dy, w_out, gu, *, name, tm=1024):
    M, K = dy.shape
    tm = _tile(M, tm)

    def body(dy_ref, b_ref, gu_ref, o_ref):
        dact = _dot_nt(dy_ref[...], b_ref[...])
        g = gu_ref[:, :FF_TILE].astype(F32)
        u = gu_ref[:, FF_TILE:].astype(F32)
        sig = jax.nn.sigmoid(g)
        silu = g * sig
        o_ref[:, :FF_TILE] = (dact * u * (sig + silu * (1.0 - sig))).astype(o_ref.dtype)
        o_ref[:, FF_TILE:] = (dact * silu).astype(o_ref.dtype)

    return pl.pallas_call(
        body, name=name, grid=(M // tm, D_FF // FF_TILE),
        in_specs=[pl.BlockSpec((tm, K), lambda i, j: (i, 0)), pl.BlockSpec((FF_TILE, K), lambda i, j: (j, 0)),
                  pl.BlockSpec((tm, 2 * FF_TILE), lambda i, j: (i, j))],
        out_specs=pl.BlockSpec((tm, 2 * FF_TILE), lambda i, j: (i, j)),
        out_shape=jax.ShapeDtypeStruct((M, 2 * D_FF), BF16),
        compiler_params=_params("parallel", "parallel"),
    )(dy, w_out, gu)


def _rope_tables(n):
    rows = n // GRID_W
    row = jnp.repeat(jnp.arange(rows), GRID_W).astype(F32)
    col = jnp.tile(jnp.arange(GRID_W), rows).astype(F32)
    half = HEAD_DIM // 2
    inv = ROPE_BASE ** (-jnp.arange(0, half, 2, dtype=F32) / half)
    ar = row[:, None] * inv
    ac = col[:, None] * inv
    ang = jnp.concatenate([ar, ar, ac, ac], axis=-1)
    return jnp.cos(ang), jnp.sin(ang)


def _rot_half(x):
    w = x.shape[-1]
    lane = lax.broadcasted_iota(jnp.int32, x.shape, 1)
    up = pltpu.roll(x, w - 16, 1)
    down = pltpu.roll(x, 16, 1)
    return jnp.where((lane & 16) == 0, -up, down)


def _rope(x, cos, sin):
    return x * cos + _rot_half(x) * sin


def _rope_bwd(d, cos, sin):
    return d * cos - _rot_half(d * sin)


def _dot_nt(a, b):
    return lax.dot_general(a.astype(BF16), b.astype(BF16), (((1,), (1,)), ((), ())), preferred_element_type=F32)


def _softmax_sink(s, sink_rows):
    mx = jnp.maximum(jnp.max(s, axis=1, keepdims=True), sink_rows)
    e = jnp.exp(s - mx)
    es = jnp.exp(sink_rows - mx)
    inv = 1.0 / (jnp.sum(e, axis=1, keepdims=True) + es)
    return e * inv, es * inv


def _attn_operands(q_ref, k_ref, v_ref, kc_ref, vc_ref, cq_ref, sq_ref, ck_ref, sk_ref, i, n, n_ctx, local):
    q = q_ref[...]
    k_all, v_all, bias, s0, ck, sk = kc_ref[...], vc_ref[...], None, None, None, None
    if local:
        start, s0 = _span_start(i, n)
        ck = ck_ref[pl.ds(s0, SPAN), :]
        sk = sk_ref[pl.ds(s0, SPAN), :]
        q = _rope(q, cq_ref[...], sq_ref[...])
        k_all = jnp.concatenate([k_all, _rope(k_ref[pl.ds(s0, SPAN), :], ck, sk)], axis=0)
        v_all = jnp.concatenate([v_all, v_ref[pl.ds(s0, SPAN), :]], axis=0)
        bias = _window_bias(start, s0, n_ctx)
    q = (q * (HEAD_DIM ** -0.5)).astype(BF16)
    return q, k_all.astype(BF16), v_all.astype(BF16), bias, s0, ck, sk


def _stack_heads(x, kh):
    return jnp.concatenate([x[:, (GROUP * kh + g) * HEAD_DIM:(GROUP * kh + g + 1) * HEAD_DIM] for g in range(GROUP)], axis=0)


def _sink_rows(sink, kh):
    return jnp.concatenate([jnp.broadcast_to(sink[:, GROUP * kh + g:GROUP * kh + g + 1], (Q_BLOCK, 1)) for g in range(GROUP)], axis=0)


def _window_bias(start, s0, n_ctx):
    r = lax.broadcasted_iota(jnp.int32, (Q_BLOCK, n_ctx + SPAN), 0)
    c = lax.broadcasted_iota(jnp.int32, (Q_BLOCK, n_ctx + SPAN), 1)
    ok = (c < n_ctx) | (jnp.abs(start - s0 + r - (c - n_ctx)) <= WINDOW)
    return jnp.concatenate([jnp.where(ok, 0.0, NEG).astype(F32)] * GROUP, axis=0)


def _span_start(i, n):
    start = i * Q_BLOCK
    s0 = jnp.clip(start - WINDOW, 0, n - SPAN)
    return start, pl.multiple_of(s0, Q_BLOCK)


def _riding(body, exch, n_in, n_out, grid):
    if exch is None:
        return body, [], [], [], []
    arrs, scatter = exch
    k = len(arrs)

    def wrapped(*refs):
        ins, xin = refs[:n_in], refs[n_in:n_in + k]
        outs, xout = refs[n_in + k:n_in + k + n_out], refs[n_in + k + n_out:n_in + 2 * k + n_out]
        sems = refs[n_in + 2 * k + n_out:]
        b, i = pl.program_id(0), pl.program_id(1)

        @pl.when((b == 0) & (i == 0))
        def _():
            _exch_start(xin, xout, sems, scatter)

        body(*ins, *outs)

        @pl.when((b == grid[0] - 1) & (i == grid[1] - 1))
        def _():
            _exch_wait(xin, xout, sems, scatter)

    any_spec = pl.BlockSpec(memory_space=pl.ANY)
    return wrapped, [any_spec] * k, [any_spec] * k, _exch_out_shapes(arrs, scatter), _exch_sems(k)


def _attn_fwd(u, kvc, sink, cos, sin, *, local, name, exch=None):
    B, n, _ = u.shape
    n_ctx = kvc.shape[1]
    nb = n // Q_BLOCK
    assert (not local) or n >= SPAN

    def body(q_ref, k_ref, v_ref, kc_ref, vc_ref, sink_ref, cq_ref, sq_ref, ck_ref, sk_ref, o_ref):
        q, k_all, v_all, bias, _, _, _ = _attn_operands(q_ref, k_ref, v_ref, kc_ref, vc_ref, cq_ref, sq_ref, ck_ref, sk_ref,
                                                        pl.program_id(1), n, n_ctx, local)
        sink_v = sink_ref[...]
        sl = lambda kh: slice(kh * HEAD_DIM, (kh + 1) * HEAD_DIM)
        ss = [_dot_nt(_stack_heads(q, kh), k_all[:, sl(kh)]) for kh in range(KV_HEADS)]
        ps = [_softmax_sink(s if bias is None else s + bias, _sink_rows(sink_v, kh))[0].astype(BF16) for kh, s in enumerate(ss)]
        for kh, p in enumerate(ps):
            o = _dot(p, v_all[:, sl(kh)])
            for g in range(GROUP):
                h = GROUP * kh + g
                o_ref[:, h * HEAD_DIM:(h + 1) * HEAD_DIM] = o[g * Q_BLOCK:(g + 1) * Q_BLOCK, :]

    seq = lambda blk: pl.BlockSpec((None, n, KV_W), lambda b, i: (b, 0, blk))
    ctxs = lambda blk: pl.BlockSpec((None, n_ctx, KV_W), lambda b, i: (b, 0, blk))
    full = lambda a: pl.BlockSpec(a.shape, lambda b, i: (0,) * a.ndim)
    cos_q, sin_q = jnp.tile(cos, (1, ATTN_HEADS)), jnp.tile(sin, (1, ATTN_HEADS))
    cos_k, sin_k = jnp.tile(cos, (1, KV_HEADS)), jnp.tile(sin, (1, KV_HEADS))
    body, x_in, x_out, x_shapes, x_sems = _riding(body, exch, 10, 1, (B, nb))
    return pl.pallas_call(
        body, name=name, grid=(B, nb),
        in_specs=[pl.BlockSpec((None, Q_BLOCK, ATTN_W), lambda b, i: (b, i, 0)),
                  seq(ATTN_W // KV_W), seq(ATTN_W // KV_W + 1), ctxs(0), ctxs(1), full(sink),
                  pl.BlockSpec((Q_BLOCK, ATTN_W), lambda b, i: (i, 0)), pl.BlockSpec((Q_BLOCK, ATTN_W), lambda b, i: (i, 0)),
                  full(cos_k), full(sin_k)] + x_in,
        out_specs=[pl.BlockSpec((None, Q_BLOCK, ATTN_W), lambda b, i: (b, i, 0))] + x_out,
        out_shape=[jax.ShapeDtypeStruct((B, n, ATTN_W), F32)] + x_shapes,
        scratch_shapes=x_sems,
        compiler_params=_params("arbitrary", "arbitrary"),
    )(u, u, u, kvc, kvc, sink, cos_q, sin_q, cos_k, sin_k, *(exch[0] if exch else []))


def _attn_bwd(u, kvc, sink, cos, sin, do_src, do_blk, *, local, name, exch=None):
    B, n, _ = u.shape
    n_ctx = kvc.shape[1]
    nb = n // Q_BLOCK

    def body(q_ref, k_ref, v_ref, kc_ref, vc_ref, sink_ref, cq_ref, sq_ref, ck_ref, sk_ref, do_ref,
             dq_ref, dk_ref, dv_ref, dkc_ref, dvc_ref, dsink_ref):
        b = pl.program_id(0)
        i = pl.program_id(1)

        @pl.when(i == 0)
        def _():
            dk_ref[...] = jnp.zeros_like(dk_ref)
            dv_ref[...] = jnp.zeros_like(dv_ref)
            dkc_ref[...] = jnp.zeros_like(dkc_ref)
            dvc_ref[...] = jnp.zeros_like(dvc_ref)

        @pl.when((i == 0) & (b == 0))
        def _():
            dsink_ref[...] = jnp.zeros_like(dsink_ref)

        q, k_all, v_all, bias, s0, ck, sk = _attn_operands(q_ref, k_ref, v_ref, kc_ref, vc_ref, cq_ref, sq_ref, ck_ref, sk_ref,
                                                           i, n, n_ctx, local)
        do = do_ref[...].astype(BF16)
        sink_v = sink_ref[...]
        sl = lambda kh: slice(kh * HEAD_DIM, (kh + 1) * HEAD_DIM)
        heads = range(KV_HEADS)
        q_st = [_stack_heads(q, kh) for kh in heads]
        do_st = [_stack_heads(do, kh) for kh in heads]
        ss = [_dot_nt(q_st[kh], k_all[:, sl(kh)]) for kh in heads]
        dps = [_dot_nt(do_st[kh], v_all[:, sl(kh)]) for kh in heads]
        p_bf, ds_bf = [], []
        dsink = jnp.zeros((1, ATTN_HEADS), F32)
        lane8 = lax.broadcasted_iota(jnp.int32, (1, ATTN_HEADS), 1)
        for kh in heads:
            p, p_sink = _softmax_sink(ss[kh] if bias is None else ss[kh] + bias, _sink_rows(sink_v, kh))
            delta = jnp.sum(p * dps[kh], axis=1, keepdims=True)
            ds = p * (dps[kh] - delta)
            dsr = -(p_sink * delta)
            for g in range(GROUP):
                dsink = dsink + jnp.where(lane8 == GROUP * kh + g, jnp.sum(dsr[g * Q_BLOCK:(g + 1) * Q_BLOCK, :]), 0.0)
            p_bf.append(p.astype(BF16))
            ds_bf.append(ds.astype(BF16))
        over_rows = (((0,), (0,)), ((), ()))
        dks, dvs = [], []
        for kh in heads:
            dq_st = _dot(ds_bf[kh], k_all[:, sl(kh)]) * (HEAD_DIM ** -0.5)
            for g in range(GROUP):
                h = GROUP * kh + g
                dq_ref[:, h * HEAD_DIM:(h + 1) * HEAD_DIM] = dq_st[g * Q_BLOCK:(g + 1) * Q_BLOCK, :]
            dvs.append(lax.dot_general(p_bf[kh], do_st[kh], over_rows, preferred_element_type=F32))
            dks.append(lax.dot_general(ds_bf[kh], q_st[kh], over_rows, preferred_element_type=F32))
        dk_cat = jnp.concatenate(dks, axis=1)
        dv_cat = jnp.concatenate(dvs, axis=1)
        dsink_ref[...] += dsink
        dkc_ref[...] += dk_cat[:n_ctx, :]
        dvc_ref[...] += dv_cat[:n_ctx, :]
        if local:
            dq_ref[...] = _rope_bwd(dq_ref[...], cq_ref[...], sq_ref[...])
            dk_ref[pl.ds(s0, SPAN), :] += _rope_bwd(dk_cat[n_ctx:, :], ck, sk)
            dv_ref[pl.ds(s0, SPAN), :] += dv_cat[n_ctx:, :]

    seq = lambda blk: pl.BlockSpec((None, n, KV_W), lambda b, i: (b, 0, blk))
    ctxs = lambda blk: pl.BlockSpec((None, n_ctx, KV_W), lambda b, i: (b, 0, blk))
    full = lambda a: pl.BlockSpec(a.shape, lambda b, i: (0,) * a.ndim)
    qblk = lambda blk: pl.BlockSpec((None, Q_BLOCK, ATTN_W), lambda b, i: (b, i, blk))
    cos_q, sin_q = jnp.tile(cos, (1, ATTN_HEADS)), jnp.tile(sin, (1, ATTN_HEADS))
    cos_k, sin_k = jnp.tile(cos, (1, KV_HEADS)), jnp.tile(sin, (1, KV_HEADS))
    acc = lambda rows: pl.BlockSpec((None, rows, KV_W), lambda b, i: (b, 0, 0))
    body, x_in, x_out, x_shapes, x_sems = _riding(body, exch, 11, 6, (B, nb))
    return pl.pallas_call(
        body, name=name, grid=(B, nb),
        in_specs=[qblk(0), seq(ATTN_W // KV_W), seq(ATTN_W // KV_W + 1), ctxs(0), ctxs(1), full(sink),
                  pl.BlockSpec((Q_BLOCK, ATTN_W), lambda b, i: (i, 0)), pl.BlockSpec((Q_BLOCK, ATTN_W), lambda b, i: (i, 0)),
                  full(cos_k), full(sin_k), qblk(do_blk)] + x_in,
        out_specs=[qblk(0), acc(n), acc(n), acc(n_ctx), acc(n_ctx), pl.BlockSpec((1, ATTN_HEADS), lambda b, i: (0, 0))] + x_out,
        out_shape=[jax.ShapeDtypeStruct((B, n, ATTN_W), F32), jax.ShapeDtypeStruct((B, n, KV_W), F32),
                   jax.ShapeDtypeStruct((B, n, KV_W), F32), jax.ShapeDtypeStruct((B, n_ctx, KV_W), F32),
                   jax.ShapeDtypeStruct((B, n_ctx, KV_W), F32), jax.ShapeDtypeStruct((1, ATTN_HEADS), F32)] + x_shapes,
        scratch_shapes=x_sems,
        compiler_params=_params("arbitrary", "arbitrary"),
    )(u, u, u, kvc, kvc, sink, cos_q, sin_q, cos_k, sin_k, do_src, *(exch[0] if exch else []))


def _conv_chunk(s, n, a_ext, g_ext, dw, dw_b, ln_g, ln_b):
    del s, n
    r = a_ext.shape[0] - 2 * HALO
    h = a_ext * jax.nn.sigmoid(g_ext)
    acc = jnp.broadcast_to(dw_b, (r, CONV_W))
    first = HALO - CONV_KERNEL // 2
    span = r + 8 * ((first + CONV_KERNEL - 1) // 8)
    shifted = [h[b:b + span, :] for b in range(8)]
    for k in range(CONV_KERNEL):
        o = first + k
        acc = acc + shifted[o % 8][o - o % 8:o - o % 8 + r, :] * dw[k:k + 1, :]
    mu = jnp.mean(acc, axis=-1, keepdims=True)
    var = jnp.mean(jnp.square(acc - mu), axis=-1, keepdims=True)
    hn = (acc - mu) * lax.rsqrt(var + EPS) * ln_g + ln_b
    return hn * jax.nn.sigmoid(hn)


def _pool_chunk(s, n, p_ext, w_bd, scale):
    r = p_ext.shape[0] - 2 * HALO
    lane = lax.broadcasted_iota(jnp.int32, (1, POOL_W), 1)
    win = jnp.left_shift(2, lane // POOL_GROUP)
    half = win // 2
    acc = jnp.zeros((r, POOL_W), F32)
    for d in range(-(POOL_WINDOWS[-1] // 2), POOL_WINDOWS[-1] - POOL_WINDOWS[-1] // 2):
        inside = (d >= -half) & (d <= win - 1 - half)
        acc = acc + jnp.where(inside, p_ext[HALO + d:HALO + d + r, :], 0.0)
    t = s + lax.broadcasted_iota(jnp.int32, (r, 1), 0)
    lo = jnp.maximum(t - half, 0)
    hi = jnp.minimum(t + win - 1 - half, n - 1)
    y = acc / (hi - lo + 1).astype(F32) - p_ext[HALO:HALO + r, :]
    out = lax.dot_general(y.astype(BF16), w_bd.astype(BF16), (((1,), (0,)), ((), ())), preferred_element_type=F32)
    return out * scale


def _seq_specs(rows, params):
    specs = [pl.BlockSpec((None, a.shape[1], w), functools.partial(lambda b, blk: (b, 0, blk), blk=blk)) for a, w, blk in rows]
    specs += [pl.BlockSpec(p.shape, functools.partial(lambda b, nd: (0,) * nd, nd=p.ndim)) for p in params]
    return specs


def _fill_padded(pad_ref, row_ref, n):
    w = pad_ref.shape[1]
    pad_ref[pl.ds(0, HALO), :] = jnp.zeros((HALO, w), F32)
    pad_ref[pl.ds(HALO + n, HALO), :] = jnp.zeros((HALO, w), F32)
    pad_ref[pl.ds(HALO, n), :] = row_ref[...]


def _seq_fwd(fn, rows, params, out_w, *, name):
    B, n = rows[0][0].shape[:2]
    r = min(SEQ_CHUNK, n)
    nr, npar = len(rows), len(params)

    def body(*refs):
        row_refs, par_refs, o_ref, pads = refs[:nr], refs[nr:nr + npar], refs[nr + npar], refs[nr + npar + 1:]
        for rr, p in zip(row_refs, pads):
            _fill_padded(p, rr, n)
        pars = [p[...] for p in par_refs]

        def chunk(ci, carry):
            s = pl.multiple_of(ci * r, r)
            ext = [p[pl.ds(s, r + 2 * HALO), :] for p in pads]
            o_ref[pl.ds(s, r), :] = fn(s, n, *ext, *pars)
            return carry

        lax.fori_loop(0, n // r, chunk, 0)

    return pl.pallas_call(
        body, name=name, grid=(B,),
        in_specs=_seq_specs(rows, params),
        out_specs=pl.BlockSpec((None, n, out_w), lambda b: (b, 0, 0)),
        out_shape=jax.ShapeDtypeStruct((B, n, out_w), F32),
        scratch_shapes=[pltpu.VMEM((n + 2 * HALO, w), F32) for _, w, _ in rows],
        compiler_params=_params("parallel"),
    )(*[a for a, _, _ in rows], *params)


def _seq_bwd(fn, rows, params, dout, *, name):
    B, n = rows[0][0].shape[:2]
    r = min(SEQ_CHUNK, n)
    nr, npar = len(rows), len(params)

    def body(*refs):
        row_refs, par_refs, do_ref = refs[:nr], refs[nr:nr + npar], refs[nr + npar]
        outs = refs[nr + npar + 1:]
        drow_refs, dpar_refs = outs[:nr], outs[nr:nr + npar]
        scratch = outs[nr + npar:]
        pads, dpads = scratch[:nr], scratch[nr:]
        for rr, p, dp in zip(row_refs, pads, dpads):
            _fill_padded(p, rr, n)
            dp[...] = jnp.zeros_like(dp)

        @pl.when(pl.program_id(0) == 0)
        def _():
            for d in dpar_refs:
                d[...] = jnp.zeros_like(d)

        pars = [p[...] for p in par_refs]

        def chunk(ci, carry):
            s = pl.multiple_of(ci * r, r)
            ext = [p[pl.ds(s, r + 2 * HALO), :] for p in pads]
            _, vjp = jax.vjp(functools.partial(fn, s, n), *ext, *pars)
            grads = vjp(do_ref[pl.ds(s, r), :])
            for dp, g in zip(dpads, grads[:nr]):
                dp[pl.ds(s, r + 2 * HALO), :] += g
            for d, g in zip(dpar_refs, grads[nr:]):
                d[...] += g
            return carry

        lax.fori_loop(0, n // r, chunk, 0)
        for d, dp in zip(drow_refs, dpads):
            d[...] = dp[pl.ds(HALO, n), :]

    da, dw_, dblk = dout
    return pl.pallas_call(
        body, name=name, grid=(B,),
        in_specs=_seq_specs(rows, params) + [pl.BlockSpec((None, n, dw_), lambda b: (b, 0, dblk))],
        out_specs=[pl.BlockSpec((None, n, w), lambda b: (b, 0, 0)) for _, w, _ in rows]
        + [pl.BlockSpec(p.shape, functools.partial(lambda b, nd: (0,) * nd, nd=p.ndim)) for p in params],
        out_shape=[jax.ShapeDtypeStruct((B, n, w), F32) for _, w, _ in rows]
        + [jax.ShapeDtypeStruct(p.shape, F32) for p in params],
        scratch_shapes=[pltpu.VMEM((n + 2 * HALO, w), F32) for _, w, _ in rows] * 2,
        compiler_params=_params("arbitrary"),
    )(*[a for a, _, _ in rows], *params, da)


_CONV_A_BLK = (ATTN_W + 2 * KV_W) // CONV_W
_CONV_G_BLK = _CONV_A_BLK + 1
_POOL_BLK = _CONV_A_BLK + 2


def _mixer_fwd(tag, u, kvc, margs, local, exch=None):
    sink, dw, dw_b, ln_g, ln_b, w_bd, scale = margs
    cos, sin = _rope_tables(max(u.shape[1], GRID_W))
    attn, *got = _attn_fwd(u, kvc, sink, cos, sin, local=local, name=f"{tag}_attn_fwd", exch=exch)
    conv = _seq_fwd(_conv_chunk, [(u, CONV_W, _CONV_A_BLK), (u, CONV_W, _CONV_G_BLK)], [dw, dw_b, ln_g, ln_b], CONV_W,
                    name=f"{tag}_conv_fwd")
    pool = _seq_fwd(_pool_chunk, [(u, POOL_W, _POOL_BLK)], [w_bd, scale], POOL_W, name=f"{tag}_pool_fwd")
    return jnp.concatenate([attn, conv, pool], axis=-1).astype(BF16), got


def _mixer_bwd(tag, u, kvc, margs, dmix, local, exch=None):
    sink, dw, dw_b, ln_g, ln_b, w_bd, scale = margs
    cos, sin = _rope_tables(max(u.shape[1], GRID_W))
    dq, dk, dv, dkc, dvc, dsink, *got = _attn_bwd(u, kvc, sink, cos, sin, dmix, 0, local=local, name=f"{tag}_attn_bwd",
                                                  exch=exch)
    da, dg, ddw, ddw_b, dln_g, dln_b = _seq_bwd(
        _conv_chunk, [(u, CONV_W, _CONV_A_BLK), (u, CONV_W, _CONV_G_BLK)], [dw, dw_b, ln_g, ln_b],
        (dmix, CONV_W, ATTN_W // CONV_W), name=f"{tag}_conv_bwd")
    dpu, dw_bd, dscale = _seq_bwd(_pool_chunk, [(u, POOL_W, _POOL_BLK)], [w_bd, scale],
                                  (dmix, POOL_W, (ATTN_W + CONV_W) // POOL_W), name=f"{tag}_pool_bwd")
    return (dq, dk, dv, da, dg, dpu), (dkc, dvc), (dsink, ddw, ddw_b, dln_g, dln_b, dw_bd, dscale), got


def _row_specs(arrs, kinds, tr):
    specs = []
    for a, kind in zip(arrs, kinds):
        if kind == "row":
            specs.append(pl.BlockSpec((None, tr, a.shape[2]), lambda b, j: (b, j, 0)))
        elif kind == "batch":
            specs.append(pl.BlockSpec((None, 1, a.shape[2]), lambda b, j: (b, 0, 0)))
        else:
            specs.append(pl.BlockSpec(a.shape, functools.partial(lambda b, j, nd: (0,) * nd, nd=a.ndim)))
    return specs


def _rowwise_fwd(fn, ins, kinds, outs, tr, *, name, transposed=None):
    B, n = ins[0].shape[:2]
    ni, no = len(ins), len(outs)
    nj = n // tr

    def body(*refs):
        res = fn(*[r[...] for r in refs[:ni]])
        for o, v in zip(refs[ni:ni + no], res):
            o[...] = v.astype(o.dtype)
        if transposed is not None:
            refs[ni + no][...] = res[transposed].T.astype(refs[ni + no].dtype)

    out_specs = [pl.BlockSpec((None, tr, w), lambda b, j: (b, j, 0)) for w, _ in outs]
    out_shape = [jax.ShapeDtypeStruct((B, n, w), dt) for w, dt in outs]
    if transposed is not None:
        w, dt = outs[transposed]
        out_specs.append(pl.BlockSpec((w, tr), lambda b, j: (0, b * nj + j)))
        out_shape.append(jax.ShapeDtypeStruct((w, B * n), dt))
    return pl.pallas_call(
        body, name=name, grid=(B, nj),
        in_specs=_row_specs(ins, kinds, tr), out_specs=out_specs, out_shape=out_shape,
        compiler_params=_params("parallel", "parallel"),
    )(*ins)


def _rowwise_bwd(fn, ins, kinds, gdtypes, cts, tr, *, name):
    B, n = ins[0].shape[:2]
    ni, nc = len(ins), len(cts)
    idx = list(range(ni))

    def body(*refs):
        in_refs, ct_refs, out_refs = refs[:ni], refs[ni:ni + nc], refs[ni + nc:]
        b, j = pl.program_id(0), pl.program_id(1)
        _, vjp = jax.vjp(fn, *[r[...].astype(F32) for r in in_refs])
        grads = vjp(tuple(c[...].astype(F32) for c in ct_refs))
        for o, i in zip(out_refs, idx):
            g = grads[i]
            if kinds[i] == "row":
                o[...] = g.astype(o.dtype)
            else:
                first = (j == 0) if kinds[i] == "batch" else ((j == 0) & (b == 0))

                @pl.when(first)
                def _(o=o, g=g):
                    o[...] = g

                @pl.when(jnp.logical_not(first))
                def _(o=o, g=g):
                    o[...] += g

    specs = _row_specs(ins, kinds, tr)
    return pl.pallas_call(
        body, name=name, grid=(B, n // tr),
        in_specs=specs + [pl.BlockSpec((None, tr, c.shape[2]), lambda b, j: (b, j, 0)) for c in cts],
        out_specs=[specs[i] for i in idx],
        out_shape=[jax.ShapeDtypeStruct(ins[i].shape, gdtypes[i]) for i in idx],
        compiler_params=_params("arbitrary", "arbitrary"),
    )(*ins, *cts)


ROW_TILE = 256


def _rms_mod(x, g, sc, sh):
    y = x * lax.rsqrt(jnp.mean(x * x, axis=-1, keepdims=True) + EPS)
    return (y * g) * (1.0 + sc) + sh


def _norm_tile(x, g, sc, sh):
    return x, _rms_mod(x, g, sc, sh)


def _res_norm_tile(xb, y, gate, g, sc, sh):
    x = xb + gate * y
    return x, _rms_mod(x, g, sc, sh)


_NORM_KINDS = ("row", "glob", "batch", "batch")
_RES_NORM_KINDS = ("row", "row", "batch", "glob", "batch", "batch")


def _norm_fwd(tag, st, g, sc, sh):
    xb, y, gate = st
    tr = min(ROW_TILE, xb.shape[1])
    d = xb.shape[2]
    if y is None:
        h, h_t = _rowwise_fwd(lambda *a: (_rms_mod(*a),), [xb, g, sc, sh], _NORM_KINDS, [(d, BF16)], tr,
                              name=f"{tag}_fwd", transposed=0)
        return xb, h, h_t
    return _rowwise_fwd(_res_norm_tile, [xb, y, gate, g, sc, sh], _RES_NORM_KINDS, [(d, F32), (d, BF16)], tr,
                        name=f"{tag}_fwd", transposed=1)


def _norm_bwd(tag, st, g, sc, sh, dx, dh):
    xb, y, gate = st
    tr = min(ROW_TILE, xb.shape[1])
    if y is None:
        dxb, dg, dsc, dsh = _rowwise_bwd(_norm_tile, [xb, g, sc, sh], _NORM_KINDS, [F32] * 4, [dx, dh], tr, name=f"{tag}_bwd")
        return dxb, None, None, dg, dsc, dsh
    return tuple(_rowwise_bwd(_res_norm_tile, [xb, y, gate, g, sc, sh], _RES_NORM_KINDS, [F32, BF16, F32, F32, F32, F32],
                              [dx, dh], tr, name=f"{tag}_bwd"))


def _loss_head(st, final_g, target, *, name):
    xb, y, gate = st
    B, n, d = xb.shape
    tr = min(ROW_TILE, n)

    def tile_loss(xv, yv, gt, g, t):
        x = xv + gt * yv
        out = x * lax.rsqrt(jnp.mean(x * x, axis=-1, keepdims=True) + EPS) * g
        return 0.5 * jnp.sum(jnp.mean(jnp.square(out - t), axis=-1))

    def body(x_ref, y_ref, gate_ref, g_ref, t_ref, loss_ref, dx_ref, dy_ref, dgate_ref, dg_ref):
        b, j = pl.program_id(0), pl.program_id(1)
        val, (dx, dy, dgate, dg) = jax.value_and_grad(tile_loss, argnums=(0, 1, 2, 3))(
            x_ref[...], y_ref[...], gate_ref[...], g_ref[...], t_ref[...])
        dx_ref[...] = dx
        dy_ref[...] = dy.astype(dy_ref.dtype)

        @pl.when(j == 0)
        def _():
            loss_ref[...] = jnp.zeros_like(loss_ref)
            dgate_ref[...] = jnp.zeros_like(dgate_ref)

        @pl.when((j == 0) & (b == 0))
        def _():
            dg_ref[...] = jnp.zeros_like(dg_ref)

        loss_ref[...] += jnp.full(loss_ref.shape, val, F32)
        dgate_ref[...] += dgate
        dg_ref[...] += dg

    row = pl.BlockSpec((None, tr, d), lambda b, j: (b, j, 0))
    per_sample = pl.BlockSpec((None, 1, d), lambda b, j: (b, 0, 0))
    whole = pl.BlockSpec((1, d), lambda b, j: (0, 0))
    return pl.pallas_call(
        body, name=name, grid=(B, n // tr),
        in_specs=[row, row, per_sample, whole, row],
        out_specs=[pl.BlockSpec((None, 1, 128), lambda b, j: (b, 0, 0)), row, row, per_sample, whole],
        out_shape=[jax.ShapeDtypeStruct((B, 1, 128), F32), jax.ShapeDtypeStruct((B, n, d), F32),
                   jax.ShapeDtypeStruct((B, n, d), BF16), jax.ShapeDtypeStruct((B, 1, d), F32), jax.ShapeDtypeStruct((1, d), F32)],
        compiler_params=_params("arbitrary", "arbitrary"),
    )(xb, y, gate, final_g, target)


def _exchange(arrs, *, scatter, name):
    k = len(arrs)

    def body(*refs):
        ins, outs, sems = refs[:k], refs[k:2 * k], refs[2 * k:]
        _exch_start(ins, outs, sems, scatter)
        _exch_wait(ins, outs, sems, scatter)

    any_spec = pl.BlockSpec(memory_space=pl.ANY)
    return pl.pallas_call(
        body, name=name,
        in_specs=[any_spec] * k, out_specs=[any_spec] * k,
        out_shape=_exch_out_shapes(arrs, scatter), scratch_shapes=_exch_sems(k),
        compiler_params=pltpu.CompilerParams(has_side_effects=True),
    )(*arrs)


def _exch_out_shapes(arrs, scatter):
    return [jax.ShapeDtypeStruct(a.shape if scatter else (N_DEV,) + a.shape, a.dtype) for a in arrs]


def _exch_sems(k):
    return [pltpu.SemaphoreType.DMA((k * (N_DEV - 1),)), pltpu.SemaphoreType.DMA((k * (N_DEV - 1),)),
            pltpu.SemaphoreType.DMA((k,))]


def _exch_copies(ins, outs, sems, scatter):
    send_sems, recv_sems, local_sems = sems
    x, y, c = lax.axis_index("x"), lax.axis_index("y"), lax.axis_index("c")
    me = 4 * x + 2 * y + c
    owns, sends, recvs = [], [], []
    for a in range(len(ins)):
        owns.append(pltpu.make_async_copy(ins[a].at[me] if scatter else ins[a], outs[a].at[me], local_sems.at[a]))
        for r in range(1, N_DEV):
            fx, fy, fc = (r >> 2) & 1, (r >> 1) & 1, r & 1
            px, py, pc = (x + fx) % 2, (y + fy) % 2, (c + fc) % 2
            peer = 4 * px + 2 * py + pc
            s = a * (N_DEV - 1) + r - 1
            mk = functools.partial(pltpu.make_async_remote_copy, src_ref=ins[a].at[peer] if scatter else ins[a],
                                   send_sem=send_sems.at[s], recv_sem=recv_sems.at[s],
                                   device_id=(px, py, pc), device_id_type=pl.DeviceIdType.MESH)
            sends.append(mk(dst_ref=outs[a].at[me]))
            recvs.append(mk(dst_ref=outs[a].at[peer]))
    return owns, sends, recvs


def _exch_start(ins, outs, sems, scatter):
    owns, sends, _ = _exch_copies(ins, outs, sems, scatter)
    for cp in owns + sends:
        cp.start()


def _exch_wait(ins, outs, sems, scatter):
    owns, sends, recvs = _exch_copies(ins, outs, sems, scatter)
    for rc in recvs:
        rc.wait_recv()
    for cp in sends:
        cp.wait_send()
    for own in owns:
        own.wait()


MOD_ROWS = 48


def _mod_tile(cc, w, b):
    s = cc * jax.nn.sigmoid(cc)
    return lax.dot_general(s.astype(BF16), w.astype(BF16), (((1,), (0,)), ((), ())), preferred_element_type=F32) + b


def _mod_fwd(cc, w_mod, b_shard, *, name):
    L, d, wcols = w_mod.shape

    def body(cc_ref, w_ref, b_ref, o_ref):
        o_ref[...] = _mod_tile(cc_ref[...], w_ref[...], b_ref[...])

    return pl.pallas_call(
        body, name=name, grid=(L,),
        in_specs=[pl.BlockSpec((MOD_ROWS, d), lambda l: (0, 0)), pl.BlockSpec((None, d, wcols), lambda l: (l, 0, 0)),
                  pl.BlockSpec((None, 1, wcols), lambda l: (l, 0, 0))],
        out_specs=pl.BlockSpec((None, MOD_ROWS, wcols), lambda l: (l, 0, 0)),
        out_shape=jax.ShapeDtypeStruct((L, MOD_ROWS, wcols), F32),
        compiler_params=_params("parallel"),
    )(cc, w_mod, b_shard)


def _mod_bwd(cc, w_mod, b_shard, dm, *, name):
    L, d, wcols = w_mod.shape

    def body(cc_ref, w_ref, b_ref, dm_ref, dcc_ref, dw_ref):
        _, vjp = jax.vjp(_mod_tile, cc_ref[...], w_ref[...], b_ref[...])
        dcc, dw, _ = vjp(dm_ref[...])
        dw_ref[...] = dw

        @pl.when(pl.program_id(0) == 0)
        def _():
            dcc_ref[...] = dcc

        @pl.when(pl.program_id(0) > 0)
        def _():
            dcc_ref[...] += dcc

    return pl.pallas_call(
        body, name=name, grid=(L,),
        in_specs=[pl.BlockSpec((MOD_ROWS, d), lambda l: (0, 0)), pl.BlockSpec((None, d, wcols), lambda l: (l, 0, 0)),
                  pl.BlockSpec((None, 1, wcols), lambda l: (l, 0, 0)), pl.BlockSpec((None, MOD_ROWS, wcols), lambda l: (l, 0, 0))],
        out_specs=[pl.BlockSpec((MOD_ROWS, d), lambda l: (0, 0)), pl.BlockSpec((None, d, wcols), lambda l: (l, 0, 0))],
        out_shape=[jax.ShapeDtypeStruct((MOD_ROWS, d), F32), jax.ShapeDtypeStruct((L, d, wcols), F32)],
        compiler_params=_params("arbitrary"),
    )(cc, w_mod, b_shard, dm)


def _sum_leading(a, *, name):
    K, R, C = a.shape
    tr = _tile8(R, 256)

    def body(a_ref, o_ref):
        acc = a_ref[0].astype(F32)
        for i in range(1, K):
            acc = acc + a_ref[i].astype(F32)
        o_ref[...] = acc

    return pl.pallas_call(
        body, name=name, grid=(R // tr,),
        in_specs=[pl.BlockSpec((K, tr, C), lambda i: (0, i, 0))],
        out_specs=pl.BlockSpec((tr, C), lambda i: (i, 0)),
        out_shape=jax.ShapeDtypeStruct((R, C), F32),
        compiler_params=_params("parallel"),
    )(a)


def _tile8(dim, target):
    if dim <= target:
        return dim
    t = (target // 8) * 8
    while t >= 8:
        if dim % t == 0:
            return t
        t -= 8
    raise ValueError(f"no row tile for {dim}")


def _adamw_math(g, w, m, v):
    m = ADAM_B1 * m + (1.0 - ADAM_B1) * g
    v = ADAM_B2 * v + (1.0 - ADAM_B2) * jnp.square(g)
    m_hat = m / (1.0 - ADAM_B1 ** ADAM_STEP)
    v_hat = v / (1.0 - ADAM_B2 ** ADAM_STEP)
    delta = -ADAM_LR * (m_hat / (jnp.sqrt(v_hat) + ADAM_EPS) + ADAM_WD * w)
    return delta, m, v


def _adamw(g, w, m, v, *, name):
    L, R, C = w.shape
    parts = isinstance(g, (list, tuple))
    gs = list(g) if parts else [g]
    ng = len(gs)
    tr = _tile8(R, 256)

    def body(*refs):
        g_refs = refs[:ng]
        w_ref, m_ref, v_ref, go_ref, d_ref, mo_ref, vo_ref = refs[ng:]
        if parts:
            layer = pl.program_id(0)
            gv = None
            for li, g_ref in enumerate(g_refs):
                acc = g_ref[0].astype(F32)
                for i in range(1, N_DEV):
                    acc = acc + g_ref[i].astype(F32)
                gv = acc if gv is None else jnp.where(layer == li, acc, gv)
        else:
            gv = g_refs[0][...]
        go_ref[...] = gv
        d_ref[...], mo_ref[...], vo_ref[...] = _adamw_math(gv, w_ref[...], m_ref[...], v_ref[...])

    tile = pl.BlockSpec((None, tr, C), lambda l, i: (l, i, 0))
    g_specs = [pl.BlockSpec((N_DEV, tr, C), lambda l, i: (0, i, 0))] * ng if parts else [tile]
    return pl.pallas_call(
        body, name=name, grid=(L, R // tr),
        in_specs=g_specs + [tile, tile, tile], out_specs=[tile] * 4,
        out_shape=[jax.ShapeDtypeStruct((L, R, C), F32)] * 4,
        compiler_params=_params("parallel", "parallel"),
    )(*gs, w, m, v)


def _adamw_small(gs, ws, ms, vs, *, name):
    k = len(ws)

    def body(*refs):
        g_refs, w_refs, m_refs, v_refs = refs[:k], refs[k:2 * k], refs[2 * k:3 * k], refs[3 * k:4 * k]
        d_refs, mo_refs, vo_refs = refs[4 * k:5 * k], refs[5 * k:6 * k], refs[6 * k:]
        for i in range(k):
            d_refs[i][...], mo_refs[i][...], vo_refs[i][...] = _adamw_math(g_refs[i][...], w_refs[i][...], m_refs[i][...],
                                                                         v_refs[i][...])

    shapes = [jax.ShapeDtypeStruct(a.shape, F32) for a in ws]
    out = pl.pallas_call(body, name=name, out_shape=shapes * 3, compiler_params=pltpu.CompilerParams(vmem_limit_bytes=VMEM_LIMIT))(
        *gs, *ws, *ms, *vs)
    return out[:k], out[k:2 * k], out[2 * k:]


def _block_diag(w):
    g = w.shape[0]
    rows = [jnp.concatenate([w[i] if j == i else jnp.zeros_like(w[i]) for j in range(g)], axis=1) for i in range(g)]
    return jnp.concatenate(rows, axis=0)


def _diag_blocks(w_bd):
    g = POOL_W // POOL_GROUP
    return jnp.stack([w_bd[i * POOL_GROUP:(i + 1) * POOL_GROUP, i * POOL_GROUP:(i + 1) * POOL_GROUP] for i in range(g)])


def _flat(a):
    return a.reshape(-1, a.shape[-1])


def _mix_half_fwd(tag, st, mods, wl, kvc, *, local, kv_only, exch=None):
    sh1, sc1, g1 = mods[:3]
    B, n, d = st[0].shape
    x, h, h_t = _norm_fwd(f"{tag}_norm1", st, wl["n1"], sc1, sh1)
    if kv_only:
        kv = _mm(_flat(h), wl["w_in"][:, ATTN_W:ATTN_W + 2 * KV_W], name=f"{tag}_kv").reshape(B, n, 2 * KV_W)
        return None, dict(st=st, h_t=h_t, kvc=kv), []
    u = _mm(_flat(h), wl["w_in"], name=f"{tag}_in").reshape(B, n, IN_W)
    if not local:
        kvc = u[:, :, ATTN_W:ATTN_W + 2 * KV_W]
    mix, got = _mixer_fwd(f"{tag}_mix", u, kvc, wl["margs"], local, exch)
    y = _mm(_flat(mix), wl["w_out"], name=f"{tag}_out").reshape(B, n, d)
    return (x, y, g1), dict(st=st, h_t=h_t, u=u, kvc=kvc, mix=mix), got


def _ffn_half_fwd(tag, st2, mods, wl, exch=None):
    sh2, sc2, g2 = mods[3:]
    B, n, d = st2[0].shape
    x1, h2, h2_t = _norm_fwd(f"{tag}_norm2", st2, wl["n2"], sc2, sh2)
    gu, act, act_t, *got = _mm_swiglu(_flat(h2), wl["w_ffn_in"], name=f"{tag}_ffn_in", exch=exch)
    y2 = _mm(act, wl["w_ffn_out"], name=f"{tag}_ffn_out").reshape(B, n, d)
    return (x1, y2, g2), dict(st2=st2, h2_t=h2_t, gu=gu, act_t=act_t), got


def _ffn_half_bwd(tag, sv, mods, wl, dx1, dy2):
    sh2, sc2, _ = mods[3:]
    B, n, d = sv["st2"][0].shape
    gw = {}
    dy2f = _flat(dy2)
    gw["w_ffn_out"] = _mm(sv["act_t"], dy2f, out_dtype=BF16, name=f"{tag}_ffn_out_dw")
    dgu = _mm_dswiglu(dy2f, wl["w_ffn_out"], sv["gu"], name=f"{tag}_ffn_out_dx")
    dh2 = _mm(dgu, wl["w_ffn_in"], trans_b=True, name=f"{tag}_ffn_in_dx").reshape(B, n, d)
    gw["w_ffn_in"] = _deinterleave_ffn(_mm(sv["h2_t"], dgu, out_dtype=BF16, name=f"{tag}_ffn_in_dw"))
    dx, dy, dg1, gw["n2"], dsc2, dsh2 = _norm_bwd(f"{tag}_norm2", sv["st2"], wl["n2"], sc2, sh2, dx1, dh2)
    return (dx, dy, dg1), gw, dict(sh2=dsh2, sc2=dsc2)


def _mix_half_bwd(tag, sv, mods, wl, dx, dy, dkv_in, *, local, kv_only, exch=None):
    sh1, sc1, _ = mods[:3]
    B, n, d = sv["st"][0].shape
    gw = {}
    if kv_only:
        dkv = _flat(dkv_in).astype(BF16)
        dh = _mm(dkv, wl["w_in"][:, ATTN_W:ATTN_W + 2 * KV_W], trans_b=True, name=f"{tag}_kv_dx").reshape(B, n, d)
        gw["w_in_kv"] = _mm(sv["h_t"], dkv, out_dtype=BF16, name=f"{tag}_kv_dw")
        dxb, dy_prev, dgate_prev, gw["n1"], dsc1, dsh1 = _norm_bwd(f"{tag}_norm1", sv["st"], wl["n1"], sc1, sh1,
                                                                    jnp.zeros((B, n, d), F32), dh)
        return (dxb, dy_prev, dgate_prev), gw, dict(sh1=dsh1, sc1=dsc1), None, []

    dyf = _flat(dy)
    dmix = _mm(dyf, wl["w_out"], trans_b=True, name=f"{tag}_out_dx").reshape(B, n, d)
    gw["w_out"] = _mm(_flat(sv["mix"]).T, dyf, out_dtype=BF16, name=f"{tag}_out_dw")
    (dq, dk, dv, da, dg, dpu), (dkc, dvc), gw["margs"], got = _mixer_bwd(f"{tag}_mix", sv["u"], sv["kvc"], wl["margs"], dmix,
                                                                     local, exch)
    if local:
        dkv_out = jnp.concatenate([dkc, dvc], axis=-1)
    else:
        dk = dkc + dkv_in[:, :, :KV_W]
        dv = dvc + dkv_in[:, :, KV_W:]
        dkv_out = None
    du = _flat(jnp.concatenate([dq, dk, dv, da, dg, dpu], axis=-1).astype(BF16))
    dh = _mm(du, wl["w_in"], trans_b=True, name=f"{tag}_in_dx").reshape(B, n, d)
    gw["w_in"] = _mm(sv["h_t"], du, out_dtype=BF16, name=f"{tag}_in_dw")
    dxb, dy_prev, dgate_prev, gw["n1"], dsc1, dsh1 = _norm_bwd(f"{tag}_norm1", sv["st"], wl["n1"], sc1, sh1, dx, dh)
    return (dxb, dy_prev, dgate_prev), gw, dict(sh1=dsh1, sc1=dsc1), dkv_out, got


BIG_W = ("w_in", "w_out", "w_ffn_in", "w_ffn_out")


def _local_step(x, ctx, m_loc, m_ctx, p, final_g, target, big):
    B = x.shape[0]
    depth = m_loc.shape[0]
    lat_mods = [[t[:, None, :] for t in jnp.split(m_loc[l], 6, axis=-1)] for l in range(depth)]
    ctx_mods = [[jnp.broadcast_to(t[None, None, :], (B, 1, D_MODEL)) for t in jnp.split(m_ctx[l], 6)] for l in range(depth)]

    st, cst = (x, None, None), (ctx, None, None)
    w_mix, w_ffn, sv_mix, sv_ffn, csv_mix, csv_ffn = [], [], [], [], [], []
    got = []
    for l in range(depth):
        last = l == depth - 1
        wb = big.mix_weights(l, got)
        wm = dict(n1=p["norm1_g"][l][None, :], w_in=wb["w_in"], w_out=wb["w_out"],
                  margs=(p["attn_sink"][l][None, :], p["conv_dw"][l], p["conv_dw_b"][l][None, :], p["conv_ln_g"][l][None, :],
                         p["conv_ln_b"][l][None, :], _block_diag(p["pool_w"][l]), p["pool_scale"][l][None, :]))
        cst, csv, _ = _mix_half_fwd(f"l{l}c", cst, ctx_mods[l], wm, None, local=False, kv_only=last)
        st, sv, got = _mix_half_fwd(f"l{l}", st, lat_mods[l], wm, csv["kvc"], local=True, kv_only=False,
                                    exch=big.ride_attn_fwd(l))
        w_mix.append(wm)
        sv_mix.append(sv)
        csv_mix.append(csv)
        wb = big.ffn_weights(l, got)
        w_ffn_in = _interleave_ffn(wb["w_ffn_in"])
        wf = dict(n2=p["norm2_g"][l][None, :], w_ffn_in=w_ffn_in, w_ffn_out=wb["w_ffn_out"])
        csv = None
        if not last:
            cst, csv, _ = _ffn_half_fwd(f"l{l}c", cst, ctx_mods[l], wf)
        st, sv, got = _ffn_half_fwd(f"l{l}", st, lat_mods[l], wf, exch=big.ride_ffn_fwd(l))
        w_ffn.append(wf)
        sv_ffn.append(sv)
        csv_ffn.append(csv)
    loss_rows, dx, dy, dgate, dfinal = _loss_head(st, final_g[None, :], target, name="loss_head")

    dm_loc, dm_ctx = [None] * depth, [None] * depth
    small = [None] * depth
    cdx = cdy = cdgate = None
    up_mix = None
    for l in reversed(range(depth)):
        last = l == depth - 1
        dm, cdm = dict(g2=dgate), {}
        (dx, dy, dm["g1"]), gf, d = _ffn_half_bwd(f"l{l}", sv_ffn[l], lat_mods[l], w_ffn[l], dx, dy)
        dm.update(d)
        if not last:
            cdm["g2"] = cdgate
            (cdx, cdy, cdm["g1"]), cgf, d = _ffn_half_bwd(f"l{l}c", csv_ffn[l], ctx_mods[l], w_ffn[l], cdx, cdy)
            cdm.update(d)
            gf = {k: gf[k] + cgf[k] for k in gf}
        ffn_grads = {k: gf[k] for k in _ShardedWeights.FFN}
        (dx, dy, dgate), gm, d, dkv, got = _mix_half_bwd(f"l{l}", sv_mix[l], lat_mods[l], w_mix[l], dx, dy, None, local=True,
                                                        kv_only=False, exch=big.ride_attn_bwd(l, ffn_grads, up_mix))
        big.took(l, ffn_grads, up_mix, got)
        dm.update(d)
        (cdx, cdy, cdgate), cgm, d, _, _ = _mix_half_bwd(f"l{l}c", csv_mix[l], ctx_mods[l], w_mix[l], cdx, cdy, dkv,
                                                        local=False, kv_only=last)
        cdm.update(d)
        order = ("sh1", "sc1", "g1", "sh2", "sc2", "g2")
        dm_loc[l] = jnp.concatenate([dm[k][:, 0, :] for k in order], axis=-1)
        dm_ctx[l] = jnp.concatenate([jnp.sum(cdm[k][:, 0, :], axis=0) if k in cdm else jnp.zeros((D_MODEL,), F32)
                                     for k in order])
        if last:
            up_mix = dict(w_in=gm["w_in"].at[:, ATTN_W:ATTN_W + 2 * KV_W].add(cgm["w_in_kv"]), w_out=gm["w_out"])
            margs = gm["margs"]
        else:
            up_mix = {k: gm[k] + cgm[k] for k in _ShardedWeights.MIX}
            margs = tuple(a + b for a, b in zip(gm["margs"], cgm["margs"]))
        small[l] = dict(n1=gm["n1"] + cgm["n1"], n2=gf["n2"], margs=margs)
    big.leftover(up_mix)

    stack = lambda f: jnp.stack([f(small[l]) for l in range(depth)])
    dp = dict(
        norm1_g=stack(lambda g: g["n1"][0]), norm2_g=stack(lambda g: g["n2"][0]),
        attn_sink=stack(lambda g: g["margs"][0][0]), conv_dw=stack(lambda g: g["margs"][1]),
        conv_dw_b=stack(lambda g: g["margs"][2][0]), conv_ln_g=stack(lambda g: g["margs"][3][0]),
        conv_ln_b=stack(lambda g: g["margs"][4][0]), pool_w=stack(lambda g: _diag_blocks(g["margs"][5])),
        pool_scale=stack(lambda g: g["margs"][6][0]))
    return jnp.sum(loss_rows[:, 0, 0]), dx, jnp.stack(dm_loc), jnp.stack(dm_ctx), dp, dfinal[0]


PACK_COLS = 1024


def _pack(arrs):
    flat = jnp.concatenate([a.reshape(-1).astype(F32) for a in arrs])
    rows = -(-flat.shape[0] // (8 * PACK_COLS)) * 8
    return jnp.pad(flat, (0, rows * PACK_COLS - flat.shape[0])).reshape(rows, PACK_COLS)


def _unpack(slab, like):
    flat = slab.reshape(-1)
    out, off = [], 0
    for a in like:
        out.append(flat[off:off + a.size].reshape(a.shape))
        off += a.size
    return out


def _shard_cols(gathered):
    _, L, R, C = gathered.shape
    return jnp.transpose(gathered, (1, 2, 0, 3)).reshape(L, R, N_DEV * C)


class _ShardedWeights:
    MIX = ("w_in", "w_out")
    FFN = ("w_ffn_in", "w_ffn_out")
    BY_COLS = ("w_in", "w_ffn_in")

    def __init__(self, shards, first):
        self.shards = shards
        self.first = first
        self.depth = shards[BIG_W[0]].shape[0]
        self.parts = [dict() for _ in range(self.depth)]
        self.left = None

    def _join(self, names, blocks):
        out = {}
        for name, g in zip(names, blocks):
            _, R, C = g.shape
            out[name] = jnp.transpose(g, (1, 0, 2)).reshape(R, N_DEV * C) if name in self.BY_COLS else g.reshape(N_DEV * R, C)
        return out

    def cut(self, names, grads):
        out = []
        for name in names:
            g = grads[name]
            if name in self.BY_COLS:
                R, C8 = g.shape
                out.append(jnp.transpose(g.reshape(R, N_DEV, C8 // N_DEV), (1, 0, 2)))
            else:
                R8, C = g.shape
                out.append(g.reshape(N_DEV, R8 // N_DEV, C))
        return out

    def mix_weights(self, l, got):
        return self._join(self.MIX, self.first if l == 0 else got)

    def ffn_weights(self, l, got):
        return self._join(self.FFN, got)

    def ride_attn_fwd(self, l):
        return [self.shards[name][l] for name in self.FFN], False

    def ride_ffn_fwd(self, l):
        if l + 1 >= self.depth:
            return None
        return [self.shards[name][l + 1] for name in self.MIX], False

    def ride_attn_bwd(self, l, ffn_grads, up_mix):
        return self.cut(self.FFN, ffn_grads) + (self.cut(self.MIX, up_mix) if up_mix is not None else []), True

    def took(self, l, ffn_grads, up_mix, got):
        self.parts[l].update(zip(self.FFN, got[:2]))
        if up_mix is not None:
            self.parts[l + 1].update(zip(self.MIX, got[2:]))

    def leftover(self, mix_grads):
        self.left = mix_grads


def _as_rows(a, leading=0):
    return a.reshape(*a.shape[:leading], -1, PACK_COLS)


SMALL = ("c_ctx", "b_mod", "norm1_g", "norm2_g", "conv_dw_b", "conv_ln_g", "conv_ln_b", "attn_sink", "pool_w",
         "pool_scale", "final_g", "conv_dw")
BIG = ("w_mod", "w_in", "w_out", "w_ffn_in", "w_ffn_out")
ORDER = ("c_ctx", "w_mod", "b_mod", "norm1_g", "norm2_g", "w_in", "conv_dw", "conv_dw_b", "conv_ln_g", "conv_ln_b",
         "attn_sink", "pool_w", "pool_scale", "w_out", "w_ffn_in", "w_ffn_out", "final_g")


def kernel(x, c, ctx, c_ctx, w_mod, b_mod, norm1_g, norm2_g, w_in, conv_dw, conv_dw_b, conv_ln_g, conv_ln_b, attn_sink, pool_w, pool_scale, w_out, w_ffn_in, w_ffn_out, final_g, loss_target, m_c_ctx, m_w_mod, m_b_mod, m_norm1_g, m_norm2_g, m_w_in, m_conv_dw, m_conv_dw_b, m_conv_ln_g, m_conv_ln_b, m_attn_sink, m_pool_w, m_pool_scale, m_w_out, m_w_ffn_in, m_w_ffn_out, m_final_g, v_c_ctx, v_w_mod, v_b_mod, v_norm1_g, v_norm2_g, v_w_in, v_conv_dw, v_conv_dw_b, v_conv_ln_g, v_conv_ln_b, v_attn_sink, v_pool_w, v_pool_scale, v_w_out, v_w_ffn_in, v_w_ffn_out, v_final_g):
    w = dict(c_ctx=c_ctx, w_mod=w_mod, b_mod=b_mod, norm1_g=norm1_g, norm2_g=norm2_g, w_in=w_in, conv_dw=conv_dw,
             conv_dw_b=conv_dw_b, conv_ln_g=conv_ln_g, conv_ln_b=conv_ln_b, attn_sink=attn_sink, pool_w=pool_w,
             pool_scale=pool_scale, w_out=w_out, w_ffn_in=w_ffn_in, w_ffn_out=w_ffn_out, final_g=final_g)
    mom = dict(c_ctx=m_c_ctx, w_mod=m_w_mod, b_mod=m_b_mod, norm1_g=m_norm1_g, norm2_g=m_norm2_g, w_in=m_w_in,
               conv_dw=m_conv_dw, conv_dw_b=m_conv_dw_b, conv_ln_g=m_conv_ln_g, conv_ln_b=m_conv_ln_b,
               attn_sink=m_attn_sink, pool_w=m_pool_w, pool_scale=m_pool_scale, w_out=m_w_out, w_ffn_in=m_w_ffn_in,
               w_ffn_out=m_w_ffn_out, final_g=m_final_g)
    var = dict(c_ctx=v_c_ctx, w_mod=v_w_mod, b_mod=v_b_mod, norm1_g=v_norm1_g, norm2_g=v_norm2_g, w_in=v_w_in,
               conv_dw=v_conv_dw, conv_dw_b=v_conv_dw_b, conv_ln_g=v_conv_ln_g, conv_ln_b=v_conv_ln_b,
               attn_sink=v_attn_sink, pool_w=v_pool_w, pool_scale=v_pool_scale, w_out=v_w_out, w_ffn_in=v_w_ffn_in,
               w_ffn_out=v_w_ffn_out, final_g=v_final_g)
    B = x.shape[0]
    depth = w_mod.shape[0]
    mod_cols = w_mod.shape[2]
    dw_cols = conv_dw.shape[2]
    me = 4 * lax.axis_index("x") + 2 * lax.axis_index("y") + lax.axis_index("c")

    shards = {name: w[name].astype(BF16) for name in BIG_W}
    c_all, dw_all, *first = _exchange([c, conv_dw] + [shards[name][0] for name in _ShardedWeights.MIX], scatter=False,
                                      name="gather_first")
    big = _ShardedWeights(shards, first)
    p = dict(norm1_g=norm1_g, norm2_g=norm2_g, conv_dw=_shard_cols(dw_all), conv_dw_b=conv_dw_b, conv_ln_g=conv_ln_g,
             conv_ln_b=conv_ln_b, attn_sink=attn_sink, pool_w=pool_w, pool_scale=pool_scale)

    cc = jnp.concatenate([c_all.reshape(N_DEV * B, D_MODEL), jnp.broadcast_to(c_ctx[None, :], (N_DEV, D_MODEL)),
                          jnp.zeros((MOD_ROWS - N_DEV * B - N_DEV, D_MODEL), F32)], axis=0)
    b_shard = lax.dynamic_slice_in_dim(b_mod, me * mod_cols, mod_cols, axis=1)[:, None, :]
    m_part = _mod_fwd(cc, w_mod, b_shard, name="mod_fwd")
    m_all, = _exchange([m_part], scatter=False, name="gather_mod")
    m_full = _shard_cols(m_all)
    m_loc = lax.dynamic_slice_in_dim(m_full, me * B, B, axis=1)
    m_ctx = m_full[:, N_DEV * B, :]

    loss_part, dx, dm_loc, dm_ctx, dp, dfinal = _local_step(x, ctx, m_loc, m_ctx, p, final_g, loss_target, big)
    loss = lax.psum(loss_part, AXES)

    dm_rows = jnp.concatenate([dm_loc, dm_ctx[:, None, :], jnp.zeros((depth, 8 - B - 1, 6 * D_MODEL), F32)], axis=1)
    dm_all, = _exchange([dm_rows], scatter=False, name="gather_dmod")
    dm_full = jnp.concatenate([
        jnp.transpose(dm_all[:, :, :B, :], (1, 0, 2, 3)).reshape(depth, N_DEV * B, 6 * D_MODEL),
        jnp.transpose(dm_all[:, :, B, :], (1, 0, 2)),
        jnp.zeros((depth, MOD_ROWS - N_DEV * B - N_DEV, 6 * D_MODEL), F32)], axis=1)
    g_b_mod = jnp.stack([_sum_leading(dm_full[l][:, None, :], name=f"b_mod_grad{l}")[0] for l in range(depth)])
    dm_mine = lax.dynamic_slice_in_dim(dm_full, me * mod_cols, mod_cols, axis=2)
    dcc, g_w_mod = _mod_bwd(cc, w_mod, b_shard, dm_mine, name="mod_bwd")
    g_c_ctx_part = jnp.sum(dcc[N_DEV * B:N_DEV * B + N_DEV], axis=0)

    small_like = [c_ctx, norm1_g, norm2_g, conv_dw_b, conv_ln_g, conv_ln_b, attn_sink, pool_w, pool_scale, final_g,
                  dp["conv_dw"]]
    small_part = _pack([g_c_ctx_part, dp["norm1_g"], dp["norm2_g"], dp["conv_dw_b"], dp["conv_ln_g"], dp["conv_ln_b"],
                        dp["attn_sink"], dp["pool_w"], dp["pool_scale"], dfinal, dp["conv_dw"]])
    small_all, = _exchange([small_part], scatter=False, name="gather_small")
    small_sum = _unpack(_sum_leading(small_all, name="sum_small"), small_like)
    g = dict(zip(("c_ctx", "norm1_g", "norm2_g", "conv_dw_b", "conv_ln_g", "conv_ln_b", "attn_sink", "pool_w",
                  "pool_scale", "final_g"), small_sum[:-1]))
    g["b_mod"] = g_b_mod
    g["conv_dw"] = lax.dynamic_slice_in_dim(small_sum[-1], me * dw_cols, dw_cols, axis=2)

    big.parts[0].update(zip(big.MIX, _exchange(big.cut(big.MIX, big.left), scatter=True, name="scatter_last")))

    delta, new_m, new_v = {}, {}, {}
    for name in BIG_W:
        g[name], delta[name], new_m[name], new_v[name] = _adamw(
            [big.parts[l][name] for l in range(depth)], w[name], mom[name], var[name], name=f"adamw_{name}")
    g["w_mod"], delta["w_mod"], new_m["w_mod"], new_v["w_mod"] = _adamw(g_w_mod, w_mod, m_w_mod, v_w_mod, name="adamw_w_mod")
    res = _adamw_small([g[k] for k in SMALL], [w[k] for k in SMALL], [mom[k] for k in SMALL], [var[k] for k in SMALL],
                       name="adamw_small")
    for dst, arrs in zip((delta, new_m, new_v), res):
        dst.update(zip(SMALL, arrs))

    return (loss, dx, *[g[k] for k in ORDER], *[delta[k] for k in ORDER], *[new_m[k] for k in ORDER],
            *[new_v[k] for k in ORDER])
```

```python
import functools

import numpy as np
import jax
import jax.numpy as jnp
from jax import lax
from jax.experimental import pallas as pl
from jax.experimental.pallas import tpu as pltpu

F32 = jnp.float32
BF16 = jnp.bfloat16

D_MODEL = 1024
GRID_W = 64
HEAD_DIM = 64
ATTN_W = 512
CONV_W = 256
POOL_W = 256
ATTN_HEADS = 8
KV_HEADS = 2
GROUP = ATTN_HEADS // KV_HEADS
KV_W = KV_HEADS * HEAD_DIM
IN_W = ATTN_W + 2 * KV_W + 2 * CONV_W + POOL_W
WINDOW = 128
Q_BLOCK = 128
SPAN = Q_BLOCK + 2 * WINDOW
CONV_KERNEL = 31
POOL_WINDOWS = (2, 4, 8, 16)
POOL_GROUP = 64
ROPE_BASE = 10000.0
D_FF = 2816
EPS = 1e-6
NEG = -1e30
N_DEV = 8
AXES = ("x", "y", "c")

ADAM_LR = 0.001
ADAM_B1 = 0.9
ADAM_B2 = 0.999
ADAM_EPS = 1e-08
ADAM_WD = 0.01
ADAM_STEP = 10

VMEM_LIMIT = 56 * 1024 * 1024
HALO = 16
SEQ_CHUNK = 256


def _params(*sem):
    return pltpu.CompilerParams(dimension_semantics=sem, vmem_limit_bytes=VMEM_LIMIT)


def _tile(dim, target):
    if dim <= target:
        return dim
    t = (target // 128) * 128
    while t >= 128:
        if dim % t == 0:
            return t
        t -= 128
    raise ValueError(f"no tile for {dim}")


MM_VMEM_BUDGET = 40 * 1024 * 1024


def _dot(a, b):
    return lax.dot_general(a.astype(BF16), b.astype(BF16), (((1,), (0,)), ((), ())), preferred_element_type=F32)


def _mm_vmem(tm, tn, tk, whole, a_bytes, b_bytes, o_bytes):
    return 2 * (tm * tk * a_bytes + tk * tn * b_bytes + tm * tn * o_bytes) + (0 if whole else tm * tn * 4)


def _mm(a, b, *, name, out_dtype=F32, tm=1408, tn=512, trans_b=False, out_block=None):
    M, K = a.shape
    N, K2 = b.shape if trans_b else b.shape[::-1]
    assert K == K2, (a.shape, b.shape)
    tm = _tile(M, tm)
    tn = _tile(N, tn)
    sizes = (a.dtype.itemsize, b.dtype.itemsize, jnp.dtype(out_dtype).itemsize)
    tk = next(t for t in range(K, 0, -128) if K % t == 0 and _mm_vmem(tm, tn, t, t == K, *sizes) <= MM_VMEM_BUDGET)
    nk = K // tk

    def body(a_ref, b_ref, o_ref, *scratch):
        part = (_dot_nt if trans_b else _dot)(a_ref[...], b_ref[...])
        if nk == 1:
            o_ref[...] = part.astype(o_ref.dtype)
        else:
            acc_ref, = scratch
            k = pl.program_id(2)

            @pl.when(k == 0)
            def _():
                acc_ref[...] = part

            @pl.when(k > 0)
            def _():
                acc_ref[...] += part

            @pl.when(k == nk - 1)
            def _():
                o_ref[...] = acc_ref[...].astype(o_ref.dtype)

    return pl.pallas_call(
        body, name=name, grid=(M // tm, N // tn, nk),
        in_specs=[pl.BlockSpec((tm, tk), lambda i, j, k: (i, k)),
                  pl.BlockSpec((tn, tk), lambda i, j, k: (j, k)) if trans_b else pl.BlockSpec((tk, tn), lambda i, j, k: (k, j))],
        out_specs=pl.BlockSpec((tm, tn), (lambda i, j, k: (i, j)) if out_block is None else (lambda i, j, k: (i, out_block(j)))),
        out_shape=jax.ShapeDtypeStruct((M, N), out_dtype),
        scratch_shapes=[pltpu.VMEM((tm, tn), F32)] if nk > 1 else [],
        compiler_params=_params("parallel", "parallel", "arbitrary"),
    )(a, b)


FF_TILE = 256


FF_TILES = D_FF // FF_TILE


def _natural_block(j):
    return (j % 2) * FF_TILES + j // 2


def _interleave_ffn(w, *, name):
    R, C = w.shape

    def body(w_ref, o_ref):
        o_ref[...] = w_ref[...]

    return pl.pallas_call(
        body, name=name, grid=(2 * FF_TILES,),
        in_specs=[pl.BlockSpec((R, FF_TILE), lambda j: (0, _natural_block(j)))],
        out_specs=pl.BlockSpec((R, FF_TILE), lambda j: (0, j)),
        out_shape=jax.ShapeDtypeStruct((R, C), w.dtype),
        compiler_params=_params("parallel"),
    )(w)


def _swiglu(gu):
    g, u = gu[:, :FF_TILE], gu[:, FF_TILE:]
    return g * jax.nn.sigmoid(g) * u


EPILOGUE_SPLIT = 1


def _mm_swiglu(a, w_il, *, name, tm=1024, exch=None, split=EPILOGUE_SPLIT):
    M, K = a.shape
    tm = _tile(M, tm)
    rc = tm // split

    def body(a_ref, b_ref, gu_ref, act_ref, act_t_ref):
        b = b_ref[...]
        parts = [_dot(a_ref[r0:r0 + rc, :], b) for r0 in range(0, tm, rc)]
        for r0, gu in zip(range(0, tm, rc), parts):
            gu_ref[r0:r0 + rc, :] = gu.astype(gu_ref.dtype)
            act = _swiglu(gu)
            act_ref[r0:r0 + rc, :] = act.astype(act_ref.dtype)
            act_t_ref[:, r0:r0 + rc] = act.T.astype(act_t_ref.dtype)

    grid = (M // tm, D_FF // FF_TILE)
    body, x_in, x_out, x_shapes, x_sems = _riding(body, exch, 2, 3, grid)
    return pl.pallas_call(
        body, name=name, grid=grid,
        in_specs=[pl.BlockSpec((tm, K), lambda i, j: (i, 0)), pl.BlockSpec((K, 2 * FF_TILE), lambda i, j: (0, j))] + x_in,
        out_specs=[pl.BlockSpec((tm, 2 * FF_TILE), lambda i, j: (i, j)), pl.BlockSpec((tm, FF_TILE), lambda i, j: (i, j)),
                   pl.BlockSpec((FF_TILE, tm), lambda i, j: (j, i))] + x_out,
        out_shape=[jax.ShapeDtypeStruct((M, 2 * D_FF), BF16), jax.ShapeDtypeStruct((M, D_FF), BF16),
                   jax.ShapeDtypeStruct((D_FF, M), BF16)] + x_shapes,
        scratch_shapes=x_sems,
        compiler_params=_params("arbitrary", "arbitrary") if exch else _params("parallel", "parallel"),
    )(a, w_il, *(exch[0] if exch else []))


def _mm_dswiglu(dy, w_out, gu, *, name, tm=1024, split=EPILOGUE_SPLIT):
    M, K = dy.shape
    tm = _tile(M, tm)
    rc = tm // split

    def body(dy_ref, b_ref, gu_ref, o_ref):
        b = b_ref[...]
        parts = [_dot_nt(dy_ref[r0:r0 + rc, :], b) for r0 in range(0, tm, rc)]
        for r0, dact in zip(range(0, tm, rc), parts):
            g = gu_ref[r0:r0 + rc, :FF_TILE].astype(F32)
            u = gu_ref[r0:r0 + rc, FF_TILE:].astype(F32)
            sig = jax.nn.sigmoid(g)
            silu = g * sig
            o_ref[r0:r0 + rc, :FF_TILE] = (dact * u * (sig + silu * (1.0 - sig))).astype(o_ref.dtype)
            o_ref[r0:r0 + rc, FF_TILE:] = (dact * silu).astype(o_ref.dtype)

    return pl.pallas_call(
        body, name=name, grid=(M // tm, D_FF // FF_TILE),
        in_specs=[pl.BlockSpec((tm, K), lambda i, j: (i, 0)), pl.BlockSpec((FF_TILE, K), lambda i, j: (j, 0)),
                  pl.BlockSpec((tm, 2 * FF_TILE), lambda i, j: (i, j))],
        out_specs=pl.BlockSpec((tm, 2 * FF_TILE), lambda i, j: (i, j)),
        out_shape=jax.ShapeDtypeStruct((M, 2 * D_FF), BF16),
        compiler_params=_params("parallel", "parallel"),
    )(dy, w_out, gu)


def _rope_tables(n):
    rows = n // GRID_W
    row = jnp.repeat(jnp.arange(rows), GRID_W).astype(F32)
    col = jnp.tile(jnp.arange(GRID_W), rows).astype(F32)
    half = HEAD_DIM // 2
    inv = ROPE_BASE ** (-jnp.arange(0, half, 2, dtype=F32) / half)
    ar = row[:, None] * inv
    ac = col[:, None] * inv
    ang = jnp.concatenate([ar, ar, ac, ac], axis=-1)
    return jnp.cos(ang), jnp.sin(ang)


def _rot_half(x):
    w = x.shape[-1]
    lane = lax.broadcasted_iota(jnp.int32, x.shape, 1)
    up = pltpu.roll(x, w - 16, 1)
    down = pltpu.roll(x, 16, 1)
    return jnp.where((lane & 16) == 0, -up, down)


def _rope(x, cos, sin):
    return x * cos + _rot_half(x) * sin


def _rope_bwd(d, cos, sin):
    return d * cos - _rot_half(d * sin)


def _dot_nt(a, b):
    return lax.dot_general(a.astype(BF16), b.astype(BF16), (((1,), (1,)), ((), ())), preferred_element_type=F32)


def _softmax_sink(s, sink_rows):
    mx = jnp.maximum(jnp.max(s, axis=1, keepdims=True), sink_rows)
    e = jnp.exp(s - mx)
    es = jnp.exp(sink_rows - mx)
    inv = 1.0 / (jnp.sum(e, axis=1, keepdims=True) + es)
    return e * inv, es * inv


def _attn_operands(q_ref, k_ref, v_ref, kc_ref, vc_ref, cq_ref, sq_ref, ck_ref, sk_ref, i, n, n_ctx, local):
    q = q_ref[...]
    k_all, v_all, bias, s0, ck, sk = kc_ref[...], vc_ref[...], None, None, None, None
    if local:
        start, s0 = _span_start(i, n)
        ck = ck_ref[pl.ds(s0, SPAN), :]
        sk = sk_ref[pl.ds(s0, SPAN), :]
        q = _rope(q, cq_ref[...], sq_ref[...])
        k_all = jnp.concatenate([k_all, _rope(k_ref[pl.ds(s0, SPAN), :], ck, sk)], axis=0)
        v_all = jnp.concatenate([v_all, v_ref[pl.ds(s0, SPAN), :]], axis=0)
        bias = _window_bias(start, s0, n_ctx)
    q = (q * (HEAD_DIM ** -0.5)).astype(BF16)
    return q, k_all.astype(BF16), v_all.astype(BF16), bias, s0, ck, sk


def _stack_heads(x, kh):
    return jnp.concatenate([x[:, (GROUP * kh + g) * HEAD_DIM:(GROUP * kh + g + 1) * HEAD_DIM] for g in range(GROUP)], axis=0)


def _sink_rows(sink, kh):
    return jnp.concatenate([jnp.broadcast_to(sink[:, GROUP * kh + g:GROUP * kh + g + 1], (Q_BLOCK, 1)) for g in range(GROUP)], axis=0)


def _window_bias(start, s0, n_ctx):
    r = lax.broadcasted_iota(jnp.int32, (Q_BLOCK, n_ctx + SPAN), 0)
    c = lax.broadcasted_iota(jnp.int32, (Q_BLOCK, n_ctx + SPAN), 1)
    ok = (c < n_ctx) | (jnp.abs(start - s0 + r - (c - n_ctx)) <= WINDOW)
    return jnp.concatenate([jnp.where(ok, 0.0, NEG).astype(F32)] * GROUP, axis=0)


def _span_start(i, n):
    start = i * Q_BLOCK
    s0 = jnp.clip(start - WINDOW, 0, n - SPAN)
    return start, pl.multiple_of(s0, Q_BLOCK)


def _riding(body, exch, n_in, n_out, grid):
    if exch is None:
        return body, [], [], [], []
    arrs, scatter = exch
    k = len(arrs)

    def wrapped(*refs):
        ins, xin = refs[:n_in], refs[n_in:n_in + k]
        outs, xout = refs[n_in + k:n_in + k + n_out], refs[n_in + k + n_out:n_in + 2 * k + n_out]
        sems = refs[n_in + 2 * k + n_out:]
        b, i = pl.program_id(0), pl.program_id(1)

        @pl.when((b == 0) & (i == 0))
        def _():
            _exch_start(xin, xout, sems, scatter)

        body(*ins, *outs)

        @pl.when((b == grid[0] - 1) & (i == grid[1] - 1))
        def _():
            _exch_wait(xin, xout, sems, scatter)

    any_spec = pl.BlockSpec(memory_space=pl.ANY)
    return wrapped, [any_spec] * k, [any_spec] * k, _exch_out_shapes(arrs, scatter), _exch_sems(k)


def _attn_fwd(u, kvc, sink, cos, sin, *, local, name, exch=None):
    B, n, _ = u.shape
    n_ctx = kvc.shape[1]
    nb = n // Q_BLOCK
    assert (not local) or n >= SPAN

    def body(q_ref, k_ref, v_ref, kc_ref, vc_ref, sink_ref, cq_ref, sq_ref, ck_ref, sk_ref, o_ref):
        q, k_all, v_all, bias, _, _, _ = _attn_operands(q_ref, k_ref, v_ref, kc_ref, vc_ref, cq_ref, sq_ref, ck_ref, sk_ref,
                                                        pl.program_id(1), n, n_ctx, local)
        sink_v = sink_ref[...]
        sl = lambda kh: slice(kh * HEAD_DIM, (kh + 1) * HEAD_DIM)
        ss = [_dot_nt(_stack_heads(q, kh), k_all[:, sl(kh)]) for kh in range(KV_HEADS)]
        ps = [_softmax_sink(s if bias is None else s + bias, _sink_rows(sink_v, kh))[0].astype(BF16) for kh, s in enumerate(ss)]
        for kh, p in enumerate(ps):
            o = _dot(p, v_all[:, sl(kh)])
            for g in range(GROUP):
                h = GROUP * kh + g
                o_ref[:, h * HEAD_DIM:(h + 1) * HEAD_DIM] = o[g * Q_BLOCK:(g + 1) * Q_BLOCK, :].astype(o_ref.dtype)

    seq = lambda blk: pl.BlockSpec((None, n, KV_W), lambda b, i: (b, 0, blk))
    ctxs = lambda blk: pl.BlockSpec((None, n_ctx, KV_W), lambda b, i: (b, 0, blk))
    full = lambda a: pl.BlockSpec(a.shape, lambda b, i: (0,) * a.ndim)
    cos_q, sin_q = jnp.tile(cos, (1, ATTN_HEADS)), jnp.tile(sin, (1, ATTN_HEADS))
    cos_k, sin_k = jnp.tile(cos, (1, KV_HEADS)), jnp.tile(sin, (1, KV_HEADS))
    body, x_in, x_out, x_shapes, x_sems = _riding(body, exch, 10, 1, (B, nb))
    return pl.pallas_call(
        body, name=name, grid=(B, nb),
        in_specs=[pl.BlockSpec((None, Q_BLOCK, ATTN_W), lambda b, i: (b, i, 0)),
                  seq(ATTN_W // KV_W), seq(ATTN_W // KV_W + 1), ctxs(0), ctxs(1), full(sink),
                  pl.BlockSpec((Q_BLOCK, ATTN_W), lambda b, i: (i, 0)), pl.BlockSpec((Q_BLOCK, ATTN_W), lambda b, i: (i, 0)),
                  full(cos_k), full(sin_k)] + x_in,
        out_specs=[pl.BlockSpec((None, Q_BLOCK, ATTN_W), lambda b, i: (b, i, 0))] + x_out,
        out_shape=[jax.ShapeDtypeStruct((B, n, ATTN_W), BF16)] + x_shapes,
        scratch_shapes=x_sems,
        compiler_params=_params("arbitrary", "arbitrary"),
    )(u, u, u, kvc, kvc, sink, cos_q, sin_q, cos_k, sin_k, *(exch[0] if exch else []))


def _attn_bwd(u, kvc, sink, cos, sin, do_src, do_blk, *, local, name, exch=None):
    B, n, _ = u.shape
    n_ctx = kvc.shape[1]
    nb = n // Q_BLOCK

    def body(q_ref, k_ref, v_ref, kc_ref, vc_ref, sink_ref, cq_ref, sq_ref, ck_ref, sk_ref, do_ref,
             dq_ref, dk_ref, dv_ref, dkc_ref, dvc_ref, dsink_ref):
        b = pl.program_id(0)
        i = pl.program_id(1)

        @pl.when(i == 0)
        def _():
            dk_ref[...] = jnp.zeros_like(dk_ref)
            dv_ref[...] = jnp.zeros_like(dv_ref)
            dkc_ref[...] = jnp.zeros_like(dkc_ref)
            dvc_ref[...] = jnp.zeros_like(dvc_ref)

        @pl.when((i == 0) & (b == 0))
        def _():
            dsink_ref[...] = jnp.zeros_like(dsink_ref)

        q, k_all, v_all, bias, s0, ck, sk = _attn_operands(q_ref, k_ref, v_ref, kc_ref, vc_ref, cq_ref, sq_ref, ck_ref, sk_ref,
                                                           i, n, n_ctx, local)
        do = do_ref[...].astype(BF16)
        sink_v = sink_ref[...]
        sl = lambda kh: slice(kh * HEAD_DIM, (kh + 1) * HEAD_DIM)
        heads = range(KV_HEADS)
        q_st = [_stack_heads(q, kh) for kh in heads]
        do_st = [_stack_heads(do, kh) for kh in heads]
        ss = [_dot_nt(q_st[kh], k_all[:, sl(kh)]) for kh in heads]
        dps = [_dot_nt(do_st[kh], v_all[:, sl(kh)]) for kh in heads]
        p_bf, ds_bf = [], []
        dsink = jnp.zeros((1, ATTN_HEADS), F32)
        lane8 = lax.broadcasted_iota(jnp.int32, (1, ATTN_HEADS), 1)
        for kh in heads:
            p, p_sink = _softmax_sink(ss[kh] if bias is None else ss[kh] + bias, _sink_rows(sink_v, kh))
            delta = jnp.sum(p * dps[kh], axis=1, keepdims=True)
            ds = p * (dps[kh] - delta)
            dsr = -(p_sink * delta)
            for g in range(GROUP):
                dsink = dsink + jnp.where(lane8 == GROUP * kh + g, jnp.sum(dsr[g * Q_BLOCK:(g + 1) * Q_BLOCK, :]), 0.0)
            p_bf.append(p.astype(BF16))
            ds_bf.append(ds.astype(BF16))
        over_rows = (((0,), (0,)), ((), ()))
        dks, dvs = [], []
        for kh in heads:
            dq_st = _dot(ds_bf[kh], k_all[:, sl(kh)]) * (HEAD_DIM ** -0.5)
            for g in range(GROUP):
                h = GROUP * kh + g
                dq_ref[:, h * HEAD_DIM:(h + 1) * HEAD_DIM] = dq_st[g * Q_BLOCK:(g + 1) * Q_BLOCK, :]
            dvs.append(lax.dot_general(p_bf[kh], do_st[kh], over_rows, preferred_element_type=F32))
            dks.append(lax.dot_general(ds_bf[kh], q_st[kh], over_rows, preferred_element_type=F32))
        dk_cat = jnp.concatenate(dks, axis=1)
        dv_cat = jnp.concatenate(dvs, axis=1)
        dsink_ref[...] += dsink
        dkc_ref[...] += dk_cat[:n_ctx, :]
        dvc_ref[...] += dv_cat[:n_ctx, :]
        if local:
            dq_ref[...] = _rope_bwd(dq_ref[...], cq_ref[...], sq_ref[...])
            dk_ref[pl.ds(s0, SPAN), :] += _rope_bwd(dk_cat[n_ctx:, :], ck, sk)
            dv_ref[pl.ds(s0, SPAN), :] += dv_cat[n_ctx:, :]

    seq = lambda blk: pl.BlockSpec((None, n, KV_W), lambda b, i: (b, 0, blk))
    ctxs = lambda blk: pl.BlockSpec((None, n_ctx, KV_W), lambda b, i: (b, 0, blk))
    full = lambda a: pl.BlockSpec(a.shape, lambda b, i: (0,) * a.ndim)
    qblk = lambda blk: pl.BlockSpec((None, Q_BLOCK, ATTN_W), lambda b, i: (b, i, blk))
    cos_q, sin_q = jnp.tile(cos, (1, ATTN_HEADS)), jnp.tile(sin, (1, ATTN_HEADS))
    cos_k, sin_k = jnp.tile(cos, (1, KV_HEADS)), jnp.tile(sin, (1, KV_HEADS))
    acc = lambda rows: pl.BlockSpec((None, rows, KV_W), lambda b, i: (b, 0, 0))
    body, x_in, x_out, x_shapes, x_sems = _riding(body, exch, 11, 6, (B, nb))
    return pl.pallas_call(
        body, name=name, grid=(B, nb),
        in_specs=[qblk(0), seq(ATTN_W // KV_W), seq(ATTN_W // KV_W + 1), ctxs(0), ctxs(1), full(sink),
                  pl.BlockSpec((Q_BLOCK, ATTN_W), lambda b, i: (i, 0)), pl.BlockSpec((Q_BLOCK, ATTN_W), lambda b, i: (i, 0)),
                  full(cos_k), full(sin_k), qblk(do_blk)] + x_in,
        out_specs=[qblk(0), acc(n), acc(n), acc(n_ctx), acc(n_ctx), pl.BlockSpec((1, ATTN_HEADS), lambda b, i: (0, 0))] + x_out,
        out_shape=[jax.ShapeDtypeStruct((B, n, ATTN_W), F32), jax.ShapeDtypeStruct((B, n, KV_W), F32),
                   jax.ShapeDtypeStruct((B, n, KV_W), F32), jax.ShapeDtypeStruct((B, n_ctx, KV_W), F32),
                   jax.ShapeDtypeStruct((B, n_ctx, KV_W), F32), jax.ShapeDtypeStruct((1, ATTN_HEADS), F32)] + x_shapes,
        scratch_shapes=x_sems,
        compiler_params=_params("arbitrary", "arbitrary"),
    )(u, u, u, kvc, kvc, sink, cos_q, sin_q, cos_k, sin_k, do_src, *(exch[0] if exch else []))


def _conv_chunk(s, n, a_ext, g_ext, dw, dw_b, ln_g, ln_b):
    del s, n
    r = a_ext.shape[0] - 2 * HALO
    h = a_ext * jax.nn.sigmoid(g_ext)
    acc = jnp.broadcast_to(dw_b, (r, CONV_W))
    first = HALO - CONV_KERNEL // 2
    span = r + 8 * ((first + CONV_KERNEL - 1) // 8)
    shifted = [h[b:b + span, :] for b in range(8)]
    for k in range(CONV_KERNEL):
        o = first + k
        acc = acc + shifted[o % 8][o - o % 8:o - o % 8 + r, :] * dw[k:k + 1, :]
    mu = jnp.mean(acc, axis=-1, keepdims=True)
    var = jnp.mean(jnp.square(acc - mu), axis=-1, keepdims=True)
    hn = (acc - mu) * lax.rsqrt(var + EPS) * ln_g + ln_b
    return hn * jax.nn.sigmoid(hn)


def _pool_chunk(s, n, p_ext, w_bd, scale):
    r = p_ext.shape[0] - 2 * HALO
    lane = lax.broadcasted_iota(jnp.int32, (1, POOL_W), 1)
    win = jnp.left_shift(2, lane // POOL_GROUP)
    half = win // 2
    acc = jnp.zeros((r, POOL_W), F32)
    for d in range(-(POOL_WINDOWS[-1] // 2), POOL_WINDOWS[-1] - POOL_WINDOWS[-1] // 2):
        inside = (d >= -half) & (d <= win - 1 - half)
        acc = acc + jnp.where(inside, p_ext[HALO + d:HALO + d + r, :], 0.0)
    t = s + lax.broadcasted_iota(jnp.int32, (r, 1), 0)
    lo = jnp.maximum(t - half, 0)
    hi = jnp.minimum(t + win - 1 - half, n - 1)
    y = acc / (hi - lo + 1).astype(F32) - p_ext[HALO:HALO + r, :]
    out = lax.dot_general(y.astype(BF16), w_bd.astype(BF16), (((1,), (0,)), ((), ())), preferred_element_type=F32)
    return out * scale


def _seq_specs(rows, params):
    specs = [pl.BlockSpec((None, a.shape[1], w), functools.partial(lambda b, blk: (b, 0, blk), blk=blk)) for a, w, blk in rows]
    specs += [pl.BlockSpec(p.shape, functools.partial(lambda b, nd: (0,) * nd, nd=p.ndim)) for p in params]
    return specs


def _fill_padded(pad_ref, row_ref, n):
    w = pad_ref.shape[1]
    pad_ref[pl.ds(0, HALO), :] = jnp.zeros((HALO, w), F32)
    pad_ref[pl.ds(HALO + n, HALO), :] = jnp.zeros((HALO, w), F32)
    pad_ref[pl.ds(HALO, n), :] = row_ref[...]


def _seq_fwd(fn, rows, params, out_w, *, name):
    B, n = rows[0][0].shape[:2]
    r = min(SEQ_CHUNK, n)
    nr, npar = len(rows), len(params)

    def body(*refs):
        row_refs, par_refs, o_ref, pads = refs[:nr], refs[nr:nr + npar], refs[nr + npar], refs[nr + npar + 1:]
        for rr, p in zip(row_refs, pads):
            _fill_padded(p, rr, n)
        pars = [p[...] for p in par_refs]

        def chunk(ci, carry):
            s = pl.multiple_of(ci * r, r)
            ext = [p[pl.ds(s, r + 2 * HALO), :] for p in pads]
            o_ref[pl.ds(s, r), :] = fn(s, n, *ext, *pars).astype(o_ref.dtype)
            return carry

        lax.fori_loop(0, n // r, chunk, 0)

    return pl.pallas_call(
        body, name=name, grid=(B,),
        in_specs=_seq_specs(rows, params),
        out_specs=pl.BlockSpec((None, n, out_w), lambda b: (b, 0, 0)),
        out_shape=jax.ShapeDtypeStruct((B, n, out_w), BF16),
        scratch_shapes=[pltpu.VMEM((n + 2 * HALO, w), F32) for _, w, _ in rows],
        compiler_params=_params("parallel"),
    )(*[a for a, _, _ in rows], *params)


def _seq_bwd(fn, rows, params, dout, *, name):
    B, n = rows[0][0].shape[:2]
    r = min(SEQ_CHUNK, n)
    nr, npar = len(rows), len(params)

    def body(*refs):
        row_refs, par_refs, do_ref = refs[:nr], refs[nr:nr + npar], refs[nr + npar]
        outs = refs[nr + npar + 1:]
        drow_refs, dpar_refs = outs[:nr], outs[nr:nr + npar]
        scratch = outs[nr + npar:]
        pads, dpads = scratch[:nr], scratch[nr:]
        for rr, p, dp in zip(row_refs, pads, dpads):
            _fill_padded(p, rr, n)
            dp[...] = jnp.zeros_like(dp)

        @pl.when(pl.program_id(0) == 0)
        def _():
            for d in dpar_refs:
                d[...] = jnp.zeros_like(d)

        pars = [p[...] for p in par_refs]

        def chunk(ci, carry):
            s = pl.multiple_of(ci * r, r)
            ext = [p[pl.ds(s, r + 2 * HALO), :] for p in pads]
            _, vjp = jax.vjp(functools.partial(fn, s, n), *ext, *pars)
            grads = vjp(do_ref[pl.ds(s, r), :])
            for dp, g in zip(dpads, grads[:nr]):
                dp[pl.ds(s, r + 2 * HALO), :] += g
            for d, g in zip(dpar_refs, grads[nr:]):
                d[...] += g
            return carry

        lax.fori_loop(0, n // r, chunk, 0)
        for d, dp in zip(drow_refs, dpads):
            d[...] = dp[pl.ds(HALO, n), :].astype(d.dtype)

    da, dw_, dblk = dout
    return pl.pallas_call(
        body, name=name, grid=(B,),
        in_specs=_seq_specs(rows, params) + [pl.BlockSpec((None, n, dw_), lambda b: (b, 0, dblk))],
        out_specs=[pl.BlockSpec((None, n, w), lambda b: (b, 0, 0)) for _, w, _ in rows]
        + [pl.BlockSpec(p.shape, functools.partial(lambda b, nd: (0,) * nd, nd=p.ndim)) for p in params],
        out_shape=[jax.ShapeDtypeStruct((B, n, w), BF16) for _, w, _ in rows]
        + [jax.ShapeDtypeStruct(p.shape, F32) for p in params],
        scratch_shapes=[pltpu.VMEM((n + 2 * HALO, w), F32) for _, w, _ in rows] * 2,
        compiler_params=_params("arbitrary"),
    )(*[a for a, _, _ in rows], *params, da)


_CONV_A_BLK = (ATTN_W + 2 * KV_W) // CONV_W
_CONV_G_BLK = _CONV_A_BLK + 1
_POOL_BLK = _CONV_A_BLK + 2


def _mixer_fwd(tag, u, kvc, margs, local, exch=None):
    sink, dw, dw_b, ln_g, ln_b, w_bd, scale = margs
    cos, sin = _rope_tables(max(u.shape[1], GRID_W))
    attn, *got = _attn_fwd(u, kvc, sink, cos, sin, local=local, name=f"{tag}_attn_fwd", exch=exch)
    conv = _seq_fwd(_conv_chunk, [(u, CONV_W, _CONV_A_BLK), (u, CONV_W, _CONV_G_BLK)], [dw, dw_b, ln_g, ln_b], CONV_W,
                    name=f"{tag}_conv_fwd")
    pool = _seq_fwd(_pool_chunk, [(u, POOL_W, _POOL_BLK)], [w_bd, scale], POOL_W, name=f"{tag}_pool_fwd")
    return jnp.concatenate([attn, conv, pool], axis=-1), got


def _mixer_bwd(tag, u, kvc, margs, dmix, local, exch=None):
    sink, dw, dw_b, ln_g, ln_b, w_bd, scale = margs
    cos, sin = _rope_tables(max(u.shape[1], GRID_W))
    dq, dk, dv, dkc, dvc, dsink, *got = _attn_bwd(u, kvc, sink, cos, sin, dmix, 0, local=local, name=f"{tag}_attn_bwd",
                                                  exch=exch)
    da, dg, ddw, ddw_b, dln_g, dln_b = _seq_bwd(
        _conv_chunk, [(u, CONV_W, _CONV_A_BLK), (u, CONV_W, _CONV_G_BLK)], [dw, dw_b, ln_g, ln_b],
        (dmix, CONV_W, ATTN_W // CONV_W), name=f"{tag}_conv_bwd")
    dpu, dw_bd, dscale = _seq_bwd(_pool_chunk, [(u, POOL_W, _POOL_BLK)], [w_bd, scale],
                                  (dmix, POOL_W, (ATTN_W + CONV_W) // POOL_W), name=f"{tag}_pool_bwd")
    return (dq, dk, dv, da, dg, dpu), (dkc, dvc), (dsink, ddw, ddw_b, dln_g, dln_b, dw_bd, dscale), got


def _row_specs(arrs, kinds, tr):
    specs = []
    for a, kind in zip(arrs, kinds):
        if kind == "row":
            specs.append(pl.BlockSpec((None, tr, a.shape[2]), lambda b, j: (b, j, 0)))
        elif kind == "batch":
            specs.append(pl.BlockSpec((None, 1, a.shape[2]), lambda b, j: (b, 0, 0)))
        else:
            specs.append(pl.BlockSpec(a.shape, functools.partial(lambda b, j, nd: (0,) * nd, nd=a.ndim)))
    return specs


def _rowwise_fwd(fn, ins, kinds, outs, tr, *, name, transposed=None):
    B, n = ins[0].shape[:2]
    ni, no = len(ins), len(outs)
    nj = n // tr

    def body(*refs):
        res = fn(*[r[...] for r in refs[:ni]])
        for o, v in zip(refs[ni:ni + no], res):
            o[...] = v.astype(o.dtype)
        if transposed is not None:
            refs[ni + no][...] = res[transposed].T.astype(refs[ni + no].dtype)

    out_specs = [pl.BlockSpec((None, tr, w), lambda b, j: (b, j, 0)) for w, _ in outs]
    out_shape = [jax.ShapeDtypeStruct((B, n, w), dt) for w, dt in outs]
    if transposed is not None:
        w, dt = outs[transposed]
        out_specs.append(pl.BlockSpec((w, tr), lambda b, j: (0, b * nj + j)))
        out_shape.append(jax.ShapeDtypeStruct((w, B * n), dt))
    return pl.pallas_call(
        body, name=name, grid=(B, nj),
        in_specs=_row_specs(ins, kinds, tr), out_specs=out_specs, out_shape=out_shape,
        compiler_params=_params("parallel", "parallel"),
    )(*ins)


def _rowwise_bwd(fn, ins, kinds, gdtypes, cts, tr, *, name):
    B, n = ins[0].shape[:2]
    ni, nc = len(ins), len(cts)
    idx = list(range(ni))

    def body(*refs):
        in_refs, ct_refs, out_refs = refs[:ni], refs[ni:ni + nc], refs[ni + nc:]
        b, j = pl.program_id(0), pl.program_id(1)
        _, vjp = jax.vjp(fn, *[r[...].astype(F32) for r in in_refs])
        grads = vjp(tuple(c[...].astype(F32) for c in ct_refs))
        for o, i in zip(out_refs, idx):
            g = grads[i]
            if kinds[i] == "row":
                o[...] = g.astype(o.dtype)
            else:
                first = (j == 0) if kinds[i] == "batch" else ((j == 0) & (b == 0))

                @pl.when(first)
                def _(o=o, g=g):
                    o[...] = g

                @pl.when(jnp.logical_not(first))
                def _(o=o, g=g):
                    o[...] += g

    specs = _row_specs(ins, kinds, tr)
    return pl.pallas_call(
        body, name=name, grid=(B, n // tr),
        in_specs=specs + [pl.BlockSpec((None, tr, c.shape[2]), lambda b, j: (b, j, 0)) for c in cts],
        out_specs=[specs[i] for i in idx],
        out_shape=[jax.ShapeDtypeStruct(ins[i].shape, gdtypes[i]) for i in idx],
        compiler_params=_params("arbitrary", "arbitrary"),
    )(*ins, *cts)


ROW_TILE = 256


def _rms_mod(x, g, sc, sh):
    y = x * lax.rsqrt(jnp.mean(x * x, axis=-1, keepdims=True) + EPS)
    return (y * g) * (1.0 + sc) + sh


def _norm_tile(x, g, sc, sh):
    return x, _rms_mod(x, g, sc, sh)


def _res_norm_tile(xb, y, gate, g, sc, sh):
    x = xb + gate * y
    return x, _rms_mod(x, g, sc, sh)


_NORM_KINDS = ("row", "glob", "batch", "batch")
_RES_NORM_KINDS = ("row", "row", "batch", "glob", "batch", "batch")


def _norm_fwd(tag, st, g, sc, sh):
    xb, y, gate = st
    tr = min(ROW_TILE, xb.shape[1])
    d = xb.shape[2]
    if y is None:
        h, h_t = _rowwise_fwd(lambda *a: (_rms_mod(*a),), [xb, g, sc, sh], _NORM_KINDS, [(d, BF16)], tr,
                              name=f"{tag}_fwd", transposed=0)
        return xb, h, h_t
    return _rowwise_fwd(_res_norm_tile, [xb, y, gate, g, sc, sh], _RES_NORM_KINDS, [(d, F32), (d, BF16)], tr,
                        name=f"{tag}_fwd", transposed=1)


def _norm_bwd(tag, st, g, sc, sh, dx, dh):
    xb, y, gate = st
    tr = min(ROW_TILE, xb.shape[1])
    if y is None:
        dxb, dg, dsc, dsh = _rowwise_bwd(_norm_tile, [xb, g, sc, sh], _NORM_KINDS, [F32] * 4, [dx, dh], tr, name=f"{tag}_bwd")
        return dxb, None, None, dg, dsc, dsh
    return tuple(_rowwise_bwd(_res_norm_tile, [xb, y, gate, g, sc, sh], _RES_NORM_KINDS, [F32, BF16, F32, F32, F32, F32],
                              [dx, dh], tr, name=f"{tag}_bwd"))


def _loss_head(st, final_g, target, *, name):
    xb, y, gate = st
    B, n, d = xb.shape
    tr = min(ROW_TILE, n)

    def tile_loss(xv, yv, gt, g, t):
        x = xv + gt * yv
        out = x * lax.rsqrt(jnp.mean(x * x, axis=-1, keepdims=True) + EPS) * g
        return 0.5 * jnp.sum(jnp.mean(jnp.square(out - t), axis=-1))

    def body(x_ref, y_ref, gate_ref, g_ref, t_ref, loss_ref, dx_ref, dy_ref, dgate_ref, dg_ref):
        b, j = pl.program_id(0), pl.program_id(1)
        val, (dx, dy, dgate, dg) = jax.value_and_grad(tile_loss, argnums=(0, 1, 2, 3))(
            x_ref[...], y_ref[...], gate_ref[...], g_ref[...], t_ref[...])
        dx_ref[...] = dx
        dy_ref[...] = dy.astype(dy_ref.dtype)

        @pl.when(j == 0)
        def _():
            loss_ref[...] = jnp.zeros_like(loss_ref)
            dgate_ref[...] = jnp.zeros_like(dgate_ref)

        @pl.when((j == 0) & (b == 0))
        def _():
            dg_ref[...] = jnp.zeros_like(dg_ref)

        loss_ref[...] += jnp.full(loss_ref.shape, val, F32)
        dgate_ref[...] += dgate
        dg_ref[...] += dg

    row = pl.BlockSpec((None, tr, d), lambda b, j: (b, j, 0))
    per_sample = pl.BlockSpec((None, 1, d), lambda b, j: (b, 0, 0))
    whole = pl.BlockSpec((1, d), lambda b, j: (0, 0))
    return pl.pallas_call(
        body, name=name, grid=(B, n // tr),
        in_specs=[row, row, per_sample, whole, row],
        out_specs=[pl.BlockSpec((None, 1, 128), lambda b, j: (b, 0, 0)), row, row, per_sample, whole],
        out_shape=[jax.ShapeDtypeStruct((B, 1, 128), F32), jax.ShapeDtypeStruct((B, n, d), F32),
                   jax.ShapeDtypeStruct((B, n, d), BF16), jax.ShapeDtypeStruct((B, 1, d), F32), jax.ShapeDtypeStruct((1, d), F32)],
        compiler_params=_params("arbitrary", "arbitrary"),
    )(xb, y, gate, final_g, target)


def _exchange(arrs, *, scatter, name):
    k = len(arrs)

    def body(*refs):
        ins, outs, sems = refs[:k], refs[k:2 * k], refs[2 * k:]
        _exch_start(ins, outs, sems, scatter)
        _exch_wait(ins, outs, sems, scatter)

    any_spec = pl.BlockSpec(memory_space=pl.ANY)
    return pl.pallas_call(
        body, name=name,
        in_specs=[any_spec] * k, out_specs=[any_spec] * k,
        out_shape=_exch_out_shapes(arrs, scatter), scratch_shapes=_exch_sems(k),
        compiler_params=pltpu.CompilerParams(has_side_effects=True),
    )(*arrs)


def _exch_out_shapes(arrs, scatter):
    return [jax.ShapeDtypeStruct(a.shape if scatter else (N_DEV,) + a.shape, a.dtype) for a in arrs]


def _exch_sems(k):
    return [pltpu.SemaphoreType.DMA((k * (N_DEV - 1),)), pltpu.SemaphoreType.DMA((k * (N_DEV - 1),)),
            pltpu.SemaphoreType.DMA((k,))]


def _exch_copies(ins, outs, sems, scatter):
    send_sems, recv_sems, local_sems = sems
    x, y, c = lax.axis_index("x"), lax.axis_index("y"), lax.axis_index("c")
    me = 4 * x + 2 * y + c
    owns, sends, recvs = [], [], []
    for a in range(len(ins)):
        owns.append(pltpu.make_async_copy(ins[a].at[me] if scatter else ins[a], outs[a].at[me], local_sems.at[a]))
        for r in range(1, N_DEV):
            fx, fy, fc = (r >> 2) & 1, (r >> 1) & 1, r & 1
            px, py, pc = (x + fx) % 2, (y + fy) % 2, (c + fc) % 2
            peer = 4 * px + 2 * py + pc
            s = a * (N_DEV - 1) + r - 1
            mk = functools.partial(pltpu.make_async_remote_copy, src_ref=ins[a].at[peer] if scatter else ins[a],
                                   send_sem=send_sems.at[s], recv_sem=recv_sems.at[s],
                                   device_id=(px, py, pc), device_id_type=pl.DeviceIdType.MESH)
            sends.append(mk(dst_ref=outs[a].at[me]))
            recvs.append(mk(dst_ref=outs[a].at[peer]))
    return owns, sends, recvs


def _exch_start(ins, outs, sems, scatter):
    owns, sends, _ = _exch_copies(ins, outs, sems, scatter)
    for cp in owns + sends:
        cp.start()


def _exch_wait(ins, outs, sems, scatter):
    owns, sends, recvs = _exch_copies(ins, outs, sems, scatter)
    for rc in recvs:
        rc.wait_recv()
    for cp in sends:
        cp.wait_send()
    for own in owns:
        own.wait()


MOD_ROWS = 48


def _mod_tile(cc, w, b):
    s = cc * jax.nn.sigmoid(cc)
    return lax.dot_general(s.astype(BF16), w.astype(BF16), (((1,), (0,)), ((), ())), preferred_element_type=F32) + b


def _mod_fwd(cc, w_mod, b_shard, *, name):
    L, d, wcols = w_mod.shape

    def body(cc_ref, w_ref, b_ref, o_ref):
        o_ref[...] = _mod_tile(cc_ref[...], w_ref[...], b_ref[...])

    return pl.pallas_call(
        body, name=name, grid=(L,),
        in_specs=[pl.BlockSpec((MOD_ROWS, d), lambda l: (0, 0)), pl.BlockSpec((None, d, wcols), lambda l: (l, 0, 0)),
                  pl.BlockSpec((None, 1, wcols), lambda l: (l, 0, 0))],
        out_specs=pl.BlockSpec((None, MOD_ROWS, wcols), lambda l: (l, 0, 0)),
        out_shape=jax.ShapeDtypeStruct((L, MOD_ROWS, wcols), F32),
        compiler_params=_params("parallel"),
    )(cc, w_mod, b_shard)


def _mod_bwd(cc, w_mod, b_shard, dm, *, name):
    L, d, wcols = w_mod.shape

    def body(cc_ref, w_ref, b_ref, dm_ref, dcc_ref, dw_ref):
        _, vjp = jax.vjp(_mod_tile, cc_ref[...], w_ref[...], b_ref[...])
        dcc, dw, _ = vjp(dm_ref[...])
        dw_ref[...] = dw

        @pl.when(pl.program_id(0) == 0)
        def _():
            dcc_ref[...] = dcc

        @pl.when(pl.program_id(0) > 0)
        def _():
            dcc_ref[...] += dcc

    return pl.pallas_call(
        body, name=name, grid=(L,),
        in_specs=[pl.BlockSpec((MOD_ROWS, d), lambda l: (0, 0)), pl.BlockSpec((None, d, wcols), lambda l: (l, 0, 0)),
                  pl.BlockSpec((None, 1, wcols), lambda l: (l, 0, 0)), pl.BlockSpec((None, MOD_ROWS, wcols), lambda l: (l, 0, 0))],
        out_specs=[pl.BlockSpec((MOD_ROWS, d), lambda l: (0, 0)), pl.BlockSpec((None, d, wcols), lambda l: (l, 0, 0))],
        out_shape=[jax.ShapeDtypeStruct((MOD_ROWS, d), F32), jax.ShapeDtypeStruct((L, d, wcols), F32)],
        compiler_params=_params("arbitrary"),
    )(cc, w_mod, b_shard, dm)


def _sum_leading(a, *, name):
    K, R, C = a.shape
    tr = _tile8(R, 256)

    def body(a_ref, o_ref):
        acc = a_ref[0].astype(F32)
        for i in range(1, K):
            acc = acc + a_ref[i].astype(F32)
        o_ref[...] = acc

    return pl.pallas_call(
        body, name=name, grid=(R // tr,),
        in_specs=[pl.BlockSpec((K, tr, C), lambda i: (0, i, 0))],
        out_specs=pl.BlockSpec((tr, C), lambda i: (i, 0)),
        out_shape=jax.ShapeDtypeStruct((R, C), F32),
        compiler_params=_params("parallel"),
    )(a)


def _tile8(dim, target):
    if dim <= target:
        return dim
    t = (target // 8) * 8
    while t >= 8:
        if dim % t == 0:
            return t
        t -= 8
    raise ValueError(f"no row tile for {dim}")


def _adamw_math(g, w, m, v):
    m = ADAM_B1 * m + (1.0 - ADAM_B1) * g
    v = ADAM_B2 * v + (1.0 - ADAM_B2) * jnp.square(g)
    m_hat = m / (1.0 - ADAM_B1 ** ADAM_STEP)
    v_hat = v / (1.0 - ADAM_B2 ** ADAM_STEP)
    delta = -ADAM_LR * (m_hat / (jnp.sqrt(v_hat) + ADAM_EPS) + ADAM_WD * w)
    return delta, m, v


def _adamw(g, w, m, v, *, name):
    L, R, C = w.shape
    parts = isinstance(g, (list, tuple))
    gs = list(g) if parts else [g]
    ng = len(gs)
    tr = _tile8(R, 256)

    def body(*refs):
        g_refs = refs[:ng]
        w_ref, m_ref, v_ref, go_ref, d_ref, mo_ref, vo_ref = refs[ng:]
        if parts:
            layer = pl.program_id(0)
            gv = None
            for li, g_ref in enumerate(g_refs):
                acc = g_ref[0].astype(F32)
                for i in range(1, N_DEV):
                    acc = acc + g_ref[i].astype(F32)
                gv = acc if gv is None else jnp.where(layer == li, acc, gv)
        else:
            gv = g_refs[0][...]
        go_ref[...] = gv
        d_ref[...], mo_ref[...], vo_ref[...] = _adamw_math(gv, w_ref[...], m_ref[...], v_ref[...])

    tile = pl.BlockSpec((None, tr, C), lambda l, i: (l, i, 0))
    g_specs = [pl.BlockSpec((N_DEV, tr, C), lambda l, i: (0, i, 0))] * ng if parts else [tile]
    return pl.pallas_call(
        body, name=name, grid=(L, R // tr),
        in_specs=g_specs + [tile, tile, tile], out_specs=[tile] * 4,
        out_shape=[jax.ShapeDtypeStruct((L, R, C), F32)] * 4,
        compiler_params=_params("parallel", "parallel"),
    )(*gs, w, m, v)


def _adamw_small(gs, ws, ms, vs, *, name):
    k = len(ws)

    def body(*refs):
        g_refs, w_refs, m_refs, v_refs = refs[:k], refs[k:2 * k], refs[2 * k:3 * k], refs[3 * k:4 * k]
        d_refs, mo_refs, vo_refs = refs[4 * k:5 * k], refs[5 * k:6 * k], refs[6 * k:]
        for i in range(k):
            d_refs[i][...], mo_refs[i][...], vo_refs[i][...] = _adamw_math(g_refs[i][...], w_refs[i][...], m_refs[i][...],
                                                                         v_refs[i][...])

    shapes = [jax.ShapeDtypeStruct(a.shape, F32) for a in ws]
    out = pl.pallas_call(body, name=name, out_shape=shapes * 3, compiler_params=pltpu.CompilerParams(vmem_limit_bytes=VMEM_LIMIT))(
        *gs, *ws, *ms, *vs)
    return out[:k], out[k:2 * k], out[2 * k:]


def _block_diag(w):
    g, c, d = w.shape
    return (w[:, :, None, :] * jnp.eye(g, dtype=w.dtype)[:, None, :, None]).reshape(g * c, g * d)


def _diag_blocks(w_bd):
    g = POOL_W // POOL_GROUP
    return jnp.stack([w_bd[i * POOL_GROUP:(i + 1) * POOL_GROUP, i * POOL_GROUP:(i + 1) * POOL_GROUP] for i in range(g)])


def _flat(a):
    return a.reshape(-1, a.shape[-1])


def _mix_half_fwd(tag, st, mods, wl, kvc, *, local, kv_only, exch=None):
    sh1, sc1, g1 = mods[:3]
    B, n, d = st[0].shape
    x, h, h_t = _norm_fwd(f"{tag}_norm1", st, wl["n1"], sc1, sh1)
    if kv_only:
        kv = _mm(_flat(h), wl["w_in"][:, ATTN_W:ATTN_W + 2 * KV_W], name=f"{tag}_kv").reshape(B, n, 2 * KV_W)
        return None, dict(st=st, h_t=h_t, kvc=kv), []
    u = _mm(_flat(h), wl["w_in"], name=f"{tag}_in").reshape(B, n, IN_W)
    if not local:
        kvc = u[:, :, ATTN_W:ATTN_W + 2 * KV_W]
    mix, got = _mixer_fwd(f"{tag}_mix", u, kvc, wl["margs"], local, exch)
    y = _mm(_flat(mix), wl["w_out"], name=f"{tag}_out").reshape(B, n, d)
    return (x, y, g1), dict(st=st, h_t=h_t, u=u, kvc=kvc, mix=mix), got


def _ffn_half_fwd(tag, st2, mods, wl, exch=None):
    sh2, sc2, g2 = mods[3:]
    B, n, d = st2[0].shape
    x1, h2, h2_t = _norm_fwd(f"{tag}_norm2", st2, wl["n2"], sc2, sh2)
    gu, act, act_t, *got = _mm_swiglu(_flat(h2), wl["w_ffn_in"], name=f"{tag}_ffn_in", exch=exch)
    y2 = _mm(act, wl["w_ffn_out"], name=f"{tag}_ffn_out").reshape(B, n, d)
    return (x1, y2, g2), dict(st2=st2, h2_t=h2_t, gu=gu, act_t=act_t), got


def _ffn_half_bwd(tag, sv, mods, wl, dx1, dy2):
    sh2, sc2, _ = mods[3:]
    B, n, d = sv["st2"][0].shape
    gw = {}
    dy2f = _flat(dy2)
    gw["w_ffn_out"] = _mm(sv["act_t"], dy2f, out_dtype=BF16, name=f"{tag}_ffn_out_dw")
    dgu = _mm_dswiglu(dy2f, wl["w_ffn_out"], sv["gu"], name=f"{tag}_ffn_out_dx")
    dh2 = _mm(dgu, wl["w_ffn_in"], trans_b=True, name=f"{tag}_ffn_in_dx").reshape(B, n, d)
    gw["w_ffn_in"] = _mm(sv["h2_t"], dgu, out_dtype=BF16, tn=FF_TILE, out_block=_natural_block, name=f"{tag}_ffn_in_dw")
    dx, dy, dg1, gw["n2"], dsc2, dsh2 = _norm_bwd(f"{tag}_norm2", sv["st2"], wl["n2"], sc2, sh2, dx1, dh2)
    return (dx, dy, dg1), gw, dict(sh2=dsh2, sc2=dsc2)


def _mix_half_bwd(tag, sv, mods, wl, dx, dy, dkv_in, *, local, kv_only, exch=None):
    sh1, sc1, _ = mods[:3]
    B, n, d = sv["st"][0].shape
    gw = {}
    if kv_only:
        dkv = _flat(dkv_in).astype(BF16)
        dh = _mm(dkv, wl["w_in"][:, ATTN_W:ATTN_W + 2 * KV_W], trans_b=True, name=f"{tag}_kv_dx").reshape(B, n, d)
        gw["w_in_kv"] = _mm(sv["h_t"], dkv, out_dtype=BF16, name=f"{tag}_kv_dw")
        dxb, dy_prev, dgate_prev, gw["n1"], dsc1, dsh1 = _norm_bwd(f"{tag}_norm1", sv["st"], wl["n1"], sc1, sh1,
                                                                    jnp.zeros((B, n, d), F32), dh)
        return (dxb, dy_prev, dgate_prev), gw, dict(sh1=dsh1, sc1=dsc1), None, []

    dyf = _flat(dy)
    dmix = _mm(dyf, wl["w_out"], trans_b=True, name=f"{tag}_out_dx").reshape(B, n, d)
    gw["w_out"] = _mm(_flat(sv["mix"]).T, dyf, out_dtype=BF16, name=f"{tag}_out_dw")
    (dq, dk, dv, da, dg, dpu), (dkc, dvc), gw["margs"], got = _mixer_bwd(f"{tag}_mix", sv["u"], sv["kvc"], wl["margs"], dmix,
                                                                     local, exch)
    if local:
        dkv_out = jnp.concatenate([dkc, dvc], axis=-1)
    else:
        dk = dkc + dkv_in[:, :, :KV_W]
        dv = dvc + dkv_in[:, :, KV_W:]
        dkv_out = None
    du = _flat(jnp.concatenate([dq, dk, dv, da, dg, dpu], axis=-1).astype(BF16))
    dh = _mm(du, wl["w_in"], trans_b=True, name=f"{tag}_in_dx").reshape(B, n, d)
    gw["w_in"] = _mm(sv["h_t"], du, out_dtype=BF16, name=f"{tag}_in_dw")
    dxb, dy_prev, dgate_prev, gw["n1"], dsc1, dsh1 = _norm_bwd(f"{tag}_norm1", sv["st"], wl["n1"], sc1, sh1, dx, dh)
    return (dxb, dy_prev, dgate_prev), gw, dict(sh1=dsh1, sc1=dsc1), dkv_out, got


BIG_W = ("w_in", "w_out", "w_ffn_in", "w_ffn_out")


def _local_step(x, ctx, m_loc, m_ctx, p, final_g, target, big):
    B = x.shape[0]
    depth = m_loc.shape[0]
    lat_mods = [[t[:, None, :] for t in jnp.split(m_loc[l], 6, axis=-1)] for l in range(depth)]
    ctx_mods = [[jnp.broadcast_to(t[None, None, :], (B, 1, D_MODEL)) for t in jnp.split(m_ctx[l], 6)] for l in range(depth)]

    st, cst = (x, None, None), (ctx, None, None)
    w_mix, w_ffn, sv_mix, sv_ffn, csv_mix, csv_ffn = [], [], [], [], [], []
    got = []
    for l in range(depth):
        last = l == depth - 1
        wb = big.mix_weights(l, got)
        wm = dict(n1=p["norm1_g"][l][None, :], w_in=wb["w_in"], w_out=wb["w_out"],
                  margs=(p["attn_sink"][l][None, :], p["conv_dw"][l], p["conv_dw_b"][l][None, :], p["conv_ln_g"][l][None, :],
                         p["conv_ln_b"][l][None, :], _block_diag(p["pool_w"][l]), p["pool_scale"][l][None, :]))
        cst, csv, _ = _mix_half_fwd(f"l{l}c", cst, ctx_mods[l], wm, None, local=False, kv_only=last)
        st, sv, got = _mix_half_fwd(f"l{l}", st, lat_mods[l], wm, csv["kvc"], local=True, kv_only=False,
                                    exch=big.ride_attn_fwd(l))
        w_mix.append(wm)
        sv_mix.append(sv)
        csv_mix.append(csv)
        wb = big.ffn_weights(l, got)
        w_ffn_in = _interleave_ffn(wb["w_ffn_in"], name=f"l{l}_ffn_in_interleave")
        wf = dict(n2=p["norm2_g"][l][None, :], w_ffn_in=w_ffn_in, w_ffn_out=wb["w_ffn_out"])
        csv = None
        if not last:
            cst, csv, _ = _ffn_half_fwd(f"l{l}c", cst, ctx_mods[l], wf)
        st, sv, got = _ffn_half_fwd(f"l{l}", st, lat_mods[l], wf, exch=big.ride_ffn_fwd(l))
        w_ffn.append(wf)
        sv_ffn.append(sv)
        csv_ffn.append(csv)
    loss_rows, dx, dy, dgate, dfinal = _loss_head(st, final_g[None, :], target, name="loss_head")

    dm_loc, dm_ctx = [None] * depth, [None] * depth
    small = [None] * depth
    cdx = cdy = cdgate = None
    up_mix = None
    for l in reversed(range(depth)):
        last = l == depth - 1
        dm, cdm = dict(g2=dgate), {}
        (dx, dy, dm["g1"]), gf, d = _ffn_half_bwd(f"l{l}", sv_ffn[l], lat_mods[l], w_ffn[l], dx, dy)
        dm.update(d)
        if not last:
            cdm["g2"] = cdgate
            (cdx, cdy, cdm["g1"]), cgf, d = _ffn_half_bwd(f"l{l}c", csv_ffn[l], ctx_mods[l], w_ffn[l], cdx, cdy)
            cdm.update(d)
            gf = {k: gf[k] + cgf[k] for k in gf}
        ffn_grads = {k: gf[k] for k in _ShardedWeights.FFN}
        (dx, dy, dgate), gm, d, dkv, got = _mix_half_bwd(f"l{l}", sv_mix[l], lat_mods[l], w_mix[l], dx, dy, None, local=True,
                                                        kv_only=False, exch=big.ride_attn_bwd(l, ffn_grads, up_mix))
        big.took(l, ffn_grads, up_mix, got)
        dm.update(d)
        (cdx, cdy, cdgate), cgm, d, _, _ = _mix_half_bwd(f"l{l}c", csv_mix[l], ctx_mods[l], w_mix[l], cdx, cdy, dkv,
                                                        local=False, kv_only=last)
        cdm.update(d)
        order = ("sh1", "sc1", "g1", "sh2", "sc2", "g2")
        dm_loc[l] = jnp.concatenate([dm[k][:, 0, :] for k in order], axis=-1)
        dm_ctx[l] = jnp.concatenate([jnp.sum(cdm[k][:, 0, :], axis=0) if k in cdm else jnp.zeros((D_MODEL,), F32)
                                     for k in order])
        if last:
            up_mix = dict(w_in=gm["w_in"].at[:, ATTN_W:ATTN_W + 2 * KV_W].add(cgm["w_in_kv"]), w_out=gm["w_out"])
            margs = gm["margs"]
        else:
            up_mix = {k: gm[k] + cgm[k] for k in _ShardedWeights.MIX}
            margs = tuple(a + b for a, b in zip(gm["margs"], cgm["margs"]))
        small[l] = dict(n1=gm["n1"] + cgm["n1"], n2=gf["n2"], margs=margs)
    big.leftover(up_mix)

    stack = lambda f: jnp.stack([f(small[l]) for l in range(depth)])
    dp = dict(
        norm1_g=stack(lambda g: g["n1"][0]), norm2_g=stack(lambda g: g["n2"][0]),
        attn_sink=stack(lambda g: g["margs"][0][0]), conv_dw=stack(lambda g: g["margs"][1]),
        conv_dw_b=stack(lambda g: g["margs"][2][0]), conv_ln_g=stack(lambda g: g["margs"][3][0]),
        conv_ln_b=stack(lambda g: g["margs"][4][0]), pool_w=stack(lambda g: _diag_blocks(g["margs"][5])),
        pool_scale=stack(lambda g: g["margs"][6][0]))
    return jnp.sum(loss_rows[:, 0, 0]), dx, jnp.stack(dm_loc), jnp.stack(dm_ctx), dp, dfinal[0]


PACK_COLS = 1024


def _pack(arrs):
    flat = jnp.concatenate([a.reshape(-1).astype(F32) for a in arrs])
    rows = -(-flat.shape[0] // (8 * PACK_COLS)) * 8
    return jnp.pad(flat, (0, rows * PACK_COLS - flat.shape[0])).reshape(rows, PACK_COLS)


def _unpack(slab, like):
    flat = slab.reshape(-1)
    out, off = [], 0
    for a in like:
        out.append(flat[off:off + a.size].reshape(a.shape))
        off += a.size
    return out


def _shard_cols(gathered):
    _, L, R, C = gathered.shape
    return jnp.transpose(gathered, (1, 2, 0, 3)).reshape(L, R, N_DEV * C)


class _ShardedWeights:
    MIX = ("w_in", "w_out")
    FFN = ("w_ffn_in", "w_ffn_out")
    BY_COLS = ("w_in", "w_ffn_in")

    def __init__(self, shards, first):
        self.shards = shards
        self.first = first
        self.depth = shards[BIG_W[0]].shape[0]
        self.parts = [dict() for _ in range(self.depth)]
        self.left = None

    def _join(self, names, blocks):
        out = {}
        for name, g in zip(names, blocks):
            _, R, C = g.shape
            out[name] = jnp.transpose(g, (1, 0, 2)).reshape(R, N_DEV * C) if name in self.BY_COLS else g.reshape(N_DEV * R, C)
        return out

    def cut(self, names, grads):
        out = []
        for name in names:
            g = grads[name]
            if name in self.BY_COLS:
                R, C8 = g.shape
                out.append(jnp.transpose(g.reshape(R, N_DEV, C8 // N_DEV), (1, 0, 2)))
            else:
                R8, C = g.shape
                out.append(g.reshape(N_DEV, R8 // N_DEV, C))
        return out

    def mix_weights(self, l, got):
        return self._join(self.MIX, self.first if l == 0 else got)

    def ffn_weights(self, l, got):
        return self._join(self.FFN, got)

    def ride_attn_fwd(self, l):
        return [self.shards[name][l] for name in self.FFN], False

    def ride_ffn_fwd(self, l):
        if l + 1 >= self.depth:
            return None
        return [self.shards[name][l + 1] for name in self.MIX], False

    def ride_attn_bwd(self, l, ffn_grads, up_mix):
        return self.cut(self.FFN, ffn_grads) + (self.cut(self.MIX, up_mix) if up_mix is not None else []), True

    def took(self, l, ffn_grads, up_mix, got):
        self.parts[l].update(zip(self.FFN, got[:2]))
        if up_mix is not None:
            self.parts[l + 1].update(zip(self.MIX, got[2:]))

    def leftover(self, mix_grads):
        self.left = mix_grads


def _as_rows(a, leading=0):
    return a.reshape(*a.shape[:leading], -1, PACK_COLS)


SMALL = ("c_ctx", "b_mod", "norm1_g", "norm2_g", "conv_dw_b", "conv_ln_g", "conv_ln_b", "attn_sink", "pool_w",
         "pool_scale", "final_g", "conv_dw")
BIG = ("w_mod", "w_in", "w_out", "w_ffn_in", "w_ffn_out")
ORDER = ("c_ctx", "w_mod", "b_mod", "norm1_g", "norm2_g", "w_in", "conv_dw", "conv_dw_b", "conv_ln_g", "conv_ln_b",
         "attn_sink", "pool_w", "pool_scale", "w_out", "w_ffn_in", "w_ffn_out", "final_g")


def kernel(x, c, ctx, c_ctx, w_mod, b_mod, norm1_g, norm2_g, w_in, conv_dw, conv_dw_b, conv_ln_g, conv_ln_b, attn_sink, pool_w, pool_scale, w_out, w_ffn_in, w_ffn_out, final_g, loss_target, m_c_ctx, m_w_mod, m_b_mod, m_norm1_g, m_norm2_g, m_w_in, m_conv_dw, m_conv_dw_b, m_conv_ln_g, m_conv_ln_b, m_attn_sink, m_pool_w, m_pool_scale, m_w_out, m_w_ffn_in, m_w_ffn_out, m_final_g, v_c_ctx, v_w_mod, v_b_mod, v_norm1_g, v_norm2_g, v_w_in, v_conv_dw, v_conv_dw_b, v_conv_ln_g, v_conv_ln_b, v_attn_sink, v_pool_w, v_pool_scale, v_w_out, v_w_ffn_in, v_w_ffn_out, v_final_g):
    w = dict(c_ctx=c_ctx, w_mod=w_mod, b_mod=b_mod, norm1_g=norm1_g, norm2_g=norm2_g, w_in=w_in, conv_dw=conv_dw,
             conv_dw_b=conv_dw_b, conv_ln_g=conv_ln_g, conv_ln_b=conv_ln_b, attn_sink=attn_sink, pool_w=pool_w,
             pool_scale=pool_scale, w_out=w_out, w_ffn_in=w_ffn_in, w_ffn_out=w_ffn_out, final_g=final_g)
    mom = dict(c_ctx=m_c_ctx, w_mod=m_w_mod, b_mod=m_b_mod, norm1_g=m_norm1_g, norm2_g=m_norm2_g, w_in=m_w_in,
               conv_dw=m_conv_dw, conv_dw_b=m_conv_dw_b, conv_ln_g=m_conv_ln_g, conv_ln_b=m_conv_ln_b,
               attn_sink=m_attn_sink, pool_w=m_pool_w, pool_scale=m_pool_scale, w_out=m_w_out, w_ffn_in=m_w_ffn_in,
               w_ffn_out=m_w_ffn_out, final_g=m_final_g)
    var = dict(c_ctx=v_c_ctx, w_mod=v_w_mod, b_mod=v_b_mod, norm1_g=v_norm1_g, norm2_g=v_norm2_g, w_in=v_w_in,
               conv_dw=v_conv_dw, conv_dw_b=v_conv_dw_b, conv_ln_g=v_conv_ln_g, conv_ln_b=v_conv_ln_b,
               attn_sink=v_attn_sink, pool_w=v_pool_w, pool_scale=v_pool_scale, w_out=v_w_out, w_ffn_in=v_w_ffn_in,
               w_ffn_out=v_w_ffn_out, final_g=v_final_g)
    B = x.shape[0]
    depth = w_mod.shape[0]
    mod_cols = w_mod.shape[2]
    dw_cols = conv_dw.shape[2]
    me = 4 * lax.axis_index("x") + 2 * lax.axis_index("y") + lax.axis_index("c")

    shards = {name: w[name].astype(BF16) for name in BIG_W}
    c_all, dw_all, *first = _exchange([c, conv_dw] + [shards[name][0] for name in _ShardedWeights.MIX], scatter=False,
                                      name="gather_first")
    big = _ShardedWeights(shards, first)
    p = dict(norm1_g=norm1_g, norm2_g=norm2_g, conv_dw=_shard_cols(dw_all), conv_dw_b=conv_dw_b, conv_ln_g=conv_ln_g,
             conv_ln_b=conv_ln_b, attn_sink=attn_sink, pool_w=pool_w, pool_scale=pool_scale)

    cc = jnp.concatenate([c_all.reshape(N_DEV * B, D_MODEL), jnp.broadcast_to(c_ctx[None, :], (N_DEV, D_MODEL)),
                          jnp.zeros((MOD_ROWS - N_DEV * B - N_DEV, D_MODEL), F32)], axis=0)
    b_shard = lax.dynamic_slice_in_dim(b_mod, me * mod_cols, mod_cols, axis=1)[:, None, :]
    m_part = _mod_fwd(cc, w_mod, b_shard, name="mod_fwd")
    m_all, = _exchange([m_part], scatter=False, name="gather_mod")
    m_full = _shard_cols(m_all)
    m_loc = lax.dynamic_slice_in_dim(m_full, me * B, B, axis=1)
    m_ctx = m_full[:, N_DEV * B, :]

    loss_part, dx, dm_loc, dm_ctx, dp, dfinal = _local_step(x, ctx, m_loc, m_ctx, p, final_g, loss_target, big)
    loss = lax.psum(loss_part, AXES)

    dm_rows = jnp.concatenate([dm_loc, dm_ctx[:, None, :], jnp.zeros((depth, 8 - B - 1, 6 * D_MODEL), F32)], axis=1)
    dm_all, = _exchange([dm_rows], scatter=False, name="gather_dmod")
    dm_full = jnp.concatenate([
        jnp.transpose(dm_all[:, :, :B, :], (1, 0, 2, 3)).reshape(depth, N_DEV * B, 6 * D_MODEL),
        jnp.transpose(dm_all[:, :, B, :], (1, 0, 2)),
        jnp.zeros((depth, MOD_ROWS - N_DEV * B - N_DEV, 6 * D_MODEL), F32)], axis=1)
    g_b_mod = jnp.stack([_sum_leading(dm_full[l][:, None, :], name=f"b_mod_grad{l}")[0] for l in range(depth)])
    dm_mine = lax.dynamic_slice_in_dim(dm_full, me * mod_cols, mod_cols, axis=2)
    dcc, g_w_mod = _mod_bwd(cc, w_mod, b_shard, dm_mine, name="mod_bwd")
    g_c_ctx_part = jnp.sum(dcc[N_DEV * B:N_DEV * B + N_DEV], axis=0)

    small_like = [c_ctx, norm1_g, norm2_g, conv_dw_b, conv_ln_g, conv_ln_b, attn_sink, pool_w, pool_scale, final_g,
                  dp["conv_dw"]]
    small_part = _pack([g_c_ctx_part, dp["norm1_g"], dp["norm2_g"], dp["conv_dw_b"], dp["conv_ln_g"], dp["conv_ln_b"],
                        dp["attn_sink"], dp["pool_w"], dp["pool_scale"], dfinal, dp["conv_dw"]])
    small_all, = _exchange([small_part], scatter=False, name="gather_small")
    small_sum = _unpack(_sum_leading(small_all, name="sum_small"), small_like)
    g = dict(zip(("c_ctx", "norm1_g", "norm2_g", "conv_dw_b", "conv_ln_g", "conv_ln_b", "attn_sink", "pool_w",
                  "pool_scale", "final_g"), small_sum[:-1]))
    g["b_mod"] = g_b_mod
    g["conv_dw"] = lax.dynamic_slice_in_dim(small_sum[-1], me * dw_cols, dw_cols, axis=2)

    big.parts[0].update(zip(big.MIX, _exchange(big.cut(big.MIX, big.left), scatter=True, name="scatter_last")))

    delta, new_m, new_v = {}, {}, {}
    for name in BIG_W:
        g[name], delta[name], new_m[name], new_v[name] = _adamw(
            [big.parts[l][name] for l in range(depth)], w[name], mom[name], var[name], name=f"adamw_{name}")
    g["w_mod"], delta["w_mod"], new_m["w_mod"], new_v["w_mod"] = _adamw(g_w_mod, w_mod, m_w_mod, v_w_mod, name="adamw_w_mod")
    res = _adamw_small([g[k] for k in SMALL], [w[k] for k in SMALL], [mom[k] for k in SMALL], [var[k] for k in SMALL],
                       name="adamw_small")
    for dst, arrs in zip((delta, new_m, new_v), res):
        dst.update(zip(SMALL, arrs))

    return (loss, dx, *[g[k] for k in ORDER], *[delta[k] for k in ORDER], *[new_m[k] for k in ORDER],
            *[new_v[k] for k in ORDER])
```

```python
import functools

import numpy as np
import jax
import jax.numpy as jnp
from jax import lax
from jax.experimental import pallas as pl
from jax.experimental.pallas import tpu as pltpu

F32 = jnp.float32
BF16 = jnp.bfloat16

D_MODEL = 1024
GRID_W = 64
HEAD_DIM = 64
ATTN_W = 512
CONV_W = 256
POOL_W = 256
ATTN_HEADS = 8
KV_HEADS = 2
GROUP = ATTN_HEADS // KV_HEADS
KV_W = KV_HEADS * HEAD_DIM
IN_W = ATTN_W + 2 * KV_W + 2 * CONV_W + POOL_W
WINDOW = 128
Q_BLOCK = 128
SPAN = Q_BLOCK + 2 * WINDOW
CONV_KERNEL = 31
POOL_WINDOWS = (2, 4, 8, 16)
POOL_GROUP = 64
ROPE_BASE = 10000.0
D_FF = 2816
EPS = 1e-6
NEG = -1e30
N_DEV = 8
AXES = ("x", "y", "c")

ADAM_LR = 0.001
ADAM_B1 = 0.9
ADAM_B2 = 0.999
ADAM_EPS = 1e-08
ADAM_WD = 0.01
ADAM_STEP = 10

VMEM_LIMIT = 56 * 1024 * 1024
HALO = 16
SEQ_CHUNK = 512
CONV_BWD_CHUNK = 256


def _params(*sem):
    return pltpu.CompilerParams(dimension_semantics=sem, vmem_limit_bytes=VMEM_LIMIT)


def _tile(dim, target):
    if dim <= target:
        return dim
    t = (target // 128) * 128
    while t >= 128:
        if dim % t == 0:
            return t
        t -= 128
    raise ValueError(f"no tile for {dim}")


MM_VMEM_BUDGET = 40 * 1024 * 1024


def _dot(a, b):
    return lax.dot_general(a.astype(BF16), b.astype(BF16), (((1,), (0,)), ((), ())), preferred_element_type=F32)


def _mm_vmem(tm, tn, tk, whole, a_bytes, b_bytes, o_bytes):
    return 2 * (tm * tk * a_bytes + tk * tn * b_bytes + tm * tn * o_bytes) + (0 if whole else tm * tn * 4)


def _mm(a, b, *, name, out_dtype=F32, tm=1408, tn=512, trans_b=False, out_block=None):
    M, K = a.shape
    N, K2 = b.shape if trans_b else b.shape[::-1]
    assert K == K2, (a.shape, b.shape)
    tm = _tile(M, tm)
    tn = _tile(N, tn)
    sizes = (a.dtype.itemsize, b.dtype.itemsize, jnp.dtype(out_dtype).itemsize)
    tk = next(t for t in range(K, 0, -128) if K % t == 0 and _mm_vmem(tm, tn, t, t == K, *sizes) <= MM_VMEM_BUDGET)
    nk = K // tk

    def body(a_ref, b_ref, o_ref, *scratch):
        part = (_dot_nt if trans_b else _dot)(a_ref[...], b_ref[...])
        if nk == 1:
            o_ref[...] = part.astype(o_ref.dtype)
        else:
            acc_ref, = scratch
            k = pl.program_id(2)

            @pl.when(k == 0)
            def _():
                acc_ref[...] = part

            @pl.when(k > 0)
            def _():
                acc_ref[...] += part

            @pl.when(k == nk - 1)
            def _():
                o_ref[...] = acc_ref[...].astype(o_ref.dtype)

    return pl.pallas_call(
        body, name=name, grid=(M // tm, N // tn, nk),
        in_specs=[pl.BlockSpec((tm, tk), lambda i, j, k: (i, k)),
                  pl.BlockSpec((tn, tk), lambda i, j, k: (j, k)) if trans_b else pl.BlockSpec((tk, tn), lambda i, j, k: (k, j))],
        out_specs=pl.BlockSpec((tm, tn), (lambda i, j, k: (i, j)) if out_block is None else (lambda i, j, k: (i, out_block(j)))),
        out_shape=jax.ShapeDtypeStruct((M, N), out_dtype),
        scratch_shapes=[pltpu.VMEM((tm, tn), F32)] if nk > 1 else [],
        compiler_params=_params("parallel", "parallel", "arbitrary"),
    )(a, b)


FF_TILE = 256


FF_TILES = D_FF // FF_TILE


def _natural_block(j):
    return (j % 2) * FF_TILES + j // 2


def _interleave_ffn(w, *, name):
    R, C = w.shape

    def body(w_ref, o_ref):
        o_ref[...] = w_ref[...]

    return pl.pallas_call(
        body, name=name, grid=(2 * FF_TILES,),
        in_specs=[pl.BlockSpec((R, FF_TILE), lambda j: (0, _natural_block(j)))],
        out_specs=pl.BlockSpec((R, FF_TILE), lambda j: (0, j)),
        out_shape=jax.ShapeDtypeStruct((R, C), w.dtype),
        compiler_params=_params("parallel"),
    )(w)


def _swiglu(gu):
    g, u = gu[:, :FF_TILE], gu[:, FF_TILE:]
    return g * jax.nn.sigmoid(g) * u


EPILOGUE_SPLIT = 1
FF_ROWS = 4096


def _mm_swiglu(a, w_il, *, name, tm=FF_ROWS, exch=None, split=EPILOGUE_SPLIT):
    M, K = a.shape
    tm = _tile(M, tm)
    rc = tm // split

    def body(a_ref, b_ref, gu_ref, act_ref, act_t_ref):
        b = b_ref[...]
        parts = [_dot(a_ref[r0:r0 + rc, :], b) for r0 in range(0, tm, rc)]
        for r0, gu in zip(range(0, tm, rc), parts):
            gu_ref[r0:r0 + rc, :] = gu.astype(gu_ref.dtype)
            act = _swiglu(gu)
            act_ref[r0:r0 + rc, :] = act.astype(act_ref.dtype)
            act_t_ref[:, r0:r0 + rc] = act.T.astype(act_t_ref.dtype)

    grid = (M // tm, D_FF // FF_TILE)
    body, x_in, x_out, x_shapes, x_sems = _riding(body, exch, 2, 3, grid)
    return pl.pallas_call(
        body, name=name, grid=grid,
        in_specs=[pl.BlockSpec((tm, K), lambda i, j: (i, 0)), pl.BlockSpec((K, 2 * FF_TILE), lambda i, j: (0, j))] + x_in,
        out_specs=[pl.BlockSpec((tm, 2 * FF_TILE), lambda i, j: (i, j)), pl.BlockSpec((tm, FF_TILE), lambda i, j: (i, j)),
                   pl.BlockSpec((FF_TILE, tm), lambda i, j: (j, i))] + x_out,
        out_shape=[jax.ShapeDtypeStruct((M, 2 * D_FF), BF16), jax.ShapeDtypeStruct((M, D_FF), BF16),
                   jax.ShapeDtypeStruct((D_FF, M), BF16)] + x_shapes,
        scratch_shapes=x_sems,
        compiler_params=_params("arbitrary", "arbitrary") if exch else _params("parallel", "parallel"),
    )(a, w_il, *(exch[0] if exch else []))


def _mm_dswiglu(dy, w_out, gu, *, name, tm=FF_ROWS, split=EPILOGUE_SPLIT):
    M, K = dy.shape
    tm = _tile(M, tm)
    rc = tm // split

    def body(dy_ref, b_ref, gu_ref, o_ref):
        b = b_ref[...]
        parts = [_dot_nt(dy_ref[r0:r0 + rc, :], b) for r0 in range(0, tm, rc)]
        for r0, dact in zip(range(0, tm, rc), parts):
            g = gu_ref[r0:r0 + rc, :FF_TILE].astype(F32)
            u = gu_ref[r0:r0 + rc, FF_TILE:].astype(F32)
            sig = jax.nn.sigmoid(g)
            silu = g * sig
            o_ref[r0:r0 + rc, :FF_TILE] = (dact * u * (sig + silu * (1.0 - sig))).astype(o_ref.dtype)
            o_ref[r0:r0 + rc, FF_TILE:] = (dact * silu).astype(o_ref.dtype)

    return pl.pallas_call(
        body, name=name, grid=(M // tm, D_FF // FF_TILE),
        in_specs=[pl.BlockSpec((tm, K), lambda i, j: (i, 0)), pl.BlockSpec((FF_TILE, K), lambda i, j: (j, 0)),
                  pl.BlockSpec((tm, 2 * FF_TILE), lambda i, j: (i, j))],
        out_specs=pl.BlockSpec((tm, 2 * FF_TILE), lambda i, j: (i, j)),
        out_shape=jax.ShapeDtypeStruct((M, 2 * D_FF), BF16),
        compiler_params=_params("parallel", "parallel"),
    )(dy, w_out, gu)


def _rope_tables(n):
    rows = n // GRID_W
    row = jnp.repeat(jnp.arange(rows), GRID_W).astype(F32)
    col = jnp.tile(jnp.arange(GRID_W), rows).astype(F32)
    half = HEAD_DIM // 2
    inv = ROPE_BASE ** (-jnp.arange(0, half, 2, dtype=F32) / half)
    ar = row[:, None] * inv
    ac = col[:, None] * inv
    ang = jnp.concatenate([ar, ar, ac, ac], axis=-1)
    return jnp.cos(ang), jnp.sin(ang)


def _rot_half(x):
    w = x.shape[-1]
    lane = lax.broadcasted_iota(jnp.int32, x.shape, 1)
    up = pltpu.roll(x, w - 16, 1)
    down = pltpu.roll(x, 16, 1)
    return jnp.where((lane & 16) == 0, -up, down)


def _rope(x, cos, sin):
    return x * cos + _rot_half(x) * sin


def _rope_bwd(d, cos, sin):
    return d * cos - _rot_half(d * sin)


def _dot_nt(a, b):
    return lax.dot_general(a.astype(BF16), b.astype(BF16), (((1,), (1,)), ((), ())), preferred_element_type=F32)


def _softmax_sink(s, sink_rows):
    mx = jnp.maximum(jnp.max(s, axis=1, keepdims=True), sink_rows)
    e = jnp.exp(s - mx)
    es = jnp.exp(sink_rows - mx)
    inv = 1.0 / (jnp.sum(e, axis=1, keepdims=True) + es)
    return e * inv, es * inv


def _attn_operands(q_ref, k_ref, v_ref, kc_ref, vc_ref, cq_ref, sq_ref, ck_ref, sk_ref, i, n, n_ctx, local):
    q = q_ref[...]
    k_all, v_all, bias, s0, ck, sk = kc_ref[...], vc_ref[...], None, None, None, None
    if local:
        start, s0 = _span_start(i, n)
        ck = ck_ref[pl.ds(s0, SPAN), :]
        sk = sk_ref[pl.ds(s0, SPAN), :]
        q = _rope(q, cq_ref[...], sq_ref[...])
        k_all = jnp.concatenate([k_all, _rope(k_ref[pl.ds(s0, SPAN), :], ck, sk)], axis=0)
        v_all = jnp.concatenate([v_all, v_ref[pl.ds(s0, SPAN), :]], axis=0)
        bias = _window_bias(start, s0, n_ctx)
    q = (q * (HEAD_DIM ** -0.5)).astype(BF16)
    return q, k_all.astype(BF16), v_all.astype(BF16), bias, s0, ck, sk


def _stack_heads(x, kh):
    return jnp.concatenate([x[:, (GROUP * kh + g) * HEAD_DIM:(GROUP * kh + g + 1) * HEAD_DIM] for g in range(GROUP)], axis=0)


def _sink_rows(sink, kh):
    return jnp.concatenate([jnp.broadcast_to(sink[:, GROUP * kh + g:GROUP * kh + g + 1], (Q_BLOCK, 1)) for g in range(GROUP)], axis=0)


def _window_bias(start, s0, n_ctx):
    r = lax.broadcasted_iota(jnp.int32, (Q_BLOCK, n_ctx + SPAN), 0)
    c = lax.broadcasted_iota(jnp.int32, (Q_BLOCK, n_ctx + SPAN), 1)
    ok = (c < n_ctx) | (jnp.abs(start - s0 + r - (c - n_ctx)) <= WINDOW)
    return jnp.concatenate([jnp.where(ok, 0.0, NEG).astype(F32)] * GROUP, axis=0)


def _span_start(i, n):
    start = i * Q_BLOCK
    s0 = jnp.clip(start - WINDOW, 0, n - SPAN)
    return start, pl.multiple_of(s0, Q_BLOCK)


def _riding(body, exch, n_in, n_out, grid):
    if exch is None:
        return body, [], [], [], []
    arrs, scatter = exch
    k = len(arrs)

    def wrapped(*refs):
        ins, xin = refs[:n_in], refs[n_in:n_in + k]
        outs, xout = refs[n_in + k:n_in + k + n_out], refs[n_in + k + n_out:n_in + 2 * k + n_out]
        sems = refs[n_in + 2 * k + n_out:]
        b, i = pl.program_id(0), pl.program_id(1)

        @pl.when((b == 0) & (i == 0))
        def _():
            _exch_start(xin, xout, sems, scatter)

        body(*ins, *outs)

        @pl.when((b == grid[0] - 1) & (i == grid[1] - 1))
        def _():
            _exch_wait(xin, xout, sems, scatter)

    any_spec = pl.BlockSpec(memory_space=pl.ANY)
    return wrapped, [any_spec] * k, [any_spec] * k, _exch_out_shapes(arrs, scatter), _exch_sems(k)


def _attn_fwd(u, kvc, sink, cos, sin, *, local, name, exch=None):
    B, n, _ = u.shape
    n_ctx = kvc.shape[1]
    nb = n // Q_BLOCK
    assert (not local) or n >= SPAN

    def body(q_ref, k_ref, v_ref, kc_ref, vc_ref, sink_ref, cq_ref, sq_ref, ck_ref, sk_ref, o_ref):
        q, k_all, v_all, bias, _, _, _ = _attn_operands(q_ref, k_ref, v_ref, kc_ref, vc_ref, cq_ref, sq_ref, ck_ref, sk_ref,
                                                        pl.program_id(1), n, n_ctx, local)
        sink_v = sink_ref[...]
        sl = lambda kh: slice(kh * HEAD_DIM, (kh + 1) * HEAD_DIM)
        ss = [_dot_nt(_stack_heads(q, kh), k_all[:, sl(kh)]) for kh in range(KV_HEADS)]
        ps = [_softmax_sink(s if bias is None else s + bias, _sink_rows(sink_v, kh))[0].astype(BF16) for kh, s in enumerate(ss)]
        for kh, p in enumerate(ps):
            o = _dot(p, v_all[:, sl(kh)])
            for g in range(GROUP):
                h = GROUP * kh + g
                o_ref[:, h * HEAD_DIM:(h + 1) * HEAD_DIM] = o[g * Q_BLOCK:(g + 1) * Q_BLOCK, :].astype(o_ref.dtype)

    seq = lambda blk: pl.BlockSpec((None, n, KV_W), lambda b, i: (b, 0, blk))
    ctxs = lambda blk: pl.BlockSpec((None, n_ctx, KV_W), lambda b, i: (b, 0, blk))
    full = lambda a: pl.BlockSpec(a.shape, lambda b, i: (0,) * a.ndim)
    cos_q, sin_q = jnp.tile(cos, (1, ATTN_HEADS)), jnp.tile(sin, (1, ATTN_HEADS))
    cos_k, sin_k = jnp.tile(cos, (1, KV_HEADS)), jnp.tile(sin, (1, KV_HEADS))
    body, x_in, x_out, x_shapes, x_sems = _riding(body, exch, 10, 1, (B, nb))
    return pl.pallas_call(
        body, name=name, grid=(B, nb),
        in_specs=[pl.BlockSpec((None, Q_BLOCK, ATTN_W), lambda b, i: (b, i, 0)),
                  seq(ATTN_W // KV_W), seq(ATTN_W // KV_W + 1), ctxs(0), ctxs(1), full(sink),
                  pl.BlockSpec((Q_BLOCK, ATTN_W), lambda b, i: (i, 0)), pl.BlockSpec((Q_BLOCK, ATTN_W), lambda b, i: (i, 0)),
                  full(cos_k), full(sin_k)] + x_in,
        out_specs=[pl.BlockSpec((None, Q_BLOCK, ATTN_W), lambda b, i: (b, i, 0))] + x_out,
        out_shape=[jax.ShapeDtypeStruct((B, n, ATTN_W), BF16)] + x_shapes,
        scratch_shapes=x_sems,
        compiler_params=_params("arbitrary", "arbitrary"),
    )(u, u, u, kvc, kvc, sink, cos_q, sin_q, cos_k, sin_k, *(exch[0] if exch else []))


def _attn_bwd(u, kvc, sink, cos, sin, do_src, do_blk, *, local, name, exch=None):
    B, n, _ = u.shape
    n_ctx = kvc.shape[1]
    nb = n // Q_BLOCK

    def body(q_ref, k_ref, v_ref, kc_ref, vc_ref, sink_ref, cq_ref, sq_ref, ck_ref, sk_ref, do_ref,
             dq_ref, dk_ref, dv_ref, dkc_ref, dvc_ref, dsink_ref):
        b = pl.program_id(0)
        i = pl.program_id(1)

        @pl.when(i == 0)
        def _():
            dk_ref[...] = jnp.zeros_like(dk_ref)
            dv_ref[...] = jnp.zeros_like(dv_ref)
            dkc_ref[...] = jnp.zeros_like(dkc_ref)
            dvc_ref[...] = jnp.zeros_like(dvc_ref)

        @pl.when((i == 0) & (b == 0))
        def _():
            dsink_ref[...] = jnp.zeros_like(dsink_ref)

        q, k_all, v_all, bias, s0, ck, sk = _attn_operands(q_ref, k_ref, v_ref, kc_ref, vc_ref, cq_ref, sq_ref, ck_ref, sk_ref,
                                                           i, n, n_ctx, local)
        do = do_ref[...].astype(BF16)
        sink_v = sink_ref[...]
        sl = lambda kh: slice(kh * HEAD_DIM, (kh + 1) * HEAD_DIM)
        heads = range(KV_HEADS)
        q_st = [_stack_heads(q, kh) for kh in heads]
        do_st = [_stack_heads(do, kh) for kh in heads]
        ss = [_dot_nt(q_st[kh], k_all[:, sl(kh)]) for kh in heads]
        dps = [_dot_nt(do_st[kh], v_all[:, sl(kh)]) for kh in heads]
        p_bf, ds_bf = [], []
        dsink = jnp.zeros((1, ATTN_HEADS), F32)
        lane8 = lax.broadcasted_iota(jnp.int32, (1, ATTN_HEADS), 1)
        for kh in heads:
            p, p_sink = _softmax_sink(ss[kh] if bias is None else ss[kh] + bias, _sink_rows(sink_v, kh))
            delta = jnp.sum(p * dps[kh], axis=1, keepdims=True)
            ds = p * (dps[kh] - delta)
            dsr = -(p_sink * delta)
            for g in range(GROUP):
                dsink = dsink + jnp.where(lane8 == GROUP * kh + g, jnp.sum(dsr[g * Q_BLOCK:(g + 1) * Q_BLOCK, :]), 0.0)
            p_bf.append(p.astype(BF16))
            ds_bf.append(ds.astype(BF16))
        over_rows = (((0,), (0,)), ((), ()))
        dks, dvs = [], []
        for kh in heads:
            dq_st = _dot(ds_bf[kh], k_all[:, sl(kh)]) * (HEAD_DIM ** -0.5)
            for g in range(GROUP):
                h = GROUP * kh + g
                dq_ref[:, h * HEAD_DIM:(h + 1) * HEAD_DIM] = dq_st[g * Q_BLOCK:(g + 1) * Q_BLOCK, :]
            dvs.append(lax.dot_general(p_bf[kh], do_st[kh], over_rows, preferred_element_type=F32))
            dks.append(lax.dot_general(ds_bf[kh], q_st[kh], over_rows, preferred_element_type=F32))
        dk_cat = jnp.concatenate(dks, axis=1)
        dv_cat = jnp.concatenate(dvs, axis=1)
        dsink_ref[...] += dsink
        dkc_ref[...] += dk_cat[:n_ctx, :]
        dvc_ref[...] += dv_cat[:n_ctx, :]
        if local:
            dq_ref[...] = _rope_bwd(dq_ref[...], cq_ref[...], sq_ref[...])
            dk_ref[pl.ds(s0, SPAN), :] += _rope_bwd(dk_cat[n_ctx:, :], ck, sk)
            dv_ref[pl.ds(s0, SPAN), :] += dv_cat[n_ctx:, :]

    seq = lambda blk: pl.BlockSpec((None, n, KV_W), lambda b, i: (b, 0, blk))
    ctxs = lambda blk: pl.BlockSpec((None, n_ctx, KV_W), lambda b, i: (b, 0, blk))
    full = lambda a: pl.BlockSpec(a.shape, lambda b, i: (0,) * a.ndim)
    qblk = lambda blk: pl.BlockSpec((None, Q_BLOCK, ATTN_W), lambda b, i: (b, i, blk))
    cos_q, sin_q = jnp.tile(cos, (1, ATTN_HEADS)), jnp.tile(sin, (1, ATTN_HEADS))
    cos_k, sin_k = jnp.tile(cos, (1, KV_HEADS)), jnp.tile(sin, (1, KV_HEADS))
    acc = lambda rows: pl.BlockSpec((None, rows, KV_W), lambda b, i: (b, 0, 0))
    body, x_in, x_out, x_shapes, x_sems = _riding(body, exch, 11, 6, (B, nb))
    return pl.pallas_call(
        body, name=name, grid=(B, nb),
        in_specs=[qblk(0), seq(ATTN_W // KV_W), seq(ATTN_W // KV_W + 1), ctxs(0), ctxs(1), full(sink),
                  pl.BlockSpec((Q_BLOCK, ATTN_W), lambda b, i: (i, 0)), pl.BlockSpec((Q_BLOCK, ATTN_W), lambda b, i: (i, 0)),
                  full(cos_k), full(sin_k), qblk(do_blk)] + x_in,
        out_specs=[qblk(0), acc(n), acc(n), acc(n_ctx), acc(n_ctx), pl.BlockSpec((1, ATTN_HEADS), lambda b, i: (0, 0))] + x_out,
        out_shape=[jax.ShapeDtypeStruct((B, n, ATTN_W), F32), jax.ShapeDtypeStruct((B, n, KV_W), F32),
                   jax.ShapeDtypeStruct((B, n, KV_W), F32), jax.ShapeDtypeStruct((B, n_ctx, KV_W), F32),
                   jax.ShapeDtypeStruct((B, n_ctx, KV_W), F32), jax.ShapeDtypeStruct((1, ATTN_HEADS), F32)] + x_shapes,
        scratch_shapes=x_sems,
        compiler_params=_params("arbitrary", "arbitrary"),
    )(u, u, u, kvc, kvc, sink, cos_q, sin_q, cos_k, sin_k, do_src, *(exch[0] if exch else []))


def _conv_chunk(s, n, a_ext, g_ext, dw, dw_b, ln_g, ln_b):
    del s, n
    r = a_ext.shape[0] - 2 * HALO
    h = a_ext * jax.nn.sigmoid(g_ext)
    acc = jnp.broadcast_to(dw_b, (r, CONV_W))
    first = HALO - CONV_KERNEL // 2
    span = r + 8 * ((first + CONV_KERNEL - 1) // 8)
    shifted = [h[b:b + span, :] for b in range(8)]
    for k in range(CONV_KERNEL):
        o = first + k
        acc = acc + shifted[o % 8][o - o % 8:o - o % 8 + r, :] * dw[k:k + 1, :]
    mu = jnp.mean(acc, axis=-1, keepdims=True)
    var = jnp.mean(jnp.square(acc - mu), axis=-1, keepdims=True)
    hn = (acc - mu) * lax.rsqrt(var + EPS) * ln_g + ln_b
    return hn * jax.nn.sigmoid(hn)


def _pool_chunk(s, n, p_ext, w_bd, scale):
    r = p_ext.shape[0] - 2 * HALO
    lane = lax.broadcasted_iota(jnp.int32, (1, POOL_W), 1)
    win = jnp.left_shift(2, lane // POOL_GROUP)
    half = win // 2
    acc = jnp.zeros((r, POOL_W), F32)
    for d in range(-(POOL_WINDOWS[-1] // 2), POOL_WINDOWS[-1] - POOL_WINDOWS[-1] // 2):
        inside = (d >= -half) & (d <= win - 1 - half)
        acc = acc + jnp.where(inside, p_ext[HALO + d:HALO + d + r, :], 0.0)
    t = s + lax.broadcasted_iota(jnp.int32, (r, 1), 0)
    lo = jnp.maximum(t - half, 0)
    hi = jnp.minimum(t + win - 1 - half, n - 1)
    y = acc / (hi - lo + 1).astype(F32) - p_ext[HALO:HALO + r, :]
    out = lax.dot_general(y.astype(BF16), w_bd.astype(BF16), (((1,), (0,)), ((), ())), preferred_element_type=F32)
    return out * scale


def _seq_specs(rows, params):
    specs = [pl.BlockSpec((None, a.shape[1], w), functools.partial(lambda b, blk: (b, 0, blk), blk=blk)) for a, w, blk in rows]
    specs += [pl.BlockSpec(p.shape, functools.partial(lambda b, nd: (0,) * nd, nd=p.ndim)) for p in params]
    return specs


def _fill_padded(pad_ref, row_ref, n):
    w = pad_ref.shape[1]
    pad_ref[pl.ds(0, HALO), :] = jnp.zeros((HALO, w), F32)
    pad_ref[pl.ds(HALO + n, HALO), :] = jnp.zeros((HALO, w), F32)
    pad_ref[pl.ds(HALO, n), :] = row_ref[...]


def _seq_fwd(fn, rows, params, out_w, *, name, chunk=SEQ_CHUNK):
    B, n = rows[0][0].shape[:2]
    r = min(chunk, n)
    nr, npar = len(rows), len(params)

    def body(*refs):
        row_refs, par_refs, o_ref, pads = refs[:nr], refs[nr:nr + npar], refs[nr + npar], refs[nr + npar + 1:]
        for rr, p in zip(row_refs, pads):
            _fill_padded(p, rr, n)
        pars = [p[...] for p in par_refs]

        def chunk(ci, carry):
            s = pl.multiple_of(ci * r, r)
            ext = [p[pl.ds(s, r + 2 * HALO), :] for p in pads]
            o_ref[pl.ds(s, r), :] = fn(s, n, *ext, *pars).astype(o_ref.dtype)
            return carry

        lax.fori_loop(0, n // r, chunk, 0)

    return pl.pallas_call(
        body, name=name, grid=(B,),
        in_specs=_seq_specs(rows, params),
        out_specs=pl.BlockSpec((None, n, out_w), lambda b: (b, 0, 0)),
        out_shape=jax.ShapeDtypeStruct((B, n, out_w), BF16),
        scratch_shapes=[pltpu.VMEM((n + 2 * HALO, w), F32) for _, w, _ in rows],
        compiler_params=_params("parallel"),
    )(*[a for a, _, _ in rows], *params)


def _seq_bwd(fn, rows, params, dout, *, name, chunk=SEQ_CHUNK):
    B, n = rows[0][0].shape[:2]
    r = min(chunk, n)
    nr, npar = len(rows), len(params)

    def body(*refs):
        row_refs, par_refs, do_ref = refs[:nr], refs[nr:nr + npar], refs[nr + npar]
        outs = refs[nr + npar + 1:]
        drow_refs, dpar_refs = outs[:nr], outs[nr:nr + npar]
        scratch = outs[nr + npar:]
        pads, dpads = scratch[:nr], scratch[nr:]
        for rr, p, dp in zip(row_refs, pads, dpads):
            _fill_padded(p, rr, n)
            dp[...] = jnp.zeros_like(dp)

        @pl.when(pl.program_id(0) == 0)
        def _():
            for d in dpar_refs:
                d[...] = jnp.zeros_like(d)

        pars = [p[...] for p in par_refs]

        def chunk(ci, carry):
            s = pl.multiple_of(ci * r, r)
            ext = [p[pl.ds(s, r + 2 * HALO), :] for p in pads]
            _, vjp = jax.vjp(functools.partial(fn, s, n), *ext, *pars)
            grads = vjp(do_ref[pl.ds(s, r), :])
            for dp, g in zip(dpads, grads[:nr]):
                dp[pl.ds(s, r + 2 * HALO), :] += g
            for d, g in zip(dpar_refs, grads[nr:]):
                d[...] += g
            return carry

        lax.fori_loop(0, n // r, chunk, 0)
        for d, dp in zip(drow_refs, dpads):
            d[...] = dp[pl.ds(HALO, n), :].astype(d.dtype)

    da, dw_, dblk = dout
    return pl.pallas_call(
        body, name=name, grid=(B,),
        in_specs=_seq_specs(rows, params) + [pl.BlockSpec((None, n, dw_), lambda b: (b, 0, dblk))],
        out_specs=[pl.BlockSpec((None, n, w), lambda b: (b, 0, 0)) for _, w, _ in rows]
        + [pl.BlockSpec(p.shape, functools.partial(lambda b, nd: (0,) * nd, nd=p.ndim)) for p in params],
        out_shape=[jax.ShapeDtypeStruct((B, n, w), BF16) for _, w, _ in rows]
        + [jax.ShapeDtypeStruct(p.shape, F32) for p in params],
        scratch_shapes=[pltpu.VMEM((n + 2 * HALO, w), F32) for _, w, _ in rows] * 2,
        compiler_params=_params("arbitrary"),
    )(*[a for a, _, _ in rows], *params, da)


_CONV_A_BLK = (ATTN_W + 2 * KV_W) // CONV_W
_CONV_G_BLK = _CONV_A_BLK + 1
_POOL_BLK = _CONV_A_BLK + 2


def _mixer_fwd(tag, u, kvc, margs, local, exch=None):
    sink, dw, dw_b, ln_g, ln_b, w_bd, scale = margs
    cos, sin = _rope_tables(max(u.shape[1], GRID_W))
    attn, *got = _attn_fwd(u, kvc, sink, cos, sin, local=local, name=f"{tag}_attn_fwd", exch=exch)
    conv = _seq_fwd(_conv_chunk, [(u, CONV_W, _CONV_A_BLK), (u, CONV_W, _CONV_G_BLK)], [dw, dw_b, ln_g, ln_b], CONV_W,
                    name=f"{tag}_conv_fwd")
    pool = _seq_fwd(_pool_chunk, [(u, POOL_W, _POOL_BLK)], [w_bd, scale], POOL_W, name=f"{tag}_pool_fwd")
    return jnp.concatenate([attn, conv, pool], axis=-1), got


def _mixer_bwd(tag, u, kvc, margs, dmix, local, exch=None):
    sink, dw, dw_b, ln_g, ln_b, w_bd, scale = margs
    cos, sin = _rope_tables(max(u.shape[1], GRID_W))
    dq, dk, dv, dkc, dvc, dsink, *got = _attn_bwd(u, kvc, sink, cos, sin, dmix, 0, local=local, name=f"{tag}_attn_bwd",
                                                  exch=exch)
    da, dg, ddw, ddw_b, dln_g, dln_b = _seq_bwd(
        _conv_chunk, [(u, CONV_W, _CONV_A_BLK), (u, CONV_W, _CONV_G_BLK)], [dw, dw_b, ln_g, ln_b],
        (dmix, CONV_W, ATTN_W // CONV_W), name=f"{tag}_conv_bwd", chunk=CONV_BWD_CHUNK)
    dpu, dw_bd, dscale = _seq_bwd(_pool_chunk, [(u, POOL_W, _POOL_BLK)], [w_bd, scale],
                                  (dmix, POOL_W, (ATTN_W + CONV_W) // POOL_W), name=f"{tag}_pool_bwd")
    return (dq, dk, dv, da, dg, dpu), (dkc, dvc), (dsink, ddw, ddw_b, dln_g, dln_b, dw_bd, dscale), got


def _row_specs(arrs, kinds, tr):
    specs = []
    for a, kind in zip(arrs, kinds):
        if kind == "row":
            specs.append(pl.BlockSpec((None, tr, a.shape[2]), lambda b, j: (b, j, 0)))
        elif kind == "batch":
            specs.append(pl.BlockSpec((None, 1, a.shape[2]), lambda b, j: (b, 0, 0)))
        else:
            specs.append(pl.BlockSpec(a.shape, functools.partial(lambda b, j, nd: (0,) * nd, nd=a.ndim)))
    return specs


def _rowwise_fwd(fn, ins, kinds, outs, tr, *, name, transposed=None):
    B, n = ins[0].shape[:2]
    ni, no = len(ins), len(outs)
    nj = n // tr

    def body(*refs):
        res = fn(*[r[...] for r in refs[:ni]])
        for o, v in zip(refs[ni:ni + no], res):
            o[...] = v.astype(o.dtype)
        if transposed is not None:
            refs[ni + no][...] = res[transposed].T.astype(refs[ni + no].dtype)

    out_specs = [pl.BlockSpec((None, tr, w), lambda b, j: (b, j, 0)) for w, _ in outs]
    out_shape = [jax.ShapeDtypeStruct((B, n, w), dt) for w, dt in outs]
    if transposed is not None:
        w, dt = outs[transposed]
        out_specs.append(pl.BlockSpec((w, tr), lambda b, j: (0, b * nj + j)))
        out_shape.append(jax.ShapeDtypeStruct((w, B * n), dt))
    return pl.pallas_call(
        body, name=name, grid=(B, nj),
        in_specs=_row_specs(ins, kinds, tr), out_specs=out_specs, out_shape=out_shape,
        compiler_params=_params("parallel", "parallel"),
    )(*ins)


def _rowwise_bwd(fn, ins, kinds, gdtypes, cts, tr, *, name):
    B, n = ins[0].shape[:2]
    ni, nc = len(ins), len(cts)
    idx = list(range(ni))

    def body(*refs):
        in_refs, ct_refs, out_refs = refs[:ni], refs[ni:ni + nc], refs[ni + nc:]
        b, j = pl.program_id(0), pl.program_id(1)
        _, vjp = jax.vjp(fn, *[r[...].astype(F32) for r in in_refs])
        grads = vjp(tuple(c[...].astype(F32) for c in ct_refs))
        for o, i in zip(out_refs, idx):
            g = grads[i]
            if kinds[i] == "row":
                o[...] = g.astype(o.dtype)
            else:
                first = (j == 0) if kinds[i] == "batch" else ((j == 0) & (b == 0))

                @pl.when(first)
                def _(o=o, g=g):
                    o[...] = g

                @pl.when(jnp.logical_not(first))
                def _(o=o, g=g):
                    o[...] += g

    specs = _row_specs(ins, kinds, tr)
    return pl.pallas_call(
        body, name=name, grid=(B, n // tr),
        in_specs=specs + [pl.BlockSpec((None, tr, c.shape[2]), lambda b, j: (b, j, 0)) for c in cts],
        out_specs=[specs[i] for i in idx],
        out_shape=[jax.ShapeDtypeStruct(ins[i].shape, gdtypes[i]) for i in idx],
        compiler_params=_params("arbitrary", "arbitrary"),
    )(*ins, *cts)


ROW_TILE = 512


def _rms_mod(x, g, sc, sh):
    y = x * lax.rsqrt(jnp.mean(x * x, axis=-1, keepdims=True) + EPS)
    return (y * g) * (1.0 + sc) + sh


def _norm_tile(x, g, sc, sh):
    return x, _rms_mod(x, g, sc, sh)


def _res_norm_tile(xb, y, gate, g, sc, sh):
    x = xb + gate * y
    return x, _rms_mod(x, g, sc, sh)


_NORM_KINDS = ("row", "glob", "batch", "batch")
_RES_NORM_KINDS = ("row", "row", "batch", "glob", "batch", "batch")


def _norm_fwd(tag, st, g, sc, sh):
    xb, y, gate = st
    tr = min(ROW_TILE, xb.shape[1])
    d = xb.shape[2]
    if y is None:
        h, h_t = _rowwise_fwd(lambda *a: (_rms_mod(*a),), [xb, g, sc, sh], _NORM_KINDS, [(d, BF16)], tr,
                              name=f"{tag}_fwd", transposed=0)
        return xb, h, h_t
    return _rowwise_fwd(_res_norm_tile, [xb, y, gate, g, sc, sh], _RES_NORM_KINDS, [(d, F32), (d, BF16)], tr,
                        name=f"{tag}_fwd", transposed=1)


def _norm_bwd(tag, st, g, sc, sh, dx, dh):
    xb, y, gate = st
    tr = min(ROW_TILE, xb.shape[1])
    if y is None:
        dxb, dg, dsc, dsh = _rowwise_bwd(_norm_tile, [xb, g, sc, sh], _NORM_KINDS, [F32] * 4, [dx, dh], tr, name=f"{tag}_bwd")
        return dxb, None, None, dg, dsc, dsh
    return tuple(_rowwise_bwd(_res_norm_tile, [xb, y, gate, g, sc, sh], _RES_NORM_KINDS, [F32, BF16, F32, F32, F32, F32],
                              [dx, dh], tr, name=f"{tag}_bwd"))


def _loss_head(st, final_g, target, *, name):
    xb, y, gate = st
    B, n, d = xb.shape
    tr = min(ROW_TILE, n)

    def tile_loss(xv, yv, gt, g, t):
        x = xv + gt * yv
        out = x * lax.rsqrt(jnp.mean(x * x, axis=-1, keepdims=True) + EPS) * g
        return 0.5 * jnp.sum(jnp.mean(jnp.square(out - t), axis=-1))

    def body(x_ref, y_ref, gate_ref, g_ref, t_ref, loss_ref, dx_ref, dy_ref, dgate_ref, dg_ref):
        b, j = pl.program_id(0), pl.program_id(1)
        val, (dx, dy, dgate, dg) = jax.value_and_grad(tile_loss, argnums=(0, 1, 2, 3))(
            x_ref[...], y_ref[...], gate_ref[...], g_ref[...], t_ref[...])
        dx_ref[...] = dx
        dy_ref[...] = dy.astype(dy_ref.dtype)

        @pl.when(j == 0)
        def _():
            loss_ref[...] = jnp.zeros_like(loss_ref)
            dgate_ref[...] = jnp.zeros_like(dgate_ref)

        @pl.when((j == 0) & (b == 0))
        def _():
            dg_ref[...] = jnp.zeros_like(dg_ref)

        loss_ref[...] += jnp.full(loss_ref.shape, val, F32)
        dgate_ref[...] += dgate
        dg_ref[...] += dg

    row = pl.BlockSpec((None, tr, d), lambda b, j: (b, j, 0))
    per_sample = pl.BlockSpec((None, 1, d), lambda b, j: (b, 0, 0))
    whole = pl.BlockSpec((1, d), lambda b, j: (0, 0))
    return pl.pallas_call(
        body, name=name, grid=(B, n // tr),
        in_specs=[row, row, per_sample, whole, row],
        out_specs=[pl.BlockSpec((None, 1, 128), lambda b, j: (b, 0, 0)), row, row, per_sample, whole],
        out_shape=[jax.ShapeDtypeStruct((B, 1, 128), F32), jax.ShapeDtypeStruct((B, n, d), F32),
                   jax.ShapeDtypeStruct((B, n, d), BF16), jax.ShapeDtypeStruct((B, 1, d), F32), jax.ShapeDtypeStruct((1, d), F32)],
        compiler_params=_params("arbitrary", "arbitrary"),
    )(xb, y, gate, final_g, target)


def _exchange(arrs, *, scatter, name):
    k = len(arrs)

    def body(*refs):
        ins, outs, sems = refs[:k], refs[k:2 * k], refs[2 * k:]
        _exch_start(ins, outs, sems, scatter)
        _exch_wait(ins, outs, sems, scatter)

    any_spec = pl.BlockSpec(memory_space=pl.ANY)
    return pl.pallas_call(
        body, name=name,
        in_specs=[any_spec] * k, out_specs=[any_spec] * k,
        out_shape=_exch_out_shapes(arrs, scatter), scratch_shapes=_exch_sems(k),
        compiler_params=pltpu.CompilerParams(has_side_effects=True),
    )(*arrs)


def _exch_out_shapes(arrs, scatter):
    return [jax.ShapeDtypeStruct(a.shape if scatter else (N_DEV,) + a.shape, a.dtype) for a in arrs]


def _exch_sems(k):
    return [pltpu.SemaphoreType.DMA((k * (N_DEV - 1),)), pltpu.SemaphoreType.DMA((k * (N_DEV - 1),)),
            pltpu.SemaphoreType.DMA((k,))]


def _exch_copies(ins, outs, sems, scatter):
    send_sems, recv_sems, local_sems = sems
    x, y, c = lax.axis_index("x"), lax.axis_index("y"), lax.axis_index("c")
    me = 4 * x + 2 * y + c
    owns, sends, recvs = [], [], []
    for a in range(len(ins)):
        owns.append(pltpu.make_async_copy(ins[a].at[me] if scatter else ins[a], outs[a].at[me], local_sems.at[a]))
        for r in range(1, N_DEV):
            fx, fy, fc = (r >> 2) & 1, (r >> 1) & 1, r & 1
            px, py, pc = (x + fx) % 2, (y + fy) % 2, (c + fc) % 2
            peer = 4 * px + 2 * py + pc
            s = a * (N_DEV - 1) + r - 1
            mk = functools.partial(pltpu.make_async_remote_copy, src_ref=ins[a].at[peer] if scatter else ins[a],
                                   send_sem=send_sems.at[s], recv_sem=recv_sems.at[s],
                                   device_id=(px, py, pc), device_id_type=pl.DeviceIdType.MESH)
            sends.append(mk(dst_ref=outs[a].at[me]))
            recvs.append(mk(dst_ref=outs[a].at[peer]))
    return owns, sends, recvs


def _exch_start(ins, outs, sems, scatter):
    owns, sends, _ = _exch_copies(ins, outs, sems, scatter)
    for cp in owns + sends:
        cp.start()


def _exch_wait(ins, outs, sems, scatter):
    owns, sends, recvs = _exch_copies(ins, outs, sems, scatter)
    for rc in recvs:
        rc.wait_recv()
    for cp in sends:
        cp.wait_send()
    for own in owns:
        own.wait()


MOD_ROWS = 48


def _mod_tile(cc, w, b):
    s = cc * jax.nn.sigmoid(cc)
    return lax.dot_general(s.astype(BF16), w.astype(BF16), (((1,), (0,)), ((), ())), preferred_element_type=F32) + b


def _mod_fwd(cc, w_mod, b_shard, *, name):
    L, d, wcols = w_mod.shape

    def body(cc_ref, w_ref, b_ref, o_ref):
        o_ref[...] = _mod_tile(cc_ref[...], w_ref[...], b_ref[...])

    return pl.pallas_call(
        body, name=name, grid=(L,),
        in_specs=[pl.BlockSpec((MOD_ROWS, d), lambda l: (0, 0)), pl.BlockSpec((None, d, wcols), lambda l: (l, 0, 0)),
                  pl.BlockSpec((None, 1, wcols), lambda l: (l, 0, 0))],
        out_specs=pl.BlockSpec((None, MOD_ROWS, wcols), lambda l: (l, 0, 0)),
        out_shape=jax.ShapeDtypeStruct((L, MOD_ROWS, wcols), F32),
        compiler_params=_params("parallel"),
    )(cc, w_mod, b_shard)


def _mod_bwd(cc, w_mod, b_shard, dm, *, name):
    L, d, wcols = w_mod.shape

    def body(cc_ref, w_ref, b_ref, dm_ref, dcc_ref, dw_ref):
        _, vjp = jax.vjp(_mod_tile, cc_ref[...], w_ref[...], b_ref[...])
        dcc, dw, _ = vjp(dm_ref[...])
        dw_ref[...] = dw

        @pl.when(pl.program_id(0) == 0)
        def _():
            dcc_ref[...] = dcc

        @pl.when(pl.program_id(0) > 0)
        def _():
            dcc_ref[...] += dcc

    return pl.pallas_call(
        body, name=name, grid=(L,),
        in_specs=[pl.BlockSpec((MOD_ROWS, d), lambda l: (0, 0)), pl.BlockSpec((None, d, wcols), lambda l: (l, 0, 0)),
                  pl.BlockSpec((None, 1, wcols), lambda l: (l, 0, 0)), pl.BlockSpec((None, MOD_ROWS, wcols), lambda l: (l, 0, 0))],
        out_specs=[pl.BlockSpec((MOD_ROWS, d), lambda l: (0, 0)), pl.BlockSpec((None, d, wcols), lambda l: (l, 0, 0))],
        out_shape=[jax.ShapeDtypeStruct((MOD_ROWS, d), F32), jax.ShapeDtypeStruct((L, d, wcols), F32)],
        compiler_params=_params("arbitrary"),
    )(cc, w_mod, b_shard, dm)


def _sum_leading(a, *, name):
    K, R, C = a.shape
    tr = _tile8(R, 256)

    def body(a_ref, o_ref):
        acc = a_ref[0].astype(F32)
        for i in range(1, K):
            acc = acc + a_ref[i].astype(F32)
        o_ref[...] = acc

    return pl.pallas_call(
        body, name=name, grid=(R // tr,),
        in_specs=[pl.BlockSpec((K, tr, C), lambda i: (0, i, 0))],
        out_specs=pl.BlockSpec((tr, C), lambda i: (i, 0)),
        out_shape=jax.ShapeDtypeStruct((R, C), F32),
        compiler_params=_params("parallel"),
    )(a)


def _tile8(dim, target):
    if dim <= target:
        return dim
    t = (target // 8) * 8
    while t >= 8:
        if dim % t == 0:
            return t
        t -= 8
    raise ValueError(f"no row tile for {dim}")


def _adamw_math(g, w, m, v):
    m = ADAM_B1 * m + (1.0 - ADAM_B1) * g
    v = ADAM_B2 * v + (1.0 - ADAM_B2) * jnp.square(g)
    m_hat = m / (1.0 - ADAM_B1 ** ADAM_STEP)
    v_hat = v / (1.0 - ADAM_B2 ** ADAM_STEP)
    delta = -ADAM_LR * (m_hat / (jnp.sqrt(v_hat) + ADAM_EPS) + ADAM_WD * w)
    return delta, m, v


def _adamw(g, w, m, v, *, name):
    L, R, C = w.shape
    parts = isinstance(g, (list, tuple))
    gs = list(g) if parts else [g]
    ng = len(gs)
    tr = _tile8(R, 256)

    def body(*refs):
        g_refs = refs[:ng]
        w_ref, m_ref, v_ref, go_ref, d_ref, mo_ref, vo_ref = refs[ng:]
        if parts:
            layer = pl.program_id(0)
            gv = None
            for li, g_ref in enumerate(g_refs):
                acc = g_ref[0].astype(F32)
                for i in range(1, N_DEV):
                    acc = acc + g_ref[i].astype(F32)
                gv = acc if gv is None else jnp.where(layer == li, acc, gv)
        else:
            gv = g_refs[0][...]
        go_ref[...] = gv
        d_ref[...], mo_ref[...], vo_ref[...] = _adamw_math(gv, w_ref[...], m_ref[...], v_ref[...])

    tile = pl.BlockSpec((None, tr, C), lambda l, i: (l, i, 0))
    g_specs = [pl.BlockSpec((N_DEV, tr, C), lambda l, i: (0, i, 0))] * ng if parts else [tile]
    return pl.pallas_call(
        body, name=name, grid=(L, R // tr),
        in_specs=g_specs + [tile, tile, tile], out_specs=[tile] * 4,
        out_shape=[jax.ShapeDtypeStruct((L, R, C), F32)] * 4,
        compiler_params=_params("parallel", "parallel"),
    )(*gs, w, m, v)


def _adamw_small(gs, ws, ms, vs, *, name):
    k = len(ws)

    def body(*refs):
        g_refs, w_refs, m_refs, v_refs = refs[:k], refs[k:2 * k], refs[2 * k:3 * k], refs[3 * k:4 * k]
        d_refs, mo_refs, vo_refs = refs[4 * k:5 * k], refs[5 * k:6 * k], refs[6 * k:]
        for i in range(k):
            d_refs[i][...], mo_refs[i][...], vo_refs[i][...] = _adamw_math(g_refs[i][...], w_refs[i][...], m_refs[i][...],
                                                                         v_refs[i][...])

    shapes = [jax.ShapeDtypeStruct(a.shape, F32) for a in ws]
    out = pl.pallas_call(body, name=name, out_shape=shapes * 3, compiler_params=pltpu.CompilerParams(vmem_limit_bytes=VMEM_LIMIT))(
        *gs, *ws, *ms, *vs)
    return out[:k], out[k:2 * k], out[2 * k:]


def _block_diag(w):
    g, c, d = w.shape
    return (w[:, :, None, :] * jnp.eye(g, dtype=w.dtype)[:, None, :, None]).reshape(g * c, g * d)


def _diag_blocks(w_bd):
    g = POOL_W // POOL_GROUP
    return jnp.stack([w_bd[i * POOL_GROUP:(i + 1) * POOL_GROUP, i * POOL_GROUP:(i + 1) * POOL_GROUP] for i in range(g)])


def _flat(a):
    return a.reshape(-1, a.shape[-1])


def _mix_half_fwd(tag, st, mods, wl, kvc, *, local, kv_only, exch=None):
    sh1, sc1, g1 = mods[:3]
    B, n, d = st[0].shape
    x, h, h_t = _norm_fwd(f"{tag}_norm1", st, wl["n1"], sc1, sh1)
    if kv_only:
        kv = _mm(_flat(h), wl["w_in"][:, ATTN_W:ATTN_W + 2 * KV_W], name=f"{tag}_kv").reshape(B, n, 2 * KV_W)
        return None, dict(st=st, h_t=h_t, kvc=kv), []
    u = _mm(_flat(h), wl["w_in"], name=f"{tag}_in", tn=IN_W).reshape(B, n, IN_W)
    if not local:
        kvc = u[:, :, ATTN_W:ATTN_W + 2 * KV_W]
    mix, got = _mixer_fwd(f"{tag}_mix", u, kvc, wl["margs"], local, exch)
    y = _mm(_flat(mix), wl["w_out"], name=f"{tag}_out", tn=D_MODEL).reshape(B, n, d)
    return (x, y, g1), dict(st=st, h_t=h_t, u=u, kvc=kvc, mix=mix), got


def _ffn_half_fwd(tag, st2, mods, wl, exch=None):
    sh2, sc2, g2 = mods[3:]
    B, n, d = st2[0].shape
    x1, h2, h2_t = _norm_fwd(f"{tag}_norm2", st2, wl["n2"], sc2, sh2)
    gu, act, act_t, *got = _mm_swiglu(_flat(h2), wl["w_ffn_in"], name=f"{tag}_ffn_in", exch=exch)
    y2 = _mm(act, wl["w_ffn_out"], name=f"{tag}_ffn_out", tn=D_MODEL).reshape(B, n, d)
    return (x1, y2, g2), dict(st2=st2, h2_t=h2_t, gu=gu, act_t=act_t), got


def _ffn_half_bwd(tag, sv, mods, wl, dx1, dy2):
    sh2, sc2, _ = mods[3:]
    B, n, d = sv["st2"][0].shape
    gw = {}
    dy2f = _flat(dy2)
    gw["w_ffn_out"] = _mm(sv["act_t"], dy2f, out_dtype=BF16, name=f"{tag}_ffn_out_dw")
    dgu = _mm_dswiglu(dy2f, wl["w_ffn_out"], sv["gu"], name=f"{tag}_ffn_out_dx")
    dh2 = _mm(dgu, wl["w_ffn_in"], trans_b=True, name=f"{tag}_ffn_in_dx").reshape(B, n, d)
    gw["w_ffn_in"] = _mm(sv["h2_t"], dgu, out_dtype=BF16, tn=FF_TILE, out_block=_natural_block, name=f"{tag}_ffn_in_dw")
    dx, dy, dg1, gw["n2"], dsc2, dsh2 = _norm_bwd(f"{tag}_norm2", sv["st2"], wl["n2"], sc2, sh2, dx1, dh2)
    return (dx, dy, dg1), gw, dict(sh2=dsh2, sc2=dsc2)


def _mix_half_bwd(tag, sv, mods, wl, dx, dy, dkv_in, *, local, kv_only, exch=None):
    sh1, sc1, _ = mods[:3]
    B, n, d = sv["st"][0].shape
    gw = {}
    if kv_only:
        dkv = _flat(dkv_in).astype(BF16)
        dh = _mm(dkv, wl["w_in"][:, ATTN_W:ATTN_W + 2 * KV_W], trans_b=True, name=f"{tag}_kv_dx").reshape(B, n, d)
        gw["w_in_kv"] = _mm(sv["h_t"], dkv, out_dtype=BF16, name=f"{tag}_kv_dw")
        dxb, dy_prev, dgate_prev, gw["n1"], dsc1, dsh1 = _norm_bwd(f"{tag}_norm1", sv["st"], wl["n1"], sc1, sh1,
                                                                    jnp.zeros((B, n, d), F32), dh)
        return (dxb, dy_prev, dgate_prev), gw, dict(sh1=dsh1, sc1=dsc1), None, []

    dyf = _flat(dy)
    dmix = _mm(dyf, wl["w_out"], trans_b=True, name=f"{tag}_out_dx", tn=D_MODEL).reshape(B, n, d)
    gw["w_out"] = _mm(_flat(sv["mix"]).T, dyf, out_dtype=BF16, name=f"{tag}_out_dw")
    (dq, dk, dv, da, dg, dpu), (dkc, dvc), gw["margs"], got = _mixer_bwd(f"{tag}_mix", sv["u"], sv["kvc"], wl["margs"], dmix,
                                                                     local, exch)
    if local:
        dkv_out = jnp.concatenate([dkc, dvc], axis=-1)
    else:
        dk = dkc + dkv_in[:, :, :KV_W]
        dv = dvc + dkv_in[:, :, KV_W:]
        dkv_out = None
    du = _flat(jnp.concatenate([dq, dk, dv, da, dg, dpu], axis=-1).astype(BF16))
    dh = _mm(du, wl["w_in"], trans_b=True, name=f"{tag}_in_dx", tn=D_MODEL).reshape(B, n, d)
    gw["w_in"] = _mm(sv["h_t"], du, out_dtype=BF16, name=f"{tag}_in_dw")
    dxb, dy_prev, dgate_prev, gw["n1"], dsc1, dsh1 = _norm_bwd(f"{tag}_norm1", sv["st"], wl["n1"], sc1, sh1, dx, dh)
    return (dxb, dy_prev, dgate_prev), gw, dict(sh1=dsh1, sc1=dsc1), dkv_out, got


BIG_W = ("w_in", "w_out", "w_ffn_in", "w_ffn_out")


def _local_step(x, ctx, m_loc, m_ctx, p, final_g, target, big):
    B = x.shape[0]
    depth = m_loc.shape[0]
    lat_mods = [[t[:, None, :] for t in jnp.split(m_loc[l], 6, axis=-1)] for l in range(depth)]
    ctx_mods = [[jnp.broadcast_to(t[None, None, :], (B, 1, D_MODEL)) for t in jnp.split(m_ctx[l], 6)] for l in range(depth)]

    st, cst = (x, None, None), (ctx, None, None)
    w_mix, w_ffn, sv_mix, sv_ffn, csv_mix, csv_ffn = [], [], [], [], [], []
    got = []
    for l in range(depth):
        last = l == depth - 1
        wb = big.mix_weights(l, got)
        wm = dict(n1=p["norm1_g"][l][None, :], w_in=wb["w_in"], w_out=wb["w_out"],
                  margs=(p["attn_sink"][l][None, :], p["conv_dw"][l], p["conv_dw_b"][l][None, :], p["conv_ln_g"][l][None, :],
                         p["conv_ln_b"][l][None, :], _block_diag(p["pool_w"][l]), p["pool_scale"][l][None, :]))
        cst, csv, _ = _mix_half_fwd(f"l{l}c", cst, ctx_mods[l], wm, None, local=False, kv_only=last)
        st, sv, got = _mix_half_fwd(f"l{l}", st, lat_mods[l], wm, csv["kvc"], local=True, kv_only=False,
                                    exch=big.ride_attn_fwd(l))
        w_mix.append(wm)
        sv_mix.append(sv)
        csv_mix.append(csv)
        wb = big.ffn_weights(l, got)
        w_ffn_in = _interleave_ffn(wb["w_ffn_in"], name=f"l{l}_ffn_in_interleave")
        wf = dict(n2=p["norm2_g"][l][None, :], w_ffn_in=w_ffn_in, w_ffn_out=wb["w_ffn_out"])
        csv = None
        if not last:
            cst, csv, _ = _ffn_half_fwd(f"l{l}c", cst, ctx_mods[l], wf)
        st, sv, got = _ffn_half_fwd(f"l{l}", st, lat_mods[l], wf, exch=big.ride_ffn_fwd(l))
        w_ffn.append(wf)
        sv_ffn.append(sv)
        csv_ffn.append(csv)
    loss_rows, dx, dy, dgate, dfinal = _loss_head(st, final_g[None, :], target, name="loss_head")

    dm_loc, dm_ctx = [None] * depth, [None] * depth
    small = [None] * depth
    cdx = cdy = cdgate = None
    up_mix = None
    for l in reversed(range(depth)):
        last = l == depth - 1
        dm, cdm = dict(g2=dgate), {}
        (dx, dy, dm["g1"]), gf, d = _ffn_half_bwd(f"l{l}", sv_ffn[l], lat_mods[l], w_ffn[l], dx, dy)
        dm.update(d)
        if not last:
            cdm["g2"] = cdgate
            (cdx, cdy, cdm["g1"]), cgf, d = _ffn_half_bwd(f"l{l}c", csv_ffn[l], ctx_mods[l], w_ffn[l], cdx, cdy)
            cdm.update(d)
            gf = {k: gf[k] + cgf[k] for k in gf}
        ffn_grads = {k: gf[k] for k in _ShardedWeights.FFN}
        (dx, dy, dgate), gm, d, dkv, got = _mix_half_bwd(f"l{l}", sv_mix[l], lat_mods[l], w_mix[l], dx, dy, None, local=True,
                                                        kv_only=False, exch=big.ride_attn_bwd(l, ffn_grads, up_mix))
        big.took(l, ffn_grads, up_mix, got)
        dm.update(d)
        (cdx, cdy, cdgate), cgm, d, _, _ = _mix_half_bwd(f"l{l}c", csv_mix[l], ctx_mods[l], w_mix[l], cdx, cdy, dkv,
                                                        local=False, kv_only=last)
        cdm.update(d)
        order = ("sh1", "sc1", "g1", "sh2", "sc2", "g2")
        dm_loc[l] = jnp.concatenate([dm[k][:, 0, :] for k in order], axis=-1)
        dm_ctx[l] = jnp.concatenate([jnp.sum(cdm[k][:, 0, :], axis=0) if k in cdm else jnp.zeros((D_MODEL,), F32)
                                     for k in order])
        if last:
            up_mix = dict(w_in=gm["w_in"].at[:, ATTN_W:ATTN_W + 2 * KV_W].add(cgm["w_in_kv"]), w_out=gm["w_out"])
            margs = gm["margs"]
        else:
            up_mix = {k: gm[k] + cgm[k] for k in _ShardedWeights.MIX}
            margs = tuple(a + b for a, b in zip(gm["margs"], cgm["margs"]))
        small[l] = dict(n1=gm["n1"] + cgm["n1"], n2=gf["n2"], margs=margs)
    big.leftover(up_mix)

    stack = lambda f: jnp.stack([f(small[l]) for l in range(depth)])
    dp = dict(
        norm1_g=stack(lambda g: g["n1"][0]), norm2_g=stack(lambda g: g["n2"][0]),
        attn_sink=stack(lambda g: g["margs"][0][0]), conv_dw=stack(lambda g: g["margs"][1]),
        conv_dw_b=stack(lambda g: g["margs"][2][0]), conv_ln_g=stack(lambda g: g["margs"][3][0]),
        conv_ln_b=stack(lambda g: g["margs"][4][0]), pool_w=stack(lambda g: _diag_blocks(g["margs"][5])),
        pool_scale=stack(lambda g: g["margs"][6][0]))
    return jnp.sum(loss_rows[:, 0, 0]), dx, jnp.stack(dm_loc), jnp.stack(dm_ctx), dp, dfinal[0]


PACK_COLS = 1024


def _pack(arrs):
    flat = jnp.concatenate([a.reshape(-1).astype(F32) for a in arrs])
    rows = -(-flat.shape[0] // (8 * PACK_COLS)) * 8
    return jnp.pad(flat, (0, rows * PACK_COLS - flat.shape[0])).reshape(rows, PACK_COLS)


def _unpack(slab, like):
    flat = slab.reshape(-1)
    out, off = [], 0
    for a in like:
        out.append(flat[off:off + a.size].reshape(a.shape))
        off += a.size
    return out


def _shard_cols(gathered):
    _, L, R, C = gathered.shape
    return jnp.transpose(gathered, (1, 2, 0, 3)).reshape(L, R, N_DEV * C)


class _ShardedWeights:
    MIX = ("w_in", "w_out")
    FFN = ("w_ffn_in", "w_ffn_out")
    BY_COLS = ("w_in", "w_ffn_in")

    def __init__(self, shards, first):
        self.shards = shards
        self.first = first
        self.depth = shards[BIG_W[0]].shape[0]
        self.parts = [dict() for _ in range(self.depth)]
        self.left = None

    def _join(self, names, blocks):
        out = {}
        for name, g in zip(names, blocks):
            _, R, C = g.shape
            out[name] = jnp.transpose(g, (1, 0, 2)).reshape(R, N_DEV * C) if name in self.BY_COLS else g.reshape(N_DEV * R, C)
        return out

    def cut(self, names, grads):
        out = []
        for name in names:
            g = grads[name]
            if name in self.BY_COLS:
                R, C8 = g.shape
                out.append(jnp.transpose(g.reshape(R, N_DEV, C8 // N_DEV), (1, 0, 2)))
            else:
                R8, C = g.shape
                out.append(g.reshape(N_DEV, R8 // N_DEV, C))
        return out

    def mix_weights(self, l, got):
        return self._join(self.MIX, self.first if l == 0 else got)

    def ffn_weights(self, l, got):
        return self._join(self.FFN, got)

    def ride_attn_fwd(self, l):
        return [self.shards[name][l] for name in self.FFN], False

    def ride_ffn_fwd(self, l):
        if l + 1 >= self.depth:
            return None
        return [self.shards[name][l + 1] for name in self.MIX], False

    def ride_attn_bwd(self, l, ffn_grads, up_mix):
        return self.cut(self.FFN, ffn_grads) + (self.cut(self.MIX, up_mix) if up_mix is not None else []), True

    def took(self, l, ffn_grads, up_mix, got):
        self.parts[l].update(zip(self.FFN, got[:2]))
        if up_mix is not None:
            self.parts[l + 1].update(zip(self.MIX, got[2:]))

    def leftover(self, mix_grads):
        self.left = mix_grads


def _as_rows(a, leading=0):
    return a.reshape(*a.shape[:leading], -1, PACK_COLS)


SMALL = ("c_ctx", "b_mod", "norm1_g", "norm2_g", "conv_dw_b", "conv_ln_g", "conv_ln_b", "attn_sink", "pool_w",
         "pool_scale", "final_g", "conv_dw")
BIG = ("w_mod", "w_in", "w_out", "w_ffn_in", "w_ffn_out")
ORDER = ("c_ctx", "w_mod", "b_mod", "norm1_g", "norm2_g", "w_in", "conv_dw", "conv_dw_b", "conv_ln_g", "conv_ln_b",
         "attn_sink", "pool_w", "pool_scale", "w_out", "w_ffn_in", "w_ffn_out", "final_g")


def kernel(x, c, ctx, c_ctx, w_mod, b_mod, norm1_g, norm2_g, w_in, conv_dw, conv_dw_b, conv_ln_g, conv_ln_b, attn_sink, pool_w, pool_scale, w_out, w_ffn_in, w_ffn_out, final_g, loss_target, m_c_ctx, m_w_mod, m_b_mod, m_norm1_g, m_norm2_g, m_w_in, m_conv_dw, m_conv_dw_b, m_conv_ln_g, m_conv_ln_b, m_attn_sink, m_pool_w, m_pool_scale, m_w_out, m_w_ffn_in, m_w_ffn_out, m_final_g, v_c_ctx, v_w_mod, v_b_mod, v_norm1_g, v_norm2_g, v_w_in, v_conv_dw, v_conv_dw_b, v_conv_ln_g, v_conv_ln_b, v_attn_sink, v_pool_w, v_pool_scale, v_w_out, v_w_ffn_in, v_w_ffn_out, v_final_g):
    w = dict(c_ctx=c_ctx, w_mod=w_mod, b_mod=b_mod, norm1_g=norm1_g, norm2_g=norm2_g, w_in=w_in, conv_dw=conv_dw,
             conv_dw_b=conv_dw_b, conv_ln_g=conv_ln_g, conv_ln_b=conv_ln_b, attn_sink=attn_sink, pool_w=pool_w,
             pool_scale=pool_scale, w_out=w_out, w_ffn_in=w_ffn_in, w_ffn_out=w_ffn_out, final_g=final_g)
    mom = dict(c_ctx=m_c_ctx, w_mod=m_w_mod, b_mod=m_b_mod, norm1_g=m_norm1_g, norm2_g=m_norm2_g, w_in=m_w_in,
               conv_dw=m_conv_dw, conv_dw_b=m_conv_dw_b, conv_ln_g=m_conv_ln_g, conv_ln_b=m_conv_ln_b,
               attn_sink=m_attn_sink, pool_w=m_pool_w, pool_scale=m_pool_scale, w_out=m_w_out, w_ffn_in=m_w_ffn_in,
               w_ffn_out=m_w_ffn_out, final_g=m_final_g)
    var = dict(c_ctx=v_c_ctx, w_mod=v_w_mod, b_mod=v_b_mod, norm1_g=v_norm1_g, norm2_g=v_norm2_g, w_in=v_w_in,
               conv_dw=v_conv_dw, conv_dw_b=v_conv_dw_b, conv_ln_g=v_conv_ln_g, conv_ln_b=v_conv_ln_b,
               attn_sink=v_attn_sink, pool_w=v_pool_w, pool_scale=v_pool_scale, w_out=v_w_out, w_ffn_in=v_w_ffn_in,
               w_ffn_out=v_w_ffn_out, final_g=v_final_g)
    B = x.shape[0]
    depth = w_mod.shape[0]
    mod_cols = w_mod.shape[2]
    dw_cols = conv_dw.shape[2]
    me = 4 * lax.axis_index("x") + 2 * lax.axis_index("y") + lax.axis_index("c")

    shards = {name: w[name].astype(BF16) for name in BIG_W}
    c_all, dw_all, *first = _exchange([c, conv_dw] + [shards[name][0] for name in _ShardedWeights.MIX], scatter=False,
                                      name="gather_first")
    big = _ShardedWeights(shards, first)
    p = dict(norm1_g=norm1_g, norm2_g=norm2_g, conv_dw=_shard_cols(dw_all), conv_dw_b=conv_dw_b, conv_ln_g=conv_ln_g,
             conv_ln_b=conv_ln_b, attn_sink=attn_sink, pool_w=pool_w, pool_scale=pool_scale)

    cc = jnp.concatenate([c_all.reshape(N_DEV * B, D_MODEL), jnp.broadcast_to(c_ctx[None, :], (N_DEV, D_MODEL)),
                          jnp.zeros((MOD_ROWS - N_DEV * B - N_DEV, D_MODEL), F32)], axis=0)
    b_shard = lax.dynamic_slice_in_dim(b_mod, me * mod_cols, mod_cols, axis=1)[:, None, :]
    m_part = _mod_fwd(cc, w_mod, b_shard, name="mod_fwd")
    m_all, = _exchange([m_part], scatter=False, name="gather_mod")
    m_full = _shard_cols(m_all)
    m_loc = lax.dynamic_slice_in_dim(m_full, me * B, B, axis=1)
    m_ctx = m_full[:, N_DEV * B, :]

    loss_part, dx, dm_loc, dm_ctx, dp, dfinal = _local_step(x, ctx, m_loc, m_ctx, p, final_g, loss_target, big)
    loss = lax.psum(loss_part, AXES)

    dm_rows = jnp.concatenate([dm_loc, dm_ctx[:, None, :], jnp.zeros((depth, 8 - B - 1, 6 * D_MODEL), F32)], axis=1)
    dm_all, = _exchange([dm_rows], scatter=False, name="gather_dmod")
    dm_full = jnp.concatenate([
        jnp.transpose(dm_all[:, :, :B, :], (1, 0, 2, 3)).reshape(depth, N_DEV * B, 6 * D_MODEL),
        jnp.transpose(dm_all[:, :, B, :], (1, 0, 2)),
        jnp.zeros((depth, MOD_ROWS - N_DEV * B - N_DEV, 6 * D_MODEL), F32)], axis=1)
    g_b_mod = jnp.stack([_sum_leading(dm_full[l][:, None, :], name=f"b_mod_grad{l}")[0] for l in range(depth)])
    dm_mine = lax.dynamic_slice_in_dim(dm_full, me * mod_cols, mod_cols, axis=2)
    dcc, g_w_mod = _mod_bwd(cc, w_mod, b_shard, dm_mine, name="mod_bwd")
    g_c_ctx_part = jnp.sum(dcc[N_DEV * B:N_DEV * B + N_DEV], axis=0)

    small_like = [c_ctx, norm1_g, norm2_g, conv_dw_b, conv_ln_g, conv_ln_b, attn_sink, pool_w, pool_scale, final_g,
                  dp["conv_dw"]]
    small_part = _pack([g_c_ctx_part, dp["norm1_g"], dp["norm2_g"], dp["conv_dw_b"], dp["conv_ln_g"], dp["conv_ln_b"],
                        dp["attn_sink"], dp["pool_w"], dp["pool_scale"], dfinal, dp["conv_dw"]])
    small_all, = _exchange([small_part], scatter=False, name="gather_small")
    small_sum = _unpack(_sum_leading(small_all, name="sum_small"), small_like)
    g = dict(zip(("c_ctx", "norm1_g", "norm2_g", "conv_dw_b", "conv_ln_g", "conv_ln_b", "attn_sink", "pool_w",
                  "pool_scale", "final_g"), small_sum[:-1]))
    g["b_mod"] = g_b_mod
    g["conv_dw"] = lax.dynamic_slice_in_dim(small_sum[-1], me * dw_cols, dw_cols, axis=2)

    big.parts[0].update(zip(big.MIX, _exchange(big.cut(big.MIX, big.left), scatter=True, name="scatter_last")))

    delta, new_m, new_v = {}, {}, {}
    for name in BIG_W:
        g[name], delta[name], new_m[name], new_v[name] = _adamw(
            [big.parts[l][name] for l in range(depth)], w[name], mom[name], var[name], name=f"adamw_{name}")
    g["w_mod"], delta["w_mod"], new_m["w_mod"], new_v["w_mod"] = _adamw(g_w_mod, w_mod, m_w_mod, v_w_mod, name="adamw_w_mod")
    res = _adamw_small([g[k] for k in SMALL], [w[k] for k in SMALL], [mom[k] for k in SMALL], [var[k] for k in SMALL],
                       name="adamw_small")
    for dst, arrs in zip((delta, new_m, new_v), res):
        dst.update(zip(SMALL, arrs))

    return (loss, dx, *[g[k] for k in ORDER], *[delta[k] for k in ORDER], *[new_m[k] for k in ORDER],
            *[new_v[k] for k in ORDER])
```

```python
import functools

import numpy as np
import jax
import jax.numpy as jnp
from jax import lax
from jax.experimental import pallas as pl
from jax.experimental.pallas import tpu as pltpu

F32 = jnp.float32
BF16 = jnp.bfloat16

D_MODEL = 1024
GRID_W = 64
HEAD_DIM = 64
ATTN_W = 512
CONV_W = 256
POOL_W = 256
ATTN_HEADS = 8
KV_HEADS = 2
GROUP = ATTN_HEADS // KV_HEADS
KV_W = KV_HEADS * HEAD_DIM
IN_W = ATTN_W + 2 * KV_W + 2 * CONV_W + POOL_W
WINDOW = 128
Q_BLOCK = 128
SPAN = Q_BLOCK + 2 * WINDOW
CONV_KERNEL = 31
POOL_WINDOWS = (2, 4, 8, 16)
POOL_GROUP = 64
ROPE_BASE = 10000.0
D_FF = 2816
EPS = 1e-6
NEG = -1e30
N_DEV = 8
AXES = ("x", "y", "c")

ADAM_LR = 0.001
ADAM_B1 = 0.9
ADAM_B2 = 0.999
ADAM_EPS = 1e-08
ADAM_WD = 0.01
ADAM_STEP = 10

VMEM_LIMIT = 56 * 1024 * 1024
HALO = 16
SEQ_CHUNK = 512
CONV_BWD_CHUNK = 256


def _params(*sem):
    return pltpu.CompilerParams(dimension_semantics=sem, vmem_limit_bytes=VMEM_LIMIT)


def _tile(dim, target):
    if dim <= target:
        return dim
    t = (target // 128) * 128
    while t >= 128:
        if dim % t == 0:
            return t
        t -= 128
    raise ValueError(f"no tile for {dim}")


MM_VMEM_BUDGET = 44 * 1024 * 1024
DW_TN = 256


def _dot(a, b):
    return lax.dot_general(a.astype(BF16), b.astype(BF16), (((1,), (0,)), ((), ())), preferred_element_type=F32)


def _mm_vmem(tm, tn, tk, whole, a_bytes, b_bytes, o_bytes):
    return 2 * (tm * tk * a_bytes + tk * tn * b_bytes + tm * tn * o_bytes) + (0 if whole else tm * tn * 4)


def _mm(a, b, *, name, out_dtype=F32, tm=1408, tn=512, trans_b=False, out_block=None):
    M, K = a.shape
    N, K2 = b.shape if trans_b else b.shape[::-1]
    assert K == K2, (a.shape, b.shape)
    tm = _tile(M, tm)
    tn = _tile(N, tn)
    sizes = (a.dtype.itemsize, b.dtype.itemsize, jnp.dtype(out_dtype).itemsize)
    tk = next(t for t in range(K, 0, -128) if K % t == 0 and _mm_vmem(tm, tn, t, t == K, *sizes) <= MM_VMEM_BUDGET)
    nk = K // tk

    def body(a_ref, b_ref, o_ref, *scratch):
        part = (_dot_nt if trans_b else _dot)(a_ref[...], b_ref[...])
        if nk == 1:
            o_ref[...] = part.astype(o_ref.dtype)
        else:
            acc_ref, = scratch
            k = pl.program_id(2)

            @pl.when(k == 0)
            def _():
                acc_ref[...] = part

            @pl.when(k > 0)
            def _():
                acc_ref[...] += part

            @pl.when(k == nk - 1)
            def _():
                o_ref[...] = acc_ref[...].astype(o_ref.dtype)

    return pl.pallas_call(
        body, name=name, grid=(M // tm, N // tn, nk),
        in_specs=[pl.BlockSpec((tm, tk), lambda i, j, k: (i, k)),
                  pl.BlockSpec((tn, tk), lambda i, j, k: (j, k)) if trans_b else pl.BlockSpec((tk, tn), lambda i, j, k: (k, j))],
        out_specs=pl.BlockSpec((tm, tn), (lambda i, j, k: (i, j)) if out_block is None else (lambda i, j, k: (i, out_block(j)))),
        out_shape=jax.ShapeDtypeStruct((M, N), out_dtype),
        scratch_shapes=[pltpu.VMEM((tm, tn), F32)] if nk > 1 else [],
        compiler_params=_params("parallel", "parallel", "arbitrary"),
    )(a, b)


FF_TILE = 256


FF_TILES = D_FF // FF_TILE


def _natural_block(j):
    return (j % 2) * FF_TILES + j // 2


def _interleave_ffn(w, *, name):
    R, C = w.shape

    def body(w_ref, o_ref):
        o_ref[...] = w_ref[...]

    return pl.pallas_call(
        body, name=name, grid=(2 * FF_TILES,),
        in_specs=[pl.BlockSpec((R, FF_TILE), lambda j: (0, _natural_block(j)))],
        out_specs=pl.BlockSpec((R, FF_TILE), lambda j: (0, j)),
        out_shape=jax.ShapeDtypeStruct((R, C), w.dtype),
        compiler_params=_params("parallel"),
    )(w)


def _swiglu(gu):
    g, u = gu[:, :FF_TILE], gu[:, FF_TILE:]
    return g * jax.nn.sigmoid(g) * u


EPILOGUE_SPLIT = 1
FF_ROWS = 4096


def _mm_swiglu(a, w_il, *, name, tm=FF_ROWS, exch=None, split=EPILOGUE_SPLIT):
    M, K = a.shape
    tm = _tile(M, tm)
    rc = tm // split

    def body(a_ref, b_ref, gu_ref, act_ref, act_t_ref):
        b = b_ref[...]
        parts = [_dot(a_ref[r0:r0 + rc, :], b) for r0 in range(0, tm, rc)]
        for r0, gu in zip(range(0, tm, rc), parts):
            gu_ref[r0:r0 + rc, :] = gu.astype(gu_ref.dtype)
            act = _swiglu(gu)
            act_ref[r0:r0 + rc, :] = act.astype(act_ref.dtype)
            act_t_ref[:, r0:r0 + rc] = act.T.astype(act_t_ref.dtype)

    grid = (M // tm, D_FF // FF_TILE)
    body, x_in, x_out, x_shapes, x_sems = _riding(body, exch, 2, 3, grid)
    return pl.pallas_call(
        body, name=name, grid=grid,
        in_specs=[pl.BlockSpec((tm, K), lambda i, j: (i, 0)), pl.BlockSpec((K, 2 * FF_TILE), lambda i, j: (0, j))] + x_in,
        out_specs=[pl.BlockSpec((tm, 2 * FF_TILE), lambda i, j: (i, j)), pl.BlockSpec((tm, FF_TILE), lambda i, j: (i, j)),
                   pl.BlockSpec((FF_TILE, tm), lambda i, j: (j, i))] + x_out,
        out_shape=[jax.ShapeDtypeStruct((M, 2 * D_FF), BF16), jax.ShapeDtypeStruct((M, D_FF), BF16),
                   jax.ShapeDtypeStruct((D_FF, M), BF16)] + x_shapes,
        scratch_shapes=x_sems,
        compiler_params=_params("arbitrary", "arbitrary") if exch else _params("parallel", "parallel"),
    )(a, w_il, *(exch[0] if exch else []))


def _mm_dswiglu(dy, w_out, gu, *, name, tm=FF_ROWS, split=EPILOGUE_SPLIT):
    M, K = dy.shape
    tm = _tile(M, tm)
    rc = tm // split

    def body(dy_ref, b_ref, gu_ref, o_ref):
        b = b_ref[...]
        parts = [_dot_nt(dy_ref[r0:r0 + rc, :], b) for r0 in range(0, tm, rc)]
        for r0, dact in zip(range(0, tm, rc), parts):
            g = gu_ref[r0:r0 + rc, :FF_TILE].astype(F32)
            u = gu_ref[r0:r0 + rc, FF_TILE:].astype(F32)
            sig = jax.nn.sigmoid(g)
            silu = g * sig
            o_ref[r0:r0 + rc, :FF_TILE] = (dact * u * (sig + silu * (1.0 - sig))).astype(o_ref.dtype)
            o_ref[r0:r0 + rc, FF_TILE:] = (dact * silu).astype(o_ref.dtype)

    return pl.pallas_call(
        body, name=name, grid=(M // tm, D_FF // FF_TILE),
        in_specs=[pl.BlockSpec((tm, K), lambda i, j: (i, 0)), pl.BlockSpec((FF_TILE, K), lambda i, j: (j, 0)),
                  pl.BlockSpec((tm, 2 * FF_TILE), lambda i, j: (i, j))],
        out_specs=pl.BlockSpec((tm, 2 * FF_TILE), lambda i, j: (i, j)),
        out_shape=jax.ShapeDtypeStruct((M, 2 * D_FF), BF16),
        compiler_params=_params("parallel", "parallel"),
    )(dy, w_out, gu)


def _rope_tables(n):
    rows = n // GRID_W
    row = jnp.repeat(jnp.arange(rows), GRID_W).astype(F32)
    col = jnp.tile(jnp.arange(GRID_W), rows).astype(F32)
    half = HEAD_DIM // 2
    inv = ROPE_BASE ** (-jnp.arange(0, half, 2, dtype=F32) / half)
    ar = row[:, None] * inv
    ac = col[:, None] * inv
    ang = jnp.concatenate([ar, ar, ac, ac], axis=-1)
    return jnp.cos(ang), jnp.sin(ang)


def _rot_half(x):
    w = x.shape[-1]
    lane = lax.broadcasted_iota(jnp.int32, x.shape, 1)
    up = pltpu.roll(x, w - 16, 1)
    down = pltpu.roll(x, 16, 1)
    return jnp.where((lane & 16) == 0, -up, down)


def _rope(x, cos, sin):
    return x * cos + _rot_half(x) * sin


def _rope_bwd(d, cos, sin):
    return d * cos - _rot_half(d * sin)


def _dot_nt(a, b):
    return lax.dot_general(a.astype(BF16), b.astype(BF16), (((1,), (1,)), ((), ())), preferred_element_type=F32)


def _softmax_sink(s, sink_rows):
    mx = jnp.maximum(jnp.max(s, axis=1, keepdims=True), sink_rows)
    e = jnp.exp(s - mx)
    es = jnp.exp(sink_rows - mx)
    inv = 1.0 / (jnp.sum(e, axis=1, keepdims=True) + es)
    return e * inv, es * inv


def _attn_operands(q_ref, k_ref, v_ref, kc_ref, vc_ref, cq_ref, sq_ref, ck_ref, sk_ref, i, n, n_ctx, local):
    q = q_ref[...]
    k_all, v_all, bias, s0, ck, sk = kc_ref[...], vc_ref[...], None, None, None, None
    if local:
        start, s0 = _span_start(i, n)
        ck = ck_ref[pl.ds(s0, SPAN), :]
        sk = sk_ref[pl.ds(s0, SPAN), :]
        q = _rope(q, cq_ref[...], sq_ref[...])
        k_all = jnp.concatenate([k_all, _rope(k_ref[pl.ds(s0, SPAN), :], ck, sk)], axis=0)
        v_all = jnp.concatenate([v_all, v_ref[pl.ds(s0, SPAN), :]], axis=0)
        bias = _window_bias(start, s0, n_ctx)
    q = (q * (HEAD_DIM ** -0.5)).astype(BF16)
    return q, k_all.astype(BF16), v_all.astype(BF16), bias, s0, ck, sk


def _stack_heads(x, kh):
    return jnp.concatenate([x[:, (GROUP * kh + g) * HEAD_DIM:(GROUP * kh + g + 1) * HEAD_DIM] for g in range(GROUP)], axis=0)


def _sink_rows(sink, kh):
    return jnp.concatenate([jnp.broadcast_to(sink[:, GROUP * kh + g:GROUP * kh + g + 1], (Q_BLOCK, 1)) for g in range(GROUP)], axis=0)


def _window_bias(start, s0, n_ctx):
    r = lax.broadcasted_iota(jnp.int32, (Q_BLOCK, n_ctx + SPAN), 0)
    c = lax.broadcasted_iota(jnp.int32, (Q_BLOCK, n_ctx + SPAN), 1)
    ok = (c < n_ctx) | (jnp.abs(start - s0 + r - (c - n_ctx)) <= WINDOW)
    return jnp.concatenate([jnp.where(ok, 0.0, NEG).astype(F32)] * GROUP, axis=0)


def _span_start(i, n):
    start = i * Q_BLOCK
    s0 = jnp.clip(start - WINDOW, 0, n - SPAN)
    return start, pl.multiple_of(s0, Q_BLOCK)


def _riding(body, exch, n_in, n_out, grid):
    if exch is None:
        return body, [], [], [], []
    arrs, scatter = exch
    k = len(arrs)

    def wrapped(*refs):
        ins, xin = refs[:n_in], refs[n_in:n_in + k]
        outs, xout = refs[n_in + k:n_in + k + n_out], refs[n_in + k + n_out:n_in + 2 * k + n_out]
        sems = refs[n_in + 2 * k + n_out:]
        b, i = pl.program_id(0), pl.program_id(1)

        @pl.when((b == 0) & (i == 0))
        def _():
            _exch_start(xin, xout, sems, scatter)

        body(*ins, *outs)

        @pl.when((b == grid[0] - 1) & (i == grid[1] - 1))
        def _():
            _exch_wait(xin, xout, sems, scatter)

    any_spec = pl.BlockSpec(memory_space=pl.ANY)
    return wrapped, [any_spec] * k, [any_spec] * k, _exch_out_shapes(arrs, scatter), _exch_sems(k)


def _attn_fwd(u, kvc, sink, cos, sin, *, local, name, exch=None):
    B, n, _ = u.shape
    n_ctx = kvc.shape[1]
    nb = n // Q_BLOCK
    assert (not local) or n >= SPAN

    def body(q_ref, k_ref, v_ref, kc_ref, vc_ref, sink_ref, cq_ref, sq_ref, ck_ref, sk_ref, o_ref):
        q, k_all, v_all, bias, _, _, _ = _attn_operands(q_ref, k_ref, v_ref, kc_ref, vc_ref, cq_ref, sq_ref, ck_ref, sk_ref,
                                                        pl.program_id(1), n, n_ctx, local)
        sink_v = sink_ref[...]
        sl = lambda kh: slice(kh * HEAD_DIM, (kh + 1) * HEAD_DIM)
        ss = [_dot_nt(_stack_heads(q, kh), k_all[:, sl(kh)]) for kh in range(KV_HEADS)]
        ps = [_softmax_sink(s if bias is None else s + bias, _sink_rows(sink_v, kh))[0].astype(BF16) for kh, s in enumerate(ss)]
        for kh, p in enumerate(ps):
            o = _dot(p, v_all[:, sl(kh)])
            for g in range(GROUP):
                h = GROUP * kh + g
                o_ref[:, h * HEAD_DIM:(h + 1) * HEAD_DIM] = o[g * Q_BLOCK:(g + 1) * Q_BLOCK, :].astype(o_ref.dtype)

    seq = lambda blk: pl.BlockSpec((None, n, KV_W), lambda b, i: (b, 0, blk))
    ctxs = lambda blk: pl.BlockSpec((None, n_ctx, KV_W), lambda b, i: (b, 0, blk))
    full = lambda a: pl.BlockSpec(a.shape, lambda b, i: (0,) * a.ndim)
    cos_q, sin_q = jnp.tile(cos, (1, ATTN_HEADS)), jnp.tile(sin, (1, ATTN_HEADS))
    cos_k, sin_k = jnp.tile(cos, (1, KV_HEADS)), jnp.tile(sin, (1, KV_HEADS))
    body, x_in, x_out, x_shapes, x_sems = _riding(body, exch, 10, 1, (B, nb))
    return pl.pallas_call(
        body, name=name, grid=(B, nb),
        in_specs=[pl.BlockSpec((None, Q_BLOCK, ATTN_W), lambda b, i: (b, i, 0)),
                  seq(ATTN_W // KV_W), seq(ATTN_W // KV_W + 1), ctxs(0), ctxs(1), full(sink),
                  pl.BlockSpec((Q_BLOCK, ATTN_W), lambda b, i: (i, 0)), pl.BlockSpec((Q_BLOCK, ATTN_W), lambda b, i: (i, 0)),
                  full(cos_k), full(sin_k)] + x_in,
        out_specs=[pl.BlockSpec((None, Q_BLOCK, ATTN_W), lambda b, i: (b, i, 0))] + x_out,
        out_shape=[jax.ShapeDtypeStruct((B, n, ATTN_W), BF16)] + x_shapes,
        scratch_shapes=x_sems,
        compiler_params=_params("arbitrary", "arbitrary"),
    )(u, u, u, kvc, kvc, sink, cos_q, sin_q, cos_k, sin_k, *(exch[0] if exch else []))


def _attn_bwd(u, kvc, sink, cos, sin, do_src, do_blk, *, local, name, exch=None):
    B, n, _ = u.shape
    n_ctx = kvc.shape[1]
    nb = n // Q_BLOCK

    def body(q_ref, k_ref, v_ref, kc_ref, vc_ref, sink_ref, cq_ref, sq_ref, ck_ref, sk_ref, do_ref,
             dq_ref, dk_ref, dv_ref, dkc_ref, dvc_ref, dsink_ref):
        b = pl.program_id(0)
        i = pl.program_id(1)

        @pl.when(i == 0)
        def _():
            dk_ref[...] = jnp.zeros_like(dk_ref)
            dv_ref[...] = jnp.zeros_like(dv_ref)
            dkc_ref[...] = jnp.zeros_like(dkc_ref)
            dvc_ref[...] = jnp.zeros_like(dvc_ref)

        @pl.when((i == 0) & (b == 0))
        def _():
            dsink_ref[...] = jnp.zeros_like(dsink_ref)

        q, k_all, v_all, bias, s0, ck, sk = _attn_operands(q_ref, k_ref, v_ref, kc_ref, vc_ref, cq_ref, sq_ref, ck_ref, sk_ref,
                                                           i, n, n_ctx, local)
        do = do_ref[...].astype(BF16)
        sink_v = sink_ref[...]
        sl = lambda kh: slice(kh * HEAD_DIM, (kh + 1) * HEAD_DIM)
        heads = range(KV_HEADS)
        q_st = [_stack_heads(q, kh) for kh in heads]
        do_st = [_stack_heads(do, kh) for kh in heads]
        ss = [_dot_nt(q_st[kh], k_all[:, sl(kh)]) for kh in heads]
        dps = [_dot_nt(do_st[kh], v_all[:, sl(kh)]) for kh in heads]
        p_bf, ds_bf = [], []
        dsink = jnp.zeros((1, ATTN_HEADS), F32)
        lane8 = lax.broadcasted_iota(jnp.int32, (1, ATTN_HEADS), 1)
        for kh in heads:
            p, p_sink = _softmax_sink(ss[kh] if bias is None else ss[kh] + bias, _sink_rows(sink_v, kh))
            delta = jnp.sum(p * dps[kh], axis=1, keepdims=True)
            ds = p * (dps[kh] - delta)
            dsr = -(p_sink * delta)
            for g in range(GROUP):
                dsink = dsink + jnp.where(lane8 == GROUP * kh + g, jnp.sum(dsr[g * Q_BLOCK:(g + 1) * Q_BLOCK, :]), 0.0)
            p_bf.append(p.astype(BF16))
            ds_bf.append(ds.astype(BF16))
        over_rows = (((0,), (0,)), ((), ()))
        dks, dvs = [], []
        for kh in heads:
            dq_st = _dot(ds_bf[kh], k_all[:, sl(kh)]) * (HEAD_DIM ** -0.5)
            for g in range(GROUP):
                h = GROUP * kh + g
                dq_ref[:, h * HEAD_DIM:(h + 1) * HEAD_DIM] = dq_st[g * Q_BLOCK:(g + 1) * Q_BLOCK, :]
            dvs.append(lax.dot_general(p_bf[kh], do_st[kh], over_rows, preferred_element_type=F32))
            dks.append(lax.dot_general(ds_bf[kh], q_st[kh], over_rows, preferred_element_type=F32))
        dk_cat = jnp.concatenate(dks, axis=1)
        dv_cat = jnp.concatenate(dvs, axis=1)
        dsink_ref[...] += dsink
        dkc_ref[...] += dk_cat[:n_ctx, :]
        dvc_ref[...] += dv_cat[:n_ctx, :]
        if local:
            dq_ref[...] = _rope_bwd(dq_ref[...], cq_ref[...], sq_ref[...])
            dk_ref[pl.ds(s0, SPAN), :] += _rope_bwd(dk_cat[n_ctx:, :], ck, sk)
            dv_ref[pl.ds(s0, SPAN), :] += dv_cat[n_ctx:, :]

    seq = lambda blk: pl.BlockSpec((None, n, KV_W), lambda b, i: (b, 0, blk))
    ctxs = lambda blk: pl.BlockSpec((None, n_ctx, KV_W), lambda b, i: (b, 0, blk))
    full = lambda a: pl.BlockSpec(a.shape, lambda b, i: (0,) * a.ndim)
    qblk = lambda blk: pl.BlockSpec((None, Q_BLOCK, ATTN_W), lambda b, i: (b, i, blk))
    cos_q, sin_q = jnp.tile(cos, (1, ATTN_HEADS)), jnp.tile(sin, (1, ATTN_HEADS))
    cos_k, sin_k = jnp.tile(cos, (1, KV_HEADS)), jnp.tile(sin, (1, KV_HEADS))
    acc = lambda rows: pl.BlockSpec((None, rows, KV_W), lambda b, i: (b, 0, 0))
    body, x_in, x_out, x_shapes, x_sems = _riding(body, exch, 11, 6, (B, nb))
    return pl.pallas_call(
        body, name=name, grid=(B, nb),
        in_specs=[qblk(0), seq(ATTN_W // KV_W), seq(ATTN_W // KV_W + 1), ctxs(0), ctxs(1), full(sink),
                  pl.BlockSpec((Q_BLOCK, ATTN_W), lambda b, i: (i, 0)), pl.BlockSpec((Q_BLOCK, ATTN_W), lambda b, i: (i, 0)),
                  full(cos_k), full(sin_k), qblk(do_blk)] + x_in,
        out_specs=[qblk(0), acc(n), acc(n), acc(n_ctx), acc(n_ctx), pl.BlockSpec((1, ATTN_HEADS), lambda b, i: (0, 0))] + x_out,
        out_shape=[jax.ShapeDtypeStruct((B, n, ATTN_W), F32), jax.ShapeDtypeStruct((B, n, KV_W), F32),
                   jax.ShapeDtypeStruct((B, n, KV_W), F32), jax.ShapeDtypeStruct((B, n_ctx, KV_W), F32),
                   jax.ShapeDtypeStruct((B, n_ctx, KV_W), F32), jax.ShapeDtypeStruct((1, ATTN_HEADS), F32)] + x_shapes,
        scratch_shapes=x_sems,
        compiler_params=_params("arbitrary", "arbitrary"),
    )(u, u, u, kvc, kvc, sink, cos_q, sin_q, cos_k, sin_k, do_src, *(exch[0] if exch else []))


def _conv_chunk(s, n, a_ext, g_ext, dw, dw_b, ln_g, ln_b):
    del s, n
    r = a_ext.shape[0] - 2 * HALO
    h = a_ext * jax.nn.sigmoid(g_ext)
    acc = jnp.broadcast_to(dw_b, (r, CONV_W))
    first = HALO - CONV_KERNEL // 2
    span = r + 8 * ((first + CONV_KERNEL - 1) // 8)
    shifted = [h[b:b + span, :] for b in range(8)]
    for k in range(CONV_KERNEL):
        o = first + k
        acc = acc + shifted[o % 8][o - o % 8:o - o % 8 + r, :] * dw[k:k + 1, :]
    mu = jnp.mean(acc, axis=-1, keepdims=True)
    var = jnp.mean(jnp.square(acc - mu), axis=-1, keepdims=True)
    hn = (acc - mu) * lax.rsqrt(var + EPS) * ln_g + ln_b
    return hn * jax.nn.sigmoid(hn)


def _pool_chunk(s, n, p_ext, w_bd, scale):
    r = p_ext.shape[0] - 2 * HALO
    lane = lax.broadcasted_iota(jnp.int32, (1, POOL_W), 1)
    win = jnp.left_shift(2, lane // POOL_GROUP)
    half = win // 2
    acc = jnp.zeros((r, POOL_W), F32)
    for d in range(-(POOL_WINDOWS[-1] // 2), POOL_WINDOWS[-1] - POOL_WINDOWS[-1] // 2):
        inside = (d >= -half) & (d <= win - 1 - half)
        acc = acc + jnp.where(inside, p_ext[HALO + d:HALO + d + r, :], 0.0)
    t = s + lax.broadcasted_iota(jnp.int32, (r, 1), 0)
    lo = jnp.maximum(t - half, 0)
    hi = jnp.minimum(t + win - 1 - half, n - 1)
    y = acc / (hi - lo + 1).astype(F32) - p_ext[HALO:HALO + r, :]
    out = lax.dot_general(y.astype(BF16), w_bd.astype(BF16), (((1,), (0,)), ((), ())), preferred_element_type=F32)
    return out * scale


def _seq_specs(rows, params):
    specs = [pl.BlockSpec((None, a.shape[1], w), functools.partial(lambda b, blk: (b, 0, blk), blk=blk)) for a, w, blk in rows]
    specs += [pl.BlockSpec(p.shape, functools.partial(lambda b, nd: (0,) * nd, nd=p.ndim)) for p in params]
    return specs


def _fill_padded(pad_ref, row_ref, n):
    w = pad_ref.shape[1]
    pad_ref[pl.ds(0, HALO), :] = jnp.zeros((HALO, w), F32)
    pad_ref[pl.ds(HALO + n, HALO), :] = jnp.zeros((HALO, w), F32)
    pad_ref[pl.ds(HALO, n), :] = row_ref[...]


def _seq_fwd(fn, rows, params, out_w, *, name, chunk=SEQ_CHUNK):
    B, n = rows[0][0].shape[:2]
    r = min(chunk, n)
    nr, npar = len(rows), len(params)

    def body(*refs):
        row_refs, par_refs, o_ref, pads = refs[:nr], refs[nr:nr + npar], refs[nr + npar], refs[nr + npar + 1:]
        for rr, p in zip(row_refs, pads):
            _fill_padded(p, rr, n)
        pars = [p[...] for p in par_refs]

        def chunk(ci, carry):
            s = pl.multiple_of(ci * r, r)
            ext = [p[pl.ds(s, r + 2 * HALO), :] for p in pads]
            o_ref[pl.ds(s, r), :] = fn(s, n, *ext, *pars).astype(o_ref.dtype)
            return carry

        lax.fori_loop(0, n // r, chunk, 0)

    return pl.pallas_call(
        body, name=name, grid=(B,),
        in_specs=_seq_specs(rows, params),
        out_specs=pl.BlockSpec((None, n, out_w), lambda b: (b, 0, 0)),
        out_shape=jax.ShapeDtypeStruct((B, n, out_w), BF16),
        scratch_shapes=[pltpu.VMEM((n + 2 * HALO, w), F32) for _, w, _ in rows],
        compiler_params=_params("parallel"),
    )(*[a for a, _, _ in rows], *params)


def _seq_bwd(fn, rows, params, dout, *, name, chunk=SEQ_CHUNK):
    B, n = rows[0][0].shape[:2]
    r = min(chunk, n)
    nr, npar = len(rows), len(params)

    def body(*refs):
        row_refs, par_refs, do_ref = refs[:nr], refs[nr:nr + npar], refs[nr + npar]
        outs = refs[nr + npar + 1:]
        drow_refs, dpar_refs = outs[:nr], outs[nr:nr + npar]
        scratch = outs[nr + npar:]
        pads, dpads = scratch[:nr], scratch[nr:]
        for rr, p, dp in zip(row_refs, pads, dpads):
            _fill_padded(p, rr, n)
            dp[...] = jnp.zeros_like(dp)

        @pl.when(pl.program_id(0) == 0)
        def _():
            for d in dpar_refs:
                d[...] = jnp.zeros_like(d)

        pars = [p[...] for p in par_refs]

        def chunk(ci, carry):
            s = pl.multiple_of(ci * r, r)
            ext = [p[pl.ds(s, r + 2 * HALO), :] for p in pads]
            _, vjp = jax.vjp(functools.partial(fn, s, n), *ext, *pars)
            grads = vjp(do_ref[pl.ds(s, r), :])
            for dp, g in zip(dpads, grads[:nr]):
                dp[pl.ds(s, r + 2 * HALO), :] += g
            for d, g in zip(dpar_refs, grads[nr:]):
                d[...] += g
            return carry

        lax.fori_loop(0, n // r, chunk, 0)
        for d, dp in zip(drow_refs, dpads):
            d[...] = dp[pl.ds(HALO, n), :].astype(d.dtype)

    da, dw_, dblk = dout
    return pl.pallas_call(
        body, name=name, grid=(B,),
        in_specs=_seq_specs(rows, params) + [pl.BlockSpec((None, n, dw_), lambda b: (b, 0, dblk))],
        out_specs=[pl.BlockSpec((None, n, w), lambda b: (b, 0, 0)) for _, w, _ in rows]
        + [pl.BlockSpec(p.shape, functools.partial(lambda b, nd: (0,) * nd, nd=p.ndim)) for p in params],
        out_shape=[jax.ShapeDtypeStruct((B, n, w), BF16) for _, w, _ in rows]
        + [jax.ShapeDtypeStruct(p.shape, F32) for p in params],
        scratch_shapes=[pltpu.VMEM((n + 2 * HALO, w), F32) for _, w, _ in rows] * 2,
        compiler_params=_params("arbitrary"),
    )(*[a for a, _, _ in rows], *params, da)


_CONV_A_BLK = (ATTN_W + 2 * KV_W) // CONV_W
_CONV_G_BLK = _CONV_A_BLK + 1
_POOL_BLK = _CONV_A_BLK + 2


def _mixer_fwd(tag, u, kvc, margs, local, exch=None):
    sink, dw, dw_b, ln_g, ln_b, w_bd, scale = margs
    cos, sin = _rope_tables(max(u.shape[1], GRID_W))
    attn, *got = _attn_fwd(u, kvc, sink, cos, sin, local=local, name=f"{tag}_attn_fwd", exch=exch)
    conv = _seq_fwd(_conv_chunk, [(u, CONV_W, _CONV_A_BLK), (u, CONV_W, _CONV_G_BLK)], [dw, dw_b, ln_g, ln_b], CONV_W,
                    name=f"{tag}_conv_fwd")
    pool = _seq_fwd(_pool_chunk, [(u, POOL_W, _POOL_BLK)], [w_bd, scale], POOL_W, name=f"{tag}_pool_fwd")
    return jnp.concatenate([attn, conv, pool], axis=-1), got


def _mixer_bwd(tag, u, kvc, margs, dmix, local, exch=None):
    sink, dw, dw_b, ln_g, ln_b, w_bd, scale = margs
    cos, sin = _rope_tables(max(u.shape[1], GRID_W))
    dq, dk, dv, dkc, dvc, dsink, *got = _attn_bwd(u, kvc, sink, cos, sin, dmix, 0, local=local, name=f"{tag}_attn_bwd",
                                                  exch=exch)
    da, dg, ddw, ddw_b, dln_g, dln_b = _seq_bwd(
        _conv_chunk, [(u, CONV_W, _CONV_A_BLK), (u, CONV_W, _CONV_G_BLK)], [dw, dw_b, ln_g, ln_b],
        (dmix, CONV_W, ATTN_W // CONV_W), name=f"{tag}_conv_bwd", chunk=CONV_BWD_CHUNK)
    dpu, dw_bd, dscale = _seq_bwd(_pool_chunk, [(u, POOL_W, _POOL_BLK)], [w_bd, scale],
                                  (dmix, POOL_W, (ATTN_W + CONV_W) // POOL_W), name=f"{tag}_pool_bwd")
    return (dq, dk, dv, da, dg, dpu), (dkc, dvc), (dsink, ddw, ddw_b, dln_g, dln_b, dw_bd, dscale), got


def _row_specs(arrs, kinds, tr):
    specs = []
    for a, kind in zip(arrs, kinds):
        if kind == "row":
            specs.append(pl.BlockSpec((None, tr, a.shape[2]), lambda b, j: (b, j, 0)))
        elif kind == "batch":
            specs.append(pl.BlockSpec((None, 1, a.shape[2]), lambda b, j: (b, 0, 0)))
        else:
            specs.append(pl.BlockSpec(a.shape, functools.partial(lambda b, j, nd: (0,) * nd, nd=a.ndim)))
    return specs


def _rowwise_fwd(fn, ins, kinds, outs, tr, *, name, transposed=None):
    B, n = ins[0].shape[:2]
    ni, no = len(ins), len(outs)
    nj = n // tr

    def body(*refs):
        res = fn(*[r[...] for r in refs[:ni]])
        for o, v in zip(refs[ni:ni + no], res):
            o[...] = v.astype(o.dtype)
        if transposed is not None:
            refs[ni + no][...] = res[transposed].T.astype(refs[ni + no].dtype)

    out_specs = [pl.BlockSpec((None, tr, w), lambda b, j: (b, j, 0)) for w, _ in outs]
    out_shape = [jax.ShapeDtypeStruct((B, n, w), dt) for w, dt in outs]
    if transposed is not None:
        w, dt = outs[transposed]
        out_specs.append(pl.BlockSpec((w, tr), lambda b, j: (0, b * nj + j)))
        out_shape.append(jax.ShapeDtypeStruct((w, B * n), dt))
    return pl.pallas_call(
        body, name=name, grid=(B, nj),
        in_specs=_row_specs(ins, kinds, tr), out_specs=out_specs, out_shape=out_shape,
        compiler_params=_params("parallel", "parallel"),
    )(*ins)


def _rowwise_bwd(fn, ins, kinds, gdtypes, cts, tr, *, name):
    B, n = ins[0].shape[:2]
    ni, nc = len(ins), len(cts)
    idx = list(range(ni))

    def body(*refs):
        in_refs, ct_refs, out_refs = refs[:ni], refs[ni:ni + nc], refs[ni + nc:]
        b, j = pl.program_id(0), pl.program_id(1)
        _, vjp = jax.vjp(fn, *[r[...].astype(F32) for r in in_refs])
        grads = vjp(tuple(c[...].astype(F32) for c in ct_refs))
        for o, i in zip(out_refs, idx):
            g = grads[i]
            if kinds[i] == "row":
                o[...] = g.astype(o.dtype)
            else:
                first = (j == 0) if kinds[i] == "batch" else ((j == 0) & (b == 0))

                @pl.when(first)
                def _(o=o, g=g):
                    o[...] = g

                @pl.when(jnp.logical_not(first))
                def _(o=o, g=g):
                    o[...] += g

    specs = _row_specs(ins, kinds, tr)
    return pl.pallas_call(
        body, name=name, grid=(B, n // tr),
        in_specs=specs + [pl.BlockSpec((None, tr, c.shape[2]), lambda b, j: (b, j, 0)) for c in cts],
        out_specs=[specs[i] for i in idx],
        out_shape=[jax.ShapeDtypeStruct(ins[i].shape, gdtypes[i]) for i in idx],
        compiler_params=_params("arbitrary", "arbitrary"),
    )(*ins, *cts)


ROW_TILE = 512


def _rms_mod(x, g, sc, sh):
    y = x * lax.rsqrt(jnp.mean(x * x, axis=-1, keepdims=True) + EPS)
    return (y * g) * (1.0 + sc) + sh


def _norm_tile(x, g, sc, sh):
    return x, _rms_mod(x, g, sc, sh)


def _res_norm_tile(xb, y, gate, g, sc, sh):
    x = xb + gate * y
    return x, _rms_mod(x, g, sc, sh)


_NORM_KINDS = ("row", "glob", "batch", "batch")
_RES_NORM_KINDS = ("row", "row", "batch", "glob", "batch", "batch")


def _norm_fwd(tag, st, g, sc, sh):
    xb, y, gate = st
    tr = min(ROW_TILE, xb.shape[1])
    d = xb.shape[2]
    if y is None:
        h, h_t = _rowwise_fwd(lambda *a: (_rms_mod(*a),), [xb, g, sc, sh], _NORM_KINDS, [(d, BF16)], tr,
                              name=f"{tag}_fwd", transposed=0)
        return xb, h, h_t
    return _rowwise_fwd(_res_norm_tile, [xb, y, gate, g, sc, sh], _RES_NORM_KINDS, [(d, F32), (d, BF16)], tr,
                        name=f"{tag}_fwd", transposed=1)


def _norm_bwd(tag, st, g, sc, sh, dx, dh):
    xb, y, gate = st
    tr = min(ROW_TILE, xb.shape[1])
    if y is None:
        dxb, dg, dsc, dsh = _rowwise_bwd(_norm_tile, [xb, g, sc, sh], _NORM_KINDS, [F32] * 4, [dx, dh], tr, name=f"{tag}_bwd")
        return dxb, None, None, dg, dsc, dsh
    return tuple(_rowwise_bwd(_res_norm_tile, [xb, y, gate, g, sc, sh], _RES_NORM_KINDS, [F32, BF16, F32, F32, F32, F32],
                              [dx, dh], tr, name=f"{tag}_bwd"))


def _loss_head(st, final_g, target, *, name):
    xb, y, gate = st
    B, n, d = xb.shape
    tr = min(ROW_TILE, n)

    def tile_loss(xv, yv, gt, g, t):
        x = xv + gt * yv
        out = x * lax.rsqrt(jnp.mean(x * x, axis=-1, keepdims=True) + EPS) * g
        return 0.5 * jnp.sum(jnp.mean(jnp.square(out - t), axis=-1))

    def body(x_ref, y_ref, gate_ref, g_ref, t_ref, loss_ref, dx_ref, dy_ref, dgate_ref, dg_ref):
        b, j = pl.program_id(0), pl.program_id(1)
        val, (dx, dy, dgate, dg) = jax.value_and_grad(tile_loss, argnums=(0, 1, 2, 3))(
            x_ref[...], y_ref[...], gate_ref[...], g_ref[...], t_ref[...])
        dx_ref[...] = dx
        dy_ref[...] = dy.astype(dy_ref.dtype)

        @pl.when(j == 0)
        def _():
            loss_ref[...] = jnp.zeros_like(loss_ref)
            dgate_ref[...] = jnp.zeros_like(dgate_ref)

        @pl.when((j == 0) & (b == 0))
        def _():
            dg_ref[...] = jnp.zeros_like(dg_ref)

        loss_ref[...] += jnp.full(loss_ref.shape, val, F32)
        dgate_ref[...] += dgate
        dg_ref[...] += dg

    row = pl.BlockSpec((None, tr, d), lambda b, j: (b, j, 0))
    per_sample = pl.BlockSpec((None, 1, d), lambda b, j: (b, 0, 0))
    whole = pl.BlockSpec((1, d), lambda b, j: (0, 0))
    return pl.pallas_call(
        body, name=name, grid=(B, n // tr),
        in_specs=[row, row, per_sample, whole, row],
        out_specs=[pl.BlockSpec((None, 1, 128), lambda b, j: (b, 0, 0)), row, row, per_sample, whole],
        out_shape=[jax.ShapeDtypeStruct((B, 1, 128), F32), jax.ShapeDtypeStruct((B, n, d), F32),
                   jax.ShapeDtypeStruct((B, n, d), BF16), jax.ShapeDtypeStruct((B, 1, d), F32), jax.ShapeDtypeStruct((1, d), F32)],
        compiler_params=_params("arbitrary", "arbitrary"),
    )(xb, y, gate, final_g, target)


def _exchange(arrs, *, scatter, name):
    k = len(arrs)

    def body(*refs):
        ins, outs, sems = refs[:k], refs[k:2 * k], refs[2 * k:]
        _exch_start(ins, outs, sems, scatter)
        _exch_wait(ins, outs, sems, scatter)

    any_spec = pl.BlockSpec(memory_space=pl.ANY)
    return pl.pallas_call(
        body, name=name,
        in_specs=[any_spec] * k, out_specs=[any_spec] * k,
        out_shape=_exch_out_shapes(arrs, scatter), scratch_shapes=_exch_sems(k),
        compiler_params=pltpu.CompilerParams(has_side_effects=True),
    )(*arrs)


def _exch_flags(scatter, k):
    return [scatter] * k if isinstance(scatter, bool) else list(scatter)


def _exch_out_shapes(arrs, scatter):
    return [jax.ShapeDtypeStruct(a.shape if f else (N_DEV,) + a.shape, a.dtype)
            for a, f in zip(arrs, _exch_flags(scatter, len(arrs)))]


def _exch_sems(k):
    return [pltpu.SemaphoreType.DMA((k * (N_DEV - 1),)), pltpu.SemaphoreType.DMA((k * (N_DEV - 1),)),
            pltpu.SemaphoreType.DMA((k,))]


def _exch_copies(ins, outs, sems, scatter):
    send_sems, recv_sems, local_sems = sems
    x, y, c = lax.axis_index("x"), lax.axis_index("y"), lax.axis_index("c")
    me = 4 * x + 2 * y + c
    owns, sends, recvs = [], [], []
    flags = _exch_flags(scatter, len(ins))
    for a in range(len(ins)):
        scatter = flags[a]
        owns.append(pltpu.make_async_copy(ins[a].at[me] if scatter else ins[a], outs[a].at[me], local_sems.at[a]))
        for r in range(1, N_DEV):
            fx, fy, fc = (r >> 2) & 1, (r >> 1) & 1, r & 1
            px, py, pc = (x + fx) % 2, (y + fy) % 2, (c + fc) % 2
            peer = 4 * px + 2 * py + pc
            s = a * (N_DEV - 1) + r - 1
            mk = functools.partial(pltpu.make_async_remote_copy, src_ref=ins[a].at[peer] if scatter else ins[a],
                                   send_sem=send_sems.at[s], recv_sem=recv_sems.at[s],
                                   device_id=(px, py, pc), device_id_type=pl.DeviceIdType.MESH)
            sends.append(mk(dst_ref=outs[a].at[me]))
            recvs.append(mk(dst_ref=outs[a].at[peer]))
    return owns, sends, recvs


def _exch_start(ins, outs, sems, scatter):
    owns, sends, _ = _exch_copies(ins, outs, sems, scatter)
    for cp in owns + sends:
        cp.start()


def _exch_wait(ins, outs, sems, scatter):
    owns, sends, recvs = _exch_copies(ins, outs, sems, scatter)
    for rc in recvs:
        rc.wait_recv()
    for cp in sends:
        cp.wait_send()
    for own in owns:
        own.wait()


MOD_ROWS = 48


def _mod_tile(cc, w, b):
    s = cc * jax.nn.sigmoid(cc)
    return lax.dot_general(s.astype(BF16), w.astype(BF16), (((1,), (0,)), ((), ())), preferred_element_type=F32) + b


def _mod_fwd(cc, w_mod, b_shard, *, name):
    L, d, wcols = w_mod.shape

    def body(cc_ref, w_ref, b_ref, o_ref):
        o_ref[...] = _mod_tile(cc_ref[...], w_ref[...], b_ref[...])

    return pl.pallas_call(
        body, name=name, grid=(L,),
        in_specs=[pl.BlockSpec((MOD_ROWS, d), lambda l: (0, 0)), pl.BlockSpec((None, d, wcols), lambda l: (l, 0, 0)),
                  pl.BlockSpec((None, 1, wcols), lambda l: (l, 0, 0))],
        out_specs=pl.BlockSpec((None, MOD_ROWS, wcols), lambda l: (l, 0, 0)),
        out_shape=jax.ShapeDtypeStruct((L, MOD_ROWS, wcols), F32),
        compiler_params=_params("parallel"),
    )(cc, w_mod, b_shard)


def _mod_bwd(cc, w_mod, b_shard, dm, *, name):
    L, d, wcols = w_mod.shape

    def body(cc_ref, w_ref, b_ref, dm_ref, dcc_ref, dw_ref):
        _, vjp = jax.vjp(_mod_tile, cc_ref[...], w_ref[...], b_ref[...])
        dcc, dw, _ = vjp(dm_ref[...])
        dw_ref[...] = dw

        @pl.when(pl.program_id(0) == 0)
        def _():
            dcc_ref[...] = dcc

        @pl.when(pl.program_id(0) > 0)
        def _():
            dcc_ref[...] += dcc

    return pl.pallas_call(
        body, name=name, grid=(L,),
        in_specs=[pl.BlockSpec((MOD_ROWS, d), lambda l: (0, 0)), pl.BlockSpec((None, d, wcols), lambda l: (l, 0, 0)),
                  pl.BlockSpec((None, 1, wcols), lambda l: (l, 0, 0)), pl.BlockSpec((None, MOD_ROWS, wcols), lambda l: (l, 0, 0))],
        out_specs=[pl.BlockSpec((MOD_ROWS, d), lambda l: (0, 0)), pl.BlockSpec((None, d, wcols), lambda l: (l, 0, 0))],
        out_shape=[jax.ShapeDtypeStruct((MOD_ROWS, d), F32), jax.ShapeDtypeStruct((L, d, wcols), F32)],
        compiler_params=_params("arbitrary"),
    )(cc, w_mod, b_shard, dm)


def _sum_leading(a, *, name):
    K, R, C = a.shape
    tr = _tile8(R, 256)

    def body(a_ref, o_ref):
        acc = a_ref[0].astype(F32)
        for i in range(1, K):
            acc = acc + a_ref[i].astype(F32)
        o_ref[...] = acc

    return pl.pallas_call(
        body, name=name, grid=(R // tr,),
        in_specs=[pl.BlockSpec((K, tr, C), lambda i: (0, i, 0))],
        out_specs=pl.BlockSpec((tr, C), lambda i: (i, 0)),
        out_shape=jax.ShapeDtypeStruct((R, C), F32),
        compiler_params=_params("parallel"),
    )(a)


def _tile8(dim, target):
    if dim <= target:
        return dim
    t = (target // 8) * 8
    while t >= 8:
        if dim % t == 0:
            return t
        t -= 8
    raise ValueError(f"no row tile for {dim}")


def _adamw_math(g, w, m, v):
    m = ADAM_B1 * m + (1.0 - ADAM_B1) * g
    v = ADAM_B2 * v + (1.0 - ADAM_B2) * jnp.square(g)
    m_hat = m / (1.0 - ADAM_B1 ** ADAM_STEP)
    v_hat = v / (1.0 - ADAM_B2 ** ADAM_STEP)
    delta = -ADAM_LR * (m_hat / (jnp.sqrt(v_hat) + ADAM_EPS) + ADAM_WD * w)
    return delta, m, v


def _adamw(g, w, m, v, *, name):
    L, R, C = w.shape
    parts = isinstance(g, (list, tuple))
    gs = list(g) if parts else [g]
    ng = len(gs)
    tr = _tile8(R, 256)

    def body(*refs):
        g_refs = refs[:ng]
        w_ref, m_ref, v_ref, go_ref, d_ref, mo_ref, vo_ref = refs[ng:]
        if parts:
            layer = pl.program_id(0)
            gv = None
            for li, g_ref in enumerate(g_refs):
                acc = g_ref[0].astype(F32)
                for i in range(1, N_DEV):
                    acc = acc + g_ref[i].astype(F32)
                gv = acc if gv is None else jnp.where(layer == li, acc, gv)
        else:
            gv = g_refs[0][...]
        go_ref[...] = gv
        d_ref[...], mo_ref[...], vo_ref[...] = _adamw_math(gv, w_ref[...], m_ref[...], v_ref[...])

    tile = pl.BlockSpec((None, tr, C), lambda l, i: (l, i, 0))
    g_specs = [pl.BlockSpec((N_DEV, tr, C), lambda l, i: (0, i, 0))] * ng if parts else [tile]
    return pl.pallas_call(
        body, name=name, grid=(L, R // tr),
        in_specs=g_specs + [tile, tile, tile], out_specs=[tile] * 4,
        out_shape=[jax.ShapeDtypeStruct((L, R, C), F32)] * 4,
        compiler_params=_params("parallel", "parallel"),
    )(*gs, w, m, v)


def _adamw_small(gs, ws, ms, vs, *, name):
    k = len(ws)

    def body(*refs):
        g_refs, w_refs, m_refs, v_refs = refs[:k], refs[k:2 * k], refs[2 * k:3 * k], refs[3 * k:4 * k]
        d_refs, mo_refs, vo_refs = refs[4 * k:5 * k], refs[5 * k:6 * k], refs[6 * k:]
        for i in range(k):
            d_refs[i][...], mo_refs[i][...], vo_refs[i][...] = _adamw_math(g_refs[i][...], w_refs[i][...], m_refs[i][...],
                                                                         v_refs[i][...])

    shapes = [jax.ShapeDtypeStruct(a.shape, F32) for a in ws]
    out = pl.pallas_call(body, name=name, out_shape=shapes * 3, compiler_params=pltpu.CompilerParams(vmem_limit_bytes=VMEM_LIMIT))(
        *gs, *ws, *ms, *vs)
    return out[:k], out[k:2 * k], out[2 * k:]


def _block_diag(w):
    g, c, d = w.shape
    return (w[:, :, None, :] * jnp.eye(g, dtype=w.dtype)[:, None, :, None]).reshape(g * c, g * d)


def _diag_blocks(w_bd):
    g = POOL_W // POOL_GROUP
    return jnp.stack([w_bd[i * POOL_GROUP:(i + 1) * POOL_GROUP, i * POOL_GROUP:(i + 1) * POOL_GROUP] for i in range(g)])


def _flat(a):
    return a.reshape(-1, a.shape[-1])


def _mix_half_fwd(tag, st, mods, wl, kvc, *, local, kv_only, exch=None):
    sh1, sc1, g1 = mods[:3]
    B, n, d = st[0].shape
    x, h, h_t = _norm_fwd(f"{tag}_norm1", st, wl["n1"], sc1, sh1)
    if kv_only:
        kv = _mm(_flat(h), wl["w_in"][:, ATTN_W:ATTN_W + 2 * KV_W], name=f"{tag}_kv").reshape(B, n, 2 * KV_W)
        return None, dict(st=st, h_t=h_t, kvc=kv), []
    u = _mm(_flat(h), wl["w_in"], name=f"{tag}_in", tn=IN_W).reshape(B, n, IN_W)
    if not local:
        kvc = u[:, :, ATTN_W:ATTN_W + 2 * KV_W]
    mix, got = _mixer_fwd(f"{tag}_mix", u, kvc, wl["margs"], local, exch)
    y = _mm(_flat(mix), wl["w_out"], name=f"{tag}_out", tn=D_MODEL).reshape(B, n, d)
    return (x, y, g1), dict(st=st, h_t=h_t, u=u, kvc=kvc, mix=mix), got


def _ffn_half_fwd(tag, st2, mods, wl, exch=None):
    sh2, sc2, g2 = mods[3:]
    B, n, d = st2[0].shape
    x1, h2, h2_t = _norm_fwd(f"{tag}_norm2", st2, wl["n2"], sc2, sh2)
    gu, act, act_t, *got = _mm_swiglu(_flat(h2), wl["w_ffn_in"], name=f"{tag}_ffn_in", exch=exch)
    y2 = _mm(act, wl["w_ffn_out"], name=f"{tag}_ffn_out", tn=D_MODEL).reshape(B, n, d)
    return (x1, y2, g2), dict(st2=st2, h2_t=h2_t, gu=gu, act_t=act_t), got


def _ffn_half_bwd(tag, sv, mods, wl, dx1, dy2):
    sh2, sc2, _ = mods[3:]
    B, n, d = sv["st2"][0].shape
    gw = {}
    dy2f = _flat(dy2)
    gw["w_ffn_out"] = _mm(sv["act_t"], dy2f, out_dtype=BF16, name=f"{tag}_ffn_out_dw")
    dgu = _mm_dswiglu(dy2f, wl["w_ffn_out"], sv["gu"], name=f"{tag}_ffn_out_dx")
    dh2 = _mm(dgu, wl["w_ffn_in"], trans_b=True, name=f"{tag}_ffn_in_dx").reshape(B, n, d)
    gw["w_ffn_in"] = _mm(sv["h2_t"], dgu, out_dtype=BF16, tn=FF_TILE, out_block=_natural_block, name=f"{tag}_ffn_in_dw")
    dx, dy, dg1, gw["n2"], dsc2, dsh2 = _norm_bwd(f"{tag}_norm2", sv["st2"], wl["n2"], sc2, sh2, dx1, dh2)
    return (dx, dy, dg1), gw, dict(sh2=dsh2, sc2=dsc2)


def _mix_half_bwd(tag, sv, mods, wl, dx, dy, dkv_in, *, local, kv_only, exch=None):
    sh1, sc1, _ = mods[:3]
    B, n, d = sv["st"][0].shape
    gw = {}
    if kv_only:
        dkv = _flat(dkv_in).astype(BF16)
        dh = _mm(dkv, wl["w_in"][:, ATTN_W:ATTN_W + 2 * KV_W], trans_b=True, name=f"{tag}_kv_dx").reshape(B, n, d)
        gw["w_in_kv"] = _mm(sv["h_t"], dkv, out_dtype=BF16, name=f"{tag}_kv_dw")
        dxb, dy_prev, dgate_prev, gw["n1"], dsc1, dsh1 = _norm_bwd(f"{tag}_norm1", sv["st"], wl["n1"], sc1, sh1,
                                                                    jnp.zeros((B, n, d), F32), dh)
        return (dxb, dy_prev, dgate_prev), gw, dict(sh1=dsh1, sc1=dsc1), None, []

    dyf = _flat(dy)
    dmix = _mm(dyf, wl["w_out"], trans_b=True, name=f"{tag}_out_dx", tn=D_MODEL).reshape(B, n, d)
    gw["w_out"] = _mm(_flat(sv["mix"]).T, dyf, out_dtype=BF16, tn=DW_TN, name=f"{tag}_out_dw")
    (dq, dk, dv, da, dg, dpu), (dkc, dvc), gw["margs"], got = _mixer_bwd(f"{tag}_mix", sv["u"], sv["kvc"], wl["margs"], dmix,
                                                                     local, exch)
    if local:
        dkv_out = jnp.concatenate([dkc, dvc], axis=-1)
    else:
        dk = dkc + dkv_in[:, :, :KV_W]
        dv = dvc + dkv_in[:, :, KV_W:]
        dkv_out = None
    du = _flat(jnp.concatenate([dq, dk, dv, da, dg, dpu], axis=-1).astype(BF16))
    dh = _mm(du, wl["w_in"], trans_b=True, name=f"{tag}_in_dx", tn=D_MODEL).reshape(B, n, d)
    gw["w_in"] = _mm(sv["h_t"], du, out_dtype=BF16, tn=DW_TN, name=f"{tag}_in_dw")
    dxb, dy_prev, dgate_prev, gw["n1"], dsc1, dsh1 = _norm_bwd(f"{tag}_norm1", sv["st"], wl["n1"], sc1, sh1, dx, dh)
    return (dxb, dy_prev, dgate_prev), gw, dict(sh1=dsh1, sc1=dsc1), dkv_out, got


BIG_W = ("w_in", "w_out", "w_ffn_in", "w_ffn_out")


def _local_step(x, ctx, m_loc, m_ctx, p, final_g, target, big):
    B = x.shape[0]
    depth = m_loc.shape[0]
    lat_mods = [[t[:, None, :] for t in jnp.split(m_loc[l], 6, axis=-1)] for l in range(depth)]
    ctx_mods = [[jnp.broadcast_to(t[None, None, :], (B, 1, D_MODEL)) for t in jnp.split(m_ctx[l], 6)] for l in range(depth)]

    st, cst = (x, None, None), (ctx, None, None)
    w_mix, w_ffn, sv_mix, sv_ffn, csv_mix, csv_ffn = [], [], [], [], [], []
    got = []
    for l in range(depth):
        last = l == depth - 1
        wb = big.mix_weights(l, got)
        wm = dict(n1=p["norm1_g"][l][None, :], w_in=wb["w_in"], w_out=wb["w_out"],
                  margs=(p["attn_sink"][l][None, :], p["conv_dw"][l], p["conv_dw_b"][l][None, :], p["conv_ln_g"][l][None, :],
                         p["conv_ln_b"][l][None, :], _block_diag(p["pool_w"][l]), p["pool_scale"][l][None, :]))
        cst, csv, _ = _mix_half_fwd(f"l{l}c", cst, ctx_mods[l], wm, None, local=False, kv_only=last)
        st, sv, got = _mix_half_fwd(f"l{l}", st, lat_mods[l], wm, csv["kvc"], local=True, kv_only=False,
                                    exch=big.ride_attn_fwd(l))
        w_mix.append(wm)
        sv_mix.append(sv)
        csv_mix.append(csv)
        wb = big.ffn_weights(l, got)
        w_ffn_in = _interleave_ffn(wb["w_ffn_in"], name=f"l{l}_ffn_in_interleave")
        wf = dict(n2=p["norm2_g"][l][None, :], w_ffn_in=w_ffn_in, w_ffn_out=wb["w_ffn_out"])
        csv = None
        if not last:
            cst, csv, _ = _ffn_half_fwd(f"l{l}c", cst, ctx_mods[l], wf)
        st, sv, got = _ffn_half_fwd(f"l{l}", st, lat_mods[l], wf, exch=big.ride_ffn_fwd(l))
        w_ffn.append(wf)
        sv_ffn.append(sv)
        csv_ffn.append(csv)
    loss_rows, dx, dy, dgate, dfinal = _loss_head(st, final_g[None, :], target, name="loss_head")

    dm_loc, dm_ctx = [None] * depth, [None] * depth
    small = [None] * depth
    cdx = cdy = cdgate = None
    up_mix = None
    for l in reversed(range(depth)):
        last = l == depth - 1
        dm, cdm = dict(g2=dgate), {}
        (dx, dy, dm["g1"]), gf, d = _ffn_half_bwd(f"l{l}", sv_ffn[l], lat_mods[l], w_ffn[l], dx, dy)
        dm.update(d)
        if not last:
            cdm["g2"] = cdgate
            (cdx, cdy, cdm["g1"]), cgf, d = _ffn_half_bwd(f"l{l}c", csv_ffn[l], ctx_mods[l], w_ffn[l], cdx, cdy)
            cdm.update(d)
            gf = {k: gf[k] + cgf[k] for k in gf}
        ffn_grads = {k: gf[k] for k in _ShardedWeights.FFN}
        (dx, dy, dgate), gm, d, dkv, got = _mix_half_bwd(f"l{l}", sv_mix[l], lat_mods[l], w_mix[l], dx, dy, None, local=True,
                                                        kv_only=False, exch=big.ride_attn_bwd(l, ffn_grads, up_mix))
        big.took(l, ffn_grads, up_mix, got)
        dm.update(d)
        (cdx, cdy, cdgate), cgm, d, _, _ = _mix_half_bwd(f"l{l}c", csv_mix[l], ctx_mods[l], w_mix[l], cdx, cdy, dkv,
                                                        local=False, kv_only=last)
        cdm.update(d)
        order = ("sh1", "sc1", "g1", "sh2", "sc2", "g2")
        dm_loc[l] = jnp.concatenate([dm[k][:, 0, :] for k in order], axis=-1)
        dm_ctx[l] = jnp.concatenate([jnp.sum(cdm[k][:, 0, :], axis=0) if k in cdm else jnp.zeros((D_MODEL,), F32)
                                     for k in order])
        if last:
            up_mix = dict(w_in=gm["w_in"].at[:, ATTN_W:ATTN_W + 2 * KV_W].add(cgm["w_in_kv"]), w_out=gm["w_out"])
            margs = gm["margs"]
        else:
            up_mix = {k: gm[k] + cgm[k] for k in _ShardedWeights.MIX}
            margs = tuple(a + b for a, b in zip(gm["margs"], cgm["margs"]))
        small[l] = dict(n1=gm["n1"] + cgm["n1"], n2=gf["n2"], margs=margs)
    big.leftover(up_mix)

    stack = lambda f: jnp.stack([f(small[l]) for l in range(depth)])
    dp = dict(
        norm1_g=stack(lambda g: g["n1"][0]), norm2_g=stack(lambda g: g["n2"][0]),
        attn_sink=stack(lambda g: g["margs"][0][0]), conv_dw=stack(lambda g: g["margs"][1]),
        conv_dw_b=stack(lambda g: g["margs"][2][0]), conv_ln_g=stack(lambda g: g["margs"][3][0]),
        conv_ln_b=stack(lambda g: g["margs"][4][0]), pool_w=stack(lambda g: _diag_blocks(g["margs"][5])),
        pool_scale=stack(lambda g: g["margs"][6][0]))
    return jnp.sum(loss_rows[:, 0, 0]), dx, jnp.stack(dm_loc), jnp.stack(dm_ctx), dp, dfinal[0]


PACK_COLS = 1024


def _pack(arrs):
    flat = jnp.concatenate([a.reshape(-1).astype(F32) for a in arrs])
    rows = -(-flat.shape[0] // (8 * PACK_COLS)) * 8
    return jnp.pad(flat, (0, rows * PACK_COLS - flat.shape[0])).reshape(rows, PACK_COLS)


def _unpack(slab, like):
    flat = slab.reshape(-1)
    out, off = [], 0
    for a in like:
        out.append(flat[off:off + a.size].reshape(a.shape))
        off += a.size
    return out


def _shard_cols(gathered):
    _, L, R, C = gathered.shape
    return jnp.transpose(gathered, (1, 2, 0, 3)).reshape(L, R, N_DEV * C)


class _ShardedWeights:
    MIX = ("w_in", "w_out")
    FFN = ("w_ffn_in", "w_ffn_out")
    BY_COLS = ("w_in", "w_ffn_in")

    def __init__(self, shards, first):
        self.shards = shards
        self.first = first
        self.depth = shards[BIG_W[0]].shape[0]
        self.parts = [dict() for _ in range(self.depth)]
        self.left = None

    def _join(self, names, blocks):
        out = {}
        for name, g in zip(names, blocks):
            _, R, C = g.shape
            out[name] = jnp.transpose(g, (1, 0, 2)).reshape(R, N_DEV * C) if name in self.BY_COLS else g.reshape(N_DEV * R, C)
        return out

    def cut(self, names, grads):
        out = []
        for name in names:
            g = grads[name]
            if name in self.BY_COLS:
                R, C8 = g.shape
                out.append(jnp.transpose(g.reshape(R, N_DEV, C8 // N_DEV), (1, 0, 2)))
            else:
                R8, C = g.shape
                out.append(g.reshape(N_DEV, R8 // N_DEV, C))
        return out

    def mix_weights(self, l, got):
        return self._join(self.MIX, self.first if l == 0 else got)

    def ffn_weights(self, l, got):
        return self._join(self.FFN, got)

    def ride_attn_fwd(self, l):
        return [self.shards[name][l] for name in self.FFN], False

    def ride_ffn_fwd(self, l):
        if l + 1 >= self.depth:
            return None
        return [self.shards[name][l + 1] for name in self.MIX], False

    def ride_attn_bwd(self, l, ffn_grads, up_mix):
        return self.cut(self.FFN, ffn_grads) + (self.cut(self.MIX, up_mix) if up_mix is not None else []), True

    def took(self, l, ffn_grads, up_mix, got):
        self.parts[l].update(zip(self.FFN, got[:2]))
        if up_mix is not None:
            self.parts[l + 1].update(zip(self.MIX, got[2:]))

    def leftover(self, mix_grads):
        self.left = mix_grads


def _as_rows(a, leading=0):
    return a.reshape(*a.shape[:leading], -1, PACK_COLS)


SMALL = ("c_ctx", "b_mod", "norm1_g", "norm2_g", "conv_dw_b", "conv_ln_g", "conv_ln_b", "attn_sink", "pool_w",
         "pool_scale", "final_g", "conv_dw")
BIG = ("w_mod", "w_in", "w_out", "w_ffn_in", "w_ffn_out")
ORDER = ("c_ctx", "w_mod", "b_mod", "norm1_g", "norm2_g", "w_in", "conv_dw", "conv_dw_b", "conv_ln_g", "conv_ln_b",
         "attn_sink", "pool_w", "pool_scale", "w_out", "w_ffn_in", "w_ffn_out", "final_g")


def kernel(x, c, ctx, c_ctx, w_mod, b_mod, norm1_g, norm2_g, w_in, conv_dw, conv_dw_b, conv_ln_g, conv_ln_b, attn_sink, pool_w, pool_scale, w_out, w_ffn_in, w_ffn_out, final_g, loss_target, m_c_ctx, m_w_mod, m_b_mod, m_norm1_g, m_norm2_g, m_w_in, m_conv_dw, m_conv_dw_b, m_conv_ln_g, m_conv_ln_b, m_attn_sink, m_pool_w, m_pool_scale, m_w_out, m_w_ffn_in, m_w_ffn_out, m_final_g, v_c_ctx, v_w_mod, v_b_mod, v_norm1_g, v_norm2_g, v_w_in, v_conv_dw, v_conv_dw_b, v_conv_ln_g, v_conv_ln_b, v_attn_sink, v_pool_w, v_pool_scale, v_w_out, v_w_ffn_in, v_w_ffn_out, v_final_g):
    w = dict(c_ctx=c_ctx, w_mod=w_mod, b_mod=b_mod, norm1_g=norm1_g, norm2_g=norm2_g, w_in=w_in, conv_dw=conv_dw,
             conv_dw_b=conv_dw_b, conv_ln_g=conv_ln_g, conv_ln_b=conv_ln_b, attn_sink=attn_sink, pool_w=pool_w,
             pool_scale=pool_scale, w_out=w_out, w_ffn_in=w_ffn_in, w_ffn_out=w_ffn_out, final_g=final_g)
    mom = dict(c_ctx=m_c_ctx, w_mod=m_w_mod, b_mod=m_b_mod, norm1_g=m_norm1_g, norm2_g=m_norm2_g, w_in=m_w_in,
               conv_dw=m_conv_dw, conv_dw_b=m_conv_dw_b, conv_ln_g=m_conv_ln_g, conv_ln_b=m_conv_ln_b,
               attn_sink=m_attn_sink, pool_w=m_pool_w, pool_scale=m_pool_scale, w_out=m_w_out, w_ffn_in=m_w_ffn_in,
               w_ffn_out=m_w_ffn_out, final_g=m_final_g)
    var = dict(c_ctx=v_c_ctx, w_mod=v_w_mod, b_mod=v_b_mod, norm1_g=v_norm1_g, norm2_g=v_norm2_g, w_in=v_w_in,
               conv_dw=v_conv_dw, conv_dw_b=v_conv_dw_b, conv_ln_g=v_conv_ln_g, conv_ln_b=v_conv_ln_b,
               attn_sink=v_attn_sink, pool_w=v_pool_w, pool_scale=v_pool_scale, w_out=v_w_out, w_ffn_in=v_w_ffn_in,
               w_ffn_out=v_w_ffn_out, final_g=v_final_g)
    B = x.shape[0]
    depth = w_mod.shape[0]
    mod_cols = w_mod.shape[2]
    dw_cols = conv_dw.shape[2]
    me = 4 * lax.axis_index("x") + 2 * lax.axis_index("y") + lax.axis_index("c")

    shards = {name: w[name].astype(BF16) for name in BIG_W}
    c_all, dw_all, *first = _exchange([c, conv_dw] + [shards[name][0] for name in _ShardedWeights.MIX], scatter=False,
                                      name="gather_first")
    big = _ShardedWeights(shards, first)
    p = dict(norm1_g=norm1_g, norm2_g=norm2_g, conv_dw=_shard_cols(dw_all), conv_dw_b=conv_dw_b, conv_ln_g=conv_ln_g,
             conv_ln_b=conv_ln_b, attn_sink=attn_sink, pool_w=pool_w, pool_scale=pool_scale)

    cc = jnp.concatenate([c_all.reshape(N_DEV * B, D_MODEL), jnp.broadcast_to(c_ctx[None, :], (N_DEV, D_MODEL)),
                          jnp.zeros((MOD_ROWS - N_DEV * B - N_DEV, D_MODEL), F32)], axis=0)
    b_shard = lax.dynamic_slice_in_dim(b_mod, me * mod_cols, mod_cols, axis=1)[:, None, :]
    m_part = _mod_fwd(cc, w_mod, b_shard, name="mod_fwd")
    m_all, = _exchange([m_part], scatter=False, name="gather_mod")
    m_full = _shard_cols(m_all)
    m_loc = lax.dynamic_slice_in_dim(m_full, me * B, B, axis=1)
    m_ctx = m_full[:, N_DEV * B, :]

    loss_part, dx, dm_loc, dm_ctx, dp, dfinal = _local_step(x, ctx, m_loc, m_ctx, p, final_g, loss_target, big)
    loss = lax.psum(loss_part, AXES)

    dm_rows = jnp.concatenate([dm_loc, dm_ctx[:, None, :], jnp.zeros((depth, 8 - B - 1, 6 * D_MODEL), F32)], axis=1)
    small_like = [norm1_g, norm2_g, conv_dw_b, conv_ln_g, conv_ln_b, attn_sink, pool_w, pool_scale, final_g, dp["conv_dw"]]
    small_part = _pack([dp["norm1_g"], dp["norm2_g"], dp["conv_dw_b"], dp["conv_ln_g"], dp["conv_ln_b"], dp["attn_sink"],
                        dp["pool_w"], dp["pool_scale"], dfinal, dp["conv_dw"]])
    dm_all, small_all, *last = _exchange([dm_rows, small_part] + big.cut(big.MIX, big.left),
                                         scatter=[False, False, True, True], name="exchange_tail")
    big.parts[0].update(zip(big.MIX, last))
    dm_full = jnp.concatenate([
        jnp.transpose(dm_all[:, :, :B, :], (1, 0, 2, 3)).reshape(depth, N_DEV * B, 6 * D_MODEL),
        jnp.transpose(dm_all[:, :, B, :], (1, 0, 2)),
        jnp.zeros((depth, MOD_ROWS - N_DEV * B - N_DEV, 6 * D_MODEL), F32)], axis=1)
    g_b_mod = jnp.stack([_sum_leading(dm_full[l][:, None, :], name=f"b_mod_grad{l}")[0] for l in range(depth)])
    dm_mine = lax.dynamic_slice_in_dim(dm_full, me * mod_cols, mod_cols, axis=2)
    dcc, g_w_mod = _mod_bwd(cc, w_mod, b_shard, dm_mine, name="mod_bwd")
    g_c_ctx_part = jnp.sum(dcc[N_DEV * B:N_DEV * B + N_DEV], axis=0)

    small_sum = _unpack(_sum_leading(small_all, name="sum_small"), small_like)
    g = dict(zip(("norm1_g", "norm2_g", "conv_dw_b", "conv_ln_g", "conv_ln_b", "attn_sink", "pool_w", "pool_scale",
                  "final_g"), small_sum[:-1]))
    g["b_mod"] = g_b_mod
    g["conv_dw"] = lax.dynamic_slice_in_dim(small_sum[-1], me * dw_cols, dw_cols, axis=2)
    c_ctx_all, = _exchange([g_c_ctx_part.reshape(8, D_MODEL // 8)], scatter=False, name="gather_c_ctx")
    g["c_ctx"] = _sum_leading(c_ctx_all, name="sum_c_ctx").reshape(D_MODEL)

    delta, new_m, new_v = {}, {}, {}
    for name in BIG_W:
        g[name], delta[name], new_m[name], new_v[name] = _adamw(
            [big.parts[l][name] for l in range(depth)], w[name], mom[name], var[name], name=f"adamw_{name}")
    g["w_mod"], delta["w_mod"], new_m["w_mod"], new_v["w_mod"] = _adamw(g_w_mod, w_mod, m_w_mod, v_w_mod, name="adamw_w_mod")
    res = _adamw_small([g[k] for k in SMALL], [w[k] for k in SMALL], [mom[k] for k in SMALL], [var[k] for k in SMALL],
                       name="adamw_small")
    for dst, arrs in zip((delta, new_m, new_v), res):
        dst.update(zip(SMALL, arrs))

    return (loss, dx, *[g[k] for k in ORDER], *[delta[k] for k in ORDER], *[new_m[k] for k in ORDER],
            *[new_v[k] for k in ORDER])
```

```python
import functools

import numpy as np
import jax
import jax.numpy as jnp
from jax import lax
from jax.experimental import pallas as pl
from jax.experimental.pallas import tpu as pltpu

F32 = jnp.float32
BF16 = jnp.bfloat16

D_MODEL = 1024
GRID_W = 64
HEAD_DIM = 64
ATTN_W = 512
CONV_W = 256
POOL_W = 256
ATTN_HEADS = 8
KV_HEADS = 2
GROUP = ATTN_HEADS // KV_HEADS
KV_W = KV_HEADS * HEAD_DIM
IN_W = ATTN_W + 2 * KV_W + 2 * CONV_W + POOL_W
WINDOW = 128
Q_BLOCK = 128
SPAN = Q_BLOCK + 2 * WINDOW
CONV_KERNEL = 31
POOL_WINDOWS = (2, 4, 8, 16)
POOL_GROUP = 64
ROPE_BASE = 10000.0
D_FF = 2816
EPS = 1e-6
NEG = -1e30
N_DEV = 8
AXES = ("x", "y", "c")

ADAM_LR = 0.001
ADAM_B1 = 0.9
ADAM_B2 = 0.999
ADAM_EPS = 1e-08
ADAM_WD = 0.01
ADAM_STEP = 10

VMEM_LIMIT = 56 * 1024 * 1024
HALO = 16
SEQ_CHUNK = 512
CONV_BWD_CHUNK = 256


def _params(*sem):
    return pltpu.CompilerParams(dimension_semantics=sem, vmem_limit_bytes=VMEM_LIMIT)


def _tile(dim, target):
    if dim <= target:
        return dim
    t = (target // 128) * 128
    while t >= 128:
        if dim % t == 0:
            return t
        t -= 128
    raise ValueError(f"no tile for {dim}")


MM_VMEM_BUDGET = 44 * 1024 * 1024
DW_TN = 256


def _dot(a, b):
    return lax.dot_general(a.astype(BF16), b.astype(BF16), (((1,), (0,)), ((), ())), preferred_element_type=F32)


def _mm_vmem(tm, tn, tk, whole, a_bytes, b_bytes, o_bytes):
    return 2 * (tm * tk * a_bytes + tk * tn * b_bytes + tm * tn * o_bytes) + (0 if whole else tm * tn * 4)


def _mm(a, b, *, name, out_dtype=F32, tm=1408, tn=512, trans_b=False, out_block=None):
    M, K = a.shape
    N, K2 = b.shape if trans_b else b.shape[::-1]
    assert K == K2, (a.shape, b.shape)
    tm = _tile(M, tm)
    tn = _tile(N, tn)
    sizes = (a.dtype.itemsize, b.dtype.itemsize, jnp.dtype(out_dtype).itemsize)
    tk = next(t for t in range(K, 0, -128) if K % t == 0 and _mm_vmem(tm, tn, t, t == K, *sizes) <= MM_VMEM_BUDGET)
    nk = K // tk

    def body(a_ref, b_ref, o_ref, *scratch):
        part = (_dot_nt if trans_b else _dot)(a_ref[...], b_ref[...])
        if nk == 1:
            o_ref[...] = part.astype(o_ref.dtype)
        else:
            acc_ref, = scratch
            k = pl.program_id(2)

            @pl.when(k == 0)
            def _():
                acc_ref[...] = part

            @pl.when(k > 0)
            def _():
                acc_ref[...] += part

            @pl.when(k == nk - 1)
            def _():
                o_ref[...] = acc_ref[...].astype(o_ref.dtype)

    return pl.pallas_call(
        body, name=name, grid=(M // tm, N // tn, nk),
        in_specs=[pl.BlockSpec((tm, tk), lambda i, j, k: (i, k)),
                  pl.BlockSpec((tn, tk), lambda i, j, k: (j, k)) if trans_b else pl.BlockSpec((tk, tn), lambda i, j, k: (k, j))],
        out_specs=pl.BlockSpec((tm, tn), (lambda i, j, k: (i, j)) if out_block is None else (lambda i, j, k: (i, out_block(j)))),
        out_shape=jax.ShapeDtypeStruct((M, N), out_dtype),
        scratch_shapes=[pltpu.VMEM((tm, tn), F32)] if nk > 1 else [],
        compiler_params=_params("parallel", "parallel", "arbitrary"),
    )(a, b)


FF_TILE = 256


FF_TILES = D_FF // FF_TILE


def _natural_block(j):
    return (j % 2) * FF_TILES + j // 2


def _interleave_ffn(w, *, name):
    R, C = w.shape

    def body(w_ref, o_ref):
        o_ref[...] = w_ref[...]

    return pl.pallas_call(
        body, name=name, grid=(2 * FF_TILES,),
        in_specs=[pl.BlockSpec((R, FF_TILE), lambda j: (0, _natural_block(j)))],
        out_specs=pl.BlockSpec((R, FF_TILE), lambda j: (0, j)),
        out_shape=jax.ShapeDtypeStruct((R, C), w.dtype),
        compiler_params=_params("parallel"),
    )(w)


def _swiglu(gu):
    g, u = gu[:, :FF_TILE], gu[:, FF_TILE:]
    return g * jax.nn.sigmoid(g) * u


EPILOGUE_SPLIT = 1
FF_ROWS = 4096


def _mm_swiglu(a, w_il, *, name, tm=FF_ROWS, exch=None, split=EPILOGUE_SPLIT):
    M, K = a.shape
    tm = _tile(M, tm)
    rc = tm // split

    def body(a_ref, b_ref, gu_ref, act_ref, act_t_ref):
        b = b_ref[...]
        parts = [_dot(a_ref[r0:r0 + rc, :], b) for r0 in range(0, tm, rc)]
        for r0, gu in zip(range(0, tm, rc), parts):
            gu_ref[r0:r0 + rc, :] = gu.astype(gu_ref.dtype)
            act = _swiglu(gu)
            act_ref[r0:r0 + rc, :] = act.astype(act_ref.dtype)
            act_t_ref[:, r0:r0 + rc] = act.T.astype(act_t_ref.dtype)

    grid = (M // tm, D_FF // FF_TILE)
    body, x_in, x_out, x_shapes, x_sems = _riding(body, exch, 2, 3, grid)
    return pl.pallas_call(
        body, name=name, grid=grid,
        in_specs=[pl.BlockSpec((tm, K), lambda i, j: (i, 0)), pl.BlockSpec((K, 2 * FF_TILE), lambda i, j: (0, j))] + x_in,
        out_specs=[pl.BlockSpec((tm, 2 * FF_TILE), lambda i, j: (i, j)), pl.BlockSpec((tm, FF_TILE), lambda i, j: (i, j)),
                   pl.BlockSpec((FF_TILE, tm), lambda i, j: (j, i))] + x_out,
        out_shape=[jax.ShapeDtypeStruct((M, 2 * D_FF), BF16), jax.ShapeDtypeStruct((M, D_FF), BF16),
                   jax.ShapeDtypeStruct((D_FF, M), BF16)] + x_shapes,
        scratch_shapes=x_sems,
        compiler_params=_params("arbitrary", "arbitrary") if exch else _params("parallel", "parallel"),
    )(a, w_il, *(exch[0] if exch else []))


def _mm_dswiglu(dy, w_out, gu, *, name, tm=FF_ROWS, split=EPILOGUE_SPLIT):
    M, K = dy.shape
    tm = _tile(M, tm)
    rc = tm // split

    def body(dy_ref, b_ref, gu_ref, o_ref):
        b = b_ref[...]
        parts = [_dot_nt(dy_ref[r0:r0 + rc, :], b) for r0 in range(0, tm, rc)]
        for r0, dact in zip(range(0, tm, rc), parts):
            g = gu_ref[r0:r0 + rc, :FF_TILE].astype(F32)
            u = gu_ref[r0:r0 + rc, FF_TILE:].astype(F32)
            sig = jax.nn.sigmoid(g)
            silu = g * sig
            o_ref[r0:r0 + rc, :FF_TILE] = (dact * u * (sig + silu * (1.0 - sig))).astype(o_ref.dtype)
            o_ref[r0:r0 + rc, FF_TILE:] = (dact * silu).astype(o_ref.dtype)

    return pl.pallas_call(
        body, name=name, grid=(M // tm, D_FF // FF_TILE),
        in_specs=[pl.BlockSpec((tm, K), lambda i, j: (i, 0)), pl.BlockSpec((FF_TILE, K), lambda i, j: (j, 0)),
                  pl.BlockSpec((tm, 2 * FF_TILE), lambda i, j: (i, j))],
        out_specs=pl.BlockSpec((tm, 2 * FF_TILE), lambda i, j: (i, j)),
        out_shape=jax.ShapeDtypeStruct((M, 2 * D_FF), BF16),
        compiler_params=_params("parallel", "parallel"),
    )(dy, w_out, gu)


def _rope_tables(n):
    rows = n // GRID_W
    row = jnp.repeat(jnp.arange(rows), GRID_W).astype(F32)
    col = jnp.tile(jnp.arange(GRID_W), rows).astype(F32)
    half = HEAD_DIM // 2
    inv = ROPE_BASE ** (-jnp.arange(0, half, 2, dtype=F32) / half)
    ar = row[:, None] * inv
    ac = col[:, None] * inv
    ang = jnp.concatenate([ar, ar, ac, ac], axis=-1)
    return jnp.cos(ang), jnp.sin(ang)


def _rot_half(x):
    w = x.shape[-1]
    lane = lax.broadcasted_iota(jnp.int32, x.shape, 1)
    up = pltpu.roll(x, w - 16, 1)
    down = pltpu.roll(x, 16, 1)
    return jnp.where((lane & 16) == 0, -up, down)


def _rope(x, cos, sin):
    return x * cos + _rot_half(x) * sin


def _rope_bwd(d, cos, sin):
    return d * cos - _rot_half(d * sin)


def _dot_nt(a, b):
    return lax.dot_general(a.astype(BF16), b.astype(BF16), (((1,), (1,)), ((), ())), preferred_element_type=F32)


def _softmax_sink(s, sink_rows):
    mx = jnp.maximum(jnp.max(s, axis=1, keepdims=True), sink_rows)
    e = jnp.exp(s - mx)
    es = jnp.exp(sink_rows - mx)
    inv = 1.0 / (jnp.sum(e, axis=1, keepdims=True) + es)
    return e * inv, es * inv


def _attn_operands(q_ref, k_ref, v_ref, kc_ref, vc_ref, cq_ref, sq_ref, ck_ref, sk_ref, i, n, n_ctx, local):
    q = q_ref[...]
    k_all, v_all, bias, s0, ck, sk = kc_ref[...], vc_ref[...], None, None, None, None
    if local:
        start, s0 = _span_start(i, n)
        ck = ck_ref[pl.ds(s0, SPAN), :]
        sk = sk_ref[pl.ds(s0, SPAN), :]
        q = _rope(q, cq_ref[...], sq_ref[...])
        k_all = jnp.concatenate([k_all, _rope(k_ref[pl.ds(s0, SPAN), :], ck, sk)], axis=0)
        v_all = jnp.concatenate([v_all, v_ref[pl.ds(s0, SPAN), :]], axis=0)
        bias = _window_bias(start, s0, n_ctx)
    q = (q * (HEAD_DIM ** -0.5)).astype(BF16)
    return q, k_all.astype(BF16), v_all.astype(BF16), bias, s0, ck, sk


def _stack_heads(x, kh):
    return jnp.concatenate([x[:, (GROUP * kh + g) * HEAD_DIM:(GROUP * kh + g + 1) * HEAD_DIM] for g in range(GROUP)], axis=0)


def _sink_rows(sink, kh):
    return jnp.concatenate([jnp.broadcast_to(sink[:, GROUP * kh + g:GROUP * kh + g + 1], (Q_BLOCK, 1)) for g in range(GROUP)], axis=0)


def _window_bias(start, s0, n_ctx):
    r = lax.broadcasted_iota(jnp.int32, (Q_BLOCK, n_ctx + SPAN), 0)
    c = lax.broadcasted_iota(jnp.int32, (Q_BLOCK, n_ctx + SPAN), 1)
    ok = (c < n_ctx) | (jnp.abs(start - s0 + r - (c - n_ctx)) <= WINDOW)
    return jnp.concatenate([jnp.where(ok, 0.0, NEG).astype(F32)] * GROUP, axis=0)


def _span_start(i, n):
    start = i * Q_BLOCK
    s0 = jnp.clip(start - WINDOW, 0, n - SPAN)
    return start, pl.multiple_of(s0, Q_BLOCK)


def _riding(body, exch, n_in, n_out, grid):
    if exch is None:
        return body, [], [], [], []
    arrs, scatter = exch
    k = len(arrs)

    def wrapped(*refs):
        ins, xin = refs[:n_in], refs[n_in:n_in + k]
        outs, xout = refs[n_in + k:n_in + k + n_out], refs[n_in + k + n_out:n_in + 2 * k + n_out]
        sems = refs[n_in + 2 * k + n_out:]
        b, i = pl.program_id(0), pl.program_id(1)

        @pl.when((b == 0) & (i == 0))
        def _():
            _exch_start(xin, xout, sems, scatter)

        body(*ins, *outs)

        @pl.when((b == grid[0] - 1) & (i == grid[1] - 1))
        def _():
            _exch_wait(xin, xout, sems, scatter)

    any_spec = pl.BlockSpec(memory_space=pl.ANY)
    return wrapped, [any_spec] * k, [any_spec] * k, _exch_out_shapes(arrs, scatter), _exch_sems(k)


def _attn_fwd(u, kvc, sink, cos, sin, *, local, name, exch=None):
    B, n, _ = u.shape
    n_ctx = kvc.shape[1]
    nb = n // Q_BLOCK
    assert (not local) or n >= SPAN

    def body(q_ref, k_ref, v_ref, kc_ref, vc_ref, sink_ref, cq_ref, sq_ref, ck_ref, sk_ref, o_ref):
        q, k_all, v_all, bias, _, _, _ = _attn_operands(q_ref, k_ref, v_ref, kc_ref, vc_ref, cq_ref, sq_ref, ck_ref, sk_ref,
                                                        pl.program_id(1), n, n_ctx, local)
        sink_v = sink_ref[...]
        sl = lambda kh: slice(kh * HEAD_DIM, (kh + 1) * HEAD_DIM)
        ss = [_dot_nt(_stack_heads(q, kh), k_all[:, sl(kh)]) for kh in range(KV_HEADS)]
        ps = [_softmax_sink(s if bias is None else s + bias, _sink_rows(sink_v, kh))[0].astype(BF16) for kh, s in enumerate(ss)]
        for kh, p in enumerate(ps):
            o = _dot(p, v_all[:, sl(kh)])
            for g in range(GROUP):
                h = GROUP * kh + g
                o_ref[:, h * HEAD_DIM:(h + 1) * HEAD_DIM] = o[g * Q_BLOCK:(g + 1) * Q_BLOCK, :].astype(o_ref.dtype)

    seq = lambda blk: pl.BlockSpec((None, n, KV_W), lambda b, i: (b, 0, blk))
    ctxs = lambda blk: pl.BlockSpec((None, n_ctx, KV_W), lambda b, i: (b, 0, blk))
    full = lambda a: pl.BlockSpec(a.shape, lambda b, i: (0,) * a.ndim)
    cos_q, sin_q = jnp.tile(cos, (1, ATTN_HEADS)), jnp.tile(sin, (1, ATTN_HEADS))
    cos_k, sin_k = jnp.tile(cos, (1, KV_HEADS)), jnp.tile(sin, (1, KV_HEADS))
    body, x_in, x_out, x_shapes, x_sems = _riding(body, exch, 10, 1, (B, nb))
    return pl.pallas_call(
        body, name=name, grid=(B, nb),
        in_specs=[pl.BlockSpec((None, Q_BLOCK, ATTN_W), lambda b, i: (b, i, 0)),
                  seq(ATTN_W // KV_W), seq(ATTN_W // KV_W + 1), ctxs(0), ctxs(1), full(sink),
                  pl.BlockSpec((Q_BLOCK, ATTN_W), lambda b, i: (i, 0)), pl.BlockSpec((Q_BLOCK, ATTN_W), lambda b, i: (i, 0)),
                  full(cos_k), full(sin_k)] + x_in,
        out_specs=[pl.BlockSpec((None, Q_BLOCK, ATTN_W), lambda b, i: (b, i, 0))] + x_out,
        out_shape=[jax.ShapeDtypeStruct((B, n, ATTN_W), BF16)] + x_shapes,
        scratch_shapes=x_sems,
        compiler_params=_params("arbitrary", "arbitrary"),
    )(u, u, u, kvc, kvc, sink, cos_q, sin_q, cos_k, sin_k, *(exch[0] if exch else []))


def _attn_bwd(u, kvc, sink, cos, sin, do_src, do_blk, *, local, name, exch=None):
    B, n, _ = u.shape
    n_ctx = kvc.shape[1]
    nb = n // Q_BLOCK

    def body(q_ref, k_ref, v_ref, kc_ref, vc_ref, sink_ref, cq_ref, sq_ref, ck_ref, sk_ref, do_ref,
             dq_ref, dk_ref, dv_ref, dkc_ref, dvc_ref, dsink_ref):
        b = pl.program_id(0)
        i = pl.program_id(1)

        @pl.when(i == 0)
        def _():
            dk_ref[...] = jnp.zeros_like(dk_ref)
            dv_ref[...] = jnp.zeros_like(dv_ref)
            dkc_ref[...] = jnp.zeros_like(dkc_ref)
            dvc_ref[...] = jnp.zeros_like(dvc_ref)

        @pl.when((i == 0) & (b == 0))
        def _():
            dsink_ref[...] = jnp.zeros_like(dsink_ref)

        q, k_all, v_all, bias, s0, ck, sk = _attn_operands(q_ref, k_ref, v_ref, kc_ref, vc_ref, cq_ref, sq_ref, ck_ref, sk_ref,
                                                           i, n, n_ctx, local)
        do = do_ref[...].astype(BF16)
        sink_v = sink_ref[...]
        sl = lambda kh: slice(kh * HEAD_DIM, (kh + 1) * HEAD_DIM)
        heads = range(KV_HEADS)
        q_st = [_stack_heads(q, kh) for kh in heads]
        do_st = [_stack_heads(do, kh) for kh in heads]
        ss = [_dot_nt(q_st[kh], k_all[:, sl(kh)]) for kh in heads]
        dps = [_dot_nt(do_st[kh], v_all[:, sl(kh)]) for kh in heads]
        p_bf, ds_bf = [], []
        dsink = jnp.zeros((1, ATTN_HEADS), F32)
        lane8 = lax.broadcasted_iota(jnp.int32, (1, ATTN_HEADS), 1)
        for kh in heads:
            p, p_sink = _softmax_sink(ss[kh] if bias is None else ss[kh] + bias, _sink_rows(sink_v, kh))
            delta = jnp.sum(p * dps[kh], axis=1, keepdims=True)
            ds = p * (dps[kh] - delta)
            dsr = -(p_sink * delta)
            for g in range(GROUP):
                dsink = dsink + jnp.where(lane8 == GROUP * kh + g, jnp.sum(dsr[g * Q_BLOCK:(g + 1) * Q_BLOCK, :]), 0.0)
            p_bf.append(p.astype(BF16))
            ds_bf.append(ds.astype(BF16))
        over_rows = (((0,), (0,)), ((), ()))
        dks, dvs = [], []
        for kh in heads:
            dq_st = _dot(ds_bf[kh], k_all[:, sl(kh)]) * (HEAD_DIM ** -0.5)
            for g in range(GROUP):
                h = GROUP * kh + g
                dq_ref[:, h * HEAD_DIM:(h + 1) * HEAD_DIM] = dq_st[g * Q_BLOCK:(g + 1) * Q_BLOCK, :]
            dvs.append(lax.dot_general(p_bf[kh], do_st[kh], over_rows, preferred_element_type=F32))
            dks.append(lax.dot_general(ds_bf[kh], q_st[kh], over_rows, preferred_element_type=F32))
        dk_cat = jnp.concatenate(dks, axis=1)
        dv_cat = jnp.concatenate(dvs, axis=1)
        dsink_ref[...] += dsink
        dkc_ref[...] += dk_cat[:n_ctx, :]
        dvc_ref[...] += dv_cat[:n_ctx, :]
        if local:
            dq_ref[...] = _rope_bwd(dq_ref[...], cq_ref[...], sq_ref[...])
            dk_ref[pl.ds(s0, SPAN), :] += _rope_bwd(dk_cat[n_ctx:, :], ck, sk)
            dv_ref[pl.ds(s0, SPAN), :] += dv_cat[n_ctx:, :]

    seq = lambda blk: pl.BlockSpec((None, n, KV_W), lambda b, i: (b, 0, blk))
    ctxs = lambda blk: pl.BlockSpec((None, n_ctx, KV_W), lambda b, i: (b, 0, blk))
    full = lambda a: pl.BlockSpec(a.shape, lambda b, i: (0,) * a.ndim)
    qblk = lambda blk: pl.BlockSpec((None, Q_BLOCK, ATTN_W), lambda b, i: (b, i, blk))
    cos_q, sin_q = jnp.tile(cos, (1, ATTN_HEADS)), jnp.tile(sin, (1, ATTN_HEADS))
    cos_k, sin_k = jnp.tile(cos, (1, KV_HEADS)), jnp.tile(sin, (1, KV_HEADS))
    acc = lambda rows: pl.BlockSpec((None, rows, KV_W), lambda b, i: (b, 0, 0))
    body, x_in, x_out, x_shapes, x_sems = _riding(body, exch, 11, 6, (B, nb))
    return pl.pallas_call(
        body, name=name, grid=(B, nb),
        in_specs=[qblk(0), seq(ATTN_W // KV_W), seq(ATTN_W // KV_W + 1), ctxs(0), ctxs(1), full(sink),
                  pl.BlockSpec((Q_BLOCK, ATTN_W), lambda b, i: (i, 0)), pl.BlockSpec((Q_BLOCK, ATTN_W), lambda b, i: (i, 0)),
                  full(cos_k), full(sin_k), qblk(do_blk)] + x_in,
        out_specs=[qblk(0), acc(n), acc(n), acc(n_ctx), acc(n_ctx), pl.BlockSpec((1, ATTN_HEADS), lambda b, i: (0, 0))] + x_out,
        out_shape=[jax.ShapeDtypeStruct((B, n, ATTN_W), F32), jax.ShapeDtypeStruct((B, n, KV_W), F32),
                   jax.ShapeDtypeStruct((B, n, KV_W), F32), jax.ShapeDtypeStruct((B, n_ctx, KV_W), F32),
                   jax.ShapeDtypeStruct((B, n_ctx, KV_W), F32), jax.ShapeDtypeStruct((1, ATTN_HEADS), F32)] + x_shapes,
        scratch_shapes=x_sems,
        compiler_params=_params("arbitrary", "arbitrary"),
    )(u, u, u, kvc, kvc, sink, cos_q, sin_q, cos_k, sin_k, do_src, *(exch[0] if exch else []))


def _conv_chunk(s, n, a_ext, g_ext, dw, dw_b, ln_g, ln_b):
    del s, n
    acc = _conv_taps(a_ext, g_ext, dw, dw_b)
    return _conv_tail(acc, ln_g, ln_b), acc


def _conv_chunk_bwd(s, n, ext, pars, acc, do):
    del s, n
    dw, dw_b, ln_g, ln_b = pars
    _, tail_vjp = jax.vjp(_conv_tail, acc, ln_g, ln_b)
    dacc, dln_g, dln_b = tail_vjp(do)
    _, taps_vjp = jax.vjp(_conv_taps, *ext, dw, dw_b)
    return (*taps_vjp(dacc), dln_g, dln_b)


def _conv_tail(acc, ln_g, ln_b):
    mu = jnp.mean(acc, axis=-1, keepdims=True)
    var = jnp.mean(jnp.square(acc - mu), axis=-1, keepdims=True)
    hn = (acc - mu) * lax.rsqrt(var + EPS) * ln_g + ln_b
    return hn * jax.nn.sigmoid(hn)


def _conv_taps(a_ext, g_ext, dw, dw_b):
    r = a_ext.shape[0] - 2 * HALO
    h = a_ext * jax.nn.sigmoid(g_ext)
    acc = jnp.broadcast_to(dw_b, (r, CONV_W))
    first = HALO - CONV_KERNEL // 2
    span = r + 8 * ((first + CONV_KERNEL - 1) // 8)
    shifted = [h[b:b + span, :] for b in range(8)]
    for k in range(CONV_KERNEL):
        o = first + k
        acc = acc + shifted[o % 8][o - o % 8:o - o % 8 + r, :] * dw[k:k + 1, :]
    return acc


def _pool_chunk(s, n, p_ext, w_bd, scale):
    r = p_ext.shape[0] - 2 * HALO
    lane = lax.broadcasted_iota(jnp.int32, (1, POOL_W), 1)
    win = jnp.left_shift(2, lane // POOL_GROUP)
    half = win // 2
    acc = jnp.zeros((r, POOL_W), F32)
    for d in range(-(POOL_WINDOWS[-1] // 2), POOL_WINDOWS[-1] - POOL_WINDOWS[-1] // 2):
        inside = (d >= -half) & (d <= win - 1 - half)
        acc = acc + jnp.where(inside, p_ext[HALO + d:HALO + d + r, :], 0.0)
    t = s + lax.broadcasted_iota(jnp.int32, (r, 1), 0)
    lo = jnp.maximum(t - half, 0)
    hi = jnp.minimum(t + win - 1 - half, n - 1)
    y = acc / (hi - lo + 1).astype(F32) - p_ext[HALO:HALO + r, :]
    out = lax.dot_general(y.astype(BF16), w_bd.astype(BF16), (((1,), (0,)), ((), ())), preferred_element_type=F32)
    return out * scale


def _seq_specs(rows, params):
    specs = [pl.BlockSpec((None, a.shape[1], w), functools.partial(lambda b, blk: (b, 0, blk), blk=blk)) for a, w, blk in rows]
    specs += [pl.BlockSpec(p.shape, functools.partial(lambda b, nd: (0,) * nd, nd=p.ndim)) for p in params]
    return specs


def _fill_padded(pad_ref, row_ref, n):
    w = pad_ref.shape[1]
    pad_ref[pl.ds(0, HALO), :] = jnp.zeros((HALO, w), F32)
    pad_ref[pl.ds(HALO + n, HALO), :] = jnp.zeros((HALO, w), F32)
    pad_ref[pl.ds(HALO, n), :] = row_ref[...]


def _seq_fwd(fn, rows, params, out_w, *, name, chunk=SEQ_CHUNK, aux_w=None):
    B, n = rows[0][0].shape[:2]
    r = min(chunk, n)
    nr, npar = len(rows), len(params)
    nout = 1 if aux_w is None else 2

    def body(*refs):
        row_refs, par_refs = refs[:nr], refs[nr:nr + npar]
        out_refs, pads = refs[nr + npar:nr + npar + nout], refs[nr + npar + nout:]
        for rr, p in zip(row_refs, pads):
            _fill_padded(p, rr, n)
        pars = [p[...] for p in par_refs]

        def chunk(ci, carry):
            s = pl.multiple_of(ci * r, r)
            ext = [p[pl.ds(s, r + 2 * HALO), :] for p in pads]
            res = fn(s, n, *ext, *pars)
            for o_ref, v in zip(out_refs, res if nout == 2 else (res,)):
                o_ref[pl.ds(s, r), :] = v.astype(o_ref.dtype)
            return carry

        lax.fori_loop(0, n // r, chunk, 0)

    widths = [(out_w, BF16)] + ([] if aux_w is None else [(aux_w, F32)])
    res = pl.pallas_call(
        body, name=name, grid=(B,),
        in_specs=_seq_specs(rows, params),
        out_specs=[pl.BlockSpec((None, n, w), lambda b: (b, 0, 0)) for w, _ in widths],
        out_shape=[jax.ShapeDtypeStruct((B, n, w), dt) for w, dt in widths],
        scratch_shapes=[pltpu.VMEM((n + 2 * HALO, w), F32) for _, w, _ in rows],
        compiler_params=_params("parallel"),
    )(*[a for a, _, _ in rows], *params)
    return res[0] if aux_w is None else res


def _seq_bwd(fn, rows, params, dout, *, name, chunk=SEQ_CHUNK, aux=None):
    B, n = rows[0][0].shape[:2]
    r = min(chunk, n)
    nr, npar = len(rows), len(params)
    naux = 0 if aux is None else 1

    def body(*refs):
        row_refs, par_refs, do_ref = refs[:nr], refs[nr:nr + npar], refs[nr + npar]
        aux_refs = refs[nr + npar + 1:nr + npar + 1 + naux]
        outs = refs[nr + npar + 1 + naux:]
        drow_refs, dpar_refs = outs[:nr], outs[nr:nr + npar]
        scratch = outs[nr + npar:]
        pads, dpads = scratch[:nr], scratch[nr:]
        for rr, p, dp in zip(row_refs, pads, dpads):
            _fill_padded(p, rr, n)
            dp[...] = jnp.zeros_like(dp)

        @pl.when(pl.program_id(0) == 0)
        def _():
            for d in dpar_refs:
                d[...] = jnp.zeros_like(d)

        pars = [p[...] for p in par_refs]

        def chunk(ci, carry):
            s = pl.multiple_of(ci * r, r)
            ext = [p[pl.ds(s, r + 2 * HALO), :] for p in pads]
            do = do_ref[pl.ds(s, r), :]
            if aux is None:
                _, vjp = jax.vjp(functools.partial(fn, s, n), *ext, *pars)
                grads = vjp(do)
            else:
                grads = fn(s, n, ext, pars, aux_refs[0][pl.ds(s, r), :], do)
            for dp, g in zip(dpads, grads[:nr]):
                dp[pl.ds(s, r + 2 * HALO), :] += g
            for d, g in zip(dpar_refs, grads[nr:]):
                d[...] += g
            return carry

        lax.fori_loop(0, n // r, chunk, 0)
        for d, dp in zip(drow_refs, dpads):
            d[...] = dp[pl.ds(HALO, n), :].astype(d.dtype)

    da, dw_, dblk = dout
    auxs = [] if aux is None else [aux]
    return pl.pallas_call(
        body, name=name, grid=(B,),
        in_specs=_seq_specs(rows, params) + [pl.BlockSpec((None, n, dw_), lambda b: (b, 0, dblk))]
        + [pl.BlockSpec((None, n, a.shape[2]), lambda b: (b, 0, 0)) for a in auxs],
        out_specs=[pl.BlockSpec((None, n, w), lambda b: (b, 0, 0)) for _, w, _ in rows]
        + [pl.BlockSpec(p.shape, functools.partial(lambda b, nd: (0,) * nd, nd=p.ndim)) for p in params],
        out_shape=[jax.ShapeDtypeStruct((B, n, w), BF16) for _, w, _ in rows]
        + [jax.ShapeDtypeStruct(p.shape, F32) for p in params],
        scratch_shapes=[pltpu.VMEM((n + 2 * HALO, w), F32) for _, w, _ in rows] * 2,
        compiler_params=_params("arbitrary"),
    )(*[a for a, _, _ in rows], *params, da, *auxs)


_CONV_A_BLK = (ATTN_W + 2 * KV_W) // CONV_W
_CONV_G_BLK = _CONV_A_BLK + 1
_POOL_BLK = _CONV_A_BLK + 2


def _mixer_fwd(tag, u, kvc, margs, local, exch=None):
    sink, dw, dw_b, ln_g, ln_b, w_bd, scale = margs
    cos, sin = _rope_tables(max(u.shape[1], GRID_W))
    attn, *got = _attn_fwd(u, kvc, sink, cos, sin, local=local, name=f"{tag}_attn_fwd", exch=exch)
    conv, conv_acc = _seq_fwd(_conv_chunk, [(u, CONV_W, _CONV_A_BLK), (u, CONV_W, _CONV_G_BLK)], [dw, dw_b, ln_g, ln_b],
                              CONV_W, name=f"{tag}_conv_fwd", aux_w=CONV_W)
    pool = _seq_fwd(_pool_chunk, [(u, POOL_W, _POOL_BLK)], [w_bd, scale], POOL_W, name=f"{tag}_pool_fwd")
    return jnp.concatenate([attn, conv, pool], axis=-1), conv_acc, got


def _mixer_bwd(tag, u, kvc, conv_acc, margs, dmix, local, exch=None):
    sink, dw, dw_b, ln_g, ln_b, w_bd, scale = margs
    cos, sin = _rope_tables(max(u.shape[1], GRID_W))
    dq, dk, dv, dkc, dvc, dsink, *got = _attn_bwd(u, kvc, sink, cos, sin, dmix, 0, local=local, name=f"{tag}_attn_bwd",
                                                  exch=exch)
    da, dg, ddw, ddw_b, dln_g, dln_b = _seq_bwd(
        _conv_chunk_bwd, [(u, CONV_W, _CONV_A_BLK), (u, CONV_W, _CONV_G_BLK)], [dw, dw_b, ln_g, ln_b],
        (dmix, CONV_W, ATTN_W // CONV_W), name=f"{tag}_conv_bwd", chunk=CONV_BWD_CHUNK, aux=conv_acc)
    dpu, dw_bd, dscale = _seq_bwd(_pool_chunk, [(u, POOL_W, _POOL_BLK)], [w_bd, scale],
                                  (dmix, POOL_W, (ATTN_W + CONV_W) // POOL_W), name=f"{tag}_pool_bwd")
    return (dq, dk, dv, da, dg, dpu), (dkc, dvc), (dsink, ddw, ddw_b, dln_g, dln_b, dw_bd, dscale), got


def _row_specs(arrs, kinds, tr):
    specs = []
    for a, kind in zip(arrs, kinds):
        if kind == "row":
            specs.append(pl.BlockSpec((None, tr, a.shape[2]), lambda b, j: (b, j, 0)))
        elif kind == "batch":
            specs.append(pl.BlockSpec((None, 1, a.shape[2]), lambda b, j: (b, 0, 0)))
        else:
            specs.append(pl.BlockSpec(a.shape, functools.partial(lambda b, j, nd: (0,) * nd, nd=a.ndim)))
    return specs


def _rowwise_fwd(fn, ins, kinds, outs, tr, *, name, transposed=None):
    B, n = ins[0].shape[:2]
    ni, no = len(ins), len(outs)
    nj = n // tr

    def body(*refs):
        res = fn(*[r[...] for r in refs[:ni]])
        for o, v in zip(refs[ni:ni + no], res):
            o[...] = v.astype(o.dtype)
        if transposed is not None:
            refs[ni + no][...] = res[transposed].T.astype(refs[ni + no].dtype)

    out_specs = [pl.BlockSpec((None, tr, w), lambda b, j: (b, j, 0)) for w, _ in outs]
    out_shape = [jax.ShapeDtypeStruct((B, n, w), dt) for w, dt in outs]
    if transposed is not None:
        w, dt = outs[transposed]
        out_specs.append(pl.BlockSpec((w, tr), lambda b, j: (0, b * nj + j)))
        out_shape.append(jax.ShapeDtypeStruct((w, B * n), dt))
    return pl.pallas_call(
        body, name=name, grid=(B, nj),
        in_specs=_row_specs(ins, kinds, tr), out_specs=out_specs, out_shape=out_shape,
        compiler_params=_params("parallel", "parallel"),
    )(*ins)


def _rowwise_bwd(fn, ins, kinds, gdtypes, cts, tr, *, name):
    B, n = ins[0].shape[:2]
    ni, nc = len(ins), len(cts)
    idx = list(range(ni))

    def body(*refs):
        in_refs, ct_refs, out_refs = refs[:ni], refs[ni:ni + nc], refs[ni + nc:]
        b, j = pl.program_id(0), pl.program_id(1)
        _, vjp = jax.vjp(fn, *[r[...].astype(F32) for r in in_refs])
        grads = vjp(tuple(c[...].astype(F32) for c in ct_refs))
        for o, i in zip(out_refs, idx):
            g = grads[i]
            if kinds[i] == "row":
                o[...] = g.astype(o.dtype)
            else:
                first = (j == 0) if kinds[i] == "batch" else ((j == 0) & (b == 0))

                @pl.when(first)
                def _(o=o, g=g):
                    o[...] = g

                @pl.when(jnp.logical_not(first))
                def _(o=o, g=g):
                    o[...] += g

    specs = _row_specs(ins, kinds, tr)
    return pl.pallas_call(
        body, name=name, grid=(B, n // tr),
        in_specs=specs + [pl.BlockSpec((None, tr, c.shape[2]), lambda b, j: (b, j, 0)) for c in cts],
        out_specs=[specs[i] for i in idx],
        out_shape=[jax.ShapeDtypeStruct(ins[i].shape, gdtypes[i]) for i in idx],
        compiler_params=_params("arbitrary", "arbitrary"),
    )(*ins, *cts)


ROW_TILE = 512


def _rms_mod(x, g, sc, sh):
    y = x * lax.rsqrt(jnp.mean(x * x, axis=-1, keepdims=True) + EPS)
    return (y * g) * (1.0 + sc) + sh


def _norm_tile(x, g, sc, sh):
    return x, _rms_mod(x, g, sc, sh)


def _res_norm_tile(xb, y, gate, g, sc, sh):
    x = xb + gate * y
    return x, _rms_mod(x, g, sc, sh)


_NORM_KINDS = ("row", "glob", "batch", "batch")
_RES_NORM_KINDS = ("row", "row", "batch", "glob", "batch", "batch")


def _norm_fwd(tag, st, g, sc, sh):
    xb, y, gate = st
    tr = min(ROW_TILE, xb.shape[1])
    d = xb.shape[2]
    if y is None:
        h, h_t = _rowwise_fwd(lambda *a: (_rms_mod(*a),), [xb, g, sc, sh], _NORM_KINDS, [(d, BF16)], tr,
                              name=f"{tag}_fwd", transposed=0)
        return xb, h, h_t
    return _rowwise_fwd(_res_norm_tile, [xb, y, gate, g, sc, sh], _RES_NORM_KINDS, [(d, F32), (d, BF16)], tr,
                        name=f"{tag}_fwd", transposed=1)


def _norm_bwd(tag, st, g, sc, sh, dx, dh):
    xb, y, gate = st
    tr = min(ROW_TILE, xb.shape[1])
    if y is None:
        dxb, dg, dsc, dsh = _rowwise_bwd(_norm_tile, [xb, g, sc, sh], _NORM_KINDS, [F32] * 4, [dx, dh], tr, name=f"{tag}_bwd")
        return dxb, None, None, dg, dsc, dsh
    return tuple(_rowwise_bwd(_res_norm_tile, [xb, y, gate, g, sc, sh], _RES_NORM_KINDS, [F32, BF16, F32, F32, F32, F32],
                              [dx, dh], tr, name=f"{tag}_bwd"))


def _loss_head(st, final_g, target, *, name):
    xb, y, gate = st
    B, n, d = xb.shape
    tr = min(ROW_TILE, n)

    def tile_loss(xv, yv, gt, g, t):
        x = xv + gt * yv
        out = x * lax.rsqrt(jnp.mean(x * x, axis=-1, keepdims=True) + EPS) * g
        return 0.5 * jnp.sum(jnp.mean(jnp.square(out - t), axis=-1))

    def body(x_ref, y_ref, gate_ref, g_ref, t_ref, loss_ref, dx_ref, dy_ref, dgate_ref, dg_ref):
        b, j = pl.program_id(0), pl.program_id(1)
        val, (dx, dy, dgate, dg) = jax.value_and_grad(tile_loss, argnums=(0, 1, 2, 3))(
            x_ref[...], y_ref[...], gate_ref[...], g_ref[...], t_ref[...])
        dx_ref[...] = dx
        dy_ref[...] = dy.astype(dy_ref.dtype)

        @pl.when(j == 0)
        def _():
            loss_ref[...] = jnp.zeros_like(loss_ref)
            dgate_ref[...] = jnp.zeros_like(dgate_ref)

        @pl.when((j == 0) & (b == 0))
        def _():
            dg_ref[...] = jnp.zeros_like(dg_ref)

        loss_ref[...] += jnp.full(loss_ref.shape, val, F32)
        dgate_ref[...] += dgate
        dg_ref[...] += dg

    row = pl.BlockSpec((None, tr, d), lambda b, j: (b, j, 0))
    per_sample = pl.BlockSpec((None, 1, d), lambda b, j: (b, 0, 0))
    whole = pl.BlockSpec((1, d), lambda b, j: (0, 0))
    return pl.pallas_call(
        body, name=name, grid=(B, n // tr),
        in_specs=[row, row, per_sample, whole, row],
        out_specs=[pl.BlockSpec((None, 1, 128), lambda b, j: (b, 0, 0)), row, row, per_sample, whole],
        out_shape=[jax.ShapeDtypeStruct((B, 1, 128), F32), jax.ShapeDtypeStruct((B, n, d), F32),
                   jax.ShapeDtypeStruct((B, n, d), BF16), jax.ShapeDtypeStruct((B, 1, d), F32), jax.ShapeDtypeStruct((1, d), F32)],
        compiler_params=_params("arbitrary", "arbitrary"),
    )(xb, y, gate, final_g, target)


def _exchange(arrs, *, scatter, name):
    k = len(arrs)

    def body(*refs):
        ins, outs, sems = refs[:k], refs[k:2 * k], refs[2 * k:]
        _exch_start(ins, outs, sems, scatter)
        _exch_wait(ins, outs, sems, scatter)

    any_spec = pl.BlockSpec(memory_space=pl.ANY)
    return pl.pallas_call(
        body, name=name,
        in_specs=[any_spec] * k, out_specs=[any_spec] * k,
        out_shape=_exch_out_shapes(arrs, scatter), scratch_shapes=_exch_sems(k),
        compiler_params=pltpu.CompilerParams(has_side_effects=True),
    )(*arrs)


def _exch_flags(scatter, k):
    return [scatter] * k if isinstance(scatter, bool) else list(scatter)


def _exch_out_shapes(arrs, scatter):
    return [jax.ShapeDtypeStruct(a.shape if f else (N_DEV,) + a.shape, a.dtype)
            for a, f in zip(arrs, _exch_flags(scatter, len(arrs)))]


def _exch_sems(k):
    return [pltpu.SemaphoreType.DMA((k * (N_DEV - 1),)), pltpu.SemaphoreType.DMA((k * (N_DEV - 1),)),
            pltpu.SemaphoreType.DMA((k,))]


def _exch_copies(ins, outs, sems, scatter):
    send_sems, recv_sems, local_sems = sems
    x, y, c = lax.axis_index("x"), lax.axis_index("y"), lax.axis_index("c")
    me = 4 * x + 2 * y + c
    owns, sends, recvs = [], [], []
    flags = _exch_flags(scatter, len(ins))
    for a in range(len(ins)):
        scatter = flags[a]
        owns.append(pltpu.make_async_copy(ins[a].at[me] if scatter else ins[a], outs[a].at[me], local_sems.at[a]))
        for r in range(1, N_DEV):
            fx, fy, fc = (r >> 2) & 1, (r >> 1) & 1, r & 1
            px, py, pc = (x + fx) % 2, (y + fy) % 2, (c + fc) % 2
            peer = 4 * px + 2 * py + pc
            s = a * (N_DEV - 1) + r - 1
            mk = functools.partial(pltpu.make_async_remote_copy, src_ref=ins[a].at[peer] if scatter else ins[a],
                                   send_sem=send_sems.at[s], recv_sem=recv_sems.at[s],
                                   device_id=(px, py, pc), device_id_type=pl.DeviceIdType.MESH)
            sends.append(mk(dst_ref=outs[a].at[me]))
            recvs.append(mk(dst_ref=outs[a].at[peer]))
    return owns, sends, recvs


def _exch_start(ins, outs, sems, scatter):
    owns, sends, _ = _exch_copies(ins, outs, sems, scatter)
    for cp in owns + sends:
        cp.start()


def _exch_wait(ins, outs, sems, scatter):
    owns, sends, recvs = _exch_copies(ins, outs, sems, scatter)
    for rc in recvs:
        rc.wait_recv()
    for cp in sends:
        cp.wait_send()
    for own in owns:
        own.wait()


MOD_ROWS = 48


def _mod_tile(cc, w, b):
    s = cc * jax.nn.sigmoid(cc)
    return lax.dot_general(s.astype(BF16), w.astype(BF16), (((1,), (0,)), ((), ())), preferred_element_type=F32) + b


def _mod_fwd(cc, w_mod, b_shard, *, name):
    L, d, wcols = w_mod.shape

    def body(cc_ref, w_ref, b_ref, o_ref):
        o_ref[...] = _mod_tile(cc_ref[...], w_ref[...], b_ref[...])

    return pl.pallas_call(
        body, name=name, grid=(L,),
        in_specs=[pl.BlockSpec((MOD_ROWS, d), lambda l: (0, 0)), pl.BlockSpec((None, d, wcols), lambda l: (l, 0, 0)),
                  pl.BlockSpec((None, 1, wcols), lambda l: (l, 0, 0))],
        out_specs=pl.BlockSpec((None, MOD_ROWS, wcols), lambda l: (l, 0, 0)),
        out_shape=jax.ShapeDtypeStruct((L, MOD_ROWS, wcols), F32),
        compiler_params=_params("parallel"),
    )(cc, w_mod, b_shard)


def _mod_bwd(cc, w_mod, b_shard, dm, *, name):
    L, d, wcols = w_mod.shape

    def body(cc_ref, w_ref, b_ref, dm_ref, dcc_ref, dw_ref):
        _, vjp = jax.vjp(_mod_tile, cc_ref[...], w_ref[...], b_ref[...])
        dcc, dw, _ = vjp(dm_ref[...])
        dw_ref[...] = dw

        @pl.when(pl.program_id(0) == 0)
        def _():
            dcc_ref[...] = dcc

        @pl.when(pl.program_id(0) > 0)
        def _():
            dcc_ref[...] += dcc

    return pl.pallas_call(
        body, name=name, grid=(L,),
        in_specs=[pl.BlockSpec((MOD_ROWS, d), lambda l: (0, 0)), pl.BlockSpec((None, d, wcols), lambda l: (l, 0, 0)),
                  pl.BlockSpec((None, 1, wcols), lambda l: (l, 0, 0)), pl.BlockSpec((None, MOD_ROWS, wcols), lambda l: (l, 0, 0))],
        out_specs=[pl.BlockSpec((MOD_ROWS, d), lambda l: (0, 0)), pl.BlockSpec((None, d, wcols), lambda l: (l, 0, 0))],
        out_shape=[jax.ShapeDtypeStruct((MOD_ROWS, d), F32), jax.ShapeDtypeStruct((L, d, wcols), F32)],
        compiler_params=_params("arbitrary"),
    )(cc, w_mod, b_shard, dm)


def _sum_leading(a, *, name):
    K, R, C = a.shape
    tr = _tile8(R, 256)

    def body(a_ref, o_ref):
        acc = a_ref[0].astype(F32)
        for i in range(1, K):
            acc = acc + a_ref[i].astype(F32)
        o_ref[...] = acc

    return pl.pallas_call(
        body, name=name, grid=(R // tr,),
        in_specs=[pl.BlockSpec((K, tr, C), lambda i: (0, i, 0))],
        out_specs=pl.BlockSpec((tr, C), lambda i: (i, 0)),
        out_shape=jax.ShapeDtypeStruct((R, C), F32),
        compiler_params=_params("parallel"),
    )(a)


def _tile8(dim, target):
    if dim <= target:
        return dim
    t = (target // 8) * 8
    while t >= 8:
        if dim % t == 0:
            return t
        t -= 8
    raise ValueError(f"no row tile for {dim}")


def _adamw_math(g, w, m, v):
    m = ADAM_B1 * m + (1.0 - ADAM_B1) * g
    v = ADAM_B2 * v + (1.0 - ADAM_B2) * jnp.square(g)
    m_hat = m / (1.0 - ADAM_B1 ** ADAM_STEP)
    v_hat = v / (1.0 - ADAM_B2 ** ADAM_STEP)
    delta = -ADAM_LR * (m_hat / (jnp.sqrt(v_hat) + ADAM_EPS) + ADAM_WD * w)
    return delta, m, v


def _adamw(g, w, m, v, *, name, exch=None):
    L, R, C = w.shape
    parts = isinstance(g, (list, tuple))
    gs = list(g) if parts else [g]
    ng = len(gs)
    tr = _tile8(R, 256)

    def body(*refs):
        g_refs = refs[:ng]
        w_ref, m_ref, v_ref, go_ref, d_ref, mo_ref, vo_ref = refs[ng:]
        if parts:
            layer = pl.program_id(0)
            gv = None
            for li, g_ref in enumerate(g_refs):
                acc = g_ref[0].astype(F32)
                for i in range(1, N_DEV):
                    acc = acc + g_ref[i].astype(F32)
                gv = acc if gv is None else jnp.where(layer == li, acc, gv)
        else:
            gv = g_refs[0][...]
        go_ref[...] = gv
        d_ref[...], mo_ref[...], vo_ref[...] = _adamw_math(gv, w_ref[...], m_ref[...], v_ref[...])

    tile = pl.BlockSpec((None, tr, C), lambda l, i: (l, i, 0))
    g_specs = [pl.BlockSpec((N_DEV, tr, C), lambda l, i: (0, i, 0))] * ng if parts else [tile]
    grid = (L, R // tr)
    body, x_in, x_out, x_shapes, x_sems = _riding(body, exch, ng + 3, 4, grid)
    return pl.pallas_call(
        body, name=name, grid=grid,
        in_specs=g_specs + [tile, tile, tile] + x_in, out_specs=[tile] * 4 + x_out,
        out_shape=[jax.ShapeDtypeStruct((L, R, C), F32)] * 4 + x_shapes,
        scratch_shapes=x_sems,
        compiler_params=_params("arbitrary", "arbitrary") if exch else _params("parallel", "parallel"),
    )(*gs, w, m, v, *(exch[0] if exch else []))


def _adamw_small(gs, ws, ms, vs, *, name):
    k = len(ws)

    def body(*refs):
        g_refs, w_refs, m_refs, v_refs = refs[:k], refs[k:2 * k], refs[2 * k:3 * k], refs[3 * k:4 * k]
        d_refs, mo_refs, vo_refs = refs[4 * k:5 * k], refs[5 * k:6 * k], refs[6 * k:]
        for i in range(k):
            d_refs[i][...], mo_refs[i][...], vo_refs[i][...] = _adamw_math(g_refs[i][...], w_refs[i][...], m_refs[i][...],
                                                                         v_refs[i][...])

    shapes = [jax.ShapeDtypeStruct(a.shape, F32) for a in ws]
    out = pl.pallas_call(body, name=name, out_shape=shapes * 3, compiler_params=pltpu.CompilerParams(vmem_limit_bytes=VMEM_LIMIT))(
        *gs, *ws, *ms, *vs)
    return out[:k], out[k:2 * k], out[2 * k:]


def _block_diag(w):
    g, c, d = w.shape
    return (w[:, :, None, :] * jnp.eye(g, dtype=w.dtype)[:, None, :, None]).reshape(g * c, g * d)


def _diag_blocks(w_bd):
    g = POOL_W // POOL_GROUP
    return jnp.stack([w_bd[i * POOL_GROUP:(i + 1) * POOL_GROUP, i * POOL_GROUP:(i + 1) * POOL_GROUP] for i in range(g)])


def _flat(a):
    return a.reshape(-1, a.shape[-1])


def _mix_half_fwd(tag, st, mods, wl, kvc, *, local, kv_only, exch=None):
    sh1, sc1, g1 = mods[:3]
    B, n, d = st[0].shape
    x, h, h_t = _norm_fwd(f"{tag}_norm1", st, wl["n1"], sc1, sh1)
    if kv_only:
        kv = _mm(_flat(h), wl["w_in"][:, ATTN_W:ATTN_W + 2 * KV_W], name=f"{tag}_kv").reshape(B, n, 2 * KV_W)
        return None, dict(st=st, h_t=h_t, kvc=kv), []
    u = _mm(_flat(h), wl["w_in"], name=f"{tag}_in", tn=IN_W).reshape(B, n, IN_W)
    if not local:
        kvc = u[:, :, ATTN_W:ATTN_W + 2 * KV_W]
    mix, conv_acc, got = _mixer_fwd(f"{tag}_mix", u, kvc, wl["margs"], local, exch)
    y = _mm(_flat(mix), wl["w_out"], name=f"{tag}_out", tn=D_MODEL).reshape(B, n, d)
    return (x, y, g1), dict(st=st, h_t=h_t, u=u, kvc=kvc, mix=mix, conv_acc=conv_acc), got


def _ffn_half_fwd(tag, st2, mods, wl, exch=None):
    sh2, sc2, g2 = mods[3:]
    B, n, d = st2[0].shape
    x1, h2, h2_t = _norm_fwd(f"{tag}_norm2", st2, wl["n2"], sc2, sh2)
    gu, act, act_t, *got = _mm_swiglu(_flat(h2), wl["w_ffn_in"], name=f"{tag}_ffn_in", exch=exch)
    y2 = _mm(act, wl["w_ffn_out"], name=f"{tag}_ffn_out", tn=D_MODEL).reshape(B, n, d)
    return (x1, y2, g2), dict(st2=st2, h2_t=h2_t, gu=gu, act_t=act_t), got


def _ffn_half_bwd(tag, sv, mods, wl, dx1, dy2):
    sh2, sc2, _ = mods[3:]
    B, n, d = sv["st2"][0].shape
    gw = {}
    dy2f = _flat(dy2)
    gw["w_ffn_out"] = _mm(sv["act_t"], dy2f, out_dtype=BF16, name=f"{tag}_ffn_out_dw")
    dgu = _mm_dswiglu(dy2f, wl["w_ffn_out"], sv["gu"], name=f"{tag}_ffn_out_dx")
    dh2 = _mm(dgu, wl["w_ffn_in"], trans_b=True, name=f"{tag}_ffn_in_dx").reshape(B, n, d)
    gw["w_ffn_in"] = _mm(sv["h2_t"], dgu, out_dtype=BF16, tn=FF_TILE, out_block=_natural_block, name=f"{tag}_ffn_in_dw")
    dx, dy, dg1, gw["n2"], dsc2, dsh2 = _norm_bwd(f"{tag}_norm2", sv["st2"], wl["n2"], sc2, sh2, dx1, dh2)
    return (dx, dy, dg1), gw, dict(sh2=dsh2, sc2=dsc2)


def _mix_half_bwd(tag, sv, mods, wl, dx, dy, dkv_in, *, local, kv_only, exch=None):
    sh1, sc1, _ = mods[:3]
    B, n, d = sv["st"][0].shape
    gw = {}
    if kv_only:
        dkv = _flat(dkv_in).astype(BF16)
        dh = _mm(dkv, wl["w_in"][:, ATTN_W:ATTN_W + 2 * KV_W], trans_b=True, name=f"{tag}_kv_dx").reshape(B, n, d)
        gw["w_in_kv"] = _mm(sv["h_t"], dkv, out_dtype=BF16, name=f"{tag}_kv_dw")
        dxb, dy_prev, dgate_prev, gw["n1"], dsc1, dsh1 = _norm_bwd(f"{tag}_norm1", sv["st"], wl["n1"], sc1, sh1,
                                                                    jnp.zeros((B, n, d), F32), dh)
        return (dxb, dy_prev, dgate_prev), gw, dict(sh1=dsh1, sc1=dsc1), None, []

    dyf = _flat(dy)
    dmix = _mm(dyf, wl["w_out"], trans_b=True, name=f"{tag}_out_dx", tn=D_MODEL).reshape(B, n, d)
    gw["w_out"] = _mm(_flat(sv["mix"]).T, dyf, out_dtype=BF16, tn=DW_TN, name=f"{tag}_out_dw")
    (dq, dk, dv, da, dg, dpu), (dkc, dvc), gw["margs"], got = _mixer_bwd(f"{tag}_mix", sv["u"], sv["kvc"], sv["conv_acc"],
                                                                     wl["margs"], dmix, local, exch)
    if local:
        dkv_out = jnp.concatenate([dkc, dvc], axis=-1)
    else:
        dk = dkc + dkv_in[:, :, :KV_W]
        dv = dvc + dkv_in[:, :, KV_W:]
        dkv_out = None
    du = _flat(jnp.concatenate([dq, dk, dv, da, dg, dpu], axis=-1).astype(BF16))
    dh = _mm(du, wl["w_in"], trans_b=True, name=f"{tag}_in_dx", tn=D_MODEL).reshape(B, n, d)
    gw["w_in"] = _mm(sv["h_t"], du, out_dtype=BF16, tn=DW_TN, name=f"{tag}_in_dw")
    dxb, dy_prev, dgate_prev, gw["n1"], dsc1, dsh1 = _norm_bwd(f"{tag}_norm1", sv["st"], wl["n1"], sc1, sh1, dx, dh)
    return (dxb, dy_prev, dgate_prev), gw, dict(sh1=dsh1, sc1=dsc1), dkv_out, got


BIG_W = ("w_in", "w_out", "w_ffn_in", "w_ffn_out")


def _local_step(x, ctx, m_loc, m_ctx, p, final_g, target, big):
    B = x.shape[0]
    depth = m_loc.shape[0]
    lat_mods = [[t[:, None, :] for t in jnp.split(m_loc[l], 6, axis=-1)] for l in range(depth)]
    ctx_mods = [[jnp.broadcast_to(t[None, None, :], (B, 1, D_MODEL)) for t in jnp.split(m_ctx[l], 6)] for l in range(depth)]

    st, cst = (x, None, None), (ctx, None, None)
    w_mix, w_ffn, sv_mix, sv_ffn, csv_mix, csv_ffn = [], [], [], [], [], []
    got = []
    for l in range(depth):
        last = l == depth - 1
        wb = big.mix_weights(l, got)
        wm = dict(n1=p["norm1_g"][l][None, :], w_in=wb["w_in"], w_out=wb["w_out"],
                  margs=(p["attn_sink"][l][None, :], p["conv_dw"][l], p["conv_dw_b"][l][None, :], p["conv_ln_g"][l][None, :],
                         p["conv_ln_b"][l][None, :], _block_diag(p["pool_w"][l]), p["pool_scale"][l][None, :]))
        cst, csv, _ = _mix_half_fwd(f"l{l}c", cst, ctx_mods[l], wm, None, local=False, kv_only=last)
        st, sv, got = _mix_half_fwd(f"l{l}", st, lat_mods[l], wm, csv["kvc"], local=True, kv_only=False,
                                    exch=big.ride_attn_fwd(l))
        w_mix.append(wm)
        sv_mix.append(sv)
        csv_mix.append(csv)
        wb = big.ffn_weights(l, got)
        w_ffn_in = _interleave_ffn(wb["w_ffn_in"], name=f"l{l}_ffn_in_interleave")
        wf = dict(n2=p["norm2_g"][l][None, :], w_ffn_in=w_ffn_in, w_ffn_out=wb["w_ffn_out"])
        csv = None
        if not last:
            cst, csv, _ = _ffn_half_fwd(f"l{l}c", cst, ctx_mods[l], wf)
        st, sv, got = _ffn_half_fwd(f"l{l}", st, lat_mods[l], wf, exch=big.ride_ffn_fwd(l))
        w_ffn.append(wf)
        sv_ffn.append(sv)
        csv_ffn.append(csv)
    loss_rows, dx, dy, dgate, dfinal = _loss_head(st, final_g[None, :], target, name="loss_head")

    dm_loc, dm_ctx = [None] * depth, [None] * depth
    small = [None] * depth
    cdx = cdy = cdgate = None
    up_mix = None
    for l in reversed(range(depth)):
        last = l == depth - 1
        dm, cdm = dict(g2=dgate), {}
        (dx, dy, dm["g1"]), gf, d = _ffn_half_bwd(f"l{l}", sv_ffn[l], lat_mods[l], w_ffn[l], dx, dy)
        dm.update(d)
        if not last:
            cdm["g2"] = cdgate
            (cdx, cdy, cdm["g1"]), cgf, d = _ffn_half_bwd(f"l{l}c", csv_ffn[l], ctx_mods[l], w_ffn[l], cdx, cdy)
            cdm.update(d)
            gf = {k: gf[k] + cgf[k] for k in gf}
        ffn_grads = {k: gf[k] for k in _ShardedWeights.FFN}
        (dx, dy, dgate), gm, d, dkv, got = _mix_half_bwd(f"l{l}", sv_mix[l], lat_mods[l], w_mix[l], dx, dy, None, local=True,
                                                        kv_only=False, exch=big.ride_attn_bwd(l, ffn_grads, up_mix))
        big.took(l, ffn_grads, up_mix, got)
        dm.update(d)
        (cdx, cdy, cdgate), cgm, d, _, _ = _mix_half_bwd(f"l{l}c", csv_mix[l], ctx_mods[l], w_mix[l], cdx, cdy, dkv,
                                                        local=False, kv_only=last)
        cdm.update(d)
        order = ("sh1", "sc1", "g1", "sh2", "sc2", "g2")
        dm_loc[l] = jnp.concatenate([dm[k][:, 0, :] for k in order], axis=-1)
        dm_ctx[l] = jnp.concatenate([jnp.sum(cdm[k][:, 0, :], axis=0) if k in cdm else jnp.zeros((D_MODEL,), F32)
                                     for k in order])
        if last:
            up_mix = dict(w_in=gm["w_in"].at[:, ATTN_W:ATTN_W + 2 * KV_W].add(cgm["w_in_kv"]), w_out=gm["w_out"])
            margs = gm["margs"]
        else:
            up_mix = {k: gm[k] + cgm[k] for k in _ShardedWeights.MIX}
            margs = tuple(a + b for a, b in zip(gm["margs"], cgm["margs"]))
        small[l] = dict(n1=gm["n1"] + cgm["n1"], n2=gf["n2"], margs=margs)
    big.leftover(up_mix)

    stack = lambda f: jnp.stack([f(small[l]) for l in range(depth)])
    dp = dict(
        norm1_g=stack(lambda g: g["n1"][0]), norm2_g=stack(lambda g: g["n2"][0]),
        attn_sink=stack(lambda g: g["margs"][0][0]), conv_dw=stack(lambda g: g["margs"][1]),
        conv_dw_b=stack(lambda g: g["margs"][2][0]), conv_ln_g=stack(lambda g: g["margs"][3][0]),
        conv_ln_b=stack(lambda g: g["margs"][4][0]), pool_w=stack(lambda g: _diag_blocks(g["margs"][5])),
        pool_scale=stack(lambda g: g["margs"][6][0]))
    return jnp.sum(loss_rows[:, 0, 0]), dx, jnp.stack(dm_loc), jnp.stack(dm_ctx), dp, dfinal[0]


PACK_COLS = 1024


def _pack(arrs):
    flat = jnp.concatenate([a.reshape(-1).astype(F32) for a in arrs])
    rows = -(-flat.shape[0] // (8 * PACK_COLS)) * 8
    return jnp.pad(flat, (0, rows * PACK_COLS - flat.shape[0])).reshape(rows, PACK_COLS)


def _unpack(slab, like):
    flat = slab.reshape(-1)
    out, off = [], 0
    for a in like:
        out.append(flat[off:off + a.size].reshape(a.shape))
        off += a.size
    return out


def _shard_cols(gathered):
    _, L, R, C = gathered.shape
    return jnp.transpose(gathered, (1, 2, 0, 3)).reshape(L, R, N_DEV * C)


class _ShardedWeights:
    MIX = ("w_in", "w_out")
    FFN = ("w_ffn_in", "w_ffn_out")
    BY_COLS = ("w_in", "w_ffn_in")

    def __init__(self, shards, first):
        self.shards = shards
        self.first = first
        self.depth = shards[BIG_W[0]].shape[0]
        self.parts = [dict() for _ in range(self.depth)]
        self.left = None

    def _join(self, names, blocks):
        out = {}
        for name, g in zip(names, blocks):
            _, R, C = g.shape
            out[name] = jnp.transpose(g, (1, 0, 2)).reshape(R, N_DEV * C) if name in self.BY_COLS else g.reshape(N_DEV * R, C)
        return out

    def cut(self, names, grads):
        out = []
        for name in names:
            g = grads[name]
            if name in self.BY_COLS:
                R, C8 = g.shape
                out.append(jnp.transpose(g.reshape(R, N_DEV, C8 // N_DEV), (1, 0, 2)))
            else:
                R8, C = g.shape
                out.append(g.reshape(N_DEV, R8 // N_DEV, C))
        return out

    def mix_weights(self, l, got):
        return self._join(self.MIX, self.first if l == 0 else got)

    def ffn_weights(self, l, got):
        return self._join(self.FFN, got)

    def ride_attn_fwd(self, l):
        return [self.shards[name][l] for name in self.FFN], False

    def ride_ffn_fwd(self, l):
        if l + 1 >= self.depth:
            return None
        return [self.shards[name][l + 1] for name in self.MIX], False

    def ride_attn_bwd(self, l, ffn_grads, up_mix):
        return self.cut(self.FFN, ffn_grads) + (self.cut(self.MIX, up_mix) if up_mix is not None else []), True

    def took(self, l, ffn_grads, up_mix, got):
        self.parts[l].update(zip(self.FFN, got[:2]))
        if up_mix is not None:
            self.parts[l + 1].update(zip(self.MIX, got[2:]))

    def leftover(self, mix_grads):
        self.left = mix_grads


def _as_rows(a, leading=0):
    return a.reshape(*a.shape[:leading], -1, PACK_COLS)


SMALL = ("c_ctx", "b_mod", "norm1_g", "norm2_g", "conv_dw_b", "conv_ln_g", "conv_ln_b", "attn_sink", "pool_w",
         "pool_scale", "final_g", "conv_dw")
BIG = ("w_mod", "w_in", "w_out", "w_ffn_in", "w_ffn_out")
ORDER = ("c_ctx", "w_mod", "b_mod", "norm1_g", "norm2_g", "w_in", "conv_dw", "conv_dw_b", "conv_ln_g", "conv_ln_b",
         "attn_sink", "pool_w", "pool_scale", "w_out", "w_ffn_in", "w_ffn_out", "final_g")


def kernel(x, c, ctx, c_ctx, w_mod, b_mod, norm1_g, norm2_g, w_in, conv_dw, conv_dw_b, conv_ln_g, conv_ln_b, attn_sink, pool_w, pool_scale, w_out, w_ffn_in, w_ffn_out, final_g, loss_target, m_c_ctx, m_w_mod, m_b_mod, m_norm1_g, m_norm2_g, m_w_in, m_conv_dw, m_conv_dw_b, m_conv_ln_g, m_conv_ln_b, m_attn_sink, m_pool_w, m_pool_scale, m_w_out, m_w_ffn_in, m_w_ffn_out, m_final_g, v_c_ctx, v_w_mod, v_b_mod, v_norm1_g, v_norm2_g, v_w_in, v_conv_dw, v_conv_dw_b, v_conv_ln_g, v_conv_ln_b, v_attn_sink, v_pool_w, v_pool_scale, v_w_out, v_w_ffn_in, v_w_ffn_out, v_final_g):
    w = dict(c_ctx=c_ctx, w_mod=w_mod, b_mod=b_mod, norm1_g=norm1_g, norm2_g=norm2_g, w_in=w_in, conv_dw=conv_dw,
             conv_dw_b=conv_dw_b, conv_ln_g=conv_ln_g, conv_ln_b=conv_ln_b, attn_sink=attn_sink, pool_w=pool_w,
             pool_scale=pool_scale, w_out=w_out, w_ffn_in=w_ffn_in, w_ffn_out=w_ffn_out, final_g=final_g)
    mom = dict(c_ctx=m_c_ctx, w_mod=m_w_mod, b_mod=m_b_mod, norm1_g=m_norm1_g, norm2_g=m_norm2_g, w_in=m_w_in,
               conv_dw=m_conv_dw, conv_dw_b=m_conv_dw_b, conv_ln_g=m_conv_ln_g, conv_ln_b=m_conv_ln_b,
               attn_sink=m_attn_sink, pool_w=m_pool_w, pool_scale=m_pool_scale, w_out=m_w_out, w_ffn_in=m_w_ffn_in,
               w_ffn_out=m_w_ffn_out, final_g=m_final_g)
    var = dict(c_ctx=v_c_ctx, w_mod=v_w_mod, b_mod=v_b_mod, norm1_g=v_norm1_g, norm2_g=v_norm2_g, w_in=v_w_in,
               conv_dw=v_conv_dw, conv_dw_b=v_conv_dw_b, conv_ln_g=v_conv_ln_g, conv_ln_b=v_conv_ln_b,
               attn_sink=v_attn_sink, pool_w=v_pool_w, pool_scale=v_pool_scale, w_out=v_w_out, w_ffn_in=v_w_ffn_in,
               w_ffn_out=v_w_ffn_out, final_g=v_final_g)
    B = x.shape[0]
    depth = w_mod.shape[0]
    mod_cols = w_mod.shape[2]
    dw_cols = conv_dw.shape[2]
    me = 4 * lax.axis_index("x") + 2 * lax.axis_index("y") + lax.axis_index("c")

    shards = {name: w[name].astype(BF16) for name in BIG_W}
    c_all, dw_all, *first = _exchange([c, conv_dw] + [shards[name][0] for name in _ShardedWeights.MIX], scatter=False,
                                      name="gather_first")
    big = _ShardedWeights(shards, first)
    p = dict(norm1_g=norm1_g, norm2_g=norm2_g, conv_dw=_shard_cols(dw_all), conv_dw_b=conv_dw_b, conv_ln_g=conv_ln_g,
             conv_ln_b=conv_ln_b, attn_sink=attn_sink, pool_w=pool_w, pool_scale=pool_scale)

    cc = jnp.concatenate([c_all.reshape(N_DEV * B, D_MODEL), jnp.broadcast_to(c_ctx[None, :], (N_DEV, D_MODEL)),
                          jnp.zeros((MOD_ROWS - N_DEV * B - N_DEV, D_MODEL), F32)], axis=0)
    b_shard = lax.dynamic_slice_in_dim(b_mod, me * mod_cols, mod_cols, axis=1)[:, None, :]
    m_part = _mod_fwd(cc, w_mod, b_shard, name="mod_fwd")
    m_all, = _exchange([m_part], scatter=False, name="gather_mod")
    m_full = _shard_cols(m_all)
    m_loc = lax.dynamic_slice_in_dim(m_full, me * B, B, axis=1)
    m_ctx = m_full[:, N_DEV * B, :]

    loss_part, dx, dm_loc, dm_ctx, dp, dfinal = _local_step(x, ctx, m_loc, m_ctx, p, final_g, loss_target, big)
    loss = lax.psum(loss_part, AXES)

    dm_rows = jnp.concatenate([dm_loc, dm_ctx[:, None, :], jnp.zeros((depth, 8 - B - 1, 6 * D_MODEL), F32)], axis=1)
    small_like = [norm1_g, norm2_g, conv_dw_b, conv_ln_g, conv_ln_b, attn_sink, pool_w, pool_scale, final_g, dp["conv_dw"]]
    small_part = _pack([dp["norm1_g"], dp["norm2_g"], dp["conv_dw_b"], dp["conv_ln_g"], dp["conv_ln_b"], dp["attn_sink"],
                        dp["pool_w"], dp["pool_scale"], dfinal, dp["conv_dw"]])
    dm_all, small_all = _exchange([dm_rows, small_part], scatter=False, name="gather_tail")
    dm_full = jnp.concatenate([
        jnp.transpose(dm_all[:, :, :B, :], (1, 0, 2, 3)).reshape(depth, N_DEV * B, 6 * D_MODEL),
        jnp.transpose(dm_all[:, :, B, :], (1, 0, 2)),
        jnp.zeros((depth, MOD_ROWS - N_DEV * B - N_DEV, 6 * D_MODEL), F32)], axis=1)
    g_b_mod = jnp.stack([_sum_leading(dm_full[l][:, None, :], name=f"b_mod_grad{l}")[0] for l in range(depth)])
    dm_mine = lax.dynamic_slice_in_dim(dm_full, me * mod_cols, mod_cols, axis=2)
    dcc, g_w_mod = _mod_bwd(cc, w_mod, b_shard, dm_mine, name="mod_bwd")
    g_c_ctx_part = jnp.sum(dcc[N_DEV * B:N_DEV * B + N_DEV], axis=0)

    small_sum = _unpack(_sum_leading(small_all, name="sum_small"), small_like)
    g = dict(zip(("norm1_g", "norm2_g", "conv_dw_b", "conv_ln_g", "conv_ln_b", "attn_sink", "pool_w", "pool_scale",
                  "final_g"), small_sum[:-1]))
    g["b_mod"] = g_b_mod
    g["conv_dw"] = lax.dynamic_slice_in_dim(small_sum[-1], me * dw_cols, dw_cols, axis=2)
    c_ctx_all, = _exchange([g_c_ctx_part.reshape(8, D_MODEL // 8)], scatter=False, name="gather_c_ctx")
    g["c_ctx"] = _sum_leading(c_ctx_all, name="sum_c_ctx").reshape(D_MODEL)

    delta, new_m, new_v = {}, {}, {}
    rides = {big.FFN[0]: (big.cut(big.MIX, big.left), True)}
    for name in big.FFN + big.MIX:
        g[name], delta[name], new_m[name], new_v[name], *got = _adamw(
            [big.parts[l][name] for l in range(depth)], w[name], mom[name], var[name], name=f"adamw_{name}",
            exch=rides.get(name))
        if got:
            big.parts[0].update(zip(big.MIX, got))
    g["w_mod"], delta["w_mod"], new_m["w_mod"], new_v["w_mod"] = _adamw(g_w_mod, w_mod, m_w_mod, v_w_mod, name="adamw_w_mod")
    res = _adamw_small([g[k] for k in SMALL], [w[k] for k in SMALL], [mom[k] for k in SMALL], [var[k] for k in SMALL],
                       name="adamw_small")
    for dst, arrs in zip((delta, new_m, new_v), res):
        dst.update(zip(SMALL, arrs))

    return (loss, dx, *[g[k] for k in ORDER], *[delta[k] for k in ORDER], *[new_m[k] for k in ORDER],
            *[new_v[k] for k in ORDER])
```

```python
import functools

import numpy as np
import jax
import jax.numpy as jnp
from jax import lax
from jax.experimental import pallas as pl
from jax.experimental.pallas import tpu as pltpu

F32 = jnp.float32
BF16 = jnp.bfloat16

D_MODEL = 1024
GRID_W = 64
HEAD_DIM = 64
ATTN_W = 512
CONV_W = 256
POOL_W = 256
ATTN_HEADS = 8
KV_HEADS = 2
GROUP = ATTN_HEADS // KV_HEADS
KV_W = KV_HEADS * HEAD_DIM
IN_W = ATTN_W + 2 * KV_W + 2 * CONV_W + POOL_W
WINDOW = 128
Q_BLOCK = 128
SPAN = Q_BLOCK + 2 * WINDOW
CONV_KERNEL = 31
POOL_WINDOWS = (2, 4, 8, 16)
POOL_GROUP = 64
ROPE_BASE = 10000.0
D_FF = 2816
EPS = 1e-6
NEG = -1e30
N_DEV = 8
AXES = ("x", "y", "c")

ADAM_LR = 0.001
ADAM_B1 = 0.9
ADAM_B2 = 0.999
ADAM_EPS = 1e-08
ADAM_WD = 0.01
ADAM_STEP = 10

VMEM_LIMIT = 56 * 1024 * 1024
HALO = 16
SEQ_CHUNK = 512
CONV_BWD_CHUNK = 256


def _params(*sem):
    return pltpu.CompilerParams(dimension_semantics=sem, vmem_limit_bytes=VMEM_LIMIT)


def _tile(dim, target):
    if dim <= target:
        return dim
    t = (target // 128) * 128
    while t >= 128:
        if dim % t == 0:
            return t
        t -= 128
    raise ValueError(f"no tile for {dim}")


MM_VMEM_BUDGET = 44 * 1024 * 1024
DW_TN = 256


def _dot(a, b):
    return lax.dot_general(a.astype(BF16), b.astype(BF16), (((1,), (0,)), ((), ())), preferred_element_type=F32)


def _mm_vmem(tm, tn, tk, whole, a_bytes, b_bytes, o_bytes):
    return 2 * (tm * tk * a_bytes + tk * tn * b_bytes + tm * tn * o_bytes) + (0 if whole else tm * tn * 4)


def _mm(a, b, *, name, out_dtype=F32, tm=1408, tn=512, trans_b=False, out_block=None):
    M, K = a.shape
    N, K2 = b.shape if trans_b else b.shape[::-1]
    assert K == K2, (a.shape, b.shape)
    tm = _tile(M, tm)
    tn = _tile(N, tn)
    sizes = (a.dtype.itemsize, b.dtype.itemsize, jnp.dtype(out_dtype).itemsize)
    tk = next(t for t in range(K, 0, -128) if K % t == 0 and _mm_vmem(tm, tn, t, t == K, *sizes) <= MM_VMEM_BUDGET)
    nk = K // tk

    def body(a_ref, b_ref, o_ref, *scratch):
        part = (_dot_nt if trans_b else _dot)(a_ref[...], b_ref[...])
        if nk == 1:
            o_ref[...] = part.astype(o_ref.dtype)
        else:
            acc_ref, = scratch
            k = pl.program_id(2)

            @pl.when(k == 0)
            def _():
                acc_ref[...] = part

            @pl.when(k > 0)
            def _():
                acc_ref[...] += part

            @pl.when(k == nk - 1)
            def _():
                o_ref[...] = acc_ref[...].astype(o_ref.dtype)

    return pl.pallas_call(
        body, name=name, grid=(M // tm, N // tn, nk),
        in_specs=[pl.BlockSpec((tm, tk), lambda i, j, k: (i, k)),
                  pl.BlockSpec((tn, tk), lambda i, j, k: (j, k)) if trans_b else pl.BlockSpec((tk, tn), lambda i, j, k: (k, j))],
        out_specs=pl.BlockSpec((tm, tn), (lambda i, j, k: (i, j)) if out_block is None else (lambda i, j, k: (i, out_block(j)))),
        out_shape=jax.ShapeDtypeStruct((M, N), out_dtype),
        scratch_shapes=[pltpu.VMEM((tm, tn), F32)] if nk > 1 else [],
        compiler_params=_params("parallel", "parallel", "arbitrary"),
    )(a, b)


FF_TILE = 256


FF_TILES = D_FF // FF_TILE


def _natural_block(j):
    return (j % 2) * FF_TILES + j // 2


def _interleave_ffn(w, *, name):
    R, C = w.shape

    def body(w_ref, o_ref):
        o_ref[...] = w_ref[...]

    return pl.pallas_call(
        body, name=name, grid=(2 * FF_TILES,),
        in_specs=[pl.BlockSpec((R, FF_TILE), lambda j: (0, _natural_block(j)))],
        out_specs=pl.BlockSpec((R, FF_TILE), lambda j: (0, j)),
        out_shape=jax.ShapeDtypeStruct((R, C), w.dtype),
        compiler_params=_params("parallel"),
    )(w)


def _swiglu(gu):
    g, u = gu[:, :FF_TILE], gu[:, FF_TILE:]
    return g * jax.nn.sigmoid(g) * u


EPILOGUE_SPLIT = 1
FF_ROWS = 4096


def _mm_swiglu(a, w_il, *, name, tm=FF_ROWS, exch=None, split=EPILOGUE_SPLIT):
    M, K = a.shape
    tm = _tile(M, tm)
    rc = tm // split

    def body(a_ref, b_ref, gu_ref, act_ref, act_t_ref):
        b = b_ref[...]
        parts = [_dot(a_ref[r0:r0 + rc, :], b) for r0 in range(0, tm, rc)]
        for r0, gu in zip(range(0, tm, rc), parts):
            gu_ref[r0:r0 + rc, :] = gu.astype(gu_ref.dtype)
            act = _swiglu(gu)
            act_ref[r0:r0 + rc, :] = act.astype(act_ref.dtype)
            act_t_ref[:, r0:r0 + rc] = act.T.astype(act_t_ref.dtype)

    grid = (M // tm, D_FF // FF_TILE)
    body, x_in, x_out, x_shapes, x_sems = _riding(body, exch, 2, 3, grid)
    return pl.pallas_call(
        body, name=name, grid=grid,
        in_specs=[pl.BlockSpec((tm, K), lambda i, j: (i, 0)), pl.BlockSpec((K, 2 * FF_TILE), lambda i, j: (0, j))] + x_in,
        out_specs=[pl.BlockSpec((tm, 2 * FF_TILE), lambda i, j: (i, j)), pl.BlockSpec((tm, FF_TILE), lambda i, j: (i, j)),
                   pl.BlockSpec((FF_TILE, tm), lambda i, j: (j, i))] + x_out,
        out_shape=[jax.ShapeDtypeStruct((M, 2 * D_FF), BF16), jax.ShapeDtypeStruct((M, D_FF), BF16),
                   jax.ShapeDtypeStruct((D_FF, M), BF16)] + x_shapes,
        scratch_shapes=x_sems,
        compiler_params=_params("arbitrary", "arbitrary") if exch else _params("parallel", "parallel"),
    )(a, w_il, *(exch[0] if exch else []))


def _mm_dswiglu(dy, w_out, gu, *, name, tm=FF_ROWS, split=EPILOGUE_SPLIT):
    M, K = dy.shape
    tm = _tile(M, tm)
    rc = tm // split

    def body(dy_ref, b_ref, gu_ref, o_ref):
        b = b_ref[...]
        parts = [_dot_nt(dy_ref[r0:r0 + rc, :], b) for r0 in range(0, tm, rc)]
        for r0, dact in zip(range(0, tm, rc), parts):
            g = gu_ref[r0:r0 + rc, :FF_TILE].astype(F32)
            u = gu_ref[r0:r0 + rc, FF_TILE:].astype(F32)
            sig = jax.nn.sigmoid(g)
            silu = g * sig
            o_ref[r0:r0 + rc, :FF_TILE] = (dact * u * (sig + silu * (1.0 - sig))).astype(o_ref.dtype)
            o_ref[r0:r0 + rc, FF_TILE:] = (dact * silu).astype(o_ref.dtype)

    return pl.pallas_call(
        body, name=name, grid=(M // tm, D_FF // FF_TILE),
        in_specs=[pl.BlockSpec((tm, K), lambda i, j: (i, 0)), pl.BlockSpec((FF_TILE, K), lambda i, j: (j, 0)),
                  pl.BlockSpec((tm, 2 * FF_TILE), lambda i, j: (i, j))],
        out_specs=pl.BlockSpec((tm, 2 * FF_TILE), lambda i, j: (i, j)),
        out_shape=jax.ShapeDtypeStruct((M, 2 * D_FF), BF16),
        compiler_params=_params("parallel", "parallel"),
    )(dy, w_out, gu)


def _rope_tables(n):
    rows = n // GRID_W
    row = jnp.repeat(jnp.arange(rows), GRID_W).astype(F32)
    col = jnp.tile(jnp.arange(GRID_W), rows).astype(F32)
    half = HEAD_DIM // 2
    inv = ROPE_BASE ** (-jnp.arange(0, half, 2, dtype=F32) / half)
    ar = row[:, None] * inv
    ac = col[:, None] * inv
    ang = jnp.concatenate([ar, ar, ac, ac], axis=-1)
    return jnp.cos(ang), jnp.sin(ang)


def _rot_half(x):
    w = x.shape[-1]
    lane = lax.broadcasted_iota(jnp.int32, x.shape, 1)
    up = pltpu.roll(x, w - 16, 1)
    down = pltpu.roll(x, 16, 1)
    return jnp.where((lane & 16) == 0, -up, down)


def _rope(x, cos, sin):
    return x * cos + _rot_half(x) * sin


def _rope_bwd(d, cos, sin):
    return d * cos - _rot_half(d * sin)


def _dot_nt(a, b):
    return lax.dot_general(a.astype(BF16), b.astype(BF16), (((1,), (1,)), ((), ())), preferred_element_type=F32)


def _softmax_sink(s, sink_rows):
    mx = jnp.maximum(jnp.max(s, axis=1, keepdims=True), sink_rows)
    e = jnp.exp(s - mx)
    es = jnp.exp(sink_rows - mx)
    inv = 1.0 / (jnp.sum(e, axis=1, keepdims=True) + es)
    return e * inv, es * inv


def _attn_operands(q_ref, k_ref, v_ref, kc_ref, vc_ref, cq_ref, sq_ref, ck_ref, sk_ref, i, n, n_ctx, local):
    q = q_ref[...]
    k_all, v_all, bias, s0, ck, sk = kc_ref[...], vc_ref[...], None, None, None, None
    if local:
        start, s0 = _span_start(i, n)
        ck = ck_ref[pl.ds(s0, SPAN), :]
        sk = sk_ref[pl.ds(s0, SPAN), :]
        q = _rope(q, cq_ref[...], sq_ref[...])
        k_all = jnp.concatenate([k_all, _rope(k_ref[pl.ds(s0, SPAN), :], ck, sk)], axis=0)
        v_all = jnp.concatenate([v_all, v_ref[pl.ds(s0, SPAN), :]], axis=0)
        bias = _window_bias(start, s0, n_ctx)
    q = (q * (HEAD_DIM ** -0.5)).astype(BF16)
    return q, k_all.astype(BF16), v_all.astype(BF16), bias, s0, ck, sk


def _stack_heads(x, kh):
    return jnp.concatenate([x[:, (GROUP * kh + g) * HEAD_DIM:(GROUP * kh + g + 1) * HEAD_DIM] for g in range(GROUP)], axis=0)


def _sink_rows(sink, kh):
    return jnp.concatenate([jnp.broadcast_to(sink[:, GROUP * kh + g:GROUP * kh + g + 1], (Q_BLOCK, 1)) for g in range(GROUP)], axis=0)


def _window_bias(start, s0, n_ctx):
    r = lax.broadcasted_iota(jnp.int32, (Q_BLOCK, n_ctx + SPAN), 0)
    c = lax.broadcasted_iota(jnp.int32, (Q_BLOCK, n_ctx + SPAN), 1)
    ok = (c < n_ctx) | (jnp.abs(start - s0 + r - (c - n_ctx)) <= WINDOW)
    return jnp.concatenate([jnp.where(ok, 0.0, NEG).astype(F32)] * GROUP, axis=0)


def _span_start(i, n):
    start = i * Q_BLOCK
    s0 = jnp.clip(start - WINDOW, 0, n - SPAN)
    return start, pl.multiple_of(s0, Q_BLOCK)


def _riding(body, exch, n_in, n_out, grid):
    if exch is None:
        return body, [], [], [], []
    arrs, scatter = exch
    k = len(arrs)

    def wrapped(*refs):
        ins, xin = refs[:n_in], refs[n_in:n_in + k]
        outs, xout = refs[n_in + k:n_in + k + n_out], refs[n_in + k + n_out:n_in + 2 * k + n_out]
        sems = refs[n_in + 2 * k + n_out:]
        b, i = pl.program_id(0), pl.program_id(1)

        @pl.when((b == 0) & (i == 0))
        def _():
            _exch_start(xin, xout, sems, scatter)

        body(*ins, *outs)

        @pl.when((b == grid[0] - 1) & (i == grid[1] - 1))
        def _():
            _exch_wait(xin, xout, sems, scatter)

    any_spec = pl.BlockSpec(memory_space=pl.ANY)
    return wrapped, [any_spec] * k, [any_spec] * k, _exch_out_shapes(arrs, scatter), _exch_sems(k)


def _attn_fwd(u, kvc, sink, cos, sin, *, local, name, exch=None):
    B, n, _ = u.shape
    n_ctx = kvc.shape[1]
    nb = n // Q_BLOCK
    assert (not local) or n >= SPAN

    def body(q_ref, k_ref, v_ref, kc_ref, vc_ref, sink_ref, cq_ref, sq_ref, ck_ref, sk_ref, o_ref):
        q, k_all, v_all, bias, _, _, _ = _attn_operands(q_ref, k_ref, v_ref, kc_ref, vc_ref, cq_ref, sq_ref, ck_ref, sk_ref,
                                                        pl.program_id(1), n, n_ctx, local)
        sink_v = sink_ref[...]
        sl = lambda kh: slice(kh * HEAD_DIM, (kh + 1) * HEAD_DIM)
        ss = [_dot_nt(_stack_heads(q, kh), k_all[:, sl(kh)]) for kh in range(KV_HEADS)]
        ps = [_softmax_sink(s if bias is None else s + bias, _sink_rows(sink_v, kh))[0].astype(BF16) for kh, s in enumerate(ss)]
        for kh, p in enumerate(ps):
            o = _dot(p, v_all[:, sl(kh)])
            for g in range(GROUP):
                h = GROUP * kh + g
                o_ref[:, h * HEAD_DIM:(h + 1) * HEAD_DIM] = o[g * Q_BLOCK:(g + 1) * Q_BLOCK, :].astype(o_ref.dtype)

    seq = lambda blk: pl.BlockSpec((None, n, KV_W), lambda b, i: (b, 0, blk))
    ctxs = lambda blk: pl.BlockSpec((None, n_ctx, KV_W), lambda b, i: (b, 0, blk))
    full = lambda a: pl.BlockSpec(a.shape, lambda b, i: (0,) * a.ndim)
    cos_q, sin_q = jnp.tile(cos, (1, ATTN_HEADS)), jnp.tile(sin, (1, ATTN_HEADS))
    cos_k, sin_k = jnp.tile(cos, (1, KV_HEADS)), jnp.tile(sin, (1, KV_HEADS))
    body, x_in, x_out, x_shapes, x_sems = _riding(body, exch, 10, 1, (B, nb))
    return pl.pallas_call(
        body, name=name, grid=(B, nb),
        in_specs=[pl.BlockSpec((None, Q_BLOCK, ATTN_W), lambda b, i: (b, i, 0)),
                  seq(ATTN_W // KV_W), seq(ATTN_W // KV_W + 1), ctxs(0), ctxs(1), full(sink),
                  pl.BlockSpec((Q_BLOCK, ATTN_W), lambda b, i: (i, 0)), pl.BlockSpec((Q_BLOCK, ATTN_W), lambda b, i: (i, 0)),
                  full(cos_k), full(sin_k)] + x_in,
        out_specs=[pl.BlockSpec((None, Q_BLOCK, ATTN_W), lambda b, i: (b, i, 0))] + x_out,
        out_shape=[jax.ShapeDtypeStruct((B, n, ATTN_W), BF16)] + x_shapes,
        scratch_shapes=x_sems,
        compiler_params=_params("arbitrary", "arbitrary"),
    )(u, u, u, kvc, kvc, sink, cos_q, sin_q, cos_k, sin_k, *(exch[0] if exch else []))


def _attn_bwd(u, kvc, sink, cos, sin, do_src, do_blk, *, local, name, exch=None):
    B, n, _ = u.shape
    n_ctx = kvc.shape[1]
    nb = n // Q_BLOCK

    def body(q_ref, k_ref, v_ref, kc_ref, vc_ref, sink_ref, cq_ref, sq_ref, ck_ref, sk_ref, do_ref,
             dq_ref, dk_ref, dv_ref, dkc_ref, dvc_ref, dsink_ref):
        b = pl.program_id(0)
        i = pl.program_id(1)

        @pl.when(i == 0)
        def _():
            dk_ref[...] = jnp.zeros_like(dk_ref)
            dv_ref[...] = jnp.zeros_like(dv_ref)
            dkc_ref[...] = jnp.zeros_like(dkc_ref)
            dvc_ref[...] = jnp.zeros_like(dvc_ref)

        @pl.when((i == 0) & (b == 0))
        def _():
            dsink_ref[...] = jnp.zeros_like(dsink_ref)

        q, k_all, v_all, bias, s0, ck, sk = _attn_operands(q_ref, k_ref, v_ref, kc_ref, vc_ref, cq_ref, sq_ref, ck_ref, sk_ref,
                                                           i, n, n_ctx, local)
        do = do_ref[...].astype(BF16)
        sink_v = sink_ref[...]
        sl = lambda kh: slice(kh * HEAD_DIM, (kh + 1) * HEAD_DIM)
        heads = range(KV_HEADS)
        q_st = [_stack_heads(q, kh) for kh in heads]
        do_st = [_stack_heads(do, kh) for kh in heads]
        ss = [_dot_nt(q_st[kh], k_all[:, sl(kh)]) for kh in heads]
        dps = [_dot_nt(do_st[kh], v_all[:, sl(kh)]) for kh in heads]
        p_bf, ds_bf = [], []
        dsink = jnp.zeros((1, ATTN_HEADS), F32)
        lane8 = lax.broadcasted_iota(jnp.int32, (1, ATTN_HEADS), 1)
        for kh in heads:
            p, p_sink = _softmax_sink(ss[kh] if bias is None else ss[kh] + bias, _sink_rows(sink_v, kh))
            delta = jnp.sum(p * dps[kh], axis=1, keepdims=True)
            ds = p * (dps[kh] - delta)
            dsr = -(p_sink * delta)
            for g in range(GROUP):
                dsink = dsink + jnp.where(lane8 == GROUP * kh + g, jnp.sum(dsr[g * Q_BLOCK:(g + 1) * Q_BLOCK, :]), 0.0)
            p_bf.append(p.astype(BF16))
            ds_bf.append(ds.astype(BF16))
        over_rows = (((0,), (0,)), ((), ()))
        dks, dvs = [], []
        for kh in heads:
            dq_st = _dot(ds_bf[kh], k_all[:, sl(kh)]) * (HEAD_DIM ** -0.5)
            for g in range(GROUP):
                h = GROUP * kh + g
                dq_ref[:, h * HEAD_DIM:(h + 1) * HEAD_DIM] = dq_st[g * Q_BLOCK:(g + 1) * Q_BLOCK, :]
            dvs.append(lax.dot_general(p_bf[kh], do_st[kh], over_rows, preferred_element_type=F32))
            dks.append(lax.dot_general(ds_bf[kh], q_st[kh], over_rows, preferred_element_type=F32))
        dk_cat = jnp.concatenate(dks, axis=1)
        dv_cat = jnp.concatenate(dvs, axis=1)
        dsink_ref[...] += dsink
        dkc_ref[...] += dk_cat[:n_ctx, :]
        dvc_ref[...] += dv_cat[:n_ctx, :]
        if local:
            dq_ref[...] = _rope_bwd(dq_ref[...], cq_ref[...], sq_ref[...])
            dk_ref[pl.ds(s0, SPAN), :] += _rope_bwd(dk_cat[n_ctx:, :], ck, sk)
            dv_ref[pl.ds(s0, SPAN), :] += dv_cat[n_ctx:, :]

    seq = lambda blk: pl.BlockSpec((None, n, KV_W), lambda b, i: (b, 0, blk))
    ctxs = lambda blk: pl.BlockSpec((None, n_ctx, KV_W), lambda b, i: (b, 0, blk))
    full = lambda a: pl.BlockSpec(a.shape, lambda b, i: (0,) * a.ndim)
    qblk = lambda blk: pl.BlockSpec((None, Q_BLOCK, ATTN_W), lambda b, i: (b, i, blk))
    cos_q, sin_q = jnp.tile(cos, (1, ATTN_HEADS)), jnp.tile(sin, (1, ATTN_HEADS))
    cos_k, sin_k = jnp.tile(cos, (1, KV_HEADS)), jnp.tile(sin, (1, KV_HEADS))
    acc = lambda rows: pl.BlockSpec((None, rows, KV_W), lambda b, i: (b, 0, 0))
    body, x_in, x_out, x_shapes, x_sems = _riding(body, exch, 11, 6, (B, nb))
    return pl.pallas_call(
        body, name=name, grid=(B, nb),
        in_specs=[qblk(0), seq(ATTN_W // KV_W), seq(ATTN_W // KV_W + 1), ctxs(0), ctxs(1), full(sink),
                  pl.BlockSpec((Q_BLOCK, ATTN_W), lambda b, i: (i, 0)), pl.BlockSpec((Q_BLOCK, ATTN_W), lambda b, i: (i, 0)),
                  full(cos_k), full(sin_k), qblk(do_blk)] + x_in,
        out_specs=[qblk(0), acc(n), acc(n), acc(n_ctx), acc(n_ctx), pl.BlockSpec((1, ATTN_HEADS), lambda b, i: (0, 0))] + x_out,
        out_shape=[jax.ShapeDtypeStruct((B, n, ATTN_W), F32), jax.ShapeDtypeStruct((B, n, KV_W), F32),
                   jax.ShapeDtypeStruct((B, n, KV_W), F32), jax.ShapeDtypeStruct((B, n_ctx, KV_W), F32),
                   jax.ShapeDtypeStruct((B, n_ctx, KV_W), F32), jax.ShapeDtypeStruct((1, ATTN_HEADS), F32)] + x_shapes,
        scratch_shapes=x_sems,
        compiler_params=_params("arbitrary", "arbitrary"),
    )(u, u, u, kvc, kvc, sink, cos_q, sin_q, cos_k, sin_k, do_src, *(exch[0] if exch else []))


def _conv_chunk(s, n, a_ext, g_ext, dw, dw_b, ln_g, ln_b):
    del s, n
    acc = _conv_taps(a_ext, g_ext, dw, dw_b)
    return _conv_tail(acc, ln_g, ln_b), acc


def _conv_chunk_bwd(s, n, ext, pars, acc, do):
    del s, n
    dw, dw_b, ln_g, ln_b = pars
    _, tail_vjp = jax.vjp(_conv_tail, acc, ln_g, ln_b)
    dacc, dln_g, dln_b = tail_vjp(do)
    _, taps_vjp = jax.vjp(_conv_taps, *ext, dw, dw_b)
    return (*taps_vjp(dacc), dln_g, dln_b)


def _conv_tail(acc, ln_g, ln_b):
    mu = jnp.mean(acc, axis=-1, keepdims=True)
    var = jnp.mean(jnp.square(acc - mu), axis=-1, keepdims=True)
    hn = (acc - mu) * lax.rsqrt(var + EPS) * ln_g + ln_b
    return hn * jax.nn.sigmoid(hn)


def _conv_taps(a_ext, g_ext, dw, dw_b):
    r = a_ext.shape[0] - 2 * HALO
    h = a_ext * jax.nn.sigmoid(g_ext)
    acc = jnp.broadcast_to(dw_b, (r, CONV_W))
    first = HALO - CONV_KERNEL // 2
    span = r + 8 * ((first + CONV_KERNEL - 1) // 8)
    shifted = [h[b:b + span, :] for b in range(8)]
    for k in range(CONV_KERNEL):
        o = first + k
        acc = acc + shifted[o % 8][o - o % 8:o - o % 8 + r, :] * dw[k:k + 1, :]
    return acc


def _pool_chunk(s, n, p_ext, w_bd, scale):
    r = p_ext.shape[0] - 2 * HALO
    lane = lax.broadcasted_iota(jnp.int32, (1, POOL_W), 1)
    win = jnp.left_shift(2, lane // POOL_GROUP)
    half = win // 2
    acc = jnp.zeros((r, POOL_W), F32)
    for d in range(-(POOL_WINDOWS[-1] // 2), POOL_WINDOWS[-1] - POOL_WINDOWS[-1] // 2):
        inside = (d >= -half) & (d <= win - 1 - half)
        acc = acc + jnp.where(inside, p_ext[HALO + d:HALO + d + r, :], 0.0)
    t = s + lax.broadcasted_iota(jnp.int32, (r, 1), 0)
    lo = jnp.maximum(t - half, 0)
    hi = jnp.minimum(t + win - 1 - half, n - 1)
    y = acc / (hi - lo + 1).astype(F32) - p_ext[HALO:HALO + r, :]
    out = lax.dot_general(y.astype(BF16), w_bd.astype(BF16), (((1,), (0,)), ((), ())), preferred_element_type=F32)
    return out * scale


def _seq_specs(rows, params):
    specs = [pl.BlockSpec((None, a.shape[1], w), functools.partial(lambda b, blk: (b, 0, blk), blk=blk)) for a, w, blk in rows]
    specs += [pl.BlockSpec(p.shape, functools.partial(lambda b, nd: (0,) * nd, nd=p.ndim)) for p in params]
    return specs


def _fill_padded(pad_ref, row_ref, n):
    w = pad_ref.shape[1]
    pad_ref[pl.ds(0, HALO), :] = jnp.zeros((HALO, w), F32)
    pad_ref[pl.ds(HALO + n, HALO), :] = jnp.zeros((HALO, w), F32)
    pad_ref[pl.ds(HALO, n), :] = row_ref[...]


def _seq_fwd(fn, rows, params, out_w, *, name, chunk=SEQ_CHUNK, aux_w=None):
    B, n = rows[0][0].shape[:2]
    r = min(chunk, n)
    nr, npar = len(rows), len(params)
    nout = 1 if aux_w is None else 2

    def body(*refs):
        row_refs, par_refs = refs[:nr], refs[nr:nr + npar]
        out_refs, pads = refs[nr + npar:nr + npar + nout], refs[nr + npar + nout:]
        for rr, p in zip(row_refs, pads):
            _fill_padded(p, rr, n)
        pars = [p[...] for p in par_refs]

        def chunk(ci, carry):
            s = pl.multiple_of(ci * r, r)
            ext = [p[pl.ds(s, r + 2 * HALO), :] for p in pads]
            res = fn(s, n, *ext, *pars)
            for o_ref, v in zip(out_refs, res if nout == 2 else (res,)):
                o_ref[pl.ds(s, r), :] = v.astype(o_ref.dtype)
            return carry

        lax.fori_loop(0, n // r, chunk, 0)

    widths = [(out_w, BF16)] + ([] if aux_w is None else [(aux_w, F32)])
    res = pl.pallas_call(
        body, name=name, grid=(B,),
        in_specs=_seq_specs(rows, params),
        out_specs=[pl.BlockSpec((None, n, w), lambda b: (b, 0, 0)) for w, _ in widths],
        out_shape=[jax.ShapeDtypeStruct((B, n, w), dt) for w, dt in widths],
        scratch_shapes=[pltpu.VMEM((n + 2 * HALO, w), F32) for _, w, _ in rows],
        compiler_params=_params("parallel"),
    )(*[a for a, _, _ in rows], *params)
    return res[0] if aux_w is None else res


def _seq_bwd(fn, rows, params, dout, *, name, chunk=SEQ_CHUNK, aux=None):
    B, n = rows[0][0].shape[:2]
    r = min(chunk, n)
    nr, npar = len(rows), len(params)
    naux = 0 if aux is None else 1

    def body(*refs):
        row_refs, par_refs, do_ref = refs[:nr], refs[nr:nr + npar], refs[nr + npar]
        aux_refs = refs[nr + npar + 1:nr + npar + 1 + naux]
        outs = refs[nr + npar + 1 + naux:]
        drow_refs, dpar_refs = outs[:nr], outs[nr:nr + npar]
        scratch = outs[nr + npar:]
        pads, dpads = scratch[:nr], scratch[nr:]
        for rr, p, dp in zip(row_refs, pads, dpads):
            _fill_padded(p, rr, n)
            dp[...] = jnp.zeros_like(dp)

        @pl.when(pl.program_id(0) == 0)
        def _():
            for d in dpar_refs:
                d[...] = jnp.zeros_like(d)

        pars = [p[...] for p in par_refs]

        def chunk(ci, carry):
            s = pl.multiple_of(ci * r, r)
            ext = [p[pl.ds(s, r + 2 * HALO), :] for p in pads]
            do = do_ref[pl.ds(s, r), :]
            if aux is None:
                _, vjp = jax.vjp(functools.partial(fn, s, n), *ext, *pars)
                grads = vjp(do)
            else:
                grads = fn(s, n, ext, pars, aux_refs[0][pl.ds(s, r), :], do)
            for dp, g in zip(dpads, grads[:nr]):
                dp[pl.ds(s, r + 2 * HALO), :] += g
            for d, g in zip(dpar_refs, grads[nr:]):
                d[...] += g
            return carry

        lax.fori_loop(0, n // r, chunk, 0)
        for d, dp in zip(drow_refs, dpads):
            d[...] = dp[pl.ds(HALO, n), :].astype(d.dtype)

    da, dw_, dblk = dout
    auxs = [] if aux is None else [aux]
    return pl.pallas_call(
        body, name=name, grid=(B,),
        in_specs=_seq_specs(rows, params) + [pl.BlockSpec((None, n, dw_), lambda b: (b, 0, dblk))]
        + [pl.BlockSpec((None, n, a.shape[2]), lambda b: (b, 0, 0)) for a in auxs],
        out_specs=[pl.BlockSpec((None, n, w), lambda b: (b, 0, 0)) for _, w, _ in rows]
        + [pl.BlockSpec(p.shape, functools.partial(lambda b, nd: (0,) * nd, nd=p.ndim)) for p in params],
        out_shape=[jax.ShapeDtypeStruct((B, n, w), BF16) for _, w, _ in rows]
        + [jax.ShapeDtypeStruct(p.shape, F32) for p in params],
        scratch_shapes=[pltpu.VMEM((n + 2 * HALO, w), F32) for _, w, _ in rows] * 2,
        compiler_params=_params("arbitrary"),
    )(*[a for a, _, _ in rows], *params, da, *auxs)


_CONV_A_BLK = (ATTN_W + 2 * KV_W) // CONV_W
_CONV_G_BLK = _CONV_A_BLK + 1
_POOL_BLK = _CONV_A_BLK + 2


def _mixer_fwd(tag, u, kvc, margs, local, exch=None):
    sink, dw, dw_b, ln_g, ln_b, w_bd, scale = margs
    cos, sin = _rope_tables(max(u.shape[1], GRID_W))
    attn, *got = _attn_fwd(u, kvc, sink, cos, sin, local=local, name=f"{tag}_attn_fwd", exch=exch)
    conv, conv_acc = _seq_fwd(_conv_chunk, [(u, CONV_W, _CONV_A_BLK), (u, CONV_W, _CONV_G_BLK)], [dw, dw_b, ln_g, ln_b],
                              CONV_W, name=f"{tag}_conv_fwd", aux_w=CONV_W)
    pool = _seq_fwd(_pool_chunk, [(u, POOL_W, _POOL_BLK)], [w_bd, scale], POOL_W, name=f"{tag}_pool_fwd")
    return jnp.concatenate([attn, conv, pool], axis=-1), conv_acc, got


def _mixer_bwd(tag, u, kvc, conv_acc, margs, dmix, local, exch=None):
    sink, dw, dw_b, ln_g, ln_b, w_bd, scale = margs
    cos, sin = _rope_tables(max(u.shape[1], GRID_W))
    dq, dk, dv, dkc, dvc, dsink, *got = _attn_bwd(u, kvc, sink, cos, sin, dmix, 0, local=local, name=f"{tag}_attn_bwd",
                                                  exch=exch)
    da, dg, ddw, ddw_b, dln_g, dln_b = _seq_bwd(
        _conv_chunk_bwd, [(u, CONV_W, _CONV_A_BLK), (u, CONV_W, _CONV_G_BLK)], [dw, dw_b, ln_g, ln_b],
        (dmix, CONV_W, ATTN_W // CONV_W), name=f"{tag}_conv_bwd", chunk=CONV_BWD_CHUNK, aux=conv_acc)
    dpu, dw_bd, dscale = _seq_bwd(_pool_chunk, [(u, POOL_W, _POOL_BLK)], [w_bd, scale],
                                  (dmix, POOL_W, (ATTN_W + CONV_W) // POOL_W), name=f"{tag}_pool_bwd")
    return (dq, dk, dv, da, dg, dpu), (dkc, dvc), (dsink, ddw, ddw_b, dln_g, dln_b, dw_bd, dscale), got


def _row_specs(arrs, kinds, tr):
    specs = []
    for a, kind in zip(arrs, kinds):
        if kind == "row":
            specs.append(pl.BlockSpec((None, tr, a.shape[2]), lambda b, j: (b, j, 0)))
        elif kind == "batch":
            specs.append(pl.BlockSpec((None, 1, a.shape[2]), lambda b, j: (b, 0, 0)))
        else:
            specs.append(pl.BlockSpec(a.shape, functools.partial(lambda b, j, nd: (0,) * nd, nd=a.ndim)))
    return specs


def _rowwise_fwd(fn, ins, kinds, outs, tr, *, name, transposed=None, exch=None):
    B, n = ins[0].shape[:2]
    ni, no = len(ins), len(outs)
    nj = n // tr

    def body(*refs):
        res = fn(*[r[...] for r in refs[:ni]])
        for o, v in zip(refs[ni:ni + no], res):
            o[...] = v.astype(o.dtype)
        if transposed is not None:
            refs[ni + no][...] = res[transposed].T.astype(refs[ni + no].dtype)

    out_specs = [pl.BlockSpec((None, tr, w), lambda b, j: (b, j, 0)) for w, _ in outs]
    out_shape = [jax.ShapeDtypeStruct((B, n, w), dt) for w, dt in outs]
    if transposed is not None:
        w, dt = outs[transposed]
        out_specs.append(pl.BlockSpec((w, tr), lambda b, j: (0, b * nj + j)))
        out_shape.append(jax.ShapeDtypeStruct((w, B * n), dt))
    body, x_in, x_out, x_shapes, x_sems = _riding(body, exch, ni, len(out_specs), (B, nj))
    return pl.pallas_call(
        body, name=name, grid=(B, nj),
        in_specs=_row_specs(ins, kinds, tr) + x_in, out_specs=out_specs + x_out, out_shape=out_shape + x_shapes,
        scratch_shapes=x_sems,
        compiler_params=_params("arbitrary", "arbitrary") if exch else _params("parallel", "parallel"),
    )(*ins, *(exch[0] if exch else []))


def _rowwise_bwd(fn, ins, kinds, gdtypes, cts, tr, *, name):
    B, n = ins[0].shape[:2]
    ni, nc = len(ins), len(cts)
    idx = list(range(ni))

    def body(*refs):
        in_refs, ct_refs, out_refs = refs[:ni], refs[ni:ni + nc], refs[ni + nc:]
        b, j = pl.program_id(0), pl.program_id(1)
        _, vjp = jax.vjp(fn, *[r[...].astype(F32) for r in in_refs])
        grads = vjp(tuple(c[...].astype(F32) for c in ct_refs))
        for o, i in zip(out_refs, idx):
            g = grads[i]
            if kinds[i] == "row":
                o[...] = g.astype(o.dtype)
            else:
                first = (j == 0) if kinds[i] == "batch" else ((j == 0) & (b == 0))

                @pl.when(first)
                def _(o=o, g=g):
                    o[...] = g

                @pl.when(jnp.logical_not(first))
                def _(o=o, g=g):
                    o[...] += g

    specs = _row_specs(ins, kinds, tr)
    return pl.pallas_call(
        body, name=name, grid=(B, n // tr),
        in_specs=specs + [pl.BlockSpec((None, tr, c.shape[2]), lambda b, j: (b, j, 0)) for c in cts],
        out_specs=[specs[i] for i in idx],
        out_shape=[jax.ShapeDtypeStruct(ins[i].shape, gdtypes[i]) for i in idx],
        compiler_params=_params("arbitrary", "arbitrary"),
    )(*ins, *cts)


ROW_TILE = 512


def _rms_mod(x, g, sc, sh):
    y = x * lax.rsqrt(jnp.mean(x * x, axis=-1, keepdims=True) + EPS)
    return (y * g) * (1.0 + sc) + sh


def _norm_tile(x, g, sc, sh):
    return x, _rms_mod(x, g, sc, sh)


def _res_norm_tile(xb, y, gate, g, sc, sh):
    x = xb + gate * y
    return x, _rms_mod(x, g, sc, sh)


_NORM_KINDS = ("row", "glob", "batch", "batch")
_RES_NORM_KINDS = ("row", "row", "batch", "glob", "batch", "batch")


def _norm_fwd(tag, st, g, sc, sh, exch=None):
    xb, y, gate = st
    tr = min(ROW_TILE, xb.shape[1])
    d = xb.shape[2]
    if y is None:
        return [xb, *_rowwise_fwd(lambda *a: (_rms_mod(*a),), [xb, g, sc, sh], _NORM_KINDS, [(d, BF16)], tr,
                                  name=f"{tag}_fwd", transposed=0, exch=exch)]
    return _rowwise_fwd(_res_norm_tile, [xb, y, gate, g, sc, sh], _RES_NORM_KINDS, [(d, F32), (d, BF16)], tr,
                        name=f"{tag}_fwd", transposed=1, exch=exch)


def _norm_bwd(tag, st, g, sc, sh, dx, dh):
    xb, y, gate = st
    tr = min(ROW_TILE, xb.shape[1])
    if y is None:
        dxb, dg, dsc, dsh = _rowwise_bwd(_norm_tile, [xb, g, sc, sh], _NORM_KINDS, [F32] * 4, [dx, dh], tr, name=f"{tag}_bwd")
        return dxb, None, None, dg, dsc, dsh
    return tuple(_rowwise_bwd(_res_norm_tile, [xb, y, gate, g, sc, sh], _RES_NORM_KINDS, [F32, BF16, F32, F32, F32, F32],
                              [dx, dh], tr, name=f"{tag}_bwd"))


def _loss_head(st, final_g, target, *, name):
    xb, y, gate = st
    B, n, d = xb.shape
    tr = min(ROW_TILE, n)

    def tile_loss(xv, yv, gt, g, t):
        x = xv + gt * yv
        out = x * lax.rsqrt(jnp.mean(x * x, axis=-1, keepdims=True) + EPS) * g
        return 0.5 * jnp.sum(jnp.mean(jnp.square(out - t), axis=-1))

    def body(x_ref, y_ref, gate_ref, g_ref, t_ref, loss_ref, dx_ref, dy_ref, dgate_ref, dg_ref):
        b, j = pl.program_id(0), pl.program_id(1)
        val, (dx, dy, dgate, dg) = jax.value_and_grad(tile_loss, argnums=(0, 1, 2, 3))(
            x_ref[...], y_ref[...], gate_ref[...], g_ref[...], t_ref[...])
        dx_ref[...] = dx
        dy_ref[...] = dy.astype(dy_ref.dtype)

        @pl.when(j == 0)
        def _():
            loss_ref[...] = jnp.zeros_like(loss_ref)
            dgate_ref[...] = jnp.zeros_like(dgate_ref)

        @pl.when((j == 0) & (b == 0))
        def _():
            dg_ref[...] = jnp.zeros_like(dg_ref)

        loss_ref[...] += jnp.full(loss_ref.shape, val, F32)
        dgate_ref[...] += dgate
        dg_ref[...] += dg

    row = pl.BlockSpec((None, tr, d), lambda b, j: (b, j, 0))
    per_sample = pl.BlockSpec((None, 1, d), lambda b, j: (b, 0, 0))
    whole = pl.BlockSpec((1, d), lambda b, j: (0, 0))
    return pl.pallas_call(
        body, name=name, grid=(B, n // tr),
        in_specs=[row, row, per_sample, whole, row],
        out_specs=[pl.BlockSpec((None, 1, 128), lambda b, j: (b, 0, 0)), row, row, per_sample, whole],
        out_shape=[jax.ShapeDtypeStruct((B, 1, 128), F32), jax.ShapeDtypeStruct((B, n, d), F32),
                   jax.ShapeDtypeStruct((B, n, d), BF16), jax.ShapeDtypeStruct((B, 1, d), F32), jax.ShapeDtypeStruct((1, d), F32)],
        compiler_params=_params("arbitrary", "arbitrary"),
    )(xb, y, gate, final_g, target)


def _exchange(arrs, *, scatter, name):
    k = len(arrs)

    def body(*refs):
        ins, outs, sems = refs[:k], refs[k:2 * k], refs[2 * k:]
        _exch_start(ins, outs, sems, scatter)
        _exch_wait(ins, outs, sems, scatter)

    any_spec = pl.BlockSpec(memory_space=pl.ANY)
    return pl.pallas_call(
        body, name=name,
        in_specs=[any_spec] * k, out_specs=[any_spec] * k,
        out_shape=_exch_out_shapes(arrs, scatter), scratch_shapes=_exch_sems(k),
        compiler_params=pltpu.CompilerParams(has_side_effects=True),
    )(*arrs)


def _exch_flags(scatter, k):
    return [scatter] * k if isinstance(scatter, bool) else list(scatter)


def _exch_out_shapes(arrs, scatter):
    return [jax.ShapeDtypeStruct(a.shape if f else (N_DEV,) + a.shape, a.dtype)
            for a, f in zip(arrs, _exch_flags(scatter, len(arrs)))]


def _exch_sems(k):
    return [pltpu.SemaphoreType.DMA((k * (N_DEV - 1),)), pltpu.SemaphoreType.DMA((k * (N_DEV - 1),)),
            pltpu.SemaphoreType.DMA((k,))]


def _exch_copies(ins, outs, sems, scatter):
    send_sems, recv_sems, local_sems = sems
    x, y, c = lax.axis_index("x"), lax.axis_index("y"), lax.axis_index("c")
    me = 4 * x + 2 * y + c
    owns, sends, recvs = [], [], []
    flags = _exch_flags(scatter, len(ins))
    for a in range(len(ins)):
        scatter = flags[a]
        owns.append(pltpu.make_async_copy(ins[a].at[me] if scatter else ins[a], outs[a].at[me], local_sems.at[a]))
        for r in range(1, N_DEV):
            fx, fy, fc = (r >> 2) & 1, (r >> 1) & 1, r & 1
            px, py, pc = (x + fx) % 2, (y + fy) % 2, (c + fc) % 2
            peer = 4 * px + 2 * py + pc
            s = a * (N_DEV - 1) + r - 1
            mk = functools.partial(pltpu.make_async_remote_copy, src_ref=ins[a].at[peer] if scatter else ins[a],
                                   send_sem=send_sems.at[s], recv_sem=recv_sems.at[s],
                                   device_id=(px, py, pc), device_id_type=pl.DeviceIdType.MESH)
            sends.append(mk(dst_ref=outs[a].at[me]))
            recvs.append(mk(dst_ref=outs[a].at[peer]))
    return owns, sends, recvs


def _exch_start(ins, outs, sems, scatter):
    owns, sends, _ = _exch_copies(ins, outs, sems, scatter)
    for cp in owns + sends:
        cp.start()


def _exch_wait(ins, outs, sems, scatter):
    owns, sends, recvs = _exch_copies(ins, outs, sems, scatter)
    for rc in recvs:
        rc.wait_recv()
    for cp in sends:
        cp.wait_send()
    for own in owns:
        own.wait()


MOD_ROWS = 48


def _mod_tile(cc, w, b):
    s = cc * jax.nn.sigmoid(cc)
    return lax.dot_general(s.astype(BF16), w.astype(BF16), (((1,), (0,)), ((), ())), preferred_element_type=F32) + b


def _mod_fwd(cc, w_mod, b_shard, *, name):
    L, d, wcols = w_mod.shape

    def body(cc_ref, w_ref, b_ref, o_ref):
        o_ref[...] = _mod_tile(cc_ref[...], w_ref[...], b_ref[...])

    return pl.pallas_call(
        body, name=name, grid=(L,),
        in_specs=[pl.BlockSpec((MOD_ROWS, d), lambda l: (0, 0)), pl.BlockSpec((None, d, wcols), lambda l: (l, 0, 0)),
                  pl.BlockSpec((None, 1, wcols), lambda l: (l, 0, 0))],
        out_specs=pl.BlockSpec((None, MOD_ROWS, wcols), lambda l: (l, 0, 0)),
        out_shape=jax.ShapeDtypeStruct((L, MOD_ROWS, wcols), F32),
        compiler_params=_params("parallel"),
    )(cc, w_mod, b_shard)


def _mod_bwd(cc, w_mod, b_shard, dm, *, name):
    L, d, wcols = w_mod.shape

    def body(cc_ref, w_ref, b_ref, dm_ref, dcc_ref, dw_ref):
        _, vjp = jax.vjp(_mod_tile, cc_ref[...], w_ref[...], b_ref[...])
        dcc, dw, _ = vjp(dm_ref[...])
        dw_ref[...] = dw

        @pl.when(pl.program_id(0) == 0)
        def _():
            dcc_ref[...] = dcc

        @pl.when(pl.program_id(0) > 0)
        def _():
            dcc_ref[...] += dcc

    return pl.pallas_call(
        body, name=name, grid=(L,),
        in_specs=[pl.BlockSpec((MOD_ROWS, d), lambda l: (0, 0)), pl.BlockSpec((None, d, wcols), lambda l: (l, 0, 0)),
                  pl.BlockSpec((None, 1, wcols), lambda l: (l, 0, 0)), pl.BlockSpec((None, MOD_ROWS, wcols), lambda l: (l, 0, 0))],
        out_specs=[pl.BlockSpec((MOD_ROWS, d), lambda l: (0, 0)), pl.BlockSpec((None, d, wcols), lambda l: (l, 0, 0))],
        out_shape=[jax.ShapeDtypeStruct((MOD_ROWS, d), F32), jax.ShapeDtypeStruct((L, d, wcols), F32)],
        compiler_params=_params("arbitrary"),
    )(cc, w_mod, b_shard, dm)


def _sum_leading(a, *, name):
    K, R, C = a.shape
    tr = _tile8(R, 256)

    def body(a_ref, o_ref):
        acc = a_ref[0].astype(F32)
        for i in range(1, K):
            acc = acc + a_ref[i].astype(F32)
        o_ref[...] = acc

    return pl.pallas_call(
        body, name=name, grid=(R // tr,),
        in_specs=[pl.BlockSpec((K, tr, C), lambda i: (0, i, 0))],
        out_specs=pl.BlockSpec((tr, C), lambda i: (i, 0)),
        out_shape=jax.ShapeDtypeStruct((R, C), F32),
        compiler_params=_params("parallel"),
    )(a)


def _tile8(dim, target):
    if dim <= target:
        return dim
    t = (target // 8) * 8
    while t >= 8:
        if dim % t == 0:
            return t
        t -= 8
    raise ValueError(f"no row tile for {dim}")


def _adamw_math(g, w, m, v):
    m = ADAM_B1 * m + (1.0 - ADAM_B1) * g
    v = ADAM_B2 * v + (1.0 - ADAM_B2) * jnp.square(g)
    m_hat = m / (1.0 - ADAM_B1 ** ADAM_STEP)
    v_hat = v / (1.0 - ADAM_B2 ** ADAM_STEP)
    delta = -ADAM_LR * (m_hat / (jnp.sqrt(v_hat) + ADAM_EPS) + ADAM_WD * w)
    return delta, m, v


def _adamw(g, w, m, v, *, name, exch=None):
    L, R, C = w.shape
    parts = isinstance(g, (list, tuple))
    gs = list(g) if parts else [g]
    ng = len(gs)
    tr = _tile8(R, 256)

    def body(*refs):
        g_refs = refs[:ng]
        w_ref, m_ref, v_ref, go_ref, d_ref, mo_ref, vo_ref = refs[ng:]
        if parts:
            layer = pl.program_id(0)
            gv = None
            for li, g_ref in enumerate(g_refs):
                acc = g_ref[0].astype(F32)
                for i in range(1, N_DEV):
                    acc = acc + g_ref[i].astype(F32)
                gv = acc if gv is None else jnp.where(layer == li, acc, gv)
        else:
            gv = g_refs[0][...]
        go_ref[...] = gv
        d_ref[...], mo_ref[...], vo_ref[...] = _adamw_math(gv, w_ref[...], m_ref[...], v_ref[...])

    tile = pl.BlockSpec((None, tr, C), lambda l, i: (l, i, 0))
    g_specs = [pl.BlockSpec((N_DEV, tr, C), lambda l, i: (0, i, 0))] * ng if parts else [tile]
    grid = (L, R // tr)
    body, x_in, x_out, x_shapes, x_sems = _riding(body, exch, ng + 3, 4, grid)
    return pl.pallas_call(
        body, name=name, grid=grid,
        in_specs=g_specs + [tile, tile, tile] + x_in, out_specs=[tile] * 4 + x_out,
        out_shape=[jax.ShapeDtypeStruct((L, R, C), F32)] * 4 + x_shapes,
        scratch_shapes=x_sems,
        compiler_params=_params("arbitrary", "arbitrary") if exch else _params("parallel", "parallel"),
    )(*gs, w, m, v, *(exch[0] if exch else []))


def _adamw_small(gs, ws, ms, vs, *, name):
    k = len(ws)

    def body(*refs):
        g_refs, w_refs, m_refs, v_refs = refs[:k], refs[k:2 * k], refs[2 * k:3 * k], refs[3 * k:4 * k]
        d_refs, mo_refs, vo_refs = refs[4 * k:5 * k], refs[5 * k:6 * k], refs[6 * k:]
        for i in range(k):
            d_refs[i][...], mo_refs[i][...], vo_refs[i][...] = _adamw_math(g_refs[i][...], w_refs[i][...], m_refs[i][...],
                                                                         v_refs[i][...])

    shapes = [jax.ShapeDtypeStruct(a.shape, F32) for a in ws]
    out = pl.pallas_call(body, name=name, out_shape=shapes * 3, compiler_params=pltpu.CompilerParams(vmem_limit_bytes=VMEM_LIMIT))(
        *gs, *ws, *ms, *vs)
    return out[:k], out[k:2 * k], out[2 * k:]


def _block_diag(w):
    g, c, d = w.shape
    return (w[:, :, None, :] * jnp.eye(g, dtype=w.dtype)[:, None, :, None]).reshape(g * c, g * d)


def _diag_blocks(w_bd):
    g = POOL_W // POOL_GROUP
    return jnp.stack([w_bd[i * POOL_GROUP:(i + 1) * POOL_GROUP, i * POOL_GROUP:(i + 1) * POOL_GROUP] for i in range(g)])


def _flat(a):
    return a.reshape(-1, a.shape[-1])


def _mix_half_fwd(tag, st, mods, wl, kvc, *, local, kv_only, exch=None, normed=None):
    sh1, sc1, g1 = mods[:3]
    B, n, d = st[0].shape
    x, h, h_t = normed if normed is not None else _norm_fwd(f"{tag}_norm1", st, wl["n1"], sc1, sh1)
    if kv_only:
        kv = _mm(_flat(h), wl["w_in"][:, ATTN_W:ATTN_W + 2 * KV_W], name=f"{tag}_kv").reshape(B, n, 2 * KV_W)
        return None, dict(st=st, h_t=h_t, kvc=kv), []
    u = _mm(_flat(h), wl["w_in"], name=f"{tag}_in", tn=IN_W).reshape(B, n, IN_W)
    if not local:
        kvc = u[:, :, ATTN_W:ATTN_W + 2 * KV_W]
    mix, conv_acc, got = _mixer_fwd(f"{tag}_mix", u, kvc, wl["margs"], local, exch)
    y = _mm(_flat(mix), wl["w_out"], name=f"{tag}_out", tn=D_MODEL).reshape(B, n, d)
    return (x, y, g1), dict(st=st, h_t=h_t, u=u, kvc=kvc, mix=mix, conv_acc=conv_acc), got


def _ffn_half_fwd(tag, st2, mods, wl, exch=None):
    sh2, sc2, g2 = mods[3:]
    B, n, d = st2[0].shape
    x1, h2, h2_t = _norm_fwd(f"{tag}_norm2", st2, wl["n2"], sc2, sh2)
    gu, act, act_t, *got = _mm_swiglu(_flat(h2), wl["w_ffn_in"], name=f"{tag}_ffn_in", exch=exch)
    y2 = _mm(act, wl["w_ffn_out"], name=f"{tag}_ffn_out", tn=D_MODEL).reshape(B, n, d)
    return (x1, y2, g2), dict(st2=st2, h2_t=h2_t, gu=gu, act_t=act_t), got


def _ffn_half_bwd(tag, sv, mods, wl, dx1, dy2):
    sh2, sc2, _ = mods[3:]
    B, n, d = sv["st2"][0].shape
    gw = {}
    dy2f = _flat(dy2)
    gw["w_ffn_out"] = _mm(sv["act_t"], dy2f, out_dtype=BF16, name=f"{tag}_ffn_out_dw")
    dgu = _mm_dswiglu(dy2f, wl["w_ffn_out"], sv["gu"], name=f"{tag}_ffn_out_dx")
    dh2 = _mm(dgu, wl["w_ffn_in"], trans_b=True, name=f"{tag}_ffn_in_dx").reshape(B, n, d)
    gw["w_ffn_in"] = _mm(sv["h2_t"], dgu, out_dtype=BF16, tn=FF_TILE, out_block=_natural_block, name=f"{tag}_ffn_in_dw")
    dx, dy, dg1, gw["n2"], dsc2, dsh2 = _norm_bwd(f"{tag}_norm2", sv["st2"], wl["n2"], sc2, sh2, dx1, dh2)
    return (dx, dy, dg1), gw, dict(sh2=dsh2, sc2=dsc2)


def _mix_half_bwd(tag, sv, mods, wl, dx, dy, dkv_in, *, local, kv_only, exch=None):
    sh1, sc1, _ = mods[:3]
    B, n, d = sv["st"][0].shape
    gw = {}
    if kv_only:
        dkv = _flat(dkv_in).astype(BF16)
        dh = _mm(dkv, wl["w_in"][:, ATTN_W:ATTN_W + 2 * KV_W], trans_b=True, name=f"{tag}_kv_dx").reshape(B, n, d)
        gw["w_in_kv"] = _mm(sv["h_t"], dkv, out_dtype=BF16, name=f"{tag}_kv_dw")
        dxb, dy_prev, dgate_prev, gw["n1"], dsc1, dsh1 = _norm_bwd(f"{tag}_norm1", sv["st"], wl["n1"], sc1, sh1,
                                                                    jnp.zeros((B, n, d), F32), dh)
        return (dxb, dy_prev, dgate_prev), gw, dict(sh1=dsh1, sc1=dsc1), None, []

    dyf = _flat(dy)
    dmix = _mm(dyf, wl["w_out"], trans_b=True, name=f"{tag}_out_dx", tn=D_MODEL).reshape(B, n, d)
    gw["w_out"] = _mm(_flat(sv["mix"]).T, dyf, out_dtype=BF16, tn=DW_TN, name=f"{tag}_out_dw")
    (dq, dk, dv, da, dg, dpu), (dkc, dvc), gw["margs"], got = _mixer_bwd(f"{tag}_mix", sv["u"], sv["kvc"], sv["conv_acc"],
                                                                     wl["margs"], dmix, local, exch)
    if local:
        dkv_out = jnp.concatenate([dkc, dvc], axis=-1)
    else:
        dk = dkc + dkv_in[:, :, :KV_W]
        dv = dvc + dkv_in[:, :, KV_W:]
        dkv_out = None
    du = _flat(jnp.concatenate([dq, dk, dv, da, dg, dpu], axis=-1).astype(BF16))
    dh = _mm(du, wl["w_in"], trans_b=True, name=f"{tag}_in_dx", tn=D_MODEL).reshape(B, n, d)
    gw["w_in"] = _mm(sv["h_t"], du, out_dtype=BF16, tn=DW_TN, name=f"{tag}_in_dw")
    dxb, dy_prev, dgate_prev, gw["n1"], dsc1, dsh1 = _norm_bwd(f"{tag}_norm1", sv["st"], wl["n1"], sc1, sh1, dx, dh)
    return (dxb, dy_prev, dgate_prev), gw, dict(sh1=dsh1, sc1=dsc1), dkv_out, got


BIG_W = ("w_in", "w_out", "w_ffn_in", "w_ffn_out")


def _local_step(x, ctx, m_loc, m_ctx, p, final_g, target, big):
    B = x.shape[0]
    depth = m_loc.shape[0]
    lat_mods = [[t[:, None, :] for t in jnp.split(m_loc[l], 6, axis=-1)] for l in range(depth)]
    ctx_mods = [[jnp.broadcast_to(t[None, None, :], (B, 1, D_MODEL)) for t in jnp.split(m_ctx[l], 6)] for l in range(depth)]

    st, cst = (x, None, None), (ctx, None, None)
    w_mix, w_ffn, sv_mix, sv_ffn, csv_mix, csv_ffn = [], [], [], [], [], []
    *normed, = _norm_fwd("l0_norm1", st, p["norm1_g"][0][None, :], lat_mods[0][1], lat_mods[0][0], exch=big.ride_first())
    normed, got = normed[:3], normed[3:]
    for l in range(depth):
        last = l == depth - 1
        wb = big.mix_weights(l, got)
        wm = dict(n1=p["norm1_g"][l][None, :], w_in=wb["w_in"], w_out=wb["w_out"],
                  margs=(p["attn_sink"][l][None, :], p["conv_dw"][l], p["conv_dw_b"][l][None, :], p["conv_ln_g"][l][None, :],
                         p["conv_ln_b"][l][None, :], _block_diag(p["pool_w"][l]), p["pool_scale"][l][None, :]))
        cst, csv, _ = _mix_half_fwd(f"l{l}c", cst, ctx_mods[l], wm, None, local=False, kv_only=last)
        st, sv, got = _mix_half_fwd(f"l{l}", st, lat_mods[l], wm, csv["kvc"], local=True, kv_only=False,
                                    exch=big.ride_attn_fwd(l), normed=normed if l == 0 else None)
        w_mix.append(wm)
        sv_mix.append(sv)
        csv_mix.append(csv)
        wb = big.ffn_weights(l, got)
        w_ffn_in = _interleave_ffn(wb["w_ffn_in"], name=f"l{l}_ffn_in_interleave")
        wf = dict(n2=p["norm2_g"][l][None, :], w_ffn_in=w_ffn_in, w_ffn_out=wb["w_ffn_out"])
        csv = None
        if not last:
            cst, csv, _ = _ffn_half_fwd(f"l{l}c", cst, ctx_mods[l], wf)
        st, sv, got = _ffn_half_fwd(f"l{l}", st, lat_mods[l], wf, exch=big.ride_ffn_fwd(l))
        w_ffn.append(wf)
        sv_ffn.append(sv)
        csv_ffn.append(csv)
    loss_rows, dx, dy, dgate, dfinal = _loss_head(st, final_g[None, :], target, name="loss_head")

    dm_loc, dm_ctx = [None] * depth, [None] * depth
    small = [None] * depth
    cdx = cdy = cdgate = None
    up_mix = None
    for l in reversed(range(depth)):
        last = l == depth - 1
        dm, cdm = dict(g2=dgate), {}
        (dx, dy, dm["g1"]), gf, d = _ffn_half_bwd(f"l{l}", sv_ffn[l], lat_mods[l], w_ffn[l], dx, dy)
        dm.update(d)
        if not last:
            cdm["g2"] = cdgate
            (cdx, cdy, cdm["g1"]), cgf, d = _ffn_half_bwd(f"l{l}c", csv_ffn[l], ctx_mods[l], w_ffn[l], cdx, cdy)
            cdm.update(d)
            gf = {k: gf[k] + cgf[k] for k in gf}
        ffn_grads = {k: gf[k] for k in _ShardedWeights.FFN}
        (dx, dy, dgate), gm, d, dkv, got = _mix_half_bwd(f"l{l}", sv_mix[l], lat_mods[l], w_mix[l], dx, dy, None, local=True,
                                                        kv_only=False, exch=big.ride_attn_bwd(l, ffn_grads, up_mix))
        big.took(l, ffn_grads, up_mix, got)
        dm.update(d)
        (cdx, cdy, cdgate), cgm, d, _, _ = _mix_half_bwd(f"l{l}c", csv_mix[l], ctx_mods[l], w_mix[l], cdx, cdy, dkv,
                                                        local=False, kv_only=last)
        cdm.update(d)
        order = ("sh1", "sc1", "g1", "sh2", "sc2", "g2")
        dm_loc[l] = jnp.concatenate([dm[k][:, 0, :] for k in order], axis=-1)
        dm_ctx[l] = jnp.concatenate([jnp.sum(cdm[k][:, 0, :], axis=0) if k in cdm else jnp.zeros((D_MODEL,), F32)
                                     for k in order])
        if last:
            up_mix = dict(w_in=gm["w_in"].at[:, ATTN_W:ATTN_W + 2 * KV_W].add(cgm["w_in_kv"]), w_out=gm["w_out"])
            margs = gm["margs"]
        else:
            up_mix = {k: gm[k] + cgm[k] for k in _ShardedWeights.MIX}
            margs = tuple(a + b for a, b in zip(gm["margs"], cgm["margs"]))
        small[l] = dict(n1=gm["n1"] + cgm["n1"], n2=gf["n2"], margs=margs)
    big.leftover(up_mix)

    stack = lambda f: jnp.stack([f(small[l]) for l in range(depth)])
    dp = dict(
        norm1_g=stack(lambda g: g["n1"][0]), norm2_g=stack(lambda g: g["n2"][0]),
        attn_sink=stack(lambda g: g["margs"][0][0]), conv_dw=stack(lambda g: g["margs"][1]),
        conv_dw_b=stack(lambda g: g["margs"][2][0]), conv_ln_g=stack(lambda g: g["margs"][3][0]),
        conv_ln_b=stack(lambda g: g["margs"][4][0]), pool_w=stack(lambda g: _diag_blocks(g["margs"][5])),
        pool_scale=stack(lambda g: g["margs"][6][0]))
    return jnp.sum(loss_rows[:, 0, 0]), dx, jnp.stack(dm_loc), jnp.stack(dm_ctx), dp, dfinal[0]


PACK_COLS = 1024


def _pack(arrs):
    flat = jnp.concatenate([a.reshape(-1).astype(F32) for a in arrs])
    rows = -(-flat.shape[0] // (8 * PACK_COLS)) * 8
    return jnp.pad(flat, (0, rows * PACK_COLS - flat.shape[0])).reshape(rows, PACK_COLS)


def _unpack(slab, like):
    flat = slab.reshape(-1)
    out, off = [], 0
    for a in like:
        out.append(flat[off:off + a.size].reshape(a.shape))
        off += a.size
    return out


def _shard_cols(gathered):
    _, L, R, C = gathered.shape
    return jnp.transpose(gathered, (1, 2, 0, 3)).reshape(L, R, N_DEV * C)


class _ShardedWeights:
    MIX = ("w_in", "w_out")
    FFN = ("w_ffn_in", "w_ffn_out")
    BY_COLS = ("w_in", "w_ffn_in")

    def __init__(self, shards):
        self.shards = shards
        self.depth = shards[BIG_W[0]].shape[0]
        self.parts = [dict() for _ in range(self.depth)]
        self.left = None

    def _join(self, names, blocks):
        out = {}
        for name, g in zip(names, blocks):
            _, R, C = g.shape
            out[name] = jnp.transpose(g, (1, 0, 2)).reshape(R, N_DEV * C) if name in self.BY_COLS else g.reshape(N_DEV * R, C)
        return out

    def cut(self, names, grads):
        out = []
        for name in names:
            g = grads[name]
            if name in self.BY_COLS:
                R, C8 = g.shape
                out.append(jnp.transpose(g.reshape(R, N_DEV, C8 // N_DEV), (1, 0, 2)))
            else:
                R8, C = g.shape
                out.append(g.reshape(N_DEV, R8 // N_DEV, C))
        return out

    def ride_first(self):
        return [self.shards[name][0] for name in self.MIX], False

    def mix_weights(self, l, got):
        return self._join(self.MIX, got)

    def ffn_weights(self, l, got):
        return self._join(self.FFN, got)

    def ride_attn_fwd(self, l):
        return [self.shards[name][l] for name in self.FFN], False

    def ride_ffn_fwd(self, l):
        if l + 1 >= self.depth:
            return None
        return [self.shards[name][l + 1] for name in self.MIX], False

    def ride_attn_bwd(self, l, ffn_grads, up_mix):
        return self.cut(self.FFN, ffn_grads) + (self.cut(self.MIX, up_mix) if up_mix is not None else []), True

    def took(self, l, ffn_grads, up_mix, got):
        self.parts[l].update(zip(self.FFN, got[:2]))
        if up_mix is not None:
            self.parts[l + 1].update(zip(self.MIX, got[2:]))

    def leftover(self, mix_grads):
        self.left = mix_grads


def _as_rows(a, leading=0):
    return a.reshape(*a.shape[:leading], -1, PACK_COLS)


SMALL = ("c_ctx", "b_mod", "norm1_g", "norm2_g", "conv_dw_b", "conv_ln_g", "conv_ln_b", "attn_sink", "pool_w",
         "pool_scale", "final_g", "conv_dw")
BIG = ("w_mod", "w_in", "w_out", "w_ffn_in", "w_ffn_out")
ORDER = ("c_ctx", "w_mod", "b_mod", "norm1_g", "norm2_g", "w_in", "conv_dw", "conv_dw_b", "conv_ln_g", "conv_ln_b",
         "attn_sink", "pool_w", "pool_scale", "w_out", "w_ffn_in", "w_ffn_out", "final_g")


def kernel(x, c, ctx, c_ctx, w_mod, b_mod, norm1_g, norm2_g, w_in, conv_dw, conv_dw_b, conv_ln_g, conv_ln_b, attn_sink, pool_w, pool_scale, w_out, w_ffn_in, w_ffn_out, final_g, loss_target, m_c_ctx, m_w_mod, m_b_mod, m_norm1_g, m_norm2_g, m_w_in, m_conv_dw, m_conv_dw_b, m_conv_ln_g, m_conv_ln_b, m_attn_sink, m_pool_w, m_pool_scale, m_w_out, m_w_ffn_in, m_w_ffn_out, m_final_g, v_c_ctx, v_w_mod, v_b_mod, v_norm1_g, v_norm2_g, v_w_in, v_conv_dw, v_conv_dw_b, v_conv_ln_g, v_conv_ln_b, v_attn_sink, v_pool_w, v_pool_scale, v_w_out, v_w_ffn_in, v_w_ffn_out, v_final_g):
    w = dict(c_ctx=c_ctx, w_mod=w_mod, b_mod=b_mod, norm1_g=norm1_g, norm2_g=norm2_g, w_in=w_in, conv_dw=conv_dw,
             conv_dw_b=conv_dw_b, conv_ln_g=conv_ln_g, conv_ln_b=conv_ln_b, attn_sink=attn_sink, pool_w=pool_w,
             pool_scale=pool_scale, w_out=w_out, w_ffn_in=w_ffn_in, w_ffn_out=w_ffn_out, final_g=final_g)
    mom = dict(c_ctx=m_c_ctx, w_mod=m_w_mod, b_mod=m_b_mod, norm1_g=m_norm1_g, norm2_g=m_norm2_g, w_in=m_w_in,
               conv_dw=m_conv_dw, conv_dw_b=m_conv_dw_b, conv_ln_g=m_conv_ln_g, conv_ln_b=m_conv_ln_b,
               attn_sink=m_attn_sink, pool_w=m_pool_w, pool_scale=m_pool_scale, w_out=m_w_out, w_ffn_in=m_w_ffn_in,
               w_ffn_out=m_w_ffn_out, final_g=m_final_g)
    var = dict(c_ctx=v_c_ctx, w_mod=v_w_mod, b_mod=v_b_mod, norm1_g=v_norm1_g, norm2_g=v_norm2_g, w_in=v_w_in,
               conv_dw=v_conv_dw, conv_dw_b=v_conv_dw_b, conv_ln_g=v_conv_ln_g, conv_ln_b=v_conv_ln_b,
               attn_sink=v_attn_sink, pool_w=v_pool_w, pool_scale=v_pool_scale, w_out=v_w_out, w_ffn_in=v_w_ffn_in,
               w_ffn_out=v_w_ffn_out, final_g=v_final_g)
    B = x.shape[0]
    depth = w_mod.shape[0]
    mod_cols = w_mod.shape[2]
    dw_cols = conv_dw.shape[2]
    me = 4 * lax.axis_index("x") + 2 * lax.axis_index("y") + lax.axis_index("c")

    shards = {name: w[name].astype(BF16) for name in BIG_W}
    c_all, dw_all = _exchange([c, conv_dw], scatter=False, name="gather_first")
    big = _ShardedWeights(shards)
    p = dict(norm1_g=norm1_g, norm2_g=norm2_g, conv_dw=_shard_cols(dw_all), conv_dw_b=conv_dw_b, conv_ln_g=conv_ln_g,
             conv_ln_b=conv_ln_b, attn_sink=attn_sink, pool_w=pool_w, pool_scale=pool_scale)

    cc = jnp.concatenate([c_all.reshape(N_DEV * B, D_MODEL), jnp.broadcast_to(c_ctx[None, :], (N_DEV, D_MODEL)),
                          jnp.zeros((MOD_ROWS - N_DEV * B - N_DEV, D_MODEL), F32)], axis=0)
    b_shard = lax.dynamic_slice_in_dim(b_mod, me * mod_cols, mod_cols, axis=1)[:, None, :]
    m_part = _mod_fwd(cc, w_mod, b_shard, name="mod_fwd")
    m_all, = _exchange([m_part], scatter=False, name="gather_mod")
    m_full = _shard_cols(m_all)
    m_loc = lax.dynamic_slice_in_dim(m_full, me * B, B, axis=1)
    m_ctx = m_full[:, N_DEV * B, :]

    loss_part, dx, dm_loc, dm_ctx, dp, dfinal = _local_step(x, ctx, m_loc, m_ctx, p, final_g, loss_target, big)
    loss = lax.psum(loss_part, AXES)

    dm_rows = jnp.concatenate([dm_loc, dm_ctx[:, None, :], jnp.zeros((depth, 8 - B - 1, 6 * D_MODEL), F32)], axis=1)
    small_like = [norm1_g, norm2_g, conv_dw_b, conv_ln_g, conv_ln_b, attn_sink, pool_w, pool_scale, final_g, dp["conv_dw"]]
    small_part = _pack([dp["norm1_g"], dp["norm2_g"], dp["conv_dw_b"], dp["conv_ln_g"], dp["conv_ln_b"], dp["attn_sink"],
                        dp["pool_w"], dp["pool_scale"], dfinal, dp["conv_dw"]])
    g, delta, new_m, new_v = {}, {}, {}, {}
    first_ffn = big.FFN[0]
    g[first_ffn], delta[first_ffn], new_m[first_ffn], new_v[first_ffn], dm_all, small_all, *last = _adamw(
        [big.parts[l][first_ffn] for l in range(depth)], w[first_ffn], mom[first_ffn], var[first_ffn],
        name=f"adamw_{first_ffn}", exch=([dm_rows, small_part] + big.cut(big.MIX, big.left), [False, False, True, True]))
    big.parts[0].update(zip(big.MIX, last))
    dm_full = jnp.concatenate([
        jnp.transpose(dm_all[:, :, :B, :], (1, 0, 2, 3)).reshape(depth, N_DEV * B, 6 * D_MODEL),
        jnp.transpose(dm_all[:, :, B, :], (1, 0, 2)),
        jnp.zeros((depth, MOD_ROWS - N_DEV * B - N_DEV, 6 * D_MODEL), F32)], axis=1)
    g_b_mod = jnp.stack([_sum_leading(dm_full[l][:, None, :], name=f"b_mod_grad{l}")[0] for l in range(depth)])
    dm_mine = lax.dynamic_slice_in_dim(dm_full, me * mod_cols, mod_cols, axis=2)
    dcc, g_w_mod = _mod_bwd(cc, w_mod, b_shard, dm_mine, name="mod_bwd")
    g_c_ctx_part = jnp.sum(dcc[N_DEV * B:N_DEV * B + N_DEV], axis=0)

    small_sum = _unpack(_sum_leading(small_all, name="sum_small"), small_like)
    g.update(zip(("norm1_g", "norm2_g", "conv_dw_b", "conv_ln_g", "conv_ln_b", "attn_sink", "pool_w", "pool_scale",
                  "final_g"), small_sum[:-1]))
    g["b_mod"] = g_b_mod
    g["conv_dw"] = lax.dynamic_slice_in_dim(small_sum[-1], me * dw_cols, dw_cols, axis=2)
    c_ctx_all, = _exchange([g_c_ctx_part.reshape(8, D_MODEL // 8)], scatter=False, name="gather_c_ctx")
    g["c_ctx"] = _sum_leading(c_ctx_all, name="sum_c_ctx").reshape(D_MODEL)

    for name in big.FFN[1:] + big.MIX:
        g[name], delta[name], new_m[name], new_v[name] = _adamw(
            [big.parts[l][name] for l in range(depth)], w[name], mom[name], var[name], name=f"adamw_{name}")
    g["w_mod"], delta["w_mod"], new_m["w_mod"], new_v["w_mod"] = _adamw(g_w_mod, w_mod, m_w_mod, v_w_mod, name="adamw_w_mod")
    res = _adamw_small([g[k] for k in SMALL], [w[k] for k in SMALL], [mom[k] for k in SMALL], [var[k] for k in SMALL],
                       name="adamw_small")
    for dst, arrs in zip((delta, new_m, new_v), res):
        dst.update(zip(SMALL, arrs))

    return (loss, dx, *[g[k] for k in ORDER], *[delta[k] for k in ORDER], *[new_m[k] for k in ORDER],
            *[new_v[k] for k in ORDER])
```

```python
import functools

import numpy as np
import jax
import jax.numpy as jnp
from jax import lax
from jax.experimental import pallas as pl
from jax.experimental.pallas import tpu as pltpu

F32 = jnp.float32
BF16 = jnp.bfloat16

D_MODEL = 1024
GRID_W = 64
HEAD_DIM = 64
ATTN_W = 512
CONV_W = 256
POOL_W = 256
ATTN_HEADS = 8
KV_HEADS = 2
GROUP = ATTN_HEADS // KV_HEADS
KV_W = KV_HEADS * HEAD_DIM
IN_W = ATTN_W + 2 * KV_W + 2 * CONV_W + POOL_W
WINDOW = 128
Q_BLOCK = 128
SPAN = Q_BLOCK + 2 * WINDOW
CONV_KERNEL = 31
POOL_WINDOWS = (2, 4, 8, 16)
POOL_GROUP = 64
ROPE_BASE = 10000.0
D_FF = 2816
EPS = 1e-6
NEG = -1e30
N_DEV = 8
AXES = ("x", "y", "c")

ADAM_LR = 0.001
ADAM_B1 = 0.9
ADAM_B2 = 0.999
ADAM_EPS = 1e-08
ADAM_WD = 0.01
ADAM_STEP = 10

VMEM_LIMIT = 56 * 1024 * 1024
HALO = 16
SEQ_CHUNK = 1024
CONV_BWD_CHUNK = 512


def _params(*sem):
    return pltpu.CompilerParams(dimension_semantics=sem, vmem_limit_bytes=VMEM_LIMIT)


def _tile(dim, target):
    if dim <= target:
        return dim
    t = (target // 128) * 128
    while t >= 128:
        if dim % t == 0:
            return t
        t -= 128
    raise ValueError(f"no tile for {dim}")


MM_VMEM_BUDGET = 44 * 1024 * 1024
DW_TN = 256


def _dot(a, b):
    return lax.dot_general(a.astype(BF16), b.astype(BF16), (((1,), (0,)), ((), ())), preferred_element_type=F32)


def _mm_vmem(tm, tn, tk, whole, a_bytes, b_bytes, o_bytes):
    return 2 * (tm * tk * a_bytes + tk * tn * b_bytes + tm * tn * o_bytes) + (0 if whole else tm * tn * 4)


def _mm(a, b, *, name, out_dtype=F32, tm=1408, tn=512, trans_b=False, out_block=None, plus=None):
    M, K = a.shape
    N, K2 = b.shape if trans_b else b.shape[::-1]
    assert K == K2, (a.shape, b.shape)
    tm = _tile(M, tm)
    tn = _tile(N, tn)
    sizes = (a.dtype.itemsize, b.dtype.itemsize, jnp.dtype(out_dtype).itemsize)
    tk = next(t for t in range(K, 0, -128) if K % t == 0 and _mm_vmem(tm, tn, t, t == K, *sizes) <= MM_VMEM_BUDGET)
    nk = K // tk

    def body(a_ref, b_ref, *rest):
        plus_ref = rest[0] if plus is not None else None
        o_ref, *scratch = rest[1:] if plus is not None else rest
        part = (_dot_nt if trans_b else _dot)(a_ref[...], b_ref[...])

        def store(total):
            if plus_ref is not None:
                total = total + plus_ref[...].astype(F32)
            o_ref[...] = total.astype(o_ref.dtype)

        if nk == 1:
            store(part)
        else:
            acc_ref, = scratch
            k = pl.program_id(2)

            @pl.when(k == 0)
            def _():
                acc_ref[...] = part

            @pl.when(k > 0)
            def _():
                acc_ref[...] += part

            @pl.when(k == nk - 1)
            def _():
                store(acc_ref[...])

    out_spec = pl.BlockSpec((tm, tn), (lambda i, j, k: (i, j)) if out_block is None else (lambda i, j, k: (i, out_block(j))))
    return pl.pallas_call(
        body, name=name, grid=(M // tm, N // tn, nk),
        in_specs=[pl.BlockSpec((tm, tk), lambda i, j, k: (i, k)),
                  pl.BlockSpec((tn, tk), lambda i, j, k: (j, k)) if trans_b else pl.BlockSpec((tk, tn), lambda i, j, k: (k, j))]
        + ([out_spec] if plus is not None else []),
        out_specs=out_spec,
        out_shape=jax.ShapeDtypeStruct((M, N), out_dtype),
        scratch_shapes=[pltpu.VMEM((tm, tn), F32)] if nk > 1 else [],
        compiler_params=_params("parallel", "parallel", "arbitrary"),
    )(a, b, *([plus] if plus is not None else []))


FF_TILE = 256


FF_TILES = D_FF // FF_TILE


def _natural_block(j):
    return (j % 2) * FF_TILES + j // 2


def _interleave_ffn(w, *, name):
    R, C = w.shape

    def body(w_ref, o_ref):
        o_ref[...] = w_ref[...]

    return pl.pallas_call(
        body, name=name, grid=(2 * FF_TILES,),
        in_specs=[pl.BlockSpec((R, FF_TILE), lambda j: (0, _natural_block(j)))],
        out_specs=pl.BlockSpec((R, FF_TILE), lambda j: (0, j)),
        out_shape=jax.ShapeDtypeStruct((R, C), w.dtype),
        compiler_params=_params("parallel"),
    )(w)


def _swiglu(gu):
    g, u = gu[:, :FF_TILE], gu[:, FF_TILE:]
    return g * jax.nn.sigmoid(g) * u


EPILOGUE_SPLIT = 1
FF_ROWS = 4096


def _mm_swiglu(a, w_il, *, name, tm=FF_ROWS, exch=None, split=EPILOGUE_SPLIT):
    M, K = a.shape
    tm = _tile(M, tm)
    rc = tm // split

    def body(a_ref, b_ref, gu_ref, act_ref, act_t_ref):
        b = b_ref[...]
        parts = [_dot(a_ref[r0:r0 + rc, :], b) for r0 in range(0, tm, rc)]
        for r0, gu in zip(range(0, tm, rc), parts):
            gu_ref[r0:r0 + rc, :] = gu.astype(gu_ref.dtype)
            act = _swiglu(gu)
            act_ref[r0:r0 + rc, :] = act.astype(act_ref.dtype)
            act_t_ref[:, r0:r0 + rc] = act.T.astype(act_t_ref.dtype)

    grid = (M // tm, D_FF // FF_TILE)
    body, x_in, x_out, x_shapes, x_sems = _riding(body, exch, 2, 3, grid)
    return pl.pallas_call(
        body, name=name, grid=grid,
        in_specs=[pl.BlockSpec((tm, K), lambda i, j: (i, 0)), pl.BlockSpec((K, 2 * FF_TILE), lambda i, j: (0, j))] + x_in,
        out_specs=[pl.BlockSpec((tm, 2 * FF_TILE), lambda i, j: (i, j)), pl.BlockSpec((tm, FF_TILE), lambda i, j: (i, j)),
                   pl.BlockSpec((FF_TILE, tm), lambda i, j: (j, i))] + x_out,
        out_shape=[jax.ShapeDtypeStruct((M, 2 * D_FF), BF16), jax.ShapeDtypeStruct((M, D_FF), BF16),
                   jax.ShapeDtypeStruct((D_FF, M), BF16)] + x_shapes,
        scratch_shapes=x_sems,
        compiler_params=_params("arbitrary", "arbitrary") if exch else _params("parallel", "parallel"),
    )(a, w_il, *(exch[0] if exch else []))


def _mm_dswiglu(dy, w_out, gu, *, name, tm=FF_ROWS, split=EPILOGUE_SPLIT):
    M, K = dy.shape
    tm = _tile(M, tm)
    rc = tm // split

    def body(dy_ref, b_ref, gu_ref, o_ref):
        b = b_ref[...]
        parts = [_dot_nt(dy_ref[r0:r0 + rc, :], b) for r0 in range(0, tm, rc)]
        for r0, dact in zip(range(0, tm, rc), parts):
            g = gu_ref[r0:r0 + rc, :FF_TILE].astype(F32)
            u = gu_ref[r0:r0 + rc, FF_TILE:].astype(F32)
            sig = jax.nn.sigmoid(g)
            silu = g * sig
            o_ref[r0:r0 + rc, :FF_TILE] = (dact * u * (sig + silu * (1.0 - sig))).astype(o_ref.dtype)
            o_ref[r0:r0 + rc, FF_TILE:] = (dact * silu).astype(o_ref.dtype)

    return pl.pallas_call(
        body, name=name, grid=(M // tm, D_FF // FF_TILE),
        in_specs=[pl.BlockSpec((tm, K), lambda i, j: (i, 0)), pl.BlockSpec((FF_TILE, K), lambda i, j: (j, 0)),
                  pl.BlockSpec((tm, 2 * FF_TILE), lambda i, j: (i, j))],
        out_specs=pl.BlockSpec((tm, 2 * FF_TILE), lambda i, j: (i, j)),
        out_shape=jax.ShapeDtypeStruct((M, 2 * D_FF), BF16),
        compiler_params=_params("parallel", "parallel"),
    )(dy, w_out, gu)


def _rope_tables(n):
    rows = n // GRID_W
    row = jnp.repeat(jnp.arange(rows), GRID_W).astype(F32)
    col = jnp.tile(jnp.arange(GRID_W), rows).astype(F32)
    half = HEAD_DIM // 2
    inv = ROPE_BASE ** (-jnp.arange(0, half, 2, dtype=F32) / half)
    ar = row[:, None] * inv
    ac = col[:, None] * inv
    ang = jnp.concatenate([ar, ar, ac, ac], axis=-1)
    return jnp.cos(ang), jnp.sin(ang)


def _rot_half(x):
    w = x.shape[-1]
    lane = lax.broadcasted_iota(jnp.int32, x.shape, 1)
    up = pltpu.roll(x, w - 16, 1)
    down = pltpu.roll(x, 16, 1)
    return jnp.where((lane & 16) == 0, -up, down)


def _rope(x, cos, sin):
    return x * cos + _rot_half(x) * sin


def _rope_bwd(d, cos, sin):
    return d * cos - _rot_half(d * sin)


def _dot_nt(a, b):
    return lax.dot_general(a.astype(BF16), b.astype(BF16), (((1,), (1,)), ((), ())), preferred_element_type=F32)


def _softmax_sink(s, sink_rows):
    mx = jnp.maximum(jnp.max(s, axis=1, keepdims=True), sink_rows)
    e = jnp.exp(s - mx)
    es = jnp.exp(sink_rows - mx)
    inv = 1.0 / (jnp.sum(e, axis=1, keepdims=True) + es)
    return e * inv, es * inv


def _attn_operands(q_ref, k_ref, v_ref, kc_ref, vc_ref, cq_ref, sq_ref, ck_ref, sk_ref, i, n, n_ctx, local):
    q = q_ref[...]
    k_all, v_all, bias, s0, ck, sk = kc_ref[...], vc_ref[...], None, None, None, None
    if local:
        start, s0 = _span_start(i, n)
        ck = ck_ref[pl.ds(s0, SPAN), :]
        sk = sk_ref[pl.ds(s0, SPAN), :]
        q = _rope(q, cq_ref[...], sq_ref[...])
        k_all = jnp.concatenate([k_all, _rope(k_ref[pl.ds(s0, SPAN), :], ck, sk)], axis=0)
        v_all = jnp.concatenate([v_all, v_ref[pl.ds(s0, SPAN), :]], axis=0)
        bias = _window_bias(start, s0, n_ctx)
    q = (q * (HEAD_DIM ** -0.5)).astype(BF16)
    return q, k_all.astype(BF16), v_all.astype(BF16), bias, s0, ck, sk


def _stack_heads(x, kh):
    return jnp.concatenate([x[:, (GROUP * kh + g) * HEAD_DIM:(GROUP * kh + g + 1) * HEAD_DIM] for g in range(GROUP)], axis=0)


def _sink_rows(sink, kh):
    return jnp.concatenate([jnp.broadcast_to(sink[:, GROUP * kh + g:GROUP * kh + g + 1], (Q_BLOCK, 1)) for g in range(GROUP)], axis=0)


def _window_bias(start, s0, n_ctx):
    r = lax.broadcasted_iota(jnp.int32, (Q_BLOCK, n_ctx + SPAN), 0)
    c = lax.broadcasted_iota(jnp.int32, (Q_BLOCK, n_ctx + SPAN), 1)
    ok = (c < n_ctx) | (jnp.abs(start - s0 + r - (c - n_ctx)) <= WINDOW)
    return jnp.concatenate([jnp.where(ok, 0.0, NEG).astype(F32)] * GROUP, axis=0)


def _span_start(i, n):
    start = i * Q_BLOCK
    s0 = jnp.clip(start - WINDOW, 0, n - SPAN)
    return start, pl.multiple_of(s0, Q_BLOCK)


def _riding(body, exch, n_in, n_out, grid):
    if exch is None:
        return body, [], [], [], []
    arrs, scatter = exch
    k = len(arrs)

    def wrapped(*refs):
        ins, xin = refs[:n_in], refs[n_in:n_in + k]
        outs, xout = refs[n_in + k:n_in + k + n_out], refs[n_in + k + n_out:n_in + 2 * k + n_out]
        sems = refs[n_in + 2 * k + n_out:]
        b, i = pl.program_id(0), pl.program_id(1)

        @pl.when((b == 0) & (i == 0))
        def _():
            _exch_start(xin, xout, sems, scatter)

        body(*ins, *outs)

        @pl.when((b == grid[0] - 1) & (i == grid[1] - 1))
        def _():
            _exch_wait(xin, xout, sems, scatter)

    any_spec = pl.BlockSpec(memory_space=pl.ANY)
    return wrapped, [any_spec] * k, [any_spec] * k, _exch_out_shapes(arrs, scatter), _exch_sems(k)


def _attn_fwd(u, kvc, sink, cos, sin, *, local, name, exch=None):
    B, n, _ = u.shape
    n_ctx = kvc.shape[1]
    nb = n // Q_BLOCK
    assert (not local) or n >= SPAN

    def body(q_ref, k_ref, v_ref, kc_ref, vc_ref, sink_ref, cq_ref, sq_ref, ck_ref, sk_ref, o_ref):
        q, k_all, v_all, bias, _, _, _ = _attn_operands(q_ref, k_ref, v_ref, kc_ref, vc_ref, cq_ref, sq_ref, ck_ref, sk_ref,
                                                        pl.program_id(1), n, n_ctx, local)
        sink_v = sink_ref[...]
        sl = lambda kh: slice(kh * HEAD_DIM, (kh + 1) * HEAD_DIM)
        ss = [_dot_nt(_stack_heads(q, kh), k_all[:, sl(kh)]) for kh in range(KV_HEADS)]
        ps = [_softmax_sink(s if bias is None else s + bias, _sink_rows(sink_v, kh))[0].astype(BF16) for kh, s in enumerate(ss)]
        for kh, p in enumerate(ps):
            o = _dot(p, v_all[:, sl(kh)])
            for g in range(GROUP):
                h = GROUP * kh + g
                o_ref[:, h * HEAD_DIM:(h + 1) * HEAD_DIM] = o[g * Q_BLOCK:(g + 1) * Q_BLOCK, :].astype(o_ref.dtype)

    seq = lambda blk: pl.BlockSpec((None, n, KV_W), lambda b, i: (b, 0, blk))
    ctxs = lambda blk: pl.BlockSpec((None, n_ctx, KV_W), lambda b, i: (b, 0, blk))
    full = lambda a: pl.BlockSpec(a.shape, lambda b, i: (0,) * a.ndim)
    cos_q, sin_q = jnp.tile(cos, (1, ATTN_HEADS)), jnp.tile(sin, (1, ATTN_HEADS))
    cos_k, sin_k = jnp.tile(cos, (1, KV_HEADS)), jnp.tile(sin, (1, KV_HEADS))
    body, x_in, x_out, x_shapes, x_sems = _riding(body, exch, 10, 1, (B, nb))
    return pl.pallas_call(
        body, name=name, grid=(B, nb),
        in_specs=[pl.BlockSpec((None, Q_BLOCK, ATTN_W), lambda b, i: (b, i, 0)),
                  seq(ATTN_W // KV_W), seq(ATTN_W // KV_W + 1), ctxs(0), ctxs(1), full(sink),
                  pl.BlockSpec((Q_BLOCK, ATTN_W), lambda b, i: (i, 0)), pl.BlockSpec((Q_BLOCK, ATTN_W), lambda b, i: (i, 0)),
                  full(cos_k), full(sin_k)] + x_in,
        out_specs=[pl.BlockSpec((None, Q_BLOCK, ATTN_W), lambda b, i: (b, i, 0))] + x_out,
        out_shape=[jax.ShapeDtypeStruct((B, n, ATTN_W), BF16)] + x_shapes,
        scratch_shapes=x_sems,
        compiler_params=_params("arbitrary", "arbitrary"),
    )(u, u, u, kvc, kvc, sink, cos_q, sin_q, cos_k, sin_k, *(exch[0] if exch else []))


def _attn_bwd(u, kvc, sink, cos, sin, do_src, do_blk, *, local, name, exch=None):
    B, n, _ = u.shape
    n_ctx = kvc.shape[1]
    nb = n // Q_BLOCK

    def body(q_ref, k_ref, v_ref, kc_ref, vc_ref, sink_ref, cq_ref, sq_ref, ck_ref, sk_ref, do_ref,
             dq_ref, dk_ref, dv_ref, dkc_ref, dvc_ref, dsink_ref):
        b = pl.program_id(0)
        i = pl.program_id(1)

        @pl.when(i == 0)
        def _():
            dk_ref[...] = jnp.zeros_like(dk_ref)
            dv_ref[...] = jnp.zeros_like(dv_ref)
            dkc_ref[...] = jnp.zeros_like(dkc_ref)
            dvc_ref[...] = jnp.zeros_like(dvc_ref)

        @pl.when((i == 0) & (b == 0))
        def _():
            dsink_ref[...] = jnp.zeros_like(dsink_ref)

        q, k_all, v_all, bias, s0, ck, sk = _attn_operands(q_ref, k_ref, v_ref, kc_ref, vc_ref, cq_ref, sq_ref, ck_ref, sk_ref,
                                                           i, n, n_ctx, local)
        do = do_ref[...].astype(BF16)
        sink_v = sink_ref[...]
        sl = lambda kh: slice(kh * HEAD_DIM, (kh + 1) * HEAD_DIM)
        heads = range(KV_HEADS)
        q_st = [_stack_heads(q, kh) for kh in heads]
        do_st = [_stack_heads(do, kh) for kh in heads]
        ss = [_dot_nt(q_st[kh], k_all[:, sl(kh)]) for kh in heads]
        dps = [_dot_nt(do_st[kh], v_all[:, sl(kh)]) for kh in heads]
        p_bf, ds_bf = [], []
        dsink = jnp.zeros((1, ATTN_HEADS), F32)
        lane8 = lax.broadcasted_iota(jnp.int32, (1, ATTN_HEADS), 1)
        for kh in heads:
            p, p_sink = _softmax_sink(ss[kh] if bias is None else ss[kh] + bias, _sink_rows(sink_v, kh))
            delta = jnp.sum(p * dps[kh], axis=1, keepdims=True)
            ds = p * (dps[kh] - delta)
            dsr = -(p_sink * delta)
            for g in range(GROUP):
                dsink = dsink + jnp.where(lane8 == GROUP * kh + g, jnp.sum(dsr[g * Q_BLOCK:(g + 1) * Q_BLOCK, :]), 0.0)
            p_bf.append(p.astype(BF16))
            ds_bf.append(ds.astype(BF16))
        over_rows = (((0,), (0,)), ((), ()))
        dks, dvs = [], []
        for kh in heads:
            dq_st = _dot(ds_bf[kh], k_all[:, sl(kh)]) * (HEAD_DIM ** -0.5)
            for g in range(GROUP):
                h = GROUP * kh + g
                dq_ref[:, h * HEAD_DIM:(h + 1) * HEAD_DIM] = dq_st[g * Q_BLOCK:(g + 1) * Q_BLOCK, :]
            dvs.append(lax.dot_general(p_bf[kh], do_st[kh], over_rows, preferred_element_type=F32))
            dks.append(lax.dot_general(ds_bf[kh], q_st[kh], over_rows, preferred_element_type=F32))
        dk_cat = jnp.concatenate(dks, axis=1)
        dv_cat = jnp.concatenate(dvs, axis=1)
        dsink_ref[...] += dsink
        dkc_ref[...] += dk_cat[:n_ctx, :]
        dvc_ref[...] += dv_cat[:n_ctx, :]
        if local:
            dq_ref[...] = _rope_bwd(dq_ref[...], cq_ref[...], sq_ref[...])
            dk_ref[pl.ds(s0, SPAN), :] += _rope_bwd(dk_cat[n_ctx:, :], ck, sk)
            dv_ref[pl.ds(s0, SPAN), :] += dv_cat[n_ctx:, :]

    seq = lambda blk: pl.BlockSpec((None, n, KV_W), lambda b, i: (b, 0, blk))
    ctxs = lambda blk: pl.BlockSpec((None, n_ctx, KV_W), lambda b, i: (b, 0, blk))
    full = lambda a: pl.BlockSpec(a.shape, lambda b, i: (0,) * a.ndim)
    qblk = lambda blk: pl.BlockSpec((None, Q_BLOCK, ATTN_W), lambda b, i: (b, i, blk))
    cos_q, sin_q = jnp.tile(cos, (1, ATTN_HEADS)), jnp.tile(sin, (1, ATTN_HEADS))
    cos_k, sin_k = jnp.tile(cos, (1, KV_HEADS)), jnp.tile(sin, (1, KV_HEADS))
    acc = lambda rows: pl.BlockSpec((None, rows, KV_W), lambda b, i: (b, 0, 0))
    body, x_in, x_out, x_shapes, x_sems = _riding(body, exch, 11, 6, (B, nb))
    return pl.pallas_call(
        body, name=name, grid=(B, nb),
        in_specs=[qblk(0), seq(ATTN_W // KV_W), seq(ATTN_W // KV_W + 1), ctxs(0), ctxs(1), full(sink),
                  pl.BlockSpec((Q_BLOCK, ATTN_W), lambda b, i: (i, 0)), pl.BlockSpec((Q_BLOCK, ATTN_W), lambda b, i: (i, 0)),
                  full(cos_k), full(sin_k), qblk(do_blk)] + x_in,
        out_specs=[qblk(0), acc(n), acc(n), acc(n_ctx), acc(n_ctx), pl.BlockSpec((1, ATTN_HEADS), lambda b, i: (0, 0))] + x_out,
        out_shape=[jax.ShapeDtypeStruct((B, n, ATTN_W), F32), jax.ShapeDtypeStruct((B, n, KV_W), F32),
                   jax.ShapeDtypeStruct((B, n, KV_W), F32), jax.ShapeDtypeStruct((B, n_ctx, KV_W), F32),
                   jax.ShapeDtypeStruct((B, n_ctx, KV_W), F32), jax.ShapeDtypeStruct((1, ATTN_HEADS), F32)] + x_shapes,
        scratch_shapes=x_sems,
        compiler_params=_params("arbitrary", "arbitrary"),
    )(u, u, u, kvc, kvc, sink, cos_q, sin_q, cos_k, sin_k, do_src, *(exch[0] if exch else []))


def _conv_chunk(s, n, a_ext, g_ext, dw, dw_b, ln_g, ln_b):
    del s, n
    acc = _conv_taps(a_ext, g_ext, dw, dw_b)
    return _conv_tail(acc, ln_g, ln_b), acc


def _conv_chunk_bwd(s, n, ext, pars, acc, do):
    del s, n
    dw, dw_b, ln_g, ln_b = pars
    _, tail_vjp = jax.vjp(_conv_tail, acc, ln_g, ln_b)
    dacc, dln_g, dln_b = tail_vjp(do)
    _, taps_vjp = jax.vjp(_conv_taps, *ext, dw, dw_b)
    return (*taps_vjp(dacc), dln_g, dln_b)


def _conv_tail(acc, ln_g, ln_b):
    mu = jnp.mean(acc, axis=-1, keepdims=True)
    var = jnp.mean(jnp.square(acc - mu), axis=-1, keepdims=True)
    hn = (acc - mu) * lax.rsqrt(var + EPS) * ln_g + ln_b
    return hn * jax.nn.sigmoid(hn)


def _conv_taps(a_ext, g_ext, dw, dw_b):
    r = a_ext.shape[0] - 2 * HALO
    h = a_ext * jax.nn.sigmoid(g_ext)
    acc = jnp.broadcast_to(dw_b, (r, CONV_W))
    first = HALO - CONV_KERNEL // 2
    span = r + 8 * ((first + CONV_KERNEL - 1) // 8)
    shifted = [h[b:b + span, :] for b in range(8)]
    for k in range(CONV_KERNEL):
        o = first + k
        acc = acc + shifted[o % 8][o - o % 8:o - o % 8 + r, :] * dw[k:k + 1, :]
    return acc


def _pool_chunk(s, n, p_ext, w_bd, scale):
    r = p_ext.shape[0] - 2 * HALO
    lane = lax.broadcasted_iota(jnp.int32, (1, POOL_W), 1)
    win = jnp.left_shift(2, lane // POOL_GROUP)
    half = win // 2
    acc = jnp.zeros((r, POOL_W), F32)
    for d in range(-(POOL_WINDOWS[-1] // 2), POOL_WINDOWS[-1] - POOL_WINDOWS[-1] // 2):
        inside = (d >= -half) & (d <= win - 1 - half)
        acc = acc + jnp.where(inside, p_ext[HALO + d:HALO + d + r, :], 0.0)
    t = s + lax.broadcasted_iota(jnp.int32, (r, 1), 0)
    lo = jnp.maximum(t - half, 0)
    hi = jnp.minimum(t + win - 1 - half, n - 1)
    y = acc / (hi - lo + 1).astype(F32) - p_ext[HALO:HALO + r, :]
    out = lax.dot_general(y.astype(BF16), w_bd.astype(BF16), (((1,), (0,)), ((), ())), preferred_element_type=F32)
    return out * scale


def _seq_specs(rows, params):
    specs = [pl.BlockSpec((None, a.shape[1], w), functools.partial(lambda b, blk: (b, 0, blk), blk=blk)) for a, w, blk in rows]
    specs += [pl.BlockSpec(p.shape, functools.partial(lambda b, nd: (0,) * nd, nd=p.ndim)) for p in params]
    return specs


def _fill_padded(pad_ref, row_ref, n):
    w = pad_ref.shape[1]
    pad_ref[pl.ds(0, HALO), :] = jnp.zeros((HALO, w), F32)
    pad_ref[pl.ds(HALO + n, HALO), :] = jnp.zeros((HALO, w), F32)
    pad_ref[pl.ds(HALO, n), :] = row_ref[...]


def _seq_fwd(fn, rows, params, out_w, *, name, chunk=SEQ_CHUNK, aux_w=None):
    B, n = rows[0][0].shape[:2]
    r = min(chunk, n)
    nr, npar = len(rows), len(params)
    nout = 1 if aux_w is None else 2

    def body(*refs):
        row_refs, par_refs = refs[:nr], refs[nr:nr + npar]
        out_refs, pads = refs[nr + npar:nr + npar + nout], refs[nr + npar + nout:]
        for rr, p in zip(row_refs, pads):
            _fill_padded(p, rr, n)
        pars = [p[...] for p in par_refs]

        def chunk(ci, carry):
            s = pl.multiple_of(ci * r, r)
            ext = [p[pl.ds(s, r + 2 * HALO), :] for p in pads]
            res = fn(s, n, *ext, *pars)
            for o_ref, v in zip(out_refs, res if nout == 2 else (res,)):
                o_ref[pl.ds(s, r), :] = v.astype(o_ref.dtype)
            return carry

        lax.fori_loop(0, n // r, chunk, 0)

    widths = [(out_w, BF16)] + ([] if aux_w is None else [(aux_w, F32)])
    res = pl.pallas_call(
        body, name=name, grid=(B,),
        in_specs=_seq_specs(rows, params),
        out_specs=[pl.BlockSpec((None, n, w), lambda b: (b, 0, 0)) for w, _ in widths],
        out_shape=[jax.ShapeDtypeStruct((B, n, w), dt) for w, dt in widths],
        scratch_shapes=[pltpu.VMEM((n + 2 * HALO, w), F32) for _, w, _ in rows],
        compiler_params=_params("parallel"),
    )(*[a for a, _, _ in rows], *params)
    return res[0] if aux_w is None else res


def _seq_bwd(fn, rows, params, dout, *, name, chunk=SEQ_CHUNK, aux=None):
    B, n = rows[0][0].shape[:2]
    r = min(chunk, n)
    nr, npar = len(rows), len(params)
    naux = 0 if aux is None else 1

    def body(*refs):
        row_refs, par_refs, do_ref = refs[:nr], refs[nr:nr + npar], refs[nr + npar]
        aux_refs = refs[nr + npar + 1:nr + npar + 1 + naux]
        outs = refs[nr + npar + 1 + naux:]
        drow_refs, dpar_refs = outs[:nr], outs[nr:nr + npar]
        scratch = outs[nr + npar:]
        pads, dpads = scratch[:nr], scratch[nr:]
        for rr, p, dp in zip(row_refs, pads, dpads):
            _fill_padded(p, rr, n)
            dp[...] = jnp.zeros_like(dp)

        @pl.when(pl.program_id(0) == 0)
        def _():
            for d in dpar_refs:
                d[...] = jnp.zeros_like(d)

        pars = [p[...] for p in par_refs]

        def chunk(ci, carry):
            s = pl.multiple_of(ci * r, r)
            ext = [p[pl.ds(s, r + 2 * HALO), :] for p in pads]
            do = do_ref[pl.ds(s, r), :]
            if aux is None:
                _, vjp = jax.vjp(functools.partial(fn, s, n), *ext, *pars)
                grads = vjp(do)
            else:
                grads = fn(s, n, ext, pars, aux_refs[0][pl.ds(s, r), :], do)
            for dp, g in zip(dpads, grads[:nr]):
                dp[pl.ds(s, r + 2 * HALO), :] += g
            for d, g in zip(dpar_refs, grads[nr:]):
                d[...] += g
            return carry

        lax.fori_loop(0, n // r, chunk, 0)
        for d, dp in zip(drow_refs, dpads):
            d[...] = dp[pl.ds(HALO, n), :].astype(d.dtype)

    da, dw_, dblk = dout
    auxs = [] if aux is None else [aux]
    return pl.pallas_call(
        body, name=name, grid=(B,),
        in_specs=_seq_specs(rows, params) + [pl.BlockSpec((None, n, dw_), lambda b: (b, 0, dblk))]
        + [pl.BlockSpec((None, n, a.shape[2]), lambda b: (b, 0, 0)) for a in auxs],
        out_specs=[pl.BlockSpec((None, n, w), lambda b: (b, 0, 0)) for _, w, _ in rows]
        + [pl.BlockSpec(p.shape, functools.partial(lambda b, nd: (0,) * nd, nd=p.ndim)) for p in params],
        out_shape=[jax.ShapeDtypeStruct((B, n, w), BF16) for _, w, _ in rows]
        + [jax.ShapeDtypeStruct(p.shape, F32) for p in params],
        scratch_shapes=[pltpu.VMEM((n + 2 * HALO, w), F32) for _, w, _ in rows] * 2,
        compiler_params=_params("arbitrary"),
    )(*[a for a, _, _ in rows], *params, da, *auxs)


_CONV_A_BLK = (ATTN_W + 2 * KV_W) // CONV_W
_CONV_G_BLK = _CONV_A_BLK + 1
_POOL_BLK = _CONV_A_BLK + 2


def _mixer_fwd(tag, u, kvc, margs, local, exch=None):
    sink, dw, dw_b, ln_g, ln_b, w_bd, scale = margs
    cos, sin = _rope_tables(max(u.shape[1], GRID_W))
    attn, *got = _attn_fwd(u, kvc, sink, cos, sin, local=local, name=f"{tag}_attn_fwd", exch=exch)
    conv, conv_acc = _seq_fwd(_conv_chunk, [(u, CONV_W, _CONV_A_BLK), (u, CONV_W, _CONV_G_BLK)], [dw, dw_b, ln_g, ln_b],
                              CONV_W, name=f"{tag}_conv_fwd", aux_w=CONV_W)
    pool = _seq_fwd(_pool_chunk, [(u, POOL_W, _POOL_BLK)], [w_bd, scale], POOL_W, name=f"{tag}_pool_fwd")
    return jnp.concatenate([attn, conv, pool], axis=-1), conv_acc, got


def _mixer_bwd(tag, u, kvc, conv_acc, margs, dmix, local, exch=None):
    sink, dw, dw_b, ln_g, ln_b, w_bd, scale = margs
    cos, sin = _rope_tables(max(u.shape[1], GRID_W))
    dq, dk, dv, dkc, dvc, dsink, *got = _attn_bwd(u, kvc, sink, cos, sin, dmix, 0, local=local, name=f"{tag}_attn_bwd",
                                                  exch=exch)
    da, dg, ddw, ddw_b, dln_g, dln_b = _seq_bwd(
        _conv_chunk_bwd, [(u, CONV_W, _CONV_A_BLK), (u, CONV_W, _CONV_G_BLK)], [dw, dw_b, ln_g, ln_b],
        (dmix, CONV_W, ATTN_W // CONV_W), name=f"{tag}_conv_bwd", chunk=CONV_BWD_CHUNK, aux=conv_acc)
    dpu, dw_bd, dscale = _seq_bwd(_pool_chunk, [(u, POOL_W, _POOL_BLK)], [w_bd, scale],
                                  (dmix, POOL_W, (ATTN_W + CONV_W) // POOL_W), name=f"{tag}_pool_bwd")
    return (dq, dk, dv, da, dg, dpu), (dkc, dvc), (dsink, ddw, ddw_b, dln_g, dln_b, dw_bd, dscale), got


def _row_specs(arrs, kinds, tr):
    specs = []
    for a, kind in zip(arrs, kinds):
        if kind == "row":
            specs.append(pl.BlockSpec((None, tr, a.shape[2]), lambda b, j: (b, j, 0)))
        elif kind == "batch":
            specs.append(pl.BlockSpec((None, 1, a.shape[2]), lambda b, j: (b, 0, 0)))
        else:
            specs.append(pl.BlockSpec(a.shape, functools.partial(lambda b, j, nd: (0,) * nd, nd=a.ndim)))
    return specs


def _rowwise_fwd(fn, ins, kinds, outs, tr, *, name, transposed=None, exch=None):
    B, n = ins[0].shape[:2]
    ni, no = len(ins), len(outs)
    nj = n // tr

    def body(*refs):
        res = fn(*[r[...] for r in refs[:ni]])
        for o, v in zip(refs[ni:ni + no], res):
            o[...] = v.astype(o.dtype)
        if transposed is not None:
            refs[ni + no][...] = res[transposed].T.astype(refs[ni + no].dtype)

    out_specs = [pl.BlockSpec((None, tr, w), lambda b, j: (b, j, 0)) for w, _ in outs]
    out_shape = [jax.ShapeDtypeStruct((B, n, w), dt) for w, dt in outs]
    if transposed is not None:
        w, dt = outs[transposed]
        out_specs.append(pl.BlockSpec((w, tr), lambda b, j: (0, b * nj + j)))
        out_shape.append(jax.ShapeDtypeStruct((w, B * n), dt))
    body, x_in, x_out, x_shapes, x_sems = _riding(body, exch, ni, len(out_specs), (B, nj))
    return pl.pallas_call(
        body, name=name, grid=(B, nj),
        in_specs=_row_specs(ins, kinds, tr) + x_in, out_specs=out_specs + x_out, out_shape=out_shape + x_shapes,
        scratch_shapes=x_sems,
        compiler_params=_params("arbitrary", "arbitrary") if exch else _params("parallel", "parallel"),
    )(*ins, *(exch[0] if exch else []))


def _rowwise_bwd(fn, ins, kinds, gdtypes, cts, tr, *, name):
    B, n = ins[0].shape[:2]
    ni, nc = len(ins), len(cts)
    idx = list(range(ni))

    def body(*refs):
        in_refs, ct_refs, out_refs = refs[:ni], refs[ni:ni + nc], refs[ni + nc:]
        b, j = pl.program_id(0), pl.program_id(1)
        _, vjp = jax.vjp(fn, *[r[...].astype(F32) for r in in_refs])
        grads = vjp(tuple(c[...].astype(F32) for c in ct_refs))
        for o, i in zip(out_refs, idx):
            g = grads[i]
            if kinds[i] == "row":
                o[...] = g.astype(o.dtype)
            else:
                first = (j == 0) if kinds[i] == "batch" else ((j == 0) & (b == 0))

                @pl.when(first)
                def _(o=o, g=g):
                    o[...] = g

                @pl.when(jnp.logical_not(first))
                def _(o=o, g=g):
                    o[...] += g

    specs = _row_specs(ins, kinds, tr)
    return pl.pallas_call(
        body, name=name, grid=(B, n // tr),
        in_specs=specs + [pl.BlockSpec((None, tr, c.shape[2]), lambda b, j: (b, j, 0)) for c in cts],
        out_specs=[specs[i] for i in idx],
        out_shape=[jax.ShapeDtypeStruct(ins[i].shape, gdtypes[i]) for i in idx],
        compiler_params=_params("arbitrary", "arbitrary"),
    )(*ins, *cts)


ROW_TILE = 512


def _rms_mod(x, g, sc, sh):
    y = x * lax.rsqrt(jnp.mean(x * x, axis=-1, keepdims=True) + EPS)
    return (y * g) * (1.0 + sc) + sh


def _norm_tile(x, g, sc, sh):
    return x, _rms_mod(x, g, sc, sh)


def _res_norm_tile(xb, y, gate, g, sc, sh):
    x = xb + gate * y
    return x, _rms_mod(x, g, sc, sh)


_NORM_KINDS = ("row", "glob", "batch", "batch")
_RES_NORM_KINDS = ("row", "row", "batch", "glob", "batch", "batch")


def _norm_fwd(tag, st, g, sc, sh, exch=None):
    xb, y, gate = st
    tr = min(ROW_TILE, xb.shape[1])
    d = xb.shape[2]
    if y is None:
        return [xb, *_rowwise_fwd(lambda *a: (_rms_mod(*a),), [xb, g, sc, sh], _NORM_KINDS, [(d, BF16)], tr,
                                  name=f"{tag}_fwd", transposed=0, exch=exch)]
    return _rowwise_fwd(_res_norm_tile, [xb, y, gate, g, sc, sh], _RES_NORM_KINDS, [(d, F32), (d, BF16)], tr,
                        name=f"{tag}_fwd", transposed=1, exch=exch)


def _norm_bwd(tag, st, g, sc, sh, dx, dh):
    xb, y, gate = st
    tr = min(ROW_TILE, xb.shape[1])
    if y is None:
        dxb, dg, dsc, dsh = _rowwise_bwd(_norm_tile, [xb, g, sc, sh], _NORM_KINDS, [F32] * 4, [dx, dh], tr, name=f"{tag}_bwd")
        return dxb, None, None, dg, dsc, dsh
    return tuple(_rowwise_bwd(_res_norm_tile, [xb, y, gate, g, sc, sh], _RES_NORM_KINDS, [F32, BF16, F32, F32, F32, F32],
                              [dx, dh], tr, name=f"{tag}_bwd"))


def _loss_head(st, final_g, target, *, name):
    xb, y, gate = st
    B, n, d = xb.shape
    tr = min(ROW_TILE, n)

    def tile_loss(xv, yv, gt, g, t):
        x = xv + gt * yv
        out = x * lax.rsqrt(jnp.mean(x * x, axis=-1, keepdims=True) + EPS) * g
        return 0.5 * jnp.sum(jnp.mean(jnp.square(out - t), axis=-1))

    def body(x_ref, y_ref, gate_ref, g_ref, t_ref, loss_ref, dx_ref, dy_ref, dgate_ref, dg_ref):
        b, j = pl.program_id(0), pl.program_id(1)
        val, (dx, dy, dgate, dg) = jax.value_and_grad(tile_loss, argnums=(0, 1, 2, 3))(
            x_ref[...], y_ref[...], gate_ref[...], g_ref[...], t_ref[...])
        dx_ref[...] = dx
        dy_ref[...] = dy.astype(dy_ref.dtype)

        @pl.when(j == 0)
        def _():
            loss_ref[...] = jnp.zeros_like(loss_ref)
            dgate_ref[...] = jnp.zeros_like(dgate_ref)

        @pl.when((j == 0) & (b == 0))
        def _():
            dg_ref[...] = jnp.zeros_like(dg_ref)

        loss_ref[...] += jnp.full(loss_ref.shape, val, F32)
        dgate_ref[...] += dgate
        dg_ref[...] += dg

    row = pl.BlockSpec((None, tr, d), lambda b, j: (b, j, 0))
    per_sample = pl.BlockSpec((None, 1, d), lambda b, j: (b, 0, 0))
    whole = pl.BlockSpec((1, d), lambda b, j: (0, 0))
    return pl.pallas_call(
        body, name=name, grid=(B, n // tr),
        in_specs=[row, row, per_sample, whole, row],
        out_specs=[pl.BlockSpec((None, 1, 128), lambda b, j: (b, 0, 0)), row, row, per_sample, whole],
        out_shape=[jax.ShapeDtypeStruct((B, 1, 128), F32), jax.ShapeDtypeStruct((B, n, d), F32),
                   jax.ShapeDtypeStruct((B, n, d), BF16), jax.ShapeDtypeStruct((B, 1, d), F32), jax.ShapeDtypeStruct((1, d), F32)],
        compiler_params=_params("arbitrary", "arbitrary"),
    )(xb, y, gate, final_g, target)


def _exchange(arrs, *, scatter, name):
    k = len(arrs)

    def body(*refs):
        ins, outs, sems = refs[:k], refs[k:2 * k], refs[2 * k:]
        _exch_start(ins, outs, sems, scatter)
        _exch_wait(ins, outs, sems, scatter)

    any_spec = pl.BlockSpec(memory_space=pl.ANY)
    return pl.pallas_call(
        body, name=name,
        in_specs=[any_spec] * k, out_specs=[any_spec] * k,
        out_shape=_exch_out_shapes(arrs, scatter), scratch_shapes=_exch_sems(k),
        compiler_params=pltpu.CompilerParams(has_side_effects=True),
    )(*arrs)


def _exch_flags(scatter, k):
    return [scatter] * k if isinstance(scatter, bool) else list(scatter)


def _exch_out_shapes(arrs, scatter):
    return [jax.ShapeDtypeStruct(a.shape if f else (N_DEV,) + a.shape, a.dtype)
            for a, f in zip(arrs, _exch_flags(scatter, len(arrs)))]


def _exch_sems(k):
    return [pltpu.SemaphoreType.DMA((k * (N_DEV - 1),)), pltpu.SemaphoreType.DMA((k * (N_DEV - 1),)),
            pltpu.SemaphoreType.DMA((k,))]


def _exch_copies(ins, outs, sems, scatter):
    send_sems, recv_sems, local_sems = sems
    x, y, c = lax.axis_index("x"), lax.axis_index("y"), lax.axis_index("c")
    me = 4 * x + 2 * y + c
    owns, sends, recvs = [], [], []
    flags = _exch_flags(scatter, len(ins))
    for a in range(len(ins)):
        scatter = flags[a]
        owns.append(pltpu.make_async_copy(ins[a].at[me] if scatter else ins[a], outs[a].at[me], local_sems.at[a]))
        for r in range(1, N_DEV):
            fx, fy, fc = (r >> 2) & 1, (r >> 1) & 1, r & 1
            px, py, pc = (x + fx) % 2, (y + fy) % 2, (c + fc) % 2
            peer = 4 * px + 2 * py + pc
            s = a * (N_DEV - 1) + r - 1
            mk = functools.partial(pltpu.make_async_remote_copy, src_ref=ins[a].at[peer] if scatter else ins[a],
                                   send_sem=send_sems.at[s], recv_sem=recv_sems.at[s],
                                   device_id=(px, py, pc), device_id_type=pl.DeviceIdType.MESH)
            sends.append(mk(dst_ref=outs[a].at[me]))
            recvs.append(mk(dst_ref=outs[a].at[peer]))
    return owns, sends, recvs


def _exch_start(ins, outs, sems, scatter):
    owns, sends, _ = _exch_copies(ins, outs, sems, scatter)
    for cp in owns + sends:
        cp.start()


def _exch_wait(ins, outs, sems, scatter):
    owns, sends, recvs = _exch_copies(ins, outs, sems, scatter)
    for rc in recvs:
        rc.wait_recv()
    for cp in sends:
        cp.wait_send()
    for own in owns:
        own.wait()


MOD_ROWS = 48


def _mod_tile(cc, w, b):
    s = cc * jax.nn.sigmoid(cc)
    return lax.dot_general(s.astype(BF16), w.astype(BF16), (((1,), (0,)), ((), ())), preferred_element_type=F32) + b


def _mod_fwd(cc, w_mod, b_shard, *, name):
    L, d, wcols = w_mod.shape

    def body(cc_ref, w_ref, b_ref, o_ref):
        o_ref[...] = _mod_tile(cc_ref[...], w_ref[...], b_ref[...])

    return pl.pallas_call(
        body, name=name, grid=(L,),
        in_specs=[pl.BlockSpec((MOD_ROWS, d), lambda l: (0, 0)), pl.BlockSpec((None, d, wcols), lambda l: (l, 0, 0)),
                  pl.BlockSpec((None, 1, wcols), lambda l: (l, 0, 0))],
        out_specs=pl.BlockSpec((None, MOD_ROWS, wcols), lambda l: (l, 0, 0)),
        out_shape=jax.ShapeDtypeStruct((L, MOD_ROWS, wcols), F32),
        compiler_params=_params("parallel"),
    )(cc, w_mod, b_shard)


def _mod_bwd(cc, w_mod, b_shard, dm, *, name):
    L, d, wcols = w_mod.shape

    def body(cc_ref, w_ref, b_ref, dm_ref, dcc_ref, dw_ref):
        _, vjp = jax.vjp(_mod_tile, cc_ref[...], w_ref[...], b_ref[...])
        dcc, dw, _ = vjp(dm_ref[...])
        dw_ref[...] = dw

        @pl.when(pl.program_id(0) == 0)
        def _():
            dcc_ref[...] = dcc

        @pl.when(pl.program_id(0) > 0)
        def _():
            dcc_ref[...] += dcc

    return pl.pallas_call(
        body, name=name, grid=(L,),
        in_specs=[pl.BlockSpec((MOD_ROWS, d), lambda l: (0, 0)), pl.BlockSpec((None, d, wcols), lambda l: (l, 0, 0)),
                  pl.BlockSpec((None, 1, wcols), lambda l: (l, 0, 0)), pl.BlockSpec((None, MOD_ROWS, wcols), lambda l: (l, 0, 0))],
        out_specs=[pl.BlockSpec((MOD_ROWS, d), lambda l: (0, 0)), pl.BlockSpec((None, d, wcols), lambda l: (l, 0, 0))],
        out_shape=[jax.ShapeDtypeStruct((MOD_ROWS, d), F32), jax.ShapeDtypeStruct((L, d, wcols), F32)],
        compiler_params=_params("arbitrary"),
    )(cc, w_mod, b_shard, dm)


def _sum_leading(a, *, name):
    K, R, C = a.shape
    tr = _tile8(R, 256)

    def body(a_ref, o_ref):
        acc = a_ref[0].astype(F32)
        for i in range(1, K):
            acc = acc + a_ref[i].astype(F32)
        o_ref[...] = acc

    return pl.pallas_call(
        body, name=name, grid=(R // tr,),
        in_specs=[pl.BlockSpec((K, tr, C), lambda i: (0, i, 0))],
        out_specs=pl.BlockSpec((tr, C), lambda i: (i, 0)),
        out_shape=jax.ShapeDtypeStruct((R, C), F32),
        compiler_params=_params("parallel"),
    )(a)


def _tile8(dim, target):
    if dim <= target:
        return dim
    t = (target // 8) * 8
    while t >= 8:
        if dim % t == 0:
            return t
        t -= 8
    raise ValueError(f"no row tile for {dim}")


def _adamw_math(g, w, m, v):
    m = ADAM_B1 * m + (1.0 - ADAM_B1) * g
    v = ADAM_B2 * v + (1.0 - ADAM_B2) * jnp.square(g)
    m_hat = m / (1.0 - ADAM_B1 ** ADAM_STEP)
    v_hat = v / (1.0 - ADAM_B2 ** ADAM_STEP)
    delta = -ADAM_LR * (m_hat / (jnp.sqrt(v_hat) + ADAM_EPS) + ADAM_WD * w)
    return delta, m, v


def _adamw(g, w, m, v, *, name, exch=None):
    L, R, C = w.shape
    parts = isinstance(g, (list, tuple))
    gs = list(g) if parts else [g]
    ng = len(gs)
    tr = _tile8(R, 256)

    def body(*refs):
        g_refs = refs[:ng]
        w_ref, m_ref, v_ref, go_ref, d_ref, mo_ref, vo_ref = refs[ng:]
        if parts:
            layer = pl.program_id(0)
            gv = None
            for li, g_ref in enumerate(g_refs):
                acc = g_ref[0].astype(F32)
                for i in range(1, N_DEV):
                    acc = acc + g_ref[i].astype(F32)
                gv = acc if gv is None else jnp.where(layer == li, acc, gv)
        else:
            gv = g_refs[0][...]
        go_ref[...] = gv
        d_ref[...], mo_ref[...], vo_ref[...] = _adamw_math(gv, w_ref[...], m_ref[...], v_ref[...])

    tile = pl.BlockSpec((None, tr, C), lambda l, i: (l, i, 0))
    g_specs = [pl.BlockSpec((N_DEV, tr, C), lambda l, i: (0, i, 0))] * ng if parts else [tile]
    grid = (L, R // tr)
    body, x_in, x_out, x_shapes, x_sems = _riding(body, exch, ng + 3, 4, grid)
    return pl.pallas_call(
        body, name=name, grid=grid,
        in_specs=g_specs + [tile, tile, tile] + x_in, out_specs=[tile] * 4 + x_out,
        out_shape=[jax.ShapeDtypeStruct((L, R, C), F32)] * 4 + x_shapes,
        scratch_shapes=x_sems,
        compiler_params=_params("arbitrary", "arbitrary") if exch else _params("parallel", "parallel"),
    )(*gs, w, m, v, *(exch[0] if exch else []))


def _adamw_small(gs, ws, ms, vs, *, name):
    k = len(ws)

    def body(*refs):
        g_refs, w_refs, m_refs, v_refs = refs[:k], refs[k:2 * k], refs[2 * k:3 * k], refs[3 * k:4 * k]
        d_refs, mo_refs, vo_refs = refs[4 * k:5 * k], refs[5 * k:6 * k], refs[6 * k:]
        for i in range(k):
            d_refs[i][...], mo_refs[i][...], vo_refs[i][...] = _adamw_math(g_refs[i][...], w_refs[i][...], m_refs[i][...],
                                                                         v_refs[i][...])

    shapes = [jax.ShapeDtypeStruct(a.shape, F32) for a in ws]
    out = pl.pallas_call(body, name=name, out_shape=shapes * 3, compiler_params=pltpu.CompilerParams(vmem_limit_bytes=VMEM_LIMIT))(
        *gs, *ws, *ms, *vs)
    return out[:k], out[k:2 * k], out[2 * k:]


def _block_diag(w):
    g, c, d = w.shape
    return (w[:, :, None, :] * jnp.eye(g, dtype=w.dtype)[:, None, :, None]).reshape(g * c, g * d)


def _diag_blocks(w_bd):
    g = POOL_W // POOL_GROUP
    return jnp.stack([w_bd[i * POOL_GROUP:(i + 1) * POOL_GROUP, i * POOL_GROUP:(i + 1) * POOL_GROUP] for i in range(g)])


def _flat(a):
    return a.reshape(-1, a.shape[-1])


def _mix_half_fwd(tag, st, mods, wl, kvc, *, local, kv_only, exch=None, normed=None):
    sh1, sc1, g1 = mods[:3]
    B, n, d = st[0].shape
    x, h, h_t = normed if normed is not None else _norm_fwd(f"{tag}_norm1", st, wl["n1"], sc1, sh1)
    if kv_only:
        kv = _mm(_flat(h), wl["w_in"][:, ATTN_W:ATTN_W + 2 * KV_W], name=f"{tag}_kv").reshape(B, n, 2 * KV_W)
        return None, dict(st=st, h_t=h_t, kvc=kv), []
    u = _mm(_flat(h), wl["w_in"], name=f"{tag}_in", tn=IN_W).reshape(B, n, IN_W)
    if not local:
        kvc = u[:, :, ATTN_W:ATTN_W + 2 * KV_W]
    mix, conv_acc, got = _mixer_fwd(f"{tag}_mix", u, kvc, wl["margs"], local, exch)
    y = _mm(_flat(mix), wl["w_out"], name=f"{tag}_out", tn=D_MODEL).reshape(B, n, d)
    return (x, y, g1), dict(st=st, h_t=h_t, u=u, kvc=kvc, mix=mix, conv_acc=conv_acc), got


def _ffn_half_fwd(tag, st2, mods, wl, exch=None):
    sh2, sc2, g2 = mods[3:]
    B, n, d = st2[0].shape
    x1, h2, h2_t = _norm_fwd(f"{tag}_norm2", st2, wl["n2"], sc2, sh2)
    gu, act, act_t, *got = _mm_swiglu(_flat(h2), wl["w_ffn_in"], name=f"{tag}_ffn_in", exch=exch)
    y2 = _mm(act, wl["w_ffn_out"], name=f"{tag}_ffn_out", tn=D_MODEL).reshape(B, n, d)
    return (x1, y2, g2), dict(st2=st2, h2_t=h2_t, gu=gu, act_t=act_t), got


def _ffn_half_bwd(tag, sv, mods, wl, dx1, dy2, plus=None):
    sh2, sc2, _ = mods[3:]
    B, n, d = sv["st2"][0].shape
    gw = {}
    dy2f = _flat(dy2)
    plus = plus or {}
    gw["w_ffn_out"] = _mm(sv["act_t"], dy2f, out_dtype=BF16, name=f"{tag}_ffn_out_dw", plus=plus.get("w_ffn_out"))
    dgu = _mm_dswiglu(dy2f, wl["w_ffn_out"], sv["gu"], name=f"{tag}_ffn_out_dx")
    dh2 = _mm(dgu, wl["w_ffn_in"], trans_b=True, name=f"{tag}_ffn_in_dx").reshape(B, n, d)
    gw["w_ffn_in"] = _mm(sv["h2_t"], dgu, out_dtype=BF16, tn=FF_TILE, out_block=_natural_block, name=f"{tag}_ffn_in_dw",
                         plus=plus.get("w_ffn_in"))
    dx, dy, dg1, gw["n2"], dsc2, dsh2 = _norm_bwd(f"{tag}_norm2", sv["st2"], wl["n2"], sc2, sh2, dx1, dh2)
    return (dx, dy, dg1), gw, dict(sh2=dsh2, sc2=dsc2)


def _mix_half_bwd(tag, sv, mods, wl, dx, dy, dkv_in, *, local, kv_only, exch=None, plus=None):
    sh1, sc1, _ = mods[:3]
    B, n, d = sv["st"][0].shape
    gw = {}
    if kv_only:
        dkv = _flat(dkv_in).astype(BF16)
        dh = _mm(dkv, wl["w_in"][:, ATTN_W:ATTN_W + 2 * KV_W], trans_b=True, name=f"{tag}_kv_dx").reshape(B, n, d)
        gw["w_in_kv"] = _mm(sv["h_t"], dkv, out_dtype=BF16, name=f"{tag}_kv_dw")
        dxb, dy_prev, dgate_prev, gw["n1"], dsc1, dsh1 = _norm_bwd(f"{tag}_norm1", sv["st"], wl["n1"], sc1, sh1,
                                                                    jnp.zeros((B, n, d), F32), dh)
        return (dxb, dy_prev, dgate_prev), gw, dict(sh1=dsh1, sc1=dsc1), None, []

    dyf = _flat(dy)
    dmix = _mm(dyf, wl["w_out"], trans_b=True, name=f"{tag}_out_dx", tn=D_MODEL).reshape(B, n, d)
    plus = plus or {}
    gw["w_out"] = _mm(_flat(sv["mix"]).T, dyf, out_dtype=BF16, tn=DW_TN, name=f"{tag}_out_dw", plus=plus.get("w_out"))
    (dq, dk, dv, da, dg, dpu), (dkc, dvc), gw["margs"], got = _mixer_bwd(f"{tag}_mix", sv["u"], sv["kvc"], sv["conv_acc"],
                                                                     wl["margs"], dmix, local, exch)
    if local:
        dkv_out = jnp.concatenate([dkc, dvc], axis=-1)
    else:
        dk = dkc + dkv_in[:, :, :KV_W]
        dv = dvc + dkv_in[:, :, KV_W:]
        dkv_out = None
    du = _flat(jnp.concatenate([dq, dk, dv, da, dg, dpu], axis=-1).astype(BF16))
    dh = _mm(du, wl["w_in"], trans_b=True, name=f"{tag}_in_dx", tn=D_MODEL).reshape(B, n, d)
    gw["w_in"] = _mm(sv["h_t"], du, out_dtype=BF16, tn=DW_TN, name=f"{tag}_in_dw", plus=plus.get("w_in"))
    dxb, dy_prev, dgate_prev, gw["n1"], dsc1, dsh1 = _norm_bwd(f"{tag}_norm1", sv["st"], wl["n1"], sc1, sh1, dx, dh)
    return (dxb, dy_prev, dgate_prev), gw, dict(sh1=dsh1, sc1=dsc1), dkv_out, got


BIG_W = ("w_in", "w_out", "w_ffn_in", "w_ffn_out")


def _local_step(x, ctx, m_loc, m_ctx, p, final_g, target, big):
    B = x.shape[0]
    depth = m_loc.shape[0]
    lat_mods = [[t[:, None, :] for t in jnp.split(m_loc[l], 6, axis=-1)] for l in range(depth)]
    ctx_mods = [[jnp.broadcast_to(t[None, None, :], (B, 1, D_MODEL)) for t in jnp.split(m_ctx[l], 6)] for l in range(depth)]

    st, cst = (x, None, None), (ctx, None, None)
    w_mix, w_ffn, sv_mix, sv_ffn, csv_mix, csv_ffn = [], [], [], [], [], []
    *normed, = _norm_fwd("l0_norm1", st, p["norm1_g"][0][None, :], lat_mods[0][1], lat_mods[0][0], exch=big.ride_first())
    normed, got = normed[:3], normed[3:]
    for l in range(depth):
        last = l == depth - 1
        wb = big.mix_weights(l, got)
        wm = dict(n1=p["norm1_g"][l][None, :], w_in=wb["w_in"], w_out=wb["w_out"],
                  margs=(p["attn_sink"][l][None, :], p["conv_dw"][l], p["conv_dw_b"][l][None, :], p["conv_ln_g"][l][None, :],
                         p["conv_ln_b"][l][None, :], _block_diag(p["pool_w"][l]), p["pool_scale"][l][None, :]))
        cst, csv, _ = _mix_half_fwd(f"l{l}c", cst, ctx_mods[l], wm, None, local=False, kv_only=last)
        st, sv, got = _mix_half_fwd(f"l{l}", st, lat_mods[l], wm, csv["kvc"], local=True, kv_only=False,
                                    exch=big.ride_attn_fwd(l), normed=normed if l == 0 else None)
        w_mix.append(wm)
        sv_mix.append(sv)
        csv_mix.append(csv)
        wb = big.ffn_weights(l, got)
        w_ffn_in = _interleave_ffn(wb["w_ffn_in"], name=f"l{l}_ffn_in_interleave")
        wf = dict(n2=p["norm2_g"][l][None, :], w_ffn_in=w_ffn_in, w_ffn_out=wb["w_ffn_out"])
        csv = None
        if not last:
            cst, csv, _ = _ffn_half_fwd(f"l{l}c", cst, ctx_mods[l], wf)
        st, sv, got = _ffn_half_fwd(f"l{l}", st, lat_mods[l], wf, exch=big.ride_ffn_fwd(l))
        w_ffn.append(wf)
        sv_ffn.append(sv)
        csv_ffn.append(csv)
    loss_rows, dx, dy, dgate, dfinal = _loss_head(st, final_g[None, :], target, name="loss_head")

    dm_loc, dm_ctx = [None] * depth, [None] * depth
    small = [None] * depth
    cdx = cdy = cdgate = None
    up_mix = None
    for l in reversed(range(depth)):
        last = l == depth - 1
        dm, cdm = dict(g2=dgate), {}
        (dx, dy, dm["g1"]), gf, d = _ffn_half_bwd(f"l{l}", sv_ffn[l], lat_mods[l], w_ffn[l], dx, dy)
        dm.update(d)
        if not last:
            cdm["g2"] = cdgate
            (cdx, cdy, cdm["g1"]), cgf, d = _ffn_half_bwd(f"l{l}c", csv_ffn[l], ctx_mods[l], w_ffn[l], cdx, cdy, plus=gf)
            cdm.update(d)
            gf = dict(cgf, n2=gf["n2"] + cgf["n2"])
        ffn_grads = {k: gf[k] for k in _ShardedWeights.FFN}
        (dx, dy, dgate), gm, d, dkv, got = _mix_half_bwd(f"l{l}", sv_mix[l], lat_mods[l], w_mix[l], dx, dy, None, local=True,
                                                        kv_only=False, exch=big.ride_attn_bwd(l, ffn_grads, up_mix))
        big.took(l, ffn_grads, up_mix, got)
        dm.update(d)
        (cdx, cdy, cdgate), cgm, d, _, _ = _mix_half_bwd(f"l{l}c", csv_mix[l], ctx_mods[l], w_mix[l], cdx, cdy, dkv,
                                                        local=False, kv_only=last, plus=gm)
        cdm.update(d)
        order = ("sh1", "sc1", "g1", "sh2", "sc2", "g2")
        dm_loc[l] = jnp.concatenate([dm[k][:, 0, :] for k in order], axis=-1)
        dm_ctx[l] = jnp.concatenate([jnp.sum(cdm[k][:, 0, :], axis=0) if k in cdm else jnp.zeros((D_MODEL,), F32)
                                     for k in order])
        if last:
            up_mix = dict(w_in=gm["w_in"].at[:, ATTN_W:ATTN_W + 2 * KV_W].add(cgm["w_in_kv"]), w_out=gm["w_out"])
            margs = gm["margs"]
        else:
            up_mix = {k: cgm[k] for k in _ShardedWeights.MIX}
            margs = tuple(a + b for a, b in zip(gm["margs"], cgm["margs"]))
        small[l] = dict(n1=gm["n1"] + cgm["n1"], n2=gf["n2"], margs=margs)
    big.leftover(up_mix)

    stack = lambda f: jnp.stack([f(small[l]) for l in range(depth)])
    dp = dict(
        norm1_g=stack(lambda g: g["n1"][0]), norm2_g=stack(lambda g: g["n2"][0]),
        attn_sink=stack(lambda g: g["margs"][0][0]), conv_dw=stack(lambda g: g["margs"][1]),
        conv_dw_b=stack(lambda g: g["margs"][2][0]), conv_ln_g=stack(lambda g: g["margs"][3][0]),
        conv_ln_b=stack(lambda g: g["margs"][4][0]), pool_w=stack(lambda g: _diag_blocks(g["margs"][5])),
        pool_scale=stack(lambda g: g["margs"][6][0]))
    return jnp.sum(loss_rows[:, 0, 0]), dx, jnp.stack(dm_loc), jnp.stack(dm_ctx), dp, dfinal[0]


PACK_COLS = 1024


def _pack(arrs):
    flat = jnp.concatenate([a.reshape(-1).astype(F32) for a in arrs])
    rows = -(-flat.shape[0] // (8 * PACK_COLS)) * 8
    return jnp.pad(flat, (0, rows * PACK_COLS - flat.shape[0])).reshape(rows, PACK_COLS)


def _unpack(slab, like):
    flat = slab.reshape(-1)
    out, off = [], 0
    for a in like:
        out.append(flat[off:off + a.size].reshape(a.shape))
        off += a.size
    return out


def _shard_cols(gathered):
    _, L, R, C = gathered.shape
    return jnp.transpose(gathered, (1, 2, 0, 3)).reshape(L, R, N_DEV * C)


class _ShardedWeights:
    MIX = ("w_in", "w_out")
    FFN = ("w_ffn_in", "w_ffn_out")
    BY_COLS = ("w_in", "w_ffn_in")

    def __init__(self, shards):
        self.shards = shards
        self.depth = shards[BIG_W[0]].shape[0]
        self.parts = [dict() for _ in range(self.depth)]
        self.left = None

    def _join(self, names, blocks):
        out = {}
        for name, g in zip(names, blocks):
            _, R, C = g.shape
            out[name] = jnp.transpose(g, (1, 0, 2)).reshape(R, N_DEV * C) if name in self.BY_COLS else g.reshape(N_DEV * R, C)
        return out

    def cut(self, names, grads):
        out = []
        for name in names:
            g = grads[name]
            if name in self.BY_COLS:
                R, C8 = g.shape
                out.append(jnp.transpose(g.reshape(R, N_DEV, C8 // N_DEV), (1, 0, 2)))
            else:
                R8, C = g.shape
                out.append(g.reshape(N_DEV, R8 // N_DEV, C))
        return out

    def ride_first(self):
        return [self.shards[name][0] for name in self.MIX], False

    def mix_weights(self, l, got):
        return self._join(self.MIX, got)

    def ffn_weights(self, l, got):
        return self._join(self.FFN, got)

    def ride_attn_fwd(self, l):
        return [self.shards[name][l] for name in self.FFN], False

    def ride_ffn_fwd(self, l):
        if l + 1 >= self.depth:
            return None
        return [self.shards[name][l + 1] for name in self.MIX], False

    def ride_attn_bwd(self, l, ffn_grads, up_mix):
        return self.cut(self.FFN, ffn_grads) + (self.cut(self.MIX, up_mix) if up_mix is not None else []), True

    def took(self, l, ffn_grads, up_mix, got):
        self.parts[l].update(zip(self.FFN, got[:2]))
        if up_mix is not None:
            self.parts[l + 1].update(zip(self.MIX, got[2:]))

    def leftover(self, mix_grads):
        self.left = mix_grads


def _as_rows(a, leading=0):
    return a.reshape(*a.shape[:leading], -1, PACK_COLS)


SMALL = ("c_ctx", "b_mod", "norm1_g", "norm2_g", "conv_dw_b", "conv_ln_g", "conv_ln_b", "attn_sink", "pool_w",
         "pool_scale", "final_g", "conv_dw")
BIG = ("w_mod", "w_in", "w_out", "w_ffn_in", "w_ffn_out")
ORDER = ("c_ctx", "w_mod", "b_mod", "norm1_g", "norm2_g", "w_in", "conv_dw", "conv_dw_b", "conv_ln_g", "conv_ln_b",
         "attn_sink", "pool_w", "pool_scale", "w_out", "w_ffn_in", "w_ffn_out", "final_g")


def kernel(x, c, ctx, c_ctx, w_mod, b_mod, norm1_g, norm2_g, w_in, conv_dw, conv_dw_b, conv_ln_g, conv_ln_b, attn_sink, pool_w, pool_scale, w_out, w_ffn_in, w_ffn_out, final_g, loss_target, m_c_ctx, m_w_mod, m_b_mod, m_norm1_g, m_norm2_g, m_w_in, m_conv_dw, m_conv_dw_b, m_conv_ln_g, m_conv_ln_b, m_attn_sink, m_pool_w, m_pool_scale, m_w_out, m_w_ffn_in, m_w_ffn_out, m_final_g, v_c_ctx, v_w_mod, v_b_mod, v_norm1_g, v_norm2_g, v_w_in, v_conv_dw, v_conv_dw_b, v_conv_ln_g, v_conv_ln_b, v_attn_sink, v_pool_w, v_pool_scale, v_w_out, v_w_ffn_in, v_w_ffn_out, v_final_g):
    w = dict(c_ctx=c_ctx, w_mod=w_mod, b_mod=b_mod, norm1_g=norm1_g, norm2_g=norm2_g, w_in=w_in, conv_dw=conv_dw,
             conv_dw_b=conv_dw_b, conv_ln_g=conv_ln_g, conv_ln_b=conv_ln_b, attn_sink=attn_sink, pool_w=pool_w,
             pool_scale=pool_scale, w_out=w_out, w_ffn_in=w_ffn_in, w_ffn_out=w_ffn_out, final_g=final_g)
    mom = dict(c_ctx=m_c_ctx, w_mod=m_w_mod, b_mod=m_b_mod, norm1_g=m_norm1_g, norm2_g=m_norm2_g, w_in=m_w_in,
               conv_dw=m_conv_dw, conv_dw_b=m_conv_dw_b, conv_ln_g=m_conv_ln_g, conv_ln_b=m_conv_ln_b,
               attn_sink=m_attn_sink, pool_w=m_pool_w, pool_scale=m_pool_scale, w_out=m_w_out, w_ffn_in=m_w_ffn_in,
               w_ffn_out=m_w_ffn_out, final_g=m_final_g)
    var = dict(c_ctx=v_c_ctx, w_mod=v_w_mod, b_mod=v_b_mod, norm1_g=v_norm1_g, norm2_g=v_norm2_g, w_in=v_w_in,
               conv_dw=v_conv_dw, conv_dw_b=v_conv_dw_b, conv_ln_g=v_conv_ln_g, conv_ln_b=v_conv_ln_b,
               attn_sink=v_attn_sink, pool_w=v_pool_w, pool_scale=v_pool_scale, w_out=v_w_out, w_ffn_in=v_w_ffn_in,
               w_ffn_out=v_w_ffn_out, final_g=v_final_g)
    B = x.shape[0]
    depth = w_mod.shape[0]
    mod_cols = w_mod.shape[2]
    dw_cols = conv_dw.shape[2]
    me = 4 * lax.axis_index("x") + 2 * lax.axis_index("y") + lax.axis_index("c")

    shards = {name: w[name].astype(BF16) for name in BIG_W}
    c_all, dw_all = _exchange([c, conv_dw], scatter=False, name="gather_first")
    big = _ShardedWeights(shards)
    p = dict(norm1_g=norm1_g, norm2_g=norm2_g, conv_dw=_shard_cols(dw_all), conv_dw_b=conv_dw_b, conv_ln_g=conv_ln_g,
             conv_ln_b=conv_ln_b, attn_sink=attn_sink, pool_w=pool_w, pool_scale=pool_scale)

    cc = jnp.concatenate([c_all.reshape(N_DEV * B, D_MODEL), jnp.broadcast_to(c_ctx[None, :], (N_DEV, D_MODEL)),
                          jnp.zeros((MOD_ROWS - N_DEV * B - N_DEV, D_MODEL), F32)], axis=0)
    b_shard = lax.dynamic_slice_in_dim(b_mod, me * mod_cols, mod_cols, axis=1)[:, None, :]
    m_part = _mod_fwd(cc, w_mod, b_shard, name="mod_fwd")
    m_all, = _exchange([m_part], scatter=False, name="gather_mod")
    m_full = _shard_cols(m_all)
    m_loc = lax.dynamic_slice_in_dim(m_full, me * B, B, axis=1)
    m_ctx = m_full[:, N_DEV * B, :]

    loss_part, dx, dm_loc, dm_ctx, dp, dfinal = _local_step(x, ctx, m_loc, m_ctx, p, final_g, loss_target, big)
    loss = lax.psum(loss_part, AXES)

    dm_rows = jnp.concatenate([dm_loc, dm_ctx[:, None, :], jnp.zeros((depth, 8 - B - 1, 6 * D_MODEL), F32)], axis=1)
    small_like = [norm1_g, norm2_g, conv_dw_b, conv_ln_g, conv_ln_b, attn_sink, pool_w, pool_scale, final_g, dp["conv_dw"]]
    small_part = _pack([dp["norm1_g"], dp["norm2_g"], dp["conv_dw_b"], dp["conv_ln_g"], dp["conv_ln_b"], dp["attn_sink"],
                        dp["pool_w"], dp["pool_scale"], dfinal, dp["conv_dw"]])
    g, delta, new_m, new_v = {}, {}, {}, {}
    first_ffn = big.FFN[0]
    g[first_ffn], delta[first_ffn], new_m[first_ffn], new_v[first_ffn], dm_all, small_all, *last = _adamw(
        [big.parts[l][first_ffn] for l in range(depth)], w[first_ffn], mom[first_ffn], var[first_ffn],
        name=f"adamw_{first_ffn}", exch=([dm_rows, small_part] + big.cut(big.MIX, big.left), [False, False, True, True]))
    big.parts[0].update(zip(big.MIX, last))
    dm_full = jnp.concatenate([
        jnp.transpose(dm_all[:, :, :B, :], (1, 0, 2, 3)).reshape(depth, N_DEV * B, 6 * D_MODEL),
        jnp.transpose(dm_all[:, :, B, :], (1, 0, 2)),
        jnp.zeros((depth, MOD_ROWS - N_DEV * B - N_DEV, 6 * D_MODEL), F32)], axis=1)
    g_b_mod = jnp.stack([_sum_leading(dm_full[l][:, None, :], name=f"b_mod_grad{l}")[0] for l in range(depth)])
    dm_mine = lax.dynamic_slice_in_dim(dm_full, me * mod_cols, mod_cols, axis=2)
    dcc, g_w_mod = _mod_bwd(cc, w_mod, b_shard, dm_mine, name="mod_bwd")
    g_c_ctx_part = jnp.sum(dcc[N_DEV * B:N_DEV * B + N_DEV], axis=0)

    small_sum = _unpack(_sum_leading(small_all, name="sum_small"), small_like)
    g.update(zip(("norm1_g", "norm2_g", "conv_dw_b", "conv_ln_g", "conv_ln_b", "attn_sink", "pool_w", "pool_scale",
                  "final_g"), small_sum[:-1]))
    g["b_mod"] = g_b_mod
    g["conv_dw"] = lax.dynamic_slice_in_dim(small_sum[-1], me * dw_cols, dw_cols, axis=2)
    c_ctx_all, = _exchange([g_c_ctx_part.reshape(8, D_MODEL // 8)], scatter=False, name="gather_c_ctx")
    g["c_ctx"] = _sum_leading(c_ctx_all, name="sum_c_ctx").reshape(D_MODEL)

    for name in big.FFN[1:] + big.MIX:
        g[name], delta[name], new_m[name], new_v[name] = _adamw(
            [big.parts[l][name] for l in range(depth)], w[name], mom[name], var[name], name=f"adamw_{name}")
    g["w_mod"], delta["w_mod"], new_m["w_mod"], new_v["w_mod"] = _adamw(g_w_mod, w_mod, m_w_mod, v_w_mod, name="adamw_w_mod")
    res = _adamw_small([g[k] for k in SMALL], [w[k] for k in SMALL], [mom[k] for k in SMALL], [var[k] for k in SMALL],
                       name="adamw_small")
    for dst, arrs in zip((delta, new_m, new_v), res):
        dst.update(zip(SMALL, arrs))

    return (loss, dx, *[g[k] for k in ORDER], *[delta[k] for k in ORDER], *[new_m[k] for k in ORDER],
            *[new_v[k] for k in ORDER])
```

```python
import functools

import numpy as np
import jax
import jax.numpy as jnp
from jax import lax
from jax.experimental import pallas as pl
from jax.experimental.pallas import tpu as pltpu

F32 = jnp.float32
BF16 = jnp.bfloat16

D_MODEL = 1024
GRID_W = 64
HEAD_DIM = 64
ATTN_W = 512
CONV_W = 256
POOL_W = 256
ATTN_HEADS = 8
KV_HEADS = 2
GROUP = ATTN_HEADS // KV_HEADS
KV_W = KV_HEADS * HEAD_DIM
IN_W = ATTN_W + 2 * KV_W + 2 * CONV_W + POOL_W
WINDOW = 128
Q_BLOCK = 128
SPAN = Q_BLOCK + 2 * WINDOW
CONV_KERNEL = 31
POOL_WINDOWS = (2, 4, 8, 16)
POOL_GROUP = 64
ROPE_BASE = 10000.0
D_FF = 2816
EPS = 1e-6
NEG = -1e30
N_DEV = 8
AXES = ("x", "y", "c")

ADAM_LR = 0.001
ADAM_B1 = 0.9
ADAM_B2 = 0.999
ADAM_EPS = 1e-08
ADAM_WD = 0.01
ADAM_STEP = 10

VMEM_LIMIT = 56 * 1024 * 1024
HALO = 16
SEQ_CHUNK = 1024
CONV_BWD_CHUNK = 512


def _params(*sem):
    return pltpu.CompilerParams(dimension_semantics=sem, vmem_limit_bytes=VMEM_LIMIT)


def _tile(dim, target):
    if dim <= target:
        return dim
    t = (target // 128) * 128
    while t >= 128:
        if dim % t == 0:
            return t
        t -= 128
    raise ValueError(f"no tile for {dim}")


MM_VMEM_BUDGET = 44 * 1024 * 1024
DW_TN = 256


def _dot(a, b):
    return lax.dot_general(a.astype(BF16), b.astype(BF16), (((1,), (0,)), ((), ())), preferred_element_type=F32)


def _mm_vmem(tm, tn, tk, whole, a_bytes, b_bytes, o_bytes):
    return 2 * (tm * tk * a_bytes + tk * tn * b_bytes + tm * tn * o_bytes) + (0 if whole else tm * tn * 4)


def _mm(a, b, *, name, out_dtype=F32, tm=1408, tn=512, trans_b=False, out_block=None, plus=None):
    M, K = a.shape
    N, K2 = b.shape if trans_b else b.shape[::-1]
    assert K == K2, (a.shape, b.shape)
    tm = _tile(M, tm)
    tn = _tile(N, tn)
    sizes = (a.dtype.itemsize, b.dtype.itemsize, jnp.dtype(out_dtype).itemsize)
    tk = next(t for t in range(K, 0, -128) if K % t == 0 and _mm_vmem(tm, tn, t, t == K, *sizes) <= MM_VMEM_BUDGET)
    nk = K // tk

    def body(a_ref, b_ref, *rest):
        plus_ref = rest[0] if plus is not None else None
        o_ref, *scratch = rest[1:] if plus is not None else rest
        part = (_dot_nt if trans_b else _dot)(a_ref[...], b_ref[...])

        def store(total):
            if plus_ref is not None:
                total = total + plus_ref[...].astype(F32)
            o_ref[...] = total.astype(o_ref.dtype)

        if nk == 1:
            store(part)
        else:
            acc_ref, = scratch
            k = pl.program_id(2)

            @pl.when(k == 0)
            def _():
                acc_ref[...] = part

            @pl.when(k > 0)
            def _():
                acc_ref[...] += part

            @pl.when(k == nk - 1)
            def _():
                store(acc_ref[...])

    out_spec = pl.BlockSpec((tm, tn), (lambda i, j, k: (i, j)) if out_block is None else (lambda i, j, k: (i, out_block(j))))
    return pl.pallas_call(
        body, name=name, grid=(M // tm, N // tn, nk),
        in_specs=[pl.BlockSpec((tm, tk), lambda i, j, k: (i, k)),
                  pl.BlockSpec((tn, tk), lambda i, j, k: (j, k)) if trans_b else pl.BlockSpec((tk, tn), lambda i, j, k: (k, j))]
        + ([out_spec] if plus is not None else []),
        out_specs=out_spec,
        out_shape=jax.ShapeDtypeStruct((M, N), out_dtype),
        scratch_shapes=[pltpu.VMEM((tm, tn), F32)] if nk > 1 else [],
        compiler_params=_params("parallel", "parallel", "arbitrary"),
    )(a, b, *([plus] if plus is not None else []))


FF_TILE = 256


FF_TILES = D_FF // FF_TILE


def _natural_block(j):
    return (j % 2) * FF_TILES + j // 2


def _interleave_ffn(w, *, name):
    R, C = w.shape

    def body(w_ref, o_ref):
        o_ref[...] = w_ref[...]

    return pl.pallas_call(
        body, name=name, grid=(2 * FF_TILES,),
        in_specs=[pl.BlockSpec((R, FF_TILE), lambda j: (0, _natural_block(j)))],
        out_specs=pl.BlockSpec((R, FF_TILE), lambda j: (0, j)),
        out_shape=jax.ShapeDtypeStruct((R, C), w.dtype),
        compiler_params=_params("parallel"),
    )(w)


def _swiglu(gu):
    g, u = gu[:, :FF_TILE], gu[:, FF_TILE:]
    return g * jax.nn.sigmoid(g) * u


EPILOGUE_SPLIT = 1
FF_ROWS = 4096


def _mm_swiglu(a, w_il, *, name, tm=FF_ROWS, exch=None, split=EPILOGUE_SPLIT):
    M, K = a.shape
    tm = _tile(M, tm)
    rc = tm // split

    def body(a_ref, b_ref, gu_ref, act_ref, act_t_ref):
        b = b_ref[...]
        parts = [_dot(a_ref[r0:r0 + rc, :], b) for r0 in range(0, tm, rc)]
        for r0, gu in zip(range(0, tm, rc), parts):
            gu_ref[r0:r0 + rc, :] = gu.astype(gu_ref.dtype)
            act = _swiglu(gu)
            act_ref[r0:r0 + rc, :] = act.astype(act_ref.dtype)
            act_t_ref[:, r0:r0 + rc] = act.T.astype(act_t_ref.dtype)

    grid = (M // tm, D_FF // FF_TILE)
    body, x_in, x_out, x_shapes, x_sems = _riding(body, exch, 2, 3, grid)
    return pl.pallas_call(
        body, name=name, grid=grid,
        in_specs=[pl.BlockSpec((tm, K), lambda i, j: (i, 0)), pl.BlockSpec((K, 2 * FF_TILE), lambda i, j: (0, j))] + x_in,
        out_specs=[pl.BlockSpec((tm, 2 * FF_TILE), lambda i, j: (i, j)), pl.BlockSpec((tm, FF_TILE), lambda i, j: (i, j)),
                   pl.BlockSpec((FF_TILE, tm), lambda i, j: (j, i))] + x_out,
        out_shape=[jax.ShapeDtypeStruct((M, 2 * D_FF), BF16), jax.ShapeDtypeStruct((M, D_FF), BF16),
                   jax.ShapeDtypeStruct((D_FF, M), BF16)] + x_shapes,
        scratch_shapes=x_sems,
        compiler_params=_params("arbitrary", "arbitrary") if exch else _params("parallel", "parallel"),
    )(a, w_il, *(exch[0] if exch else []))


def _mm_dswiglu(dy, w_out, gu, *, name, tm=FF_ROWS, split=EPILOGUE_SPLIT):
    M, K = dy.shape
    tm = _tile(M, tm)
    rc = tm // split

    def body(dy_ref, b_ref, gu_ref, o_ref):
        b = b_ref[...]
        parts = [_dot_nt(dy_ref[r0:r0 + rc, :], b) for r0 in range(0, tm, rc)]
        for r0, dact in zip(range(0, tm, rc), parts):
            g = gu_ref[r0:r0 + rc, :FF_TILE].astype(F32)
            u = gu_ref[r0:r0 + rc, FF_TILE:].astype(F32)
            sig = jax.nn.sigmoid(g)
            silu = g * sig
            o_ref[r0:r0 + rc, :FF_TILE] = (dact * u * (sig + silu * (1.0 - sig))).astype(o_ref.dtype)
            o_ref[r0:r0 + rc, FF_TILE:] = (dact * silu).astype(o_ref.dtype)

    return pl.pallas_call(
        body, name=name, grid=(M // tm, D_FF // FF_TILE),
        in_specs=[pl.BlockSpec((tm, K), lambda i, j: (i, 0)), pl.BlockSpec((FF_TILE, K), lambda i, j: (j, 0)),
                  pl.BlockSpec((tm, 2 * FF_TILE), lambda i, j: (i, j))],
        out_specs=pl.BlockSpec((tm, 2 * FF_TILE), lambda i, j: (i, j)),
        out_shape=jax.ShapeDtypeStruct((M, 2 * D_FF), BF16),
        compiler_params=_params("parallel", "parallel"),
    )(dy, w_out, gu)


def _rope_tables(n):
    rows = n // GRID_W
    row = jnp.repeat(jnp.arange(rows), GRID_W).astype(F32)
    col = jnp.tile(jnp.arange(GRID_W), rows).astype(F32)
    half = HEAD_DIM // 2
    inv = ROPE_BASE ** (-jnp.arange(0, half, 2, dtype=F32) / half)
    ar = row[:, None] * inv
    ac = col[:, None] * inv
    ang = jnp.concatenate([ar, ar, ac, ac], axis=-1)
    return jnp.cos(ang), jnp.sin(ang)


def _rot_half(x):
    w = x.shape[-1]
    lane = lax.broadcasted_iota(jnp.int32, x.shape, 1)
    up = pltpu.roll(x, w - 16, 1)
    down = pltpu.roll(x, 16, 1)
    return jnp.where((lane & 16) == 0, -up, down)


def _rope(x, cos, sin):
    return x * cos + _rot_half(x) * sin


def _rope_bwd(d, cos, sin):
    return d * cos - _rot_half(d * sin)


def _dot_nt(a, b):
    return lax.dot_general(a.astype(BF16), b.astype(BF16), (((1,), (1,)), ((), ())), preferred_element_type=F32)


def _softmax_sink(s, sink_rows):
    mx = jnp.maximum(jnp.max(s, axis=1, keepdims=True), sink_rows)
    e = jnp.exp(s - mx)
    es = jnp.exp(sink_rows - mx)
    inv = 1.0 / (jnp.sum(e, axis=1, keepdims=True) + es)
    return e * inv, es * inv


def _attn_operands(q_ref, k_ref, v_ref, kc_ref, vc_ref, cq_ref, sq_ref, ck_ref, sk_ref, i, n, n_ctx, local):
    q = q_ref[...]
    k_all, v_all, bias, s0, ck, sk = kc_ref[...], vc_ref[...], None, None, None, None
    if local:
        start, s0 = _span_start(i, n)
        ck = ck_ref[pl.ds(s0, SPAN), :]
        sk = sk_ref[pl.ds(s0, SPAN), :]
        q = _rope(q, cq_ref[...], sq_ref[...])
        k_all = jnp.concatenate([k_all, _rope(k_ref[pl.ds(s0, SPAN), :], ck, sk)], axis=0)
        v_all = jnp.concatenate([v_all, v_ref[pl.ds(s0, SPAN), :]], axis=0)
        bias = _window_bias(start, s0, n_ctx)
    q = (q * (HEAD_DIM ** -0.5)).astype(BF16)
    return q, k_all.astype(BF16), v_all.astype(BF16), bias, s0, ck, sk


def _stack_heads(x, kh):
    return jnp.concatenate([x[:, (GROUP * kh + g) * HEAD_DIM:(GROUP * kh + g + 1) * HEAD_DIM] for g in range(GROUP)], axis=0)


def _sink_rows(sink, kh):
    return jnp.concatenate([jnp.broadcast_to(sink[:, GROUP * kh + g:GROUP * kh + g + 1], (Q_BLOCK, 1)) for g in range(GROUP)], axis=0)


def _window_bias(start, s0, n_ctx):
    r = lax.broadcasted_iota(jnp.int32, (Q_BLOCK, n_ctx + SPAN), 0)
    c = lax.broadcasted_iota(jnp.int32, (Q_BLOCK, n_ctx + SPAN), 1)
    ok = (c < n_ctx) | (jnp.abs(start - s0 + r - (c - n_ctx)) <= WINDOW)
    return jnp.concatenate([jnp.where(ok, 0.0, NEG).astype(F32)] * GROUP, axis=0)


def _span_start(i, n):
    start = i * Q_BLOCK
    s0 = jnp.clip(start - WINDOW, 0, n - SPAN)
    return start, pl.multiple_of(s0, Q_BLOCK)


def _riding(body, exch, n_in, n_out, grid):
    if exch is None:
        return body, [], [], [], []
    arrs, scatter = exch
    k = len(arrs)

    def wrapped(*refs):
        ins, xin = refs[:n_in], refs[n_in:n_in + k]
        outs, xout = refs[n_in + k:n_in + k + n_out], refs[n_in + k + n_out:n_in + 2 * k + n_out]
        sems = refs[n_in + 2 * k + n_out:]
        b, i = pl.program_id(0), pl.program_id(1)

        @pl.when((b == 0) & (i == 0))
        def _():
            _exch_start(xin, xout, sems, scatter)

        body(*ins, *outs)

        @pl.when((b == grid[0] - 1) & (i == grid[1] - 1))
        def _():
            _exch_wait(xin, xout, sems, scatter)

    any_spec = pl.BlockSpec(memory_space=pl.ANY)
    return wrapped, [any_spec] * k, [any_spec] * k, _exch_out_shapes(arrs, scatter), _exch_sems(k)


def _attn_fwd(u, kvc, sink, cos, sin, *, local, name, exch=None):
    B, n, _ = u.shape
    n_ctx = kvc.shape[1]
    nb = n // Q_BLOCK
    assert (not local) or n >= SPAN

    def body(q_ref, k_ref, v_ref, kc_ref, vc_ref, sink_ref, cq_ref, sq_ref, ck_ref, sk_ref, o_ref):
        q, k_all, v_all, bias, _, _, _ = _attn_operands(q_ref, k_ref, v_ref, kc_ref, vc_ref, cq_ref, sq_ref, ck_ref, sk_ref,
                                                        pl.program_id(1), n, n_ctx, local)
        sink_v = sink_ref[...]
        sl = lambda kh: slice(kh * HEAD_DIM, (kh + 1) * HEAD_DIM)
        ss = [_dot_nt(_stack_heads(q, kh), k_all[:, sl(kh)]) for kh in range(KV_HEADS)]
        ps = [_softmax_sink(s if bias is None else s + bias, _sink_rows(sink_v, kh))[0].astype(BF16) for kh, s in enumerate(ss)]
        for kh, p in enumerate(ps):
            o = _dot(p, v_all[:, sl(kh)])
            for g in range(GROUP):
                h = GROUP * kh + g
                o_ref[:, h * HEAD_DIM:(h + 1) * HEAD_DIM] = o[g * Q_BLOCK:(g + 1) * Q_BLOCK, :].astype(o_ref.dtype)

    seq = lambda blk: pl.BlockSpec((None, n, KV_W), lambda b, i: (b, 0, blk))
    ctxs = lambda blk: pl.BlockSpec((None, n_ctx, KV_W), lambda b, i: (b, 0, blk))
    full = lambda a: pl.BlockSpec(a.shape, lambda b, i: (0,) * a.ndim)
    cos_q, sin_q = jnp.tile(cos, (1, ATTN_HEADS)), jnp.tile(sin, (1, ATTN_HEADS))
    cos_k, sin_k = jnp.tile(cos, (1, KV_HEADS)), jnp.tile(sin, (1, KV_HEADS))
    body, x_in, x_out, x_shapes, x_sems = _riding(body, exch, 10, 1, (B, nb))
    return pl.pallas_call(
        body, name=name, grid=(B, nb),
        in_specs=[pl.BlockSpec((None, Q_BLOCK, ATTN_W), lambda b, i: (b, i, 0)),
                  seq(ATTN_W // KV_W), seq(ATTN_W // KV_W + 1), ctxs(0), ctxs(1), full(sink),
                  pl.BlockSpec((Q_BLOCK, ATTN_W), lambda b, i: (i, 0)), pl.BlockSpec((Q_BLOCK, ATTN_W), lambda b, i: (i, 0)),
                  full(cos_k), full(sin_k)] + x_in,
        out_specs=[pl.BlockSpec((None, Q_BLOCK, ATTN_W), lambda b, i: (b, i, 0))] + x_out,
        out_shape=[jax.ShapeDtypeStruct((B, n, ATTN_W), BF16)] + x_shapes,
        scratch_shapes=x_sems,
        compiler_params=_params("arbitrary", "arbitrary"),
    )(u, u, u, kvc, kvc, sink, cos_q, sin_q, cos_k, sin_k, *(exch[0] if exch else []))


def _attn_bwd(u, kvc, sink, cos, sin, do_src, do_blk, *, local, name, exch=None):
    B, n, _ = u.shape
    n_ctx = kvc.shape[1]
    nb = n // Q_BLOCK

    def body(q_ref, k_ref, v_ref, kc_ref, vc_ref, sink_ref, cq_ref, sq_ref, ck_ref, sk_ref, do_ref,
             dq_ref, dk_ref, dv_ref, dkc_ref, dvc_ref, dsink_ref):
        b = pl.program_id(0)
        i = pl.program_id(1)

        @pl.when(i == 0)
        def _():
            dk_ref[...] = jnp.zeros_like(dk_ref)
            dv_ref[...] = jnp.zeros_like(dv_ref)
            dkc_ref[...] = jnp.zeros_like(dkc_ref)
            dvc_ref[...] = jnp.zeros_like(dvc_ref)

        @pl.when((i == 0) & (b == 0))
        def _():
            dsink_ref[...] = jnp.zeros_like(dsink_ref)

        q, k_all, v_all, bias, s0, ck, sk = _attn_operands(q_ref, k_ref, v_ref, kc_ref, vc_ref, cq_ref, sq_ref, ck_ref, sk_ref,
                                                           i, n, n_ctx, local)
        do = do_ref[...].astype(BF16)
        sink_v = sink_ref[...]
        sl = lambda kh: slice(kh * HEAD_DIM, (kh + 1) * HEAD_DIM)
        heads = range(KV_HEADS)
        q_st = [_stack_heads(q, kh) for kh in heads]
        do_st = [_stack_heads(do, kh) for kh in heads]
        ss = [_dot_nt(q_st[kh], k_all[:, sl(kh)]) for kh in heads]
        dps = [_dot_nt(do_st[kh], v_all[:, sl(kh)]) for kh in heads]
        p_bf, ds_bf = [], []
        dsink = jnp.zeros((1, ATTN_HEADS), F32)
        lane8 = lax.broadcasted_iota(jnp.int32, (1, ATTN_HEADS), 1)
        for kh in heads:
            p, p_sink = _softmax_sink(ss[kh] if bias is None else ss[kh] + bias, _sink_rows(sink_v, kh))
            delta = jnp.sum(p * dps[kh], axis=1, keepdims=True)
            ds = p * (dps[kh] - delta)
            dsr = -(p_sink * delta)
            for g in range(GROUP):
                dsink = dsink + jnp.where(lane8 == GROUP * kh + g, jnp.sum(dsr[g * Q_BLOCK:(g + 1) * Q_BLOCK, :]), 0.0)
            p_bf.append(p.astype(BF16))
            ds_bf.append(ds.astype(BF16))
        over_rows = (((0,), (0,)), ((), ()))
        dks, dvs = [], []
        for kh in heads:
            dq_st = _dot(ds_bf[kh], k_all[:, sl(kh)]) * (HEAD_DIM ** -0.5)
            for g in range(GROUP):
                h = GROUP * kh + g
                dq_ref[:, h * HEAD_DIM:(h + 1) * HEAD_DIM] = dq_st[g * Q_BLOCK:(g + 1) * Q_BLOCK, :]
            dvs.append(lax.dot_general(p_bf[kh], do_st[kh], over_rows, preferred_element_type=F32))
            dks.append(lax.dot_general(ds_bf[kh], q_st[kh], over_rows, preferred_element_type=F32))
        dk_cat = jnp.concatenate(dks, axis=1)
        dv_cat = jnp.concatenate(dvs, axis=1)
        dsink_ref[...] += dsink
        dkc_ref[...] += dk_cat[:n_ctx, :]
        dvc_ref[...] += dv_cat[:n_ctx, :]
        if local:
            dq_ref[...] = _rope_bwd(dq_ref[...], cq_ref[...], sq_ref[...])
            dk_ref[pl.ds(s0, SPAN), :] += _rope_bwd(dk_cat[n_ctx:, :], ck, sk)
            dv_ref[pl.ds(s0, SPAN), :] += dv_cat[n_ctx:, :]

    seq = lambda blk: pl.BlockSpec((None, n, KV_W), lambda b, i: (b, 0, blk))
    ctxs = lambda blk: pl.BlockSpec((None, n_ctx, KV_W), lambda b, i: (b, 0, blk))
    full = lambda a: pl.BlockSpec(a.shape, lambda b, i: (0,) * a.ndim)
    qblk = lambda blk: pl.BlockSpec((None, Q_BLOCK, ATTN_W), lambda b, i: (b, i, blk))
    cos_q, sin_q = jnp.tile(cos, (1, ATTN_HEADS)), jnp.tile(sin, (1, ATTN_HEADS))
    cos_k, sin_k = jnp.tile(cos, (1, KV_HEADS)), jnp.tile(sin, (1, KV_HEADS))
    acc = lambda rows: pl.BlockSpec((None, rows, KV_W), lambda b, i: (b, 0, 0))
    body, x_in, x_out, x_shapes, x_sems = _riding(body, exch, 11, 6, (B, nb))
    return pl.pallas_call(
        body, name=name, grid=(B, nb),
        in_specs=[qblk(0), seq(ATTN_W // KV_W), seq(ATTN_W // KV_W + 1), ctxs(0), ctxs(1), full(sink),
                  pl.BlockSpec((Q_BLOCK, ATTN_W), lambda b, i: (i, 0)), pl.BlockSpec((Q_BLOCK, ATTN_W), lambda b, i: (i, 0)),
                  full(cos_k), full(sin_k), qblk(do_blk)] + x_in,
        out_specs=[qblk(0), acc(n), acc(n), acc(n_ctx), acc(n_ctx), pl.BlockSpec((1, ATTN_HEADS), lambda b, i: (0, 0))] + x_out,
        out_shape=[jax.ShapeDtypeStruct((B, n, ATTN_W), F32), jax.ShapeDtypeStruct((B, n, KV_W), F32),
                   jax.ShapeDtypeStruct((B, n, KV_W), F32), jax.ShapeDtypeStruct((B, n_ctx, KV_W), F32),
                   jax.ShapeDtypeStruct((B, n_ctx, KV_W), F32), jax.ShapeDtypeStruct((1, ATTN_HEADS), F32)] + x_shapes,
        scratch_shapes=x_sems,
        compiler_params=_params("arbitrary", "arbitrary"),
    )(u, u, u, kvc, kvc, sink, cos_q, sin_q, cos_k, sin_k, do_src, *(exch[0] if exch else []))


def _conv_chunk(s, n, a_ext, g_ext, dw, dw_b, ln_g, ln_b):
    del s, n
    acc = _conv_taps(a_ext, g_ext, dw, dw_b)
    return _conv_tail(acc, ln_g, ln_b), acc


def _conv_chunk_bwd(s, n, ext, pars, acc, do):
    del s, n
    dw, dw_b, ln_g, ln_b = pars
    _, tail_vjp = jax.vjp(_conv_tail, acc, ln_g, ln_b)
    dacc, dln_g, dln_b = tail_vjp(do)
    _, taps_vjp = jax.vjp(_conv_taps, *ext, dw, dw_b)
    return (*taps_vjp(dacc), dln_g, dln_b)


def _conv_tail(acc, ln_g, ln_b):
    mu = jnp.mean(acc, axis=-1, keepdims=True)
    var = jnp.mean(jnp.square(acc - mu), axis=-1, keepdims=True)
    hn = (acc - mu) * lax.rsqrt(var + EPS) * ln_g + ln_b
    return hn * jax.nn.sigmoid(hn)


def _conv_taps(a_ext, g_ext, dw, dw_b):
    r = a_ext.shape[0] - 2 * HALO
    h = a_ext * jax.nn.sigmoid(g_ext)
    acc = jnp.broadcast_to(dw_b, (r, CONV_W))
    first = HALO - CONV_KERNEL // 2
    span = r + 8 * ((first + CONV_KERNEL - 1) // 8)
    shifted = [h[b:b + span, :] for b in range(8)]
    for k in range(CONV_KERNEL):
        o = first + k
        acc = acc + shifted[o % 8][o - o % 8:o - o % 8 + r, :] * dw[k:k + 1, :]
    return acc


def _pool_chunk(s, n, p_ext, w_bd, scale):
    r = p_ext.shape[0] - 2 * HALO
    lane = lax.broadcasted_iota(jnp.int32, (1, POOL_W), 1)
    win = jnp.left_shift(2, lane // POOL_GROUP)
    half = win // 2
    acc = jnp.zeros((r, POOL_W), F32)
    for d in range(-(POOL_WINDOWS[-1] // 2), POOL_WINDOWS[-1] - POOL_WINDOWS[-1] // 2):
        inside = (d >= -half) & (d <= win - 1 - half)
        acc = acc + jnp.where(inside, p_ext[HALO + d:HALO + d + r, :], 0.0)
    t = s + lax.broadcasted_iota(jnp.int32, (r, 1), 0)
    lo = jnp.maximum(t - half, 0)
    hi = jnp.minimum(t + win - 1 - half, n - 1)
    y = acc / (hi - lo + 1).astype(F32) - p_ext[HALO:HALO + r, :]
    out = lax.dot_general(y.astype(BF16), w_bd.astype(BF16), (((1,), (0,)), ((), ())), preferred_element_type=F32)
    return out * scale


def _seq_specs(rows, params):
    specs = [pl.BlockSpec((None, a.shape[1], w), functools.partial(lambda b, blk: (b, 0, blk), blk=blk)) for a, w, blk in rows]
    specs += [pl.BlockSpec(p.shape, functools.partial(lambda b, nd: (0,) * nd, nd=p.ndim)) for p in params]
    return specs


def _fill_padded(pad_ref, row_ref, n):
    w = pad_ref.shape[1]
    pad_ref[pl.ds(0, HALO), :] = jnp.zeros((HALO, w), F32)
    pad_ref[pl.ds(HALO + n, HALO), :] = jnp.zeros((HALO, w), F32)
    pad_ref[pl.ds(HALO, n), :] = row_ref[...]


def _seq_fwd(fn, rows, params, out_w, *, name, chunk=SEQ_CHUNK, aux_w=None):
    B, n = rows[0][0].shape[:2]
    r = min(chunk, n)
    nr, npar = len(rows), len(params)
    nout = 1 if aux_w is None else 2

    def body(*refs):
        row_refs, par_refs = refs[:nr], refs[nr:nr + npar]
        out_refs, pads = refs[nr + npar:nr + npar + nout], refs[nr + npar + nout:]
        for rr, p in zip(row_refs, pads):
            _fill_padded(p, rr, n)
        pars = [p[...] for p in par_refs]

        def chunk(ci, carry):
            s = pl.multiple_of(ci * r, r)
            ext = [p[pl.ds(s, r + 2 * HALO), :] for p in pads]
            res = fn(s, n, *ext, *pars)
            for o_ref, v in zip(out_refs, res if nout == 2 else (res,)):
                o_ref[pl.ds(s, r), :] = v.astype(o_ref.dtype)
            return carry

        lax.fori_loop(0, n // r, chunk, 0)

    widths = [(out_w, BF16)] + ([] if aux_w is None else [(aux_w, F32)])
    res = pl.pallas_call(
        body, name=name, grid=(B,),
        in_specs=_seq_specs(rows, params),
        out_specs=[pl.BlockSpec((None, n, w), lambda b: (b, 0, 0)) for w, _ in widths],
        out_shape=[jax.ShapeDtypeStruct((B, n, w), dt) for w, dt in widths],
        scratch_shapes=[pltpu.VMEM((n + 2 * HALO, w), F32) for _, w, _ in rows],
        compiler_params=_params("parallel"),
    )(*[a for a, _, _ in rows], *params)
    return res[0] if aux_w is None else res


def _seq_bwd(fn, rows, params, dout, *, name, chunk=SEQ_CHUNK, aux=None):
    B, n = rows[0][0].shape[:2]
    r = min(chunk, n)
    nr, npar = len(rows), len(params)
    naux = 0 if aux is None else 1

    def body(*refs):
        row_refs, par_refs, do_ref = refs[:nr], refs[nr:nr + npar], refs[nr + npar]
        aux_refs = refs[nr + npar + 1:nr + npar + 1 + naux]
        outs = refs[nr + npar + 1 + naux:]
        drow_refs, dpar_refs = outs[:nr], outs[nr:nr + npar]
        scratch = outs[nr + npar:]
        pads, dpads = scratch[:nr], scratch[nr:]
        for rr, p, dp in zip(row_refs, pads, dpads):
            _fill_padded(p, rr, n)
            dp[...] = jnp.zeros_like(dp)

        @pl.when(pl.program_id(0) == 0)
        def _():
            for d in dpar_refs:
                d[...] = jnp.zeros_like(d)

        pars = [p[...] for p in par_refs]

        def chunk(ci, carry):
            s = pl.multiple_of(ci * r, r)
            ext = [p[pl.ds(s, r + 2 * HALO), :] for p in pads]
            do = do_ref[pl.ds(s, r), :]
            if aux is None:
                _, vjp = jax.vjp(functools.partial(fn, s, n), *ext, *pars)
                grads = vjp(do)
            else:
                grads = fn(s, n, ext, pars, aux_refs[0][pl.ds(s, r), :], do)
            for dp, g in zip(dpads, grads[:nr]):
                dp[pl.ds(s, r + 2 * HALO), :] += g
            for d, g in zip(dpar_refs, grads[nr:]):
                d[...] += g
            return carry

        lax.fori_loop(0, n // r, chunk, 0)
        for d, dp in zip(drow_refs, dpads):
            d[...] = dp[pl.ds(HALO, n), :].astype(d.dtype)

    da, dw_, dblk = dout
    auxs = [] if aux is None else [aux]
    return pl.pallas_call(
        body, name=name, grid=(B,),
        in_specs=_seq_specs(rows, params) + [pl.BlockSpec((None, n, dw_), lambda b: (b, 0, dblk))]
        + [pl.BlockSpec((None, n, a.shape[2]), lambda b: (b, 0, 0)) for a in auxs],
        out_specs=[pl.BlockSpec((None, n, w), lambda b: (b, 0, 0)) for _, w, _ in rows]
        + [pl.BlockSpec(p.shape, functools.partial(lambda b, nd: (0,) * nd, nd=p.ndim)) for p in params],
        out_shape=[jax.ShapeDtypeStruct((B, n, w), BF16) for _, w, _ in rows]
        + [jax.ShapeDtypeStruct(p.shape, F32) for p in params],
        scratch_shapes=[pltpu.VMEM((n + 2 * HALO, w), F32) for _, w, _ in rows] * 2,
        compiler_params=_params("arbitrary"),
    )(*[a for a, _, _ in rows], *params, da, *auxs)


_CONV_A_BLK = (ATTN_W + 2 * KV_W) // CONV_W
_CONV_G_BLK = _CONV_A_BLK + 1
_POOL_BLK = _CONV_A_BLK + 2


def _mixer_fwd(tag, u, kvc, margs, local, exch=None):
    sink, dw, dw_b, ln_g, ln_b, w_bd, scale = margs
    cos, sin = _rope_tables(max(u.shape[1], GRID_W))
    attn, *got = _attn_fwd(u, kvc, sink, cos, sin, local=local, name=f"{tag}_attn_fwd", exch=exch)
    conv, conv_acc = _seq_fwd(_conv_chunk, [(u, CONV_W, _CONV_A_BLK), (u, CONV_W, _CONV_G_BLK)], [dw, dw_b, ln_g, ln_b],
                              CONV_W, name=f"{tag}_conv_fwd", aux_w=CONV_W)
    pool = _seq_fwd(_pool_chunk, [(u, POOL_W, _POOL_BLK)], [w_bd, scale], POOL_W, name=f"{tag}_pool_fwd")
    return jnp.concatenate([attn, conv, pool], axis=-1), conv_acc, got


def _mixer_bwd(tag, u, kvc, conv_acc, margs, dmix, local, exch=None):
    sink, dw, dw_b, ln_g, ln_b, w_bd, scale = margs
    cos, sin = _rope_tables(max(u.shape[1], GRID_W))
    dq, dk, dv, dkc, dvc, dsink, *got = _attn_bwd(u, kvc, sink, cos, sin, dmix, 0, local=local, name=f"{tag}_attn_bwd",
                                                  exch=exch)
    da, dg, ddw, ddw_b, dln_g, dln_b = _seq_bwd(
        _conv_chunk_bwd, [(u, CONV_W, _CONV_A_BLK), (u, CONV_W, _CONV_G_BLK)], [dw, dw_b, ln_g, ln_b],
        (dmix, CONV_W, ATTN_W // CONV_W), name=f"{tag}_conv_bwd", chunk=CONV_BWD_CHUNK, aux=conv_acc)
    dpu, dw_bd, dscale = _seq_bwd(_pool_chunk, [(u, POOL_W, _POOL_BLK)], [w_bd, scale],
                                  (dmix, POOL_W, (ATTN_W + CONV_W) // POOL_W), name=f"{tag}_pool_bwd")
    return (dq, dk, dv, da, dg, dpu), (dkc, dvc), (dsink, ddw, ddw_b, dln_g, dln_b, dw_bd, dscale), got


def _row_specs(arrs, kinds, tr):
    specs = []
    for a, kind in zip(arrs, kinds):
        if kind == "row":
            specs.append(pl.BlockSpec((None, tr, a.shape[2]), lambda b, j: (b, j, 0)))
        elif kind == "batch":
            specs.append(pl.BlockSpec((None, 1, a.shape[2]), lambda b, j: (b, 0, 0)))
        else:
            specs.append(pl.BlockSpec(a.shape, functools.partial(lambda b, j, nd: (0,) * nd, nd=a.ndim)))
    return specs


def _rowwise_fwd(fn, ins, kinds, outs, tr, *, name, transposed=None, exch=None):
    B, n = ins[0].shape[:2]
    ni, no = len(ins), len(outs)
    nj = n // tr

    def body(*refs):
        res = fn(*[r[...] for r in refs[:ni]])
        for o, v in zip(refs[ni:ni + no], res):
            o[...] = v.astype(o.dtype)
        if transposed is not None:
            refs[ni + no][...] = res[transposed].T.astype(refs[ni + no].dtype)

    out_specs = [pl.BlockSpec((None, tr, w), lambda b, j: (b, j, 0)) for w, _ in outs]
    out_shape = [jax.ShapeDtypeStruct((B, n, w), dt) for w, dt in outs]
    if transposed is not None:
        w, dt = outs[transposed]
        out_specs.append(pl.BlockSpec((w, tr), lambda b, j: (0, b * nj + j)))
        out_shape.append(jax.ShapeDtypeStruct((w, B * n), dt))
    body, x_in, x_out, x_shapes, x_sems = _riding(body, exch, ni, len(out_specs), (B, nj))
    return pl.pallas_call(
        body, name=name, grid=(B, nj),
        in_specs=_row_specs(ins, kinds, tr) + x_in, out_specs=out_specs + x_out, out_shape=out_shape + x_shapes,
        scratch_shapes=x_sems,
        compiler_params=_params("arbitrary", "arbitrary") if exch else _params("parallel", "parallel"),
    )(*ins, *(exch[0] if exch else []))


def _rowwise_bwd(fn, ins, kinds, gdtypes, cts, tr, *, name):
    B, n = ins[0].shape[:2]
    ni, nc = len(ins), len(cts)
    idx = list(range(ni))

    def body(*refs):
        in_refs, ct_refs, out_refs = refs[:ni], refs[ni:ni + nc], refs[ni + nc:]
        b, j = pl.program_id(0), pl.program_id(1)
        _, vjp = jax.vjp(fn, *[r[...].astype(F32) for r in in_refs])
        grads = vjp(tuple(c[...].astype(F32) for c in ct_refs))
        for o, i in zip(out_refs, idx):
            g = grads[i]
            if kinds[i] == "row":
                o[...] = g.astype(o.dtype)
            else:
                first = (j == 0) if kinds[i] == "batch" else ((j == 0) & (b == 0))

                @pl.when(first)
                def _(o=o, g=g):
                    o[...] = g

                @pl.when(jnp.logical_not(first))
                def _(o=o, g=g):
                    o[...] += g

    specs = _row_specs(ins, kinds, tr)
    return pl.pallas_call(
        body, name=name, grid=(B, n // tr),
        in_specs=specs + [pl.BlockSpec((None, tr, c.shape[2]), lambda b, j: (b, j, 0)) for c in cts],
        out_specs=[specs[i] for i in idx],
        out_shape=[jax.ShapeDtypeStruct(ins[i].shape, gdtypes[i]) for i in idx],
        compiler_params=_params("arbitrary", "arbitrary"),
    )(*ins, *cts)


ROW_TILE = 512


def _rms_mod(x, g, sc, sh):
    y = x * lax.rsqrt(jnp.mean(x * x, axis=-1, keepdims=True) + EPS)
    return (y * g) * (1.0 + sc) + sh


def _norm_tile(x, g, sc, sh):
    return x, _rms_mod(x, g, sc, sh)


def _res_norm_tile(xb, y, gate, g, sc, sh):
    x = xb + gate * y
    return x, _rms_mod(x, g, sc, sh)


_NORM_KINDS = ("row", "glob", "batch", "batch")
_RES_NORM_KINDS = ("row", "row", "batch", "glob", "batch", "batch")


def _norm_fwd(tag, st, g, sc, sh, exch=None):
    xb, y, gate = st
    tr = min(ROW_TILE, xb.shape[1])
    d = xb.shape[2]
    if y is None:
        return [xb, *_rowwise_fwd(lambda *a: (_rms_mod(*a),), [xb, g, sc, sh], _NORM_KINDS, [(d, BF16)], tr,
                                  name=f"{tag}_fwd", transposed=0, exch=exch)]
    return _rowwise_fwd(_res_norm_tile, [xb, y, gate, g, sc, sh], _RES_NORM_KINDS, [(d, F32), (d, BF16)], tr,
                        name=f"{tag}_fwd", transposed=1, exch=exch)


def _norm_bwd(tag, st, g, sc, sh, dx, dh):
    xb, y, gate = st
    tr = min(ROW_TILE, xb.shape[1])
    if y is None:
        dxb, dg, dsc, dsh = _rowwise_bwd(_norm_tile, [xb, g, sc, sh], _NORM_KINDS, [F32] * 4, [dx, dh], tr, name=f"{tag}_bwd")
        return dxb, None, None, dg, dsc, dsh
    return tuple(_rowwise_bwd(_res_norm_tile, [xb, y, gate, g, sc, sh], _RES_NORM_KINDS, [F32, BF16, F32, F32, F32, F32],
                              [dx, dh], tr, name=f"{tag}_bwd"))


def _loss_head(st, final_g, target, *, name):
    xb, y, gate = st
    B, n, d = xb.shape
    tr = min(ROW_TILE, n)

    def tile_loss(xv, yv, gt, g, t):
        x = xv + gt * yv
        out = x * lax.rsqrt(jnp.mean(x * x, axis=-1, keepdims=True) + EPS) * g
        return 0.5 * jnp.sum(jnp.mean(jnp.square(out - t), axis=-1))

    def body(x_ref, y_ref, gate_ref, g_ref, t_ref, loss_ref, dx_ref, dy_ref, dgate_ref, dg_ref):
        b, j = pl.program_id(0), pl.program_id(1)
        val, (dx, dy, dgate, dg) = jax.value_and_grad(tile_loss, argnums=(0, 1, 2, 3))(
            x_ref[...], y_ref[...], gate_ref[...], g_ref[...], t_ref[...])
        dx_ref[...] = dx
        dy_ref[...] = dy.astype(dy_ref.dtype)

        @pl.when(j == 0)
        def _():
            loss_ref[...] = jnp.zeros_like(loss_ref)
            dgate_ref[...] = jnp.zeros_like(dgate_ref)

        @pl.when((j == 0) & (b == 0))
        def _():
            dg_ref[...] = jnp.zeros_like(dg_ref)

        loss_ref[...] += jnp.full(loss_ref.shape, val, F32)
        dgate_ref[...] += dgate
        dg_ref[...] += dg

    row = pl.BlockSpec((None, tr, d), lambda b, j: (b, j, 0))
    per_sample = pl.BlockSpec((None, 1, d), lambda b, j: (b, 0, 0))
    whole = pl.BlockSpec((1, d), lambda b, j: (0, 0))
    return pl.pallas_call(
        body, name=name, grid=(B, n // tr),
        in_specs=[row, row, per_sample, whole, row],
        out_specs=[pl.BlockSpec((None, 1, 128), lambda b, j: (b, 0, 0)), row, row, per_sample, whole],
        out_shape=[jax.ShapeDtypeStruct((B, 1, 128), F32), jax.ShapeDtypeStruct((B, n, d), F32),
                   jax.ShapeDtypeStruct((B, n, d), BF16), jax.ShapeDtypeStruct((B, 1, d), F32), jax.ShapeDtypeStruct((1, d), F32)],
        compiler_params=_params("arbitrary", "arbitrary"),
    )(xb, y, gate, final_g, target)


def _exchange(arrs, *, scatter, name):
    k = len(arrs)

    def body(*refs):
        ins, outs, sems = refs[:k], refs[k:2 * k], refs[2 * k:]
        _exch_start(ins, outs, sems, scatter)
        _exch_wait(ins, outs, sems, scatter)

    any_spec = pl.BlockSpec(memory_space=pl.ANY)
    return pl.pallas_call(
        body, name=name,
        in_specs=[any_spec] * k, out_specs=[any_spec] * k,
        out_shape=_exch_out_shapes(arrs, scatter), scratch_shapes=_exch_sems(k),
        compiler_params=pltpu.CompilerParams(has_side_effects=True),
    )(*arrs)


def _exch_flags(scatter, k):
    return [scatter] * k if isinstance(scatter, bool) else list(scatter)


def _exch_out_shapes(arrs, scatter):
    return [jax.ShapeDtypeStruct(a.shape if f else (N_DEV,) + a.shape, a.dtype)
            for a, f in zip(arrs, _exch_flags(scatter, len(arrs)))]


def _exch_sems(k):
    return [pltpu.SemaphoreType.DMA((k * (N_DEV - 1),)), pltpu.SemaphoreType.DMA((k * (N_DEV - 1),)),
            pltpu.SemaphoreType.DMA((k,))]


def _exch_copies(ins, outs, sems, scatter):
    send_sems, recv_sems, local_sems = sems
    x, y, c = lax.axis_index("x"), lax.axis_index("y"), lax.axis_index("c")
    me = 4 * x + 2 * y + c
    owns, sends, recvs = [], [], []
    flags = _exch_flags(scatter, len(ins))
    for a in range(len(ins)):
        scatter = flags[a]
        owns.append(pltpu.make_async_copy(ins[a].at[me] if scatter else ins[a], outs[a].at[me], local_sems.at[a]))
        for r in range(1, N_DEV):
            fx, fy, fc = (r >> 2) & 1, (r >> 1) & 1, r & 1
            px, py, pc = (x + fx) % 2, (y + fy) % 2, (c + fc) % 2
            peer = 4 * px + 2 * py + pc
            s = a * (N_DEV - 1) + r - 1
            mk = functools.partial(pltpu.make_async_remote_copy, src_ref=ins[a].at[peer] if scatter else ins[a],
                                   send_sem=send_sems.at[s], recv_sem=recv_sems.at[s],
                                   device_id=(px, py, pc), device_id_type=pl.DeviceIdType.MESH)
            sends.append(mk(dst_ref=outs[a].at[me]))
            recvs.append(mk(dst_ref=outs[a].at[peer]))
    return owns, sends, recvs


def _exch_start(ins, outs, sems, scatter):
    owns, sends, _ = _exch_copies(ins, outs, sems, scatter)
    for cp in owns + sends:
        cp.start()


def _exch_wait(ins, outs, sems, scatter):
    owns, sends, recvs = _exch_copies(ins, outs, sems, scatter)
    for rc in recvs:
        rc.wait_recv()
    for cp in sends:
        cp.wait_send()
    for own in owns:
        own.wait()


MOD_ROWS = 48


def _mod_tile(cc, w, b):
    s = cc * jax.nn.sigmoid(cc)
    return lax.dot_general(s.astype(BF16), w.astype(BF16), (((1,), (0,)), ((), ())), preferred_element_type=F32) + b


def _mod_fwd(cc, w_mod, b_shard, *, name):
    L, d, wcols = w_mod.shape

    def body(cc_ref, w_ref, b_ref, o_ref):
        o_ref[...] = _mod_tile(cc_ref[...], w_ref[...], b_ref[...])

    return pl.pallas_call(
        body, name=name, grid=(L,),
        in_specs=[pl.BlockSpec((MOD_ROWS, d), lambda l: (0, 0)), pl.BlockSpec((None, d, wcols), lambda l: (l, 0, 0)),
                  pl.BlockSpec((None, 1, wcols), lambda l: (l, 0, 0))],
        out_specs=pl.BlockSpec((None, MOD_ROWS, wcols), lambda l: (l, 0, 0)),
        out_shape=jax.ShapeDtypeStruct((L, MOD_ROWS, wcols), F32),
        compiler_params=_params("parallel"),
    )(cc, w_mod, b_shard)


def _mod_bwd(cc, w_mod, b_shard, dm, *, name):
    L, d, wcols = w_mod.shape

    def body(cc_ref, w_ref, b_ref, dm_ref, dcc_ref, dw_ref):
        _, vjp = jax.vjp(_mod_tile, cc_ref[...], w_ref[...], b_ref[...])
        dcc, dw, _ = vjp(dm_ref[...])
        dw_ref[...] = dw

        @pl.when(pl.program_id(0) == 0)
        def _():
            dcc_ref[...] = dcc

        @pl.when(pl.program_id(0) > 0)
        def _():
            dcc_ref[...] += dcc

    return pl.pallas_call(
        body, name=name, grid=(L,),
        in_specs=[pl.BlockSpec((MOD_ROWS, d), lambda l: (0, 0)), pl.BlockSpec((None, d, wcols), lambda l: (l, 0, 0)),
                  pl.BlockSpec((None, 1, wcols), lambda l: (l, 0, 0)), pl.BlockSpec((None, MOD_ROWS, wcols), lambda l: (l, 0, 0))],
        out_specs=[pl.BlockSpec((MOD_ROWS, d), lambda l: (0, 0)), pl.BlockSpec((None, d, wcols), lambda l: (l, 0, 0))],
        out_shape=[jax.ShapeDtypeStruct((MOD_ROWS, d), F32), jax.ShapeDtypeStruct((L, d, wcols), F32)],
        compiler_params=_params("arbitrary"),
    )(cc, w_mod, b_shard, dm)


def _sum_leading(a, *, name):
    K, R, C = a.shape
    tr = _tile8(R, 256)

    def body(a_ref, o_ref):
        acc = a_ref[0].astype(F32)
        for i in range(1, K):
            acc = acc + a_ref[i].astype(F32)
        o_ref[...] = acc

    return pl.pallas_call(
        body, name=name, grid=(R // tr,),
        in_specs=[pl.BlockSpec((K, tr, C), lambda i: (0, i, 0))],
        out_specs=pl.BlockSpec((tr, C), lambda i: (i, 0)),
        out_shape=jax.ShapeDtypeStruct((R, C), F32),
        compiler_params=_params("parallel"),
    )(a)


def _tile8(dim, target):
    if dim <= target:
        return dim
    t = (target // 8) * 8
    while t >= 8:
        if dim % t == 0:
            return t
        t -= 8
    raise ValueError(f"no row tile for {dim}")


def _adamw_math(g, w, m, v):
    m = ADAM_B1 * m + (1.0 - ADAM_B1) * g
    v = ADAM_B2 * v + (1.0 - ADAM_B2) * jnp.square(g)
    m_hat = m / (1.0 - ADAM_B1 ** ADAM_STEP)
    v_hat = v / (1.0 - ADAM_B2 ** ADAM_STEP)
    delta = -ADAM_LR * (m_hat / (jnp.sqrt(v_hat) + ADAM_EPS) + ADAM_WD * w)
    return delta, m, v


def _adamw(g, w, m, v, *, name, exch=None):
    L, R, C = w.shape
    parts = isinstance(g, (list, tuple))
    gs = list(g) if parts else [g]
    ng = len(gs)
    tr = _tile8(R, 256)

    def body(*refs):
        g_refs = refs[:ng]
        w_ref, m_ref, v_ref, go_ref, d_ref, mo_ref, vo_ref = refs[ng:]
        if parts:
            layer = pl.program_id(0)
            gv = None
            for li, g_ref in enumerate(g_refs):
                acc = g_ref[0].astype(F32)
                for i in range(1, N_DEV):
                    acc = acc + g_ref[i].astype(F32)
                gv = acc if gv is None else jnp.where(layer == li, acc, gv)
        else:
            gv = g_refs[0][...]
        go_ref[...] = gv
        d_ref[...], mo_ref[...], vo_ref[...] = _adamw_math(gv, w_ref[...], m_ref[...], v_ref[...])

    tile = pl.BlockSpec((None, tr, C), lambda l, i: (l, i, 0))
    g_specs = [pl.BlockSpec((N_DEV, tr, C), lambda l, i: (0, i, 0))] * ng if parts else [tile]
    grid = (L, R // tr)
    body, x_in, x_out, x_shapes, x_sems = _riding(body, exch, ng + 3, 4, grid)
    return pl.pallas_call(
        body, name=name, grid=grid,
        in_specs=g_specs + [tile, tile, tile] + x_in, out_specs=[tile] * 4 + x_out,
        out_shape=[jax.ShapeDtypeStruct((L, R, C), F32)] * 4 + x_shapes,
        scratch_shapes=x_sems,
        compiler_params=_params("arbitrary", "arbitrary") if exch else _params("parallel", "parallel"),
    )(*gs, w, m, v, *(exch[0] if exch else []))


def _adamw_small(gs, ws, ms, vs, *, name):
    k = len(ws)

    def body(*refs):
        g_refs, w_refs, m_refs, v_refs = refs[:k], refs[k:2 * k], refs[2 * k:3 * k], refs[3 * k:4 * k]
        d_refs, mo_refs, vo_refs = refs[4 * k:5 * k], refs[5 * k:6 * k], refs[6 * k:]
        for i in range(k):
            d_refs[i][...], mo_refs[i][...], vo_refs[i][...] = _adamw_math(g_refs[i][...], w_refs[i][...], m_refs[i][...],
                                                                         v_refs[i][...])

    shapes = [jax.ShapeDtypeStruct(a.shape, F32) for a in ws]
    out = pl.pallas_call(body, name=name, out_shape=shapes * 3, compiler_params=pltpu.CompilerParams(vmem_limit_bytes=VMEM_LIMIT))(
        *gs, *ws, *ms, *vs)
    return out[:k], out[k:2 * k], out[2 * k:]


def _block_diag(w):
    g, c, d = w.shape
    return (w[:, :, None, :] * jnp.eye(g, dtype=w.dtype)[:, None, :, None]).reshape(g * c, g * d)


def _diag_blocks(w_bd):
    g = POOL_W // POOL_GROUP
    return jnp.stack([w_bd[i * POOL_GROUP:(i + 1) * POOL_GROUP, i * POOL_GROUP:(i + 1) * POOL_GROUP] for i in range(g)])


def _flat(a):
    return a.reshape(-1, a.shape[-1])


def _mix_half_fwd(tag, st, mods, wl, kvc, *, local, kv_only, exch=None, normed=None):
    sh1, sc1, g1 = mods[:3]
    B, n, d = st[0].shape
    x, h, h_t = normed if normed is not None else _norm_fwd(f"{tag}_norm1", st, wl["n1"], sc1, sh1)
    if kv_only:
        kv = _mm(_flat(h), wl["w_in"][:, ATTN_W:ATTN_W + 2 * KV_W], name=f"{tag}_kv").reshape(B, n, 2 * KV_W)
        return None, dict(st=st, h_t=h_t, kvc=kv), []
    u = _mm(_flat(h), wl["w_in"], name=f"{tag}_in", tn=IN_W).reshape(B, n, IN_W)
    if not local:
        kvc = u[:, :, ATTN_W:ATTN_W + 2 * KV_W]
    mix, conv_acc, got = _mixer_fwd(f"{tag}_mix", u, kvc, wl["margs"], local, exch)
    y = _mm(_flat(mix), wl["w_out"], name=f"{tag}_out", tn=D_MODEL).reshape(B, n, d)
    return (x, y, g1), dict(st=st, h_t=h_t, u=u, kvc=kvc, mix=mix, conv_acc=conv_acc), got


def _ffn_half_fwd(tag, st2, mods, wl, exch=None):
    sh2, sc2, g2 = mods[3:]
    B, n, d = st2[0].shape
    x1, h2, h2_t = _norm_fwd(f"{tag}_norm2", st2, wl["n2"], sc2, sh2)
    gu, act, act_t, *got = _mm_swiglu(_flat(h2), wl["w_ffn_in"], name=f"{tag}_ffn_in", exch=exch)
    y2 = _mm(act, wl["w_ffn_out"], name=f"{tag}_ffn_out", tn=D_MODEL).reshape(B, n, d)
    return (x1, y2, g2), dict(st2=st2, h2_t=h2_t, gu=gu, act_t=act_t), got


def _ffn_half_bwd(tag, sv, mods, wl, dx1, dy2, plus=None):
    sh2, sc2, _ = mods[3:]
    B, n, d = sv["st2"][0].shape
    gw = {}
    dy2f = _flat(dy2)
    plus = plus or {}
    gw["w_ffn_out"] = _mm(sv["act_t"], dy2f, out_dtype=BF16, name=f"{tag}_ffn_out_dw", plus=plus.get("w_ffn_out"))
    dgu = _mm_dswiglu(dy2f, wl["w_ffn_out"], sv["gu"], name=f"{tag}_ffn_out_dx")
    dh2 = _mm(dgu, wl["w_ffn_in"], trans_b=True, name=f"{tag}_ffn_in_dx").reshape(B, n, d)
    gw["w_ffn_in"] = _mm(sv["h2_t"], dgu, out_dtype=BF16, tn=FF_TILE, out_block=_natural_block, name=f"{tag}_ffn_in_dw",
                         plus=plus.get("w_ffn_in"))
    dx, dy, dg1, gw["n2"], dsc2, dsh2 = _norm_bwd(f"{tag}_norm2", sv["st2"], wl["n2"], sc2, sh2, dx1, dh2)
    return (dx, dy, dg1), gw, dict(sh2=dsh2, sc2=dsc2)


def _mix_half_bwd(tag, sv, mods, wl, dx, dy, dkv_in, *, local, kv_only, exch=None, plus=None):
    sh1, sc1, _ = mods[:3]
    B, n, d = sv["st"][0].shape
    gw = {}
    if kv_only:
        dkv = _flat(dkv_in).astype(BF16)
        dh = _mm(dkv, wl["w_in"][:, ATTN_W:ATTN_W + 2 * KV_W], trans_b=True, name=f"{tag}_kv_dx").reshape(B, n, d)
        gw["w_in_kv"] = _mm(sv["h_t"], dkv, out_dtype=BF16, name=f"{tag}_kv_dw")
        dxb, dy_prev, dgate_prev, gw["n1"], dsc1, dsh1 = _norm_bwd(f"{tag}_norm1", sv["st"], wl["n1"], sc1, sh1,
                                                                    jnp.zeros((B, n, d), F32), dh)
        return (dxb, dy_prev, dgate_prev), gw, dict(sh1=dsh1, sc1=dsc1), None, []

    dyf = _flat(dy)
    dmix = _mm(dyf, wl["w_out"], trans_b=True, name=f"{tag}_out_dx", tn=D_MODEL).reshape(B, n, d)
    plus = plus or {}
    gw["w_out"] = _mm(_flat(sv["mix"]).T, dyf, out_dtype=BF16, tn=DW_TN, name=f"{tag}_out_dw", plus=plus.get("w_out"))
    (dq, dk, dv, da, dg, dpu), (dkc, dvc), gw["margs"], got = _mixer_bwd(f"{tag}_mix", sv["u"], sv["kvc"], sv["conv_acc"],
                                                                     wl["margs"], dmix, local, exch)
    if local:
        dkv_out = jnp.concatenate([dkc, dvc], axis=-1)
    else:
        dk = dkc + dkv_in[:, :, :KV_W]
        dv = dvc + dkv_in[:, :, KV_W:]
        dkv_out = None
    du = _flat(jnp.concatenate([dq, dk, dv, da, dg, dpu], axis=-1).astype(BF16))
    dh = _mm(du, wl["w_in"], trans_b=True, name=f"{tag}_in_dx", tn=D_MODEL).reshape(B, n, d)
    gw["w_in"] = _mm(sv["h_t"], du, out_dtype=BF16, tn=DW_TN, name=f"{tag}_in_dw", plus=plus.get("w_in"))
    dxb, dy_prev, dgate_prev, gw["n1"], dsc1, dsh1 = _norm_bwd(f"{tag}_norm1", sv["st"], wl["n1"], sc1, sh1, dx, dh)
    return (dxb, dy_prev, dgate_prev), gw, dict(sh1=dsh1, sc1=dsc1), dkv_out, got


BIG_W = ("w_in", "w_out", "w_ffn_in", "w_ffn_out")


def _local_step(x, ctx, m_loc, m_ctx, p, final_g, target, big):
    B = x.shape[0]
    depth = m_loc.shape[0]
    lat_mods = [[t[:, None, :] for t in jnp.split(m_loc[l], 6, axis=-1)] for l in range(depth)]
    ctx_mods = [[jnp.broadcast_to(t[None, None, :], (B, 1, D_MODEL)) for t in jnp.split(m_ctx[l], 6)] for l in range(depth)]

    st, cst = (x, None, None), (ctx, None, None)
    w_mix, w_ffn, sv_mix, sv_ffn, csv_mix, csv_ffn = [], [], [], [], [], []
    *normed, = _norm_fwd("l0_norm1", st, p["norm1_g"][0][None, :], lat_mods[0][1], lat_mods[0][0], exch=big.ride_first())
    normed, got = normed[:3], normed[3:]
    for l in range(depth):
        last = l == depth - 1
        wb = big.mix_weights(l, got)
        wm = dict(n1=p["norm1_g"][l][None, :], w_in=wb["w_in"], w_out=wb["w_out"],
                  margs=(p["attn_sink"][l][None, :], p["conv_dw"][l], p["conv_dw_b"][l][None, :], p["conv_ln_g"][l][None, :],
                         p["conv_ln_b"][l][None, :], _block_diag(p["pool_w"][l]), p["pool_scale"][l][None, :]))
        cst, csv, _ = _mix_half_fwd(f"l{l}c", cst, ctx_mods[l], wm, None, local=False, kv_only=last)
        st, sv, got = _mix_half_fwd(f"l{l}", st, lat_mods[l], wm, csv["kvc"], local=True, kv_only=False,
                                    exch=big.ride_attn_fwd(l), normed=normed if l == 0 else None)
        w_mix.append(wm)
        sv_mix.append(sv)
        csv_mix.append(csv)
        wb = big.ffn_weights(l, got)
        w_ffn_in = _interleave_ffn(wb["w_ffn_in"], name=f"l{l}_ffn_in_interleave")
        wf = dict(n2=p["norm2_g"][l][None, :], w_ffn_in=w_ffn_in, w_ffn_out=wb["w_ffn_out"])
        csv = None
        if not last:
            cst, csv, _ = _ffn_half_fwd(f"l{l}c", cst, ctx_mods[l], wf)
        st, sv, got = _ffn_half_fwd(f"l{l}", st, lat_mods[l], wf, exch=big.ride_ffn_fwd(l))
        w_ffn.append(wf)
        sv_ffn.append(sv)
        csv_ffn.append(csv)
    loss_rows, dx, dy, dgate, dfinal = _loss_head(st, final_g[None, :], target, name="loss_head")

    dm_loc, dm_ctx = [None] * depth, [None] * depth
    small = [None] * depth
    cdx = cdy = cdgate = None
    up_mix = None
    for l in reversed(range(depth)):
        last = l == depth - 1
        dm, cdm = dict(g2=dgate), {}
        (dx, dy, dm["g1"]), gf, d = _ffn_half_bwd(f"l{l}", sv_ffn[l], lat_mods[l], w_ffn[l], dx, dy)
        dm.update(d)
        if not last:
            cdm["g2"] = cdgate
            (cdx, cdy, cdm["g1"]), cgf, d = _ffn_half_bwd(f"l{l}c", csv_ffn[l], ctx_mods[l], w_ffn[l], cdx, cdy, plus=gf)
            cdm.update(d)
            gf = dict(cgf, n2=gf["n2"] + cgf["n2"])
        ffn_grads = {k: gf[k] for k in _ShardedWeights.FFN}
        (dx, dy, dgate), gm, d, dkv, got = _mix_half_bwd(f"l{l}", sv_mix[l], lat_mods[l], w_mix[l], dx, dy, None, local=True,
                                                        kv_only=False, exch=big.ride_attn_bwd(l, ffn_grads, up_mix))
        big.took(l, ffn_grads, up_mix, got)
        dm.update(d)
        (cdx, cdy, cdgate), cgm, d, _, _ = _mix_half_bwd(f"l{l}c", csv_mix[l], ctx_mods[l], w_mix[l], cdx, cdy, dkv,
                                                        local=False, kv_only=last, plus=gm)
        cdm.update(d)
        order = ("sh1", "sc1", "g1", "sh2", "sc2", "g2")
        dm_loc[l] = jnp.concatenate([dm[k][:, 0, :] for k in order], axis=-1)
        dm_ctx[l] = jnp.concatenate([jnp.sum(cdm[k][:, 0, :], axis=0) if k in cdm else jnp.zeros((D_MODEL,), F32)
                                     for k in order])
        if last:
            up_mix = dict(w_in=gm["w_in"].at[:, ATTN_W:ATTN_W + 2 * KV_W].add(cgm["w_in_kv"]), w_out=gm["w_out"])
            margs = gm["margs"]
        else:
            up_mix = {k: cgm[k] for k in _ShardedWeights.MIX}
            margs = tuple(a + b for a, b in zip(gm["margs"], cgm["margs"]))
        small[l] = dict(n1=gm["n1"] + cgm["n1"], n2=gf["n2"], margs=margs)
    big.leftover(up_mix)

    stack = lambda f: jnp.stack([f(small[l]) for l in range(depth)])
    dp = dict(
        norm1_g=stack(lambda g: g["n1"][0]), norm2_g=stack(lambda g: g["n2"][0]),
        attn_sink=stack(lambda g: g["margs"][0][0]), conv_dw=stack(lambda g: g["margs"][1]),
        conv_dw_b=stack(lambda g: g["margs"][2][0]), conv_ln_g=stack(lambda g: g["margs"][3][0]),
        conv_ln_b=stack(lambda g: g["margs"][4][0]), pool_w=stack(lambda g: _diag_blocks(g["margs"][5])),
        pool_scale=stack(lambda g: g["margs"][6][0]))
    return jnp.sum(loss_rows[:, 0, 0]), dx, jnp.stack(dm_loc), jnp.stack(dm_ctx), dp, dfinal[0]


PACK_COLS = 1024


def _pack(arrs):
    flat = jnp.concatenate([a.reshape(-1).astype(F32) for a in arrs])
    rows = -(-flat.shape[0] // (8 * PACK_COLS)) * 8
    return jnp.pad(flat, (0, rows * PACK_COLS - flat.shape[0])).reshape(rows, PACK_COLS)


def _unpack(slab, like):
    flat = slab.reshape(-1)
    out, off = [], 0
    for a in like:
        out.append(flat[off:off + a.size].reshape(a.shape))
        off += a.size
    return out


def _shard_cols(gathered):
    _, L, R, C = gathered.shape
    return jnp.transpose(gathered, (1, 2, 0, 3)).reshape(L, R, N_DEV * C)


class _ShardedWeights:
    MIX = ("w_in", "w_out")
    FFN = ("w_ffn_in", "w_ffn_out")
    BY_COLS = ("w_in", "w_ffn_in")

    def __init__(self, shards):
        self.shards = shards
        self.depth = shards[BIG_W[0]].shape[0]
        self.parts = [dict() for _ in range(self.depth)]
        self.left = None

    def _join(self, names, blocks):
        out = {}
        for name, g in zip(names, blocks):
            _, R, C = g.shape
            out[name] = jnp.transpose(g, (1, 0, 2)).reshape(R, N_DEV * C) if name in self.BY_COLS else g.reshape(N_DEV * R, C)
        return out

    def cut(self, names, grads):
        out = []
        for name in names:
            g = grads[name]
            if name in self.BY_COLS:
                R, C8 = g.shape
                out.append(jnp.transpose(g.reshape(R, N_DEV, C8 // N_DEV), (1, 0, 2)))
            else:
                R8, C = g.shape
                out.append(g.reshape(N_DEV, R8 // N_DEV, C))
        return out

    def ride_first(self):
        return [self.shards[name][0] for name in self.MIX], False

    def mix_weights(self, l, got):
        return self._join(self.MIX, got)

    def ffn_weights(self, l, got):
        return self._join(self.FFN, got)

    def ride_attn_fwd(self, l):
        return [self.shards[name][l] for name in self.FFN], False

    def ride_ffn_fwd(self, l):
        if l + 1 >= self.depth:
            return None
        return [self.shards[name][l + 1] for name in self.MIX], False

    def ride_attn_bwd(self, l, ffn_grads, up_mix):
        return self.cut(self.FFN, ffn_grads) + (self.cut(self.MIX, up_mix) if up_mix is not None else []), True

    def took(self, l, ffn_grads, up_mix, got):
        self.parts[l].update(zip(self.FFN, got[:2]))
        if up_mix is not None:
            self.parts[l + 1].update(zip(self.MIX, got[2:]))

    def leftover(self, mix_grads):
        self.left = mix_grads


def _as_rows(a, leading=0):
    return a.reshape(*a.shape[:leading], -1, PACK_COLS)


SMALL = ("c_ctx", "b_mod", "norm1_g", "norm2_g", "conv_dw_b", "conv_ln_g", "conv_ln_b", "attn_sink", "pool_w",
         "pool_scale", "final_g", "conv_dw")
BIG = ("w_mod", "w_in", "w_out", "w_ffn_in", "w_ffn_out")
ORDER = ("c_ctx", "w_mod", "b_mod", "norm1_g", "norm2_g", "w_in", "conv_dw", "conv_dw_b", "conv_ln_g", "conv_ln_b",
         "attn_sink", "pool_w", "pool_scale", "w_out", "w_ffn_in", "w_ffn_out", "final_g")


def kernel(x, c, ctx, c_ctx, w_mod, b_mod, norm1_g, norm2_g, w_in, conv_dw, conv_dw_b, conv_ln_g, conv_ln_b, attn_sink, pool_w, pool_scale, w_out, w_ffn_in, w_ffn_out, final_g, loss_target, m_c_ctx, m_w_mod, m_b_mod, m_norm1_g, m_norm2_g, m_w_in, m_conv_dw, m_conv_dw_b, m_conv_ln_g, m_conv_ln_b, m_attn_sink, m_pool_w, m_pool_scale, m_w_out, m_w_ffn_in, m_w_ffn_out, m_final_g, v_c_ctx, v_w_mod, v_b_mod, v_norm1_g, v_norm2_g, v_w_in, v_conv_dw, v_conv_dw_b, v_conv_ln_g, v_conv_ln_b, v_attn_sink, v_pool_w, v_pool_scale, v_w_out, v_w_ffn_in, v_w_ffn_out, v_final_g):
    w = dict(c_ctx=c_ctx, w_mod=w_mod, b_mod=b_mod, norm1_g=norm1_g, norm2_g=norm2_g, w_in=w_in, conv_dw=conv_dw,
             conv_dw_b=conv_dw_b, conv_ln_g=conv_ln_g, conv_ln_b=conv_ln_b, attn_sink=attn_sink, pool_w=pool_w,
             pool_scale=pool_scale, w_out=w_out, w_ffn_in=w_ffn_in, w_ffn_out=w_ffn_out, final_g=final_g)
    mom = dict(c_ctx=m_c_ctx, w_mod=m_w_mod, b_mod=m_b_mod, norm1_g=m_norm1_g, norm2_g=m_norm2_g, w_in=m_w_in,
               conv_dw=m_conv_dw, conv_dw_b=m_conv_dw_b, conv_ln_g=m_conv_ln_g, conv_ln_b=m_conv_ln_b,
               attn_sink=m_attn_sink, pool_w=m_pool_w, pool_scale=m_pool_scale, w_out=m_w_out, w_ffn_in=m_w_ffn_in,
               w_ffn_out=m_w_ffn_out, final_g=m_final_g)
    var = dict(c_ctx=v_c_ctx, w_mod=v_w_mod, b_mod=v_b_mod, norm1_g=v_norm1_g, norm2_g=v_norm2_g, w_in=v_w_in,
               conv_dw=v_conv_dw, conv_dw_b=v_conv_dw_b, conv_ln_g=v_conv_ln_g, conv_ln_b=v_conv_ln_b,
               attn_sink=v_attn_sink, pool_w=v_pool_w, pool_scale=v_pool_scale, w_out=v_w_out, w_ffn_in=v_w_ffn_in,
               w_ffn_out=v_w_ffn_out, final_g=v_final_g)
    B = x.shape[0]
    depth = w_mod.shape[0]
    mod_cols = w_mod.shape[2]
    dw_cols = conv_dw.shape[2]
    me = 4 * lax.axis_index("x") + 2 * lax.axis_index("y") + lax.axis_index("c")

    shards = {name: w[name].astype(BF16) for name in BIG_W}
    c_all, dw_all = _exchange([c, conv_dw], scatter=False, name="gather_first")
    big = _ShardedWeights(shards)
    p = dict(norm1_g=norm1_g, norm2_g=norm2_g, conv_dw=_shard_cols(dw_all), conv_dw_b=conv_dw_b, conv_ln_g=conv_ln_g,
             conv_ln_b=conv_ln_b, attn_sink=attn_sink, pool_w=pool_w, pool_scale=pool_scale)

    cc = jnp.concatenate([c_all.reshape(N_DEV * B, D_MODEL), jnp.broadcast_to(c_ctx[None, :], (N_DEV, D_MODEL)),
                          jnp.zeros((MOD_ROWS - N_DEV * B - N_DEV, D_MODEL), F32)], axis=0)
    b_shard = lax.dynamic_slice_in_dim(b_mod, me * mod_cols, mod_cols, axis=1)[:, None, :]
    m_part = _mod_fwd(cc, w_mod, b_shard, name="mod_fwd")
    m_all, = _exchange([m_part], scatter=False, name="gather_mod")
    m_full = _shard_cols(m_all)
    m_loc = lax.dynamic_slice_in_dim(m_full, me * B, B, axis=1)
    m_ctx = m_full[:, N_DEV * B, :]

    loss_part, dx, dm_loc, dm_ctx, dp, dfinal = _local_step(x, ctx, m_loc, m_ctx, p, final_g, loss_target, big)
    loss = lax.psum(loss_part, AXES)

    dm_rows = jnp.concatenate([dm_loc, dm_ctx[:, None, :], jnp.zeros((depth, 8 - B - 1, 6 * D_MODEL), F32)], axis=1)
    small_like = [norm1_g, norm2_g, conv_dw_b, conv_ln_g, conv_ln_b, attn_sink, pool_w, pool_scale, final_g, dp["conv_dw"]]
    small_part = _pack([dp["norm1_g"], dp["norm2_g"], dp["conv_dw_b"], dp["conv_ln_g"], dp["conv_ln_b"], dp["attn_sink"],
                        dp["pool_w"], dp["pool_scale"], dfinal, dp["conv_dw"]])
    g, delta, new_m, new_v = {}, {}, {}, {}
    tail_rides = (([dm_rows, small_part], False), (big.cut(big.MIX, big.left), True))
    for name, ride in zip(big.FFN, tail_rides):
        g[name], delta[name], new_m[name], new_v[name], *got = _adamw(
            [big.parts[l][name] for l in range(depth)], w[name], mom[name], var[name], name=f"adamw_{name}", exch=ride)
        if name == big.FFN[0]:
            dm_all, small_all = got
        else:
            big.parts[0].update(zip(big.MIX, got))
    dm_full = jnp.concatenate([
        jnp.transpose(dm_all[:, :, :B, :], (1, 0, 2, 3)).reshape(depth, N_DEV * B, 6 * D_MODEL),
        jnp.transpose(dm_all[:, :, B, :], (1, 0, 2)),
        jnp.zeros((depth, MOD_ROWS - N_DEV * B - N_DEV, 6 * D_MODEL), F32)], axis=1)
    g_b_mod = jnp.stack([_sum_leading(dm_full[l][:, None, :], name=f"b_mod_grad{l}")[0] for l in range(depth)])
    dm_mine = lax.dynamic_slice_in_dim(dm_full, me * mod_cols, mod_cols, axis=2)
    dcc, g_w_mod = _mod_bwd(cc, w_mod, b_shard, dm_mine, name="mod_bwd")
    g_c_ctx_part = jnp.sum(dcc[N_DEV * B:N_DEV * B + N_DEV], axis=0)

    small_sum = _unpack(_sum_leading(small_all, name="sum_small"), small_like)
    g.update(zip(("norm1_g", "norm2_g", "conv_dw_b", "conv_ln_g", "conv_ln_b", "attn_sink", "pool_w", "pool_scale",
                  "final_g"), small_sum[:-1]))
    g["b_mod"] = g_b_mod
    g["conv_dw"] = lax.dynamic_slice_in_dim(small_sum[-1], me * dw_cols, dw_cols, axis=2)
    c_ctx_all, = _exchange([g_c_ctx_part.reshape(8, D_MODEL // 8)], scatter=False, name="gather_c_ctx")
    g["c_ctx"] = _sum_leading(c_ctx_all, name="sum_c_ctx").reshape(D_MODEL)

    for name in big.MIX:
        g[name], delta[name], new_m[name], new_v[name] = _adamw(
            [big.parts[l][name] for l in range(depth)], w[name], mom[name], var[name], name=f"adamw_{name}")
    g["w_mod"], delta["w_mod"], new_m["w_mod"], new_v["w_mod"] = _adamw(g_w_mod, w_mod, m_w_mod, v_w_mod, name="adamw_w_mod")
    res = _adamw_small([g[k] for k in SMALL], [w[k] for k in SMALL], [mom[k] for k in SMALL], [var[k] for k in SMALL],
                       name="adamw_small")
    for dst, arrs in zip((delta, new_m, new_v), res):
        dst.update(zip(SMALL, arrs))

    return (loss, dx, *[g[k] for k in ORDER], *[delta[k] for k in ORDER], *[new_m[k] for k in ORDER],
            *[new_v[k] for k in ORDER])
```

```python
import functools

import numpy as np
import jax
import jax.numpy as jnp
from jax import lax
from jax.experimental import pallas as pl
from jax.experimental.pallas import tpu as pltpu

F32 = jnp.float32
BF16 = jnp.bfloat16

D_MODEL = 1024
GRID_W = 64
HEAD_DIM = 64
ATTN_W = 512
CONV_W = 256
POOL_W = 256
ATTN_HEADS = 8
KV_HEADS = 2
GROUP = ATTN_HEADS // KV_HEADS
KV_W = KV_HEADS * HEAD_DIM
IN_W = ATTN_W + 2 * KV_W + 2 * CONV_W + POOL_W
WINDOW = 128
Q_BLOCK = 128
SPAN = Q_BLOCK + 2 * WINDOW
CONV_KERNEL = 31
POOL_WINDOWS = (2, 4, 8, 16)
POOL_GROUP = 64
ROPE_BASE = 10000.0
D_FF = 2816
EPS = 1e-6
NEG = -1e30
N_DEV = 8
AXES = ("x", "y", "c")

ADAM_LR = 0.001
ADAM_B1 = 0.9
ADAM_B2 = 0.999
ADAM_EPS = 1e-08
ADAM_WD = 0.01
ADAM_STEP = 10

VMEM_LIMIT = 56 * 1024 * 1024
HALO = 16
SEQ_CHUNK = 1024
CONV_BWD_CHUNK = 512


def _params(*sem):
    return pltpu.CompilerParams(dimension_semantics=sem, vmem_limit_bytes=VMEM_LIMIT)


def _tile(dim, target):
    if dim <= target:
        return dim
    t = (target // 128) * 128
    while t >= 128:
        if dim % t == 0:
            return t
        t -= 128
    raise ValueError(f"no tile for {dim}")


MM_VMEM_BUDGET = 44 * 1024 * 1024
DW_TN = 256


def _dot(a, b):
    return lax.dot_general(a.astype(BF16), b.astype(BF16), (((1,), (0,)), ((), ())), preferred_element_type=F32)


def _mm_vmem(tm, tn, tk, whole, a_bytes, b_bytes, o_bytes):
    return 2 * (tm * tk * a_bytes + tk * tn * b_bytes + tm * tn * o_bytes) + (0 if whole else tm * tn * 4)


def _mm(a, b, *, name, out_dtype=F32, tm=1408, tn=512, trans_b=False, out_block=None, plus=None):
    M, K = a.shape
    N, K2 = b.shape if trans_b else b.shape[::-1]
    assert K == K2, (a.shape, b.shape)
    tm = _tile(M, tm)
    tn = _tile(N, tn)
    sizes = (a.dtype.itemsize, b.dtype.itemsize, jnp.dtype(out_dtype).itemsize)
    tk = next(t for t in range(K, 0, -128) if K % t == 0 and _mm_vmem(tm, tn, t, t == K, *sizes) <= MM_VMEM_BUDGET)
    nk = K // tk

    def body(a_ref, b_ref, *rest):
        plus_ref = rest[0] if plus is not None else None
        o_ref, *scratch = rest[1:] if plus is not None else rest
        part = (_dot_nt if trans_b else _dot)(a_ref[...], b_ref[...])

        def store(total):
            if plus_ref is not None:
                total = total + plus_ref[...].astype(F32)
            o_ref[...] = total.astype(o_ref.dtype)

        if nk == 1:
            store(part)
        else:
            acc_ref, = scratch
            k = pl.program_id(2)

            @pl.when(k == 0)
            def _():
                acc_ref[...] = part

            @pl.when(k > 0)
            def _():
                acc_ref[...] += part

            @pl.when(k == nk - 1)
            def _():
                store(acc_ref[...])

    out_spec = pl.BlockSpec((tm, tn), (lambda i, j, k: (i, j)) if out_block is None else (lambda i, j, k: (i, out_block(j))))
    return pl.pallas_call(
        body, name=name, grid=(M // tm, N // tn, nk),
        in_specs=[pl.BlockSpec((tm, tk), lambda i, j, k: (i, k)),
                  pl.BlockSpec((tn, tk), lambda i, j, k: (j, k)) if trans_b else pl.BlockSpec((tk, tn), lambda i, j, k: (k, j))]
        + ([out_spec] if plus is not None else []),
        out_specs=out_spec,
        out_shape=jax.ShapeDtypeStruct((M, N), out_dtype),
        scratch_shapes=[pltpu.VMEM((tm, tn), F32)] if nk > 1 else [],
        compiler_params=_params("parallel", "parallel", "arbitrary"),
    )(a, b, *([plus] if plus is not None else []))


FF_TILE = 256


FF_TILES = D_FF // FF_TILE


def _natural_block(j):
    return (j % 2) * FF_TILES + j // 2


def _interleave_ffn(w, *, name):
    R, C = w.shape

    def body(w_ref, o_ref):
        o_ref[...] = w_ref[...]

    return pl.pallas_call(
        body, name=name, grid=(2 * FF_TILES,),
        in_specs=[pl.BlockSpec((R, FF_TILE), lambda j: (0, _natural_block(j)))],
        out_specs=pl.BlockSpec((R, FF_TILE), lambda j: (0, j)),
        out_shape=jax.ShapeDtypeStruct((R, C), w.dtype),
        compiler_params=_params("parallel"),
    )(w)


def _swiglu(gu):
    g, u = gu[:, :FF_TILE], gu[:, FF_TILE:]
    return g * jax.nn.sigmoid(g) * u


EPILOGUE_SPLIT = 1
FF_ROWS = 4096


def _mm_swiglu(a, w_il, *, name, tm=FF_ROWS, exch=None, split=EPILOGUE_SPLIT):
    M, K = a.shape
    tm = _tile(M, tm)
    rc = tm // split

    def body(a_ref, b_ref, gu_ref, act_ref, act_t_ref):
        b = b_ref[...]
        parts = [_dot(a_ref[r0:r0 + rc, :], b) for r0 in range(0, tm, rc)]
        for r0, gu in zip(range(0, tm, rc), parts):
            gu_ref[r0:r0 + rc, :] = gu.astype(gu_ref.dtype)
            act = _swiglu(gu)
            act_ref[r0:r0 + rc, :] = act.astype(act_ref.dtype)
            act_t_ref[:, r0:r0 + rc] = act.T.astype(act_t_ref.dtype)

    grid = (M // tm, D_FF // FF_TILE)
    body, x_in, x_out, x_shapes, x_sems = _riding(body, exch, 2, 3, grid)
    return pl.pallas_call(
        body, name=name, grid=grid,
        in_specs=[pl.BlockSpec((tm, K), lambda i, j: (i, 0)), pl.BlockSpec((K, 2 * FF_TILE), lambda i, j: (0, j))] + x_in,
        out_specs=[pl.BlockSpec((tm, 2 * FF_TILE), lambda i, j: (i, j)), pl.BlockSpec((tm, FF_TILE), lambda i, j: (i, j)),
                   pl.BlockSpec((FF_TILE, tm), lambda i, j: (j, i))] + x_out,
        out_shape=[jax.ShapeDtypeStruct((M, 2 * D_FF), BF16), jax.ShapeDtypeStruct((M, D_FF), BF16),
                   jax.ShapeDtypeStruct((D_FF, M), BF16)] + x_shapes,
        scratch_shapes=x_sems,
        compiler_params=_params("arbitrary", "arbitrary") if exch else _params("parallel", "parallel"),
    )(a, w_il, *(exch[0] if exch else []))


def _mm_dswiglu(dy, w_out, gu, *, name, tm=FF_ROWS, split=EPILOGUE_SPLIT):
    M, K = dy.shape
    tm = _tile(M, tm)
    rc = tm // split

    def body(dy_ref, b_ref, gu_ref, o_ref):
        b = b_ref[...]
        parts = [_dot_nt(dy_ref[r0:r0 + rc, :], b) for r0 in range(0, tm, rc)]
        for r0, dact in zip(range(0, tm, rc), parts):
            g = gu_ref[r0:r0 + rc, :FF_TILE].astype(F32)
            u = gu_ref[r0:r0 + rc, FF_TILE:].astype(F32)
            sig = jax.nn.sigmoid(g)
            silu = g * sig
            o_ref[r0:r0 + rc, :FF_TILE] = (dact * u * (sig + silu * (1.0 - sig))).astype(o_ref.dtype)
            o_ref[r0:r0 + rc, FF_TILE:] = (dact * silu).astype(o_ref.dtype)

    return pl.pallas_call(
        body, name=name, grid=(M // tm, D_FF // FF_TILE),
        in_specs=[pl.BlockSpec((tm, K), lambda i, j: (i, 0)), pl.BlockSpec((FF_TILE, K), lambda i, j: (j, 0)),
                  pl.BlockSpec((tm, 2 * FF_TILE), lambda i, j: (i, j))],
        out_specs=pl.BlockSpec((tm, 2 * FF_TILE), lambda i, j: (i, j)),
        out_shape=jax.ShapeDtypeStruct((M, 2 * D_FF), BF16),
        compiler_params=_params("parallel", "parallel"),
    )(dy, w_out, gu)


def _rope_tables(n):
    rows = n // GRID_W
    row = jnp.repeat(jnp.arange(rows), GRID_W).astype(F32)
    col = jnp.tile(jnp.arange(GRID_W), rows).astype(F32)
    half = HEAD_DIM // 2
    inv = ROPE_BASE ** (-jnp.arange(0, half, 2, dtype=F32) / half)
    ar = row[:, None] * inv
    ac = col[:, None] * inv
    ang = jnp.concatenate([ar, ar, ac, ac], axis=-1)
    return jnp.cos(ang), jnp.sin(ang)


def _rot_half(x):
    w = x.shape[-1]
    lane = lax.broadcasted_iota(jnp.int32, x.shape, 1)
    up = pltpu.roll(x, w - 16, 1)
    down = pltpu.roll(x, 16, 1)
    return jnp.where((lane & 16) == 0, -up, down)


def _rope(x, cos, sin):
    return x * cos + _rot_half(x) * sin


def _rope_bwd(d, cos, sin):
    return d * cos - _rot_half(d * sin)


def _dot_nt(a, b):
    return lax.dot_general(a.astype(BF16), b.astype(BF16), (((1,), (1,)), ((), ())), preferred_element_type=F32)


def _softmax_sink(s, sink_rows):
    mx = jnp.maximum(jnp.max(s, axis=1, keepdims=True), sink_rows)
    e = jnp.exp(s - mx)
    es = jnp.exp(sink_rows - mx)
    inv = 1.0 / (jnp.sum(e, axis=1, keepdims=True) + es)
    return e * inv, es * inv


def _attn_operands(q_ref, k_ref, v_ref, kc_ref, vc_ref, cq_ref, sq_ref, ck_ref, sk_ref, bias_ref, i, n, n_ctx, local):
    q = q_ref[...]
    k_all, v_all, bias, s0, ck, sk = kc_ref[...], vc_ref[...], None, None, None, None
    if local:
        start, s0 = _span_start(i, n)
        ck = ck_ref[pl.ds(s0, SPAN), :]
        sk = sk_ref[pl.ds(s0, SPAN), :]
        q = _rope(q, cq_ref[...], sq_ref[...])
        k_all = jnp.concatenate([k_all, _rope(k_ref[pl.ds(s0, SPAN), :], ck, sk)], axis=0)
        v_all = jnp.concatenate([v_all, v_ref[pl.ds(s0, SPAN), :]], axis=0)
        bias = jnp.concatenate([bias_ref[(start - s0) // Q_BLOCK]] * GROUP, axis=0)
    q = (q * (HEAD_DIM ** -0.5)).astype(BF16)
    return q, k_all.astype(BF16), v_all.astype(BF16), bias, s0, ck, sk


def _stack_heads(x, kh):
    return jnp.concatenate([x[:, (GROUP * kh + g) * HEAD_DIM:(GROUP * kh + g + 1) * HEAD_DIM] for g in range(GROUP)], axis=0)


def _sink_rows(sink, kh):
    return jnp.concatenate([jnp.broadcast_to(sink[:, GROUP * kh + g:GROUP * kh + g + 1], (Q_BLOCK, 1)) for g in range(GROUP)], axis=0)


def _window_bias_table(n_ctx, local):
    if not local:
        return jnp.zeros((3, 8, 128), F32)
    off = (jnp.arange(3, dtype=jnp.int32) * Q_BLOCK)[:, None, None]
    r = jnp.arange(Q_BLOCK, dtype=jnp.int32)[None, :, None]
    c = jnp.arange(n_ctx + SPAN, dtype=jnp.int32)[None, None, :]
    ok = (c < n_ctx) | (jnp.abs(off + r - (c - n_ctx)) <= WINDOW)
    return jnp.where(ok, 0.0, NEG).astype(F32)


def _span_start(i, n):
    start = i * Q_BLOCK
    s0 = jnp.clip(start - WINDOW, 0, n - SPAN)
    return start, pl.multiple_of(s0, Q_BLOCK)


def _riding(body, exch, n_in, n_out, grid):
    if exch is None:
        return body, [], [], [], []
    arrs, scatter = exch
    k = len(arrs)

    def wrapped(*refs):
        ins, xin = refs[:n_in], refs[n_in:n_in + k]
        outs, xout = refs[n_in + k:n_in + k + n_out], refs[n_in + k + n_out:n_in + 2 * k + n_out]
        sems = refs[n_in + 2 * k + n_out:]
        b, i = pl.program_id(0), pl.program_id(1)

        @pl.when((b == 0) & (i == 0))
        def _():
            _exch_start(xin, xout, sems, scatter)

        body(*ins, *outs)

        @pl.when((b == grid[0] - 1) & (i == grid[1] - 1))
        def _():
            _exch_wait(xin, xout, sems, scatter)

    any_spec = pl.BlockSpec(memory_space=pl.ANY)
    return wrapped, [any_spec] * k, [any_spec] * k, _exch_out_shapes(arrs, scatter), _exch_sems(k)


def _attn_fwd(u, kvc, sink, cos, sin, *, local, name, exch=None):
    B, n, _ = u.shape
    n_ctx = kvc.shape[1]
    nb = n // Q_BLOCK
    assert (not local) or n >= SPAN

    def body(q_ref, k_ref, v_ref, kc_ref, vc_ref, sink_ref, cq_ref, sq_ref, ck_ref, sk_ref, bias_ref, o_ref):
        q, k_all, v_all, bias, _, _, _ = _attn_operands(q_ref, k_ref, v_ref, kc_ref, vc_ref, cq_ref, sq_ref, ck_ref, sk_ref,
                                                        bias_ref, pl.program_id(1), n, n_ctx, local)
        sink_v = sink_ref[...]
        sl = lambda kh: slice(kh * HEAD_DIM, (kh + 1) * HEAD_DIM)
        ss = [_dot_nt(_stack_heads(q, kh), k_all[:, sl(kh)]) for kh in range(KV_HEADS)]
        ps = [_softmax_sink(s if bias is None else s + bias, _sink_rows(sink_v, kh))[0].astype(BF16) for kh, s in enumerate(ss)]
        for kh, p in enumerate(ps):
            o = _dot(p, v_all[:, sl(kh)])
            for g in range(GROUP):
                h = GROUP * kh + g
                o_ref[:, h * HEAD_DIM:(h + 1) * HEAD_DIM] = o[g * Q_BLOCK:(g + 1) * Q_BLOCK, :].astype(o_ref.dtype)

    seq = lambda blk: pl.BlockSpec((None, n, KV_W), lambda b, i: (b, 0, blk))
    ctxs = lambda blk: pl.BlockSpec((None, n_ctx, KV_W), lambda b, i: (b, 0, blk))
    full = lambda a: pl.BlockSpec(a.shape, lambda b, i: (0,) * a.ndim)
    cos_q, sin_q = jnp.tile(cos, (1, ATTN_HEADS)), jnp.tile(sin, (1, ATTN_HEADS))
    cos_k, sin_k = jnp.tile(cos, (1, KV_HEADS)), jnp.tile(sin, (1, KV_HEADS))
    bias_tab = _window_bias_table(n_ctx, local)
    body, x_in, x_out, x_shapes, x_sems = _riding(body, exch, 11, 1, (B, nb))
    return pl.pallas_call(
        body, name=name, grid=(B, nb),
        in_specs=[pl.BlockSpec((None, Q_BLOCK, ATTN_W), lambda b, i: (b, i, 0)),
                  seq(ATTN_W // KV_W), seq(ATTN_W // KV_W + 1), ctxs(0), ctxs(1), full(sink),
                  pl.BlockSpec((Q_BLOCK, ATTN_W), lambda b, i: (i, 0)), pl.BlockSpec((Q_BLOCK, ATTN_W), lambda b, i: (i, 0)),
                  full(cos_k), full(sin_k), full(bias_tab)] + x_in,
        out_specs=[pl.BlockSpec((None, Q_BLOCK, ATTN_W), lambda b, i: (b, i, 0))] + x_out,
        out_shape=[jax.ShapeDtypeStruct((B, n, ATTN_W), BF16)] + x_shapes,
        scratch_shapes=x_sems,
        compiler_params=_params("arbitrary", "arbitrary"),
    )(u, u, u, kvc, kvc, sink, cos_q, sin_q, cos_k, sin_k, bias_tab, *(exch[0] if exch else []))


def _attn_bwd(u, kvc, sink, cos, sin, do_src, do_blk, *, local, name, exch=None):
    B, n, _ = u.shape
    n_ctx = kvc.shape[1]
    nb = n // Q_BLOCK

    def body(q_ref, k_ref, v_ref, kc_ref, vc_ref, sink_ref, cq_ref, sq_ref, ck_ref, sk_ref, bias_ref, do_ref,
             dq_ref, dk_ref, dv_ref, dkc_ref, dvc_ref, dsink_ref):
        b = pl.program_id(0)
        i = pl.program_id(1)

        @pl.when(i == 0)
        def _():
            dk_ref[...] = jnp.zeros_like(dk_ref)
            dv_ref[...] = jnp.zeros_like(dv_ref)
            dkc_ref[...] = jnp.zeros_like(dkc_ref)
            dvc_ref[...] = jnp.zeros_like(dvc_ref)

        @pl.when((i == 0) & (b == 0))
        def _():
            dsink_ref[...] = jnp.zeros_like(dsink_ref)

        q, k_all, v_all, bias, s0, ck, sk = _attn_operands(q_ref, k_ref, v_ref, kc_ref, vc_ref, cq_ref, sq_ref, ck_ref, sk_ref,
                                                           bias_ref, i, n, n_ctx, local)
        do = do_ref[...].astype(BF16)
        sink_v = sink_ref[...]
        sl = lambda kh: slice(kh * HEAD_DIM, (kh + 1) * HEAD_DIM)
        heads = range(KV_HEADS)
        q_st = [_stack_heads(q, kh) for kh in heads]
        do_st = [_stack_heads(do, kh) for kh in heads]
        ss = [_dot_nt(q_st[kh], k_all[:, sl(kh)]) for kh in heads]
        dps = [_dot_nt(do_st[kh], v_all[:, sl(kh)]) for kh in heads]
        p_bf, ds_bf = [], []
        dsink = jnp.zeros((1, ATTN_HEADS), F32)
        lane8 = lax.broadcasted_iota(jnp.int32, (1, ATTN_HEADS), 1)
        for kh in heads:
            p, p_sink = _softmax_sink(ss[kh] if bias is None else ss[kh] + bias, _sink_rows(sink_v, kh))
            delta = jnp.sum(p * dps[kh], axis=1, keepdims=True)
            ds = p * (dps[kh] - delta)
            dsr = -(p_sink * delta)
            for g in range(GROUP):
                dsink = dsink + jnp.where(lane8 == GROUP * kh + g, jnp.sum(dsr[g * Q_BLOCK:(g + 1) * Q_BLOCK, :]), 0.0)
            p_bf.append(p.astype(BF16))
            ds_bf.append(ds.astype(BF16))
        over_rows = (((0,), (0,)), ((), ()))
        dks, dvs = [], []
        for kh in heads:
            dq_st = _dot(ds_bf[kh], k_all[:, sl(kh)]) * (HEAD_DIM ** -0.5)
            for g in range(GROUP):
                h = GROUP * kh + g
                dq_ref[:, h * HEAD_DIM:(h + 1) * HEAD_DIM] = dq_st[g * Q_BLOCK:(g + 1) * Q_BLOCK, :]
            dvs.append(lax.dot_general(p_bf[kh], do_st[kh], over_rows, preferred_element_type=F32))
            dks.append(lax.dot_general(ds_bf[kh], q_st[kh], over_rows, preferred_element_type=F32))
        dk_cat = jnp.concatenate(dks, axis=1)
        dv_cat = jnp.concatenate(dvs, axis=1)
        dsink_ref[...] += dsink
        dkc_ref[...] += dk_cat[:n_ctx, :]
        dvc_ref[...] += dv_cat[:n_ctx, :]
        if local:
            dq_ref[...] = _rope_bwd(dq_ref[...], cq_ref[...], sq_ref[...])
            dk_ref[pl.ds(s0, SPAN), :] += _rope_bwd(dk_cat[n_ctx:, :], ck, sk)
            dv_ref[pl.ds(s0, SPAN), :] += dv_cat[n_ctx:, :]

    seq = lambda blk: pl.BlockSpec((None, n, KV_W), lambda b, i: (b, 0, blk))
    ctxs = lambda blk: pl.BlockSpec((None, n_ctx, KV_W), lambda b, i: (b, 0, blk))
    full = lambda a: pl.BlockSpec(a.shape, lambda b, i: (0,) * a.ndim)
    qblk = lambda blk: pl.BlockSpec((None, Q_BLOCK, ATTN_W), lambda b, i: (b, i, blk))
    cos_q, sin_q = jnp.tile(cos, (1, ATTN_HEADS)), jnp.tile(sin, (1, ATTN_HEADS))
    cos_k, sin_k = jnp.tile(cos, (1, KV_HEADS)), jnp.tile(sin, (1, KV_HEADS))
    acc = lambda rows: pl.BlockSpec((None, rows, KV_W), lambda b, i: (b, 0, 0))
    bias_tab = _window_bias_table(n_ctx, local)
    body, x_in, x_out, x_shapes, x_sems = _riding(body, exch, 12, 6, (B, nb))
    return pl.pallas_call(
        body, name=name, grid=(B, nb),
        in_specs=[qblk(0), seq(ATTN_W // KV_W), seq(ATTN_W // KV_W + 1), ctxs(0), ctxs(1), full(sink),
                  pl.BlockSpec((Q_BLOCK, ATTN_W), lambda b, i: (i, 0)), pl.BlockSpec((Q_BLOCK, ATTN_W), lambda b, i: (i, 0)),
                  full(cos_k), full(sin_k), full(bias_tab), qblk(do_blk)] + x_in,
        out_specs=[qblk(0), acc(n), acc(n), acc(n_ctx), acc(n_ctx), pl.BlockSpec((1, ATTN_HEADS), lambda b, i: (0, 0))] + x_out,
        out_shape=[jax.ShapeDtypeStruct((B, n, ATTN_W), F32), jax.ShapeDtypeStruct((B, n, KV_W), F32),
                   jax.ShapeDtypeStruct((B, n, KV_W), F32), jax.ShapeDtypeStruct((B, n_ctx, KV_W), F32),
                   jax.ShapeDtypeStruct((B, n_ctx, KV_W), F32), jax.ShapeDtypeStruct((1, ATTN_HEADS), F32)] + x_shapes,
        scratch_shapes=x_sems,
        compiler_params=_params("arbitrary", "arbitrary"),
    )(u, u, u, kvc, kvc, sink, cos_q, sin_q, cos_k, sin_k, bias_tab, do_src, *(exch[0] if exch else []))


def _conv_chunk(s, n, a_ext, g_ext, dw, dw_b, ln_g, ln_b):
    del s, n
    acc = _conv_taps(a_ext, g_ext, dw, dw_b)
    return _conv_tail(acc, ln_g, ln_b), acc


def _conv_chunk_bwd(s, n, ext, pars, acc, do):
    del s, n
    dw, dw_b, ln_g, ln_b = pars
    _, tail_vjp = jax.vjp(_conv_tail, acc, ln_g, ln_b)
    dacc, dln_g, dln_b = tail_vjp(do)
    _, taps_vjp = jax.vjp(_conv_taps, *ext, dw, dw_b)
    return (*taps_vjp(dacc), dln_g, dln_b)


def _conv_tail(acc, ln_g, ln_b):
    mu = jnp.mean(acc, axis=-1, keepdims=True)
    var = jnp.mean(jnp.square(acc - mu), axis=-1, keepdims=True)
    hn = (acc - mu) * lax.rsqrt(var + EPS) * ln_g + ln_b
    return hn * jax.nn.sigmoid(hn)


def _conv_taps(a_ext, g_ext, dw, dw_b):
    r = a_ext.shape[0] - 2 * HALO
    h = a_ext * jax.nn.sigmoid(g_ext)
    acc = jnp.broadcast_to(dw_b, (r, CONV_W))
    first = HALO - CONV_KERNEL // 2
    span = r + 8 * ((first + CONV_KERNEL - 1) // 8)
    shifted = [h[b:b + span, :] for b in range(8)]
    for k in range(CONV_KERNEL):
        o = first + k
        acc = acc + shifted[o % 8][o - o % 8:o - o % 8 + r, :] * dw[k:k + 1, :]
    return acc


def _pool_chunk(s, n, p_ext, w_bd, scale):
    r = p_ext.shape[0] - 2 * HALO
    lane = lax.broadcasted_iota(jnp.int32, (1, POOL_W), 1)
    win = jnp.left_shift(2, lane // POOL_GROUP)
    half = win // 2
    acc = jnp.zeros((r, POOL_W), F32)
    for d in range(-(POOL_WINDOWS[-1] // 2), POOL_WINDOWS[-1] - POOL_WINDOWS[-1] // 2):
        inside = (d >= -half) & (d <= win - 1 - half)
        acc = acc + jnp.where(inside, p_ext[HALO + d:HALO + d + r, :], 0.0)
    t = s + lax.broadcasted_iota(jnp.int32, (r, 1), 0)
    lo = jnp.maximum(t - half, 0)
    hi = jnp.minimum(t + win - 1 - half, n - 1)
    y = acc / (hi - lo + 1).astype(F32) - p_ext[HALO:HALO + r, :]
    out = lax.dot_general(y.astype(BF16), w_bd.astype(BF16), (((1,), (0,)), ((), ())), preferred_element_type=F32)
    return out * scale


def _seq_specs(rows, params):
    specs = [pl.BlockSpec((None, a.shape[1], w), functools.partial(lambda b, blk: (b, 0, blk), blk=blk)) for a, w, blk in rows]
    specs += [pl.BlockSpec(p.shape, functools.partial(lambda b, nd: (0,) * nd, nd=p.ndim)) for p in params]
    return specs


def _fill_padded(pad_ref, row_ref, n):
    w = pad_ref.shape[1]
    pad_ref[pl.ds(0, HALO), :] = jnp.zeros((HALO, w), F32)
    pad_ref[pl.ds(HALO + n, HALO), :] = jnp.zeros((HALO, w), F32)
    pad_ref[pl.ds(HALO, n), :] = row_ref[...]


def _seq_fwd(fn, rows, params, out_w, *, name, chunk=SEQ_CHUNK, aux_w=None):
    B, n = rows[0][0].shape[:2]
    r = min(chunk, n)
    nr, npar = len(rows), len(params)
    nout = 1 if aux_w is None else 2

    def body(*refs):
        row_refs, par_refs = refs[:nr], refs[nr:nr + npar]
        out_refs, pads = refs[nr + npar:nr + npar + nout], refs[nr + npar + nout:]
        for rr, p in zip(row_refs, pads):
            _fill_padded(p, rr, n)
        pars = [p[...] for p in par_refs]

        def chunk(ci, carry):
            s = pl.multiple_of(ci * r, r)
            ext = [p[pl.ds(s, r + 2 * HALO), :] for p in pads]
            res = fn(s, n, *ext, *pars)
            for o_ref, v in zip(out_refs, res if nout == 2 else (res,)):
                o_ref[pl.ds(s, r), :] = v.astype(o_ref.dtype)
            return carry

        lax.fori_loop(0, n // r, chunk, 0)

    widths = [(out_w, BF16)] + ([] if aux_w is None else [(aux_w, F32)])
    res = pl.pallas_call(
        body, name=name, grid=(B,),
        in_specs=_seq_specs(rows, params),
        out_specs=[pl.BlockSpec((None, n, w), lambda b: (b, 0, 0)) for w, _ in widths],
        out_shape=[jax.ShapeDtypeStruct((B, n, w), dt) for w, dt in widths],
        scratch_shapes=[pltpu.VMEM((n + 2 * HALO, w), F32) for _, w, _ in rows],
        compiler_params=_params("parallel"),
    )(*[a for a, _, _ in rows], *params)
    return res[0] if aux_w is None else res


def _seq_bwd(fn, rows, params, dout, *, name, chunk=SEQ_CHUNK, aux=None):
    B, n = rows[0][0].shape[:2]
    r = min(chunk, n)
    nr, npar = len(rows), len(params)
    naux = 0 if aux is None else 1

    def body(*refs):
        row_refs, par_refs, do_ref = refs[:nr], refs[nr:nr + npar], refs[nr + npar]
        aux_refs = refs[nr + npar + 1:nr + npar + 1 + naux]
        outs = refs[nr + npar + 1 + naux:]
        drow_refs, dpar_refs = outs[:nr], outs[nr:nr + npar]
        scratch = outs[nr + npar:]
        pads, dpads = scratch[:nr], scratch[nr:]
        for rr, p, dp in zip(row_refs, pads, dpads):
            _fill_padded(p, rr, n)
            dp[...] = jnp.zeros_like(dp)

        @pl.when(pl.program_id(0) == 0)
        def _():
            for d in dpar_refs:
                d[...] = jnp.zeros_like(d)

        pars = [p[...] for p in par_refs]

        def chunk(ci, carry):
            s = pl.multiple_of(ci * r, r)
            ext = [p[pl.ds(s, r + 2 * HALO), :] for p in pads]
            do = do_ref[pl.ds(s, r), :]
            if aux is None:
                _, vjp = jax.vjp(functools.partial(fn, s, n), *ext, *pars)
                grads = vjp(do)
            else:
                grads = fn(s, n, ext, pars, aux_refs[0][pl.ds(s, r), :], do)
            for dp, g in zip(dpads, grads[:nr]):
                dp[pl.ds(s, r + 2 * HALO), :] += g
            for d, g in zip(dpar_refs, grads[nr:]):
                d[...] += g
            return carry

        lax.fori_loop(0, n // r, chunk, 0)
        for d, dp in zip(drow_refs, dpads):
            d[...] = dp[pl.ds(HALO, n), :].astype(d.dtype)

    da, dw_, dblk = dout
    auxs = [] if aux is None else [aux]
    return pl.pallas_call(
        body, name=name, grid=(B,),
        in_specs=_seq_specs(rows, params) + [pl.BlockSpec((None, n, dw_), lambda b: (b, 0, dblk))]
        + [pl.BlockSpec((None, n, a.shape[2]), lambda b: (b, 0, 0)) for a in auxs],
        out_specs=[pl.BlockSpec((None, n, w), lambda b: (b, 0, 0)) for _, w, _ in rows]
        + [pl.BlockSpec(p.shape, functools.partial(lambda b, nd: (0,) * nd, nd=p.ndim)) for p in params],
        out_shape=[jax.ShapeDtypeStruct((B, n, w), BF16) for _, w, _ in rows]
        + [jax.ShapeDtypeStruct(p.shape, F32) for p in params],
        scratch_shapes=[pltpu.VMEM((n + 2 * HALO, w), F32) for _, w, _ in rows] * 2,
        compiler_params=_params("arbitrary"),
    )(*[a for a, _, _ in rows], *params, da, *auxs)


_CONV_A_BLK = (ATTN_W + 2 * KV_W) // CONV_W
_CONV_G_BLK = _CONV_A_BLK + 1
_POOL_BLK = _CONV_A_BLK + 2


def _mixer_fwd(tag, u, kvc, margs, local, exch=None):
    sink, dw, dw_b, ln_g, ln_b, w_bd, scale = margs
    cos, sin = _rope_tables(max(u.shape[1], GRID_W))
    attn, *got = _attn_fwd(u, kvc, sink, cos, sin, local=local, name=f"{tag}_attn_fwd", exch=exch)
    conv, conv_acc = _seq_fwd(_conv_chunk, [(u, CONV_W, _CONV_A_BLK), (u, CONV_W, _CONV_G_BLK)], [dw, dw_b, ln_g, ln_b],
                              CONV_W, name=f"{tag}_conv_fwd", aux_w=CONV_W)
    pool = _seq_fwd(_pool_chunk, [(u, POOL_W, _POOL_BLK)], [w_bd, scale], POOL_W, name=f"{tag}_pool_fwd")
    return jnp.concatenate([attn, conv, pool], axis=-1), conv_acc, got


def _mixer_bwd(tag, u, kvc, conv_acc, margs, dmix, local, exch=None):
    sink, dw, dw_b, ln_g, ln_b, w_bd, scale = margs
    cos, sin = _rope_tables(max(u.shape[1], GRID_W))
    dq, dk, dv, dkc, dvc, dsink, *got = _attn_bwd(u, kvc, sink, cos, sin, dmix, 0, local=local, name=f"{tag}_attn_bwd",
                                                  exch=exch)
    da, dg, ddw, ddw_b, dln_g, dln_b = _seq_bwd(
        _conv_chunk_bwd, [(u, CONV_W, _CONV_A_BLK), (u, CONV_W, _CONV_G_BLK)], [dw, dw_b, ln_g, ln_b],
        (dmix, CONV_W, ATTN_W // CONV_W), name=f"{tag}_conv_bwd", chunk=CONV_BWD_CHUNK, aux=conv_acc)
    dpu, dw_bd, dscale = _seq_bwd(_pool_chunk, [(u, POOL_W, _POOL_BLK)], [w_bd, scale],
                                  (dmix, POOL_W, (ATTN_W + CONV_W) // POOL_W), name=f"{tag}_pool_bwd")
    return (dq, dk, dv, da, dg, dpu), (dkc, dvc), (dsink, ddw, ddw_b, dln_g, dln_b, dw_bd, dscale), got


def _row_specs(arrs, kinds, tr):
    specs = []
    for a, kind in zip(arrs, kinds):
        if kind == "row":
            specs.append(pl.BlockSpec((None, tr, a.shape[2]), lambda b, j: (b, j, 0)))
        elif kind == "batch":
            specs.append(pl.BlockSpec((None, 1, a.shape[2]), lambda b, j: (b, 0, 0)))
        else:
            specs.append(pl.BlockSpec(a.shape, functools.partial(lambda b, j, nd: (0,) * nd, nd=a.ndim)))
    return specs


def _rowwise_fwd(fn, ins, kinds, outs, tr, *, name, transposed=None, exch=None):
    B, n = ins[0].shape[:2]
    ni, no = len(ins), len(outs)
    nj = n // tr

    def body(*refs):
        res = fn(*[r[...] for r in refs[:ni]])
        for o, v in zip(refs[ni:ni + no], res):
            o[...] = v.astype(o.dtype)
        if transposed is not None:
            refs[ni + no][...] = res[transposed].T.astype(refs[ni + no].dtype)

    out_specs = [pl.BlockSpec((None, tr, w), lambda b, j: (b, j, 0)) for w, _ in outs]
    out_shape = [jax.ShapeDtypeStruct((B, n, w), dt) for w, dt in outs]
    if transposed is not None:
        w, dt = outs[transposed]
        out_specs.append(pl.BlockSpec((w, tr), lambda b, j: (0, b * nj + j)))
        out_shape.append(jax.ShapeDtypeStruct((w, B * n), dt))
    body, x_in, x_out, x_shapes, x_sems = _riding(body, exch, ni, len(out_specs), (B, nj))
    return pl.pallas_call(
        body, name=name, grid=(B, nj),
        in_specs=_row_specs(ins, kinds, tr) + x_in, out_specs=out_specs + x_out, out_shape=out_shape + x_shapes,
        scratch_shapes=x_sems,
        compiler_params=_params("arbitrary", "arbitrary") if exch else _params("parallel", "parallel"),
    )(*ins, *(exch[0] if exch else []))


def _rowwise_bwd(fn, ins, kinds, gdtypes, cts, tr, *, name):
    B, n = ins[0].shape[:2]
    ni, nc = len(ins), len(cts)
    idx = list(range(ni))

    def body(*refs):
        in_refs, ct_refs, out_refs = refs[:ni], refs[ni:ni + nc], refs[ni + nc:]
        b, j = pl.program_id(0), pl.program_id(1)
        _, vjp = jax.vjp(fn, *[r[...].astype(F32) for r in in_refs])
        grads = vjp(tuple(c[...].astype(F32) for c in ct_refs))
        for o, i in zip(out_refs, idx):
            g = grads[i]
            if kinds[i] == "row":
                o[...] = g.astype(o.dtype)
            else:
                first = (j == 0) if kinds[i] == "batch" else ((j == 0) & (b == 0))

                @pl.when(first)
                def _(o=o, g=g):
                    o[...] = g

                @pl.when(jnp.logical_not(first))
                def _(o=o, g=g):
                    o[...] += g

    specs = _row_specs(ins, kinds, tr)
    return pl.pallas_call(
        body, name=name, grid=(B, n // tr),
        in_specs=specs + [pl.BlockSpec((None, tr, c.shape[2]), lambda b, j: (b, j, 0)) for c in cts],
        out_specs=[specs[i] for i in idx],
        out_shape=[jax.ShapeDtypeStruct(ins[i].shape, gdtypes[i]) for i in idx],
        compiler_params=_params("arbitrary", "arbitrary"),
    )(*ins, *cts)


ROW_TILE = 512


def _rms_mod(x, g, sc, sh):
    y = x * lax.rsqrt(jnp.mean(x * x, axis=-1, keepdims=True) + EPS)
    return (y * g) * (1.0 + sc) + sh


def _norm_tile(x, g, sc, sh):
    return x, _rms_mod(x, g, sc, sh)


def _res_norm_tile(xb, y, gate, g, sc, sh):
    x = xb + gate * y
    return x, _rms_mod(x, g, sc, sh)


_NORM_KINDS = ("row", "glob", "batch", "batch")
_RES_NORM_KINDS = ("row", "row", "batch", "glob", "batch", "batch")


def _norm_fwd(tag, st, g, sc, sh, exch=None):
    xb, y, gate = st
    tr = min(ROW_TILE, xb.shape[1])
    d = xb.shape[2]
    if y is None:
        return [xb, *_rowwise_fwd(lambda *a: (_rms_mod(*a),), [xb, g, sc, sh], _NORM_KINDS, [(d, BF16)], tr,
                                  name=f"{tag}_fwd", transposed=0, exch=exch)]
    return _rowwise_fwd(_res_norm_tile, [xb, y, gate, g, sc, sh], _RES_NORM_KINDS, [(d, F32), (d, BF16)], tr,
                        name=f"{tag}_fwd", transposed=1, exch=exch)


def _norm_bwd(tag, st, g, sc, sh, dx, dh):
    xb, y, gate = st
    tr = min(ROW_TILE, xb.shape[1])
    if y is None:
        dxb, dg, dsc, dsh = _rowwise_bwd(_norm_tile, [xb, g, sc, sh], _NORM_KINDS, [F32] * 4, [dx, dh], tr, name=f"{tag}_bwd")
        return dxb, None, None, dg, dsc, dsh
    return tuple(_rowwise_bwd(_res_norm_tile, [xb, y, gate, g, sc, sh], _RES_NORM_KINDS, [F32, BF16, F32, F32, F32, F32],
                              [dx, dh], tr, name=f"{tag}_bwd"))


def _loss_head(st, final_g, target, *, name):
    xb, y, gate = st
    B, n, d = xb.shape
    tr = min(ROW_TILE, n)

    def tile_loss(xv, yv, gt, g, t):
        x = xv + gt * yv
        out = x * lax.rsqrt(jnp.mean(x * x, axis=-1, keepdims=True) + EPS) * g
        return 0.5 * jnp.sum(jnp.mean(jnp.square(out - t), axis=-1))

    def body(x_ref, y_ref, gate_ref, g_ref, t_ref, loss_ref, dx_ref, dy_ref, dgate_ref, dg_ref):
        b, j = pl.program_id(0), pl.program_id(1)
        val, (dx, dy, dgate, dg) = jax.value_and_grad(tile_loss, argnums=(0, 1, 2, 3))(
            x_ref[...], y_ref[...], gate_ref[...], g_ref[...], t_ref[...])
        dx_ref[...] = dx
        dy_ref[...] = dy.astype(dy_ref.dtype)

        @pl.when(j == 0)
        def _():
            loss_ref[...] = jnp.zeros_like(loss_ref)
            dgate_ref[...] = jnp.zeros_like(dgate_ref)

        @pl.when((j == 0) & (b == 0))
        def _():
            dg_ref[...] = jnp.zeros_like(dg_ref)

        loss_ref[...] += jnp.full(loss_ref.shape, val, F32)
        dgate_ref[...] += dgate
        dg_ref[...] += dg

    row = pl.BlockSpec((None, tr, d), lambda b, j: (b, j, 0))
    per_sample = pl.BlockSpec((None, 1, d), lambda b, j: (b, 0, 0))
    whole = pl.BlockSpec((1, d), lambda b, j: (0, 0))
    return pl.pallas_call(
        body, name=name, grid=(B, n // tr),
        in_specs=[row, row, per_sample, whole, row],
        out_specs=[pl.BlockSpec((None, 1, 128), lambda b, j: (b, 0, 0)), row, row, per_sample, whole],
        out_shape=[jax.ShapeDtypeStruct((B, 1, 128), F32), jax.ShapeDtypeStruct((B, n, d), F32),
                   jax.ShapeDtypeStruct((B, n, d), BF16), jax.ShapeDtypeStruct((B, 1, d), F32), jax.ShapeDtypeStruct((1, d), F32)],
        compiler_params=_params("arbitrary", "arbitrary"),
    )(xb, y, gate, final_g, target)


def _exchange(arrs, *, scatter, name):
    k = len(arrs)

    def body(*refs):
        ins, outs, sems = refs[:k], refs[k:2 * k], refs[2 * k:]
        _exch_start(ins, outs, sems, scatter)
        _exch_wait(ins, outs, sems, scatter)

    any_spec = pl.BlockSpec(memory_space=pl.ANY)
    return pl.pallas_call(
        body, name=name,
        in_specs=[any_spec] * k, out_specs=[any_spec] * k,
        out_shape=_exch_out_shapes(arrs, scatter), scratch_shapes=_exch_sems(k),
        compiler_params=pltpu.CompilerParams(has_side_effects=True),
    )(*arrs)


def _exch_flags(scatter, k):
    return [scatter] * k if isinstance(scatter, bool) else list(scatter)


def _exch_out_shapes(arrs, scatter):
    return [jax.ShapeDtypeStruct(a.shape if f else (N_DEV,) + a.shape, a.dtype)
            for a, f in zip(arrs, _exch_flags(scatter, len(arrs)))]


def _exch_sems(k):
    return [pltpu.SemaphoreType.DMA((k * (N_DEV - 1),)), pltpu.SemaphoreType.DMA((k * (N_DEV - 1),)),
            pltpu.SemaphoreType.DMA((k,))]


def _exch_copies(ins, outs, sems, scatter):
    send_sems, recv_sems, local_sems = sems
    x, y, c = lax.axis_index("x"), lax.axis_index("y"), lax.axis_index("c")
    me = 4 * x + 2 * y + c
    owns, sends, recvs = [], [], []
    flags = _exch_flags(scatter, len(ins))
    for a in range(len(ins)):
        scatter = flags[a]
        owns.append(pltpu.make_async_copy(ins[a].at[me] if scatter else ins[a], outs[a].at[me], local_sems.at[a]))
        for r in range(1, N_DEV):
            fx, fy, fc = (r >> 2) & 1, (r >> 1) & 1, r & 1
            px, py, pc = (x + fx) % 2, (y + fy) % 2, (c + fc) % 2
            peer = 4 * px + 2 * py + pc
            s = a * (N_DEV - 1) + r - 1
            mk = functools.partial(pltpu.make_async_remote_copy, src_ref=ins[a].at[peer] if scatter else ins[a],
                                   send_sem=send_sems.at[s], recv_sem=recv_sems.at[s],
                                   device_id=(px, py, pc), device_id_type=pl.DeviceIdType.MESH)
            sends.append(mk(dst_ref=outs[a].at[me]))
            recvs.append(mk(dst_ref=outs[a].at[peer]))
    return owns, sends, recvs


def _exch_start(ins, outs, sems, scatter):
    owns, sends, _ = _exch_copies(ins, outs, sems, scatter)
    for cp in owns + sends:
        cp.start()


def _exch_wait(ins, outs, sems, scatter):
    owns, sends, recvs = _exch_copies(ins, outs, sems, scatter)
    for rc in recvs:
        rc.wait_recv()
    for cp in sends:
        cp.wait_send()
    for own in owns:
        own.wait()


MOD_ROWS = 48


def _mod_tile(cc, w, b):
    s = cc * jax.nn.sigmoid(cc)
    return lax.dot_general(s.astype(BF16), w.astype(BF16), (((1,), (0,)), ((), ())), preferred_element_type=F32) + b


def _mod_fwd(cc, w_mod, b_shard, *, name):
    L, d, wcols = w_mod.shape

    def body(cc_ref, w_ref, b_ref, o_ref):
        o_ref[...] = _mod_tile(cc_ref[...], w_ref[...], b_ref[...])

    return pl.pallas_call(
        body, name=name, grid=(L,),
        in_specs=[pl.BlockSpec((MOD_ROWS, d), lambda l: (0, 0)), pl.BlockSpec((None, d, wcols), lambda l: (l, 0, 0)),
                  pl.BlockSpec((None, 1, wcols), lambda l: (l, 0, 0))],
        out_specs=pl.BlockSpec((None, MOD_ROWS, wcols), lambda l: (l, 0, 0)),
        out_shape=jax.ShapeDtypeStruct((L, MOD_ROWS, wcols), F32),
        compiler_params=_params("parallel"),
    )(cc, w_mod, b_shard)


def _mod_bwd(cc, w_mod, b_shard, dm, *, name):
    L, d, wcols = w_mod.shape

    def body(cc_ref, w_ref, b_ref, dm_ref, dcc_ref, dw_ref):
        _, vjp = jax.vjp(_mod_tile, cc_ref[...], w_ref[...], b_ref[...])
        dcc, dw, _ = vjp(dm_ref[...])
        dw_ref[...] = dw

        @pl.when(pl.program_id(0) == 0)
        def _():
            dcc_ref[...] = dcc

        @pl.when(pl.program_id(0) > 0)
        def _():
            dcc_ref[...] += dcc

    return pl.pallas_call(
        body, name=name, grid=(L,),
        in_specs=[pl.BlockSpec((MOD_ROWS, d), lambda l: (0, 0)), pl.BlockSpec((None, d, wcols), lambda l: (l, 0, 0)),
                  pl.BlockSpec((None, 1, wcols), lambda l: (l, 0, 0)), pl.BlockSpec((None, MOD_ROWS, wcols), lambda l: (l, 0, 0))],
        out_specs=[pl.BlockSpec((MOD_ROWS, d), lambda l: (0, 0)), pl.BlockSpec((None, d, wcols), lambda l: (l, 0, 0))],
        out_shape=[jax.ShapeDtypeStruct((MOD_ROWS, d), F32), jax.ShapeDtypeStruct((L, d, wcols), F32)],
        compiler_params=_params("arbitrary"),
    )(cc, w_mod, b_shard, dm)


def _sum_leading(a, *, name):
    K, R, C = a.shape
    tr = _tile8(R, 256)

    def body(a_ref, o_ref):
        acc = a_ref[0].astype(F32)
        for i in range(1, K):
            acc = acc + a_ref[i].astype(F32)
        o_ref[...] = acc

    return pl.pallas_call(
        body, name=name, grid=(R // tr,),
        in_specs=[pl.BlockSpec((K, tr, C), lambda i: (0, i, 0))],
        out_specs=pl.BlockSpec((tr, C), lambda i: (i, 0)),
        out_shape=jax.ShapeDtypeStruct((R, C), F32),
        compiler_params=_params("parallel"),
    )(a)


def _tile8(dim, target):
    if dim <= target:
        return dim
    t = (target // 8) * 8
    while t >= 8:
        if dim % t == 0:
            return t
        t -= 8
    raise ValueError(f"no row tile for {dim}")


def _adamw_math(g, w, m, v):
    m = ADAM_B1 * m + (1.0 - ADAM_B1) * g
    v = ADAM_B2 * v + (1.0 - ADAM_B2) * jnp.square(g)
    m_hat = m / (1.0 - ADAM_B1 ** ADAM_STEP)
    v_hat = v / (1.0 - ADAM_B2 ** ADAM_STEP)
    delta = -ADAM_LR * (m_hat / (jnp.sqrt(v_hat) + ADAM_EPS) + ADAM_WD * w)
    return delta, m, v


def _adamw(g, w, m, v, *, name, exch=None):
    L, R, C = w.shape
    parts = isinstance(g, (list, tuple))
    gs = list(g) if parts else [g]
    ng = len(gs)
    tr = _tile8(R, 256)

    def body(*refs):
        g_refs = refs[:ng]
        w_ref, m_ref, v_ref, go_ref, d_ref, mo_ref, vo_ref = refs[ng:]
        if parts:
            layer = pl.program_id(0)
            gv = None
            for li, g_ref in enumerate(g_refs):
                acc = g_ref[0].astype(F32)
                for i in range(1, N_DEV):
                    acc = acc + g_ref[i].astype(F32)
                gv = acc if gv is None else jnp.where(layer == li, acc, gv)
        else:
            gv = g_refs[0][...]
        go_ref[...] = gv
        d_ref[...], mo_ref[...], vo_ref[...] = _adamw_math(gv, w_ref[...], m_ref[...], v_ref[...])

    tile = pl.BlockSpec((None, tr, C), lambda l, i: (l, i, 0))
    g_specs = [pl.BlockSpec((N_DEV, tr, C), lambda l, i: (0, i, 0))] * ng if parts else [tile]
    grid = (L, R // tr)
    body, x_in, x_out, x_shapes, x_sems = _riding(body, exch, ng + 3, 4, grid)
    return pl.pallas_call(
        body, name=name, grid=grid,
        in_specs=g_specs + [tile, tile, tile] + x_in, out_specs=[tile] * 4 + x_out,
        out_shape=[jax.ShapeDtypeStruct((L, R, C), F32)] * 4 + x_shapes,
        scratch_shapes=x_sems,
        compiler_params=_params("arbitrary", "arbitrary") if exch else _params("parallel", "parallel"),
    )(*gs, w, m, v, *(exch[0] if exch else []))


def _adamw_small(gs, ws, ms, vs, *, name):
    k = len(ws)

    def body(*refs):
        g_refs, w_refs, m_refs, v_refs = refs[:k], refs[k:2 * k], refs[2 * k:3 * k], refs[3 * k:4 * k]
        d_refs, mo_refs, vo_refs = refs[4 * k:5 * k], refs[5 * k:6 * k], refs[6 * k:]
        for i in range(k):
            d_refs[i][...], mo_refs[i][...], vo_refs[i][...] = _adamw_math(g_refs[i][...], w_refs[i][...], m_refs[i][...],
                                                                         v_refs[i][...])

    shapes = [jax.ShapeDtypeStruct(a.shape, F32) for a in ws]
    out = pl.pallas_call(body, name=name, out_shape=shapes * 3, compiler_params=pltpu.CompilerParams(vmem_limit_bytes=VMEM_LIMIT))(
        *gs, *ws, *ms, *vs)
    return out[:k], out[k:2 * k], out[2 * k:]


def _block_diag(w):
    g, c, d = w.shape
    return (w[:, :, None, :] * jnp.eye(g, dtype=w.dtype)[:, None, :, None]).reshape(g * c, g * d)


def _diag_blocks(w_bd):
    g = POOL_W // POOL_GROUP
    return jnp.stack([w_bd[i * POOL_GROUP:(i + 1) * POOL_GROUP, i * POOL_GROUP:(i + 1) * POOL_GROUP] for i in range(g)])


def _flat(a):
    return a.reshape(-1, a.shape[-1])


def _mix_half_fwd(tag, st, mods, wl, kvc, *, local, kv_only, exch=None, normed=None):
    sh1, sc1, g1 = mods[:3]
    B, n, d = st[0].shape
    x, h, h_t = normed if normed is not None else _norm_fwd(f"{tag}_norm1", st, wl["n1"], sc1, sh1)
    if kv_only:
        kv = _mm(_flat(h), wl["w_in"][:, ATTN_W:ATTN_W + 2 * KV_W], name=f"{tag}_kv").reshape(B, n, 2 * KV_W)
        return None, dict(st=st, h_t=h_t, kvc=kv), []
    u = _mm(_flat(h), wl["w_in"], name=f"{tag}_in", tn=IN_W).reshape(B, n, IN_W)
    if not local:
        kvc = u[:, :, ATTN_W:ATTN_W + 2 * KV_W]
    mix, conv_acc, got = _mixer_fwd(f"{tag}_mix", u, kvc, wl["margs"], local, exch)
    y = _mm(_flat(mix), wl["w_out"], name=f"{tag}_out", tn=D_MODEL).reshape(B, n, d)
    return (x, y, g1), dict(st=st, h_t=h_t, u=u, kvc=kvc, mix=mix, conv_acc=conv_acc), got


def _ffn_half_fwd(tag, st2, mods, wl, exch=None):
    sh2, sc2, g2 = mods[3:]
    B, n, d = st2[0].shape
    x1, h2, h2_t = _norm_fwd(f"{tag}_norm2", st2, wl["n2"], sc2, sh2)
    gu, act, act_t, *got = _mm_swiglu(_flat(h2), wl["w_ffn_in"], name=f"{tag}_ffn_in", exch=exch)
    y2 = _mm(act, wl["w_ffn_out"], name=f"{tag}_ffn_out", tn=D_MODEL).reshape(B, n, d)
    return (x1, y2, g2), dict(st2=st2, h2_t=h2_t, gu=gu, act_t=act_t), got


def _ffn_half_bwd(tag, sv, mods, wl, dx1, dy2, plus=None):
    sh2, sc2, _ = mods[3:]
    B, n, d = sv["st2"][0].shape
    gw = {}
    dy2f = _flat(dy2)
    plus = plus or {}
    gw["w_ffn_out"] = _mm(sv["act_t"], dy2f, out_dtype=BF16, name=f"{tag}_ffn_out_dw", plus=plus.get("w_ffn_out"))
    dgu = _mm_dswiglu(dy2f, wl["w_ffn_out"], sv["gu"], name=f"{tag}_ffn_out_dx")
    dh2 = _mm(dgu, wl["w_ffn_in"], trans_b=True, name=f"{tag}_ffn_in_dx").reshape(B, n, d)
    gw["w_ffn_in"] = _mm(sv["h2_t"], dgu, out_dtype=BF16, tn=FF_TILE, out_block=_natural_block, name=f"{tag}_ffn_in_dw",
                         plus=plus.get("w_ffn_in"))
    dx, dy, dg1, gw["n2"], dsc2, dsh2 = _norm_bwd(f"{tag}_norm2", sv["st2"], wl["n2"], sc2, sh2, dx1, dh2)
    return (dx, dy, dg1), gw, dict(sh2=dsh2, sc2=dsc2)


def _mix_half_bwd(tag, sv, mods, wl, dx, dy, dkv_in, *, local, kv_only, exch=None, plus=None):
    sh1, sc1, _ = mods[:3]
    B, n, d = sv["st"][0].shape
    gw = {}
    if kv_only:
        dkv = _flat(dkv_in).astype(BF16)
        dh = _mm(dkv, wl["w_in"][:, ATTN_W:ATTN_W + 2 * KV_W], trans_b=True, name=f"{tag}_kv_dx").reshape(B, n, d)
        gw["w_in_kv"] = _mm(sv["h_t"], dkv, out_dtype=BF16, name=f"{tag}_kv_dw")
        dxb, dy_prev, dgate_prev, gw["n1"], dsc1, dsh1 = _norm_bwd(f"{tag}_norm1", sv["st"], wl["n1"], sc1, sh1,
                                                                    jnp.zeros((B, n, d), F32), dh)
        return (dxb, dy_prev, dgate_prev), gw, dict(sh1=dsh1, sc1=dsc1), None, []

    dyf = _flat(dy)
    dmix = _mm(dyf, wl["w_out"], trans_b=True, name=f"{tag}_out_dx", tn=D_MODEL).reshape(B, n, d)
    plus = plus or {}
    gw["w_out"] = _mm(_flat(sv["mix"]).T, dyf, out_dtype=BF16, tn=DW_TN, name=f"{tag}_out_dw", plus=plus.get("w_out"))
    (dq, dk, dv, da, dg, dpu), (dkc, dvc), gw["margs"], got = _mixer_bwd(f"{tag}_mix", sv["u"], sv["kvc"], sv["conv_acc"],
                                                                     wl["margs"], dmix, local, exch)
    if local:
        dkv_out = jnp.concatenate([dkc, dvc], axis=-1)
    else:
        dk = dkc + dkv_in[:, :, :KV_W]
        dv = dvc + dkv_in[:, :, KV_W:]
        dkv_out = None
    du = _flat(jnp.concatenate([dq, dk, dv, da, dg, dpu], axis=-1).astype(BF16))
    dh = _mm(du, wl["w_in"], trans_b=True, name=f"{tag}_in_dx", tn=D_MODEL).reshape(B, n, d)
    gw["w_in"] = _mm(sv["h_t"], du, out_dtype=BF16, tn=DW_TN, name=f"{tag}_in_dw", plus=plus.get("w_in"))
    dxb, dy_prev, dgate_prev, gw["n1"], dsc1, dsh1 = _norm_bwd(f"{tag}_norm1", sv["st"], wl["n1"], sc1, sh1, dx, dh)
    return (dxb, dy_prev, dgate_prev), gw, dict(sh1=dsh1, sc1=dsc1), dkv_out, got


BIG_W = ("w_in", "w_out", "w_ffn_in", "w_ffn_out")


def _local_step(x, ctx, m_loc, m_ctx, p, final_g, target, big):
    B = x.shape[0]
    depth = m_loc.shape[0]
    lat_mods = [[t[:, None, :] for t in jnp.split(m_loc[l], 6, axis=-1)] for l in range(depth)]
    ctx_mods = [[jnp.broadcast_to(t[None, None, :], (B, 1, D_MODEL)) for t in jnp.split(m_ctx[l], 6)] for l in range(depth)]

    st, cst = (x, None, None), (ctx, None, None)
    w_mix, w_ffn, sv_mix, sv_ffn, csv_mix, csv_ffn = [], [], [], [], [], []
    *normed, = _norm_fwd("l0_norm1", st, p["norm1_g"][0][None, :], lat_mods[0][1], lat_mods[0][0], exch=big.ride_first())
    normed, got = normed[:3], normed[3:]
    for l in range(depth):
        last = l == depth - 1
        wb = big.mix_weights(l, got)
        wm = dict(n1=p["norm1_g"][l][None, :], w_in=wb["w_in"], w_out=wb["w_out"],
                  margs=(p["attn_sink"][l][None, :], p["conv_dw"][l], p["conv_dw_b"][l][None, :], p["conv_ln_g"][l][None, :],
                         p["conv_ln_b"][l][None, :], _block_diag(p["pool_w"][l]), p["pool_scale"][l][None, :]))
        cst, csv, _ = _mix_half_fwd(f"l{l}c", cst, ctx_mods[l], wm, None, local=False, kv_only=last)
        st, sv, got = _mix_half_fwd(f"l{l}", st, lat_mods[l], wm, csv["kvc"], local=True, kv_only=False,
                                    exch=big.ride_attn_fwd(l), normed=normed if l == 0 else None)
        w_mix.append(wm)
        sv_mix.append(sv)
        csv_mix.append(csv)
        wb = big.ffn_weights(l, got)
        w_ffn_in = _interleave_ffn(wb["w_ffn_in"], name=f"l{l}_ffn_in_interleave")
        wf = dict(n2=p["norm2_g"][l][None, :], w_ffn_in=w_ffn_in, w_ffn_out=wb["w_ffn_out"])
        csv = None
        if not last:
            cst, csv, _ = _ffn_half_fwd(f"l{l}c", cst, ctx_mods[l], wf)
        st, sv, got = _ffn_half_fwd(f"l{l}", st, lat_mods[l], wf, exch=big.ride_ffn_fwd(l))
        w_ffn.append(wf)
        sv_ffn.append(sv)
        csv_ffn.append(csv)
    loss_rows, dx, dy, dgate, dfinal = _loss_head(st, final_g[None, :], target, name="loss_head")

    dm_loc, dm_ctx = [None] * depth, [None] * depth
    small = [None] * depth
    cdx = cdy = cdgate = None
    up_mix = None
    for l in reversed(range(depth)):
        last = l == depth - 1
        dm, cdm = dict(g2=dgate), {}
        (dx, dy, dm["g1"]), gf, d = _ffn_half_bwd(f"l{l}", sv_ffn[l], lat_mods[l], w_ffn[l], dx, dy)
        dm.update(d)
        if not last:
            cdm["g2"] = cdgate
            (cdx, cdy, cdm["g1"]), cgf, d = _ffn_half_bwd(f"l{l}c", csv_ffn[l], ctx_mods[l], w_ffn[l], cdx, cdy, plus=gf)
            cdm.update(d)
            gf = dict(cgf, n2=gf["n2"] + cgf["n2"])
        ffn_grads = {k: gf[k] for k in _ShardedWeights.FFN}
        (dx, dy, dgate), gm, d, dkv, got = _mix_half_bwd(f"l{l}", sv_mix[l], lat_mods[l], w_mix[l], dx, dy, None, local=True,
                                                        kv_only=False, exch=big.ride_attn_bwd(l, ffn_grads, up_mix))
        big.took(l, ffn_grads, up_mix, got)
        dm.update(d)
        (cdx, cdy, cdgate), cgm, d, _, _ = _mix_half_bwd(f"l{l}c", csv_mix[l], ctx_mods[l], w_mix[l], cdx, cdy, dkv,
                                                        local=False, kv_only=last, plus=gm)
        cdm.update(d)
        order = ("sh1", "sc1", "g1", "sh2", "sc2", "g2")
        dm_loc[l] = jnp.concatenate([dm[k][:, 0, :] for k in order], axis=-1)
        dm_ctx[l] = jnp.concatenate([jnp.sum(cdm[k][:, 0, :], axis=0) if k in cdm else jnp.zeros((D_MODEL,), F32)
                                     for k in order])
        if last:
            up_mix = dict(w_in=gm["w_in"].at[:, ATTN_W:ATTN_W + 2 * KV_W].add(cgm["w_in_kv"]), w_out=gm["w_out"])
            margs = gm["margs"]
        else:
            up_mix = {k: cgm[k] for k in _ShardedWeights.MIX}
            margs = tuple(a + b for a, b in zip(gm["margs"], cgm["margs"]))
        small[l] = dict(n1=gm["n1"] + cgm["n1"], n2=gf["n2"], margs=margs)
    big.leftover(up_mix)

    stack = lambda f: jnp.stack([f(small[l]) for l in range(depth)])
    dp = dict(
        norm1_g=stack(lambda g: g["n1"][0]), norm2_g=stack(lambda g: g["n2"][0]),
        attn_sink=stack(lambda g: g["margs"][0][0]), conv_dw=stack(lambda g: g["margs"][1]),
        conv_dw_b=stack(lambda g: g["margs"][2][0]), conv_ln_g=stack(lambda g: g["margs"][3][0]),
        conv_ln_b=stack(lambda g: g["margs"][4][0]), pool_w=stack(lambda g: _diag_blocks(g["margs"][5])),
        pool_scale=stack(lambda g: g["margs"][6][0]))
    return jnp.sum(loss_rows[:, 0, 0]), dx, jnp.stack(dm_loc), jnp.stack(dm_ctx), dp, dfinal[0]


PACK_COLS = 1024


def _pack(arrs):
    flat = jnp.concatenate([a.reshape(-1).astype(F32) for a in arrs])
    rows = -(-flat.shape[0] // (8 * PACK_COLS)) * 8
    return jnp.pad(flat, (0, rows * PACK_COLS - flat.shape[0])).reshape(rows, PACK_COLS)


def _unpack(slab, like):
    flat = slab.reshape(-1)
    out, off = [], 0
    for a in like:
        out.append(flat[off:off + a.size].reshape(a.shape))
        off += a.size
    return out


def _shard_cols(gathered):
    _, L, R, C = gathered.shape
    return jnp.transpose(gathered, (1, 2, 0, 3)).reshape(L, R, N_DEV * C)


class _ShardedWeights:
    MIX = ("w_in", "w_out")
    FFN = ("w_ffn_in", "w_ffn_out")
    BY_COLS = ("w_in", "w_ffn_in")

    def __init__(self, shards):
        self.shards = shards
        self.depth = shards[BIG_W[0]].shape[0]
        self.parts = [dict() for _ in range(self.depth)]
        self.left = None

    def _join(self, names, blocks):
        out = {}
        for name, g in zip(names, blocks):
            _, R, C = g.shape
            out[name] = jnp.transpose(g, (1, 0, 2)).reshape(R, N_DEV * C) if name in self.BY_COLS else g.reshape(N_DEV * R, C)
        return out

    def cut(self, names, grads):
        out = []
        for name in names:
            g = grads[name]
            if name in self.BY_COLS:
                R, C8 = g.shape
                out.append(jnp.transpose(g.reshape(R, N_DEV, C8 // N_DEV), (1, 0, 2)))
            else:
                R8, C = g.shape
                out.append(g.reshape(N_DEV, R8 // N_DEV, C))
        return out

    def ride_first(self):
        return [self.shards[name][0] for name in self.MIX], False

    def mix_weights(self, l, got):
        return self._join(self.MIX, got)

    def ffn_weights(self, l, got):
        return self._join(self.FFN, got)

    def ride_attn_fwd(self, l):
        return [self.shards[name][l] for name in self.FFN], False

    def ride_ffn_fwd(self, l):
        if l + 1 >= self.depth:
            return None
        return [self.shards[name][l + 1] for name in self.MIX], False

    def ride_attn_bwd(self, l, ffn_grads, up_mix):
        return self.cut(self.FFN, ffn_grads) + (self.cut(self.MIX, up_mix) if up_mix is not None else []), True

    def took(self, l, ffn_grads, up_mix, got):
        self.parts[l].update(zip(self.FFN, got[:2]))
        if up_mix is not None:
            self.parts[l + 1].update(zip(self.MIX, got[2:]))

    def leftover(self, mix_grads):
        self.left = mix_grads


def _as_rows(a, leading=0):
    return a.reshape(*a.shape[:leading], -1, PACK_COLS)


SMALL = ("c_ctx", "b_mod", "norm1_g", "norm2_g", "conv_dw_b", "conv_ln_g", "conv_ln_b", "attn_sink", "pool_w",
         "pool_scale", "final_g", "conv_dw")
BIG = ("w_mod", "w_in", "w_out", "w_ffn_in", "w_ffn_out")
ORDER = ("c_ctx", "w_mod", "b_mod", "norm1_g", "norm2_g", "w_in", "conv_dw", "conv_dw_b", "conv_ln_g", "conv_ln_b",
         "attn_sink", "pool_w", "pool_scale", "w_out", "w_ffn_in", "w_ffn_out", "final_g")


def kernel(x, c, ctx, c_ctx, w_mod, b_mod, norm1_g, norm2_g, w_in, conv_dw, conv_dw_b, conv_ln_g, conv_ln_b, attn_sink, pool_w, pool_scale, w_out, w_ffn_in, w_ffn_out, final_g, loss_target, m_c_ctx, m_w_mod, m_b_mod, m_norm1_g, m_norm2_g, m_w_in, m_conv_dw, m_conv_dw_b, m_conv_ln_g, m_conv_ln_b, m_attn_sink, m_pool_w, m_pool_scale, m_w_out, m_w_ffn_in, m_w_ffn_out, m_final_g, v_c_ctx, v_w_mod, v_b_mod, v_norm1_g, v_norm2_g, v_w_in, v_conv_dw, v_conv_dw_b, v_conv_ln_g, v_conv_ln_b, v_attn_sink, v_pool_w, v_pool_scale, v_w_out, v_w_ffn_in, v_w_ffn_out, v_final_g):
    w = dict(c_ctx=c_ctx, w_mod=w_mod, b_mod=b_mod, norm1_g=norm1_g, norm2_g=norm2_g, w_in=w_in, conv_dw=conv_dw,
             conv_dw_b=conv_dw_b, conv_ln_g=conv_ln_g, conv_ln_b=conv_ln_b, attn_sink=attn_sink, pool_w=pool_w,
             pool_scale=pool_scale, w_out=w_out, w_ffn_in=w_ffn_in, w_ffn_out=w_ffn_out, final_g=final_g)
    mom = dict(c_ctx=m_c_ctx, w_mod=m_w_mod, b_mod=m_b_mod, norm1_g=m_norm1_g, norm2_g=m_norm2_g, w_in=m_w_in,
               conv_dw=m_conv_dw, conv_dw_b=m_conv_dw_b, conv_ln_g=m_conv_ln_g, conv_ln_b=m_conv_ln_b,
               attn_sink=m_attn_sink, pool_w=m_pool_w, pool_scale=m_pool_scale, w_out=m_w_out, w_ffn_in=m_w_ffn_in,
               w_ffn_out=m_w_ffn_out, final_g=m_final_g)
    var = dict(c_ctx=v_c_ctx, w_mod=v_w_mod, b_mod=v_b_mod, norm1_g=v_norm1_g, norm2_g=v_norm2_g, w_in=v_w_in,
               conv_dw=v_conv_dw, conv_dw_b=v_conv_dw_b, conv_ln_g=v_conv_ln_g, conv_ln_b=v_conv_ln_b,
               attn_sink=v_attn_sink, pool_w=v_pool_w, pool_scale=v_pool_scale, w_out=v_w_out, w_ffn_in=v_w_ffn_in,
               w_ffn_out=v_w_ffn_out, final_g=v_final_g)
    B = x.shape[0]
    depth = w_mod.shape[0]
    mod_cols = w_mod.shape[2]
    dw_cols = conv_dw.shape[2]
    me = 4 * lax.axis_index("x") + 2 * lax.axis_index("y") + lax.axis_index("c")

    shards = {name: w[name].astype(BF16) for name in BIG_W}
    c_all, dw_all = _exchange([c, conv_dw], scatter=False, name="gather_first")
    big = _ShardedWeights(shards)
    p = dict(norm1_g=norm1_g, norm2_g=norm2_g, conv_dw=_shard_cols(dw_all), conv_dw_b=conv_dw_b, conv_ln_g=conv_ln_g,
             conv_ln_b=conv_ln_b, attn_sink=attn_sink, pool_w=pool_w, pool_scale=pool_scale)

    cc = jnp.concatenate([c_all.reshape(N_DEV * B, D_MODEL), jnp.broadcast_to(c_ctx[None, :], (N_DEV, D_MODEL)),
                          jnp.zeros((MOD_ROWS - N_DEV * B - N_DEV, D_MODEL), F32)], axis=0)
    b_shard = lax.dynamic_slice_in_dim(b_mod, me * mod_cols, mod_cols, axis=1)[:, None, :]
    m_part = _mod_fwd(cc, w_mod, b_shard, name="mod_fwd")
    m_all, = _exchange([m_part], scatter=False, name="gather_mod")
    m_full = _shard_cols(m_all)
    m_loc = lax.dynamic_slice_in_dim(m_full, me * B, B, axis=1)
    m_ctx = m_full[:, N_DEV * B, :]

    loss_part, dx, dm_loc, dm_ctx, dp, dfinal = _local_step(x, ctx, m_loc, m_ctx, p, final_g, loss_target, big)
    loss = lax.psum(loss_part, AXES)

    dm_rows = jnp.concatenate([dm_loc, dm_ctx[:, None, :], jnp.zeros((depth, 8 - B - 1, 6 * D_MODEL), F32)], axis=1)
    small_like = [norm1_g, norm2_g, conv_dw_b, conv_ln_g, conv_ln_b, attn_sink, pool_w, pool_scale, final_g, dp["conv_dw"]]
    small_part = _pack([dp["norm1_g"], dp["norm2_g"], dp["conv_dw_b"], dp["conv_ln_g"], dp["conv_ln_b"], dp["attn_sink"],
                        dp["pool_w"], dp["pool_scale"], dfinal, dp["conv_dw"]])
    g, delta, new_m, new_v = {}, {}, {}, {}
    first_ffn = big.FFN[0]
    g[first_ffn], delta[first_ffn], new_m[first_ffn], new_v[first_ffn], dm_all, small_all, *last = _adamw(
        [big.parts[l][first_ffn] for l in range(depth)], w[first_ffn], mom[first_ffn], var[first_ffn],
        name=f"adamw_{first_ffn}", exch=([dm_rows, small_part] + big.cut(big.MIX, big.left), [False, False, True, True]))
    big.parts[0].update(zip(big.MIX, last))
    dm_full = jnp.concatenate([
        jnp.transpose(dm_all[:, :, :B, :], (1, 0, 2, 3)).reshape(depth, N_DEV * B, 6 * D_MODEL),
        jnp.transpose(dm_all[:, :, B, :], (1, 0, 2)),
        jnp.zeros((depth, MOD_ROWS - N_DEV * B - N_DEV, 6 * D_MODEL), F32)], axis=1)
    g_b_mod = jnp.stack([_sum_leading(dm_full[l][:, None, :], name=f"b_mod_grad{l}")[0] for l in range(depth)])
    dm_mine = lax.dynamic_slice_in_dim(dm_full, me * mod_cols, mod_cols, axis=2)
    dcc, g_w_mod = _mod_bwd(cc, w_mod, b_shard, dm_mine, name="mod_bwd")
    g_c_ctx_part = jnp.sum(dcc[N_DEV * B:N_DEV * B + N_DEV], axis=0)

    small_sum = _unpack(_sum_leading(small_all, name="sum_small"), small_like)
    g.update(zip(("norm1_g", "norm2_g", "conv_dw_b", "conv_ln_g", "conv_ln_b", "attn_sink", "pool_w", "pool_scale",
                  "final_g"), small_sum[:-1]))
    g["b_mod"] = g_b_mod
    g["conv_dw"] = lax.dynamic_slice_in_dim(small_sum[-1], me * dw_cols, dw_cols, axis=2)
    c_ctx_all, = _exchange([g_c_ctx_part.reshape(8, D_MODEL // 8)], scatter=False, name="gather_c_ctx")
    g["c_ctx"] = _sum_leading(c_ctx_all, name="sum_c_ctx").reshape(D_MODEL)

    for name in big.FFN[1:] + big.MIX:
        g[name], delta[name], new_m[name], new_v[name] = _adamw(
            [big.parts[l][name] for l in range(depth)], w[name], mom[name], var[name], name=f"adamw_{name}")
    g["w_mod"], delta["w_mod"], new_m["w_mod"], new_v["w_mod"] = _adamw(g_w_mod, w_mod, m_w_mod, v_w_mod, name="adamw_w_mod")
    res = _adamw_small([g[k] for k in SMALL], [w[k] for k in SMALL], [mom[k] for k in SMALL], [var[k] for k in SMALL],
                       name="adamw_small")
    for dst, arrs in zip((delta, new_m, new_v), res):
        dst.update(zip(SMALL, arrs))

    return (loss, dx, *[g[k] for k in ORDER], *[delta[k] for k in ORDER], *[new_m[k] for k in ORDER],
            *[new_v[k] for k in ORDER])
```
